```python
import math
import jax
import jax.numpy as jnp
from jax import lax
import numpy as np

D_MODEL = 1024
BATCH = 4
SEQ = 4096
DEPTH = 2

GRID_W = 64
CTX_LEN = 256
CHUNK = 64
CONV_K = 3
EPS = 1e-6

BRANCH_DIM = D_MODEL // 2
N_BRANCH = 4
SSM_HEAD_DIM = 64
SSM_HEADS = BRANCH_DIM // SSM_HEAD_DIM
SSM_GROUPS = 2
SSM_STATE = 64
GLA_HEADS = 4
GLA_DV = BRANCH_DIM // GLA_HEADS
GLA_DK = GLA_DV // 2
GLA_GATE_RANK = 16
GLA_TAU = 16.0
RWKV_HEAD = 64
RWKV_HEADS = BRANCH_DIM // RWKV_HEAD
RWKV_DECAY_RANK = 64
RWKV_ICLR_RANK = 64
RWKV_GATE_RANK = 128
RWKV_LN_EPS = 64e-5
GDN_HEAD = 128
GDN_HEADS = BRANCH_DIM // GDN_HEAD
N_EXPERTS = 16
N_GROUPS = 4
EXPERTS_PER_GROUP = N_EXPERTS // N_GROUPS
TOP_K = 2
EXPERT_FF = D_MODEL // 2

SSM_COLS = (BRANCH_DIM, BRANCH_DIM + 2 * SSM_GROUPS * SSM_STATE, 2 * SSM_HEADS)
GLA_COLS = (GLA_HEADS * GLA_DK, GLA_HEADS * GLA_DK, BRANCH_DIM, BRANCH_DIM, 2 * GLA_GATE_RANK)
RWKV_COLS = (3 * BRANCH_DIM, 2 * RWKV_DECAY_RANK, 2 * RWKV_ICLR_RANK, RWKV_GATE_RANK)
GDN_COLS = (3 * BRANCH_DIM, BRANCH_DIM, 2 * GDN_HEADS, 2 * GDN_HEADS)
MIXER_COLS = (sum(SSM_COLS), sum(GLA_COLS), sum(RWKV_COLS), sum(GDN_COLS))
N_IN = sum(MIXER_COLS)

kernel_name = 'hybrid_ssd_gla_rwkv7_gdn_moe_diffusion'


def rms_norm(x, w):
    xf = x.astype(jnp.float32)
    y = xf * lax.rsqrt(jnp.mean(xf * xf, -1, keepdims=True) + EPS)
    return (y * w).astype(x.dtype)


def l2_normalize(x):
    xf = x.astype(jnp.float32)
    return (xf * lax.rsqrt(jnp.sum(xf * xf, -1, keepdims=True) + EPS)).astype(x.dtype)


def modulate(h, shift, scale):
    return h * (1 + scale) + shift


def split(x, sizes):
    return jnp.split(x, np.cumsum(sizes)[:-1].tolist(), axis=-1)


def to_heads(t, n):
    b, s, _ = t.shape
    return t.reshape(b, s, n, -1).transpose(0, 2, 1, 3)


def from_heads(t):
    b, n, s, d = t.shape
    return t.transpose(0, 2, 1, 3).reshape(b, s, n * d)


def centred_conv(x, w):
    pad = w.shape[0] // 2
    return lax.conv_general_dilated(x, w[:, None, :], window_strides=(1,), padding=[(pad, pad)],
                                    dimension_numbers=('NWC', 'WIO', 'NWC'), feature_group_count=x.shape[-1])


def centred_shift(x):
    xp = jnp.pad(x, ((0, 0), (1, 1), (0, 0)))
    return 0.5 * (xp[:, :-2] + xp[:, 2:])


def to_scan_order(t, layer):
    if layer % 2 == 0:
        return t
    b, n, d = t.shape
    rows = n // GRID_W
    return t.reshape(b, rows, GRID_W, d).transpose(0, 2, 1, 3).reshape(b, n, d)


def from_scan_order(t, layer):
    if layer % 2 == 0:
        return t
    b, n, d = t.shape
    rows = n // GRID_W
    return t.reshape(b, GRID_W, rows, d).transpose(0, 2, 1, 3).reshape(b, n, d)


def chunk_linear_attention(q, k, v, log_a, s0):
    f32 = jnp.float32
    q, k, v, log_a = (t.astype(f32) for t in (q, k, v, log_a))
    b, h, t, dk = q.shape
    dv = v.shape[-1]
    n, c = t // CHUNK, CHUNK
    q = q.reshape(b, h, n, c, dk)
    k = k.reshape(b, h, n, c, dk)
    v = v.reshape(b, h, n, c, dv)
    g = jnp.cumsum(log_a.reshape(b, h, n, c, -1), axis=3)
    tri = jnp.tril(jnp.ones((c, c), bool))
    if g.shape[-1] == 1:
        gs = g[..., 0]
        decay = jnp.exp(jnp.where(tri, gs[..., :, None] - gs[..., None, :], -jnp.inf))
        scores = jnp.einsum('bhnik,bhnjk->bhnij', q, k) * decay
    else:
        scores = jnp.einsum('bhnik,bhnjk->bhnij', q * jnp.exp(g), k * jnp.exp(-g))
        scores = jnp.where(tri, scores, 0.0)
    intra = jnp.einsum('bhnij,bhnjv->bhniv', scores, v)
    g_last = g[..., -1:, :]
    chunk_state = jnp.einsum('bhnck,bhncv->bhnkv', k * jnp.exp(g_last - g), v)
    chunk_decay = jnp.exp(g_last[..., 0, :])

    def step(s, inp):
        st, dec = inp
        return s * dec[..., None] + st, s

    s_fin, s_in = lax.scan(step, s0, (jnp.moveaxis(chunk_state, 2, 0), jnp.moveaxis(chunk_decay, 2, 0)))
    inter = jnp.einsum('bhnck,nbhkv->bhncv', q * jnp.exp(g), s_in)
    return (intra + inter).reshape(b, h, t, dv), s_fin


def gated_delta_chunked(q, k, v, g, beta, s0):
    f32 = jnp.float32
    q, k, v, g, beta = (t.astype(f32) for t in (q, k, v, g, beta))
    b, h, t, dk = q.shape
    dv = v.shape[-1]
    n, c = t // CHUNK, CHUNK
    q = q.reshape(b, h, n, c, dk)
    k = k.reshape(b, h, n, c, dk)
    v = v.reshape(b, h, n, c, dv)
    beta = beta.reshape(b, h, n, c)
    g = jnp.cumsum(g.reshape(b, h, n, c), -1)
    tri = jnp.tril(jnp.ones((c, c), bool))
    strict = jnp.tril(jnp.ones((c, c), bool), -1)
    decay = jnp.exp(jnp.where(tri, g[..., :, None] - g[..., None, :], -jnp.inf))
    lower = jnp.where(strict, jnp.einsum('bhnik,bhnjk->bhnij', k, k) * decay * beta[..., None], 0.0)
    rhs = jnp.concatenate([v * beta[..., None], k * (beta * jnp.exp(g))[..., None]], -1)
    sol = lax.linalg.triangular_solve(lower + jnp.eye(c, dtype=f32), rhs, left_side=True, lower=True,
                                      unit_diagonal=True)
    u, w = sol[..., :dv], sol[..., dv:]
    attn = jnp.einsum('bhnik,bhnjk->bhnij', q, k) * decay
    g_last = g[..., -1:]
    q_dec = q * jnp.exp(g)[..., None]
    k_dec = k * jnp.exp(g_last - g)[..., None]

    def step(s, inp):
        q_i, k_i, u_i, w_i, a_i, gl = inp
        v_new = u_i - jnp.einsum('bhck,bhkv->bhcv', w_i, s)
        o = jnp.einsum('bhck,bhkv->bhcv', q_i, s) + jnp.einsum('bhcs,bhsv->bhcv', a_i, v_new)
        s = s * gl[..., None] + jnp.einsum('bhck,bhcv->bhkv', k_i, v_new)
        return s, o

    xs = tuple(jnp.moveaxis(a, 2, 0) for a in (q_dec, k_dec, u, w, attn, jnp.exp(g_last)))
    s_fin, o = lax.scan(step, s0, xs)
    return jnp.moveaxis(o, 0, 2).reshape(b, h, t, dv), s_fin


def rwkv7_scan(r, w, k, v, kk, a, s0):
    f32 = jnp.float32

    def step(s, inp):
        r_t, w_t, k_t, v_t, kk_t, a_t = inp
        sa = jnp.einsum('bhvk,bhk->bhv', s, kk_t)
        s = (s * w_t[:, :, None, :] - sa[..., None] * (kk_t * a_t)[:, :, None, :]
             + v_t[..., None] * k_t[:, :, None, :])
        return s, jnp.einsum('bhvk,bhk->bhv', s, r_t)

    xs = tuple(jnp.moveaxis(t.astype(f32), 2, 0) for t in (r, w, k, v, kk, a))
    s_fin, o = lax.scan(step, s0, xs)
    return jnp.moveaxis(o, 0, 2), s_fin


def bidir(scan_fn, s0, ctx_fwd, ctx_bwd, lat_fwd, lat_bwd):
    rev = lambda args: [jnp.flip(a, axis=2) for a in args]
    oc_f, sc_f = scan_fn(*ctx_fwd, s0)
    ol_f, _ = scan_fn(*lat_fwd, sc_f)
    oc_b, sc_b = scan_fn(*rev(ctx_bwd), s0)
    ol_b, _ = scan_fn(*rev(lat_bwd), sc_b)
    return oc_f + jnp.flip(oc_b, 2), ol_f + jnp.flip(ol_b, 2)


def ssm_mixer(p_ctx, p_lat, lp, need_ctx):
    rep = SSM_HEADS // SSM_GROUPS

    def prep(p):
        b, t, _ = p.shape
        z, xbc, dt = split(p, SSM_COLS)
        xbc = jax.nn.silu(centred_conv(xbc, lp['ssm_conv_w']) + lp['ssm_conv_b'])
        xs, bs, cs = split(xbc, (BRANCH_DIM, SSM_GROUPS * SSM_STATE, SSM_GROUPS * SSM_STATE))
        xh = to_heads(xs, SSM_HEADS)
        bh = jnp.repeat(to_heads(bs, SSM_GROUPS), rep, axis=1)
        ch = jnp.repeat(to_heads(cs, SSM_GROUPS), rep, axis=1)
        dt = jax.nn.softplus(dt.reshape(b, t, 2, SSM_HEADS) + lp['ssm_dt_bias']).transpose(2, 0, 3, 1)
        log_a = (dt * -jnp.exp(lp['ssm_a_log'])[:, None, :, None])[..., None]
        kd = bh[None] * dt[..., None]
        return z, xh, (ch, kd[0], xh, log_a[0]), (ch, kd[1], xh, log_a[1])

    z_c, x_c, fc, bc = prep(p_ctx)
    z_l, x_l, fl, bl = prep(p_lat)
    s0 = jnp.zeros((p_ctx.shape[0], SSM_HEADS, SSM_STATE, SSM_HEAD_DIM), jnp.float32)
    y_c, y_l = bidir(chunk_linear_attention, s0, fc, bc, fl, bl)

    def post(y, xh, z):
        y = y.astype(z.dtype) + lp['ssm_d'][:, None, None] * xh
        y = from_heads(y) * jax.nn.silu(z)
        b, t, _ = y.shape
        return rms_norm(y.reshape(b, t, SSM_GROUPS, -1), lp['ssm_norm'].reshape(SSM_GROUPS, -1)).reshape(b, t, -1)

    return (post(y_c, x_c, z_c) if need_ctx else None), post(y_l, x_l, z_l)


def gla_mixer(p_ctx, p_lat, lp, need_ctx):
    def prep(p):
        b, t, _ = p.shape
        q, k, v, r, glr = split(p, GLA_COLS)
        q = to_heads(q, GLA_HEADS) * GLA_DK ** -0.5
        k = to_heads(k, GLA_HEADS)
        v = to_heads(v, GLA_HEADS)
        glr = glr.reshape(b, t, 2, GLA_GATE_RANK)
        logit = jnp.einsum('bter,erk->ebtk', glr, lp['gla_w2']) + lp['gla_b'][:, None, None, :]
        log_a = jax.nn.log_sigmoid(logit) / GLA_TAU
        return r, (q, k, v, to_heads(log_a[0], GLA_HEADS)), (q, k, v, to_heads(log_a[1], GLA_HEADS))

    r_c, fc, bc = prep(p_ctx)
    r_l, fl, bl = prep(p_lat)
    s0 = jnp.zeros((p_ctx.shape[0], GLA_HEADS, GLA_DK, GLA_DV), jnp.float32)
    y_c, y_l = bidir(chunk_linear_attention, s0, fc, bc, fl, bl)

    def post(y, r):
        return from_heads(rms_norm(y, lp['gla_norm'])).astype(r.dtype) * jax.nn.silu(r)

    return (post(y_c, r_c) if need_ctx else None), post(y_l, r_l)


def head_group_norm(y, w, b):
    yf = y.astype(jnp.float32)
    mu = jnp.mean(yf, -1, keepdims=True)
    var = jnp.mean(jnp.square(yf - mu), -1, keepdims=True)
    return from_heads((yf - mu) * lax.rsqrt(var + RWKV_LN_EPS)) * w + b


def rwkv_mixer(p_ctx, p_lat, lp, need_ctx):
    def prep(p):
        b, t, _ = p.shape
        p = p + lp['rwkv_mu'] * (centred_shift(p) - p)
        rkv, wlr, alr, glr = split(p, RWKV_COLS)
        r, k, v = split(rkv, (BRANCH_DIM,) * 3)
        wlr = wlr.reshape(b, t, 2, RWKV_DECAY_RANK)
        alr = alr.reshape(b, t, 2, RWKV_ICLR_RANK)
        w_raw = lp['rwkv_w0'][:, None, None, :] + jnp.einsum('bter,erc->ebtc', jnp.tanh(wlr), lp['rwkv_w2'])
        w = jnp.exp(-jnp.exp(-jax.nn.softplus(-w_raw) - 0.5))
        a = jax.nn.sigmoid(lp['rwkv_a0'][:, None, None, :] + jnp.einsum('bter,erc->ebtc', alr, lp['rwkv_a2']))
        g = jax.nn.sigmoid(glr) @ lp['rwkv_g2']
        kk = l2_normalize(to_heads(k * lp['rwkv_k_k'], RWKV_HEADS))
        k_dir = k[None] * (1 + (a - 1) * lp['rwkv_k_a'])
        rh = to_heads(r, RWKV_HEADS)
        vh = to_heads(v, RWKV_HEADS)
        kd = [to_heads(k_dir[i], RWKV_HEADS) for i in range(2)]
        wd = [to_heads(w[i], RWKV_HEADS) for i in range(2)]
        ad = [to_heads(a[i], RWKV_HEADS) for i in range(2)]
        r_k = lp['rwkv_r_k'][:, None, :]
        bonus = (jnp.sum(rh * kd[0] * r_k, -1, keepdims=True) + jnp.sum(rh * kd[1] * r_k, -1, keepdims=True)) * vh
        return g, from_heads(bonus), (rh, wd[0], kd[0], vh, kk, ad[0]), (rh, wd[1], kd[1], vh, kk, ad[1])

    g_c, bo_c, fc, bc = prep(p_ctx)
    g_l, bo_l, fl, bl = prep(p_lat)
    s0 = jnp.zeros((p_ctx.shape[0], RWKV_HEADS, RWKV_HEAD, RWKV_HEAD), jnp.float32)
    y_c, y_l = bidir(rwkv7_scan, s0, fc, bc, fl, bl)

    def post(y, bonus, g):
        y = head_group_norm(y, lp['rwkv_ln_w'], lp['rwkv_ln_b']).astype(g.dtype)
        return (y + bonus) * g

    return (post(y_c, bo_c, g_c) if need_ctx else None), post(y_l, bo_l, g_l)


def gdn_mixer(p_ctx, p_lat, lp, need_ctx):
    def prep(p):
        b, t, _ = p.shape
        qkv, z, a_raw, b_raw = split(p, GDN_COLS)
        qkv = jax.nn.silu(centred_conv(qkv, lp['gdn_conv_w']))
        q, k, v = split(qkv, (BRANCH_DIM,) * 3)
        q = l2_normalize(to_heads(q, GDN_HEADS)) * GDN_HEAD ** -0.5
        k = l2_normalize(to_heads(k, GDN_HEADS))
        v = to_heads(v, GDN_HEADS)
        a_raw = a_raw.reshape(b, t, 2, GDN_HEADS)
        b_raw = b_raw.reshape(b, t, 2, GDN_HEADS)
        g = (-jnp.exp(lp['gdn_a_log']) * jax.nn.softplus(a_raw + lp['gdn_dt_bias'])).transpose(2, 0, 3, 1)
        beta = jax.nn.sigmoid(b_raw).transpose(2, 0, 3, 1)
        return z, (q, k, v, g[0], beta[0]), (q, k, v, g[1], beta[1])

    z_c, fc, bc = prep(p_ctx)
    z_l, fl, bl = prep(p_lat)
    s0 = jnp.zeros((p_ctx.shape[0], GDN_HEADS, GDN_HEAD, GDN_HEAD), jnp.float32)
    y_c, y_l = bidir(gated_delta_chunked, s0, fc, bc, fl, bl)

    def post(y, z):
        return from_heads(rms_norm(y, lp['gdn_norm'])).astype(z.dtype) * jax.nn.silu(z)

    return (post(y_c, z_c) if need_ctx else None), post(y_l, z_l)


def mixer_block(h_ctx, h_lat, lp, need_ctx):
    pc = split(h_ctx @ lp['w_in'], MIXER_COLS)
    pl = split(h_lat @ lp['w_in'], MIXER_COLS)
    mixers = (ssm_mixer, gla_mixer, rwkv_mixer, gdn_mixer)
    outs = [m(p_c, p_l, lp, need_ctx) for m, p_c, p_l in zip(mixers, pc, pl)]

    def merge(h, ys):
        acc = None
        for i, y in enumerate(ys):
            term = jax.nn.sigmoid(h @ lp['w_gate'][i]) * (y @ lp['w_branch'][i])
            acc = term if acc is None else acc + term
        return acc @ lp['w_out']

    y_lat = merge(h_lat, [o[1] for o in outs])
    y_ctx = merge(h_ctx, [o[0] for o in outs]) if need_ctx else None
    return y_ctx, y_lat


def moe_ffn(h, router_w, router_b, w_gate, w_up, w_down):
    scores = jax.nn.sigmoid((h @ router_w).astype(jnp.float32))
    sel = scores + router_b.astype(jnp.float32)
    grouped = sel.reshape(*sel.shape[:-1], N_GROUPS, EXPERTS_PER_GROUP)
    group = jnp.argmax(jnp.sum(lax.top_k(grouped, TOP_K)[0], -1), -1)
    in_group = (jnp.arange(N_EXPERTS) // EXPERTS_PER_GROUP) == group[..., None]
    _, idx = lax.top_k(jnp.where(in_group, sel, -jnp.inf), TOP_K)
    wts = jnp.take_along_axis(scores, idx, -1)
    wts = wts / jnp.sum(wts, -1, keepdims=True)
    gate = jnp.sum(jax.nn.one_hot(idx, N_EXPERTS, dtype=wts.dtype) * wts[..., None], -2).astype(h.dtype)
    out = None
    for e in range(N_EXPERTS):
        y = (jax.nn.silu(h @ w_gate[e]) * (h @ w_up[e])) @ w_down[e]
        term = gate[..., e:e + 1] * y
        out = term if out is None else out + term
    return out


def setup_inputs(seed: int = 0) -> dict:
    key = jax.random.key(seed)
    ks = iter(jax.random.split(key, 64))
    D, L = D_MODEL, DEPTH

    def nrm(shape, scale):
        return jax.random.normal(next(ks), shape, jnp.float32) * scale

    def gain(shape):
        return 1.0 + nrm(shape, 0.02)

    def a_log(shape):
        return jnp.log(jax.random.uniform(next(ks), shape, jnp.float32, 1.0, 16.0))

    def dt_bias(shape):
        dt = jnp.exp(jax.random.uniform(next(ks), shape, jnp.float32, math.log(1e-3), math.log(1e-1)))
        return dt + jnp.log(-jnp.expm1(-dt))

    return {
        'x': nrm((BATCH, SEQ, D), 1.0),
        'c': nrm((BATCH, D), 1.0),
        'ctx': nrm((BATCH, CTX_LEN, D), 1.0),
        'c_ctx': nrm((D,), 1.0),
        'ada_w': nrm((L, D, 6 * D), 0.5 * D ** -0.5),
        'ada_b': nrm((L, 6 * D), 0.01),
        'norm_mix': gain((L, D)),
        'norm_ffn': gain((L, D)),
        'w_in': nrm((L, D, N_IN), D ** -0.5),
        'w_gate': nrm((L, N_BRANCH, D, D), D ** -0.5),
        'w_branch': nrm((L, N_BRANCH, BRANCH_DIM, D), BRANCH_DIM ** -0.5),
        'w_out': nrm((L, D, D), D ** -0.5),
        'ssm_conv_w': nrm((L, CONV_K, SSM_COLS[1]), CONV_K ** -0.5),
        'ssm_conv_b': nrm((L, SSM_COLS[1]), 0.02),
        'ssm_a_log': a_log((L, 2, SSM_HEADS)),
        'ssm_dt_bias': dt_bias((L, 2, SSM_HEADS)),
        'ssm_d': 1.0 + nrm((L, SSM_HEADS), 0.1),
        'ssm_norm': gain((L, BRANCH_DIM)),
        'gla_w2': nrm((L, 2, GLA_GATE_RANK, GLA_HEADS * GLA_DK), GLA_GATE_RANK ** -0.5),
        'gla_b': nrm((L, 2, GLA_HEADS * GLA_DK), 0.1),
        'gla_norm': gain((L, GLA_DV)),
        'rwkv_mu': jax.random.uniform(next(ks), (L, MIXER_COLS[2]), jnp.float32),
        'rwkv_w0': jax.random.uniform(next(ks), (L, 2, BRANCH_DIM), jnp.float32, -6.5, -1.5),
        'rwkv_w2': nrm((L, 2, RWKV_DECAY_RANK, BRANCH_DIM), 0.5 * RWKV_DECAY_RANK ** -0.5),
        'rwkv_a0': nrm((L, 2, BRANCH_DIM), 0.1),
        'rwkv_a2': nrm((L, 2, RWKV_ICLR_RANK, BRANCH_DIM), RWKV_ICLR_RANK ** -0.5),
        'rwkv_g2': nrm((L, RWKV_GATE_RANK, BRANCH_DIM), RWKV_GATE_RANK ** -0.5),
        'rwkv_k_k': 0.85 + nrm((L, BRANCH_DIM), 0.02),
        'rwkv_k_a': 1.0 + nrm((L, BRANCH_DIM), 0.02),
        'rwkv_r_k': nrm((L, RWKV_HEADS, RWKV_HEAD), 0.1),
        'rwkv_ln_w': gain((L, BRANCH_DIM)),
        'rwkv_ln_b': nrm((L, BRANCH_DIM), 0.01),
        'gdn_conv_w': nrm((L, CONV_K, 3 * BRANCH_DIM), CONV_K ** -0.5),
        'gdn_a_log': a_log((L, 2, GDN_HEADS)),
        'gdn_dt_bias': dt_bias((L, 2, GDN_HEADS)),
        'gdn_norm': gain((L, GDN_HEAD)),
        'router_w': nrm((D, N_EXPERTS), D ** -0.5),
        'router_b': nrm((N_EXPERTS,), 0.01),
        'moe_w_gate': nrm((L, N_EXPERTS, D, EXPERT_FF), D ** -0.5),
        'moe_w_up': nrm((L, N_EXPERTS, D, EXPERT_FF), D ** -0.5),
        'moe_w_down': nrm((L, N_EXPERTS, EXPERT_FF, D), EXPERT_FF ** -0.5),
        'final_norm': gain((D,)),
    }


def reference(x, c, ctx, c_ctx, ada_w, ada_b, norm_mix, norm_ffn, w_in, w_gate, w_branch, w_out,
              ssm_conv_w, ssm_conv_b, ssm_a_log, ssm_dt_bias, ssm_d, ssm_norm,
              gla_w2, gla_b, gla_norm,
              rwkv_mu, rwkv_w0, rwkv_w2, rwkv_a0, rwkv_a2, rwkv_g2, rwkv_k_k, rwkv_k_a, rwkv_r_k,
              rwkv_ln_w, rwkv_ln_b,
              gdn_conv_w, gdn_a_log, gdn_dt_bias, gdn_norm,
              router_w, router_b, moe_w_gate, moe_w_up, moe_w_down, final_norm):
    xc = ctx
    silu_c = jax.nn.silu(c)
    silu_cc = jax.nn.silu(c_ctx)
    for l in range(DEPTH):
        last = l == DEPTH - 1
        lp = dict(w_in=w_in[l], w_gate=w_gate[l], w_branch=w_branch[l], w_out=w_out[l],
                  ssm_conv_w=ssm_conv_w[l], ssm_conv_b=ssm_conv_b[l], ssm_a_log=ssm_a_log[l],
                  ssm_dt_bias=ssm_dt_bias[l], ssm_d=ssm_d[l], ssm_norm=ssm_norm[l],
                  gla_w2=gla_w2[l], gla_b=gla_b[l], gla_norm=gla_norm[l],
                  rwkv_mu=rwkv_mu[l], rwkv_w0=rwkv_w0[l], rwkv_w2=rwkv_w2[l], rwkv_a0=rwkv_a0[l],
                  rwkv_a2=rwkv_a2[l], rwkv_g2=rwkv_g2[l], rwkv_k_k=rwkv_k_k[l], rwkv_k_a=rwkv_k_a[l],
                  rwkv_r_k=rwkv_r_k[l], rwkv_ln_w=rwkv_ln_w[l], rwkv_ln_b=rwkv_ln_b[l],
                  gdn_conv_w=gdn_conv_w[l], gdn_a_log=gdn_a_log[l], gdn_dt_bias=gdn_dt_bias[l],
                  gdn_norm=gdn_norm[l])
        mod = jnp.split((silu_c @ ada_w[l] + ada_b[l])[:, None, :], 6, axis=-1)
        mod_c = jnp.split(silu_cc @ ada_w[l] + ada_b[l], 6, axis=-1)
        h = to_scan_order(modulate(rms_norm(x, norm_mix[l]), mod[0], mod[1]), l)
        hc = modulate(rms_norm(xc, norm_mix[l]), mod_c[0], mod_c[1])
        y_ctx, y_lat = mixer_block(hc, h, lp, not last)
        x = x + mod[2] * from_scan_order(y_lat, l)
        moe_w = (router_w, router_b, moe_w_gate[l], moe_w_up[l], moe_w_down[l])
        x = x + mod[5] * moe_ffn(modulate(rms_norm(x, norm_ffn[l]), mod[3], mod[4]), *moe_w)
        if not last:
            xc = xc + mod_c[2] * y_ctx
            xc = xc + mod_c[5] * moe_ffn(modulate(rms_norm(xc, norm_ffn[l]), mod_c[3], mod_c[4]), *moe_w)
    return rms_norm(x, final_norm)
```

```python
import functools

import numpy as np
import jax
import jax.numpy as jnp
from jax import lax
from jax.experimental import pallas as pl
from jax.experimental.pallas import tpu as pltpu

F32 = jnp.float32
BF16 = jnp.bfloat16
HI = lax.Precision.HIGHEST

D_MODEL = 1024
DEPTH = 2
GRID_W = 64
CHUNK = 64
EPS = 1e-6
BRANCH = D_MODEL // 2
SSM_HEADS, SSM_P, SSM_GROUPS, SSM_N = 8, 64, 2, 64
GLA_HEADS, GLA_DK, GLA_DV, GLA_RANK, GLA_TAU = 4, 64, 128, 16, 16.0
RWKV_HEADS, RWKV_N, RWKV_LN_EPS = 8, 64, 64e-5
GDN_HEADS, GDN_N = 4, 128
N_EXPERTS, N_GROUPS, EXPERTS_PER_GROUP = 16, 4, 4
EXPERT_FF = D_MODEL // 2
LANES = 128

_IN_BLOCKS = (
    ("ssm_z", 512), ("ssm_xbc", 768), ("ssm_dt", 16),
    ("gla_q", 256), ("gla_k", 256), ("gla_v", 512), ("gla_r", 512), ("gla_glr", 32),
    ("rwkv_rkv", 1536), ("rwkv_wlr", 128), ("rwkv_alr", 128), ("rwkv_glr", 128),
    ("gdn_qkv", 1536), ("gdn_z", 512), ("gdn_ab", 16),
)


def _packed_layout():
    src, offs, s, o = [], {}, 0, 0
    for name, w in _IN_BLOCKS:
        wp = -(-w // LANES) * LANES
        offs[name] = o
        src += list(range(s, s + w)) + [-1] * (wp - w)
        s += w
        o += wp
    return np.asarray(src, np.int32), offs, o


_SRC_COLS, _OFF, N_PACKED = _packed_layout()
VMEM_LIMIT = 48 * 1024 * 1024


def _cparams(*sem):
    return pltpu.CompilerParams(dimension_semantics=sem, vmem_limit_bytes=VMEM_LIMIT)


def _dot(a, b):
    return jnp.dot(a.astype(BF16), b.astype(BF16), preferred_element_type=F32)


def _dot_nt(a, b):
    return lax.dot_general(a.astype(BF16), b.astype(BF16), (((1,), (1,)), ((), ())),
                           preferred_element_type=F32)


def _dot_tn(a, b):
    return lax.dot_general(a.astype(BF16), b.astype(BF16), (((0,), (0,)), ((), ())),
                           preferred_element_type=F32)


def _dot_hi(a, b):
    return jnp.dot(a, b, precision=HI, preferred_element_type=F32)


def _mm_kernel(a_ref, w_ref, o_ref, *, act, precise):
    if precise:
        r = _dot_hi(a_ref[...].astype(F32), w_ref[...].astype(F32))
    else:
        r = _dot(a_ref[...], w_ref[...])
    if act == "sigmoid":
        r = jax.nn.sigmoid(r)
    o_ref[...] = r.astype(o_ref.dtype)


def _pick_tile(m, pref):
    t = pref
    while m % t:
        t //= 2
    return t


def pmatmul(a, w, *, tm, tn, act=None, precise=False, out_dtype=F32):
    m, k = a.shape
    n = w.shape[1]
    tm = _pick_tile(m, tm)
    assert tm % 8 == 0 and n % tn == 0, (m, tm, n, tn)
    return pl.pallas_call(
        functools.partial(_mm_kernel, act=act, precise=precise),
        grid=(n // tn, m // tm),
        in_specs=[pl.BlockSpec((tm, k), lambda j, i: (i, 0)),
                  pl.BlockSpec((k, tn), lambda j, i: (0, j))],
        out_specs=pl.BlockSpec((tm, tn), lambda j, i: (i, j)),
        out_shape=jax.ShapeDtypeStruct((m, n), out_dtype),
        compiler_params=_cparams("parallel", "parallel"),
    )(a, w)


def _norm_mod_kernel(x_ref, w_ref, shift_ref, scale_ref, o_ref):
    x = x_ref[0]
    y = x * lax.rsqrt(jnp.mean(x * x, axis=-1, keepdims=True) + EPS) * w_ref[...]
    o_ref[0] = (y * (1.0 + scale_ref[0, 0]) + shift_ref[0, 0]).astype(o_ref.dtype)


def norm_modulate(x_all, w, shift, scale, *, n_ctx, tm, out_dtype=BF16):
    b, t, d = x_all.shape
    assert n_ctx % tm == 0 and t % tm == 0
    nct = n_ctx // tm
    sel = lambda bi, i: (bi, jnp.where(i < nct, 0, 1), 0, 0)
    return pl.pallas_call(
        _norm_mod_kernel,
        grid=(b, t // tm),
        in_specs=[pl.BlockSpec((1, tm, d), lambda bi, i: (bi, i, 0)),
                  pl.BlockSpec((1, d), lambda bi, i: (0, 0)),
                  pl.BlockSpec((1, 1, 1, d), sel),
                  pl.BlockSpec((1, 1, 1, d), sel)],
        out_specs=pl.BlockSpec((1, tm, d), lambda bi, i: (bi, i, 0)),
        out_shape=jax.ShapeDtypeStruct((b, t, d), out_dtype),
        compiler_params=_cparams("parallel", "parallel"),
    )(x_all, w.reshape(1, d), shift, scale)


def _chunk_masks(reverse):
    row = lax.broadcasted_iota(jnp.int32, (CHUNK, CHUNK), 0)
    col = lax.broadcasted_iota(jnp.int32, (CHUNK, CHUNK), 1)
    if reverse:
        return col >= row, col > row, row >= col
    return col <= row, col < row, row <= col


def _chunk_order(i, n_ctx_chunks, n_chunks, reverse):
    if not reverse:
        return i
    return jnp.where(i < n_ctx_chunks, n_ctx_chunks - 1 - i, n_chunks - 1 - (i - n_ctx_chunks))


def _unit_tri_inverse(a):
    row = lax.broadcasted_iota(jnp.int32, (CHUNK, CHUNK), 0)
    col = lax.broadcasted_iota(jnp.int32, (CHUNK, CHUNK), 1)
    eye = (row == col).astype(F32)
    t = eye - a
    p = a
    for _ in range(int(np.log2(CHUNK)) - 1):
        p = _dot_hi(p, p)
        t = t + _dot_hi(t, p)
    return t


def _ssd_kernel(x_ref, b_ref, c_ref, la_ref, lat_ref, dt_ref, dtt_ref, o_ref, s_ref, *, reverse):
    @pl.when(pl.program_id(1) == 0)
    def _():
        s_ref[...] = jnp.zeros_like(s_ref)

    incl, _, incl_t = _chunk_masks(reverse)
    last = 0 if reverse else CHUNK - 1
    g = _dot_hi(incl.astype(F32), la_ref[0])
    gt = _dot_hi(lat_ref[0, 0], incl_t.astype(F32))
    dt = dt_ref[0]
    dtt = dtt_ref[0, 0]
    x = x_ref[0]
    rep = SSM_HEADS // SSM_GROUPS
    for grp in range(SSM_GROUPS):
        bm = b_ref[0, :, grp * SSM_N:(grp + 1) * SSM_N]
        cm = c_ref[0, :, grp * SSM_N:(grp + 1) * SSM_N]
        cb = _dot_nt(cm, bm)
        for h in range(grp * rep, (grp + 1) * rep):
            gh = g[:, h:h + 1]
            decay = jnp.exp(jnp.where(incl, gh - gt[h:h + 1, :], -jnp.inf))
            scores = cb * decay * dtt[h:h + 1, :]
            xh = x[:, h * SSM_P:(h + 1) * SSM_P]
            s = s_ref[h]
            o = _dot(scores, xh) + _dot(cm * jnp.exp(gh), s)
            o_ref[0, :, h * SSM_P:(h + 1) * SSM_P] = o
            gl = g[last:last + 1, h:h + 1]
            kdec = bm * (dt[:, h:h + 1] * jnp.exp(gl - gh))
            s_ref[h] = s * jnp.exp(gl) + _dot_tn(kdec, xh)


def ssd_scan(x, bm, cm, la, dt, *, n_ctx, reverse):
    b, t, _ = x.shape
    nc, ncc = t // CHUNK, n_ctx // CHUNK
    lat = la.reshape(b, nc, CHUNK, SSM_HEADS).transpose(0, 1, 3, 2)
    dtt = dt.reshape(b, nc, CHUNK, SSM_HEADS).transpose(0, 1, 3, 2)
    tok = lambda bi, i: (bi, _chunk_order(i, ncc, nc, reverse), 0)
    tokt = lambda bi, i: (bi, _chunk_order(i, ncc, nc, reverse), 0, 0)
    specs = [pl.BlockSpec((1, CHUNK, BRANCH), tok),
             pl.BlockSpec((1, CHUNK, SSM_GROUPS * SSM_N), tok),
             pl.BlockSpec((1, CHUNK, SSM_GROUPS * SSM_N), tok),
             pl.BlockSpec((1, CHUNK, SSM_HEADS), tok),
             pl.BlockSpec((1, 1, SSM_HEADS, CHUNK), tokt),
             pl.BlockSpec((1, CHUNK, SSM_HEADS), tok),
             pl.BlockSpec((1, 1, SSM_HEADS, CHUNK), tokt)]
    return pl.pallas_call(
        functools.partial(_ssd_kernel, reverse=reverse),
        grid=(b, nc), in_specs=specs,
        out_specs=pl.BlockSpec((1, CHUNK, BRANCH), tok),
        out_shape=jax.ShapeDtypeStruct((b, t, BRANCH), F32),
        scratch_shapes=[pltpu.VMEM((SSM_HEADS, SSM_N, SSM_P), F32)],
        compiler_params=_cparams("parallel", "arbitrary"),
    )(x, bm, cm, la, lat, dt, dtt)


def _gla_kernel(q_ref, k_ref, v_ref, la_ref, o_ref, s_ref, *, reverse):
    @pl.when(pl.program_id(1) == 0)
    def _():
        s_ref[...] = jnp.zeros_like(s_ref)

    incl, _, _ = _chunk_masks(reverse)
    last = 0 if reverse else CHUNK - 1
    g_all = _dot_hi(incl.astype(F32), la_ref[0])
    for h in range(GLA_HEADS):
        ks = slice(h * GLA_DK, (h + 1) * GLA_DK)
        vs = slice(h * GLA_DV, (h + 1) * GLA_DV)
        g = g_all[:, ks]
        gl = g[last:last + 1, :]
        q = q_ref[0, :, ks]
        k = k_ref[0, :, ks]
        v = v_ref[0, :, vs]
        qg = q * jnp.exp(g)
        scores = jnp.where(incl, _dot_nt(qg, k * jnp.exp(-g)), 0.0)
        st = s_ref[h]
        o_ref[0, :, vs] = _dot(scores, v) + _dot_nt(qg, st)
        s_ref[h] = st * jnp.exp(gl) + _dot_tn(v, k * jnp.exp(gl - g))


def gla_scan(q, k, v, la, *, n_ctx, reverse):
    b, t, _ = q.shape
    nc, ncc = t // CHUNK, n_ctx // CHUNK
    tok = lambda bi, i: (bi, _chunk_order(i, ncc, nc, reverse), 0)
    kw = GLA_HEADS * GLA_DK
    specs = [pl.BlockSpec((1, CHUNK, kw), tok), pl.BlockSpec((1, CHUNK, kw), tok),
             pl.BlockSpec((1, CHUNK, BRANCH), tok), pl.BlockSpec((1, CHUNK, kw), tok)]
    return pl.pallas_call(
        functools.partial(_gla_kernel, reverse=reverse),
        grid=(b, nc), in_specs=specs,
        out_specs=pl.BlockSpec((1, CHUNK, BRANCH), tok),
        out_shape=jax.ShapeDtypeStruct((b, t, BRANCH), F32),
        scratch_shapes=[pltpu.VMEM((GLA_HEADS, GLA_DV, GLA_DK), F32)],
        compiler_params=_cparams("parallel", "arbitrary"),
    )(q, k, v, la)


def _rwkv_kernel(r_ref, k_ref, v_ref, kk_ref, a_ref, lw_ref, o_ref, s_ref, *, reverse):
    @pl.when(pl.program_id(1) == 0)
    def _():
        s_ref[...] = jnp.zeros_like(s_ref)

    incl, strict, _ = _chunk_masks(reverse)
    last = 0 if reverse else CHUNK - 1
    lw_all = lw_ref[0]
    g_all = _dot_hi(incl.astype(F32), lw_all)
    for h in range(RWKV_HEADS):
        hs = slice(h * RWKV_N, (h + 1) * RWKV_N)
        g = g_all[:, hs]
        gl = g[last:last + 1, :]
        eneg = jnp.exp(-g)
        kk = kk_ref[0, :, hs]
        bvec = kk * a_ref[0, :, hs]
        k = k_ref[0, :, hs]
        v = v_ref[0, :, hs]
        kkg = kk * jnp.exp(g - lw_all[:, hs])
        rg = r_ref[0, :, hs] * jnp.exp(g)
        bh = bvec * eneg
        kh = k * eneg
        a_mat = jnp.where(strict, _dot_nt(kkg, bh), 0.0)
        b_mat = jnp.where(strict, _dot_nt(kkg, kh), 0.0)
        t_inv = _unit_tri_inverse(a_mat)
        s = s_ref[h]
        rhs = _dot(b_mat, v) + _dot_nt(kkg, s)
        u = -_dot_hi(t_inv, rhs)
        o = (_dot_nt(rg, s) + _dot(jnp.where(incl, _dot_nt(rg, bh), 0.0), u)
             + _dot(jnp.where(incl, _dot_nt(rg, kh), 0.0), v))
        o_ref[0, :, hs] = o
        edec = jnp.exp(gl - g)
        s_ref[h] = s * jnp.exp(gl) + _dot_tn(u, bvec * edec) + _dot_tn(v, k * edec)


def rwkv_scan(r, k, v, kk, a, lw, *, n_ctx, reverse):
    b, t, _ = r.shape
    nc, ncc = t // CHUNK, n_ctx // CHUNK
    tok = lambda bi, i: (bi, _chunk_order(i, ncc, nc, reverse), 0)
    spec = pl.BlockSpec((1, CHUNK, BRANCH), tok)
    return pl.pallas_call(
        functools.partial(_rwkv_kernel, reverse=reverse),
        grid=(b, nc), in_specs=[spec] * 6,
        out_specs=spec,
        out_shape=jax.ShapeDtypeStruct((b, t, BRANCH), F32),
        scratch_shapes=[pltpu.VMEM((RWKV_HEADS, RWKV_N, RWKV_N), F32)],
        compiler_params=_cparams("parallel", "arbitrary"),
    )(r, k, v, kk, a, lw)


def _gdn_kernel(q_ref, k_ref, v_ref, gl_ref, glt_ref, beta_ref, o_ref, s_ref, *, reverse):
    @pl.when(pl.program_id(1) == 0)
    def _():
        s_ref[...] = jnp.zeros_like(s_ref)

    incl, strict, incl_t = _chunk_masks(reverse)
    last = 0 if reverse else CHUNK - 1
    g_all = _dot_hi(incl.astype(F32), gl_ref[0])
    gt_all = _dot_hi(glt_ref[0, 0], incl_t.astype(F32))
    for h in range(GDN_HEADS):
        hs = slice(h * GDN_N, (h + 1) * GDN_N)
        g = g_all[:, h:h + 1]
        gl = g[last:last + 1, :]
        beta = beta_ref[0, :, h:h + 1]
        q = q_ref[0, :, hs]
        k = k_ref[0, :, hs]
        v = v_ref[0, :, hs]
        decay = jnp.exp(jnp.where(incl, g - gt_all[h:h + 1, :], -jnp.inf))
        lower = jnp.where(strict, _dot_nt(k, k) * decay * beta, 0.0)
        t_inv = _unit_tri_inverse(lower)
        u = _dot_hi(t_inv, v * beta)
        w = _dot_hi(t_inv, k * (beta * jnp.exp(g)))
        attn = _dot_nt(q, k) * decay
        s = s_ref[h]
        v_new = u - _dot(w, s)
        o_ref[0, :, hs] = _dot(q * jnp.exp(g), s) + _dot(attn, v_new)
        s_ref[h] = s * jnp.exp(gl) + _dot_tn(k * jnp.exp(gl - g), v_new)


def gdn_scan(q, k, v, gl, beta, *, n_ctx, reverse):
    b, t, _ = q.shape
    nc, ncc = t // CHUNK, n_ctx // CHUNK
    glt = gl.reshape(b, nc, CHUNK, GDN_HEADS).transpose(0, 1, 3, 2)
    tok = lambda bi, i: (bi, _chunk_order(i, ncc, nc, reverse), 0)
    tokt = lambda bi, i: (bi, _chunk_order(i, ncc, nc, reverse), 0, 0)
    big = pl.BlockSpec((1, CHUNK, BRANCH), tok)
    small = pl.BlockSpec((1, CHUNK, GDN_HEADS), tok)
    return pl.pallas_call(
        functools.partial(_gdn_kernel, reverse=reverse),
        grid=(b, nc),
        in_specs=[big, big, big, small, pl.BlockSpec((1, 1, GDN_HEADS, CHUNK), tokt), small],
        out_specs=big,
        out_shape=jax.ShapeDtypeStruct((b, t, BRANCH), F32),
        scratch_shapes=[pltpu.VMEM((GDN_HEADS, GDN_N, GDN_N), F32)],
        compiler_params=_cparams("parallel", "arbitrary"),
    )(q, k, v, gl, glt, beta)


def _merge_kernel(y_ref, g_ref, wb_ref, wo_ref, x_ref, m_ref, o_ref):
    acc = None
    for i in range(4):
        term = g_ref[0, :, i * D_MODEL:(i + 1) * D_MODEL].astype(F32) * _dot(
            y_ref[0, :, i * BRANCH:(i + 1) * BRANCH], wb_ref[i])
        acc = term if acc is None else acc + term
    o_ref[0] = x_ref[0] + m_ref[0, 0] * _dot(acc, wo_ref[...])


def merge_residual(y_cat, gates, w_branch, w_out, x_all, gate_mod, *, n_ctx, tm):
    b, t, d = x_all.shape
    nct = n_ctx // tm
    sel = lambda bi, i: (bi, jnp.where(i < nct, 0, 1), 0, 0)
    tok = lambda bi, i: (bi, i, 0)
    return pl.pallas_call(
        _merge_kernel,
        grid=(b, t // tm),
        in_specs=[pl.BlockSpec((1, tm, 4 * BRANCH), tok),
                  pl.BlockSpec((1, tm, 4 * d), tok),
                  pl.BlockSpec((4, BRANCH, d), lambda bi, i: (0, 0, 0)),
                  pl.BlockSpec((d, d), lambda bi, i: (0, 0)),
                  pl.BlockSpec((1, tm, d), tok),
                  pl.BlockSpec((1, 1, 1, d), sel)],
        out_specs=pl.BlockSpec((1, tm, d), tok),
        out_shape=jax.ShapeDtypeStruct((b, t, d), F32),
        compiler_params=_cparams("parallel", "parallel"),
    )(y_cat, gates, w_branch, w_out, x_all, gate_mod)


def _route_kernel(h_ref, rw_ref, rb_ref, o_ref):
    logits = lax.dot_general(rw_ref[...], h_ref[...].astype(F32), (((1,), (1,)), ((), ())),
                             precision=HI, preferred_element_type=F32)
    scores = jax.nn.sigmoid(logits)
    sel = scores + rb_ref[...]
    rows = [sel[e:e + 1, :] for e in range(N_EXPERTS)]
    sc = [scores[e:e + 1, :] for e in range(N_EXPERTS)]
    neg = jnp.full_like(rows[0], -jnp.inf)

    def top2(vals):
        v1, i1 = vals[0], jnp.zeros(vals[0].shape, jnp.int32)
        for j in range(1, len(vals)):
            better = vals[j] > v1
            v1 = jnp.where(better, vals[j], v1)
            i1 = jnp.where(better, j, i1)
        v2, i2 = None, None
        for j in range(len(vals)):
            if v2 is None:
                v2 = jnp.where(i1 == 0, vals[1], vals[0])
                i2 = jnp.where(i1 == 0, 1, 0)
                continue
            better = (vals[j] > v2) & (i1 != j)
            v2 = jnp.where(better, vals[j], v2)
            i2 = jnp.where(better, j, i2)
        return v1, i1, v2, i2

    gsum = []
    for grp in range(N_GROUPS):
        v1, _, v2, _ = top2(rows[grp * EXPERTS_PER_GROUP:(grp + 1) * EXPERTS_PER_GROUP])
        gsum.append(v1 + v2)
    best, gidx = gsum[0], jnp.zeros(gsum[0].shape, jnp.int32)
    for grp in range(1, N_GROUPS):
        better = gsum[grp] > best
        best = jnp.where(better, gsum[grp], best)
        gidx = jnp.where(better, grp, gidx)
    chosen = [rows[0]] * EXPERTS_PER_GROUP
    chosen_sc = [sc[0]] * EXPERTS_PER_GROUP
    for j in range(EXPERTS_PER_GROUP):
        cj, sj = rows[j], sc[j]
        for grp in range(1, N_GROUPS):
            cj = jnp.where(gidx == grp, rows[grp * EXPERTS_PER_GROUP + j], cj)
            sj = jnp.where(gidx == grp, sc[grp * EXPERTS_PER_GROUP + j], sj)
        chosen[j], chosen_sc[j] = cj, sj
    _, i1, _, i2 = top2(chosen)
    w1, w2 = jnp.zeros_like(best), jnp.zeros_like(best)
    for j in range(EXPERTS_PER_GROUP):
        w1 = jnp.where(i1 == j, chosen_sc[j], w1)
        w2 = jnp.where(i2 == j, chosen_sc[j], w2)
    tot = w1 + w2
    w1, w2 = w1 / tot, w2 / tot
    e1 = gidx * EXPERTS_PER_GROUP + i1
    e2 = gidx * EXPERTS_PER_GROUP + i2
    eid = lax.broadcasted_iota(jnp.int32, scores.shape, 0)
    o_ref[...] = jnp.where(eid == e1, w1, 0.0) + jnp.where(eid == e2, w2, 0.0)


def moe_route(h, router_w, router_b, *, tm):
    m, d = h.shape
    return pl.pallas_call(
        _route_kernel,
        grid=(m // tm,),
        in_specs=[pl.BlockSpec((tm, d), lambda i: (i, 0)),
                  pl.BlockSpec((N_EXPERTS, d), lambda i: (0, 0)),
                  pl.BlockSpec((N_EXPERTS, 1), lambda i: (0, 0))],
        out_specs=pl.BlockSpec((N_EXPERTS, tm), lambda i: (0, i)),
        out_shape=jax.ShapeDtypeStruct((N_EXPERTS, m), F32),
        compiler_params=_cparams("parallel"),
    )(h, router_w.T, router_b.reshape(N_EXPERTS, 1))


def _expert_kernel(h_ref, g_ref, wg_ref, wu_ref, wd_ref, x_ref, m_ref, o_ref, acc_ref):
    e = pl.program_id(2)

    @pl.when(e == 0)
    def _():
        acc_ref[...] = jnp.zeros_like(acc_ref)

    h = h_ref[0]
    gates = g_ref[0]
    lane = lax.broadcasted_iota(jnp.int32, gates.shape, 1)
    ge = jnp.sum(jnp.where(lane == e, gates, 0.0), axis=1, keepdims=True)
    hid = jax.nn.silu(_dot(h, wg_ref[0])) * _dot(h, wu_ref[0])
    acc_ref[...] += ge * _dot(hid, wd_ref[0])

    @pl.when(e == N_EXPERTS - 1)
    def _():
        o_ref[0] = x_ref[0] + m_ref[0, 0] * acc_ref[...]


def moe_experts(h, gates, wg, wu, wd, x_all, gate_mod, *, n_ctx, tm):
    b, t, d = x_all.shape
    nct = n_ctx // tm
    sel = lambda bi, i, e: (bi, jnp.where(i < nct, 0, 1), 0, 0)
    tok = lambda bi, i, e: (bi, i, 0)
    return pl.pallas_call(
        _expert_kernel,
        grid=(b, t // tm, N_EXPERTS),
        in_specs=[pl.BlockSpec((1, tm, d), tok),
                  pl.BlockSpec((1, tm, N_EXPERTS), tok),
                  pl.BlockSpec((1, d, EXPERT_FF), lambda bi, i, e: (e, 0, 0)),
                  pl.BlockSpec((1, d, EXPERT_FF), lambda bi, i, e: (e, 0, 0)),
                  pl.BlockSpec((1, EXPERT_FF, d), lambda bi, i, e: (e, 0, 0)),
                  pl.BlockSpec((1, tm, d), tok),
                  pl.BlockSpec((1, 1, 1, d), sel)],
        out_specs=pl.BlockSpec((1, tm, d), tok),
        out_shape=jax.ShapeDtypeStruct((b, t, d), F32),
        scratch_shapes=[pltpu.VMEM((tm, d), F32)],
        compiler_params=_cparams("parallel", "parallel", "arbitrary"),
    )(h, gates, wg, wu, wd, x_all, gate_mod)


def _final_norm_kernel(x_ref, w_ref, o_ref):
    x = x_ref[...]
    o_ref[...] = x * lax.rsqrt(jnp.mean(x * x, axis=-1, keepdims=True) + EPS) * w_ref[...]


def final_rms_norm(x, w, *, tm):
    m, d = x.shape
    return pl.pallas_call(
        _final_norm_kernel,
        grid=(m // tm,),
        in_specs=[pl.BlockSpec((tm, d), lambda i: (i, 0)), pl.BlockSpec((1, d), lambda i: (0, 0))],
        out_specs=pl.BlockSpec((tm, d), lambda i: (i, 0)),
        out_shape=jax.ShapeDtypeStruct((m, d), F32),
        compiler_params=_cparams("parallel"),
    )(x, w.reshape(1, d))


def _prev_tok(x, n_ctx):
    t = jnp.arange(x.shape[1])[None, :, None]
    return jnp.where(t == n_ctx, 0.0, jnp.pad(x, ((0, 0), (1, 0), (0, 0)))[:, :-1])


def _next_tok(x, n_ctx):
    t = jnp.arange(x.shape[1])[None, :, None]
    return jnp.where(t == n_ctx - 1, 0.0, jnp.pad(x, ((0, 0), (0, 1), (0, 0)))[:, 1:])


def _conv3(x, w, n_ctx):
    return _prev_tok(x, n_ctx) * w[0] + x * w[1] + _next_tok(x, n_ctx) * w[2]


def _group_rms(y, w, n):
    b, t, c = y.shape
    yg = y.reshape(b, t, c // n, n)
    yg = yg * lax.rsqrt(jnp.mean(yg * yg, -1, keepdims=True) + EPS)
    return yg.reshape(b, t, c) * w


def _blk(p, name, width):
    o = _OFF[name]
    return p[..., o:o + width]


def _both(fn, *args, **kw):
    return fn(*args, reverse=False, **kw) + fn(*args, reverse=True, **kw)


def ssm_mixer(p, lp, n_ctx):
    b, t, _ = p.shape
    z = _blk(p, "ssm_z", BRANCH)
    xbc = jax.nn.silu(_conv3(_blk(p, "ssm_xbc", 768), lp["ssm_conv_w"], n_ctx) + lp["ssm_conv_b"])
    xs, bs, cs = xbc[..., :BRANCH], xbc[..., BRANCH:BRANCH + 128], xbc[..., BRANCH + 128:]
    dt = jax.nn.softplus(_blk(p, "ssm_dt", 16).reshape(b, t, 2, SSM_HEADS) + lp["ssm_dt_bias"])
    la = dt * -jnp.exp(lp["ssm_a_log"])
    y = (ssd_scan(xs, bs, cs, la[:, :, 0], dt[:, :, 0], n_ctx=n_ctx, reverse=False)
         + ssd_scan(xs, bs, cs, la[:, :, 1], dt[:, :, 1], n_ctx=n_ctx, reverse=True))
    y = (y + jnp.repeat(lp["ssm_d"], SSM_P) * xs) * jax.nn.silu(z)
    return _group_rms(y, lp["ssm_norm"], BRANCH // SSM_GROUPS)


def _lowrank(a, w_pair):
    b, t, k = a.shape
    r, c = w_pair.shape[1:]
    w = jnp.zeros((k, 2 * c), F32).at[:r, :c].set(w_pair[0]).at[r:2 * r, c:].set(w_pair[1])
    return pmatmul(a.reshape(b * t, k), w, tm=1024, tn=2 * c, precise=True).reshape(b, t, 2 * c)


def gla_mixer(p, lp, n_ctx):
    q = _blk(p, "gla_q", 256) * GLA_DK ** -0.5
    k = _blk(p, "gla_k", 256)
    v = _blk(p, "gla_v", BRANCH)
    r = _blk(p, "gla_r", BRANCH)
    logit = _lowrank(_blk(p, "gla_glr", LANES), lp["gla_w2"]) + lp["gla_b"].reshape(-1)
    la = jax.nn.log_sigmoid(logit) / GLA_TAU
    y = (gla_scan(q, k, v, la[..., :256], n_ctx=n_ctx, reverse=False)
         + gla_scan(q, k, v, la[..., 256:], n_ctx=n_ctx, reverse=True))
    return _group_rms(y, jnp.tile(lp["gla_norm"], GLA_HEADS), GLA_DV) * jax.nn.silu(r)


def rwkv_mixer(p, lp, n_ctx):
    b, t, _ = p.shape
    o = _OFF["rwkv_rkv"]
    pr = p[..., o:o + 1920]
    pr = pr + lp["rwkv_mu"] * (0.5 * (_prev_tok(pr, n_ctx) + _next_tok(pr, n_ctx)) - pr)
    r, k, v = pr[..., :512], pr[..., 512:1024], pr[..., 1024:1536]
    wlr, alr, glr = pr[..., 1536:1664], pr[..., 1664:1792], pr[..., 1792:1920]
    w_raw = _lowrank(jnp.tanh(wlr), lp["rwkv_w2"]) + lp["rwkv_w0"].reshape(-1)
    lw = -jnp.exp(-jax.nn.softplus(-w_raw) - 0.5)
    a = jax.nn.sigmoid(_lowrank(alr, lp["rwkv_a2"]) + lp["rwkv_a0"].reshape(-1))
    g = pmatmul(jax.nn.sigmoid(glr).reshape(b * t, 128), lp["rwkv_g2"], tm=1024, tn=BRANCH,
                precise=True).reshape(b, t, BRANCH)
    kk = k * lp["rwkv_k_k"]
    kkh = kk.reshape(b, t, RWKV_HEADS, RWKV_N)
    kk = (kkh * lax.rsqrt(jnp.sum(kkh * kkh, -1, keepdims=True) + EPS)).reshape(b, t, BRANCH)
    y = 0.0
    bonus = 0.0
    r_k = lp["rwkv_r_k"].reshape(-1)
    for d in range(2):
        ad = a[..., d * BRANCH:(d + 1) * BRANCH]
        kd = k * (1 + (ad - 1) * lp["rwkv_k_a"])
        y = y + rwkv_scan(r, kd, v, kk, ad, lw[..., d * BRANCH:(d + 1) * BRANCH], n_ctx=n_ctx,
                          reverse=bool(d))
        bonus = bonus + jnp.sum((r * kd * r_k).reshape(b, t, RWKV_HEADS, RWKV_N), -1, keepdims=True)
    bonus = (bonus * v.reshape(b, t, RWKV_HEADS, RWKV_N)).reshape(b, t, BRANCH)
    yh = y.reshape(b, t, RWKV_HEADS, RWKV_N)
    mu = jnp.mean(yh, -1, keepdims=True)
    var = jnp.mean(jnp.square(yh - mu), -1, keepdims=True)
    yn = ((yh - mu) * lax.rsqrt(var + RWKV_LN_EPS)).reshape(b, t, BRANCH)
    return (yn * lp["rwkv_ln_w"] + lp["rwkv_ln_b"] + bonus) * g


def gdn_mixer(p, lp, n_ctx):
    b, t, _ = p.shape
    qkv = jax.nn.silu(_conv3(_blk(p, "gdn_qkv", 1536), lp["gdn_conv_w"], n_ctx))
    z = _blk(p, "gdn_z", BRANCH)
    ab = _blk(p, "gdn_ab", 16)

    def l2n(x):
        xh = x.reshape(b, t, GDN_HEADS, GDN_N)
        return (xh * lax.rsqrt(jnp.sum(xh * xh, -1, keepdims=True) + EPS)).reshape(b, t, BRANCH)

    q = l2n(qkv[..., :512]) * GDN_N ** -0.5
    k = l2n(qkv[..., 512:1024])
    v = qkv[..., 1024:]
    a_raw = ab[..., :8].reshape(b, t, 2, GDN_HEADS)
    b_raw = ab[..., 8:].reshape(b, t, 2, GDN_HEADS)
    g = -jnp.exp(lp["gdn_a_log"]) * jax.nn.softplus(a_raw + lp["gdn_dt_bias"])
    beta = jax.nn.sigmoid(b_raw)
    y = (gdn_scan(q, k, v, g[:, :, 0], beta[:, :, 0], n_ctx=n_ctx, reverse=False)
         + gdn_scan(q, k, v, g[:, :, 1], beta[:, :, 1], n_ctx=n_ctx, reverse=True))
    return _group_rms(y, jnp.tile(lp["gdn_norm"], GDN_HEADS), GDN_N) * jax.nn.silu(z)


def _to_scan_order(t):
    b, n, d = t.shape
    return t.reshape(b, n // GRID_W, GRID_W, d).transpose(0, 2, 1, 3).reshape(b, n, d)


def _from_scan_order(t):
    b, n, d = t.shape
    return t.reshape(b, GRID_W, n // GRID_W, d).transpose(0, 2, 1, 3).reshape(b, n, d)


def kernel(x, c, ctx, c_ctx, ada_w, ada_b, norm_mix, norm_ffn, w_in, w_gate, w_branch, w_out, ssm_conv_w, ssm_conv_b, ssm_a_log, ssm_dt_bias, ssm_d, ssm_norm, gla_w2, gla_b, gla_norm, rwkv_mu, rwkv_w0, rwkv_w2, rwkv_a0, rwkv_a2, rwkv_g2, rwkv_k_k, rwkv_k_a, rwkv_r_k, rwkv_ln_w, rwkv_ln_b, gdn_conv_w, gdn_a_log, gdn_dt_bias, gdn_norm, router_w, router_b, moe_w_gate, moe_w_up, moe_w_down, final_norm):
    bsz, seq, d = x.shape
    n_ctx = ctx.shape[1]
    t_all = n_ctx + seq
    m_all = bsz * t_all
    tm = 256

    cond = jnp.concatenate([jax.nn.silu(c), jax.nn.silu(c_ctx)[None]], 0)
    cond = jnp.pad(cond, ((0, 8 - cond.shape[0]), (0, 0)))
    mods = []
    for l in range(DEPTH):
        mod = pmatmul(cond, ada_w[l], tm=8, tn=1024, precise=True) + ada_b[l]
        lat = mod[:bsz].reshape(bsz, 6, d)
        cx = jnp.broadcast_to(mod[bsz].reshape(1, 6, d), (bsz, 6, d))
        mods.append(jnp.stack([cx, lat], axis=1))

    src = jnp.asarray(np.maximum(_SRC_COLS, 0))
    valid = jnp.asarray(_SRC_COLS >= 0)

    x_all = jnp.concatenate([ctx, x], axis=1)
    scan_order = False
    for l in range(DEPTH):
        if (l % 2 == 1) != scan_order:
            reorder = _from_scan_order if scan_order else _to_scan_order
            x_all = jnp.concatenate([x_all[:, :n_ctx], reorder(x_all[:, n_ctx:])], axis=1)
            scan_order = not scan_order
        lp = dict(ssm_conv_w=ssm_conv_w[l], ssm_conv_b=ssm_conv_b[l], ssm_a_log=ssm_a_log[l],
                  ssm_dt_bias=ssm_dt_bias[l], ssm_d=ssm_d[l], ssm_norm=ssm_norm[l],
                  gla_w2=gla_w2[l], gla_b=gla_b[l], gla_norm=gla_norm[l],
                  rwkv_mu=rwkv_mu[l], rwkv_w0=rwkv_w0[l], rwkv_w2=rwkv_w2[l], rwkv_a0=rwkv_a0[l],
                  rwkv_a2=rwkv_a2[l], rwkv_g2=rwkv_g2[l], rwkv_k_k=rwkv_k_k[l], rwkv_k_a=rwkv_k_a[l],
                  rwkv_r_k=rwkv_r_k[l], rwkv_ln_w=rwkv_ln_w[l], rwkv_ln_b=rwkv_ln_b[l],
                  gdn_conv_w=gdn_conv_w[l], gdn_a_log=gdn_a_log[l], gdn_dt_bias=gdn_dt_bias[l],
                  gdn_norm=gdn_norm[l])
        mod = mods[l]
        msel = lambda i: mod[:, :, i][:, :, None, :]

        h = norm_modulate(x_all, norm_mix[l], msel(0), msel(1), n_ctx=n_ctx, tm=tm)
        h2d = h.reshape(m_all, d)
        w_in_p = jnp.where(valid[None, :], w_in[l][:, src], 0.0).astype(BF16)
        p = pmatmul(h2d, w_in_p, tm=1024, tn=1024).reshape(bsz, t_all, N_PACKED)
        wg_cat = jnp.concatenate([w_gate[l, i] for i in range(4)], axis=1).astype(BF16)
        gates = pmatmul(h2d, wg_cat, tm=1024, tn=1024, act="sigmoid", out_dtype=BF16)
        gates = gates.reshape(bsz, t_all, 4 * d)

        y_cat = jnp.concatenate([ssm_mixer(p, lp, n_ctx), gla_mixer(p, lp, n_ctx),
                                 rwkv_mixer(p, lp, n_ctx), gdn_mixer(p, lp, n_ctx)], axis=-1)
        x_all = merge_residual(y_cat, gates, w_branch[l].astype(BF16), w_out[l].astype(BF16),
                               x_all, msel(2), n_ctx=n_ctx, tm=tm)

        h = norm_modulate(x_all, norm_ffn[l], msel(3), msel(4), n_ctx=n_ctx, tm=tm, out_dtype=F32)
        gate = moe_route(h.reshape(m_all, d), router_w, router_b, tm=1024)
        gate = gate.T.reshape(bsz, t_all, N_EXPERTS)
        x_all = moe_experts(h, gate, moe_w_gate[l].astype(BF16), moe_w_up[l].astype(BF16),
                            moe_w_down[l].astype(BF16), x_all, msel(5), n_ctx=n_ctx, tm=tm)

    lat = x_all[:, n_ctx:]
    if scan_order:
        lat = _from_scan_order(lat)
    return final_rms_norm(lat.reshape(bsz * seq, d), final_norm, tm=1024).reshape(bsz, seq, d)
```

```python
import functools

import numpy as np
import jax
import jax.numpy as jnp
from jax import lax
from jax.experimental import pallas as pl
from jax.experimental.pallas import tpu as pltpu

F32 = jnp.float32
BF16 = jnp.bfloat16
HI = lax.Precision.HIGHEST

D_MODEL = 1024
DEPTH = 2
GRID_W = 64
CHUNK = 64
EPS = 1e-6
BRANCH = D_MODEL // 2
SSM_HEADS, SSM_P, SSM_GROUPS, SSM_N = 8, 64, 2, 64
GLA_HEADS, GLA_DK, GLA_DV, GLA_RANK, GLA_TAU = 4, 64, 128, 16, 16.0
RWKV_HEADS, RWKV_N, RWKV_LN_EPS = 8, 64, 64e-5
GDN_HEADS, GDN_N = 4, 128
N_EXPERTS, N_GROUPS, EXPERTS_PER_GROUP = 16, 4, 4
EXPERT_FF = D_MODEL // 2
LANES = 128

_IN_BLOCKS = (
    ("ssm_z", 512), ("ssm_xbc", 768), ("ssm_dt", 16),
    ("gla_q", 256), ("gla_k", 256), ("gla_v", 512), ("gla_r", 512), ("gla_glr", 32),
    ("rwkv_rkv", 1536), ("rwkv_wlr", 128), ("rwkv_alr", 128), ("rwkv_glr", 128),
    ("gdn_qkv", 1536), ("gdn_z", 512), ("gdn_ab", 16),
)


def _packed_layout():
    src, offs, s, o = [], {}, 0, 0
    for name, w in _IN_BLOCKS:
        wp = -(-w // LANES) * LANES
        offs[name] = o
        src += list(range(s, s + w)) + [-1] * (wp - w)
        s += w
        o += wp
    return np.asarray(src, np.int32), offs, o


_SRC_COLS, _OFF, N_PACKED = _packed_layout()
VMEM_LIMIT = 48 * 1024 * 1024


def _cparams(*sem):
    return pltpu.CompilerParams(dimension_semantics=sem, vmem_limit_bytes=VMEM_LIMIT)


def _dot(a, b):
    return jnp.dot(a.astype(BF16), b.astype(BF16), preferred_element_type=F32)


def _dot_nt(a, b):
    return lax.dot_general(a.astype(BF16), b.astype(BF16), (((1,), (1,)), ((), ())),
                           preferred_element_type=F32)


def _dot_tn(a, b):
    return lax.dot_general(a.astype(BF16), b.astype(BF16), (((0,), (0,)), ((), ())),
                           preferred_element_type=F32)


def _dot_hi(a, b):
    return jnp.dot(a, b, precision=HI, preferred_element_type=F32)


def _mm_kernel(a_ref, w_ref, o_ref, *, act, precise):
    if precise:
        r = _dot_hi(a_ref[...].astype(F32), w_ref[...].astype(F32))
    else:
        r = _dot(a_ref[...], w_ref[...])
    if act == "sigmoid":
        r = jax.nn.sigmoid(r)
    o_ref[...] = r.astype(o_ref.dtype)


def _pick_tile(m, pref):
    t = pref
    while m % t:
        t //= 2
    return t


def pmatmul(a, w, *, tm, tn, act=None, precise=False, out_dtype=F32):
    m, k = a.shape
    n = w.shape[1]
    tm = _pick_tile(m, tm)
    assert tm % 8 == 0 and n % tn == 0, (m, tm, n, tn)
    return pl.pallas_call(
        functools.partial(_mm_kernel, act=act, precise=precise),
        grid=(n // tn, m // tm),
        in_specs=[pl.BlockSpec((tm, k), lambda j, i: (i, 0)),
                  pl.BlockSpec((k, tn), lambda j, i: (0, j))],
        out_specs=pl.BlockSpec((tm, tn), lambda j, i: (i, j)),
        out_shape=jax.ShapeDtypeStruct((m, n), out_dtype),
        compiler_params=_cparams("parallel", "parallel"),
    )(a, w)


def _norm_mod_kernel(x_ref, w_ref, shift_ref, scale_ref, o_ref):
    x = x_ref[0]
    y = x * lax.rsqrt(jnp.mean(x * x, axis=-1, keepdims=True) + EPS) * w_ref[...]
    o_ref[0] = (y * (1.0 + scale_ref[0, 0]) + shift_ref[0, 0]).astype(o_ref.dtype)


def norm_modulate(x_all, w, shift, scale, *, n_ctx, tm, out_dtype=BF16):
    b, t, d = x_all.shape
    assert n_ctx % tm == 0 and t % tm == 0
    nct = n_ctx // tm
    sel = lambda bi, i: (bi, jnp.where(i < nct, 0, 1), 0, 0)
    return pl.pallas_call(
        _norm_mod_kernel,
        grid=(b, t // tm),
        in_specs=[pl.BlockSpec((1, tm, d), lambda bi, i: (bi, i, 0)),
                  pl.BlockSpec((1, d), lambda bi, i: (0, 0)),
                  pl.BlockSpec((1, 1, 1, d), sel),
                  pl.BlockSpec((1, 1, 1, d), sel)],
        out_specs=pl.BlockSpec((1, tm, d), lambda bi, i: (bi, i, 0)),
        out_shape=jax.ShapeDtypeStruct((b, t, d), out_dtype),
        compiler_params=_cparams("parallel", "parallel"),
    )(x_all, w.reshape(1, d), shift, scale)


def _chunk_masks(reverse):
    row = lax.broadcasted_iota(jnp.int32, (CHUNK, CHUNK), 0)
    col = lax.broadcasted_iota(jnp.int32, (CHUNK, CHUNK), 1)
    if reverse:
        return col >= row, col > row, row >= col
    return col <= row, col < row, row <= col


def _chunk_order(i, n_ctx_chunks, n_chunks, reverse):
    if not reverse:
        return i
    return jnp.where(i < n_ctx_chunks, n_ctx_chunks - 1 - i, n_chunks - 1 - (i - n_ctx_chunks))


def _dot_x3(a, b):
    ah = a.astype(BF16)
    al = (a - ah.astype(F32)).astype(BF16)
    bh = b.astype(BF16)
    bl = (b - bh.astype(F32)).astype(BF16)
    f = lambda u, v: jnp.dot(u, v, preferred_element_type=F32)
    return f(ah, bh) + (f(ah, bl) + f(al, bh))


def _unit_tri_solve(mats, rhs):
    n = range(len(mats))
    x = [rhs[h] - _dot_x3(mats[h], rhs[h]) for h in n]
    p = mats
    for _ in range(int(np.log2(CHUNK)) - 1):
        p = [_dot_x3(p[h], p[h]) for h in n]
        x = [x[h] + _dot_x3(p[h], x[h]) for h in n]
    return x


def _ssd_kernel(x_ref, b_ref, c_ref, la_ref, lat_ref, dt_ref, dtt_ref, o_ref, s_ref, *, reverse):
    @pl.when(pl.program_id(1) == 0)
    def _():
        s_ref[...] = jnp.zeros_like(s_ref)

    incl, _, incl_t = _chunk_masks(reverse)
    last = 0 if reverse else CHUNK - 1
    g = _dot_hi(incl.astype(F32), la_ref[0])
    gt = _dot_hi(lat_ref[0, 0], incl_t.astype(F32))
    dt = dt_ref[0]
    dtt = dtt_ref[0, 0]
    heads = range(SSM_HEADS)
    rep = SSM_HEADS // SSM_GROUPS
    bm = [b_ref[0, :, grp * SSM_N:(grp + 1) * SSM_N] for grp in range(SSM_GROUPS)]
    cm = [c_ref[0, :, grp * SSM_N:(grp + 1) * SSM_N] for grp in range(SSM_GROUPS)]
    cb = [_dot_nt(cm[grp], bm[grp]) for grp in range(SSM_GROUPS)]
    xh = [x_ref[0, :, h * SSM_P:(h + 1) * SSM_P] for h in heads]
    s = [s_ref[h] for h in heads]
    gh = [g[:, h:h + 1] for h in heads]
    gl = [g[last:last + 1, h:h + 1] for h in heads]
    scores = [cb[h // rep] * jnp.exp(jnp.where(incl, gh[h] - gt[h:h + 1, :], -jnp.inf)) * dtt[h:h + 1, :]
              for h in heads]
    intra = [_dot(scores[h], xh[h]) for h in heads]
    inter = [_dot(cm[h // rep] * jnp.exp(gh[h]), s[h]) for h in heads]
    upd = [_dot_tn(bm[h // rep] * (dt[:, h:h + 1] * jnp.exp(gl[h] - gh[h])), xh[h]) for h in heads]
    for h in heads:
        o_ref[0, :, h * SSM_P:(h + 1) * SSM_P] = intra[h] + inter[h]
        s_ref[h] = s[h] * jnp.exp(gl[h]) + upd[h]


def ssd_scan(x, bm, cm, la, dt, *, n_ctx, reverse):
    b, t, _ = x.shape
    nc, ncc = t // CHUNK, n_ctx // CHUNK
    lat = la.reshape(b, nc, CHUNK, SSM_HEADS).transpose(0, 1, 3, 2)
    dtt = dt.reshape(b, nc, CHUNK, SSM_HEADS).transpose(0, 1, 3, 2)
    tok = lambda bi, i: (bi, _chunk_order(i, ncc, nc, reverse), 0)
    tokt = lambda bi, i: (bi, _chunk_order(i, ncc, nc, reverse), 0, 0)
    specs = [pl.BlockSpec((1, CHUNK, BRANCH), tok),
             pl.BlockSpec((1, CHUNK, SSM_GROUPS * SSM_N), tok),
             pl.BlockSpec((1, CHUNK, SSM_GROUPS * SSM_N), tok),
             pl.BlockSpec((1, CHUNK, SSM_HEADS), tok),
             pl.BlockSpec((1, 1, SSM_HEADS, CHUNK), tokt),
             pl.BlockSpec((1, CHUNK, SSM_HEADS), tok),
             pl.BlockSpec((1, 1, SSM_HEADS, CHUNK), tokt)]
    return pl.pallas_call(
        functools.partial(_ssd_kernel, reverse=reverse),
        grid=(b, nc), in_specs=specs,
        out_specs=pl.BlockSpec((1, CHUNK, BRANCH), tok),
        out_shape=jax.ShapeDtypeStruct((b, t, BRANCH), F32),
        scratch_shapes=[pltpu.VMEM((SSM_HEADS, SSM_N, SSM_P), F32)],
        compiler_params=_cparams("parallel", "arbitrary"),
    )(x, bm, cm, la, lat, dt, dtt)


def _gla_kernel(q_ref, k_ref, v_ref, la_ref, o_ref, s_ref, *, reverse):
    @pl.when(pl.program_id(1) == 0)
    def _():
        s_ref[...] = jnp.zeros_like(s_ref)

    incl, _, _ = _chunk_masks(reverse)
    last = 0 if reverse else CHUNK - 1
    g_all = _dot_hi(incl.astype(F32), la_ref[0])
    heads = range(GLA_HEADS)
    ks = [slice(h * GLA_DK, (h + 1) * GLA_DK) for h in heads]
    vs = [slice(h * GLA_DV, (h + 1) * GLA_DV) for h in heads]
    g = [g_all[:, ks[h]] for h in heads]
    gl = [g[h][last:last + 1, :] for h in heads]
    k = [k_ref[0, :, ks[h]] for h in heads]
    v = [v_ref[0, :, vs[h]] for h in heads]
    qg = [q_ref[0, :, ks[h]] * jnp.exp(g[h]) for h in heads]
    st = [s_ref[h] for h in heads]
    scores = [jnp.where(incl, _dot_nt(qg[h], k[h] * jnp.exp(-g[h])), 0.0) for h in heads]
    intra = [_dot(scores[h], v[h]) for h in heads]
    inter = [_dot_nt(qg[h], st[h]) for h in heads]
    upd = [_dot_tn(v[h], k[h] * jnp.exp(gl[h] - g[h])) for h in heads]
    for h in heads:
        o_ref[0, :, vs[h]] = intra[h] + inter[h]
        s_ref[h] = st[h] * jnp.exp(gl[h]) + upd[h]


def gla_scan(q, k, v, la, *, n_ctx, reverse):
    b, t, _ = q.shape
    nc, ncc = t // CHUNK, n_ctx // CHUNK
    tok = lambda bi, i: (bi, _chunk_order(i, ncc, nc, reverse), 0)
    kw = GLA_HEADS * GLA_DK
    specs = [pl.BlockSpec((1, CHUNK, kw), tok), pl.BlockSpec((1, CHUNK, kw), tok),
             pl.BlockSpec((1, CHUNK, BRANCH), tok), pl.BlockSpec((1, CHUNK, kw), tok)]
    return pl.pallas_call(
        functools.partial(_gla_kernel, reverse=reverse),
        grid=(b, nc), in_specs=specs,
        out_specs=pl.BlockSpec((1, CHUNK, BRANCH), tok),
        out_shape=jax.ShapeDtypeStruct((b, t, BRANCH), F32),
        scratch_shapes=[pltpu.VMEM((GLA_HEADS, GLA_DV, GLA_DK), F32)],
        compiler_params=_cparams("parallel", "arbitrary"),
    )(q, k, v, la)


def _rwkv_kernel(r_ref, k_ref, v_ref, kk_ref, a_ref, lw_ref, o_ref, s_ref, *, reverse):
    @pl.when(pl.program_id(1) == 0)
    def _():
        s_ref[...] = jnp.zeros_like(s_ref)

    incl, strict, _ = _chunk_masks(reverse)
    last = 0 if reverse else CHUNK - 1
    lw_all = lw_ref[0]
    g_all = _dot_hi(incl.astype(F32), lw_all)
    heads = range(RWKV_HEADS)
    hs = [slice(h * RWKV_N, (h + 1) * RWKV_N) for h in heads]
    g = [g_all[:, hs[h]] for h in heads]
    gl = [g[h][last:last + 1, :] for h in heads]
    eneg = [jnp.exp(-g[h]) for h in heads]
    edec = [jnp.exp(gl[h] - g[h]) for h in heads]
    kk = [kk_ref[0, :, hs[h]] for h in heads]
    bvec = [kk[h] * a_ref[0, :, hs[h]] for h in heads]
    k = [k_ref[0, :, hs[h]] for h in heads]
    v = [v_ref[0, :, hs[h]] for h in heads]
    kkg = [kk[h] * jnp.exp(g[h] - lw_all[:, hs[h]]) for h in heads]
    rg = [r_ref[0, :, hs[h]] * jnp.exp(g[h]) for h in heads]
    bh = [bvec[h] * eneg[h] for h in heads]
    kh = [k[h] * eneg[h] for h in heads]
    s = [s_ref[h] for h in heads]
    a_mat = [jnp.where(strict, _dot_nt(kkg[h], bh[h]), 0.0) for h in heads]
    b_mat = [jnp.where(strict, _dot_nt(kkg[h], kh[h]), 0.0) for h in heads]
    aqb = [jnp.where(incl, _dot_nt(rg[h], bh[h]), 0.0) for h in heads]
    aqk = [jnp.where(incl, _dot_nt(rg[h], kh[h]), 0.0) for h in heads]
    rhs = [_dot(b_mat[h], v[h]) + _dot_nt(kkg[h], s[h]) for h in heads]
    o_part = [_dot_nt(rg[h], s[h]) + _dot(aqk[h], v[h]) for h in heads]
    u = [-x for x in _unit_tri_solve(a_mat, rhs)]
    for h in heads:
        o_ref[0, :, hs[h]] = o_part[h] + _dot(aqb[h], u[h])
    for h in heads:
        s_ref[h] = (s[h] * jnp.exp(gl[h]) + _dot_tn(u[h], bvec[h] * edec[h])
                    + _dot_tn(v[h], k[h] * edec[h]))


def rwkv_scan(r, k, v, kk, a, lw, *, n_ctx, reverse):
    b, t, _ = r.shape
    nc, ncc = t // CHUNK, n_ctx // CHUNK
    tok = lambda bi, i: (bi, _chunk_order(i, ncc, nc, reverse), 0)
    spec = pl.BlockSpec((1, CHUNK, BRANCH), tok)
    return pl.pallas_call(
        functools.partial(_rwkv_kernel, reverse=reverse),
        grid=(b, nc), in_specs=[spec] * 6,
        out_specs=spec,
        out_shape=jax.ShapeDtypeStruct((b, t, BRANCH), F32),
        scratch_shapes=[pltpu.VMEM((RWKV_HEADS, RWKV_N, RWKV_N), F32)],
        compiler_params=_cparams("parallel", "arbitrary"),
    )(r, k, v, kk, a, lw)


def _gdn_kernel(q_ref, k_ref, v_ref, gl_ref, glt_ref, beta_ref, o_ref, s_ref, *, reverse):
    @pl.when(pl.program_id(1) == 0)
    def _():
        s_ref[...] = jnp.zeros_like(s_ref)

    incl, strict, incl_t = _chunk_masks(reverse)
    last = 0 if reverse else CHUNK - 1
    g_all = _dot_hi(incl.astype(F32), gl_ref[0])
    gt_all = _dot_hi(glt_ref[0, 0], incl_t.astype(F32))
    heads = range(GDN_HEADS)
    hs = [slice(h * GDN_N, (h + 1) * GDN_N) for h in heads]
    g = [g_all[:, h:h + 1] for h in heads]
    gl = [g[h][last:last + 1, :] for h in heads]
    beta = [beta_ref[0, :, h:h + 1] for h in heads]
    q = [q_ref[0, :, hs[h]] for h in heads]
    k = [k_ref[0, :, hs[h]] for h in heads]
    v = [v_ref[0, :, hs[h]] for h in heads]
    s = [s_ref[h] for h in heads]
    decay = [jnp.exp(jnp.where(incl, g[h] - gt_all[h:h + 1, :], -jnp.inf)) for h in heads]
    lower = [jnp.where(strict, _dot_nt(k[h], k[h]) * decay[h] * beta[h], 0.0) for h in heads]
    attn = [_dot_nt(q[h], k[h]) * decay[h] for h in heads]
    o_part = [_dot(q[h] * jnp.exp(g[h]), s[h]) for h in heads]
    rhs = [jnp.concatenate([v[h] * beta[h], k[h] * (beta[h] * jnp.exp(g[h]))], axis=1) for h in heads]
    sol = _unit_tri_solve(lower, rhs)
    v_new = [sol[h][:, :GDN_N] - _dot(sol[h][:, GDN_N:], s[h]) for h in heads]
    for h in heads:
        o_ref[0, :, hs[h]] = o_part[h] + _dot(attn[h], v_new[h])
        s_ref[h] = s[h] * jnp.exp(gl[h]) + _dot_tn(k[h] * jnp.exp(gl[h] - g[h]), v_new[h])


def gdn_scan(q, k, v, gl, beta, *, n_ctx, reverse):
    b, t, _ = q.shape
    nc, ncc = t // CHUNK, n_ctx // CHUNK
    glt = gl.reshape(b, nc, CHUNK, GDN_HEADS).transpose(0, 1, 3, 2)
    tok = lambda bi, i: (bi, _chunk_order(i, ncc, nc, reverse), 0)
    tokt = lambda bi, i: (bi, _chunk_order(i, ncc, nc, reverse), 0, 0)
    big = pl.BlockSpec((1, CHUNK, BRANCH), tok)
    small = pl.BlockSpec((1, CHUNK, GDN_HEADS), tok)
    return pl.pallas_call(
        functools.partial(_gdn_kernel, reverse=reverse),
        grid=(b, nc),
        in_specs=[big, big, big, small, pl.BlockSpec((1, 1, GDN_HEADS, CHUNK), tokt), small],
        out_specs=big,
        out_shape=jax.ShapeDtypeStruct((b, t, BRANCH), F32),
        scratch_shapes=[pltpu.VMEM((GDN_HEADS, GDN_N, GDN_N), F32)],
        compiler_params=_cparams("parallel", "arbitrary"),
    )(q, k, v, gl, glt, beta)


def _merge_kernel(y_ref, g_ref, wb_ref, wo_ref, x_ref, m_ref, o_ref):
    acc = None
    for i in range(4):
        term = g_ref[0, :, i * D_MODEL:(i + 1) * D_MODEL].astype(F32) * _dot(
            y_ref[0, :, i * BRANCH:(i + 1) * BRANCH], wb_ref[i])
        acc = term if acc is None else acc + term
    o_ref[0] = x_ref[0] + m_ref[0, 0] * _dot(acc, wo_ref[...])


def merge_residual(y_cat, gates, w_branch, w_out, x_all, gate_mod, *, n_ctx, tm):
    b, t, d = x_all.shape
    nct = n_ctx // tm
    sel = lambda bi, i: (bi, jnp.where(i < nct, 0, 1), 0, 0)
    tok = lambda bi, i: (bi, i, 0)
    return pl.pallas_call(
        _merge_kernel,
        grid=(b, t // tm),
        in_specs=[pl.BlockSpec((1, tm, 4 * BRANCH), tok),
                  pl.BlockSpec((1, tm, 4 * d), tok),
                  pl.BlockSpec((4, BRANCH, d), lambda bi, i: (0, 0, 0)),
                  pl.BlockSpec((d, d), lambda bi, i: (0, 0)),
                  pl.BlockSpec((1, tm, d), tok),
                  pl.BlockSpec((1, 1, 1, d), sel)],
        out_specs=pl.BlockSpec((1, tm, d), tok),
        out_shape=jax.ShapeDtypeStruct((b, t, d), F32),
        compiler_params=_cparams("parallel", "parallel"),
    )(y_cat, gates, w_branch, w_out, x_all, gate_mod)


def _route_kernel(h_ref, rw_ref, rb_ref, o_ref):
    logits = lax.dot_general(rw_ref[...], h_ref[...].astype(F32), (((1,), (1,)), ((), ())),
                             precision=HI, preferred_element_type=F32)
    scores = jax.nn.sigmoid(logits)
    sel = scores + rb_ref[...]
    rows = [sel[e:e + 1, :] for e in range(N_EXPERTS)]
    sc = [scores[e:e + 1, :] for e in range(N_EXPERTS)]
    neg = jnp.full_like(rows[0], -jnp.inf)

    def top2(vals):
        v1, i1 = vals[0], jnp.zeros(vals[0].shape, jnp.int32)
        for j in range(1, len(vals)):
            better = vals[j] > v1
            v1 = jnp.where(better, vals[j], v1)
            i1 = jnp.where(better, j, i1)
        v2, i2 = None, None
        for j in range(len(vals)):
            if v2 is None:
                v2 = jnp.where(i1 == 0, vals[1], vals[0])
                i2 = jnp.where(i1 == 0, 1, 0)
                continue
            better = (vals[j] > v2) & (i1 != j)
            v2 = jnp.where(better, vals[j], v2)
            i2 = jnp.where(better, j, i2)
        return v1, i1, v2, i2

    gsum = []
    for grp in range(N_GROUPS):
        v1, _, v2, _ = top2(rows[grp * EXPERTS_PER_GROUP:(grp + 1) * EXPERTS_PER_GROUP])
        gsum.append(v1 + v2)
    best, gidx = gsum[0], jnp.zeros(gsum[0].shape, jnp.int32)
    for grp in range(1, N_GROUPS):
        better = gsum[grp] > best
        best = jnp.where(better, gsum[grp], best)
        gidx = jnp.where(better, grp, gidx)
    chosen = [rows[0]] * EXPERTS_PER_GROUP
    chosen_sc = [sc[0]] * EXPERTS_PER_GROUP
    for j in range(EXPERTS_PER_GROUP):
        cj, sj = rows[j], sc[j]
        for grp in range(1, N_GROUPS):
            cj = jnp.where(gidx == grp, rows[grp * EXPERTS_PER_GROUP + j], cj)
            sj = jnp.where(gidx == grp, sc[grp * EXPERTS_PER_GROUP + j], sj)
        chosen[j], chosen_sc[j] = cj, sj
    _, i1, _, i2 = top2(chosen)
    w1, w2 = jnp.zeros_like(best), jnp.zeros_like(best)
    for j in range(EXPERTS_PER_GROUP):
        w1 = jnp.where(i1 == j, chosen_sc[j], w1)
        w2 = jnp.where(i2 == j, chosen_sc[j], w2)
    tot = w1 + w2
    w1, w2 = w1 / tot, w2 / tot
    e1 = gidx * EXPERTS_PER_GROUP + i1
    e2 = gidx * EXPERTS_PER_GROUP + i2
    eid = lax.broadcasted_iota(jnp.int32, scores.shape, 0)
    o_ref[...] = jnp.where(eid == e1, w1, 0.0) + jnp.where(eid == e2, w2, 0.0)


def moe_route(h, router_w, router_b, *, tm):
    m, d = h.shape
    return pl.pallas_call(
        _route_kernel,
        grid=(m // tm,),
        in_specs=[pl.BlockSpec((tm, d), lambda i: (i, 0)),
                  pl.BlockSpec((N_EXPERTS, d), lambda i: (0, 0)),
                  pl.BlockSpec((N_EXPERTS, 1), lambda i: (0, 0))],
        out_specs=pl.BlockSpec((N_EXPERTS, tm), lambda i: (0, i)),
        out_shape=jax.ShapeDtypeStruct((N_EXPERTS, m), F32),
        compiler_params=_cparams("parallel"),
    )(h, router_w.T, router_b.reshape(N_EXPERTS, 1))


def _expert_kernel(h_ref, g_ref, wg_ref, wu_ref, wd_ref, x_ref, m_ref, o_ref, acc_ref):
    e = pl.program_id(2)

    @pl.when(e == 0)
    def _():
        acc_ref[...] = jnp.zeros_like(acc_ref)

    h = h_ref[0]
    gates = g_ref[0]
    lane = lax.broadcasted_iota(jnp.int32, gates.shape, 1)
    ge = jnp.sum(jnp.where(lane == e, gates, 0.0), axis=1, keepdims=True)
    hid = jax.nn.silu(_dot(h, wg_ref[0])) * _dot(h, wu_ref[0])
    acc_ref[...] += ge * _dot(hid, wd_ref[0])

    @pl.when(e == N_EXPERTS - 1)
    def _():
        o_ref[0] = x_ref[0] + m_ref[0, 0] * acc_ref[...]


def moe_experts(h, gates, wg, wu, wd, x_all, gate_mod, *, n_ctx, tm):
    b, t, d = x_all.shape
    nct = n_ctx // tm
    sel = lambda bi, i, e: (bi, jnp.where(i < nct, 0, 1), 0, 0)
    tok = lambda bi, i, e: (bi, i, 0)
    return pl.pallas_call(
        _expert_kernel,
        grid=(b, t // tm, N_EXPERTS),
        in_specs=[pl.BlockSpec((1, tm, d), tok),
                  pl.BlockSpec((1, tm, N_EXPERTS), tok),
                  pl.BlockSpec((1, d, EXPERT_FF), lambda bi, i, e: (e, 0, 0)),
                  pl.BlockSpec((1, d, EXPERT_FF), lambda bi, i, e: (e, 0, 0)),
                  pl.BlockSpec((1, EXPERT_FF, d), lambda bi, i, e: (e, 0, 0)),
                  pl.BlockSpec((1, tm, d), tok),
                  pl.BlockSpec((1, 1, 1, d), sel)],
        out_specs=pl.BlockSpec((1, tm, d), tok),
        out_shape=jax.ShapeDtypeStruct((b, t, d), F32),
        scratch_shapes=[pltpu.VMEM((tm, d), F32)],
        compiler_params=_cparams("parallel", "parallel", "arbitrary"),
    )(h, gates, wg, wu, wd, x_all, gate_mod)


def _final_norm_kernel(x_ref, w_ref, o_ref):
    x = x_ref[...]
    o_ref[...] = x * lax.rsqrt(jnp.mean(x * x, axis=-1, keepdims=True) + EPS) * w_ref[...]


def final_rms_norm(x, w, *, tm):
    m, d = x.shape
    return pl.pallas_call(
        _final_norm_kernel,
        grid=(m // tm,),
        in_specs=[pl.BlockSpec((tm, d), lambda i: (i, 0)), pl.BlockSpec((1, d), lambda i: (0, 0))],
        out_specs=pl.BlockSpec((tm, d), lambda i: (i, 0)),
        out_shape=jax.ShapeDtypeStruct((m, d), F32),
        compiler_params=_cparams("parallel"),
    )(x, w.reshape(1, d))


def _prev_tok(x, n_ctx):
    t = jnp.arange(x.shape[1])[None, :, None]
    return jnp.where(t == n_ctx, 0.0, jnp.pad(x, ((0, 0), (1, 0), (0, 0)))[:, :-1])


def _next_tok(x, n_ctx):
    t = jnp.arange(x.shape[1])[None, :, None]
    return jnp.where(t == n_ctx - 1, 0.0, jnp.pad(x, ((0, 0), (0, 1), (0, 0)))[:, 1:])


def _conv3(x, w, n_ctx):
    return _prev_tok(x, n_ctx) * w[0] + x * w[1] + _next_tok(x, n_ctx) * w[2]


def _group_rms(y, w, n):
    b, t, c = y.shape
    yg = y.reshape(b, t, c // n, n)
    yg = yg * lax.rsqrt(jnp.mean(yg * yg, -1, keepdims=True) + EPS)
    return yg.reshape(b, t, c) * w


def _blk(p, name, width):
    o = _OFF[name]
    return p[..., o:o + width]


def _both(fn, *args, **kw):
    return fn(*args, reverse=False, **kw) + fn(*args, reverse=True, **kw)


def ssm_mixer(p, lp, n_ctx):
    b, t, _ = p.shape
    z = _blk(p, "ssm_z", BRANCH)
    xbc = jax.nn.silu(_conv3(_blk(p, "ssm_xbc", 768), lp["ssm_conv_w"], n_ctx) + lp["ssm_conv_b"])
    xs, bs, cs = xbc[..., :BRANCH], xbc[..., BRANCH:BRANCH + 128], xbc[..., BRANCH + 128:]
    dt = jax.nn.softplus(_blk(p, "ssm_dt", 16).reshape(b, t, 2, SSM_HEADS) + lp["ssm_dt_bias"])
    la = dt * -jnp.exp(lp["ssm_a_log"])
    y = (ssd_scan(xs, bs, cs, la[:, :, 0], dt[:, :, 0], n_ctx=n_ctx, reverse=False)
         + ssd_scan(xs, bs, cs, la[:, :, 1], dt[:, :, 1], n_ctx=n_ctx, reverse=True))
    y = (y + jnp.repeat(lp["ssm_d"], SSM_P) * xs) * jax.nn.silu(z)
    return _group_rms(y, lp["ssm_norm"], BRANCH // SSM_GROUPS)


def _lowrank(a, w_pair):
    b, t, k = a.shape
    r, c = w_pair.shape[1:]
    w = jnp.zeros((k, 2 * c), F32).at[:r, :c].set(w_pair[0]).at[r:2 * r, c:].set(w_pair[1])
    return pmatmul(a.reshape(b * t, k), w, tm=1024, tn=2 * c, precise=True).reshape(b, t, 2 * c)


def gla_mixer(p, lp, n_ctx):
    q = _blk(p, "gla_q", 256) * GLA_DK ** -0.5
    k = _blk(p, "gla_k", 256)
    v = _blk(p, "gla_v", BRANCH)
    r = _blk(p, "gla_r", BRANCH)
    logit = _lowrank(_blk(p, "gla_glr", LANES), lp["gla_w2"]) + lp["gla_b"].reshape(-1)
    la = jax.nn.log_sigmoid(logit) / GLA_TAU
    y = (gla_scan(q, k, v, la[..., :256], n_ctx=n_ctx, reverse=False)
         + gla_scan(q, k, v, la[..., 256:], n_ctx=n_ctx, reverse=True))
    return _group_rms(y, jnp.tile(lp["gla_norm"], GLA_HEADS), GLA_DV) * jax.nn.silu(r)


def rwkv_mixer(p, lp, n_ctx):
    b, t, _ = p.shape
    o = _OFF["rwkv_rkv"]
    pr = p[..., o:o + 1920]
    pr = pr + lp["rwkv_mu"] * (0.5 * (_prev_tok(pr, n_ctx) + _next_tok(pr, n_ctx)) - pr)
    r, k, v = pr[..., :512], pr[..., 512:1024], pr[..., 1024:1536]
    wlr, alr, glr = pr[..., 1536:1664], pr[..., 1664:1792], pr[..., 1792:1920]
    w_raw = _lowrank(jnp.tanh(wlr), lp["rwkv_w2"]) + lp["rwkv_w0"].reshape(-1)
    lw = -jnp.exp(-jax.nn.softplus(-w_raw) - 0.5)
    a = jax.nn.sigmoid(_lowrank(alr, lp["rwkv_a2"]) + lp["rwkv_a0"].reshape(-1))
    g = pmatmul(jax.nn.sigmoid(glr).reshape(b * t, 128), lp["rwkv_g2"], tm=1024, tn=BRANCH,
                precise=True).reshape(b, t, BRANCH)
    kk = k * lp["rwkv_k_k"]
    kkh = kk.reshape(b, t, RWKV_HEADS, RWKV_N)
    kk = (kkh * lax.rsqrt(jnp.sum(kkh * kkh, -1, keepdims=True) + EPS)).reshape(b, t, BRANCH)
    y = 0.0
    bonus = 0.0
    r_k = lp["rwkv_r_k"].reshape(-1)
    for d in range(2):
        ad = a[..., d * BRANCH:(d + 1) * BRANCH]
        kd = k * (1 + (ad - 1) * lp["rwkv_k_a"])
        y = y + rwkv_scan(r, kd, v, kk, ad, lw[..., d * BRANCH:(d + 1) * BRANCH], n_ctx=n_ctx,
                          reverse=bool(d))
        bonus = bonus + jnp.sum((r * kd * r_k).reshape(b, t, RWKV_HEADS, RWKV_N), -1, keepdims=True)
    bonus = (bonus * v.reshape(b, t, RWKV_HEADS, RWKV_N)).reshape(b, t, BRANCH)
    yh = y.reshape(b, t, RWKV_HEADS, RWKV_N)
    mu = jnp.mean(yh, -1, keepdims=True)
    var = jnp.mean(jnp.square(yh - mu), -1, keepdims=True)
    yn = ((yh - mu) * lax.rsqrt(var + RWKV_LN_EPS)).reshape(b, t, BRANCH)
    return (yn * lp["rwkv_ln_w"] + lp["rwkv_ln_b"] + bonus) * g


def gdn_mixer(p, lp, n_ctx):
    b, t, _ = p.shape
    qkv = jax.nn.silu(_conv3(_blk(p, "gdn_qkv", 1536), lp["gdn_conv_w"], n_ctx))
    z = _blk(p, "gdn_z", BRANCH)
    ab = _blk(p, "gdn_ab", 16)

    def l2n(x):
        xh = x.reshape(b, t, GDN_HEADS, GDN_N)
        return (xh * lax.rsqrt(jnp.sum(xh * xh, -1, keepdims=True) + EPS)).reshape(b, t, BRANCH)

    q = l2n(qkv[..., :512]) * GDN_N ** -0.5
    k = l2n(qkv[..., 512:1024])
    v = qkv[..., 1024:]
    a_raw = ab[..., :8].reshape(b, t, 2, GDN_HEADS)
    b_raw = ab[..., 8:].reshape(b, t, 2, GDN_HEADS)
    g = -jnp.exp(lp["gdn_a_log"]) * jax.nn.softplus(a_raw + lp["gdn_dt_bias"])
    beta = jax.nn.sigmoid(b_raw)
    y = (gdn_scan(q, k, v, g[:, :, 0], beta[:, :, 0], n_ctx=n_ctx, reverse=False)
         + gdn_scan(q, k, v, g[:, :, 1], beta[:, :, 1], n_ctx=n_ctx, reverse=True))
    return _group_rms(y, jnp.tile(lp["gdn_norm"], GDN_HEADS), GDN_N) * jax.nn.silu(z)


def _to_scan_order(t):
    b, n, d = t.shape
    return t.reshape(b, n // GRID_W, GRID_W, d).transpose(0, 2, 1, 3).reshape(b, n, d)


def _from_scan_order(t):
    b, n, d = t.shape
    return t.reshape(b, GRID_W, n // GRID_W, d).transpose(0, 2, 1, 3).reshape(b, n, d)


def kernel(x, c, ctx, c_ctx, ada_w, ada_b, norm_mix, norm_ffn, w_in, w_gate, w_branch, w_out, ssm_conv_w, ssm_conv_b, ssm_a_log, ssm_dt_bias, ssm_d, ssm_norm, gla_w2, gla_b, gla_norm, rwkv_mu, rwkv_w0, rwkv_w2, rwkv_a0, rwkv_a2, rwkv_g2, rwkv_k_k, rwkv_k_a, rwkv_r_k, rwkv_ln_w, rwkv_ln_b, gdn_conv_w, gdn_a_log, gdn_dt_bias, gdn_norm, router_w, router_b, moe_w_gate, moe_w_up, moe_w_down, final_norm):
    bsz, seq, d = x.shape
    n_ctx = ctx.shape[1]
    t_all = n_ctx + seq
    m_all = bsz * t_all
    tm = 256

    cond = jnp.concatenate([jax.nn.silu(c), jax.nn.silu(c_ctx)[None]], 0)
    cond = jnp.pad(cond, ((0, 8 - cond.shape[0]), (0, 0)))
    mods = []
    for l in range(DEPTH):
        mod = pmatmul(cond, ada_w[l], tm=8, tn=1024, precise=True) + ada_b[l]
        lat = mod[:bsz].reshape(bsz, 6, d)
        cx = jnp.broadcast_to(mod[bsz].reshape(1, 6, d), (bsz, 6, d))
        mods.append(jnp.stack([cx, lat], axis=1))

    src = jnp.asarray(np.maximum(_SRC_COLS, 0))
    valid = jnp.asarray(_SRC_COLS >= 0)

    x_all = jnp.concatenate([ctx, x], axis=1)
    scan_order = False
    for l in range(DEPTH):
        if (l % 2 == 1) != scan_order:
            reorder = _from_scan_order if scan_order else _to_scan_order
            x_all = jnp.concatenate([x_all[:, :n_ctx], reorder(x_all[:, n_ctx:])], axis=1)
            scan_order = not scan_order
        lp = dict(ssm_conv_w=ssm_conv_w[l], ssm_conv_b=ssm_conv_b[l], ssm_a_log=ssm_a_log[l],
                  ssm_dt_bias=ssm_dt_bias[l], ssm_d=ssm_d[l], ssm_norm=ssm_norm[l],
                  gla_w2=gla_w2[l], gla_b=gla_b[l], gla_norm=gla_norm[l],
                  rwkv_mu=rwkv_mu[l], rwkv_w0=rwkv_w0[l], rwkv_w2=rwkv_w2[l], rwkv_a0=rwkv_a0[l],
                  rwkv_a2=rwkv_a2[l], rwkv_g2=rwkv_g2[l], rwkv_k_k=rwkv_k_k[l], rwkv_k_a=rwkv_k_a[l],
                  rwkv_r_k=rwkv_r_k[l], rwkv_ln_w=rwkv_ln_w[l], rwkv_ln_b=rwkv_ln_b[l],
                  gdn_conv_w=gdn_conv_w[l], gdn_a_log=gdn_a_log[l], gdn_dt_bias=gdn_dt_bias[l],
                  gdn_norm=gdn_norm[l])
        mod = mods[l]
        msel = lambda i: mod[:, :, i][:, :, None, :]

        h = norm_modulate(x_all, norm_mix[l], msel(0), msel(1), n_ctx=n_ctx, tm=tm)
        h2d = h.reshape(m_all, d)
        w_in_p = jnp.where(valid[None, :], w_in[l][:, src], 0.0).astype(BF16)
        p = pmatmul(h2d, w_in_p, tm=1024, tn=1024).reshape(bsz, t_all, N_PACKED)
        wg_cat = jnp.concatenate([w_gate[l, i] for i in range(4)], axis=1).astype(BF16)
        gates = pmatmul(h2d, wg_cat, tm=1024, tn=1024, act="sigmoid", out_dtype=BF16)
        gates = gates.reshape(bsz, t_all, 4 * d)

        y_cat = jnp.concatenate([ssm_mixer(p, lp, n_ctx), gla_mixer(p, lp, n_ctx),
                                 rwkv_mixer(p, lp, n_ctx), gdn_mixer(p, lp, n_ctx)], axis=-1)
        x_all = merge_residual(y_cat, gates, w_branch[l].astype(BF16), w_out[l].astype(BF16),
                               x_all, msel(2), n_ctx=n_ctx, tm=tm)

        h = norm_modulate(x_all, norm_ffn[l], msel(3), msel(4), n_ctx=n_ctx, tm=tm, out_dtype=F32)
        gate = moe_route(h.reshape(m_all, d), router_w, router_b, tm=1024)
        gate = gate.T.reshape(bsz, t_all, N_EXPERTS)
        x_all = moe_experts(h, gate, moe_w_gate[l].astype(BF16), moe_w_up[l].astype(BF16),
                            moe_w_down[l].astype(BF16), x_all, msel(5), n_ctx=n_ctx, tm=tm)

    lat = x_all[:, n_ctx:]
    if scan_order:
        lat = _from_scan_order(lat)
    return final_rms_norm(lat.reshape(bsz * seq, d), final_norm, tm=1024).reshape(bsz, seq, d)
```

```python
import functools

import numpy as np
import jax
import jax.numpy as jnp
from jax import lax
from jax.experimental import pallas as pl
from jax.experimental.pallas import tpu as pltpu

F32 = jnp.float32
BF16 = jnp.bfloat16
HI = lax.Precision.HIGHEST

D_MODEL = 1024
DEPTH = 2
GRID_W = 64
CHUNK = 64
EPS = 1e-6
BRANCH = D_MODEL // 2
SSM_HEADS, SSM_P, SSM_GROUPS, SSM_N = 8, 64, 2, 64
GLA_HEADS, GLA_DK, GLA_DV, GLA_RANK, GLA_TAU = 4, 64, 128, 16, 16.0
RWKV_HEADS, RWKV_N, RWKV_LN_EPS = 8, 64, 64e-5
GDN_HEADS, GDN_N = 4, 128
N_EXPERTS, N_GROUPS, EXPERTS_PER_GROUP = 16, 4, 4
EXPERT_FF = D_MODEL // 2
LANES = 128
SUBLANES = 8
VMEM_LIMIT = 48 * 1024 * 1024
ROW_TILE = 256

_REF_BLOCKS = (
    ("ssm", "z", 512), ("ssm", "xbc", 768), ("ssm", "dt", 16),
    ("gla", "q", 256), ("gla", "k", 256), ("gla", "v", 512), ("gla", "r", 512), ("gla", "glr", 32),
    ("rwkv", "all", 1920),
    ("gdn", "qkv", 1536), ("gdn", "z", 512), ("gdn", "ab", 16),
)
_PACKED = {
    "ssm": (("z", 512), ("dt", 128), ("pad", 128), ("xbc", 768)),
    "gla": (("q", 256), ("k", 256), ("v", 512), ("r", 512), ("glr", 128)),
    "rwkv": (("all", 1920),),
    "gdn": (("qkv", 1536), ("z", 512), ("ab", 128)),
}


def _packed_columns():
    start, s = {}, 0
    for mixer, blk, w in _REF_BLOCKS:
        start[(mixer, blk)] = (s, w)
        s += w
    out = {}
    for mixer, blocks in _PACKED.items():
        cols = []
        for blk, wp in blocks:
            s0, w = start.get((mixer, blk), (0, 0))
            cols += list(range(s0, s0 + w)) + [-1] * (wp - w)
        out[mixer] = np.asarray(cols, np.int32)
    return out


_SRC_COLS = _packed_columns()


def _cparams(*sem):
    return pltpu.CompilerParams(dimension_semantics=sem, vmem_limit_bytes=VMEM_LIMIT)


def _dot(a, b):
    return jnp.dot(a.astype(BF16), b.astype(BF16), preferred_element_type=F32)


def _dot_nt(a, b):
    return lax.dot_general(a.astype(BF16), b.astype(BF16), (((1,), (1,)), ((), ())),
                           preferred_element_type=F32)


def _dot_tn(a, b):
    return lax.dot_general(a.astype(BF16), b.astype(BF16), (((0,), (0,)), ((), ())),
                           preferred_element_type=F32)


def _dot_hi(a, b):
    return jnp.dot(a, b, precision=HI, preferred_element_type=F32)


def _dot_x3(a, b):
    ah = a.astype(BF16)
    al = (a - ah.astype(F32)).astype(BF16)
    bh = b.astype(BF16)
    bl = (b - bh.astype(F32)).astype(BF16)
    f = lambda u, v: jnp.dot(u, v, preferred_element_type=F32)
    return f(ah, bh) + (f(ah, bl) + f(al, bh))


def _dot_x2(a, w):
    ah = a.astype(BF16)
    al = (a - ah.astype(F32)).astype(BF16)
    return jnp.dot(ah, w, preferred_element_type=F32) + jnp.dot(al, w, preferred_element_type=F32)


def _softplus(x):
    return jnp.maximum(x, 0.0) + jnp.log(1.0 + jnp.exp(-jnp.abs(x)))


def _sigmoid(x):
    return 1.0 / (1.0 + jnp.exp(-x))


def _silu(x):
    return x * _sigmoid(x)


def _pick_tile(m, pref):
    t = pref
    while m % t:
        t //= 2
    return t


def _block_diag_ones(n, width=BRANCH):
    idx = np.arange(width) // n
    return jnp.asarray(idx[:, None] == idx[None, :], BF16)


def _mm_kernel(a_ref, w_ref, o_ref, *, act, precise):
    if precise:
        r = _dot_hi(a_ref[...].astype(F32), w_ref[...].astype(F32))
    else:
        r = _dot(a_ref[...], w_ref[...])
    if act == "sigmoid":
        r = _sigmoid(r)
    o_ref[...] = r.astype(o_ref.dtype)


def pmatmul(a, w, *, tm, tn, act=None, precise=False, out_dtype=F32):
    m, k = a.shape
    n = w.shape[1]
    tm = _pick_tile(m, tm)
    assert tm % SUBLANES == 0 and n % tn == 0, (m, tm, n, tn)
    return pl.pallas_call(
        functools.partial(_mm_kernel, act=act, precise=precise),
        grid=(n // tn, m // tm),
        in_specs=[pl.BlockSpec((tm, k), lambda j, i: (i, 0)),
                  pl.BlockSpec((k, tn), lambda j, i: (0, j))],
        out_specs=pl.BlockSpec((tm, tn), lambda j, i: (i, j)),
        out_shape=jax.ShapeDtypeStruct((m, n), out_dtype),
        compiler_params=_cparams("parallel", "parallel"),
    )(a, w)


def _norm_mod_kernel(x_ref, w_ref, shift_ref, scale_ref, o_ref):
    x = x_ref[0]
    y = x * lax.rsqrt(jnp.mean(x * x, axis=-1, keepdims=True) + EPS) * w_ref[...]
    o_ref[0] = (y * (1.0 + scale_ref[0, 0]) + shift_ref[0, 0]).astype(o_ref.dtype)


def _mod_sel(nct):
    return lambda bi, i, *_: (bi, jnp.where(i < nct, 0, 1), 0, 0)


def norm_modulate(x_all, w, shift, scale, *, n_ctx, out_dtype=BF16):
    b, t, d = x_all.shape
    tm = _pick_tile(n_ctx, ROW_TILE)
    assert t % tm == 0
    tok = lambda bi, i: (bi, i, 0)
    return pl.pallas_call(
        _norm_mod_kernel,
        grid=(b, t // tm),
        in_specs=[pl.BlockSpec((1, tm, d), tok),
                  pl.BlockSpec((1, d), lambda bi, i: (0, 0)),
                  pl.BlockSpec((1, 1, 1, d), _mod_sel(n_ctx // tm)),
                  pl.BlockSpec((1, 1, 1, d), _mod_sel(n_ctx // tm))],
        out_specs=pl.BlockSpec((1, tm, d), tok),
        out_shape=jax.ShapeDtypeStruct((b, t, d), out_dtype),
        compiler_params=_cparams("parallel", "parallel"),
    )(x_all, w.reshape(1, d), shift, scale)


def _row(v):
    return v.reshape(1, -1).astype(F32)


def _const_spec(shape):
    return pl.BlockSpec(shape, lambda *_: (0,) * len(shape))


def _tile_specs(tt, width, col):
    r8 = tt // SUBLANES
    main = pl.BlockSpec((1, tt, width), lambda bi, i: (bi, i, col))
    prev = pl.BlockSpec((1, SUBLANES, width), lambda bi, i: (bi, jnp.maximum(i * r8 - 1, 0), col))
    return main, prev, r8


def _halo_specs(tt, width, col, t):
    main, prev, r8 = _tile_specs(tt, width, col)
    last8 = t // SUBLANES - 1
    nxt = pl.BlockSpec((1, SUBLANES, width), lambda bi, i: (bi, jnp.minimum((i + 1) * r8, last8), col))
    return [main, prev, nxt]


def _neighbours(x, prev8, next8, *, nct, nt):
    i = pl.program_id(1)
    tt = x.shape[0]
    row = lax.broadcasted_iota(jnp.int32, x.shape, 0)
    first = (i == 0) | (i == nct)
    last = (i == nct - 1) | (i == nt - 1)
    pr = jnp.where(first, 0.0, prev8[SUBLANES - 1:SUBLANES, :])
    nx = jnp.where(last, 0.0, next8[0:1, :])
    xp = jnp.where(row == 0, pr, pltpu.roll(x, 1, 0))
    xn = jnp.where(row == tt - 1, nx, pltpu.roll(x, tt - 1, 0))
    return xp, xn


def _prep_call(kernel, ins, in_specs, out_widths, *, b, t, tt, out_dtype=F32):
    tok = lambda bi, i: (bi, i, 0)
    return pl.pallas_call(
        kernel,
        grid=(b, t // tt),
        in_specs=in_specs,
        out_specs=[pl.BlockSpec((1, tt, w), tok) for w in out_widths],
        out_shape=[jax.ShapeDtypeStruct((b, t, w), out_dtype) for w in out_widths],
        compiler_params=_cparams("parallel", "parallel"),
    )(*ins)


def _chunk_masks(reverse):
    row = lax.broadcasted_iota(jnp.int32, (CHUNK, CHUNK), 0)
    col = lax.broadcasted_iota(jnp.int32, (CHUNK, CHUNK), 1)
    if reverse:
        return col >= row, col > row
    return col <= row, col < row


def _chunk_order(i, n_ctx_chunks, n_chunks, reverse):
    if not reverse:
        return i
    return jnp.where(i < n_ctx_chunks, n_ctx_chunks - 1 - i, n_chunks - 1 - (i - n_ctx_chunks))


def _transpose_small(x):
    row = lax.broadcasted_iota(jnp.int32, (LANES, LANES), 0)
    col = lax.broadcasted_iota(jnp.int32, (LANES, LANES), 1)
    eye = (row == col).astype(F32)
    return lax.dot_general(eye, x, (((1,), (1,)), ((), ())), precision=HI, preferred_element_type=F32)


def _unit_tri_solve(mats, rhs):
    n = range(len(mats))
    x = [rhs[h] - _dot_x3(mats[h], rhs[h]) for h in n]
    p = mats
    for _ in range(int(np.log2(CHUNK)) - 1):
        p = [_dot_x3(p[h], p[h]) for h in n]
        x = [x[h] + _dot_x3(p[h], x[h]) for h in n]
    return x


def _scan_call(kernel, ins, in_specs, state_shape, *, b, t, n_ctx, reverse):
    nc, ncc = t // CHUNK, n_ctx // CHUNK
    tok = lambda bi, i: (bi, _chunk_order(i, ncc, nc, reverse), 0)
    return pl.pallas_call(
        kernel,
        grid=(b, nc),
        in_specs=in_specs,
        out_specs=pl.BlockSpec((1, CHUNK, BRANCH), tok),
        out_shape=jax.ShapeDtypeStruct((b, t, BRANCH), F32),
        scratch_shapes=[pltpu.VMEM(state_shape, F32)],
        compiler_params=_cparams("parallel", "arbitrary"),
    )(*ins)


def _chunk_spec(width, col, *, t, n_ctx, reverse):
    nc, ncc = t // CHUNK, n_ctx // CHUNK
    return pl.BlockSpec((1, CHUNK, width), lambda bi, i: (bi, _chunk_order(i, ncc, nc, reverse), col))


def _zero_state_at_start(s_ref):
    @pl.when(pl.program_id(1) == 0)
    def _():
        s_ref[...] = jnp.zeros_like(s_ref)


def _ssm_prep_kernel(x_ref, xp_ref, xn_ref, dt_ref, cw_ref, cb_ref, dtb_ref, xs_ref, bc_ref, sm_ref,
                     *, nct, nt):
    x = x_ref[0]
    xp, xn = _neighbours(x, xp_ref[0], xn_ref[0], nct=nct, nt=nt)
    y = _silu(xp * cw_ref[0:1, :] + x * cw_ref[1:2, :] + xn * cw_ref[2:3, :] + cb_ref[...])
    xs_ref[0] = y[:, :BRANCH]
    bc_ref[0] = y[:, BRANCH:]
    sm_ref[0] = _softplus(dt_ref[0] + dtb_ref[...])


def ssm_prep(p, lp, *, n_ctx):
    b, t, _ = p.shape
    tt = _pick_tile(n_ctx, ROW_TILE)
    dtb = jnp.pad(lp["ssm_dt_bias"].reshape(1, -1), ((0, 0), (0, LANES - 2 * SSM_HEADS)))
    specs = _halo_specs(tt, 768, 1, t) + [pl.BlockSpec((1, tt, LANES), lambda bi, i: (bi, i, 4)),
                                          _const_spec((3, 768)), _const_spec((1, 768)), _const_spec((1, LANES))]
    kern = functools.partial(_ssm_prep_kernel, nct=n_ctx // tt, nt=t // tt)
    return _prep_call(kern, (p, p, p, p, lp["ssm_conv_w"], _row(lp["ssm_conv_b"]), dtb), specs,
                      (BRANCH, 2 * SSM_GROUPS * SSM_N, LANES), b=b, t=t, tt=tt)


def _ssd_kernel(x_ref, bc_ref, sm_ref, na_ref, o_ref, s_ref, *, reverse):
    _zero_state_at_start(s_ref)
    incl, _ = _chunk_masks(reverse)
    last = 0 if reverse else CHUNK - 1
    off = SSM_HEADS if reverse else 0
    dt_all = sm_ref[0]
    g_all = _dot_hi(incl.astype(F32), dt_all * na_ref[...])
    gt_all = _transpose_small(g_all)
    dtt_all = _transpose_small(dt_all)
    heads = range(SSM_HEADS)
    rep = SSM_HEADS // SSM_GROUPS
    gw = SSM_GROUPS * SSM_N
    bm = [bc_ref[0, :, grp * SSM_N:(grp + 1) * SSM_N] for grp in range(SSM_GROUPS)]
    cm = [bc_ref[0, :, gw + grp * SSM_N:gw + (grp + 1) * SSM_N] for grp in range(SSM_GROUPS)]
    cb = [_dot_nt(cm[grp], bm[grp]) for grp in range(SSM_GROUPS)]
    xh = [x_ref[0, :, h * SSM_P:(h + 1) * SSM_P] for h in heads]
    s = [s_ref[h] for h in heads]
    gh = [g_all[:, off + h:off + h + 1] for h in heads]
    gl = [gh[h][last:last + 1, :] for h in heads]
    dth = [dt_all[:, off + h:off + h + 1] for h in heads]
    scores = [cb[h // rep] * jnp.exp(jnp.where(incl, gh[h] - gt_all[off + h:off + h + 1, :], -jnp.inf))
              * dtt_all[off + h:off + h + 1, :] for h in heads]
    intra = [_dot(scores[h], xh[h]) for h in heads]
    inter = [_dot(cm[h // rep] * jnp.exp(gh[h]), s[h]) for h in heads]
    upd = [_dot_tn(bm[h // rep] * (dth[h] * jnp.exp(gl[h] - gh[h])), xh[h]) for h in heads]
    for h in heads:
        o_ref[0, :, h * SSM_P:(h + 1) * SSM_P] = intra[h] + inter[h]
        s_ref[h] = s[h] * jnp.exp(gl[h]) + upd[h]


def ssd_scan(xs, bc, sm, neg_a, *, n_ctx, reverse):
    b, t, _ = xs.shape
    kw = dict(t=t, n_ctx=n_ctx, reverse=reverse)
    specs = [_chunk_spec(BRANCH, 0, **kw), _chunk_spec(2 * SSM_GROUPS * SSM_N, 0, **kw),
             _chunk_spec(LANES, 0, **kw), _const_spec((1, LANES))]
    return _scan_call(functools.partial(_ssd_kernel, reverse=reverse), (xs, bc, sm, neg_a), specs,
                      (SSM_HEADS, SSM_N, SSM_P), b=b, **kw)


def _gla_kernel(q_ref, k_ref, v_ref, glr_ref, w2_ref, gb_ref, o_ref, s_ref, *, reverse):
    _zero_state_at_start(s_ref)
    incl, _ = _chunk_masks(reverse)
    last = 0 if reverse else CHUNK - 1
    logit = _dot_x3(glr_ref[0], w2_ref[...]) + gb_ref[...]
    la = -_softplus(-logit) * (1.0 / GLA_TAU)
    g_all = _dot_hi(incl.astype(F32), la)
    heads = range(GLA_HEADS)
    ks = [slice(h * GLA_DK, (h + 1) * GLA_DK) for h in heads]
    vs = [slice(h * GLA_DV, (h + 1) * GLA_DV) for h in heads]
    g = [g_all[:, ks[h]] for h in heads]
    gl = [g[h][last:last + 1, :] for h in heads]
    k = [k_ref[0, :, ks[h]] for h in heads]
    v = [v_ref[0, :, vs[h]] for h in heads]
    qg = [q_ref[0, :, ks[h]] * (GLA_DK ** -0.5) * jnp.exp(g[h]) for h in heads]
    st = [s_ref[h] for h in heads]
    scores = [jnp.where(incl, _dot_nt(qg[h], k[h] * jnp.exp(-g[h])), 0.0) for h in heads]
    intra = [_dot(scores[h], v[h]) for h in heads]
    inter = [_dot_nt(qg[h], st[h]) for h in heads]
    upd = [_dot_tn(v[h], k[h] * jnp.exp(gl[h] - g[h])) for h in heads]
    for h in heads:
        o_ref[0, :, vs[h]] = intra[h] + inter[h]
        s_ref[h] = st[h] * jnp.exp(gl[h]) + upd[h]


def gla_scan(p, w2, gb, *, n_ctx, reverse):
    b, t, _ = p.shape
    kw = dict(t=t, n_ctx=n_ctx, reverse=reverse)
    kwid = GLA_HEADS * GLA_DK
    specs = [_chunk_spec(kwid, 0, **kw), _chunk_spec(kwid, 1, **kw), _chunk_spec(BRANCH, 1, **kw),
             _chunk_spec(LANES, 12, **kw), _const_spec((LANES, kwid)), _const_spec((1, kwid))]
    return _scan_call(functools.partial(_gla_kernel, reverse=reverse), (p, p, p, p, w2, gb), specs,
                      (GLA_HEADS, GLA_DV, GLA_DK), b=b, **kw)


def _rwkv_prep_kernel(x_ref, xp_ref, xn_ref, mu_ref, w2_ref, w0_ref, a2_ref, a0_ref, g2_ref, kk_ref_w,
                      ka_ref, rk_ref, bd_ref, r_ref, k_ref, v_ref, kk_ref, a_ref, lw_ref, g_ref, bo_ref,
                      *, nct, nt):
    x = x_ref[0]
    xp, xn = _neighbours(x, xp_ref[0], xn_ref[0], nct=nct, nt=nt)
    x = x + mu_ref[...] * (0.5 * (xp + xn) - x)
    r, k, v = x[:, :BRANCH], x[:, BRANCH:2 * BRANCH], x[:, 2 * BRANCH:3 * BRANCH]
    wlr = x[:, 3 * BRANCH:3 * BRANCH + LANES]
    alr = x[:, 3 * BRANCH + LANES:3 * BRANCH + 2 * LANES]
    glr = x[:, 3 * BRANCH + 2 * LANES:]
    w_raw = _dot_x3(jnp.tanh(wlr), w2_ref[...]) + w0_ref[...]
    lw_ref[0] = -jnp.exp(-_softplus(-w_raw) - 0.5)
    a = _sigmoid(_dot_x3(alr, a2_ref[...]) + a0_ref[...])
    a_ref[0] = a
    g_ref[0] = _dot_x3(_sigmoid(glr), g2_ref[...])
    kk = k * kk_ref_w[...]
    kk_ref[0] = kk * lax.rsqrt(_dot_x2(kk * kk, bd_ref[...]) + EPS)
    ksum = k * (2.0 + (a[:, :BRANCH] + a[:, BRANCH:] - 2.0) * ka_ref[...])
    bo_ref[0] = _dot_x2(r * ksum * rk_ref[...], bd_ref[...]) * v
    r_ref[0] = r
    k_ref[0] = k
    v_ref[0] = v


def rwkv_prep(p, lp, *, n_ctx):
    b, t, w = p.shape
    tt = _pick_tile(n_ctx, ROW_TILE)

    def pair(wp):
        r, c = wp.shape[1:]
        return jnp.zeros((LANES, 2 * c), F32).at[:r, :c].set(wp[0]).at[r:2 * r, c:].set(wp[1])

    consts = (_row(lp["rwkv_mu"]), pair(lp["rwkv_w2"]), _row(lp["rwkv_w0"]), pair(lp["rwkv_a2"]),
              _row(lp["rwkv_a0"]), lp["rwkv_g2"], _row(lp["rwkv_k_k"]), _row(lp["rwkv_k_a"]),
              _row(lp["rwkv_r_k"]), _block_diag_ones(RWKV_N))
    specs = _halo_specs(tt, w, 0, t) + [_const_spec(c.shape) for c in consts]
    kern = functools.partial(_rwkv_prep_kernel, nct=n_ctx // tt, nt=t // tt)
    return _prep_call(kern, (p, p, p) + consts, specs,
                      (BRANCH, BRANCH, BRANCH, BRANCH, 2 * BRANCH, 2 * BRANCH, BRANCH, BRANCH),
                      b=b, t=t, tt=tt)


def _rwkv_kernel(r_ref, k_ref, v_ref, kk_ref, a_ref, lw_ref, ka_ref, o_ref, s_ref, *, reverse):
    _zero_state_at_start(s_ref)
    incl, strict = _chunk_masks(reverse)
    last = 0 if reverse else CHUNK - 1
    lw_all = lw_ref[0]
    g_all = _dot_hi(incl.astype(F32), lw_all)
    a_all = a_ref[0]
    k_all = k_ref[0] * (1.0 + (a_all - 1.0) * ka_ref[...])
    heads = range(RWKV_HEADS)
    hs = [slice(h * RWKV_N, (h + 1) * RWKV_N) for h in heads]
    g = [g_all[:, hs[h]] for h in heads]
    gl = [g[h][last:last + 1, :] for h in heads]
    eneg = [jnp.exp(-g[h]) for h in heads]
    edec = [jnp.exp(gl[h] - g[h]) for h in heads]
    kk = [kk_ref[0, :, hs[h]] for h in heads]
    bvec = [kk[h] * a_all[:, hs[h]] for h in heads]
    k = [k_all[:, hs[h]] for h in heads]
    v = [v_ref[0, :, hs[h]] for h in heads]
    kkg = [kk[h] * jnp.exp(g[h] - lw_all[:, hs[h]]) for h in heads]
    rg = [r_ref[0, :, hs[h]] * jnp.exp(g[h]) for h in heads]
    bh = [bvec[h] * eneg[h] for h in heads]
    kh = [k[h] * eneg[h] for h in heads]
    s = [s_ref[h] for h in heads]
    a_mat = [jnp.where(strict, _dot_nt(kkg[h], bh[h]), 0.0) for h in heads]
    b_mat = [jnp.where(strict, _dot_nt(kkg[h], kh[h]), 0.0) for h in heads]
    aqb = [jnp.where(incl, _dot_nt(rg[h], bh[h]), 0.0) for h in heads]
    aqk = [jnp.where(incl, _dot_nt(rg[h], kh[h]), 0.0) for h in heads]
    rhs = [_dot(b_mat[h], v[h]) + _dot_nt(kkg[h], s[h]) for h in heads]
    o_part = [_dot_nt(rg[h], s[h]) + _dot(aqk[h], v[h]) for h in heads]
    u = [-x for x in _unit_tri_solve(a_mat, rhs)]
    for h in heads:
        o_ref[0, :, hs[h]] = o_part[h] + _dot(aqb[h], u[h])
    for h in heads:
        s_ref[h] = (s[h] * jnp.exp(gl[h]) + _dot_tn(u[h], bvec[h] * edec[h])
                    + _dot_tn(v[h], k[h] * edec[h]))


def rwkv_scan(r, k, v, kk, a, lw, k_a, *, n_ctx, reverse):
    b, t, _ = r.shape
    kw = dict(t=t, n_ctx=n_ctx, reverse=reverse)
    d = int(reverse)
    one = _chunk_spec(BRANCH, 0, **kw)
    specs = [one, one, one, one, _chunk_spec(BRANCH, d, **kw), _chunk_spec(BRANCH, d, **kw),
             _const_spec((1, BRANCH))]
    return _scan_call(functools.partial(_rwkv_kernel, reverse=reverse), (r, k, v, kk, a, lw, k_a), specs,
                      (RWKV_HEADS, RWKV_N, RWKV_N), b=b, **kw)


def _gdn_prep_kernel(x_ref, xp_ref, xn_ref, ab_ref, cw_ref, na_ref, dtb_ref, bd_ref,
                     q_ref, k_ref, v_ref, sm_ref, *, nct, nt):
    x = x_ref[0]
    xp, xn = _neighbours(x, xp_ref[0], xn_ref[0], nct=nct, nt=nt)
    y = _silu(xp * cw_ref[0:1, :] + x * cw_ref[1:2, :] + xn * cw_ref[2:3, :])
    q, k = y[:, :BRANCH], y[:, BRANCH:2 * BRANCH]
    q_ref[0] = q * lax.rsqrt(_dot_x2(q * q, bd_ref[...]) + EPS) * (GDN_N ** -0.5)
    k_ref[0] = k * lax.rsqrt(_dot_x2(k * k, bd_ref[...]) + EPS)
    v_ref[0] = y[:, 2 * BRANCH:]
    ab = ab_ref[0]
    lane = lax.broadcasted_iota(jnp.int32, ab.shape, 1)
    sm_ref[0] = jnp.where(lane < 2 * GDN_HEADS, na_ref[...] * _softplus(ab + dtb_ref[...]), _sigmoid(ab))


def gdn_prep(p, lp, *, n_ctx):
    b, t, _ = p.shape
    tt = _pick_tile(n_ctx, ROW_TILE)
    padrow = lambda v: jnp.pad(v.reshape(1, -1), ((0, 0), (0, LANES - 2 * GDN_HEADS)))
    consts = (lp["gdn_conv_w"], padrow(-jnp.exp(lp["gdn_a_log"])), padrow(lp["gdn_dt_bias"]),
              _block_diag_ones(GDN_N))
    specs = (_halo_specs(tt, 3 * BRANCH, 0, t) + [pl.BlockSpec((1, tt, LANES), lambda bi, i: (bi, i, 16))]
             + [_const_spec(c.shape) for c in consts])
    kern = functools.partial(_gdn_prep_kernel, nct=n_ctx // tt, nt=t // tt)
    return _prep_call(kern, (p, p, p, p) + consts, specs, (BRANCH, BRANCH, BRANCH, LANES), b=b, t=t, tt=tt)


def _gdn_kernel(q_ref, k_ref, v_ref, sm_ref, o_ref, s_ref, *, reverse):
    _zero_state_at_start(s_ref)
    incl, strict = _chunk_masks(reverse)
    last = 0 if reverse else CHUNK - 1
    off = GDN_HEADS if reverse else 0
    sm = sm_ref[0]
    g_all = _dot_hi(incl.astype(F32), sm)
    gt_all = _transpose_small(g_all)
    heads = range(GDN_HEADS)
    hs = [slice(h * GDN_N, (h + 1) * GDN_N) for h in heads]
    g = [g_all[:, off + h:off + h + 1] for h in heads]
    gl = [g[h][last:last + 1, :] for h in heads]
    beta = [sm[:, 2 * GDN_HEADS + off + h:2 * GDN_HEADS + off + h + 1] for h in heads]
    q = [q_ref[0, :, hs[h]] for h in heads]
    k = [k_ref[0, :, hs[h]] for h in heads]
    v = [v_ref[0, :, hs[h]] for h in heads]
    s = [s_ref[h] for h in heads]
    decay = [jnp.exp(jnp.where(incl, g[h] - gt_all[off + h:off + h + 1, :], -jnp.inf)) for h in heads]
    lower = [jnp.where(strict, _dot_nt(k[h], k[h]) * decay[h] * beta[h], 0.0) for h in heads]
    attn = [_dot_nt(q[h], k[h]) * decay[h] for h in heads]
    o_part = [_dot(q[h] * jnp.exp(g[h]), s[h]) for h in heads]
    rhs = [jnp.concatenate([v[h] * beta[h], k[h] * (beta[h] * jnp.exp(g[h]))], axis=1) for h in heads]
    sol = _unit_tri_solve(lower, rhs)
    v_new = [sol[h][:, :GDN_N] - _dot(sol[h][:, GDN_N:], s[h]) for h in heads]
    for h in heads:
        o_ref[0, :, hs[h]] = o_part[h] + _dot(attn[h], v_new[h])
        s_ref[h] = s[h] * jnp.exp(gl[h]) + _dot_tn(k[h] * jnp.exp(gl[h] - g[h]), v_new[h])


def gdn_scan(q, k, v, sm, *, n_ctx, reverse):
    b, t, _ = q.shape
    kw = dict(t=t, n_ctx=n_ctx, reverse=reverse)
    one = _chunk_spec(BRANCH, 0, **kw)
    specs = [one, one, one, _chunk_spec(LANES, 0, **kw)]
    return _scan_call(functools.partial(_gdn_kernel, reverse=reverse), (q, k, v, sm), specs,
                      (GDN_HEADS, GDN_N, GDN_N), b=b, **kw)


def _merge_kernel(sf_ref, sb_ref, sx_ref, sz_ref, gf_ref, gb_ref, gr_ref, rf_ref, rb_ref, rg_ref, rbo_ref,
                  df_ref, db_ref, dz_ref, gate_ref, x_ref, m_ref,
                  sd_ref, sn_ref, gn_ref, lnw_ref, lnb_ref, dn_ref, bd64_ref, bd128_ref, bd256_ref,
                  wb_ref, wo_ref, o_ref):
    def group_rms(y, bd_ref, n, w_ref):
        return y * lax.rsqrt(_dot_x2(y * y, bd_ref[...]) * (1.0 / n) + EPS) * w_ref[...]

    y = (sf_ref[0] + sb_ref[0] + sd_ref[...] * sx_ref[0]) * _silu(sz_ref[0])
    ys = group_rms(y, bd256_ref, BRANCH // SSM_GROUPS, sn_ref)
    yg = group_rms(gf_ref[0] + gb_ref[0], bd128_ref, GLA_DV, gn_ref) * _silu(gr_ref[0])
    y = rf_ref[0] + rb_ref[0]
    yc = y - _dot_x2(y, bd64_ref[...]) * (1.0 / RWKV_N)
    var = _dot_x2(yc * yc, bd64_ref[...]) * (1.0 / RWKV_N)
    yr = (yc * lax.rsqrt(var + RWKV_LN_EPS) * lnw_ref[...] + lnb_ref[...] + rbo_ref[0]) * rg_ref[0]
    yd = group_rms(df_ref[0] + db_ref[0], bd128_ref, GDN_N, dn_ref) * _silu(dz_ref[0])
    acc = None
    for i, yi in enumerate((ys, yg, yr, yd)):
        term = gate_ref[0, :, i * D_MODEL:(i + 1) * D_MODEL].astype(F32) * _dot(yi, wb_ref[i])
        acc = term if acc is None else acc + term
    o_ref[0] = x_ref[0] + m_ref[0, 0] * _dot(acc, wo_ref[...])


def merge_residual(ssm, gla, rwkv, gdn, gates, x_all, gate_mod, lp, w_branch, w_out, *, n_ctx):
    b, t, d = x_all.shape
    tm = _pick_tile(n_ctx, ROW_TILE)
    tok = lambda bi, i: (bi, i, 0)
    blk = lambda c: pl.BlockSpec((1, tm, BRANCH), lambda bi, i: (bi, i, c))
    half = blk(0)
    consts = (_row(jnp.repeat(lp["ssm_d"], SSM_P)), _row(lp["ssm_norm"]),
              _row(jnp.tile(lp["gla_norm"], GLA_HEADS)), _row(lp["rwkv_ln_w"]), _row(lp["rwkv_ln_b"]),
              _row(jnp.tile(lp["gdn_norm"], GDN_HEADS)),
              _block_diag_ones(RWKV_N), _block_diag_ones(LANES), _block_diag_ones(BRANCH // SSM_GROUPS),
              w_branch, w_out)
    ins = (ssm[0], ssm[1], ssm[2], ssm[3], gla[0], gla[1], gla[2], rwkv[0], rwkv[1], rwkv[2], rwkv[3],
           gdn[0], gdn[1], gdn[2], gates, x_all, gate_mod) + consts
    specs = ([half, half, half, blk(0), half, half, blk(2), half, half, half, half, half, half, blk(3),
              pl.BlockSpec((1, tm, 4 * d), tok), pl.BlockSpec((1, tm, d), tok),
              pl.BlockSpec((1, 1, 1, d), _mod_sel(n_ctx // tm))]
             + [_const_spec(c.shape) for c in consts])
    return pl.pallas_call(
        _merge_kernel,
        grid=(b, t // tm),
        in_specs=specs,
        out_specs=pl.BlockSpec((1, tm, d), tok),
        out_shape=jax.ShapeDtypeStruct((b, t, d), F32),
        compiler_params=_cparams("parallel", "parallel"),
    )(*ins)


def _route_kernel(h_ref, rw_ref, rb_ref, o_ref):
    logits = lax.dot_general(rw_ref[...], h_ref[...].astype(F32), (((1,), (1,)), ((), ())),
                             precision=HI, preferred_element_type=F32)
    scores = _sigmoid(logits)
    sel = scores + rb_ref[...]
    rows = [sel[e:e + 1, :] for e in range(N_EXPERTS)]
    sc = [scores[e:e + 1, :] for e in range(N_EXPERTS)]

    def top2(vals):
        v1, i1 = vals[0], jnp.zeros(vals[0].shape, jnp.int32)
        for j in range(1, len(vals)):
            better = vals[j] > v1
            v1 = jnp.where(better, vals[j], v1)
            i1 = jnp.where(better, j, i1)
        v2 = jnp.where(i1 == 0, vals[1], vals[0])
        i2 = jnp.where(i1 == 0, 1, 0)
        for j in range(1, len(vals)):
            better = (vals[j] > v2) & (i1 != j)
            v2 = jnp.where(better, vals[j], v2)
            i2 = jnp.where(better, j, i2)
        return v1, i1, v2, i2

    gsum = []
    for grp in range(N_GROUPS):
        v1, _, v2, _ = top2(rows[grp * EXPERTS_PER_GROUP:(grp + 1) * EXPERTS_PER_GROUP])
        gsum.append(v1 + v2)
    best, gidx = gsum[0], jnp.zeros(gsum[0].shape, jnp.int32)
    for grp in range(1, N_GROUPS):
        better = gsum[grp] > best
        best = jnp.where(better, gsum[grp], best)
        gidx = jnp.where(better, grp, gidx)
    chosen, chosen_sc = [], []
    for j in range(EXPERTS_PER_GROUP):
        cj, sj = rows[j], sc[j]
        for grp in range(1, N_GROUPS):
            cj = jnp.where(gidx == grp, rows[grp * EXPERTS_PER_GROUP + j], cj)
            sj = jnp.where(gidx == grp, sc[grp * EXPERTS_PER_GROUP + j], sj)
        chosen.append(cj)
        chosen_sc.append(sj)
    _, i1, _, i2 = top2(chosen)
    w1, w2 = jnp.zeros_like(best), jnp.zeros_like(best)
    for j in range(EXPERTS_PER_GROUP):
        w1 = jnp.where(i1 == j, chosen_sc[j], w1)
        w2 = jnp.where(i2 == j, chosen_sc[j], w2)
    tot = w1 + w2
    w1, w2 = w1 / tot, w2 / tot
    e1 = gidx * EXPERTS_PER_GROUP + i1
    e2 = gidx * EXPERTS_PER_GROUP + i2
    eid = lax.broadcasted_iota(jnp.int32, scores.shape, 0)
    o_ref[...] = jnp.where(eid == e1, w1, 0.0) + jnp.where(eid == e2, w2, 0.0)


def moe_route(h, router_w, router_b, *, tm):
    m, d = h.shape
    tm = _pick_tile(m, tm)
    return pl.pallas_call(
        _route_kernel,
        grid=(m // tm,),
        in_specs=[pl.BlockSpec((tm, d), lambda i: (i, 0)),
                  pl.BlockSpec((N_EXPERTS, d), lambda i: (0, 0)),
                  pl.BlockSpec((N_EXPERTS, 1), lambda i: (0, 0))],
        out_specs=pl.BlockSpec((N_EXPERTS, tm), lambda i: (0, i)),
        out_shape=jax.ShapeDtypeStruct((N_EXPERTS, m), F32),
        compiler_params=_cparams("parallel"),
    )(h, router_w.T, router_b.reshape(N_EXPERTS, 1))


def _expert_kernel(h_ref, g_ref, wg_ref, wu_ref, wd_ref, x_ref, m_ref, o_ref, acc_ref):
    e = pl.program_id(2)

    @pl.when(e == 0)
    def _():
        acc_ref[...] = jnp.zeros_like(acc_ref)

    h = h_ref[0]
    gates = g_ref[0]
    lane = lax.broadcasted_iota(jnp.int32, gates.shape, 1)
    ge = jnp.sum(jnp.where(lane == e, gates, 0.0), axis=1, keepdims=True)
    hid = _silu(_dot(h, wg_ref[0])) * _dot(h, wu_ref[0])
    acc_ref[...] += _dot(ge * hid, wd_ref[0])

    @pl.when(e == N_EXPERTS - 1)
    def _():
        o_ref[0] = x_ref[0] + m_ref[0, 0] * acc_ref[...]


def moe_experts(h, gates, wg, wu, wd, x_all, gate_mod, *, n_ctx):
    b, t, d = x_all.shape
    tm = _pick_tile(n_ctx, ROW_TILE)
    tok = lambda bi, i, e: (bi, i, 0)
    return pl.pallas_call(
        _expert_kernel,
        grid=(b, t // tm, N_EXPERTS),
        in_specs=[pl.BlockSpec((1, tm, d), tok),
                  pl.BlockSpec((1, tm, N_EXPERTS), tok),
                  pl.BlockSpec((1, d, EXPERT_FF), lambda bi, i, e: (e, 0, 0)),
                  pl.BlockSpec((1, d, EXPERT_FF), lambda bi, i, e: (e, 0, 0)),
                  pl.BlockSpec((1, EXPERT_FF, d), lambda bi, i, e: (e, 0, 0)),
                  pl.BlockSpec((1, tm, d), tok),
                  pl.BlockSpec((1, 1, 1, d), _mod_sel(n_ctx // tm))],
        out_specs=pl.BlockSpec((1, tm, d), tok),
        out_shape=jax.ShapeDtypeStruct((b, t, d), F32),
        scratch_shapes=[pltpu.VMEM((tm, d), F32)],
        compiler_params=_cparams("parallel", "parallel", "arbitrary"),
    )(h, gates, wg, wu, wd, x_all, gate_mod)


def _final_norm_kernel(x_ref, w_ref, o_ref):
    x = x_ref[...]
    o_ref[...] = x * lax.rsqrt(jnp.mean(x * x, axis=-1, keepdims=True) + EPS) * w_ref[...]


def final_rms_norm(x, w, *, tm):
    m, d = x.shape
    return pl.pallas_call(
        _final_norm_kernel,
        grid=(m // tm,),
        in_specs=[pl.BlockSpec((tm, d), lambda i: (i, 0)), pl.BlockSpec((1, d), lambda i: (0, 0))],
        out_specs=pl.BlockSpec((tm, d), lambda i: (i, 0)),
        out_shape=jax.ShapeDtypeStruct((m, d), F32),
        compiler_params=_cparams("parallel"),
    )(x, w.reshape(1, d))


def _pack_w_in(w_in, mixer):
    cols = _SRC_COLS[mixer]
    return jnp.where(jnp.asarray(cols >= 0)[None, :], w_in[:, np.maximum(cols, 0)], 0.0).astype(BF16)


def mixer_scans(ps, lp, *, n_ctx):
    p_ssm, p_gla, p_rwkv, p_gdn = ps
    both = lambda fn: (fn(False), fn(True))

    xs, bc, sm = ssm_prep(p_ssm, lp, n_ctx=n_ctx)
    neg_a = jnp.pad(-jnp.exp(lp["ssm_a_log"]).reshape(1, -1), ((0, 0), (0, LANES - 2 * SSM_HEADS)))
    ssm = both(lambda rev: ssd_scan(xs, bc, sm, neg_a, n_ctx=n_ctx, reverse=rev)) + (xs, p_ssm)

    def gla_dir(rev):
        d = int(rev)
        w2 = jnp.zeros((LANES, GLA_HEADS * GLA_DK), F32).at[d * GLA_RANK:(d + 1) * GLA_RANK].set(lp["gla_w2"][d])
        return gla_scan(p_gla, w2, _row(lp["gla_b"][d]), n_ctx=n_ctx, reverse=rev)

    gla = both(gla_dir) + (p_gla,)

    r, k, v, kk, a, lw, g, bonus = rwkv_prep(p_rwkv, lp, n_ctx=n_ctx)
    k_a = _row(lp["rwkv_k_a"])
    rwkv = both(lambda rev: rwkv_scan(r, k, v, kk, a, lw, k_a, n_ctx=n_ctx, reverse=rev)) + (g, bonus)

    q, kd, vd, smd = gdn_prep(p_gdn, lp, n_ctx=n_ctx)
    gdn = both(lambda rev: gdn_scan(q, kd, vd, smd, n_ctx=n_ctx, reverse=rev)) + (p_gdn,)
    return ssm, gla, rwkv, gdn


def _to_scan_order(t):
    b, n, d = t.shape
    return t.reshape(b, n // GRID_W, GRID_W, d).transpose(0, 2, 1, 3).reshape(b, n, d)


def _from_scan_order(t):
    b, n, d = t.shape
    return t.reshape(b, GRID_W, n // GRID_W, d).transpose(0, 2, 1, 3).reshape(b, n, d)


def kernel(x, c, ctx, c_ctx, ada_w, ada_b, norm_mix, norm_ffn, w_in, w_gate, w_branch, w_out, ssm_conv_w, ssm_conv_b, ssm_a_log, ssm_dt_bias, ssm_d, ssm_norm, gla_w2, gla_b, gla_norm, rwkv_mu, rwkv_w0, rwkv_w2, rwkv_a0, rwkv_a2, rwkv_g2, rwkv_k_k, rwkv_k_a, rwkv_r_k, rwkv_ln_w, rwkv_ln_b, gdn_conv_w, gdn_a_log, gdn_dt_bias, gdn_norm, router_w, router_b, moe_w_gate, moe_w_up, moe_w_down, final_norm):
    bsz, seq, d = x.shape
    n_ctx = ctx.shape[1]
    t_all = n_ctx + seq
    m_all = bsz * t_all

    cond = jnp.concatenate([jax.nn.silu(c), jax.nn.silu(c_ctx)[None]], 0)
    cond = jnp.pad(cond, ((0, SUBLANES - cond.shape[0]), (0, 0)))
    mods = []
    for l in range(DEPTH):
        mod = pmatmul(cond, ada_w[l], tm=SUBLANES, tn=1024, precise=True) + ada_b[l]
        lat = mod[:bsz].reshape(bsz, 6, d)
        cx = jnp.broadcast_to(mod[bsz].reshape(1, 6, d), (bsz, 6, d))
        mods.append(jnp.stack([cx, lat], axis=1))

    x_all = jnp.concatenate([ctx, x], axis=1)
    scan_order = False
    for l in range(DEPTH):
        if (l % 2 == 1) != scan_order:
            reorder = _from_scan_order if scan_order else _to_scan_order
            x_all = jnp.concatenate([x_all[:, :n_ctx], reorder(x_all[:, n_ctx:])], axis=1)
            scan_order = not scan_order
        lp = dict(ssm_conv_w=ssm_conv_w[l], ssm_conv_b=ssm_conv_b[l], ssm_a_log=ssm_a_log[l],
                  ssm_dt_bias=ssm_dt_bias[l], ssm_d=ssm_d[l], ssm_norm=ssm_norm[l],
                  gla_w2=gla_w2[l], gla_b=gla_b[l], gla_norm=gla_norm[l],
                  rwkv_mu=rwkv_mu[l], rwkv_w0=rwkv_w0[l], rwkv_w2=rwkv_w2[l], rwkv_a0=rwkv_a0[l],
                  rwkv_a2=rwkv_a2[l], rwkv_g2=rwkv_g2[l], rwkv_k_k=rwkv_k_k[l], rwkv_k_a=rwkv_k_a[l],
                  rwkv_r_k=rwkv_r_k[l], rwkv_ln_w=rwkv_ln_w[l], rwkv_ln_b=rwkv_ln_b[l],
                  gdn_conv_w=gdn_conv_w[l], gdn_a_log=gdn_a_log[l], gdn_dt_bias=gdn_dt_bias[l],
                  gdn_norm=gdn_norm[l])
        mod = mods[l]
        msel = lambda i: mod[:, :, i][:, :, None, :]

        h = norm_modulate(x_all, norm_mix[l], msel(0), msel(1), n_ctx=n_ctx)
        h2d = h.reshape(m_all, d)
        ps = []
        for mixer in ("ssm", "gla", "rwkv", "gdn"):
            wp = _pack_w_in(w_in[l], mixer)
            ps.append(pmatmul(h2d, wp, tm=512, tn=wp.shape[1]).reshape(bsz, t_all, wp.shape[1]))
        wg_cat = jnp.concatenate([w_gate[l, i] for i in range(4)], axis=1).astype(BF16)
        gates = pmatmul(h2d, wg_cat, tm=1024, tn=1024, act="sigmoid", out_dtype=BF16)
        gates = gates.reshape(bsz, t_all, 4 * d)

        ssm, gla, rwkv, gdn = mixer_scans(ps, lp, n_ctx=n_ctx)
        x_all = merge_residual(ssm, gla, rwkv, gdn, gates, x_all, msel(2), lp,
                               w_branch[l].astype(BF16), w_out[l].astype(BF16), n_ctx=n_ctx)

        h = norm_modulate(x_all, norm_ffn[l], msel(3), msel(4), n_ctx=n_ctx, out_dtype=F32)
        gate = moe_route(h.reshape(m_all, d), router_w, router_b, tm=1024)
        gate = gate.T.reshape(bsz, t_all, N_EXPERTS)
        x_all = moe_experts(h, gate, moe_w_gate[l].astype(BF16), moe_w_up[l].astype(BF16),
                            moe_w_down[l].astype(BF16), x_all, msel(5), n_ctx=n_ctx)

    lat = x_all[:, n_ctx:]
    if scan_order:
        lat = _from_scan_order(lat)
    return final_rms_norm(lat.reshape(bsz * seq, d), final_norm, tm=1024).reshape(bsz, seq, d)
```

```python
import functools
import itertools

import numpy as np
import jax
import jax.numpy as jnp
from jax import lax
from jax.experimental import pallas as pl
from jax.experimental.pallas import tpu as pltpu

F32 = jnp.float32
BF16 = jnp.bfloat16
HI = lax.Precision.HIGHEST

D_MODEL = 1024
DEPTH = 2
GRID_W = 64
CHUNK = 64
EPS = 1e-6
BRANCH = D_MODEL // 2
SSM_HEADS, SSM_P, SSM_GROUPS, SSM_N = 8, 64, 2, 64
GLA_HEADS, GLA_DK, GLA_DV, GLA_RANK, GLA_TAU = 4, 64, 128, 16, 16.0
RWKV_HEADS, RWKV_N, RWKV_LN_EPS = 8, 64, 64e-5
GDN_HEADS, GDN_N = 4, 128
N_EXPERTS, N_GROUPS, EXPERTS_PER_GROUP = 16, 4, 4
EXPERT_FF = D_MODEL // 2
LANES = 128
SUBLANES = 8
VMEM_LIMIT = 48 * 1024 * 1024
ROW_TILE = 256

_REF_BLOCKS = (
    ("ssm", "z", 512), ("ssm", "xbc", 768), ("ssm", "dt", 16),
    ("gla", "q", 256), ("gla", "k", 256), ("gla", "v", 512), ("gla", "r", 512), ("gla", "glr", 32),
    ("rwkv", "all", 1920),
    ("gdn", "qkv", 1536), ("gdn", "z", 512), ("gdn", "ab", 16),
)
_PACKED = {
    "ssm": (("z", 512), ("dt", 128), ("pad", 128), ("xbc", 768)),
    "gla": (("q", 256), ("k", 256), ("v", 512), ("r", 512), ("glr", 128)),
    "rwkv": (("all", 1920),),
    "gdn": (("qkv", 1536), ("z", 512), ("ab", 128)),
}


def _packed_columns():
    start, s = {}, 0
    for mixer, blk, w in _REF_BLOCKS:
        start[(mixer, blk)] = (s, w)
        s += w
    out = {}
    for mixer, blocks in _PACKED.items():
        cols = []
        for blk, wp in blocks:
            s0, w = start.get((mixer, blk), (0, 0))
            cols += list(range(s0, s0 + w)) + [-1] * (wp - w)
        out[mixer] = np.asarray(cols, np.int32)
    return out


_SRC_COLS = _packed_columns()


def _cparams(*sem):
    return pltpu.CompilerParams(dimension_semantics=sem, vmem_limit_bytes=VMEM_LIMIT)


def _dot(a, b):
    return jnp.dot(a.astype(BF16), b.astype(BF16), preferred_element_type=F32)


def _dot_nt(a, b):
    return lax.dot_general(a.astype(BF16), b.astype(BF16), (((1,), (1,)), ((), ())),
                           preferred_element_type=F32)


def _dot_tn(a, b):
    return lax.dot_general(a.astype(BF16), b.astype(BF16), (((0,), (0,)), ((), ())),
                           preferred_element_type=F32)


def _dot_hi(a, b):
    return jnp.dot(a, b, precision=HI, preferred_element_type=F32)


def _dot_x3(a, b):
    ah = a.astype(BF16)
    al = (a - ah.astype(F32)).astype(BF16)
    bh = b.astype(BF16)
    bl = (b - bh.astype(F32)).astype(BF16)
    f = lambda u, v: jnp.dot(u, v, preferred_element_type=F32)
    return f(ah, bh) + (f(ah, bl) + f(al, bh))


def _dot_x2(a, w):
    ah = a.astype(BF16)
    al = (a - ah.astype(F32)).astype(BF16)
    return jnp.dot(ah, w, preferred_element_type=F32) + jnp.dot(al, w, preferred_element_type=F32)


def _softplus(x):
    return jnp.maximum(x, 0.0) + jnp.log(1.0 + jnp.exp(-jnp.abs(x)))


def _sigmoid(x):
    return 1.0 / (1.0 + jnp.exp(-x))


def _silu(x):
    return x * _sigmoid(x)


def _pick_tile(m, pref):
    t = pref
    while m % t:
        t //= 2
    return t


def _block_diag_ones(n, width=BRANCH):
    idx = np.arange(width) // n
    return jnp.asarray(idx[:, None] == idx[None, :], BF16)


def _mm_kernel(a_ref, w_ref, o_ref, *, act, precise):
    if precise:
        r = _dot_hi(a_ref[...].astype(F32), w_ref[...].astype(F32))
    else:
        r = _dot(a_ref[...], w_ref[...])
    if act == "sigmoid":
        r = _sigmoid(r)
    o_ref[...] = r.astype(o_ref.dtype)


def pmatmul(a, w, *, tm, tn, act=None, precise=False, out_dtype=F32):
    m, k = a.shape
    n = w.shape[1]
    tm = _pick_tile(m, tm)
    assert tm % SUBLANES == 0 and n % tn == 0, (m, tm, n, tn)
    return pl.pallas_call(
        functools.partial(_mm_kernel, act=act, precise=precise),
        grid=(n // tn, m // tm),
        in_specs=[pl.BlockSpec((tm, k), lambda j, i: (i, 0)),
                  pl.BlockSpec((k, tn), lambda j, i: (0, j))],
        out_specs=pl.BlockSpec((tm, tn), lambda j, i: (i, j)),
        out_shape=jax.ShapeDtypeStruct((m, n), out_dtype),
        compiler_params=_cparams("parallel", "parallel"),
    )(a, w)


def _norm_mod_kernel(x_ref, w_ref, shift_ref, scale_ref, o_ref):
    x = x_ref[0]
    y = x * lax.rsqrt(jnp.mean(x * x, axis=-1, keepdims=True) + EPS) * w_ref[...]
    o_ref[0] = (y * (1.0 + scale_ref[0, 0]) + shift_ref[0, 0]).astype(o_ref.dtype)


def _mod_sel(nct):
    return lambda bi, i, *_: (bi, jnp.where(i < nct, 0, 1), 0, 0)


def norm_modulate(x_all, w, shift, scale, *, n_ctx, out_dtype=BF16):
    b, t, d = x_all.shape
    tm = _pick_tile(n_ctx, ROW_TILE)
    assert t % tm == 0
    tok = lambda bi, i: (bi, i, 0)
    return pl.pallas_call(
        _norm_mod_kernel,
        grid=(b, t // tm),
        in_specs=[pl.BlockSpec((1, tm, d), tok),
                  pl.BlockSpec((1, d), lambda bi, i: (0, 0)),
                  pl.BlockSpec((1, 1, 1, d), _mod_sel(n_ctx // tm)),
                  pl.BlockSpec((1, 1, 1, d), _mod_sel(n_ctx // tm))],
        out_specs=pl.BlockSpec((1, tm, d), tok),
        out_shape=jax.ShapeDtypeStruct((b, t, d), out_dtype),
        compiler_params=_cparams("parallel", "parallel"),
    )(x_all, w.reshape(1, d), shift, scale)


def _row(v):
    return v.reshape(1, -1).astype(F32)


def _const_spec(shape):
    return pl.BlockSpec(shape, lambda *_: (0,) * len(shape))


def _tile_specs(tt, width, col):
    r8 = tt // SUBLANES
    main = pl.BlockSpec((1, tt, width), lambda bi, i: (bi, i, col))
    prev = pl.BlockSpec((1, SUBLANES, width), lambda bi, i: (bi, jnp.maximum(i * r8 - 1, 0), col))
    return main, prev, r8


def _halo_specs(tt, width, col, t):
    main, prev, r8 = _tile_specs(tt, width, col)
    last8 = t // SUBLANES - 1
    nxt = pl.BlockSpec((1, SUBLANES, width), lambda bi, i: (bi, jnp.minimum((i + 1) * r8, last8), col))
    return [main, prev, nxt]


def _neighbours(x, prev8, next8, *, nct, nt):
    i = pl.program_id(1)
    tt = x.shape[0]
    row = lax.broadcasted_iota(jnp.int32, x.shape, 0)
    first = (i == 0) | (i == nct)
    last = (i == nct - 1) | (i == nt - 1)
    pr = jnp.where(first, 0.0, prev8[SUBLANES - 1:SUBLANES, :])
    nx = jnp.where(last, 0.0, next8[0:1, :])
    xp = jnp.where(row == 0, pr, pltpu.roll(x, 1, 0))
    xn = jnp.where(row == tt - 1, nx, pltpu.roll(x, tt - 1, 0))
    return xp, xn


def _prep_call(kernel, ins, in_specs, out_widths, *, b, t, tt, out_dtype=F32):
    tok = lambda bi, i: (bi, i, 0)
    return pl.pallas_call(
        kernel,
        grid=(b, t // tt),
        in_specs=in_specs,
        out_specs=[pl.BlockSpec((1, tt, w), tok) for w in out_widths],
        out_shape=[jax.ShapeDtypeStruct((b, t, w), out_dtype) for w in out_widths],
        compiler_params=_cparams("parallel", "parallel"),
    )(*ins)


def _chunk_masks(reverse):
    row = lax.broadcasted_iota(jnp.int32, (CHUNK, CHUNK), 0)
    col = lax.broadcasted_iota(jnp.int32, (CHUNK, CHUNK), 1)
    if reverse:
        return col >= row, col > row
    return col <= row, col < row


def _chunk_order(i, n_ctx_chunks, n_chunks, reverse):
    if not reverse:
        return i
    return jnp.where(i < n_ctx_chunks, n_ctx_chunks - 1 - i, n_chunks - 1 - (i - n_ctx_chunks))


def _transpose_small(x):
    row = lax.broadcasted_iota(jnp.int32, (LANES, LANES), 0)
    col = lax.broadcasted_iota(jnp.int32, (LANES, LANES), 1)
    eye = (row == col).astype(F32)
    return lax.dot_general(eye, x, (((1,), (1,)), ((), ())), precision=HI, preferred_element_type=F32)


def _unit_tri_solve(mats, rhs):
    n = range(len(mats))
    x = [rhs[h] - _dot_x3(mats[h], rhs[h]) for h in n]
    yield
    p = mats
    for _ in range(int(np.log2(CHUNK)) - 1):
        p = [_dot(p[h], p[h]) for h in n]
        yield
        x = [x[h] + _dot(p[h], x[h]) for h in n]
        yield
    return x


def _bidir_scan(body, tok_ins, const_ins, state_shape, *, b, t, n_ctx, lockstep=True):
    nc, ncc = t // CHUNK, n_ctx // CHUNK

    def chunk_spec(width, col, reverse):
        return pl.BlockSpec((1, CHUNK, width), lambda bi, i: (bi, _chunk_order(i, ncc, nc, reverse), col))

    def direction(reverse):
        d = int(reverse)
        specs = [chunk_spec(w, cols[d], reverse) for _, w, *cols in tok_ins]
        specs += [_const_spec(pair[d].shape) for pair in const_ins]
        return specs, [a for a, *_ in tok_ins] + [pair[d] for pair in const_ins]

    (spec_f, arg_f), (spec_b, arg_b) = direction(False), direction(True)
    n_in = len(arg_f)

    def kern(*refs):
        o_f, o_b, s_f, s_b = refs[2 * n_in:]

        @pl.when(pl.program_id(1) == 0)
        def _():
            s_f[...] = jnp.zeros_like(s_f)
            s_b[...] = jnp.zeros_like(s_b)

        gens = [body(*refs[:n_in], o_f, s_f, reverse=False),
                body(*refs[n_in:2 * n_in], o_b, s_b, reverse=True)]
        if not lockstep:
            gens = [itertools.chain(*gens)]
        while gens:
            gens = [g for g in gens if next(g, _DONE) is not _DONE]

    return pl.pallas_call(
        kern,
        grid=(b, nc),
        in_specs=spec_f + spec_b,
        out_specs=[chunk_spec(BRANCH, 0, False), chunk_spec(BRANCH, 0, True)],
        out_shape=[jax.ShapeDtypeStruct((b, t, BRANCH), F32)] * 2,
        scratch_shapes=[pltpu.VMEM(state_shape, F32)] * 2,
        compiler_params=_cparams("parallel", "arbitrary"),
    )(*arg_f, *arg_b)


_DONE = object()


def _ssm_prep_kernel(x_ref, xp_ref, xn_ref, dt_ref, cw_ref, cb_ref, dtb_ref, xs_ref, bc_ref, sm_ref,
                     *, nct, nt):
    x = x_ref[0]
    xp, xn = _neighbours(x, xp_ref[0], xn_ref[0], nct=nct, nt=nt)
    y = _silu(xp * cw_ref[0:1, :] + x * cw_ref[1:2, :] + xn * cw_ref[2:3, :] + cb_ref[...])
    xs_ref[0] = y[:, :BRANCH]
    bc_ref[0] = y[:, BRANCH:]
    sm_ref[0] = _softplus(dt_ref[0] + dtb_ref[...])


def ssm_prep(p, lp, *, n_ctx):
    b, t, _ = p.shape
    tt = _pick_tile(n_ctx, ROW_TILE)
    dtb = jnp.pad(lp["ssm_dt_bias"].reshape(1, -1), ((0, 0), (0, LANES - 2 * SSM_HEADS)))
    specs = _halo_specs(tt, 768, 1, t) + [pl.BlockSpec((1, tt, LANES), lambda bi, i: (bi, i, 4)),
                                          _const_spec((3, 768)), _const_spec((1, 768)), _const_spec((1, LANES))]
    kern = functools.partial(_ssm_prep_kernel, nct=n_ctx // tt, nt=t // tt)
    return _prep_call(kern, (p, p, p, p, lp["ssm_conv_w"], _row(lp["ssm_conv_b"]), dtb), specs,
                      (BRANCH, 2 * SSM_GROUPS * SSM_N, LANES), b=b, t=t, tt=tt)


def _ssd_body(x_ref, bc_ref, sm_ref, na_ref, o_ref, s_ref, *, reverse):
    incl, _ = _chunk_masks(reverse)
    last = 0 if reverse else CHUNK - 1
    off = SSM_HEADS if reverse else 0
    dt_all = sm_ref[0]
    g_all = _dot_hi(incl.astype(F32), dt_all * na_ref[...])
    yield
    gt_all = _transpose_small(g_all)
    dtt_all = _transpose_small(dt_all)
    yield
    heads = range(SSM_HEADS)
    rep = SSM_HEADS // SSM_GROUPS
    gw = SSM_GROUPS * SSM_N
    bm = [bc_ref[0, :, grp * SSM_N:(grp + 1) * SSM_N] for grp in range(SSM_GROUPS)]
    cm = [bc_ref[0, :, gw + grp * SSM_N:gw + (grp + 1) * SSM_N] for grp in range(SSM_GROUPS)]
    cb = [_dot_nt(cm[grp], bm[grp]) for grp in range(SSM_GROUPS)]
    xh = [x_ref[0, :, h * SSM_P:(h + 1) * SSM_P] for h in heads]
    s = [s_ref[h] for h in heads]
    gh = [g_all[:, off + h:off + h + 1] for h in heads]
    gl = [gh[h][last:last + 1, :] for h in heads]
    dth = [dt_all[:, off + h:off + h + 1] for h in heads]
    yield
    scores = [cb[h // rep] * jnp.exp(jnp.where(incl, gh[h] - gt_all[off + h:off + h + 1, :], -jnp.inf))
              * dtt_all[off + h:off + h + 1, :] for h in heads]
    yield
    intra = [_dot(scores[h], xh[h]) for h in heads]
    yield
    inter = [_dot(cm[h // rep] * jnp.exp(gh[h]), s[h]) for h in heads]
    yield
    upd = [_dot_tn(bm[h // rep] * (dth[h] * jnp.exp(gl[h] - gh[h])), xh[h]) for h in heads]
    yield
    for h in heads:
        o_ref[0, :, h * SSM_P:(h + 1) * SSM_P] = intra[h] + inter[h]
        s_ref[h] = s[h] * jnp.exp(gl[h]) + upd[h]


def ssd_scan(xs, bc, sm, neg_a, *, n_ctx):
    b, t, _ = xs.shape
    toks = [(xs, BRANCH, 0, 0), (bc, 2 * SSM_GROUPS * SSM_N, 0, 0), (sm, LANES, 0, 0)]
    return _bidir_scan(_ssd_body, toks, [(neg_a, neg_a)], (SSM_HEADS, SSM_N, SSM_P), b=b, t=t, n_ctx=n_ctx,
                       lockstep=False)


def _gla_body(q_ref, k_ref, v_ref, glr_ref, w2_ref, gb_ref, o_ref, s_ref, *, reverse):
    incl, _ = _chunk_masks(reverse)
    last = 0 if reverse else CHUNK - 1
    logit = _dot_x3(glr_ref[0], w2_ref[...]) + gb_ref[...]
    yield
    la = -_softplus(-logit) * (1.0 / GLA_TAU)
    g_all = _dot_hi(incl.astype(F32), la)
    yield
    heads = range(GLA_HEADS)
    ks = [slice(h * GLA_DK, (h + 1) * GLA_DK) for h in heads]
    vs = [slice(h * GLA_DV, (h + 1) * GLA_DV) for h in heads]
    g = [g_all[:, ks[h]] for h in heads]
    gl = [g[h][last:last + 1, :] for h in heads]
    k = [k_ref[0, :, ks[h]] for h in heads]
    v = [v_ref[0, :, vs[h]] for h in heads]
    qg = [q_ref[0, :, ks[h]] * (GLA_DK ** -0.5) * jnp.exp(g[h]) for h in heads]
    st = [s_ref[h] for h in heads]
    yield
    scores = [jnp.where(incl, _dot_nt(qg[h], k[h] * jnp.exp(-g[h])), 0.0) for h in heads]
    yield
    intra = [_dot(scores[h], v[h]) for h in heads]
    yield
    inter = [_dot_nt(qg[h], st[h]) for h in heads]
    yield
    upd = [_dot_tn(v[h], k[h] * jnp.exp(gl[h] - g[h])) for h in heads]
    yield
    for h in heads:
        o_ref[0, :, vs[h]] = intra[h] + inter[h]
        s_ref[h] = st[h] * jnp.exp(gl[h]) + upd[h]


def gla_scan(p, w2_pair, gb_pair, *, n_ctx):
    b, t, _ = p.shape
    kwid = GLA_HEADS * GLA_DK
    toks = [(p, kwid, 0, 0), (p, kwid, 1, 1), (p, BRANCH, 1, 1), (p, LANES, 12, 12)]
    return _bidir_scan(_gla_body, toks, [w2_pair, gb_pair], (GLA_HEADS, GLA_DV, GLA_DK), b=b, t=t, n_ctx=n_ctx)


def _rwkv_prep_kernel(x_ref, xp_ref, xn_ref, mu_ref, w2_ref, w0_ref, a2_ref, a0_ref, g2_ref, kk_ref_w,
                      ka_ref, rk_ref, bd_ref, r_ref, k_ref, v_ref, kk_ref, a_ref, lw_ref, g_ref, bo_ref,
                      *, nct, nt):
    x = x_ref[0]
    xp, xn = _neighbours(x, xp_ref[0], xn_ref[0], nct=nct, nt=nt)
    x = x + mu_ref[...] * (0.5 * (xp + xn) - x)
    r, k, v = x[:, :BRANCH], x[:, BRANCH:2 * BRANCH], x[:, 2 * BRANCH:3 * BRANCH]
    wlr = x[:, 3 * BRANCH:3 * BRANCH + LANES]
    alr = x[:, 3 * BRANCH + LANES:3 * BRANCH + 2 * LANES]
    glr = x[:, 3 * BRANCH + 2 * LANES:]
    w_raw = _dot_x3(jnp.tanh(wlr), w2_ref[...]) + w0_ref[...]
    lw_ref[0] = -jnp.exp(-_softplus(-w_raw) - 0.5)
    a = _sigmoid(_dot_x3(alr, a2_ref[...]) + a0_ref[...])
    a_ref[0] = a
    g_ref[0] = _dot_x3(_sigmoid(glr), g2_ref[...])
    kk = k * kk_ref_w[...]
    kk_ref[0] = kk * lax.rsqrt(_dot_x2(kk * kk, bd_ref[...]) + EPS)
    ksum = k * (2.0 + (a[:, :BRANCH] + a[:, BRANCH:] - 2.0) * ka_ref[...])
    bo_ref[0] = _dot_x2(r * ksum * rk_ref[...], bd_ref[...]) * v
    r_ref[0] = r
    k_ref[0] = k
    v_ref[0] = v


def rwkv_prep(p, lp, *, n_ctx):
    b, t, w = p.shape
    tt = _pick_tile(n_ctx, ROW_TILE)

    def pair(wp):
        r, c = wp.shape[1:]
        return jnp.zeros((LANES, 2 * c), F32).at[:r, :c].set(wp[0]).at[r:2 * r, c:].set(wp[1])

    consts = (_row(lp["rwkv_mu"]), pair(lp["rwkv_w2"]), _row(lp["rwkv_w0"]), pair(lp["rwkv_a2"]),
              _row(lp["rwkv_a0"]), lp["rwkv_g2"], _row(lp["rwkv_k_k"]), _row(lp["rwkv_k_a"]),
              _row(lp["rwkv_r_k"]), _block_diag_ones(RWKV_N))
    specs = _halo_specs(tt, w, 0, t) + [_const_spec(c.shape) for c in consts]
    kern = functools.partial(_rwkv_prep_kernel, nct=n_ctx // tt, nt=t // tt)
    return _prep_call(kern, (p, p, p) + consts, specs,
                      (BRANCH, BRANCH, BRANCH, BRANCH, 2 * BRANCH, 2 * BRANCH, BRANCH, BRANCH),
                      b=b, t=t, tt=tt)


def _rwkv_body(r_ref, k_ref, v_ref, kk_ref, a_ref, lw_ref, ka_ref, o_ref, s_ref, *, reverse):
    incl, strict = _chunk_masks(reverse)
    last = 0 if reverse else CHUNK - 1
    lw_all = lw_ref[0]
    g_all = _dot_hi(incl.astype(F32), lw_all)
    a_all = a_ref[0]
    k_all = k_ref[0] * (1.0 + (a_all - 1.0) * ka_ref[...])
    yield
    heads = range(RWKV_HEADS)
    hs = [slice(h * RWKV_N, (h + 1) * RWKV_N) for h in heads]
    g = [g_all[:, hs[h]] for h in heads]
    gl = [g[h][last:last + 1, :] for h in heads]
    eneg = [jnp.exp(-g[h]) for h in heads]
    edec = [jnp.exp(gl[h] - g[h]) for h in heads]
    kk = [kk_ref[0, :, hs[h]] for h in heads]
    bvec = [kk[h] * a_all[:, hs[h]] for h in heads]
    k = [k_all[:, hs[h]] for h in heads]
    v = [v_ref[0, :, hs[h]] for h in heads]
    kkg = [kk[h] * jnp.exp(g[h] - lw_all[:, hs[h]]) for h in heads]
    rg = [r_ref[0, :, hs[h]] * jnp.exp(g[h]) for h in heads]
    bh = [bvec[h] * eneg[h] for h in heads]
    kh = [k[h] * eneg[h] for h in heads]
    s = [s_ref[h] for h in heads]
    yield
    both = [jnp.concatenate([kkg[h], rg[h]], axis=0) for h in heads]
    mask2 = jnp.concatenate([strict, incl], axis=0)
    mb = [jnp.where(mask2, _dot_nt(both[h], bh[h]), 0.0) for h in heads]
    yield
    mk = [jnp.where(mask2, _dot_nt(both[h], kh[h]), 0.0) for h in heads]
    yield
    part = [_dot(mk[h], v[h]) + _dot_nt(both[h], s[h]) for h in heads]
    yield
    x = yield from _unit_tri_solve([mb[h][:CHUNK] for h in heads], [part[h][:CHUNK] for h in heads])
    u = [-xh for xh in x]
    for h in heads:
        o_ref[0, :, hs[h]] = part[h][CHUNK:] + _dot(mb[h][CHUNK:], u[h])
    yield
    for h in heads:
        upd = _dot_tn(jnp.concatenate([u[h], v[h]], axis=0),
                      jnp.concatenate([bvec[h] * edec[h], k[h] * edec[h]], axis=0))
        s_ref[h] = s[h] * jnp.exp(gl[h]) + upd


def rwkv_scan(r, k, v, kk, a, lw, k_a, *, n_ctx):
    b, t, _ = r.shape
    toks = [(r, BRANCH, 0, 0), (k, BRANCH, 0, 0), (v, BRANCH, 0, 0), (kk, BRANCH, 0, 0),
            (a, BRANCH, 0, 1), (lw, BRANCH, 0, 1)]
    return _bidir_scan(_rwkv_body, toks, [(k_a, k_a)], (RWKV_HEADS, RWKV_N, RWKV_N), b=b, t=t, n_ctx=n_ctx)


def _gdn_prep_kernel(x_ref, xp_ref, xn_ref, ab_ref, cw_ref, na_ref, dtb_ref, bd_ref,
                     q_ref, k_ref, v_ref, sm_ref, *, nct, nt):
    x = x_ref[0]
    xp, xn = _neighbours(x, xp_ref[0], xn_ref[0], nct=nct, nt=nt)
    y = _silu(xp * cw_ref[0:1, :] + x * cw_ref[1:2, :] + xn * cw_ref[2:3, :])
    q, k = y[:, :BRANCH], y[:, BRANCH:2 * BRANCH]
    q_ref[0] = q * lax.rsqrt(_dot_x2(q * q, bd_ref[...]) + EPS) * (GDN_N ** -0.5)
    k_ref[0] = k * lax.rsqrt(_dot_x2(k * k, bd_ref[...]) + EPS)
    v_ref[0] = y[:, 2 * BRANCH:]
    ab = ab_ref[0]
    lane = lax.broadcasted_iota(jnp.int32, ab.shape, 1)
    sm_ref[0] = jnp.where(lane < 2 * GDN_HEADS, na_ref[...] * _softplus(ab + dtb_ref[...]), _sigmoid(ab))


def gdn_prep(p, lp, *, n_ctx):
    b, t, _ = p.shape
    tt = _pick_tile(n_ctx, ROW_TILE)
    padrow = lambda v: jnp.pad(v.reshape(1, -1), ((0, 0), (0, LANES - 2 * GDN_HEADS)))
    consts = (lp["gdn_conv_w"], padrow(-jnp.exp(lp["gdn_a_log"])), padrow(lp["gdn_dt_bias"]),
              _block_diag_ones(GDN_N))
    specs = (_halo_specs(tt, 3 * BRANCH, 0, t) + [pl.BlockSpec((1, tt, LANES), lambda bi, i: (bi, i, 16))]
             + [_const_spec(c.shape) for c in consts])
    kern = functools.partial(_gdn_prep_kernel, nct=n_ctx // tt, nt=t // tt)
    return _prep_call(kern, (p, p, p, p) + consts, specs, (BRANCH, BRANCH, BRANCH, LANES), b=b, t=t, tt=tt)


def _gdn_body(q_ref, k_ref, v_ref, sm_ref, o_ref, s_ref, *, reverse):
    incl, strict = _chunk_masks(reverse)
    last = 0 if reverse else CHUNK - 1
    off = GDN_HEADS if reverse else 0
    sm = sm_ref[0]
    g_all = _dot_hi(incl.astype(F32), sm)
    yield
    gt_all = _transpose_small(g_all)
    yield
    heads = range(GDN_HEADS)
    hs = [slice(h * GDN_N, (h + 1) * GDN_N) for h in heads]
    g = [g_all[:, off + h:off + h + 1] for h in heads]
    gl = [g[h][last:last + 1, :] for h in heads]
    beta = [sm[:, 2 * GDN_HEADS + off + h:2 * GDN_HEADS + off + h + 1] for h in heads]
    q = [q_ref[0, :, hs[h]] for h in heads]
    k = [k_ref[0, :, hs[h]] for h in heads]
    v = [v_ref[0, :, hs[h]] for h in heads]
    s = [s_ref[h] for h in heads]
    decay = [jnp.exp(jnp.where(incl, g[h] - gt_all[off + h:off + h + 1, :], -jnp.inf)) for h in heads]
    yield
    kq = [_dot_nt(jnp.concatenate([k[h], q[h]], axis=0), k[h]) for h in heads]
    yield
    lower = [jnp.where(strict, kq[h][:CHUNK] * decay[h] * beta[h], 0.0) for h in heads]
    attn = [kq[h][CHUNK:] * decay[h] for h in heads]
    o_part = [_dot(q[h] * jnp.exp(g[h]), s[h]) for h in heads]
    yield
    rhs = [jnp.concatenate([v[h] * beta[h], k[h] * (beta[h] * jnp.exp(g[h]))], axis=1) for h in heads]
    sol = yield from _unit_tri_solve(lower, rhs)
    v_new = [sol[h][:, :GDN_N] - _dot(sol[h][:, GDN_N:], s[h]) for h in heads]
    yield
    for h in heads:
        o_ref[0, :, hs[h]] = o_part[h] + _dot(attn[h], v_new[h])
    yield
    for h in heads:
        s_ref[h] = s[h] * jnp.exp(gl[h]) + _dot_tn(k[h] * jnp.exp(gl[h] - g[h]), v_new[h])


def gdn_scan(q, k, v, sm, *, n_ctx):
    b, t, _ = q.shape
    toks = [(q, BRANCH, 0, 0), (k, BRANCH, 0, 0), (v, BRANCH, 0, 0), (sm, LANES, 0, 0)]
    return _bidir_scan(_gdn_body, toks, [], (GDN_HEADS, GDN_N, GDN_N), b=b, t=t, n_ctx=n_ctx)


def _merge_kernel(sf_ref, sb_ref, sx_ref, sz_ref, gf_ref, gb_ref, gr_ref, rf_ref, rb_ref, rg_ref, rbo_ref,
                  df_ref, db_ref, dz_ref, gate_ref, x_ref, m_ref,
                  sd_ref, sn_ref, gn_ref, lnw_ref, lnb_ref, dn_ref, bd64_ref, bd128_ref, bd256_ref,
                  wb_ref, wo_ref, o_ref):
    def group_rms(y, bd_ref, n, w_ref):
        return y * lax.rsqrt(_dot_x2(y * y, bd_ref[...]) * (1.0 / n) + EPS) * w_ref[...]

    y = (sf_ref[0] + sb_ref[0] + sd_ref[...] * sx_ref[0]) * _silu(sz_ref[0])
    ys = group_rms(y, bd256_ref, BRANCH // SSM_GROUPS, sn_ref)
    yg = group_rms(gf_ref[0] + gb_ref[0], bd128_ref, GLA_DV, gn_ref) * _silu(gr_ref[0])
    y = rf_ref[0] + rb_ref[0]
    yc = y - _dot_x2(y, bd64_ref[...]) * (1.0 / RWKV_N)
    var = _dot_x2(yc * yc, bd64_ref[...]) * (1.0 / RWKV_N)
    yr = (yc * lax.rsqrt(var + RWKV_LN_EPS) * lnw_ref[...] + lnb_ref[...] + rbo_ref[0]) * rg_ref[0]
    yd = group_rms(df_ref[0] + db_ref[0], bd128_ref, GDN_N, dn_ref) * _silu(dz_ref[0])
    acc = None
    for i, yi in enumerate((ys, yg, yr, yd)):
        term = gate_ref[0, :, i * D_MODEL:(i + 1) * D_MODEL].astype(F32) * _dot(yi, wb_ref[i])
        acc = term if acc is None else acc + term
    o_ref[0] = x_ref[0] + m_ref[0, 0] * _dot(acc, wo_ref[...])


def merge_residual(ssm, gla, rwkv, gdn, gates, x_all, gate_mod, lp, w_branch, w_out, *, n_ctx):
    b, t, d = x_all.shape
    tm = _pick_tile(n_ctx, ROW_TILE)
    tok = lambda bi, i: (bi, i, 0)
    blk = lambda c: pl.BlockSpec((1, tm, BRANCH), lambda bi, i: (bi, i, c))
    half = blk(0)
    consts = (_row(jnp.repeat(lp["ssm_d"], SSM_P)), _row(lp["ssm_norm"]),
              _row(jnp.tile(lp["gla_norm"], GLA_HEADS)), _row(lp["rwkv_ln_w"]), _row(lp["rwkv_ln_b"]),
              _row(jnp.tile(lp["gdn_norm"], GDN_HEADS)),
              _block_diag_ones(RWKV_N), _block_diag_ones(LANES), _block_diag_ones(BRANCH // SSM_GROUPS),
              w_branch, w_out)
    ins = (ssm[0], ssm[1], ssm[2], ssm[3], gla[0], gla[1], gla[2], rwkv[0], rwkv[1], rwkv[2], rwkv[3],
           gdn[0], gdn[1], gdn[2], gates, x_all, gate_mod) + consts
    specs = ([half, half, half, blk(0), half, half, blk(2), half, half, half, half, half, half, blk(3),
              pl.BlockSpec((1, tm, 4 * d), tok), pl.BlockSpec((1, tm, d), tok),
              pl.BlockSpec((1, 1, 1, d), _mod_sel(n_ctx // tm))]
             + [_const_spec(c.shape) for c in consts])
    return pl.pallas_call(
        _merge_kernel,
        grid=(b, t // tm),
        in_specs=specs,
        out_specs=pl.BlockSpec((1, tm, d), tok),
        out_shape=jax.ShapeDtypeStruct((b, t, d), F32),
        compiler_params=_cparams("parallel", "parallel"),
    )(*ins)


def _route_kernel(h_ref, rw_ref, rb_ref, o_ref):
    logits = lax.dot_general(rw_ref[...], h_ref[...].astype(F32), (((1,), (1,)), ((), ())),
                             precision=HI, preferred_element_type=F32)
    scores = _sigmoid(logits)
    sel = scores + rb_ref[...]
    rows = [sel[e:e + 1, :] for e in range(N_EXPERTS)]
    sc = [scores[e:e + 1, :] for e in range(N_EXPERTS)]

    def top2(vals):
        v1, i1 = vals[0], jnp.zeros(vals[0].shape, jnp.int32)
        for j in range(1, len(vals)):
            better = vals[j] > v1
            v1 = jnp.where(better, vals[j], v1)
            i1 = jnp.where(better, j, i1)
        v2 = jnp.where(i1 == 0, vals[1], vals[0])
        i2 = jnp.where(i1 == 0, 1, 0)
        for j in range(1, len(vals)):
            better = (vals[j] > v2) & (i1 != j)
            v2 = jnp.where(better, vals[j], v2)
            i2 = jnp.where(better, j, i2)
        return v1, i1, v2, i2

    gsum = []
    for grp in range(N_GROUPS):
        v1, _, v2, _ = top2(rows[grp * EXPERTS_PER_GROUP:(grp + 1) * EXPERTS_PER_GROUP])
        gsum.append(v1 + v2)
    best, gidx = gsum[0], jnp.zeros(gsum[0].shape, jnp.int32)
    for grp in range(1, N_GROUPS):
        better = gsum[grp] > best
        best = jnp.where(better, gsum[grp], best)
        gidx = jnp.where(better, grp, gidx)
    chosen, chosen_sc = [], []
    for j in range(EXPERTS_PER_GROUP):
        cj, sj = rows[j], sc[j]
        for grp in range(1, N_GROUPS):
            cj = jnp.where(gidx == grp, rows[grp * EXPERTS_PER_GROUP + j], cj)
            sj = jnp.where(gidx == grp, sc[grp * EXPERTS_PER_GROUP + j], sj)
        chosen.append(cj)
        chosen_sc.append(sj)
    _, i1, _, i2 = top2(chosen)
    w1, w2 = jnp.zeros_like(best), jnp.zeros_like(best)
    for j in range(EXPERTS_PER_GROUP):
        w1 = jnp.where(i1 == j, chosen_sc[j], w1)
        w2 = jnp.where(i2 == j, chosen_sc[j], w2)
    tot = w1 + w2
    w1, w2 = w1 / tot, w2 / tot
    e1 = gidx * EXPERTS_PER_GROUP + i1
    e2 = gidx * EXPERTS_PER_GROUP + i2
    eid = lax.broadcasted_iota(jnp.int32, scores.shape, 0)
    o_ref[...] = jnp.where(eid == e1, w1, 0.0) + jnp.where(eid == e2, w2, 0.0)


def moe_route(h, router_w, router_b, *, tm):
    m, d = h.shape
    tm = _pick_tile(m, tm)
    return pl.pallas_call(
        _route_kernel,
        grid=(m // tm,),
        in_specs=[pl.BlockSpec((tm, d), lambda i: (i, 0)),
                  pl.BlockSpec((N_EXPERTS, d), lambda i: (0, 0)),
                  pl.BlockSpec((N_EXPERTS, 1), lambda i: (0, 0))],
        out_specs=pl.BlockSpec((N_EXPERTS, tm), lambda i: (0, i)),
        out_shape=jax.ShapeDtypeStruct((N_EXPERTS, m), F32),
        compiler_params=_cparams("parallel"),
    )(h, router_w.T, router_b.reshape(N_EXPERTS, 1))


def _expert_kernel(h_ref, g_ref, wg_ref, wu_ref, wd_ref, x_ref, m_ref, o_ref, acc_ref, hb_ref,
                   *, bsz, t_all, n_ctx):
    e = pl.program_id(1)

    @pl.when(e == 0)
    def _():
        acc_ref[...] = jnp.zeros_like(acc_ref)
        hb_ref[...] = h_ref[...].astype(BF16)

    h = hb_ref[...]
    gates = g_ref[...]
    lane = lax.broadcasted_iota(jnp.int32, gates.shape, 1)
    ge = jnp.sum(jnp.where(lane == e, gates, 0.0), axis=1, keepdims=True)
    hid = _silu(_dot(h, wg_ref[0])) * _dot(h, wu_ref[0])
    acc_ref[...] += _dot(ge * hid, wd_ref[0])

    @pl.when(e == N_EXPERTS - 1)
    def _():
        tm = acc_ref.shape[0]
        row = pl.program_id(0) * tm + lax.broadcasted_iota(jnp.int32, (tm, 1), 0)
        mod = jnp.zeros(acc_ref.shape, F32)
        ctx = jnp.zeros((tm, 1), jnp.bool_)
        for bi in range(bsz):
            lo = bi * t_all
            ctx = ctx | ((row >= lo) & (row < lo + n_ctx))
            lat = (row >= lo + n_ctx) & (row < lo + t_all)
            mod = mod + jnp.where(lat, m_ref[bi:bi + 1, :], 0.0)
        mod = mod + jnp.where(ctx, m_ref[bsz:bsz + 1, :], 0.0)
        o_ref[...] = x_ref[...] + mod * acc_ref[...]


def moe_experts(h, gates, wg, wu, wd, x_all, gate_rows, *, n_ctx, tm=1024):
    b, t, d = x_all.shape
    m = b * t
    tm = _pick_tile(m, tm)
    tok = lambda i, e: (i, 0)
    kern = functools.partial(_expert_kernel, bsz=b, t_all=t, n_ctx=n_ctx)
    out = pl.pallas_call(
        kern,
        grid=(m // tm, N_EXPERTS),
        in_specs=[pl.BlockSpec((tm, d), tok),
                  pl.BlockSpec((tm, N_EXPERTS), tok),
                  pl.BlockSpec((1, d, EXPERT_FF), lambda i, e: (e, 0, 0)),
                  pl.BlockSpec((1, d, EXPERT_FF), lambda i, e: (e, 0, 0)),
                  pl.BlockSpec((1, EXPERT_FF, d), lambda i, e: (e, 0, 0)),
                  pl.BlockSpec((tm, d), tok),
                  _const_spec(gate_rows.shape)],
        out_specs=pl.BlockSpec((tm, d), tok),
        out_shape=jax.ShapeDtypeStruct((m, d), F32),
        scratch_shapes=[pltpu.VMEM((tm, d), F32), pltpu.VMEM((tm, d), BF16)],
        compiler_params=_cparams("parallel", "arbitrary"),
    )(h.reshape(m, d), gates, wg, wu, wd, x_all.reshape(m, d), gate_rows)
    return out.reshape(b, t, d)


def _final_norm_kernel(x_ref, w_ref, o_ref):
    x = x_ref[...]
    o_ref[...] = x * lax.rsqrt(jnp.mean(x * x, axis=-1, keepdims=True) + EPS) * w_ref[...]


def final_rms_norm(x, w, *, tm):
    m, d = x.shape
    return pl.pallas_call(
        _final_norm_kernel,
        grid=(m // tm,),
        in_specs=[pl.BlockSpec((tm, d), lambda i: (i, 0)), pl.BlockSpec((1, d), lambda i: (0, 0))],
        out_specs=pl.BlockSpec((tm, d), lambda i: (i, 0)),
        out_shape=jax.ShapeDtypeStruct((m, d), F32),
        compiler_params=_cparams("parallel"),
    )(x, w.reshape(1, d))


def _pack_w_in(w_in, mixer):
    cols = _SRC_COLS[mixer]
    return jnp.where(jnp.asarray(cols >= 0)[None, :], w_in[:, np.maximum(cols, 0)], 0.0).astype(BF16)


def mixer_scans(ps, lp, *, n_ctx):
    p_ssm, p_gla, p_rwkv, p_gdn = ps

    xs, bc, sm = ssm_prep(p_ssm, lp, n_ctx=n_ctx)
    neg_a = jnp.pad(-jnp.exp(lp["ssm_a_log"]).reshape(1, -1), ((0, 0), (0, LANES - 2 * SSM_HEADS)))
    ssm = tuple(ssd_scan(xs, bc, sm, neg_a, n_ctx=n_ctx)) + (xs, p_ssm)

    w2 = [jnp.zeros((LANES, GLA_HEADS * GLA_DK), F32).at[d * GLA_RANK:(d + 1) * GLA_RANK].set(lp["gla_w2"][d])
          for d in range(2)]
    gb = [_row(lp["gla_b"][d]) for d in range(2)]
    gla = tuple(gla_scan(p_gla, w2, gb, n_ctx=n_ctx)) + (p_gla,)

    r, k, v, kk, a, lw, g, bonus = rwkv_prep(p_rwkv, lp, n_ctx=n_ctx)
    rwkv = tuple(rwkv_scan(r, k, v, kk, a, lw, _row(lp["rwkv_k_a"]), n_ctx=n_ctx)) + (g, bonus)

    q, kd, vd, smd = gdn_prep(p_gdn, lp, n_ctx=n_ctx)
    gdn = tuple(gdn_scan(q, kd, vd, smd, n_ctx=n_ctx)) + (p_gdn,)
    return ssm, gla, rwkv, gdn


def _to_scan_order(t):
    b, n, d = t.shape
    return t.reshape(b, n // GRID_W, GRID_W, d).transpose(0, 2, 1, 3).reshape(b, n, d)


def _from_scan_order(t):
    b, n, d = t.shape
    return t.reshape(b, GRID_W, n // GRID_W, d).transpose(0, 2, 1, 3).reshape(b, n, d)


def kernel(x, c, ctx, c_ctx, ada_w, ada_b, norm_mix, norm_ffn, w_in, w_gate, w_branch, w_out, ssm_conv_w, ssm_conv_b, ssm_a_log, ssm_dt_bias, ssm_d, ssm_norm, gla_w2, gla_b, gla_norm, rwkv_mu, rwkv_w0, rwkv_w2, rwkv_a0, rwkv_a2, rwkv_g2, rwkv_k_k, rwkv_k_a, rwkv_r_k, rwkv_ln_w, rwkv_ln_b, gdn_conv_w, gdn_a_log, gdn_dt_bias, gdn_norm, router_w, router_b, moe_w_gate, moe_w_up, moe_w_down, final_norm):
    bsz, seq, d = x.shape
    n_ctx = ctx.shape[1]
    t_all = n_ctx + seq
    m_all = bsz * t_all

    cond = jnp.concatenate([jax.nn.silu(c), jax.nn.silu(c_ctx)[None]], 0)
    cond = jnp.pad(cond, ((0, SUBLANES - cond.shape[0]), (0, 0)))
    mods, mod_rows = [], []
    for l in range(DEPTH):
        mod = pmatmul(cond, ada_w[l], tm=SUBLANES, tn=1024, precise=True) + ada_b[l]
        mod_rows.append(mod)
        lat = mod[:bsz].reshape(bsz, 6, d)
        cx = jnp.broadcast_to(mod[bsz].reshape(1, 6, d), (bsz, 6, d))
        mods.append(jnp.stack([cx, lat], axis=1))

    x_all = jnp.concatenate([ctx, x], axis=1)
    scan_order = False
    for l in range(DEPTH):
        if (l % 2 == 1) != scan_order:
            reorder = _from_scan_order if scan_order else _to_scan_order
            x_all = jnp.concatenate([x_all[:, :n_ctx], reorder(x_all[:, n_ctx:])], axis=1)
            scan_order = not scan_order
        lp = dict(ssm_conv_w=ssm_conv_w[l], ssm_conv_b=ssm_conv_b[l], ssm_a_log=ssm_a_log[l],
                  ssm_dt_bias=ssm_dt_bias[l], ssm_d=ssm_d[l], ssm_norm=ssm_norm[l],
                  gla_w2=gla_w2[l], gla_b=gla_b[l], gla_norm=gla_norm[l],
                  rwkv_mu=rwkv_mu[l], rwkv_w0=rwkv_w0[l], rwkv_w2=rwkv_w2[l], rwkv_a0=rwkv_a0[l],
                  rwkv_a2=rwkv_a2[l], rwkv_g2=rwkv_g2[l], rwkv_k_k=rwkv_k_k[l], rwkv_k_a=rwkv_k_a[l],
                  rwkv_r_k=rwkv_r_k[l], rwkv_ln_w=rwkv_ln_w[l], rwkv_ln_b=rwkv_ln_b[l],
                  gdn_conv_w=gdn_conv_w[l], gdn_a_log=gdn_a_log[l], gdn_dt_bias=gdn_dt_bias[l],
                  gdn_norm=gdn_norm[l])
        mod = mods[l]
        msel = lambda i: mod[:, :, i][:, :, None, :]

        h = norm_modulate(x_all, norm_mix[l], msel(0), msel(1), n_ctx=n_ctx)
        h2d = h.reshape(m_all, d)
        ps = []
        for mixer in ("ssm", "gla", "rwkv", "gdn"):
            wp = _pack_w_in(w_in[l], mixer)
            ps.append(pmatmul(h2d, wp, tm=512, tn=wp.shape[1]).reshape(bsz, t_all, wp.shape[1]))
        wg_cat = jnp.concatenate([w_gate[l, i] for i in range(4)], axis=1).astype(BF16)
        gates = pmatmul(h2d, wg_cat, tm=1024, tn=1024, act="sigmoid", out_dtype=BF16)
        gates = gates.reshape(bsz, t_all, 4 * d)

        ssm, gla, rwkv, gdn = mixer_scans(ps, lp, n_ctx=n_ctx)
        x_all = merge_residual(ssm, gla, rwkv, gdn, gates, x_all, msel(2), lp,
                               w_branch[l].astype(BF16), w_out[l].astype(BF16), n_ctx=n_ctx)

        h = norm_modulate(x_all, norm_ffn[l], msel(3), msel(4), n_ctx=n_ctx, out_dtype=F32)
        gate = moe_route(h.reshape(m_all, d), router_w, router_b, tm=1024)
        x_all = moe_experts(h, gate.T, moe_w_gate[l].astype(BF16), moe_w_up[l].astype(BF16),
                            moe_w_down[l].astype(BF16), x_all, mod_rows[l][:, 5 * d:], n_ctx=n_ctx)

    lat = x_all[:, n_ctx:]
    if scan_order:
        lat = _from_scan_order(lat)
    return final_rms_norm(lat.reshape(bsz * seq, d), final_norm, tm=1024).reshape(bsz, seq, d)
```

```python
import functools
import itertools

import numpy as np
import jax
import jax.numpy as jnp
from jax import lax
from jax.experimental import pallas as pl
from jax.experimental.pallas import tpu as pltpu

F32 = jnp.float32
BF16 = jnp.bfloat16
HI = lax.Precision.HIGHEST

D_MODEL = 1024
DEPTH = 2
GRID_W = 64
CHUNK = 64
EPS = 1e-6
BRANCH = D_MODEL // 2
SSM_HEADS, SSM_P, SSM_GROUPS, SSM_N = 8, 64, 2, 64
GLA_HEADS, GLA_DK, GLA_DV, GLA_RANK, GLA_TAU = 4, 64, 128, 16, 16.0
RWKV_HEADS, RWKV_N, RWKV_LN_EPS = 8, 64, 64e-5
GDN_HEADS, GDN_N = 4, 128
N_EXPERTS, N_GROUPS, EXPERTS_PER_GROUP = 16, 4, 4
EXPERT_FF = D_MODEL // 2
LANES = 128
SUBLANES = 8
VMEM_LIMIT = 48 * 1024 * 1024
ROW_TILE = 256

_REF_BLOCKS = (
    ("ssm", "z", 512), ("ssm", "xbc", 768), ("ssm", "dt", 16),
    ("gla", "q", 256), ("gla", "k", 256), ("gla", "v", 512), ("gla", "r", 512), ("gla", "glr", 32),
    ("rwkv", "all", 1920),
    ("gdn", "qkv", 1536), ("gdn", "z", 512), ("gdn", "ab", 16),
)
_PACKED = {
    "ssm": (("z", 512), ("dt", 128), ("pad", 128), ("xbc", 768)),
    "gla": (("q", 256), ("k", 256), ("v", 512), ("r", 512), ("glr", 128)),
    "rwkv": (("all", 1920),),
    "gdn": (("qkv", 1536), ("z", 512), ("ab", 128)),
}


def _packed_columns():
    start, s = {}, 0
    for mixer, blk, w in _REF_BLOCKS:
        start[(mixer, blk)] = (s, w)
        s += w
    out = {}
    for mixer, blocks in _PACKED.items():
        cols = []
        for blk, wp in blocks:
            s0, w = start.get((mixer, blk), (0, 0))
            cols += list(range(s0, s0 + w)) + [-1] * (wp - w)
        out[mixer] = np.asarray(cols, np.int32)
    return out


_SRC_COLS = _packed_columns()


def _cparams(*sem):
    return pltpu.CompilerParams(dimension_semantics=sem, vmem_limit_bytes=VMEM_LIMIT)


def _dot(a, b):
    return jnp.dot(a.astype(BF16), b.astype(BF16), preferred_element_type=F32)


def _dot_nt(a, b):
    return lax.dot_general(a.astype(BF16), b.astype(BF16), (((1,), (1,)), ((), ())),
                           preferred_element_type=F32)


def _dot_tn(a, b):
    return lax.dot_general(a.astype(BF16), b.astype(BF16), (((0,), (0,)), ((), ())),
                           preferred_element_type=F32)


def _dot_hi(a, b):
    return jnp.dot(a, b, precision=HI, preferred_element_type=F32)


def _dot_x3(a, b):
    ah = a.astype(BF16)
    al = (a - ah.astype(F32)).astype(BF16)
    bh = b.astype(BF16)
    bl = (b - bh.astype(F32)).astype(BF16)
    f = lambda u, v: jnp.dot(u, v, preferred_element_type=F32)
    return f(ah, bh) + (f(ah, bl) + f(al, bh))


def _dot_x2(a, w):
    ah = a.astype(BF16)
    al = (a - ah.astype(F32)).astype(BF16)
    return jnp.dot(ah, w, preferred_element_type=F32) + jnp.dot(al, w, preferred_element_type=F32)


def _softplus(x):
    return jnp.maximum(x, 0.0) + jnp.log(1.0 + jnp.exp(-jnp.abs(x)))


def _sigmoid(x):
    return 1.0 / (1.0 + jnp.exp(-x))


def _silu(x):
    return x * _sigmoid(x)


def _pick_tile(m, pref):
    t = pref
    while m % t:
        t //= 2
    return t


def _block_diag_ones(n, width=BRANCH):
    idx = np.arange(width) // n
    return jnp.asarray(idx[:, None] == idx[None, :], BF16)


def _mm_kernel(a_ref, w_ref, o_ref, *, act, precise):
    if precise:
        r = _dot_hi(a_ref[...].astype(F32), w_ref[...].astype(F32))
    else:
        r = _dot(a_ref[...], w_ref[...])
    if act == "sigmoid":
        r = _sigmoid(r)
    o_ref[...] = r.astype(o_ref.dtype)


def pmatmul(a, w, *, tm, tn, act=None, precise=False, out_dtype=F32):
    m, k = a.shape
    n = w.shape[1]
    tm = _pick_tile(m, tm)
    assert tm % SUBLANES == 0 and n % tn == 0, (m, tm, n, tn)
    return pl.pallas_call(
        functools.partial(_mm_kernel, act=act, precise=precise),
        grid=(n // tn, m // tm),
        in_specs=[pl.BlockSpec((tm, k), lambda j, i: (i, 0)),
                  pl.BlockSpec((k, tn), lambda j, i: (0, j))],
        out_specs=pl.BlockSpec((tm, tn), lambda j, i: (i, j)),
        out_shape=jax.ShapeDtypeStruct((m, n), out_dtype),
        compiler_params=_cparams("parallel", "parallel"),
    )(a, w)


def _norm_mod_kernel(x_ref, w_ref, shift_ref, scale_ref, o_ref):
    x = x_ref[0]
    y = x * lax.rsqrt(jnp.mean(x * x, axis=-1, keepdims=True) + EPS) * w_ref[...]
    o_ref[0] = (y * (1.0 + scale_ref[0, 0]) + shift_ref[0, 0]).astype(o_ref.dtype)


def _mod_sel(nct):
    return lambda bi, i, *_: (bi, jnp.where(i < nct, 0, 1), 0, 0)


def norm_modulate(x_all, w, shift, scale, *, n_ctx, out_dtype=BF16):
    b, t, d = x_all.shape
    tm = _pick_tile(n_ctx, ROW_TILE)
    assert t % tm == 0
    tok = lambda bi, i: (bi, i, 0)
    return pl.pallas_call(
        _norm_mod_kernel,
        grid=(b, t // tm),
        in_specs=[pl.BlockSpec((1, tm, d), tok),
                  pl.BlockSpec((1, d), lambda bi, i: (0, 0)),
                  pl.BlockSpec((1, 1, 1, d), _mod_sel(n_ctx // tm)),
                  pl.BlockSpec((1, 1, 1, d), _mod_sel(n_ctx // tm))],
        out_specs=pl.BlockSpec((1, tm, d), tok),
        out_shape=jax.ShapeDtypeStruct((b, t, d), out_dtype),
        compiler_params=_cparams("parallel", "parallel"),
    )(x_all, w.reshape(1, d), shift, scale)


def _row(v):
    return v.reshape(1, -1).astype(F32)


def _const_spec(shape):
    return pl.BlockSpec(shape, lambda *_: (0,) * len(shape))


def _tile_specs(tt, width, col):
    r8 = tt // SUBLANES
    main = pl.BlockSpec((1, tt, width), lambda bi, i: (bi, i, col))
    prev = pl.BlockSpec((1, SUBLANES, width), lambda bi, i: (bi, jnp.maximum(i * r8 - 1, 0), col))
    return main, prev, r8


def _halo_specs(tt, width, col, t):
    main, prev, r8 = _tile_specs(tt, width, col)
    last8 = t // SUBLANES - 1
    nxt = pl.BlockSpec((1, SUBLANES, width), lambda bi, i: (bi, jnp.minimum((i + 1) * r8, last8), col))
    return [main, prev, nxt]


def _neighbours(x, prev8, next8, *, nct, nt):
    i = pl.program_id(1)
    tt = x.shape[0]
    row = lax.broadcasted_iota(jnp.int32, x.shape, 0)
    first = (i == 0) | (i == nct)
    last = (i == nct - 1) | (i == nt - 1)
    pr = jnp.where(first, 0.0, prev8[SUBLANES - 1:SUBLANES, :])
    nx = jnp.where(last, 0.0, next8[0:1, :])
    xp = jnp.where(row == 0, pr, pltpu.roll(x, 1, 0))
    xn = jnp.where(row == tt - 1, nx, pltpu.roll(x, tt - 1, 0))
    return xp, xn


def _prep_call(kernel, ins, in_specs, out_widths, *, b, t, tt, out_dtype=F32):
    tok = lambda bi, i: (bi, i, 0)
    return pl.pallas_call(
        kernel,
        grid=(b, t // tt),
        in_specs=in_specs,
        out_specs=[pl.BlockSpec((1, tt, w), tok) for w in out_widths],
        out_shape=[jax.ShapeDtypeStruct((b, t, w), out_dtype) for w in out_widths],
        compiler_params=_cparams("parallel", "parallel"),
    )(*ins)


def _chunk_masks(reverse):
    row = lax.broadcasted_iota(jnp.int32, (CHUNK, CHUNK), 0)
    col = lax.broadcasted_iota(jnp.int32, (CHUNK, CHUNK), 1)
    if reverse:
        return col >= row, col > row
    return col <= row, col < row


def _chunk_order(i, n_ctx_chunks, n_chunks, reverse):
    if not reverse:
        return i
    return jnp.where(i < n_ctx_chunks, n_ctx_chunks - 1 - i, n_chunks - 1 - (i - n_ctx_chunks))


def _transpose_small(x):
    row = lax.broadcasted_iota(jnp.int32, (LANES, LANES), 0)
    col = lax.broadcasted_iota(jnp.int32, (LANES, LANES), 1)
    eye = (row == col).astype(F32)
    return lax.dot_general(eye, x, (((1,), (1,)), ((), ())), precision=HI, preferred_element_type=F32)


def _unit_tri_solve(mats, rhs):
    n = range(len(mats))
    x = [rhs[h] - _dot_x3(mats[h], rhs[h]) for h in n]
    yield
    p = mats
    for _ in range(int(np.log2(CHUNK)) - 1):
        p = [_dot(p[h], p[h]) for h in n]
        yield
        x = [x[h] + _dot(p[h], x[h]) for h in n]
        yield
    return x


def _bidir_scan(body, tok_ins, const_ins, state_shape, *, b, t, n_ctx, lockstep=True, batch_block=1):
    nc, ncc = t // CHUNK, n_ctx // CHUNK
    nb = batch_block
    assert b % nb == 0

    def chunk_spec(width, col, reverse):
        return pl.BlockSpec((nb, CHUNK, width), lambda bi, i: (bi, _chunk_order(i, ncc, nc, reverse), col))

    def direction(reverse):
        d = int(reverse)
        specs = [chunk_spec(w, cols[d], reverse) for _, w, *cols in tok_ins]
        specs += [_const_spec(pair[d].shape) for pair in const_ins]
        return specs, [a for a, *_ in tok_ins] + [pair[d] for pair in const_ins]

    (spec_f, arg_f), (spec_b, arg_b) = direction(False), direction(True)
    n_tok, n_in = len(tok_ins), len(arg_f)

    def kern(*refs):
        o_f, o_b, s_f, s_b = refs[2 * n_in:]

        @pl.when(pl.program_id(1) == 0)
        def _():
            s_f[...] = jnp.zeros_like(s_f)
            s_b[...] = jnp.zeros_like(s_b)

        def one(j, ins, o_ref, s_ref, reverse):
            ins = [r.at[pl.ds(j, 1)] if k < n_tok else r for k, r in enumerate(ins)]
            return body(*ins, o_ref.at[pl.ds(j, 1)], s_ref.at[j], reverse=reverse)

        gens = []
        for j in range(nb):
            gens += [one(j, refs[:n_in], o_f, s_f, False), one(j, refs[n_in:2 * n_in], o_b, s_b, True)]
        if not lockstep:
            gens = [itertools.chain(*gens)]
        while gens:
            gens = [g for g in gens if next(g, _DONE) is not _DONE]

    return pl.pallas_call(
        kern,
        grid=(b // nb, nc),
        in_specs=spec_f + spec_b,
        out_specs=[chunk_spec(BRANCH, 0, False), chunk_spec(BRANCH, 0, True)],
        out_shape=[jax.ShapeDtypeStruct((b, t, BRANCH), F32)] * 2,
        scratch_shapes=[pltpu.VMEM((nb,) + tuple(state_shape), F32)] * 2,
        compiler_params=_cparams("parallel", "arbitrary"),
    )(*arg_f, *arg_b)


_DONE = object()


def _batch_block(b, pref):
    return pref if b % pref == 0 else 1


def _ssm_prep_kernel(x_ref, xp_ref, xn_ref, dt_ref, cw_ref, cb_ref, dtb_ref, xs_ref, bc_ref, sm_ref,
                     *, nct, nt):
    x = x_ref[0]
    xp, xn = _neighbours(x, xp_ref[0], xn_ref[0], nct=nct, nt=nt)
    y = _silu(xp * cw_ref[0:1, :] + x * cw_ref[1:2, :] + xn * cw_ref[2:3, :] + cb_ref[...])
    xs_ref[0] = y[:, :BRANCH]
    bc_ref[0] = y[:, BRANCH:]
    sm_ref[0] = _softplus(dt_ref[0] + dtb_ref[...])


def ssm_prep(p, lp, *, n_ctx):
    b, t, _ = p.shape
    tt = _pick_tile(n_ctx, ROW_TILE)
    dtb = jnp.pad(lp["ssm_dt_bias"].reshape(1, -1), ((0, 0), (0, LANES - 2 * SSM_HEADS)))
    specs = _halo_specs(tt, 768, 1, t) + [pl.BlockSpec((1, tt, LANES), lambda bi, i: (bi, i, 4)),
                                          _const_spec((3, 768)), _const_spec((1, 768)), _const_spec((1, LANES))]
    kern = functools.partial(_ssm_prep_kernel, nct=n_ctx // tt, nt=t // tt)
    return _prep_call(kern, (p, p, p, p, lp["ssm_conv_w"], _row(lp["ssm_conv_b"]), dtb), specs,
                      (BRANCH, 2 * SSM_GROUPS * SSM_N, LANES), b=b, t=t, tt=tt)


def _ssd_body(x_ref, bc_ref, sm_ref, na_ref, o_ref, s_ref, *, reverse):
    incl, _ = _chunk_masks(reverse)
    last = 0 if reverse else CHUNK - 1
    off = SSM_HEADS if reverse else 0
    dt_all = sm_ref[0]
    g_all = _dot_hi(incl.astype(F32), dt_all * na_ref[...])
    yield
    gt_all = _transpose_small(g_all)
    dtt_all = _transpose_small(dt_all)
    yield
    heads = range(SSM_HEADS)
    rep = SSM_HEADS // SSM_GROUPS
    gw = SSM_GROUPS * SSM_N
    bm = [bc_ref[0, :, grp * SSM_N:(grp + 1) * SSM_N] for grp in range(SSM_GROUPS)]
    cm = [bc_ref[0, :, gw + grp * SSM_N:gw + (grp + 1) * SSM_N] for grp in range(SSM_GROUPS)]
    cb = [_dot_nt(cm[grp], bm[grp]) for grp in range(SSM_GROUPS)]
    xh = [x_ref[0, :, h * SSM_P:(h + 1) * SSM_P] for h in heads]
    s = [s_ref[h] for h in heads]
    gh = [g_all[:, off + h:off + h + 1] for h in heads]
    gl = [gh[h][last:last + 1, :] for h in heads]
    dth = [dt_all[:, off + h:off + h + 1] for h in heads]
    yield
    scores = [cb[h // rep] * jnp.exp(jnp.where(incl, gh[h] - gt_all[off + h:off + h + 1, :], -jnp.inf))
              * dtt_all[off + h:off + h + 1, :] for h in heads]
    yield
    intra = [_dot(scores[h], xh[h]) for h in heads]
    yield
    inter = [_dot(cm[h // rep] * jnp.exp(gh[h]), s[h]) for h in heads]
    yield
    upd = [_dot_tn(bm[h // rep] * (dth[h] * jnp.exp(gl[h] - gh[h])), xh[h]) for h in heads]
    yield
    for h in heads:
        o_ref[0, :, h * SSM_P:(h + 1) * SSM_P] = intra[h] + inter[h]
        s_ref[h] = s[h] * jnp.exp(gl[h]) + upd[h]


def ssd_scan(xs, bc, sm, neg_a, *, n_ctx):
    b, t, _ = xs.shape
    toks = [(xs, BRANCH, 0, 0), (bc, 2 * SSM_GROUPS * SSM_N, 0, 0), (sm, LANES, 0, 0)]
    return _bidir_scan(_ssd_body, toks, [(neg_a, neg_a)], (SSM_HEADS, SSM_N, SSM_P), b=b, t=t, n_ctx=n_ctx,
                       lockstep=False, batch_block=_batch_block(b, 2))


def _gla_body(q_ref, k_ref, v_ref, glr_ref, w2_ref, gb_ref, o_ref, s_ref, *, reverse):
    incl, _ = _chunk_masks(reverse)
    last = 0 if reverse else CHUNK - 1
    logit = _dot_x3(glr_ref[0], w2_ref[...]) + gb_ref[...]
    yield
    la = -_softplus(-logit) * (1.0 / GLA_TAU)
    g_all = _dot_hi(incl.astype(F32), la)
    yield
    heads = range(GLA_HEADS)
    ks = [slice(h * GLA_DK, (h + 1) * GLA_DK) for h in heads]
    vs = [slice(h * GLA_DV, (h + 1) * GLA_DV) for h in heads]
    g = [g_all[:, ks[h]] for h in heads]
    gl = [g[h][last:last + 1, :] for h in heads]
    k = [k_ref[0, :, ks[h]] for h in heads]
    v = [v_ref[0, :, vs[h]] for h in heads]
    qg = [q_ref[0, :, ks[h]] * (GLA_DK ** -0.5) * jnp.exp(g[h]) for h in heads]
    st = [s_ref[h] for h in heads]
    yield
    scores = [jnp.where(incl, _dot_nt(qg[h], k[h] * jnp.exp(-g[h])), 0.0) for h in heads]
    yield
    intra = [_dot(scores[h], v[h]) for h in heads]
    yield
    inter = [_dot_nt(qg[h], st[h]) for h in heads]
    yield
    upd = [_dot_tn(v[h], k[h] * jnp.exp(gl[h] - g[h])) for h in heads]
    yield
    for h in heads:
        o_ref[0, :, vs[h]] = intra[h] + inter[h]
        s_ref[h] = st[h] * jnp.exp(gl[h]) + upd[h]


def gla_scan(p, w2_pair, gb_pair, *, n_ctx):
    b, t, _ = p.shape
    kwid = GLA_HEADS * GLA_DK
    toks = [(p, kwid, 0, 0), (p, kwid, 1, 1), (p, BRANCH, 1, 1), (p, LANES, 12, 12)]
    return _bidir_scan(_gla_body, toks, [w2_pair, gb_pair], (GLA_HEADS, GLA_DV, GLA_DK), b=b, t=t, n_ctx=n_ctx,
                       batch_block=_batch_block(b, 4))


def _rwkv_prep_kernel(x_ref, xp_ref, xn_ref, mu_ref, w2_ref, w0_ref, a2_ref, a0_ref, g2_ref, kk_ref_w,
                      ka_ref, rk_ref, bd_ref, r_ref, k_ref, v_ref, kk_ref, a_ref, lw_ref, g_ref, bo_ref,
                      *, nct, nt):
    x = x_ref[0]
    xp, xn = _neighbours(x, xp_ref[0], xn_ref[0], nct=nct, nt=nt)
    x = x + mu_ref[...] * (0.5 * (xp + xn) - x)
    r, k, v = x[:, :BRANCH], x[:, BRANCH:2 * BRANCH], x[:, 2 * BRANCH:3 * BRANCH]
    wlr = x[:, 3 * BRANCH:3 * BRANCH + LANES]
    alr = x[:, 3 * BRANCH + LANES:3 * BRANCH + 2 * LANES]
    glr = x[:, 3 * BRANCH + 2 * LANES:]
    w_raw = _dot_x3(jnp.tanh(wlr), w2_ref[...]) + w0_ref[...]
    lw_ref[0] = -jnp.exp(-_softplus(-w_raw) - 0.5)
    a = _sigmoid(_dot_x3(alr, a2_ref[...]) + a0_ref[...])
    a_ref[0] = a
    g_ref[0] = _dot_x3(_sigmoid(glr), g2_ref[...])
    kk = k * kk_ref_w[...]
    kk_ref[0] = kk * lax.rsqrt(_dot_x2(kk * kk, bd_ref[...]) + EPS)
    ksum = k * (2.0 + (a[:, :BRANCH] + a[:, BRANCH:] - 2.0) * ka_ref[...])
    bo_ref[0] = _dot_x2(r * ksum * rk_ref[...], bd_ref[...]) * v
    r_ref[0] = r
    k_ref[0] = k
    v_ref[0] = v


def rwkv_prep(p, lp, *, n_ctx):
    b, t, w = p.shape
    tt = _pick_tile(n_ctx, ROW_TILE)

    def pair(wp):
        r, c = wp.shape[1:]
        return jnp.zeros((LANES, 2 * c), F32).at[:r, :c].set(wp[0]).at[r:2 * r, c:].set(wp[1])

    consts = (_row(lp["rwkv_mu"]), pair(lp["rwkv_w2"]), _row(lp["rwkv_w0"]), pair(lp["rwkv_a2"]),
              _row(lp["rwkv_a0"]), lp["rwkv_g2"], _row(lp["rwkv_k_k"]), _row(lp["rwkv_k_a"]),
              _row(lp["rwkv_r_k"]), _block_diag_ones(RWKV_N))
    specs = _halo_specs(tt, w, 0, t) + [_const_spec(c.shape) for c in consts]
    kern = functools.partial(_rwkv_prep_kernel, nct=n_ctx // tt, nt=t // tt)
    return _prep_call(kern, (p, p, p) + consts, specs,
                      (BRANCH, BRANCH, BRANCH, BRANCH, 2 * BRANCH, 2 * BRANCH, BRANCH, BRANCH),
                      b=b, t=t, tt=tt)


def _rwkv_body(r_ref, k_ref, v_ref, kk_ref, a_ref, lw_ref, ka_ref, o_ref, s_ref, *, reverse):
    incl, strict = _chunk_masks(reverse)
    last = 0 if reverse else CHUNK - 1
    lw_all = lw_ref[0]
    g_all = _dot_hi(incl.astype(F32), lw_all)
    a_all = a_ref[0]
    k_all = k_ref[0] * (1.0 + (a_all - 1.0) * ka_ref[...])
    yield
    heads = range(RWKV_HEADS)
    hs = [slice(h * RWKV_N, (h + 1) * RWKV_N) for h in heads]
    g = [g_all[:, hs[h]] for h in heads]
    gl = [g[h][last:last + 1, :] for h in heads]
    eneg = [jnp.exp(-g[h]) for h in heads]
    edec = [jnp.exp(gl[h] - g[h]) for h in heads]
    kk = [kk_ref[0, :, hs[h]] for h in heads]
    bvec = [kk[h] * a_all[:, hs[h]] for h in heads]
    k = [k_all[:, hs[h]] for h in heads]
    v = [v_ref[0, :, hs[h]] for h in heads]
    kkg = [kk[h] * jnp.exp(g[h] - lw_all[:, hs[h]]) for h in heads]
    rg = [r_ref[0, :, hs[h]] * jnp.exp(g[h]) for h in heads]
    bh = [bvec[h] * eneg[h] for h in heads]
    kh = [k[h] * eneg[h] for h in heads]
    s = [s_ref[h] for h in heads]
    yield
    both = [jnp.concatenate([kkg[h], rg[h]], axis=0) for h in heads]
    mask2 = jnp.concatenate([strict, incl], axis=0)
    mb = [jnp.where(mask2, _dot_nt(both[h], bh[h]), 0.0) for h in heads]
    yield
    mk = [jnp.where(mask2, _dot_nt(both[h], kh[h]), 0.0) for h in heads]
    yield
    part = [_dot(mk[h], v[h]) + _dot_nt(both[h], s[h]) for h in heads]
    yield
    x = yield from _unit_tri_solve([mb[h][:CHUNK] for h in heads], [part[h][:CHUNK] for h in heads])
    u = [-xh for xh in x]
    for h in heads:
        o_ref[0, :, hs[h]] = part[h][CHUNK:] + _dot(mb[h][CHUNK:], u[h])
    yield
    for h in heads:
        upd = _dot_tn(jnp.concatenate([u[h], v[h]], axis=0),
                      jnp.concatenate([bvec[h] * edec[h], k[h] * edec[h]], axis=0))
        s_ref[h] = s[h] * jnp.exp(gl[h]) + upd


def rwkv_scan(r, k, v, kk, a, lw, k_a, *, n_ctx):
    b, t, _ = r.shape
    toks = [(r, BRANCH, 0, 0), (k, BRANCH, 0, 0), (v, BRANCH, 0, 0), (kk, BRANCH, 0, 0),
            (a, BRANCH, 0, 1), (lw, BRANCH, 0, 1)]
    return _bidir_scan(_rwkv_body, toks, [(k_a, k_a)], (RWKV_HEADS, RWKV_N, RWKV_N), b=b, t=t, n_ctx=n_ctx,
                       batch_block=_batch_block(b, 2))


def _gdn_prep_kernel(x_ref, xp_ref, xn_ref, ab_ref, cw_ref, na_ref, dtb_ref, bd_ref,
                     q_ref, k_ref, v_ref, sm_ref, *, nct, nt):
    x = x_ref[0]
    xp, xn = _neighbours(x, xp_ref[0], xn_ref[0], nct=nct, nt=nt)
    y = _silu(xp * cw_ref[0:1, :] + x * cw_ref[1:2, :] + xn * cw_ref[2:3, :])
    q, k = y[:, :BRANCH], y[:, BRANCH:2 * BRANCH]
    q_ref[0] = q * lax.rsqrt(_dot_x2(q * q, bd_ref[...]) + EPS) * (GDN_N ** -0.5)
    k_ref[0] = k * lax.rsqrt(_dot_x2(k * k, bd_ref[...]) + EPS)
    v_ref[0] = y[:, 2 * BRANCH:]
    ab = ab_ref[0]
    lane = lax.broadcasted_iota(jnp.int32, ab.shape, 1)
    sm_ref[0] = jnp.where(lane < 2 * GDN_HEADS, na_ref[...] * _softplus(ab + dtb_ref[...]), _sigmoid(ab))


def gdn_prep(p, lp, *, n_ctx):
    b, t, _ = p.shape
    tt = _pick_tile(n_ctx, ROW_TILE)
    padrow = lambda v: jnp.pad(v.reshape(1, -1), ((0, 0), (0, LANES - 2 * GDN_HEADS)))
    consts = (lp["gdn_conv_w"], padrow(-jnp.exp(lp["gdn_a_log"])), padrow(lp["gdn_dt_bias"]),
              _block_diag_ones(GDN_N))
    specs = (_halo_specs(tt, 3 * BRANCH, 0, t) + [pl.BlockSpec((1, tt, LANES), lambda bi, i: (bi, i, 16))]
             + [_const_spec(c.shape) for c in consts])
    kern = functools.partial(_gdn_prep_kernel, nct=n_ctx // tt, nt=t // tt)
    return _prep_call(kern, (p, p, p, p) + consts, specs, (BRANCH, BRANCH, BRANCH, LANES), b=b, t=t, tt=tt)


def _gdn_body(q_ref, k_ref, v_ref, sm_ref, o_ref, s_ref, *, reverse):
    incl, strict = _chunk_masks(reverse)
    last = 0 if reverse else CHUNK - 1
    off = GDN_HEADS if reverse else 0
    sm = sm_ref[0]
    g_all = _dot_hi(incl.astype(F32), sm)
    yield
    gt_all = _transpose_small(g_all)
    yield
    heads = range(GDN_HEADS)
    hs = [slice(h * GDN_N, (h + 1) * GDN_N) for h in heads]
    g = [g_all[:, off + h:off + h + 1] for h in heads]
    gl = [g[h][last:last + 1, :] for h in heads]
    beta = [sm[:, 2 * GDN_HEADS + off + h:2 * GDN_HEADS + off + h + 1] for h in heads]
    q = [q_ref[0, :, hs[h]] for h in heads]
    k = [k_ref[0, :, hs[h]] for h in heads]
    v = [v_ref[0, :, hs[h]] for h in heads]
    s = [s_ref[h] for h in heads]
    decay = [jnp.exp(jnp.where(incl, g[h] - gt_all[off + h:off + h + 1, :], -jnp.inf)) for h in heads]
    yield
    kq = [_dot_nt(jnp.concatenate([k[h], q[h]], axis=0), k[h]) for h in heads]
    yield
    lower = [jnp.where(strict, kq[h][:CHUNK] * decay[h] * beta[h], 0.0) for h in heads]
    attn = [kq[h][CHUNK:] * decay[h] for h in heads]
    o_part = [_dot(q[h] * jnp.exp(g[h]), s[h]) for h in heads]
    yield
    rhs = [jnp.concatenate([v[h] * beta[h], k[h] * (beta[h] * jnp.exp(g[h]))], axis=1) for h in heads]
    sol = yield from _unit_tri_solve(lower, rhs)
    v_new = [sol[h][:, :GDN_N] - _dot(sol[h][:, GDN_N:], s[h]) for h in heads]
    yield
    for h in heads:
        o_ref[0, :, hs[h]] = o_part[h] + _dot(attn[h], v_new[h])
    yield
    for h in heads:
        s_ref[h] = s[h] * jnp.exp(gl[h]) + _dot_tn(k[h] * jnp.exp(gl[h] - g[h]), v_new[h])


def gdn_scan(q, k, v, sm, *, n_ctx):
    b, t, _ = q.shape
    toks = [(q, BRANCH, 0, 0), (k, BRANCH, 0, 0), (v, BRANCH, 0, 0), (sm, LANES, 0, 0)]
    return _bidir_scan(_gdn_body, toks, [], (GDN_HEADS, GDN_N, GDN_N), b=b, t=t, n_ctx=n_ctx,
                       batch_block=_batch_block(b, 4))


def _merge_kernel(sf_ref, sb_ref, sx_ref, sz_ref, gf_ref, gb_ref, gr_ref, rf_ref, rb_ref, rg_ref, rbo_ref,
                  df_ref, db_ref, dz_ref, gate_ref, x_ref, m_ref,
                  sd_ref, sn_ref, gn_ref, lnw_ref, lnb_ref, dn_ref, bd64_ref, bd128_ref, bd256_ref,
                  wb_ref, wo_ref, o_ref):
    def group_rms(y, bd_ref, n, w_ref):
        return y * lax.rsqrt(_dot_x2(y * y, bd_ref[...]) * (1.0 / n) + EPS) * w_ref[...]

    y = (sf_ref[0] + sb_ref[0] + sd_ref[...] * sx_ref[0]) * _silu(sz_ref[0])
    ys = group_rms(y, bd256_ref, BRANCH // SSM_GROUPS, sn_ref)
    yg = group_rms(gf_ref[0] + gb_ref[0], bd128_ref, GLA_DV, gn_ref) * _silu(gr_ref[0])
    y = rf_ref[0] + rb_ref[0]
    yc = y - _dot_x2(y, bd64_ref[...]) * (1.0 / RWKV_N)
    var = _dot_x2(yc * yc, bd64_ref[...]) * (1.0 / RWKV_N)
    yr = (yc * lax.rsqrt(var + RWKV_LN_EPS) * lnw_ref[...] + lnb_ref[...] + rbo_ref[0]) * rg_ref[0]
    yd = group_rms(df_ref[0] + db_ref[0], bd128_ref, GDN_N, dn_ref) * _silu(dz_ref[0])
    acc = None
    for i, yi in enumerate((ys, yg, yr, yd)):
        term = gate_ref[0, :, i * D_MODEL:(i + 1) * D_MODEL].astype(F32) * _dot(yi, wb_ref[i])
        acc = term if acc is None else acc + term
    o_ref[0] = x_ref[0] + m_ref[0, 0] * _dot(acc, wo_ref[...])


def merge_residual(ssm, gla, rwkv, gdn, gates, x_all, gate_mod, lp, w_branch, w_out, *, n_ctx):
    b, t, d = x_all.shape
    tm = _pick_tile(n_ctx, ROW_TILE)
    tok = lambda bi, i: (bi, i, 0)
    blk = lambda c: pl.BlockSpec((1, tm, BRANCH), lambda bi, i: (bi, i, c))
    half = blk(0)
    consts = (_row(jnp.repeat(lp["ssm_d"], SSM_P)), _row(lp["ssm_norm"]),
              _row(jnp.tile(lp["gla_norm"], GLA_HEADS)), _row(lp["rwkv_ln_w"]), _row(lp["rwkv_ln_b"]),
              _row(jnp.tile(lp["gdn_norm"], GDN_HEADS)),
              _block_diag_ones(RWKV_N), _block_diag_ones(LANES), _block_diag_ones(BRANCH // SSM_GROUPS),
              w_branch, w_out)
    ins = (ssm[0], ssm[1], ssm[2], ssm[3], gla[0], gla[1], gla[2], rwkv[0], rwkv[1], rwkv[2], rwkv[3],
           gdn[0], gdn[1], gdn[2], gates, x_all, gate_mod) + consts
    specs = ([half, half, half, blk(0), half, half, blk(2), half, half, half, half, half, half, blk(3),
              pl.BlockSpec((1, tm, 4 * d), tok), pl.BlockSpec((1, tm, d), tok),
              pl.BlockSpec((1, 1, 1, d), _mod_sel(n_ctx // tm))]
             + [_const_spec(c.shape) for c in consts])
    return pl.pallas_call(
        _merge_kernel,
        grid=(b, t // tm),
        in_specs=specs,
        out_specs=pl.BlockSpec((1, tm, d), tok),
        out_shape=jax.ShapeDtypeStruct((b, t, d), F32),
        compiler_params=_cparams("parallel", "parallel"),
    )(*ins)


def _route_kernel(h_ref, rw_ref, rb_ref, o_ref):
    logits = lax.dot_general(rw_ref[...], h_ref[...].astype(F32), (((1,), (1,)), ((), ())),
                             precision=HI, preferred_element_type=F32)
    scores = _sigmoid(logits)
    sel = scores + rb_ref[...]
    rows = [sel[e:e + 1, :] for e in range(N_EXPERTS)]
    sc = [scores[e:e + 1, :] for e in range(N_EXPERTS)]

    def top2(vals):
        v1, i1 = vals[0], jnp.zeros(vals[0].shape, jnp.int32)
        for j in range(1, len(vals)):
            better = vals[j] > v1
            v1 = jnp.where(better, vals[j], v1)
            i1 = jnp.where(better, j, i1)
        v2 = jnp.where(i1 == 0, vals[1], vals[0])
        i2 = jnp.where(i1 == 0, 1, 0)
        for j in range(1, len(vals)):
            better = (vals[j] > v2) & (i1 != j)
            v2 = jnp.where(better, vals[j], v2)
            i2 = jnp.where(better, j, i2)
        return v1, i1, v2, i2

    gsum = []
    for grp in range(N_GROUPS):
        v1, _, v2, _ = top2(rows[grp * EXPERTS_PER_GROUP:(grp + 1) * EXPERTS_PER_GROUP])
        gsum.append(v1 + v2)
    best, gidx = gsum[0], jnp.zeros(gsum[0].shape, jnp.int32)
    for grp in range(1, N_GROUPS):
        better = gsum[grp] > best
        best = jnp.where(better, gsum[grp], best)
        gidx = jnp.where(better, grp, gidx)
    chosen, chosen_sc = [], []
    for j in range(EXPERTS_PER_GROUP):
        cj, sj = rows[j], sc[j]
        for grp in range(1, N_GROUPS):
            cj = jnp.where(gidx == grp, rows[grp * EXPERTS_PER_GROUP + j], cj)
            sj = jnp.where(gidx == grp, sc[grp * EXPERTS_PER_GROUP + j], sj)
        chosen.append(cj)
        chosen_sc.append(sj)
    _, i1, _, i2 = top2(chosen)
    w1, w2 = jnp.zeros_like(best), jnp.zeros_like(best)
    for j in range(EXPERTS_PER_GROUP):
        w1 = jnp.where(i1 == j, chosen_sc[j], w1)
        w2 = jnp.where(i2 == j, chosen_sc[j], w2)
    tot = w1 + w2
    w1, w2 = w1 / tot, w2 / tot
    e1 = gidx * EXPERTS_PER_GROUP + i1
    e2 = gidx * EXPERTS_PER_GROUP + i2
    eid = lax.broadcasted_iota(jnp.int32, scores.shape, 0)
    o_ref[...] = jnp.where(eid == e1, w1, 0.0) + jnp.where(eid == e2, w2, 0.0)


def moe_route(h, router_w, router_b, *, tm):
    m, d = h.shape
    tm = _pick_tile(m, tm)
    return pl.pallas_call(
        _route_kernel,
        grid=(m // tm,),
        in_specs=[pl.BlockSpec((tm, d), lambda i: (i, 0)),
                  pl.BlockSpec((N_EXPERTS, d), lambda i: (0, 0)),
                  pl.BlockSpec((N_EXPERTS, 1), lambda i: (0, 0))],
        out_specs=pl.BlockSpec((N_EXPERTS, tm), lambda i: (0, i)),
        out_shape=jax.ShapeDtypeStruct((N_EXPERTS, m), F32),
        compiler_params=_cparams("parallel"),
    )(h, router_w.T, router_b.reshape(N_EXPERTS, 1))


def _expert_kernel(h_ref, g_ref, wg_ref, wu_ref, wd_ref, x_ref, m_ref, o_ref, acc_ref, hb_ref,
                   *, bsz, t_all, n_ctx):
    e = pl.program_id(1)

    @pl.when(e == 0)
    def _():
        acc_ref[...] = jnp.zeros_like(acc_ref)
        hb_ref[...] = h_ref[...].astype(BF16)

    h = hb_ref[...]
    gates = g_ref[...]
    lane = lax.broadcasted_iota(jnp.int32, gates.shape, 1)
    ge = jnp.sum(jnp.where(lane == e, gates, 0.0), axis=1, keepdims=True)
    hid = _silu(_dot(h, wg_ref[0])) * _dot(h, wu_ref[0])
    acc_ref[...] += _dot(ge * hid, wd_ref[0])

    @pl.when(e == N_EXPERTS - 1)
    def _():
        tm = acc_ref.shape[0]
        row = pl.program_id(0) * tm + lax.broadcasted_iota(jnp.int32, (tm, 1), 0)
        mod = jnp.zeros(acc_ref.shape, F32)
        ctx = jnp.zeros((tm, 1), jnp.bool_)
        for bi in range(bsz):
            lo = bi * t_all
            ctx = ctx | ((row >= lo) & (row < lo + n_ctx))
            lat = (row >= lo + n_ctx) & (row < lo + t_all)
            mod = mod + jnp.where(lat, m_ref[bi:bi + 1, :], 0.0)
        mod = mod + jnp.where(ctx, m_ref[bsz:bsz + 1, :], 0.0)
        o_ref[...] = x_ref[...] + mod * acc_ref[...]


def moe_experts(h, gates, wg, wu, wd, x_all, gate_rows, *, n_ctx, tm=1024):
    b, t, d = x_all.shape
    m = b * t
    tm = _pick_tile(m, tm)
    tok = lambda i, e: (i, 0)
    kern = functools.partial(_expert_kernel, bsz=b, t_all=t, n_ctx=n_ctx)
    out = pl.pallas_call(
        kern,
        grid=(m // tm, N_EXPERTS),
        in_specs=[pl.BlockSpec((tm, d), tok),
                  pl.BlockSpec((tm, N_EXPERTS), tok),
                  pl.BlockSpec((1, d, EXPERT_FF), lambda i, e: (e, 0, 0)),
                  pl.BlockSpec((1, d, EXPERT_FF), lambda i, e: (e, 0, 0)),
                  pl.BlockSpec((1, EXPERT_FF, d), lambda i, e: (e, 0, 0)),
                  pl.BlockSpec((tm, d), tok),
                  _const_spec(gate_rows.shape)],
        out_specs=pl.BlockSpec((tm, d), tok),
        out_shape=jax.ShapeDtypeStruct((m, d), F32),
        scratch_shapes=[pltpu.VMEM((tm, d), F32), pltpu.VMEM((tm, d), BF16)],
        compiler_params=_cparams("parallel", "arbitrary"),
    )(h.reshape(m, d), gates, wg, wu, wd, x_all.reshape(m, d), gate_rows)
    return out.reshape(b, t, d)


def _final_norm_kernel(x_ref, w_ref, o_ref):
    x = x_ref[...]
    o_ref[...] = x * lax.rsqrt(jnp.mean(x * x, axis=-1, keepdims=True) + EPS) * w_ref[...]


def final_rms_norm(x, w, *, tm):
    m, d = x.shape
    return pl.pallas_call(
        _final_norm_kernel,
        grid=(m // tm,),
        in_specs=[pl.BlockSpec((tm, d), lambda i: (i, 0)), pl.BlockSpec((1, d), lambda i: (0, 0))],
        out_specs=pl.BlockSpec((tm, d), lambda i: (i, 0)),
        out_shape=jax.ShapeDtypeStruct((m, d), F32),
        compiler_params=_cparams("parallel"),
    )(x, w.reshape(1, d))


def _pack_w_in(w_in, mixer):
    cols = _SRC_COLS[mixer]
    return jnp.where(jnp.asarray(cols >= 0)[None, :], w_in[:, np.maximum(cols, 0)], 0.0).astype(BF16)


def mixer_scans(ps, lp, *, n_ctx):
    p_ssm, p_gla, p_rwkv, p_gdn = ps

    xs, bc, sm = ssm_prep(p_ssm, lp, n_ctx=n_ctx)
    neg_a = jnp.pad(-jnp.exp(lp["ssm_a_log"]).reshape(1, -1), ((0, 0), (0, LANES - 2 * SSM_HEADS)))
    ssm = tuple(ssd_scan(xs, bc, sm, neg_a, n_ctx=n_ctx)) + (xs, p_ssm)

    w2 = [jnp.zeros((LANES, GLA_HEADS * GLA_DK), F32).at[d * GLA_RANK:(d + 1) * GLA_RANK].set(lp["gla_w2"][d])
          for d in range(2)]
    gb = [_row(lp["gla_b"][d]) for d in range(2)]
    gla = tuple(gla_scan(p_gla, w2, gb, n_ctx=n_ctx)) + (p_gla,)

    r, k, v, kk, a, lw, g, bonus = rwkv_prep(p_rwkv, lp, n_ctx=n_ctx)
    rwkv = tuple(rwkv_scan(r, k, v, kk, a, lw, _row(lp["rwkv_k_a"]), n_ctx=n_ctx)) + (g, bonus)

    q, kd, vd, smd = gdn_prep(p_gdn, lp, n_ctx=n_ctx)
    gdn = tuple(gdn_scan(q, kd, vd, smd, n_ctx=n_ctx)) + (p_gdn,)
    return ssm, gla, rwkv, gdn


def _to_scan_order(t):
    b, n, d = t.shape
    return t.reshape(b, n // GRID_W, GRID_W, d).transpose(0, 2, 1, 3).reshape(b, n, d)


def _from_scan_order(t):
    b, n, d = t.shape
    return t.reshape(b, GRID_W, n // GRID_W, d).transpose(0, 2, 1, 3).reshape(b, n, d)


def kernel(x, c, ctx, c_ctx, ada_w, ada_b, norm_mix, norm_ffn, w_in, w_gate, w_branch, w_out, ssm_conv_w, ssm_conv_b, ssm_a_log, ssm_dt_bias, ssm_d, ssm_norm, gla_w2, gla_b, gla_norm, rwkv_mu, rwkv_w0, rwkv_w2, rwkv_a0, rwkv_a2, rwkv_g2, rwkv_k_k, rwkv_k_a, rwkv_r_k, rwkv_ln_w, rwkv_ln_b, gdn_conv_w, gdn_a_log, gdn_dt_bias, gdn_norm, router_w, router_b, moe_w_gate, moe_w_up, moe_w_down, final_norm):
    bsz, seq, d = x.shape
    n_ctx = ctx.shape[1]
    t_all = n_ctx + seq
    m_all = bsz * t_all

    cond = jnp.concatenate([jax.nn.silu(c), jax.nn.silu(c_ctx)[None]], 0)
    cond = jnp.pad(cond, ((0, SUBLANES - cond.shape[0]), (0, 0)))
    mods, mod_rows = [], []
    for l in range(DEPTH):
        mod = pmatmul(cond, ada_w[l], tm=SUBLANES, tn=1024, precise=True) + ada_b[l]
        mod_rows.append(mod)
        lat = mod[:bsz].reshape(bsz, 6, d)
        cx = jnp.broadcast_to(mod[bsz].reshape(1, 6, d), (bsz, 6, d))
        mods.append(jnp.stack([cx, lat], axis=1))

    x_all = jnp.concatenate([ctx, x], axis=1)
    scan_order = False
    for l in range(DEPTH):
        if (l % 2 == 1) != scan_order:
            reorder = _from_scan_order if scan_order else _to_scan_order
            x_all = jnp.concatenate([x_all[:, :n_ctx], reorder(x_all[:, n_ctx:])], axis=1)
            scan_order = not scan_order
        lp = dict(ssm_conv_w=ssm_conv_w[l], ssm_conv_b=ssm_conv_b[l], ssm_a_log=ssm_a_log[l],
                  ssm_dt_bias=ssm_dt_bias[l], ssm_d=ssm_d[l], ssm_norm=ssm_norm[l],
                  gla_w2=gla_w2[l], gla_b=gla_b[l], gla_norm=gla_norm[l],
                  rwkv_mu=rwkv_mu[l], rwkv_w0=rwkv_w0[l], rwkv_w2=rwkv_w2[l], rwkv_a0=rwkv_a0[l],
                  rwkv_a2=rwkv_a2[l], rwkv_g2=rwkv_g2[l], rwkv_k_k=rwkv_k_k[l], rwkv_k_a=rwkv_k_a[l],
                  rwkv_r_k=rwkv_r_k[l], rwkv_ln_w=rwkv_ln_w[l], rwkv_ln_b=rwkv_ln_b[l],
                  gdn_conv_w=gdn_conv_w[l], gdn_a_log=gdn_a_log[l], gdn_dt_bias=gdn_dt_bias[l],
                  gdn_norm=gdn_norm[l])
        mod = mods[l]
        msel = lambda i: mod[:, :, i][:, :, None, :]

        h = norm_modulate(x_all, norm_mix[l], msel(0), msel(1), n_ctx=n_ctx)
        h2d = h.reshape(m_all, d)
        ps = []
        for mixer in ("ssm", "gla", "rwkv", "gdn"):
            wp = _pack_w_in(w_in[l], mixer)
            ps.append(pmatmul(h2d, wp, tm=512, tn=wp.shape[1]).reshape(bsz, t_all, wp.shape[1]))
        wg_cat = jnp.concatenate([w_gate[l, i] for i in range(4)], axis=1).astype(BF16)
        gates = pmatmul(h2d, wg_cat, tm=1024, tn=1024, act="sigmoid", out_dtype=BF16)
        gates = gates.reshape(bsz, t_all, 4 * d)

        ssm, gla, rwkv, gdn = mixer_scans(ps, lp, n_ctx=n_ctx)
        x_all = merge_residual(ssm, gla, rwkv, gdn, gates, x_all, msel(2), lp,
                               w_branch[l].astype(BF16), w_out[l].astype(BF16), n_ctx=n_ctx)

        h = norm_modulate(x_all, norm_ffn[l], msel(3), msel(4), n_ctx=n_ctx, out_dtype=F32)
        gate = moe_route(h.reshape(m_all, d), router_w, router_b, tm=1024)
        x_all = moe_experts(h, gate.T, moe_w_gate[l].astype(BF16), moe_w_up[l].astype(BF16),
                            moe_w_down[l].astype(BF16), x_all, mod_rows[l][:, 5 * d:], n_ctx=n_ctx)

    lat = x_all[:, n_ctx:]
    if scan_order:
        lat = _from_scan_order(lat)
    return final_rms_norm(lat.reshape(bsz * seq, d), final_norm, tm=1024).reshape(bsz, seq, d)
```

```python
import functools
import itertools

import numpy as np
import jax
import jax.numpy as jnp
from jax import lax
from jax.experimental import pallas as pl
from jax.experimental.pallas import tpu as pltpu

F32 = jnp.float32
BF16 = jnp.bfloat16
HI = lax.Precision.HIGHEST

D_MODEL = 1024
DEPTH = 2
GRID_W = 64
CHUNK = 64
EPS = 1e-6
BRANCH = D_MODEL // 2
SSM_HEADS, SSM_P, SSM_GROUPS, SSM_N = 8, 64, 2, 64
GLA_HEADS, GLA_DK, GLA_DV, GLA_RANK, GLA_TAU = 4, 64, 128, 16, 16.0
RWKV_HEADS, RWKV_N, RWKV_LN_EPS = 8, 64, 64e-5
GDN_HEADS, GDN_N = 4, 128
N_EXPERTS, N_GROUPS, EXPERTS_PER_GROUP = 16, 4, 4
EXPERT_FF = D_MODEL // 2
LANES = 128
SUBLANES = 8
VMEM_LIMIT = 48 * 1024 * 1024
ROW_TILE = 256

_REF_BLOCKS = (
    ("ssm", "z", 512), ("ssm", "xbc", 768), ("ssm", "dt", 16),
    ("gla", "q", 256), ("gla", "k", 256), ("gla", "v", 512), ("gla", "r", 512), ("gla", "glr", 32),
    ("rwkv", "all", 1920),
    ("gdn", "qkv", 1536), ("gdn", "z", 512), ("gdn", "ab", 16),
)
_PACKED = {
    "ssm": (("z", 512), ("dt", 128), ("pad", 128), ("xbc", 768)),
    "gla": (("q", 256), ("k", 256), ("v", 512), ("r", 512), ("glr", 128)),
    "rwkv": (("all", 1920),),
    "gdn": (("qkv", 1536), ("z", 512), ("ab", 128)),
}


def _packed_columns():
    start, s = {}, 0
    for mixer, blk, w in _REF_BLOCKS:
        start[(mixer, blk)] = (s, w)
        s += w
    out = {}
    for mixer, blocks in _PACKED.items():
        cols = []
        for blk, wp in blocks:
            s0, w = start.get((mixer, blk), (0, 0))
            cols += list(range(s0, s0 + w)) + [-1] * (wp - w)
        out[mixer] = np.asarray(cols, np.int32)
    return out


_SRC_COLS = _packed_columns()


def _cparams(*sem):
    return pltpu.CompilerParams(dimension_semantics=sem, vmem_limit_bytes=VMEM_LIMIT)


def _dot(a, b):
    return jnp.dot(a.astype(BF16), b.astype(BF16), preferred_element_type=F32)


def _dot_nt(a, b):
    return lax.dot_general(a.astype(BF16), b.astype(BF16), (((1,), (1,)), ((), ())),
                           preferred_element_type=F32)


def _dot_tn(a, b):
    return lax.dot_general(a.astype(BF16), b.astype(BF16), (((0,), (0,)), ((), ())),
                           preferred_element_type=F32)


def _dot_hi(a, b):
    return jnp.dot(a, b, precision=HI, preferred_element_type=F32)


def _dot_x3(a, b):
    ah = a.astype(BF16)
    al = (a - ah.astype(F32)).astype(BF16)
    bh = b.astype(BF16)
    bl = (b - bh.astype(F32)).astype(BF16)
    f = lambda u, v: jnp.dot(u, v, preferred_element_type=F32)
    return f(ah, bh) + (f(ah, bl) + f(al, bh))


def _dot_x2(a, w):
    ah = a.astype(BF16)
    al = (a - ah.astype(F32)).astype(BF16)
    return jnp.dot(ah, w, preferred_element_type=F32) + jnp.dot(al, w, preferred_element_type=F32)


def _softplus(x):
    return jnp.maximum(x, 0.0) + jnp.log(1.0 + jnp.exp(-jnp.abs(x)))


def _sigmoid(x):
    return 1.0 / (1.0 + jnp.exp(-x))


def _silu(x):
    return x * _sigmoid(x)


def _pick_tile(m, pref):
    t = pref
    while m % t:
        t //= 2
    return t


def _block_diag_ones(n, width=BRANCH):
    idx = np.arange(width) // n
    return jnp.asarray(idx[:, None] == idx[None, :], BF16)


def _mm_kernel(a_ref, w_ref, o_ref, *, act, precise):
    if precise:
        r = _dot_hi(a_ref[...].astype(F32), w_ref[...].astype(F32))
    else:
        r = _dot(a_ref[...], w_ref[...])
    if act == "sigmoid":
        r = _sigmoid(r)
    o_ref[...] = r.astype(o_ref.dtype)


def pmatmul(a, w, *, tm, tn, act=None, precise=False, out_dtype=F32):
    m, k = a.shape
    n = w.shape[1]
    tm = _pick_tile(m, tm)
    assert tm % SUBLANES == 0 and n % tn == 0, (m, tm, n, tn)
    return pl.pallas_call(
        functools.partial(_mm_kernel, act=act, precise=precise),
        grid=(n // tn, m // tm),
        in_specs=[pl.BlockSpec((tm, k), lambda j, i: (i, 0)),
                  pl.BlockSpec((k, tn), lambda j, i: (0, j))],
        out_specs=pl.BlockSpec((tm, tn), lambda j, i: (i, j)),
        out_shape=jax.ShapeDtypeStruct((m, n), out_dtype),
        compiler_params=_cparams("parallel", "parallel"),
    )(a, w)


def _norm_mod_kernel(x_ref, w_ref, shift_ref, scale_ref, o_ref):
    x = x_ref[0]
    y = x * lax.rsqrt(jnp.mean(x * x, axis=-1, keepdims=True) + EPS) * w_ref[...]
    o_ref[0] = (y * (1.0 + scale_ref[0, 0]) + shift_ref[0, 0]).astype(o_ref.dtype)


def _mod_sel(nct):
    return lambda bi, i, *_: (bi, jnp.where(i < nct, 0, 1), 0, 0)


def norm_modulate(x_all, w, shift, scale, *, n_ctx, out_dtype=BF16):
    b, t, d = x_all.shape
    tm = _pick_tile(n_ctx, ROW_TILE)
    assert t % tm == 0
    tok = lambda bi, i: (bi, i, 0)
    return pl.pallas_call(
        _norm_mod_kernel,
        grid=(b, t // tm),
        in_specs=[pl.BlockSpec((1, tm, d), tok),
                  pl.BlockSpec((1, d), lambda bi, i: (0, 0)),
                  pl.BlockSpec((1, 1, 1, d), _mod_sel(n_ctx // tm)),
                  pl.BlockSpec((1, 1, 1, d), _mod_sel(n_ctx // tm))],
        out_specs=pl.BlockSpec((1, tm, d), tok),
        out_shape=jax.ShapeDtypeStruct((b, t, d), out_dtype),
        compiler_params=_cparams("parallel", "parallel"),
    )(x_all, w.reshape(1, d), shift, scale)


def _row(v):
    return v.reshape(1, -1).astype(F32)


def _const_spec(shape):
    return pl.BlockSpec(shape, lambda *_: (0,) * len(shape))


def _tile_specs(tt, width, col):
    r8 = tt // SUBLANES
    main = pl.BlockSpec((1, tt, width), lambda bi, i: (bi, i, col))
    prev = pl.BlockSpec((1, SUBLANES, width), lambda bi, i: (bi, jnp.maximum(i * r8 - 1, 0), col))
    return main, prev, r8


def _halo_specs(tt, width, col, t):
    main, prev, r8 = _tile_specs(tt, width, col)
    last8 = t // SUBLANES - 1
    nxt = pl.BlockSpec((1, SUBLANES, width), lambda bi, i: (bi, jnp.minimum((i + 1) * r8, last8), col))
    return [main, prev, nxt]


def _neighbours(x, prev8, next8, *, nct, nt):
    i = pl.program_id(1)
    tt = x.shape[0]
    row = lax.broadcasted_iota(jnp.int32, x.shape, 0)
    first = (i == 0) | (i == nct)
    last = (i == nct - 1) | (i == nt - 1)
    pr = jnp.where(first, 0.0, prev8[SUBLANES - 1:SUBLANES, :])
    nx = jnp.where(last, 0.0, next8[0:1, :])
    xp = jnp.where(row == 0, pr, pltpu.roll(x, 1, 0))
    xn = jnp.where(row == tt - 1, nx, pltpu.roll(x, tt - 1, 0))
    return xp, xn


def _prep_call(kernel, ins, in_specs, out_widths, *, b, t, tt, out_dtype=F32):
    tok = lambda bi, i: (bi, i, 0)
    return pl.pallas_call(
        kernel,
        grid=(b, t // tt),
        in_specs=in_specs,
        out_specs=[pl.BlockSpec((1, tt, w), tok) for w in out_widths],
        out_shape=[jax.ShapeDtypeStruct((b, t, w), out_dtype) for w in out_widths],
        compiler_params=_cparams("parallel", "parallel"),
    )(*ins)


def _chunk_masks(reverse):
    row = lax.broadcasted_iota(jnp.int32, (CHUNK, CHUNK), 0)
    col = lax.broadcasted_iota(jnp.int32, (CHUNK, CHUNK), 1)
    if reverse:
        return col >= row, col > row
    return col <= row, col < row


def _chunk_order(i, n_ctx_chunks, n_chunks, reverse):
    if not reverse:
        return i
    return jnp.where(i < n_ctx_chunks, n_ctx_chunks - 1 - i, n_chunks - 1 - (i - n_ctx_chunks))


def _split3(a):
    hi = a.astype(BF16)
    r = a - hi.astype(F32)
    mid = r.astype(BF16)
    return hi, mid, (r - mid.astype(F32)).astype(BF16)


def _transpose_small(x):
    row = lax.broadcasted_iota(jnp.int32, (LANES, LANES), 0)
    col = lax.broadcasted_iota(jnp.int32, (LANES, LANES), 1)
    eye = (row == col).astype(BF16)
    nt = lambda p: lax.dot_general(eye, p, (((1,), (1,)), ((), ())), preferred_element_type=F32)
    hi, mid, lo = _split3(x)
    return nt(hi) + (nt(mid) + nt(lo))


def _chunk_cumsum(incl, x):
    m = incl.astype(BF16)
    hi, mid, lo = _split3(x)
    f = lambda p: jnp.dot(m, p, preferred_element_type=F32)
    return f(hi) + (f(mid) + f(lo))


def _select_columns(x, sel):
    c = x.shape[0]
    y = jnp.dot(jnp.concatenate(_split3(x), axis=0), sel, preferred_element_type=F32)
    return y[:c] + (y[c:2 * c] + y[2 * c:])


def _unit_tri_solve(mats, rhs, precise_levels=0):
    n = range(len(mats))
    x = [rhs[h] - _dot_x3(mats[h], rhs[h]) for h in n]
    yield
    p = mats
    for level in range(int(np.log2(CHUNK)) - 1):
        dot = _dot_x3 if level < precise_levels else _dot
        p = [dot(p[h], p[h]) for h in n]
        yield
        x = [x[h] + dot(p[h], x[h]) for h in n]
        yield
    return x


def _bidir_scan(body, tok_ins, const_ins, state_shape, *, b, t, n_ctx, lockstep=True, batch_block=1):
    nc, ncc = t // CHUNK, n_ctx // CHUNK
    nb = batch_block
    assert b % nb == 0

    def chunk_spec(width, col, reverse):
        return pl.BlockSpec((nb, CHUNK, width), lambda bi, i: (bi, _chunk_order(i, ncc, nc, reverse), col))

    def direction(reverse):
        d = int(reverse)
        specs = [chunk_spec(w, cols[d], reverse) for _, w, *cols in tok_ins]
        specs += [_const_spec(pair[d].shape) for pair in const_ins]
        return specs, [a for a, *_ in tok_ins] + [pair[d] for pair in const_ins]

    (spec_f, arg_f), (spec_b, arg_b) = direction(False), direction(True)
    n_tok, n_in = len(tok_ins), len(arg_f)

    def kern(*refs):
        o_f, o_b, s_f, s_b = refs[2 * n_in:]

        @pl.when(pl.program_id(1) == 0)
        def _():
            s_f[...] = jnp.zeros_like(s_f)
            s_b[...] = jnp.zeros_like(s_b)

        def one(j, ins, o_ref, s_ref, reverse):
            ins = [r.at[pl.ds(j, 1)] if k < n_tok else r for k, r in enumerate(ins)]
            return body(*ins, o_ref.at[pl.ds(j, 1)], s_ref.at[j], reverse=reverse)

        gens = []
        for j in range(nb):
            gens += [one(j, refs[:n_in], o_f, s_f, False), one(j, refs[n_in:2 * n_in], o_b, s_b, True)]
        if not lockstep:
            gens = [itertools.chain(*gens)]
        while gens:
            gens = [g for g in gens if next(g, _DONE) is not _DONE]

    return pl.pallas_call(
        kern,
        grid=(b // nb, nc),
        in_specs=spec_f + spec_b,
        out_specs=[chunk_spec(BRANCH, 0, False), chunk_spec(BRANCH, 0, True)],
        out_shape=[jax.ShapeDtypeStruct((b, t, BRANCH), F32)] * 2,
        scratch_shapes=[pltpu.VMEM((nb,) + tuple(state_shape), F32)] * 2,
        compiler_params=_cparams("parallel", "arbitrary"),
    )(*arg_f, *arg_b)


_DONE = object()


def _batch_block(b, pref):
    return pref if b % pref == 0 else 1


def _ssm_prep_kernel(x_ref, xp_ref, xn_ref, dt_ref, cw_ref, cb_ref, dtb_ref, xs_ref, bc_ref, sm_ref,
                     *, nct, nt):
    x = x_ref[0]
    xp, xn = _neighbours(x, xp_ref[0], xn_ref[0], nct=nct, nt=nt)
    y = _silu(xp * cw_ref[0:1, :] + x * cw_ref[1:2, :] + xn * cw_ref[2:3, :] + cb_ref[...])
    xs_ref[0] = y[:, :BRANCH]
    bc_ref[0] = y[:, BRANCH:]
    sm_ref[0] = _softplus(dt_ref[0] + dtb_ref[...])


def ssm_prep(p, lp, *, n_ctx):
    b, t, _ = p.shape
    tt = _pick_tile(n_ctx, ROW_TILE)
    dtb = jnp.pad(lp["ssm_dt_bias"].reshape(1, -1), ((0, 0), (0, LANES - 2 * SSM_HEADS)))
    specs = _halo_specs(tt, 768, 1, t) + [pl.BlockSpec((1, tt, LANES), lambda bi, i: (bi, i, 4)),
                                          _const_spec((3, 768)), _const_spec((1, 768)), _const_spec((1, LANES))]
    kern = functools.partial(_ssm_prep_kernel, nct=n_ctx // tt, nt=t // tt)
    return _prep_call(kern, (p, p, p, p, lp["ssm_conv_w"], _row(lp["ssm_conv_b"]), dtb), specs,
                      (BRANCH, 2 * SSM_GROUPS * SSM_N, LANES), b=b, t=t, tt=tt)


def _ssd_body(x_ref, bc_ref, sm_ref, na_ref, o_ref, s_ref, *, reverse):
    incl, _ = _chunk_masks(reverse)
    last = 0 if reverse else CHUNK - 1
    off = SSM_HEADS if reverse else 0
    dt_all = sm_ref[0]
    g_all = _chunk_cumsum(incl, dt_all * na_ref[...])
    yield
    expand = _expand_matrix(off, SSM_HEADS, SSM_P)
    gx = _select_columns(g_all, expand)
    dx = _select_columns(dt_all, expand)
    gt_all = _transpose_small(g_all)
    dtt_all = _transpose_small(dt_all)
    yield
    heads = range(SSM_HEADS)
    rep = SSM_HEADS // SSM_GROUPS
    gw = SSM_GROUPS * SSM_N
    pw = rep * SSM_P
    hs = [slice(h * SSM_P, (h + 1) * SSM_P) for h in heads]
    glx = gx[last:last + 1, :]
    egx = jnp.exp(gx)
    wx = dx * jnp.exp(glx - gx)
    eglx = jnp.exp(glx)
    x = x_ref[0]
    bm = [bc_ref[0, :, grp * SSM_N:(grp + 1) * SSM_N] for grp in range(SSM_GROUPS)]
    cm = [bc_ref[0, :, gw + grp * SSM_N:gw + (grp + 1) * SSM_N] for grp in range(SSM_GROUPS)]
    cb = [_dot_nt(cm[grp], bm[grp]) for grp in range(SSM_GROUPS)]
    s = [s_ref[grp] for grp in range(SSM_GROUPS)]
    yield
    scores = [cb[h // rep] * jnp.exp(jnp.where(incl, gx[:, hs[h]] - gt_all[off + h:off + h + 1, :], -jnp.inf))
              * dtt_all[off + h:off + h + 1, :] for h in heads]
    yield
    intra = [_dot(scores[h], x[:, hs[h]]) for h in heads]
    yield
    inter = [_dot(cm[grp], s[grp]) for grp in range(SSM_GROUPS)]
    yield
    upd = [_dot_tn(bm[h // rep] * wx[:, hs[h]], x[:, hs[h]]) for h in heads]
    yield
    for h in heads:
        grp, ls = h // rep, slice((h % rep) * SSM_P, (h % rep + 1) * SSM_P)
        o_ref[0, :, hs[h]] = intra[h] + egx[:, hs[h]] * inter[grp][:, ls]
        s_ref[grp, :, ls] = s[grp][:, ls] * eglx[:, hs[h]] + upd[h]


def _expand_matrix(off, n_heads, width):
    row = lax.broadcasted_iota(jnp.int32, (LANES, n_heads * width), 0)
    col = lax.broadcasted_iota(jnp.int32, (LANES, n_heads * width), 1)
    lo = row * width - off * width
    return ((col >= lo) & (col < lo + width)).astype(BF16)


def ssd_scan(xs, bc, sm, neg_a, *, n_ctx):
    b, t, _ = xs.shape
    toks = [(xs, BRANCH, 0, 0), (bc, 2 * SSM_GROUPS * SSM_N, 0, 0), (sm, LANES, 0, 0)]
    state = (SSM_GROUPS, SSM_N, (SSM_HEADS // SSM_GROUPS) * SSM_P)
    return _bidir_scan(_ssd_body, toks, [(neg_a, neg_a)], state, b=b, t=t, n_ctx=n_ctx,
                       batch_block=_batch_block(b, 2))


def _gla_body(q_ref, k_ref, v_ref, glr_ref, w2_ref, gb_ref, o_ref, s_ref, *, reverse):
    incl, _ = _chunk_masks(reverse)
    last = 0 if reverse else CHUNK - 1
    logit = _dot_x3(glr_ref[0], w2_ref[...]) + gb_ref[...]
    yield
    la = -_softplus(-logit) * (1.0 / GLA_TAU)
    g_all = _chunk_cumsum(incl, la)
    yield
    heads = range(GLA_HEADS)
    ks = [slice(h * GLA_DK, (h + 1) * GLA_DK) for h in heads]
    vs = [slice(h * GLA_DV, (h + 1) * GLA_DV) for h in heads]
    g = [g_all[:, ks[h]] for h in heads]
    gl = [g[h][last:last + 1, :] for h in heads]
    k = [k_ref[0, :, ks[h]] for h in heads]
    v = [v_ref[0, :, vs[h]] for h in heads]
    qg = [q_ref[0, :, ks[h]] * (GLA_DK ** -0.5) * jnp.exp(g[h]) for h in heads]
    st = [s_ref[h] for h in heads]
    yield
    scores = [jnp.where(incl, _dot_nt(qg[h], k[h] * jnp.exp(-g[h])), 0.0) for h in heads]
    yield
    intra = [_dot(scores[h], v[h]) for h in heads]
    yield
    inter = [_dot_nt(qg[h], st[h]) for h in heads]
    yield
    upd = [_dot_tn(v[h], k[h] * jnp.exp(gl[h] - g[h])) for h in heads]
    yield
    for h in heads:
        o_ref[0, :, vs[h]] = intra[h] + inter[h]
        s_ref[h] = st[h] * jnp.exp(gl[h]) + upd[h]


def gla_scan(p, w2_pair, gb_pair, *, n_ctx):
    b, t, _ = p.shape
    kwid = GLA_HEADS * GLA_DK
    toks = [(p, kwid, 0, 0), (p, kwid, 1, 1), (p, BRANCH, 1, 1), (p, LANES, 12, 12)]
    return _bidir_scan(_gla_body, toks, [w2_pair, gb_pair], (GLA_HEADS, GLA_DV, GLA_DK), b=b, t=t, n_ctx=n_ctx,
                       batch_block=_batch_block(b, 4))


def _rwkv_prep_kernel(x_ref, xp_ref, xn_ref, mu_ref, w2_ref, w0_ref, a2_ref, a0_ref, g2_ref, kk_ref_w,
                      ka_ref, rk_ref, bd_ref, r_ref, k_ref, v_ref, kk_ref, a_ref, lw_ref, g_ref, bo_ref,
                      *, nct, nt):
    x = x_ref[0]
    xp, xn = _neighbours(x, xp_ref[0], xn_ref[0], nct=nct, nt=nt)
    x = x + mu_ref[...] * (0.5 * (xp + xn) - x)
    r, k, v = x[:, :BRANCH], x[:, BRANCH:2 * BRANCH], x[:, 2 * BRANCH:3 * BRANCH]
    wlr = x[:, 3 * BRANCH:3 * BRANCH + LANES]
    alr = x[:, 3 * BRANCH + LANES:3 * BRANCH + 2 * LANES]
    glr = x[:, 3 * BRANCH + 2 * LANES:]
    w_raw = _dot_x3(jnp.tanh(wlr), w2_ref[...]) + w0_ref[...]
    lw_ref[0] = -jnp.exp(-_softplus(-w_raw) - 0.5)
    a = _sigmoid(_dot_x3(alr, a2_ref[...]) + a0_ref[...])
    a_ref[0] = a
    g_ref[0] = _dot_x3(_sigmoid(glr), g2_ref[...])
    kk = k * kk_ref_w[...]
    kk_ref[0] = kk * lax.rsqrt(_dot_x2(kk * kk, bd_ref[...]) + EPS)
    ksum = k * (2.0 + (a[:, :BRANCH] + a[:, BRANCH:] - 2.0) * ka_ref[...])
    bo_ref[0] = _dot_x2(r * ksum * rk_ref[...], bd_ref[...]) * v
    r_ref[0] = r
    k_ref[0] = k
    v_ref[0] = v


def rwkv_prep(p, lp, *, n_ctx):
    b, t, w = p.shape
    tt = _pick_tile(n_ctx, ROW_TILE)

    def pair(wp):
        r, c = wp.shape[1:]
        return jnp.zeros((LANES, 2 * c), F32).at[:r, :c].set(wp[0]).at[r:2 * r, c:].set(wp[1])

    consts = (_row(lp["rwkv_mu"]), pair(lp["rwkv_w2"]), _row(lp["rwkv_w0"]), pair(lp["rwkv_a2"]),
              _row(lp["rwkv_a0"]), lp["rwkv_g2"], _row(lp["rwkv_k_k"]), _row(lp["rwkv_k_a"]),
              _row(lp["rwkv_r_k"]), _block_diag_ones(RWKV_N))
    specs = _halo_specs(tt, w, 0, t) + [_const_spec(c.shape) for c in consts]
    kern = functools.partial(_rwkv_prep_kernel, nct=n_ctx // tt, nt=t // tt)
    return _prep_call(kern, (p, p, p) + consts, specs,
                      (BRANCH, BRANCH, BRANCH, BRANCH, 2 * BRANCH, 2 * BRANCH, BRANCH, BRANCH),
                      b=b, t=t, tt=tt)


def _rwkv_body(r_ref, k_ref, v_ref, kk_ref, a_ref, lw_ref, ka_ref, o_ref, s_ref, *, reverse):
    incl, strict = _chunk_masks(reverse)
    last = 0 if reverse else CHUNK - 1
    lw_all = lw_ref[0]
    g_all = _chunk_cumsum(incl, lw_all)
    a_all = a_ref[0]
    k_all = k_ref[0] * (1.0 + (a_all - 1.0) * ka_ref[...])
    yield
    heads = range(RWKV_HEADS)
    hs = [slice(h * RWKV_N, (h + 1) * RWKV_N) for h in heads]
    g = [g_all[:, hs[h]] for h in heads]
    gl = [g[h][last:last + 1, :] for h in heads]
    eneg = [jnp.exp(-g[h]) for h in heads]
    edec = [jnp.exp(gl[h] - g[h]) for h in heads]
    kk = [kk_ref[0, :, hs[h]] for h in heads]
    bvec = [kk[h] * a_all[:, hs[h]] for h in heads]
    k = [k_all[:, hs[h]] for h in heads]
    v = [v_ref[0, :, hs[h]] for h in heads]
    kkg = [kk[h] * jnp.exp(g[h] - lw_all[:, hs[h]]) for h in heads]
    rg = [r_ref[0, :, hs[h]] * jnp.exp(g[h]) for h in heads]
    bh = [bvec[h] * eneg[h] for h in heads]
    kh = [k[h] * eneg[h] for h in heads]
    s = [s_ref[h] for h in heads]
    yield
    both = [jnp.concatenate([kkg[h], rg[h]], axis=0) for h in heads]
    mask2 = jnp.concatenate([strict, incl], axis=0)
    mb = [jnp.where(mask2, _dot_nt(both[h], bh[h]), 0.0) for h in heads]
    yield
    mk = [jnp.where(mask2, _dot_nt(both[h], kh[h]), 0.0) for h in heads]
    yield
    part = [_dot(mk[h], v[h]) + _dot_nt(both[h], s[h]) for h in heads]
    yield
    x = yield from _unit_tri_solve([mb[h][:CHUNK] for h in heads], [part[h][:CHUNK] for h in heads])
    u = [-xh for xh in x]
    for h in heads:
        o_ref[0, :, hs[h]] = part[h][CHUNK:] + _dot(mb[h][CHUNK:], u[h])
    yield
    for h in heads:
        upd = _dot_tn(jnp.concatenate([u[h], v[h]], axis=0),
                      jnp.concatenate([bvec[h] * edec[h], k[h] * edec[h]], axis=0))
        s_ref[h] = s[h] * jnp.exp(gl[h]) + upd


def rwkv_scan(r, k, v, kk, a, lw, k_a, *, n_ctx):
    b, t, _ = r.shape
    toks = [(r, BRANCH, 0, 0), (k, BRANCH, 0, 0), (v, BRANCH, 0, 0), (kk, BRANCH, 0, 0),
            (a, BRANCH, 0, 1), (lw, BRANCH, 0, 1)]
    return _bidir_scan(_rwkv_body, toks, [(k_a, k_a)], (RWKV_HEADS, RWKV_N, RWKV_N), b=b, t=t, n_ctx=n_ctx,
                       batch_block=_batch_block(b, 2))


def _gdn_prep_kernel(x_ref, xp_ref, xn_ref, ab_ref, cw_ref, na_ref, dtb_ref, bd_ref,
                     q_ref, k_ref, v_ref, sm_ref, *, nct, nt):
    x = x_ref[0]
    xp, xn = _neighbours(x, xp_ref[0], xn_ref[0], nct=nct, nt=nt)
    y = _silu(xp * cw_ref[0:1, :] + x * cw_ref[1:2, :] + xn * cw_ref[2:3, :])
    q, k = y[:, :BRANCH], y[:, BRANCH:2 * BRANCH]
    q_ref[0] = q * lax.rsqrt(_dot_x2(q * q, bd_ref[...]) + EPS) * (GDN_N ** -0.5)
    k_ref[0] = k * lax.rsqrt(_dot_x2(k * k, bd_ref[...]) + EPS)
    v_ref[0] = y[:, 2 * BRANCH:]
    ab = ab_ref[0]
    lane = lax.broadcasted_iota(jnp.int32, ab.shape, 1)
    sm_ref[0] = jnp.where(lane < 2 * GDN_HEADS, na_ref[...] * _softplus(ab + dtb_ref[...]), _sigmoid(ab))


def gdn_prep(p, lp, *, n_ctx):
    b, t, _ = p.shape
    tt = _pick_tile(n_ctx, ROW_TILE)
    padrow = lambda v: jnp.pad(v.reshape(1, -1), ((0, 0), (0, LANES - 2 * GDN_HEADS)))
    consts = (lp["gdn_conv_w"], padrow(-jnp.exp(lp["gdn_a_log"])), padrow(lp["gdn_dt_bias"]),
              _block_diag_ones(GDN_N))
    specs = (_halo_specs(tt, 3 * BRANCH, 0, t) + [pl.BlockSpec((1, tt, LANES), lambda bi, i: (bi, i, 16))]
             + [_const_spec(c.shape) for c in consts])
    kern = functools.partial(_gdn_prep_kernel, nct=n_ctx // tt, nt=t // tt)
    return _prep_call(kern, (p, p, p, p) + consts, specs, (BRANCH, BRANCH, BRANCH, LANES), b=b, t=t, tt=tt)


def _gdn_body(q_ref, k_ref, v_ref, sm_ref, o_ref, s_ref, *, reverse):
    incl, strict = _chunk_masks(reverse)
    last = 0 if reverse else CHUNK - 1
    off = GDN_HEADS if reverse else 0
    sm = sm_ref[0]
    g_all = _chunk_cumsum(incl, sm)
    yield
    gt_all = _transpose_small(g_all)
    yield
    heads = range(GDN_HEADS)
    hs = [slice(h * GDN_N, (h + 1) * GDN_N) for h in heads]
    g = [g_all[:, off + h:off + h + 1] for h in heads]
    gl = [g[h][last:last + 1, :] for h in heads]
    beta = [sm[:, 2 * GDN_HEADS + off + h:2 * GDN_HEADS + off + h + 1] for h in heads]
    q = [q_ref[0, :, hs[h]] for h in heads]
    k = [k_ref[0, :, hs[h]] for h in heads]
    v = [v_ref[0, :, hs[h]] for h in heads]
    s = [s_ref[h] for h in heads]
    decay = [jnp.exp(jnp.where(incl, g[h] - gt_all[off + h:off + h + 1, :], -jnp.inf)) for h in heads]
    yield
    kq = [_dot_nt(jnp.concatenate([k[h], q[h]], axis=0), k[h]) for h in heads]
    yield
    lower = [jnp.where(strict, kq[h][:CHUNK] * decay[h] * beta[h], 0.0) for h in heads]
    attn = [kq[h][CHUNK:] * decay[h] for h in heads]
    o_part = [_dot(q[h] * jnp.exp(g[h]), s[h]) for h in heads]
    yield
    rhs = [jnp.concatenate([v[h] * beta[h], k[h] * (beta[h] * jnp.exp(g[h]))], axis=1) for h in heads]
    sol = yield from _unit_tri_solve(lower, rhs, precise_levels=2)
    v_new = [sol[h][:, :GDN_N] - _dot(sol[h][:, GDN_N:], s[h]) for h in heads]
    yield
    for h in heads:
        o_ref[0, :, hs[h]] = o_part[h] + _dot(attn[h], v_new[h])
    yield
    for h in heads:
        s_ref[h] = s[h] * jnp.exp(gl[h]) + _dot_tn(k[h] * jnp.exp(gl[h] - g[h]), v_new[h])


def gdn_scan(q, k, v, sm, *, n_ctx):
    b, t, _ = q.shape
    toks = [(q, BRANCH, 0, 0), (k, BRANCH, 0, 0), (v, BRANCH, 0, 0), (sm, LANES, 0, 0)]
    return _bidir_scan(_gdn_body, toks, [], (GDN_HEADS, GDN_N, GDN_N), b=b, t=t, n_ctx=n_ctx,
                       batch_block=_batch_block(b, 4))


def _merge_kernel(sf_ref, sb_ref, sx_ref, sz_ref, gf_ref, gb_ref, gr_ref, rf_ref, rb_ref, rg_ref, rbo_ref,
                  df_ref, db_ref, dz_ref, gate_ref, x_ref, m_ref,
                  sd_ref, sn_ref, gn_ref, lnw_ref, lnb_ref, dn_ref, bd64_ref, bd128_ref, bd256_ref,
                  wb_ref, wo_ref, o_ref):
    def group_rms(y, bd_ref, n, w_ref):
        return y * lax.rsqrt(_dot_x2(y * y, bd_ref[...]) * (1.0 / n) + EPS) * w_ref[...]

    y = (sf_ref[0] + sb_ref[0] + sd_ref[...] * sx_ref[0]) * _silu(sz_ref[0])
    ys = group_rms(y, bd256_ref, BRANCH // SSM_GROUPS, sn_ref)
    yg = group_rms(gf_ref[0] + gb_ref[0], bd128_ref, GLA_DV, gn_ref) * _silu(gr_ref[0])
    y = rf_ref[0] + rb_ref[0]
    yc = y - _dot_x2(y, bd64_ref[...]) * (1.0 / RWKV_N)
    var = _dot_x2(yc * yc, bd64_ref[...]) * (1.0 / RWKV_N)
    yr = (yc * lax.rsqrt(var + RWKV_LN_EPS) * lnw_ref[...] + lnb_ref[...] + rbo_ref[0]) * rg_ref[0]
    yd = group_rms(df_ref[0] + db_ref[0], bd128_ref, GDN_N, dn_ref) * _silu(dz_ref[0])
    acc = None
    for i, yi in enumerate((ys, yg, yr, yd)):
        term = gate_ref[0, :, i * D_MODEL:(i + 1) * D_MODEL].astype(F32) * _dot(yi, wb_ref[i])
        acc = term if acc is None else acc + term
    o_ref[0] = x_ref[0] + m_ref[0, 0] * _dot(acc, wo_ref[...])


def merge_residual(ssm, gla, rwkv, gdn, gates, x_all, gate_mod, lp, w_branch, w_out, *, n_ctx):
    b, t, d = x_all.shape
    tm = _pick_tile(n_ctx, ROW_TILE)
    tok = lambda bi, i: (bi, i, 0)
    blk = lambda c: pl.BlockSpec((1, tm, BRANCH), lambda bi, i: (bi, i, c))
    half = blk(0)
    consts = (_row(jnp.repeat(lp["ssm_d"], SSM_P)), _row(lp["ssm_norm"]),
              _row(jnp.tile(lp["gla_norm"], GLA_HEADS)), _row(lp["rwkv_ln_w"]), _row(lp["rwkv_ln_b"]),
              _row(jnp.tile(lp["gdn_norm"], GDN_HEADS)),
              _block_diag_ones(RWKV_N), _block_diag_ones(LANES), _block_diag_ones(BRANCH // SSM_GROUPS),
              w_branch, w_out)
    ins = (ssm[0], ssm[1], ssm[2], ssm[3], gla[0], gla[1], gla[2], rwkv[0], rwkv[1], rwkv[2], rwkv[3],
           gdn[0], gdn[1], gdn[2], gates, x_all, gate_mod) + consts
    specs = ([half, half, half, blk(0), half, half, blk(2), half, half, half, half, half, half, blk(3),
              pl.BlockSpec((1, tm, 4 * d), tok), pl.BlockSpec((1, tm, d), tok),
              pl.BlockSpec((1, 1, 1, d), _mod_sel(n_ctx // tm))]
             + [_const_spec(c.shape) for c in consts])
    return pl.pallas_call(
        _merge_kernel,
        grid=(b, t // tm),
        in_specs=specs,
        out_specs=pl.BlockSpec((1, tm, d), tok),
        out_shape=jax.ShapeDtypeStruct((b, t, d), F32),
        compiler_params=_cparams("parallel", "parallel"),
    )(*ins)


def _route_kernel(h_ref, rw_ref, rb_ref, o_ref):
    logits = lax.dot_general(rw_ref[...], h_ref[...].astype(F32), (((1,), (1,)), ((), ())),
                             precision=HI, preferred_element_type=F32)
    scores = _sigmoid(logits)
    sel = scores + rb_ref[...]
    rows = [sel[e:e + 1, :] for e in range(N_EXPERTS)]
    sc = [scores[e:e + 1, :] for e in range(N_EXPERTS)]

    def top2(vals):
        v1, i1 = vals[0], jnp.zeros(vals[0].shape, jnp.int32)
        for j in range(1, len(vals)):
            better = vals[j] > v1
            v1 = jnp.where(better, vals[j], v1)
            i1 = jnp.where(better, j, i1)
        v2 = jnp.where(i1 == 0, vals[1], vals[0])
        i2 = jnp.where(i1 == 0, 1, 0)
        for j in range(1, len(vals)):
            better = (vals[j] > v2) & (i1 != j)
            v2 = jnp.where(better, vals[j], v2)
            i2 = jnp.where(better, j, i2)
        return v1, i1, v2, i2

    gsum = []
    for grp in range(N_GROUPS):
        v1, _, v2, _ = top2(rows[grp * EXPERTS_PER_GROUP:(grp + 1) * EXPERTS_PER_GROUP])
        gsum.append(v1 + v2)
    best, gidx = gsum[0], jnp.zeros(gsum[0].shape, jnp.int32)
    for grp in range(1, N_GROUPS):
        better = gsum[grp] > best
        best = jnp.where(better, gsum[grp], best)
        gidx = jnp.where(better, grp, gidx)
    chosen, chosen_sc = [], []
    for j in range(EXPERTS_PER_GROUP):
        cj, sj = rows[j], sc[j]
        for grp in range(1, N_GROUPS):
            cj = jnp.where(gidx == grp, rows[grp * EXPERTS_PER_GROUP + j], cj)
            sj = jnp.where(gidx == grp, sc[grp * EXPERTS_PER_GROUP + j], sj)
        chosen.append(cj)
        chosen_sc.append(sj)
    _, i1, _, i2 = top2(chosen)
    w1, w2 = jnp.zeros_like(best), jnp.zeros_like(best)
    for j in range(EXPERTS_PER_GROUP):
        w1 = jnp.where(i1 == j, chosen_sc[j], w1)
        w2 = jnp.where(i2 == j, chosen_sc[j], w2)
    tot = w1 + w2
    w1, w2 = w1 / tot, w2 / tot
    e1 = gidx * EXPERTS_PER_GROUP + i1
    e2 = gidx * EXPERTS_PER_GROUP + i2
    eid = lax.broadcasted_iota(jnp.int32, scores.shape, 0)
    o_ref[...] = jnp.where(eid == e1, w1, 0.0) + jnp.where(eid == e2, w2, 0.0)


def moe_route(h, router_w, router_b, *, tm):
    m, d = h.shape
    tm = _pick_tile(m, tm)
    return pl.pallas_call(
        _route_kernel,
        grid=(m // tm,),
        in_specs=[pl.BlockSpec((tm, d), lambda i: (i, 0)),
                  pl.BlockSpec((N_EXPERTS, d), lambda i: (0, 0)),
                  pl.BlockSpec((N_EXPERTS, 1), lambda i: (0, 0))],
        out_specs=pl.BlockSpec((N_EXPERTS, tm), lambda i: (0, i)),
        out_shape=jax.ShapeDtypeStruct((N_EXPERTS, m), F32),
        compiler_params=_cparams("parallel"),
    )(h, router_w.T, router_b.reshape(N_EXPERTS, 1))


def _expert_kernel(h_ref, g_ref, wg_ref, wu_ref, wd_ref, x_ref, m_ref, o_ref, acc_ref, hb_ref,
                   *, bsz, t_all, n_ctx):
    e = pl.program_id(1)

    @pl.when(e == 0)
    def _():
        acc_ref[...] = jnp.zeros_like(acc_ref)
        hb_ref[...] = h_ref[...].astype(BF16)

    h = hb_ref[...]
    gates = g_ref[...]
    lane = lax.broadcasted_iota(jnp.int32, gates.shape, 1)
    ge = jnp.sum(jnp.where(lane == e, gates, 0.0), axis=1, keepdims=True)
    hid = _silu(_dot(h, wg_ref[0])) * _dot(h, wu_ref[0])
    acc_ref[...] += _dot(ge * hid, wd_ref[0])

    @pl.when(e == N_EXPERTS - 1)
    def _():
        tm = acc_ref.shape[0]
        row = pl.program_id(0) * tm + lax.broadcasted_iota(jnp.int32, (tm, 1), 0)
        mod = jnp.zeros(acc_ref.shape, F32)
        ctx = jnp.zeros((tm, 1), jnp.bool_)
        for bi in range(bsz):
            lo = bi * t_all
            ctx = ctx | ((row >= lo) & (row < lo + n_ctx))
            lat = (row >= lo + n_ctx) & (row < lo + t_all)
            mod = mod + jnp.where(lat, m_ref[bi:bi + 1, :], 0.0)
        mod = mod + jnp.where(ctx, m_ref[bsz:bsz + 1, :], 0.0)
        o_ref[...] = x_ref[...] + mod * acc_ref[...]


def moe_experts(h, gates, wg, wu, wd, x_all, gate_rows, *, n_ctx, tm=1024):
    b, t, d = x_all.shape
    m = b * t
    tm = _pick_tile(m, tm)
    tok = lambda i, e: (i, 0)
    kern = functools.partial(_expert_kernel, bsz=b, t_all=t, n_ctx=n_ctx)
    out = pl.pallas_call(
        kern,
        grid=(m // tm, N_EXPERTS),
        in_specs=[pl.BlockSpec((tm, d), tok),
                  pl.BlockSpec((tm, N_EXPERTS), tok),
                  pl.BlockSpec((1, d, EXPERT_FF), lambda i, e: (e, 0, 0)),
                  pl.BlockSpec((1, d, EXPERT_FF), lambda i, e: (e, 0, 0)),
                  pl.BlockSpec((1, EXPERT_FF, d), lambda i, e: (e, 0, 0)),
                  pl.BlockSpec((tm, d), tok),
                  _const_spec(gate_rows.shape)],
        out_specs=pl.BlockSpec((tm, d), tok),
        out_shape=jax.ShapeDtypeStruct((m, d), F32),
        scratch_shapes=[pltpu.VMEM((tm, d), F32), pltpu.VMEM((tm, d), BF16)],
        compiler_params=_cparams("parallel", "arbitrary"),
    )(h.reshape(m, d), gates, wg, wu, wd, x_all.reshape(m, d), gate_rows)
    return out.reshape(b, t, d)


def _final_norm_kernel(x_ref, w_ref, o_ref):
    x = x_ref[...]
    o_ref[...] = x * lax.rsqrt(jnp.mean(x * x, axis=-1, keepdims=True) + EPS) * w_ref[...]


def final_rms_norm(x, w, *, tm):
    m, d = x.shape
    return pl.pallas_call(
        _final_norm_kernel,
        grid=(m // tm,),
        in_specs=[pl.BlockSpec((tm, d), lambda i: (i, 0)), pl.BlockSpec((1, d), lambda i: (0, 0))],
        out_specs=pl.BlockSpec((tm, d), lambda i: (i, 0)),
        out_shape=jax.ShapeDtypeStruct((m, d), F32),
        compiler_params=_cparams("parallel"),
    )(x, w.reshape(1, d))


def _pack_w_in(w_in, mixer):
    cols = _SRC_COLS[mixer]
    return jnp.where(jnp.asarray(cols >= 0)[None, :], w_in[:, np.maximum(cols, 0)], 0.0).astype(BF16)


def mixer_scans(ps, lp, *, n_ctx):
    p_ssm, p_gla, p_rwkv, p_gdn = ps

    xs, bc, sm = ssm_prep(p_ssm, lp, n_ctx=n_ctx)
    neg_a = jnp.pad(-jnp.exp(lp["ssm_a_log"]).reshape(1, -1), ((0, 0), (0, LANES - 2 * SSM_HEADS)))
    ssm = tuple(ssd_scan(xs, bc, sm, neg_a, n_ctx=n_ctx)) + (xs, p_ssm)

    w2 = [jnp.zeros((LANES, GLA_HEADS * GLA_DK), F32).at[d * GLA_RANK:(d + 1) * GLA_RANK].set(lp["gla_w2"][d])
          for d in range(2)]
    gb = [_row(lp["gla_b"][d]) for d in range(2)]
    gla = tuple(gla_scan(p_gla, w2, gb, n_ctx=n_ctx)) + (p_gla,)

    r, k, v, kk, a, lw, g, bonus = rwkv_prep(p_rwkv, lp, n_ctx=n_ctx)
    rwkv = tuple(rwkv_scan(r, k, v, kk, a, lw, _row(lp["rwkv_k_a"]), n_ctx=n_ctx)) + (g, bonus)

    q, kd, vd, smd = gdn_prep(p_gdn, lp, n_ctx=n_ctx)
    gdn = tuple(gdn_scan(q, kd, vd, smd, n_ctx=n_ctx)) + (p_gdn,)
    return ssm, gla, rwkv, gdn


def _to_scan_order(t):
    b, n, d = t.shape
    return t.reshape(b, n // GRID_W, GRID_W, d).transpose(0, 2, 1, 3).reshape(b, n, d)


def _from_scan_order(t):
    b, n, d = t.shape
    return t.reshape(b, GRID_W, n // GRID_W, d).transpose(0, 2, 1, 3).reshape(b, n, d)


def kernel(x, c, ctx, c_ctx, ada_w, ada_b, norm_mix, norm_ffn, w_in, w_gate, w_branch, w_out, ssm_conv_w, ssm_conv_b, ssm_a_log, ssm_dt_bias, ssm_d, ssm_norm, gla_w2, gla_b, gla_norm, rwkv_mu, rwkv_w0, rwkv_w2, rwkv_a0, rwkv_a2, rwkv_g2, rwkv_k_k, rwkv_k_a, rwkv_r_k, rwkv_ln_w, rwkv_ln_b, gdn_conv_w, gdn_a_log, gdn_dt_bias, gdn_norm, router_w, router_b, moe_w_gate, moe_w_up, moe_w_down, final_norm):
    bsz, seq, d = x.shape
    n_ctx = ctx.shape[1]
    t_all = n_ctx + seq
    m_all = bsz * t_all

    cond = jnp.concatenate([jax.nn.silu(c), jax.nn.silu(c_ctx)[None]], 0)
    cond = jnp.pad(cond, ((0, SUBLANES - cond.shape[0]), (0, 0)))
    mods, mod_rows = [], []
    for l in range(DEPTH):
        mod = pmatmul(cond, ada_w[l], tm=SUBLANES, tn=1024, precise=True) + ada_b[l]
        mod_rows.append(mod)
        lat = mod[:bsz].reshape(bsz, 6, d)
        cx = jnp.broadcast_to(mod[bsz].reshape(1, 6, d), (bsz, 6, d))
        mods.append(jnp.stack([cx, lat], axis=1))

    x_all = jnp.concatenate([ctx, x], axis=1)
    scan_order = False
    for l in range(DEPTH):
        if (l % 2 == 1) != scan_order:
            reorder = _from_scan_order if scan_order else _to_scan_order
            x_all = jnp.concatenate([x_all[:, :n_ctx], reorder(x_all[:, n_ctx:])], axis=1)
            scan_order = not scan_order
        lp = dict(ssm_conv_w=ssm_conv_w[l], ssm_conv_b=ssm_conv_b[l], ssm_a_log=ssm_a_log[l],
                  ssm_dt_bias=ssm_dt_bias[l], ssm_d=ssm_d[l], ssm_norm=ssm_norm[l],
                  gla_w2=gla_w2[l], gla_b=gla_b[l], gla_norm=gla_norm[l],
                  rwkv_mu=rwkv_mu[l], rwkv_w0=rwkv_w0[l], rwkv_w2=rwkv_w2[l], rwkv_a0=rwkv_a0[l],
                  rwkv_a2=rwkv_a2[l], rwkv_g2=rwkv_g2[l], rwkv_k_k=rwkv_k_k[l], rwkv_k_a=rwkv_k_a[l],
                  rwkv_r_k=rwkv_r_k[l], rwkv_ln_w=rwkv_ln_w[l], rwkv_ln_b=rwkv_ln_b[l],
                  gdn_conv_w=gdn_conv_w[l], gdn_a_log=gdn_a_log[l], gdn_dt_bias=gdn_dt_bias[l],
                  gdn_norm=gdn_norm[l])
        mod = mods[l]
        msel = lambda i: mod[:, :, i][:, :, None, :]

        h = norm_modulate(x_all, norm_mix[l], msel(0), msel(1), n_ctx=n_ctx)
        h2d = h.reshape(m_all, d)
        ps = []
        for mixer in ("ssm", "gla", "rwkv", "gdn"):
            wp = _pack_w_in(w_in[l], mixer)
            ps.append(pmatmul(h2d, wp, tm=512, tn=wp.shape[1]).reshape(bsz, t_all, wp.shape[1]))
        wg_cat = jnp.concatenate([w_gate[l, i] for i in range(4)], axis=1).astype(BF16)
        gates = pmatmul(h2d, wg_cat, tm=1024, tn=1024, act="sigmoid", out_dtype=BF16)
        gates = gates.reshape(bsz, t_all, 4 * d)

        ssm, gla, rwkv, gdn = mixer_scans(ps, lp, n_ctx=n_ctx)
        x_all = merge_residual(ssm, gla, rwkv, gdn, gates, x_all, msel(2), lp,
                               w_branch[l].astype(BF16), w_out[l].astype(BF16), n_ctx=n_ctx)

        h = norm_modulate(x_all, norm_ffn[l], msel(3), msel(4), n_ctx=n_ctx, out_dtype=F32)
        gate = moe_route(h.reshape(m_all, d), router_w, router_b, tm=1024)
        x_all = moe_experts(h, gate.T, moe_w_gate[l].astype(BF16), moe_w_up[l].astype(BF16),
                            moe_w_down[l].astype(BF16), x_all, mod_rows[l][:, 5 * d:], n_ctx=n_ctx)

    lat = x_all[:, n_ctx:]
    if scan_order:
        lat = _from_scan_order(lat)
    return final_rms_norm(lat.reshape(bsz * seq, d), final_norm, tm=1024).reshape(bsz, seq, d)
```

```python
import functools
import itertools

import numpy as np
import jax
import jax.numpy as jnp
from jax import lax
from jax.experimental import pallas as pl
from jax.experimental.pallas import tpu as pltpu

F32 = jnp.float32
BF16 = jnp.bfloat16
HI = lax.Precision.HIGHEST

D_MODEL = 1024
DEPTH = 2
GRID_W = 64
CHUNK = 64
EPS = 1e-6
BRANCH = D_MODEL // 2
SSM_HEADS, SSM_P, SSM_GROUPS, SSM_N = 8, 64, 2, 64
GLA_HEADS, GLA_DK, GLA_DV, GLA_RANK, GLA_TAU = 4, 64, 128, 16, 16.0
RWKV_HEADS, RWKV_N, RWKV_LN_EPS = 8, 64, 64e-5
GDN_HEADS, GDN_N = 4, 128
N_EXPERTS, N_GROUPS, EXPERTS_PER_GROUP = 16, 4, 4
EXPERT_FF = D_MODEL // 2
LANES = 128
SUBLANES = 8
VMEM_LIMIT = 48 * 1024 * 1024
ROW_TILE = 256

_REF_BLOCKS = (
    ("ssm", "z", 512), ("ssm", "xbc", 768), ("ssm", "dt", 16),
    ("gla", "q", 256), ("gla", "k", 256), ("gla", "v", 512), ("gla", "r", 512), ("gla", "glr", 32),
    ("rwkv", "all", 1920),
    ("gdn", "qkv", 1536), ("gdn", "z", 512), ("gdn", "ab", 16),
)
_PACKED = {
    "ssm": (("z", 512), ("dt", 128), ("pad", 128), ("xbc", 768)),
    "gla": (("q", 256), ("k", 256), ("v", 512), ("r", 512), ("glr", 128)),
    "rwkv": (("all", 1920),),
    "gdn": (("qkv", 1536), ("z", 512), ("ab", 128)),
}


def _packed_columns():
    start, s = {}, 0
    for mixer, blk, w in _REF_BLOCKS:
        start[(mixer, blk)] = (s, w)
        s += w
    out = {}
    for mixer, blocks in _PACKED.items():
        cols = []
        for blk, wp in blocks:
            s0, w = start.get((mixer, blk), (0, 0))
            cols += list(range(s0, s0 + w)) + [-1] * (wp - w)
        out[mixer] = np.asarray(cols, np.int32)
    return out


_SRC_COLS = _packed_columns()


def _cparams(*sem):
    return pltpu.CompilerParams(dimension_semantics=sem, vmem_limit_bytes=VMEM_LIMIT)


def _dot(a, b):
    return jnp.dot(a.astype(BF16), b.astype(BF16), preferred_element_type=F32)


def _dot_nt(a, b):
    return lax.dot_general(a.astype(BF16), b.astype(BF16), (((1,), (1,)), ((), ())),
                           preferred_element_type=F32)


def _dot_tn(a, b):
    return lax.dot_general(a.astype(BF16), b.astype(BF16), (((0,), (0,)), ((), ())),
                           preferred_element_type=F32)


def _dot_hi(a, b):
    return jnp.dot(a, b, precision=HI, preferred_element_type=F32)


def _dot_x3(a, b):
    ah = a.astype(BF16)
    al = (a - ah.astype(F32)).astype(BF16)
    bh = b.astype(BF16)
    bl = (b - bh.astype(F32)).astype(BF16)
    f = lambda u, v: jnp.dot(u, v, preferred_element_type=F32)
    return f(ah, bh) + (f(ah, bl) + f(al, bh))


def _dot_x2(a, w):
    ah = a.astype(BF16)
    al = (a - ah.astype(F32)).astype(BF16)
    return jnp.dot(ah, w, preferred_element_type=F32) + jnp.dot(al, w, preferred_element_type=F32)


def _softplus(x):
    return jnp.maximum(x, 0.0) + jnp.log(1.0 + jnp.exp(-jnp.abs(x)))


def _sigmoid(x):
    return 1.0 / (1.0 + jnp.exp(-x))


def _silu(x):
    return x * _sigmoid(x)


def _pick_tile(m, pref):
    t = pref
    while m % t:
        t //= 2
    return t


def _block_diag_ones(n, width=BRANCH):
    idx = np.arange(width) // n
    return jnp.asarray(idx[:, None] == idx[None, :], BF16)


def _mm_kernel(a_ref, w_ref, o_ref, *, act, precise):
    if precise:
        r = _dot_hi(a_ref[...].astype(F32), w_ref[...].astype(F32))
    else:
        r = _dot(a_ref[...], w_ref[...])
    if act == "sigmoid":
        r = _sigmoid(r)
    o_ref[...] = r.astype(o_ref.dtype)


def pmatmul(a, w, *, tm, tn, act=None, precise=False, out_dtype=F32):
    m, k = a.shape
    n = w.shape[1]
    tm = _pick_tile(m, tm)
    assert tm % SUBLANES == 0 and n % tn == 0, (m, tm, n, tn)
    return pl.pallas_call(
        functools.partial(_mm_kernel, act=act, precise=precise),
        grid=(n // tn, m // tm),
        in_specs=[pl.BlockSpec((tm, k), lambda j, i: (i, 0)),
                  pl.BlockSpec((k, tn), lambda j, i: (0, j))],
        out_specs=pl.BlockSpec((tm, tn), lambda j, i: (i, j)),
        out_shape=jax.ShapeDtypeStruct((m, n), out_dtype),
        compiler_params=_cparams("parallel", "parallel"),
    )(a, w)


def _norm_mod_kernel(x_ref, w_ref, shift_ref, scale_ref, o_ref):
    x = x_ref[0]
    y = x * lax.rsqrt(jnp.mean(x * x, axis=-1, keepdims=True) + EPS) * w_ref[...]
    o_ref[0] = (y * (1.0 + scale_ref[0, 0]) + shift_ref[0, 0]).astype(o_ref.dtype)


def _mod_sel(nct):
    return lambda bi, i, *_: (bi, jnp.where(i < nct, 0, 1), 0, 0)


def norm_modulate(x_all, w, shift, scale, *, n_ctx, out_dtype=BF16):
    b, t, d = x_all.shape
    tm = _pick_tile(n_ctx, ROW_TILE)
    assert t % tm == 0
    tok = lambda bi, i: (bi, i, 0)
    return pl.pallas_call(
        _norm_mod_kernel,
        grid=(b, t // tm),
        in_specs=[pl.BlockSpec((1, tm, d), tok),
                  pl.BlockSpec((1, d), lambda bi, i: (0, 0)),
                  pl.BlockSpec((1, 1, 1, d), _mod_sel(n_ctx // tm)),
                  pl.BlockSpec((1, 1, 1, d), _mod_sel(n_ctx // tm))],
        out_specs=pl.BlockSpec((1, tm, d), tok),
        out_shape=jax.ShapeDtypeStruct((b, t, d), out_dtype),
        compiler_params=_cparams("parallel", "parallel"),
    )(x_all, w.reshape(1, d), shift, scale)


def _row(v):
    return v.reshape(1, -1).astype(F32)


def _const_spec(shape):
    return pl.BlockSpec(shape, lambda *_: (0,) * len(shape))


def _tile_specs(tt, width, col):
    r8 = tt // SUBLANES
    main = pl.BlockSpec((1, tt, width), lambda bi, i: (bi, i, col))
    prev = pl.BlockSpec((1, SUBLANES, width), lambda bi, i: (bi, jnp.maximum(i * r8 - 1, 0), col))
    return main, prev, r8


def _halo_specs(tt, width, col, t):
    main, prev, r8 = _tile_specs(tt, width, col)
    last8 = t // SUBLANES - 1
    nxt = pl.BlockSpec((1, SUBLANES, width), lambda bi, i: (bi, jnp.minimum((i + 1) * r8, last8), col))
    return [main, prev, nxt]


def _neighbours(x, prev8, next8, *, nct, nt):
    i = pl.program_id(1)
    tt = x.shape[0]
    row = lax.broadcasted_iota(jnp.int32, x.shape, 0)
    first = (i == 0) | (i == nct)
    last = (i == nct - 1) | (i == nt - 1)
    pr = jnp.where(first, 0.0, prev8[SUBLANES - 1:SUBLANES, :])
    nx = jnp.where(last, 0.0, next8[0:1, :])
    xp = jnp.where(row == 0, pr, pltpu.roll(x, 1, 0))
    xn = jnp.where(row == tt - 1, nx, pltpu.roll(x, tt - 1, 0))
    return xp, xn


def _prep_call(kernel, ins, in_specs, out_widths, *, b, t, tt, out_dtype=F32):
    tok = lambda bi, i: (bi, i, 0)
    return pl.pallas_call(
        kernel,
        grid=(b, t // tt),
        in_specs=in_specs,
        out_specs=[pl.BlockSpec((1, tt, w), tok) for w in out_widths],
        out_shape=[jax.ShapeDtypeStruct((b, t, w), out_dtype) for w in out_widths],
        compiler_params=_cparams("parallel", "parallel"),
    )(*ins)


def _chunk_masks(reverse):
    row = lax.broadcasted_iota(jnp.int32, (CHUNK, CHUNK), 0)
    col = lax.broadcasted_iota(jnp.int32, (CHUNK, CHUNK), 1)
    if reverse:
        return col >= row, col > row
    return col <= row, col < row


def _chunk_order(i, n_ctx_chunks, n_chunks, reverse):
    if not reverse:
        return i
    return jnp.where(i < n_ctx_chunks, n_ctx_chunks - 1 - i, n_chunks - 1 - (i - n_ctx_chunks))


def _split3(a):
    hi = a.astype(BF16)
    r = a - hi.astype(F32)
    mid = r.astype(BF16)
    return hi, mid, (r - mid.astype(F32)).astype(BF16)


def _transpose_small(x):
    row = lax.broadcasted_iota(jnp.int32, (LANES, LANES), 0)
    col = lax.broadcasted_iota(jnp.int32, (LANES, LANES), 1)
    eye = (row == col).astype(BF16)
    nt = lambda p: lax.dot_general(eye, p, (((1,), (1,)), ((), ())), preferred_element_type=F32)
    hi, mid, lo = _split3(x)
    return nt(hi) + (nt(mid) + nt(lo))


def _chunk_cumsum(incl, x):
    m = incl.astype(BF16)
    hi, mid, lo = _split3(x)
    f = lambda p: jnp.dot(m, p, preferred_element_type=F32)
    return f(hi) + (f(mid) + f(lo))


def _select_columns(x, sel):
    c = x.shape[0]
    y = jnp.dot(jnp.concatenate(_split3(x), axis=0), sel, preferred_element_type=F32)
    return y[:c] + (y[c:2 * c] + y[2 * c:])


def _unit_tri_solve(mats, rhs, precise_levels=0):
    n = range(len(mats))
    x = [rhs[h] - _dot_x3(mats[h], rhs[h]) for h in n]
    yield
    p = mats
    for level in range(int(np.log2(CHUNK)) - 1):
        dot = _dot_x3 if level < precise_levels else _dot
        p = [dot(p[h], p[h]) for h in n]
        yield
        x = [x[h] + dot(p[h], x[h]) for h in n]
        yield
    return x


def _bidir_scan(body, tok_ins, const_ins, state_shape, *, b, t, n_ctx, lockstep=True, batch_block=1):
    nc, ncc = t // CHUNK, n_ctx // CHUNK
    nb = batch_block
    assert b % nb == 0

    def chunk_spec(width, col, reverse):
        return pl.BlockSpec((nb, CHUNK, width), lambda bi, i: (bi, _chunk_order(i, ncc, nc, reverse), col))

    def direction(reverse):
        d = int(reverse)
        specs = [chunk_spec(w, cols[d], reverse) for _, w, *cols in tok_ins]
        specs += [_const_spec(pair[d].shape) for pair in const_ins]
        return specs, [a for a, *_ in tok_ins] + [pair[d] for pair in const_ins]

    (spec_f, arg_f), (spec_b, arg_b) = direction(False), direction(True)
    n_tok, n_in = len(tok_ins), len(arg_f)

    def kern(*refs):
        o_f, o_b, s_f, s_b = refs[2 * n_in:]

        @pl.when(pl.program_id(1) == 0)
        def _():
            s_f[...] = jnp.zeros_like(s_f)
            s_b[...] = jnp.zeros_like(s_b)

        def one(j, ins, o_ref, s_ref, reverse):
            ins = [r.at[pl.ds(j, 1)] if k < n_tok else r for k, r in enumerate(ins)]
            return body(*ins, o_ref.at[pl.ds(j, 1)], s_ref.at[j], reverse=reverse)

        gens = []
        for j in range(nb):
            gens += [one(j, refs[:n_in], o_f, s_f, False), one(j, refs[n_in:2 * n_in], o_b, s_b, True)]
        if not lockstep:
            gens = [itertools.chain(*gens)]
        while gens:
            gens = [g for g in gens if next(g, _DONE) is not _DONE]

    return pl.pallas_call(
        kern,
        grid=(b // nb, nc),
        in_specs=spec_f + spec_b,
        out_specs=[chunk_spec(BRANCH, 0, False), chunk_spec(BRANCH, 0, True)],
        out_shape=[jax.ShapeDtypeStruct((b, t, BRANCH), F32)] * 2,
        scratch_shapes=[pltpu.VMEM((nb,) + tuple(state_shape), F32)] * 2,
        compiler_params=_cparams("parallel", "arbitrary"),
    )(*arg_f, *arg_b)


_DONE = object()


def _batch_block(b, pref):
    return pref if b % pref == 0 else 1


def _ssm_prep_kernel(x_ref, xp_ref, xn_ref, dt_ref, cw_ref, cb_ref, dtb_ref, xs_ref, bc_ref, sm_ref,
                     *, nct, nt):
    x = x_ref[0]
    xp, xn = _neighbours(x, xp_ref[0], xn_ref[0], nct=nct, nt=nt)
    y = _silu(xp * cw_ref[0:1, :] + x * cw_ref[1:2, :] + xn * cw_ref[2:3, :] + cb_ref[...])
    xs_ref[0] = y[:, :BRANCH]
    bc_ref[0] = y[:, BRANCH:]
    sm_ref[0] = _softplus(dt_ref[0] + dtb_ref[...])


def ssm_prep(p, lp, *, n_ctx):
    b, t, _ = p.shape
    tt = _pick_tile(n_ctx, ROW_TILE)
    dtb = jnp.pad(lp["ssm_dt_bias"].reshape(1, -1), ((0, 0), (0, LANES - 2 * SSM_HEADS)))
    specs = _halo_specs(tt, 768, 1, t) + [pl.BlockSpec((1, tt, LANES), lambda bi, i: (bi, i, 4)),
                                          _const_spec((3, 768)), _const_spec((1, 768)), _const_spec((1, LANES))]
    kern = functools.partial(_ssm_prep_kernel, nct=n_ctx // tt, nt=t // tt)
    return _prep_call(kern, (p, p, p, p, lp["ssm_conv_w"], _row(lp["ssm_conv_b"]), dtb), specs,
                      (BRANCH, 2 * SSM_GROUPS * SSM_N, LANES), b=b, t=t, tt=tt)


def _ssd_body(x_ref, bc_ref, sm_ref, na_ref, o_ref, s_ref, *, reverse):
    incl, _ = _chunk_masks(reverse)
    last = 0 if reverse else CHUNK - 1
    off = SSM_HEADS if reverse else 0
    dt_all = sm_ref[0]
    g_all = _chunk_cumsum(incl, dt_all * na_ref[...])
    yield
    expand = _expand_matrix(off, SSM_HEADS, SSM_P)
    gx = _select_columns(g_all, expand)
    dx = _select_columns(dt_all, expand)
    gt_all = _transpose_small(g_all)
    dtt_all = _transpose_small(dt_all)
    yield
    heads = range(SSM_HEADS)
    rep = SSM_HEADS // SSM_GROUPS
    gw = SSM_GROUPS * SSM_N
    pw = rep * SSM_P
    hs = [slice(h * SSM_P, (h + 1) * SSM_P) for h in heads]
    glx = gx[last:last + 1, :]
    egx = jnp.exp(gx)
    wx = dx * jnp.exp(glx - gx)
    eglx = jnp.exp(glx)
    x = x_ref[0]
    bm = [bc_ref[0, :, grp * SSM_N:(grp + 1) * SSM_N] for grp in range(SSM_GROUPS)]
    cm = [bc_ref[0, :, gw + grp * SSM_N:gw + (grp + 1) * SSM_N] for grp in range(SSM_GROUPS)]
    cb = [_dot_nt(cm[grp], bm[grp]) for grp in range(SSM_GROUPS)]
    s = [s_ref[grp] for grp in range(SSM_GROUPS)]
    yield
    scores = [cb[h // rep] * jnp.exp(jnp.where(incl, gx[:, hs[h]] - gt_all[off + h:off + h + 1, :], -jnp.inf))
              * dtt_all[off + h:off + h + 1, :] for h in heads]
    yield
    intra = [_dot(scores[h], x[:, hs[h]]) for h in heads]
    yield
    inter = [_dot(cm[grp], s[grp]) for grp in range(SSM_GROUPS)]
    yield
    upd = [_dot_tn(bm[h // rep] * wx[:, hs[h]], x[:, hs[h]]) for h in heads]
    yield
    for h in heads:
        grp, ls = h // rep, slice((h % rep) * SSM_P, (h % rep + 1) * SSM_P)
        o_ref[0, :, hs[h]] = intra[h] + egx[:, hs[h]] * inter[grp][:, ls]
        s_ref[grp, :, ls] = s[grp][:, ls] * eglx[:, hs[h]] + upd[h]


def _expand_matrix(off, n_heads, width):
    row = lax.broadcasted_iota(jnp.int32, (LANES, n_heads * width), 0)
    col = lax.broadcasted_iota(jnp.int32, (LANES, n_heads * width), 1)
    lo = row * width - off * width
    return ((col >= lo) & (col < lo + width)).astype(BF16)


def ssd_scan(xs, bc, sm, neg_a, *, n_ctx):
    b, t, _ = xs.shape
    toks = [(xs, BRANCH, 0, 0), (bc, 2 * SSM_GROUPS * SSM_N, 0, 0), (sm, LANES, 0, 0)]
    state = (SSM_GROUPS, SSM_N, (SSM_HEADS // SSM_GROUPS) * SSM_P)
    return _bidir_scan(_ssd_body, toks, [(neg_a, neg_a)], state, b=b, t=t, n_ctx=n_ctx,
                       batch_block=_batch_block(b, 2))


def _gla_body(q_ref, k_ref, v_ref, glr_ref, w2_ref, gb_ref, o_ref, s_ref, *, reverse):
    incl, _ = _chunk_masks(reverse)
    last = 0 if reverse else CHUNK - 1
    logit = _dot_x3(glr_ref[0], w2_ref[...]) + gb_ref[...]
    yield
    la = -_softplus(-logit) * (1.0 / GLA_TAU)
    g_all = _chunk_cumsum(incl, la)
    yield
    heads = range(GLA_HEADS)
    ks = [slice(h * GLA_DK, (h + 1) * GLA_DK) for h in heads]
    vs = [slice(h * GLA_DV, (h + 1) * GLA_DV) for h in heads]
    g = [g_all[:, ks[h]] for h in heads]
    gl = [g[h][last:last + 1, :] for h in heads]
    k = [k_ref[0, :, ks[h]] for h in heads]
    v = [v_ref[0, :, vs[h]] for h in heads]
    qg = [q_ref[0, :, ks[h]] * (GLA_DK ** -0.5) * jnp.exp(g[h]) for h in heads]
    st = [s_ref[h] for h in heads]
    yield
    scores = [jnp.where(incl, _dot_nt(qg[h], k[h] * jnp.exp(-g[h])), 0.0) for h in heads]
    yield
    intra = [_dot(scores[h], v[h]) for h in heads]
    yield
    inter = [_dot_nt(qg[h], st[h]) for h in heads]
    yield
    upd = [_dot_tn(v[h], k[h] * jnp.exp(gl[h] - g[h])) for h in heads]
    yield
    for h in heads:
        o_ref[0, :, vs[h]] = intra[h] + inter[h]
        s_ref[h] = st[h] * jnp.exp(gl[h]) + upd[h]


def gla_scan(p, w2_pair, gb_pair, *, n_ctx):
    b, t, _ = p.shape
    kwid = GLA_HEADS * GLA_DK
    toks = [(p, kwid, 0, 0), (p, kwid, 1, 1), (p, BRANCH, 1, 1), (p, LANES, 12, 12)]
    return _bidir_scan(_gla_body, toks, [w2_pair, gb_pair], (GLA_HEADS, GLA_DV, GLA_DK), b=b, t=t, n_ctx=n_ctx,
                       batch_block=_batch_block(b, 4))


def _rwkv_prep_kernel(x_ref, xp_ref, xn_ref, mu_ref, w2_ref, w0_ref, a2_ref, a0_ref, g2_ref, kk_ref_w,
                      ka_ref, rk_ref, bd_ref, r_ref, k_ref, v_ref, kk_ref, a_ref, lw_ref, g_ref, bo_ref,
                      *, nct, nt):
    x = x_ref[0]
    xp, xn = _neighbours(x, xp_ref[0], xn_ref[0], nct=nct, nt=nt)
    x = x + mu_ref[...] * (0.5 * (xp + xn) - x)
    r, k, v = x[:, :BRANCH], x[:, BRANCH:2 * BRANCH], x[:, 2 * BRANCH:3 * BRANCH]
    wlr = x[:, 3 * BRANCH:3 * BRANCH + LANES]
    alr = x[:, 3 * BRANCH + LANES:3 * BRANCH + 2 * LANES]
    glr = x[:, 3 * BRANCH + 2 * LANES:]
    w_raw = _dot_x3(jnp.tanh(wlr), w2_ref[...]) + w0_ref[...]
    lw_ref[0] = -jnp.exp(-_softplus(-w_raw) - 0.5)
    a = _sigmoid(_dot_x3(alr, a2_ref[...]) + a0_ref[...])
    a_ref[0] = a
    g_ref[0] = _dot_x3(_sigmoid(glr), g2_ref[...])
    kk = k * kk_ref_w[...]
    kk_ref[0] = kk * lax.rsqrt(_dot_x2(kk * kk, bd_ref[...]) + EPS)
    ksum = k * (2.0 + (a[:, :BRANCH] + a[:, BRANCH:] - 2.0) * ka_ref[...])
    bo_ref[0] = _dot_x2(r * ksum * rk_ref[...], bd_ref[...]) * v
    r_ref[0] = r
    k_ref[0] = k
    v_ref[0] = v


def rwkv_prep(p, lp, *, n_ctx):
    b, t, w = p.shape
    tt = _pick_tile(n_ctx, ROW_TILE)

    def pair(wp):
        r, c = wp.shape[1:]
        return jnp.zeros((LANES, 2 * c), F32).at[:r, :c].set(wp[0]).at[r:2 * r, c:].set(wp[1])

    consts = (_row(lp["rwkv_mu"]), pair(lp["rwkv_w2"]), _row(lp["rwkv_w0"]), pair(lp["rwkv_a2"]),
              _row(lp["rwkv_a0"]), lp["rwkv_g2"], _row(lp["rwkv_k_k"]), _row(lp["rwkv_k_a"]),
              _row(lp["rwkv_r_k"]), _block_diag_ones(RWKV_N))
    specs = _halo_specs(tt, w, 0, t) + [_const_spec(c.shape) for c in consts]
    kern = functools.partial(_rwkv_prep_kernel, nct=n_ctx // tt, nt=t // tt)
    return _prep_call(kern, (p, p, p) + consts, specs,
                      (BRANCH, BRANCH, BRANCH, BRANCH, 2 * BRANCH, 2 * BRANCH, BRANCH, BRANCH),
                      b=b, t=t, tt=tt)


def _rwkv_body(r_ref, k_ref, v_ref, kk_ref, a_ref, lw_ref, ka_ref, o_ref, s_ref, *, reverse):
    incl, strict = _chunk_masks(reverse)
    last = 0 if reverse else CHUNK - 1
    lw_all = lw_ref[0]
    g_all = _chunk_cumsum(incl, lw_all)
    a_all = a_ref[0]
    k_all = k_ref[0] * (1.0 + (a_all - 1.0) * ka_ref[...])
    yield
    heads = range(RWKV_HEADS)
    hs = [slice(h * RWKV_N, (h + 1) * RWKV_N) for h in heads]
    g = [g_all[:, hs[h]] for h in heads]
    gl = [g[h][last:last + 1, :] for h in heads]
    eneg = [jnp.exp(-g[h]) for h in heads]
    edec = [jnp.exp(gl[h] - g[h]) for h in heads]
    kk = [kk_ref[0, :, hs[h]] for h in heads]
    bvec = [kk[h] * a_all[:, hs[h]] for h in heads]
    k = [k_all[:, hs[h]] for h in heads]
    v = [v_ref[0, :, hs[h]] for h in heads]
    kkg = [kk[h] * jnp.exp(g[h] - lw_all[:, hs[h]]) for h in heads]
    rg = [r_ref[0, :, hs[h]] * jnp.exp(g[h]) for h in heads]
    bh = [bvec[h] * eneg[h] for h in heads]
    kh = [k[h] * eneg[h] for h in heads]
    s = [s_ref[h] for h in heads]
    yield
    both = [jnp.concatenate([kkg[h], rg[h]], axis=0) for h in heads]
    mask2 = jnp.concatenate([strict, incl], axis=0)
    mb = [jnp.where(mask2, _dot_nt(both[h], bh[h]), 0.0) for h in heads]
    yield
    mk = [jnp.where(mask2, _dot_nt(both[h], kh[h]), 0.0) for h in heads]
    yield
    part = [_dot(mk[h], v[h]) + _dot_nt(both[h], s[h]) for h in heads]
    yield
    x = yield from _unit_tri_solve([mb[h][:CHUNK] for h in heads], [part[h][:CHUNK] for h in heads])
    u = [-xh for xh in x]
    for h in heads:
        o_ref[0, :, hs[h]] = part[h][CHUNK:] + _dot(mb[h][CHUNK:], u[h])
    yield
    for h in heads:
        upd = _dot_tn(jnp.concatenate([u[h], v[h]], axis=0),
                      jnp.concatenate([bvec[h] * edec[h], k[h] * edec[h]], axis=0))
        s_ref[h] = s[h] * jnp.exp(gl[h]) + upd


def rwkv_scan(r, k, v, kk, a, lw, k_a, *, n_ctx):
    b, t, _ = r.shape
    toks = [(r, BRANCH, 0, 0), (k, BRANCH, 0, 0), (v, BRANCH, 0, 0), (kk, BRANCH, 0, 0),
            (a, BRANCH, 0, 1), (lw, BRANCH, 0, 1)]
    return _bidir_scan(_rwkv_body, toks, [(k_a, k_a)], (RWKV_HEADS, RWKV_N, RWKV_N), b=b, t=t, n_ctx=n_ctx,
                       batch_block=_batch_block(b, 2))


def _gdn_prep_kernel(x_ref, xp_ref, xn_ref, ab_ref, cw_ref, na_ref, dtb_ref, bd_ref,
                     q_ref, k_ref, v_ref, sm_ref, *, nct, nt):
    x = x_ref[0]
    xp, xn = _neighbours(x, xp_ref[0], xn_ref[0], nct=nct, nt=nt)
    y = _silu(xp * cw_ref[0:1, :] + x * cw_ref[1:2, :] + xn * cw_ref[2:3, :])
    q, k = y[:, :BRANCH], y[:, BRANCH:2 * BRANCH]
    q_ref[0] = q * lax.rsqrt(_dot_x2(q * q, bd_ref[...]) + EPS) * (GDN_N ** -0.5)
    k_ref[0] = k * lax.rsqrt(_dot_x2(k * k, bd_ref[...]) + EPS)
    v_ref[0] = y[:, 2 * BRANCH:]
    ab = ab_ref[0]
    lane = lax.broadcasted_iota(jnp.int32, ab.shape, 1)
    sm_ref[0] = jnp.where(lane < 2 * GDN_HEADS, na_ref[...] * _softplus(ab + dtb_ref[...]), _sigmoid(ab))


def gdn_prep(p, lp, *, n_ctx):
    b, t, _ = p.shape
    tt = _pick_tile(n_ctx, ROW_TILE)
    padrow = lambda v: jnp.pad(v.reshape(1, -1), ((0, 0), (0, LANES - 2 * GDN_HEADS)))
    consts = (lp["gdn_conv_w"], padrow(-jnp.exp(lp["gdn_a_log"])), padrow(lp["gdn_dt_bias"]),
              _block_diag_ones(GDN_N))
    specs = (_halo_specs(tt, 3 * BRANCH, 0, t) + [pl.BlockSpec((1, tt, LANES), lambda bi, i: (bi, i, 16))]
             + [_const_spec(c.shape) for c in consts])
    kern = functools.partial(_gdn_prep_kernel, nct=n_ctx // tt, nt=t // tt)
    return _prep_call(kern, (p, p, p, p) + consts, specs, (BRANCH, BRANCH, BRANCH, LANES), b=b, t=t, tt=tt)


def _gdn_body(q_ref, k_ref, v_ref, sm_ref, o_ref, s_ref, *, reverse):
    incl, strict = _chunk_masks(reverse)
    last = 0 if reverse else CHUNK - 1
    off = GDN_HEADS if reverse else 0
    sm = sm_ref[0]
    g_all = _chunk_cumsum(incl, sm)
    yield
    gt_all = _transpose_small(g_all)
    yield
    heads = range(GDN_HEADS)
    hs = [slice(h * GDN_N, (h + 1) * GDN_N) for h in heads]
    g = [g_all[:, off + h:off + h + 1] for h in heads]
    gl = [g[h][last:last + 1, :] for h in heads]
    beta = [sm[:, 2 * GDN_HEADS + off + h:2 * GDN_HEADS + off + h + 1] for h in heads]
    q = [q_ref[0, :, hs[h]] for h in heads]
    k = [k_ref[0, :, hs[h]] for h in heads]
    v = [v_ref[0, :, hs[h]] for h in heads]
    s = [s_ref[h] for h in heads]
    decay = [jnp.exp(jnp.where(incl, g[h] - gt_all[off + h:off + h + 1, :], -jnp.inf)) for h in heads]
    yield
    kq = [_dot_nt(jnp.concatenate([k[h], q[h]], axis=0), k[h]) for h in heads]
    yield
    lower = [jnp.where(strict, kq[h][:CHUNK] * decay[h] * beta[h], 0.0) for h in heads]
    attn = [kq[h][CHUNK:] * decay[h] for h in heads]
    o_part = [_dot(q[h] * jnp.exp(g[h]), s[h]) for h in heads]
    yield
    rhs = [jnp.concatenate([v[h] * beta[h], k[h] * (beta[h] * jnp.exp(g[h]))], axis=1) for h in heads]
    sol = yield from _unit_tri_solve(lower, rhs, precise_levels=2)
    v_new = [sol[h][:, :GDN_N] - _dot(sol[h][:, GDN_N:], s[h]) for h in heads]
    yield
    for h in heads:
        o_ref[0, :, hs[h]] = o_part[h] + _dot(attn[h], v_new[h])
    yield
    for h in heads:
        s_ref[h] = s[h] * jnp.exp(gl[h]) + _dot_tn(k[h] * jnp.exp(gl[h] - g[h]), v_new[h])


def gdn_scan(q, k, v, sm, *, n_ctx):
    b, t, _ = q.shape
    toks = [(q, BRANCH, 0, 0), (k, BRANCH, 0, 0), (v, BRANCH, 0, 0), (sm, LANES, 0, 0)]
    return _bidir_scan(_gdn_body, toks, [], (GDN_HEADS, GDN_N, GDN_N), b=b, t=t, n_ctx=n_ctx,
                       batch_block=_batch_block(b, 4))


def _merge_kernel(sf_ref, sb_ref, sx_ref, sz_ref, gf_ref, gb_ref, gr_ref, rf_ref, rb_ref, rg_ref, rbo_ref,
                  df_ref, db_ref, dz_ref, gate_ref, x_ref, m_ref,
                  sd_ref, sn_ref, gn_ref, lnw_ref, lnb_ref, dn_ref, bd64_ref, bd128_ref, bd256_ref,
                  wb_ref, wo_ref, o_ref):
    def group_rms(y, bd_ref, n, w_ref):
        return y * lax.rsqrt(_dot_x2(y * y, bd_ref[...]) * (1.0 / n) + EPS) * w_ref[...]

    y = (sf_ref[0] + sb_ref[0] + sd_ref[...] * sx_ref[0]) * _silu(sz_ref[0])
    ys = group_rms(y, bd256_ref, BRANCH // SSM_GROUPS, sn_ref)
    yg = group_rms(gf_ref[0] + gb_ref[0], bd128_ref, GLA_DV, gn_ref) * _silu(gr_ref[0])
    y = rf_ref[0] + rb_ref[0]
    yc = y - _dot_x2(y, bd64_ref[...]) * (1.0 / RWKV_N)
    var = _dot_x2(yc * yc, bd64_ref[...]) * (1.0 / RWKV_N)
    yr = (yc * lax.rsqrt(var + RWKV_LN_EPS) * lnw_ref[...] + lnb_ref[...] + rbo_ref[0]) * rg_ref[0]
    yd = group_rms(df_ref[0] + db_ref[0], bd128_ref, GDN_N, dn_ref) * _silu(dz_ref[0])
    acc = None
    for i, yi in enumerate((ys, yg, yr, yd)):
        term = gate_ref[0, :, i * D_MODEL:(i + 1) * D_MODEL].astype(F32) * _dot(yi, wb_ref[i])
        acc = term if acc is None else acc + term
    o_ref[0] = x_ref[0] + m_ref[0, 0] * _dot(acc, wo_ref[...])


def merge_residual(ssm, gla, rwkv, gdn, gates, x_all, gate_mod, lp, w_branch, w_out, *, n_ctx):
    b, t, d = x_all.shape
    tm = _pick_tile(n_ctx, ROW_TILE)
    tok = lambda bi, i: (bi, i, 0)
    blk = lambda c: pl.BlockSpec((1, tm, BRANCH), lambda bi, i: (bi, i, c))
    half = blk(0)
    consts = (_row(jnp.repeat(lp["ssm_d"], SSM_P)), _row(lp["ssm_norm"]),
              _row(jnp.tile(lp["gla_norm"], GLA_HEADS)), _row(lp["rwkv_ln_w"]), _row(lp["rwkv_ln_b"]),
              _row(jnp.tile(lp["gdn_norm"], GDN_HEADS)),
              _block_diag_ones(RWKV_N), _block_diag_ones(LANES), _block_diag_ones(BRANCH // SSM_GROUPS),
              w_branch, w_out)
    ins = (ssm[0], ssm[1], ssm[2], ssm[3], gla[0], gla[1], gla[2], rwkv[0], rwkv[1], rwkv[2], rwkv[3],
           gdn[0], gdn[1], gdn[2], gates, x_all, gate_mod) + consts
    specs = ([half, half, half, blk(0), half, half, blk(2), half, half, half, half, half, half, blk(3),
              pl.BlockSpec((1, tm, 4 * d), tok), pl.BlockSpec((1, tm, d), tok),
              pl.BlockSpec((1, 1, 1, d), _mod_sel(n_ctx // tm))]
             + [_const_spec(c.shape) for c in consts])
    return pl.pallas_call(
        _merge_kernel,
        grid=(b, t // tm),
        in_specs=specs,
        out_specs=pl.BlockSpec((1, tm, d), tok),
        out_shape=jax.ShapeDtypeStruct((b, t, d), F32),
        compiler_params=_cparams("parallel", "parallel"),
    )(*ins)


def _route_kernel(h_ref, rw_ref, rb_ref, u_ref, o_ref, cnt_ref, hb_ref):
    logits = lax.dot_general(rw_ref[...], h_ref[...].astype(F32), (((1,), (1,)), ((), ())),
                             precision=HI, preferred_element_type=F32)
    scores = _sigmoid(logits)
    sel = scores + rb_ref[...]
    rows = [sel[e:e + 1, :] for e in range(N_EXPERTS)]
    sc = [scores[e:e + 1, :] for e in range(N_EXPERTS)]

    def top2(vals):
        v1, i1 = vals[0], jnp.zeros(vals[0].shape, jnp.int32)
        for j in range(1, len(vals)):
            better = vals[j] > v1
            v1 = jnp.where(better, vals[j], v1)
            i1 = jnp.where(better, j, i1)
        v2 = jnp.where(i1 == 0, vals[1], vals[0])
        i2 = jnp.where(i1 == 0, 1, 0)
        for j in range(1, len(vals)):
            better = (vals[j] > v2) & (i1 != j)
            v2 = jnp.where(better, vals[j], v2)
            i2 = jnp.where(better, j, i2)
        return v1, i1, v2, i2

    gsum = []
    for grp in range(N_GROUPS):
        v1, _, v2, _ = top2(rows[grp * EXPERTS_PER_GROUP:(grp + 1) * EXPERTS_PER_GROUP])
        gsum.append(v1 + v2)
    best, gidx = gsum[0], jnp.zeros(gsum[0].shape, jnp.int32)
    for grp in range(1, N_GROUPS):
        better = gsum[grp] > best
        best = jnp.where(better, gsum[grp], best)
        gidx = jnp.where(better, grp, gidx)
    chosen, chosen_sc = [], []
    for j in range(EXPERTS_PER_GROUP):
        cj, sj = rows[j], sc[j]
        for grp in range(1, N_GROUPS):
            cj = jnp.where(gidx == grp, rows[grp * EXPERTS_PER_GROUP + j], cj)
            sj = jnp.where(gidx == grp, sc[grp * EXPERTS_PER_GROUP + j], sj)
        chosen.append(cj)
        chosen_sc.append(sj)
    _, i1, _, i2 = top2(chosen)
    w1, w2 = jnp.zeros_like(best), jnp.zeros_like(best)
    for j in range(EXPERTS_PER_GROUP):
        w1 = jnp.where(i1 == j, chosen_sc[j], w1)
        w2 = jnp.where(i2 == j, chosen_sc[j], w2)
    tot = w1 + w2
    w1, w2 = w1 / tot, w2 / tot
    tm = scores.shape[1]
    sub = lax.broadcasted_iota(jnp.int32, (SUBLANES, tm), 0)
    ind8 = jnp.zeros((SUBLANES, tm), F32)
    meta = jnp.zeros((SUBLANES, tm), F32)
    for j in range(EXPERTS_PER_GROUP):
        gate_j = jnp.where(i1 == j, w1, 0.0) + jnp.where(i2 == j, w2, 0.0)
        meta = jnp.where(sub == j, gate_j, meta)
    for grp in range(N_GROUPS):
        ind8 = jnp.where((sub == grp) & (gidx == grp), 1.0, ind8)
    before = jnp.dot(ind8.astype(BF16), u_ref[...], preferred_element_type=F32)
    rank = jnp.sum(ind8 * before, axis=0, keepdims=True)
    meta = jnp.where(sub == _META_GROUP, gidx.astype(F32), meta)
    meta = jnp.where(sub == _META_RANK, rank, meta)
    o_ref[...] = meta
    counts = jnp.sum(ind8, axis=1, keepdims=True)
    lane = lax.broadcasted_iota(jnp.int32, (SUBLANES, LANES), 1)
    row = lax.broadcasted_iota(jnp.int32, (SUBLANES, LANES), 0)
    cnt_ref[0] = jnp.broadcast_to(jnp.sum(jnp.where(lane == row, counts, 0.0), axis=0, keepdims=True),
                                  (SUBLANES, LANES)).astype(jnp.int32)
    hb_ref[...] = h_ref[...].astype(BF16)


_META_GROUP, _META_RANK = EXPERTS_PER_GROUP, EXPERTS_PER_GROUP + 1


def moe_route(h, router_w, router_b, *, tm):
    m, d = h.shape
    upper = jnp.asarray(np.triu(np.ones((tm, tm), np.float32), 1), BF16)
    return pl.pallas_call(
        _route_kernel,
        grid=(m // tm,),
        in_specs=[pl.BlockSpec((tm, d), lambda i: (i, 0)),
                  pl.BlockSpec((N_EXPERTS, d), lambda i: (0, 0)),
                  pl.BlockSpec((N_EXPERTS, 1), lambda i: (0, 0)),
                  _const_spec((tm, tm))],
        out_specs=[pl.BlockSpec((SUBLANES, tm), lambda i: (0, i)),
                   pl.BlockSpec((1, SUBLANES, LANES), lambda i: (i, 0, 0)),
                   pl.BlockSpec((tm, d), lambda i: (i, 0))],
        out_shape=[jax.ShapeDtypeStruct((SUBLANES, m), F32),
                   jax.ShapeDtypeStruct((m // tm, SUBLANES, LANES), jnp.int32),
                   jax.ShapeDtypeStruct((m, d), BF16)],
        compiler_params=_cparams("parallel"),
    )(h, router_w.T, router_b.reshape(N_EXPERTS, 1), upper)


MOE_TILE = 1024
MOE_SUB_ROWS = 320


def _expert_kernel(cnt_ref, h_ref, mr_ref, mc_ref, wg_ref, wu_ref, wd_ref, o_ref):
    i, grp = pl.program_id(0), pl.program_id(1)

    @pl.when(grp == 0)
    def _():
        o_ref[...] = jnp.zeros_like(o_ref)

    tm = h_ref.shape[0]
    rows = MOE_SUB_ROWS
    n_sub = (cnt_ref[i * N_GROUPS + grp] + rows - 1) // rows
    grp_f = grp.astype(F32)
    sel_row = jnp.where(mr_ref[_META_GROUP:_META_GROUP + 1, :] == grp_f, mr_ref[_META_RANK:_META_RANK + 1, :], -1.0)
    sel_col = jnp.where(mc_ref[:, _META_GROUP:_META_GROUP + 1] == grp_f, mc_ref[:, _META_RANK:_META_RANK + 1], -1.0)
    gate_parts = _split3(mc_ref[...])
    slot_r = lax.broadcasted_iota(jnp.int32, (rows, tm), 0).astype(F32)
    slot_c = lax.broadcasted_iota(jnp.int32, (tm, rows), 1).astype(F32)

    def sub_block(s, carry):
        base = (s * rows).astype(F32)
        pick = (sel_row - base == slot_r).astype(BF16)
        put = (sel_col - base == slot_c).astype(BF16)
        xg = jnp.dot(pick, h_ref[...], preferred_element_type=F32).astype(BF16)
        gates = sum(jnp.dot(pick, p, preferred_element_type=F32) for p in gate_parts)
        y = jnp.zeros((rows, o_ref.shape[1]), F32)
        for e in range(EXPERTS_PER_GROUP):
            hid = _silu(_dot(xg, wg_ref[e])) * _dot(xg, wu_ref[e])
            y = y + _dot(gates[:, e:e + 1] * hid, wd_ref[e])
        yh = y.astype(BF16)
        yl = (y - yh.astype(F32)).astype(BF16)
        o_ref[...] += (jnp.dot(put, yh, preferred_element_type=F32)
                       + jnp.dot(put, yl, preferred_element_type=F32))
        return carry

    lax.fori_loop(0, n_sub, sub_block, 0)


def moe_experts(hb, meta, counts, wg, wu, wd, *, tm):
    m, d = hb.shape
    tok = lambda i, g, cnt: (i, 0)
    grid_spec = pltpu.PrefetchScalarGridSpec(
        num_scalar_prefetch=1,
        grid=(m // tm, N_GROUPS),
        in_specs=[pl.BlockSpec((tm, d), tok),
                  pl.BlockSpec((SUBLANES, tm), lambda i, g, cnt: (0, i)),
                  pl.BlockSpec((tm, SUBLANES), tok),
                  pl.BlockSpec((EXPERTS_PER_GROUP, d, EXPERT_FF), lambda i, g, cnt: (g, 0, 0)),
                  pl.BlockSpec((EXPERTS_PER_GROUP, d, EXPERT_FF), lambda i, g, cnt: (g, 0, 0)),
                  pl.BlockSpec((EXPERTS_PER_GROUP, EXPERT_FF, d), lambda i, g, cnt: (g, 0, 0))],
        out_specs=pl.BlockSpec((tm, d), tok))
    return pl.pallas_call(
        _expert_kernel,
        grid_spec=grid_spec,
        out_shape=jax.ShapeDtypeStruct((m, d), F32),
        compiler_params=_cparams("parallel", "arbitrary"),
    )(counts[:, 0, :N_GROUPS].reshape(-1), hb, meta, meta.T, wg, wu, wd)


def _moe_residual_kernel(x_ref, y_ref, m_ref, o_ref, *, bsz, t_all, n_ctx):
    tm = x_ref.shape[0]
    row = pl.program_id(0) * tm + lax.broadcasted_iota(jnp.int32, (tm, 1), 0)
    mod = jnp.zeros(x_ref.shape, F32)
    ctx = jnp.zeros((tm, 1), jnp.bool_)
    for bi in range(bsz):
        lo = bi * t_all
        ctx = ctx | ((row >= lo) & (row < lo + n_ctx))
        lat = (row >= lo + n_ctx) & (row < lo + t_all)
        mod = mod + jnp.where(lat, m_ref[bi:bi + 1, :], 0.0)
    mod = mod + jnp.where(ctx, m_ref[bsz:bsz + 1, :], 0.0)
    o_ref[...] = x_ref[...] + mod * y_ref[...]


def moe_residual(x_all, y, gate_rows, *, n_ctx, tm=1024):
    b, t, d = x_all.shape
    m = b * t
    tm = _pick_tile(m, tm)
    tok = lambda i: (i, 0)
    out = pl.pallas_call(
        functools.partial(_moe_residual_kernel, bsz=b, t_all=t, n_ctx=n_ctx),
        grid=(m // tm,),
        in_specs=[pl.BlockSpec((tm, d), tok), pl.BlockSpec((tm, d), tok), _const_spec(gate_rows.shape)],
        out_specs=pl.BlockSpec((tm, d), tok),
        out_shape=jax.ShapeDtypeStruct((m, d), F32),
        compiler_params=_cparams("parallel"),
    )(x_all.reshape(m, d), y, gate_rows)
    return out.reshape(b, t, d)


def _final_norm_kernel(x_ref, w_ref, o_ref):
    x = x_ref[...]
    o_ref[...] = x * lax.rsqrt(jnp.mean(x * x, axis=-1, keepdims=True) + EPS) * w_ref[...]


def final_rms_norm(x, w, *, tm):
    m, d = x.shape
    return pl.pallas_call(
        _final_norm_kernel,
        grid=(m // tm,),
        in_specs=[pl.BlockSpec((tm, d), lambda i: (i, 0)), pl.BlockSpec((1, d), lambda i: (0, 0))],
        out_specs=pl.BlockSpec((tm, d), lambda i: (i, 0)),
        out_shape=jax.ShapeDtypeStruct((m, d), F32),
        compiler_params=_cparams("parallel"),
    )(x, w.reshape(1, d))


def _pack_w_in(w_in, mixer):
    cols = _SRC_COLS[mixer]
    return jnp.where(jnp.asarray(cols >= 0)[None, :], w_in[:, np.maximum(cols, 0)], 0.0).astype(BF16)


def mixer_scans(ps, lp, *, n_ctx):
    p_ssm, p_gla, p_rwkv, p_gdn = ps

    xs, bc, sm = ssm_prep(p_ssm, lp, n_ctx=n_ctx)
    neg_a = jnp.pad(-jnp.exp(lp["ssm_a_log"]).reshape(1, -1), ((0, 0), (0, LANES - 2 * SSM_HEADS)))
    ssm = tuple(ssd_scan(xs, bc, sm, neg_a, n_ctx=n_ctx)) + (xs, p_ssm)

    w2 = [jnp.zeros((LANES, GLA_HEADS * GLA_DK), F32).at[d * GLA_RANK:(d + 1) * GLA_RANK].set(lp["gla_w2"][d])
          for d in range(2)]
    gb = [_row(lp["gla_b"][d]) for d in range(2)]
    gla = tuple(gla_scan(p_gla, w2, gb, n_ctx=n_ctx)) + (p_gla,)

    r, k, v, kk, a, lw, g, bonus = rwkv_prep(p_rwkv, lp, n_ctx=n_ctx)
    rwkv = tuple(rwkv_scan(r, k, v, kk, a, lw, _row(lp["rwkv_k_a"]), n_ctx=n_ctx)) + (g, bonus)

    q, kd, vd, smd = gdn_prep(p_gdn, lp, n_ctx=n_ctx)
    gdn = tuple(gdn_scan(q, kd, vd, smd, n_ctx=n_ctx)) + (p_gdn,)
    return ssm, gla, rwkv, gdn


def _to_scan_order(t):
    b, n, d = t.shape
    return t.reshape(b, n // GRID_W, GRID_W, d).transpose(0, 2, 1, 3).reshape(b, n, d)


def _from_scan_order(t):
    b, n, d = t.shape
    return t.reshape(b, GRID_W, n // GRID_W, d).transpose(0, 2, 1, 3).reshape(b, n, d)


def kernel(x, c, ctx, c_ctx, ada_w, ada_b, norm_mix, norm_ffn, w_in, w_gate, w_branch, w_out, ssm_conv_w, ssm_conv_b, ssm_a_log, ssm_dt_bias, ssm_d, ssm_norm, gla_w2, gla_b, gla_norm, rwkv_mu, rwkv_w0, rwkv_w2, rwkv_a0, rwkv_a2, rwkv_g2, rwkv_k_k, rwkv_k_a, rwkv_r_k, rwkv_ln_w, rwkv_ln_b, gdn_conv_w, gdn_a_log, gdn_dt_bias, gdn_norm, router_w, router_b, moe_w_gate, moe_w_up, moe_w_down, final_norm):
    bsz, seq, d = x.shape
    n_ctx = ctx.shape[1]
    t_all = n_ctx + seq
    m_all = bsz * t_all

    cond = jnp.concatenate([jax.nn.silu(c), jax.nn.silu(c_ctx)[None]], 0)
    cond = jnp.pad(cond, ((0, SUBLANES - cond.shape[0]), (0, 0)))
    mods, mod_rows = [], []
    for l in range(DEPTH):
        mod = pmatmul(cond, ada_w[l], tm=SUBLANES, tn=1024, precise=True) + ada_b[l]
        mod_rows.append(mod)
        lat = mod[:bsz].reshape(bsz, 6, d)
        cx = jnp.broadcast_to(mod[bsz].reshape(1, 6, d), (bsz, 6, d))
        mods.append(jnp.stack([cx, lat], axis=1))

    x_all = jnp.concatenate([ctx, x], axis=1)
    scan_order = False
    for l in range(DEPTH):
        if (l % 2 == 1) != scan_order:
            reorder = _from_scan_order if scan_order else _to_scan_order
            x_all = jnp.concatenate([x_all[:, :n_ctx], reorder(x_all[:, n_ctx:])], axis=1)
            scan_order = not scan_order
        lp = dict(ssm_conv_w=ssm_conv_w[l], ssm_conv_b=ssm_conv_b[l], ssm_a_log=ssm_a_log[l],
                  ssm_dt_bias=ssm_dt_bias[l], ssm_d=ssm_d[l], ssm_norm=ssm_norm[l],
                  gla_w2=gla_w2[l], gla_b=gla_b[l], gla_norm=gla_norm[l],
                  rwkv_mu=rwkv_mu[l], rwkv_w0=rwkv_w0[l], rwkv_w2=rwkv_w2[l], rwkv_a0=rwkv_a0[l],
                  rwkv_a2=rwkv_a2[l], rwkv_g2=rwkv_g2[l], rwkv_k_k=rwkv_k_k[l], rwkv_k_a=rwkv_k_a[l],
                  rwkv_r_k=rwkv_r_k[l], rwkv_ln_w=rwkv_ln_w[l], rwkv_ln_b=rwkv_ln_b[l],
                  gdn_conv_w=gdn_conv_w[l], gdn_a_log=gdn_a_log[l], gdn_dt_bias=gdn_dt_bias[l],
                  gdn_norm=gdn_norm[l])
        mod = mods[l]
        msel = lambda i: mod[:, :, i][:, :, None, :]

        h = norm_modulate(x_all, norm_mix[l], msel(0), msel(1), n_ctx=n_ctx)
        h2d = h.reshape(m_all, d)
        ps = []
        for mixer in ("ssm", "gla", "rwkv", "gdn"):
            wp = _pack_w_in(w_in[l], mixer)
            ps.append(pmatmul(h2d, wp, tm=512, tn=wp.shape[1]).reshape(bsz, t_all, wp.shape[1]))
        wg_cat = jnp.concatenate([w_gate[l, i] for i in range(4)], axis=1).astype(BF16)
        gates = pmatmul(h2d, wg_cat, tm=1024, tn=1024, act="sigmoid", out_dtype=BF16)
        gates = gates.reshape(bsz, t_all, 4 * d)

        ssm, gla, rwkv, gdn = mixer_scans(ps, lp, n_ctx=n_ctx)
        x_all = merge_residual(ssm, gla, rwkv, gdn, gates, x_all, msel(2), lp,
                               w_branch[l].astype(BF16), w_out[l].astype(BF16), n_ctx=n_ctx)

        h = norm_modulate(x_all, norm_ffn[l], msel(3), msel(4), n_ctx=n_ctx, out_dtype=F32)
        tm_moe = _pick_tile(m_all, MOE_TILE)
        meta, counts, hb = moe_route(h.reshape(m_all, d), router_w, router_b, tm=tm_moe)
        y = moe_experts(hb, meta, counts, moe_w_gate[l].astype(BF16), moe_w_up[l].astype(BF16),
                        moe_w_down[l].astype(BF16), tm=tm_moe)
        x_all = moe_residual(x_all, y, mod_rows[l][:, 5 * d:], n_ctx=n_ctx)

    lat = x_all[:, n_ctx:]
    if scan_order:
        lat = _from_scan_order(lat)
    return final_rms_norm(lat.reshape(bsz * seq, d), final_norm, tm=1024).reshape(bsz, seq, d)
```

```python
import functools
import itertools

import numpy as np
import jax
import jax.numpy as jnp
from jax import lax
from jax.experimental import pallas as pl
from jax.experimental.pallas import tpu as pltpu

F32 = jnp.float32
BF16 = jnp.bfloat16
HI = lax.Precision.HIGHEST

D_MODEL = 1024
DEPTH = 2
GRID_W = 64
CHUNK = 64
EPS = 1e-6
BRANCH = D_MODEL // 2
SSM_HEADS, SSM_P, SSM_GROUPS, SSM_N = 8, 64, 2, 64
GLA_HEADS, GLA_DK, GLA_DV, GLA_RANK, GLA_TAU = 4, 64, 128, 16, 16.0
RWKV_HEADS, RWKV_N, RWKV_LN_EPS = 8, 64, 64e-5
GDN_HEADS, GDN_N = 4, 128
N_EXPERTS, N_GROUPS, EXPERTS_PER_GROUP = 16, 4, 4
EXPERT_FF = D_MODEL // 2
LANES = 128
SUBLANES = 8
VMEM_LIMIT = 48 * 1024 * 1024
ROW_TILE = 256

_REF_BLOCKS = (
    ("ssm", "z", 512), ("ssm", "xbc", 768), ("ssm", "dt", 16),
    ("gla", "q", 256), ("gla", "k", 256), ("gla", "v", 512), ("gla", "r", 512), ("gla", "glr", 32),
    ("rwkv", "all", 1920),
    ("gdn", "qkv", 1536), ("gdn", "z", 512), ("gdn", "ab", 16),
)
_PACKED = {
    "ssm": (("z", 512), ("dt", 128), ("pad", 128), ("xbc", 768)),
    "gla": (("q", 256), ("k", 256), ("v", 512), ("r", 512), ("glr", 128)),
    "rwkv": (("all", 1920),),
    "gdn": (("qkv", 1536), ("z", 512), ("ab", 128)),
}


def _packed_columns():
    start, s = {}, 0
    for mixer, blk, w in _REF_BLOCKS:
        start[(mixer, blk)] = (s, w)
        s += w
    out = {}
    for mixer, blocks in _PACKED.items():
        cols = []
        for blk, wp in blocks:
            s0, w = start.get((mixer, blk), (0, 0))
            cols += list(range(s0, s0 + w)) + [-1] * (wp - w)
        out[mixer] = np.asarray(cols, np.int32)
    return out


_SRC_COLS = _packed_columns()


def _cparams(*sem):
    return pltpu.CompilerParams(dimension_semantics=sem, vmem_limit_bytes=VMEM_LIMIT)


def _dot(a, b):
    return jnp.dot(a.astype(BF16), b.astype(BF16), preferred_element_type=F32)


def _dot_nt(a, b):
    return lax.dot_general(a.astype(BF16), b.astype(BF16), (((1,), (1,)), ((), ())),
                           preferred_element_type=F32)


def _dot_tn(a, b):
    return lax.dot_general(a.astype(BF16), b.astype(BF16), (((0,), (0,)), ((), ())),
                           preferred_element_type=F32)


def _dot_hi(a, b):
    return jnp.dot(a, b, precision=HI, preferred_element_type=F32)


def _dot_x3(a, b):
    ah = a.astype(BF16)
    al = (a - ah.astype(F32)).astype(BF16)
    bh = b.astype(BF16)
    bl = (b - bh.astype(F32)).astype(BF16)
    f = lambda u, v: jnp.dot(u, v, preferred_element_type=F32)
    return f(ah, bh) + (f(ah, bl) + f(al, bh))


def _dot_x2(a, w):
    ah = a.astype(BF16)
    al = (a - ah.astype(F32)).astype(BF16)
    return jnp.dot(ah, w, preferred_element_type=F32) + jnp.dot(al, w, preferred_element_type=F32)


def _softplus(x):
    return jnp.maximum(x, 0.0) + jnp.log(1.0 + jnp.exp(-jnp.abs(x)))


def _sigmoid(x):
    return 1.0 / (1.0 + jnp.exp(-x))


def _silu(x):
    return x * _sigmoid(x)


def _pick_tile(m, pref):
    t = pref
    while m % t:
        t //= 2
    return t


def _block_diag_ones(n, width=BRANCH):
    idx = np.arange(width) // n
    return jnp.asarray(idx[:, None] == idx[None, :], BF16)


def _mm_kernel(a_ref, w_ref, o_ref, *, act, precise):
    if precise:
        r = _dot_hi(a_ref[...].astype(F32), w_ref[...].astype(F32))
    else:
        r = _dot(a_ref[...], w_ref[...])
    if act == "sigmoid":
        r = _sigmoid(r)
    o_ref[...] = r.astype(o_ref.dtype)


def pmatmul(a, w, *, tm, tn, act=None, precise=False, out_dtype=F32):
    m, k = a.shape
    n = w.shape[1]
    tm = _pick_tile(m, tm)
    assert tm % SUBLANES == 0 and n % tn == 0, (m, tm, n, tn)
    return pl.pallas_call(
        functools.partial(_mm_kernel, act=act, precise=precise),
        grid=(n // tn, m // tm),
        in_specs=[pl.BlockSpec((tm, k), lambda j, i: (i, 0)),
                  pl.BlockSpec((k, tn), lambda j, i: (0, j))],
        out_specs=pl.BlockSpec((tm, tn), lambda j, i: (i, j)),
        out_shape=jax.ShapeDtypeStruct((m, n), out_dtype),
        compiler_params=_cparams("parallel", "parallel"),
    )(a, w)


def _norm_mod_kernel(x_ref, w_ref, shift_ref, scale_ref, o_ref):
    x = x_ref[0]
    y = x * lax.rsqrt(jnp.mean(x * x, axis=-1, keepdims=True) + EPS) * w_ref[...]
    o_ref[0] = (y * (1.0 + scale_ref[0, 0]) + shift_ref[0, 0]).astype(o_ref.dtype)


def _mod_sel(nct):
    return lambda bi, i, *_: (bi, jnp.where(i < nct, 0, 1), 0, 0)


def norm_modulate(x_all, w, shift, scale, *, n_ctx, out_dtype=BF16):
    b, t, d = x_all.shape
    tm = _pick_tile(n_ctx, ROW_TILE)
    assert t % tm == 0
    tok = lambda bi, i: (bi, i, 0)
    return pl.pallas_call(
        _norm_mod_kernel,
        grid=(b, t // tm),
        in_specs=[pl.BlockSpec((1, tm, d), tok),
                  pl.BlockSpec((1, d), lambda bi, i: (0, 0)),
                  pl.BlockSpec((1, 1, 1, d), _mod_sel(n_ctx // tm)),
                  pl.BlockSpec((1, 1, 1, d), _mod_sel(n_ctx // tm))],
        out_specs=pl.BlockSpec((1, tm, d), tok),
        out_shape=jax.ShapeDtypeStruct((b, t, d), out_dtype),
        compiler_params=_cparams("parallel", "parallel"),
    )(x_all, w.reshape(1, d), shift, scale)


def _row(v):
    return v.reshape(1, -1).astype(F32)


def _const_spec(shape):
    return pl.BlockSpec(shape, lambda *_: (0,) * len(shape))


def _tile_specs(tt, width, col):
    r8 = tt // SUBLANES
    main = pl.BlockSpec((1, tt, width), lambda bi, i: (bi, i, col))
    prev = pl.BlockSpec((1, SUBLANES, width), lambda bi, i: (bi, jnp.maximum(i * r8 - 1, 0), col))
    return main, prev, r8


def _halo_specs(tt, width, col, t):
    main, prev, r8 = _tile_specs(tt, width, col)
    last8 = t // SUBLANES - 1
    nxt = pl.BlockSpec((1, SUBLANES, width), lambda bi, i: (bi, jnp.minimum((i + 1) * r8, last8), col))
    return [main, prev, nxt]


def _neighbours(x, prev8, next8, *, nct, nt):
    i = pl.program_id(1)
    tt = x.shape[0]
    row = lax.broadcasted_iota(jnp.int32, x.shape, 0)
    first = (i == 0) | (i == nct)
    last = (i == nct - 1) | (i == nt - 1)
    pr = jnp.where(first, 0.0, prev8[SUBLANES - 1:SUBLANES, :])
    nx = jnp.where(last, 0.0, next8[0:1, :])
    xp = jnp.where(row == 0, pr, pltpu.roll(x, 1, 0))
    xn = jnp.where(row == tt - 1, nx, pltpu.roll(x, tt - 1, 0))
    return xp, xn


def _prep_call(kernel, ins, in_specs, out_widths, *, b, t, tt, out_dtype=F32):
    tok = lambda bi, i: (bi, i, 0)
    return pl.pallas_call(
        kernel,
        grid=(b, t // tt),
        in_specs=in_specs,
        out_specs=[pl.BlockSpec((1, tt, w), tok) for w in out_widths],
        out_shape=[jax.ShapeDtypeStruct((b, t, w), out_dtype) for w in out_widths],
        compiler_params=_cparams("parallel", "parallel"),
    )(*ins)


def _chunk_masks(reverse):
    row = lax.broadcasted_iota(jnp.int32, (CHUNK, CHUNK), 0)
    col = lax.broadcasted_iota(jnp.int32, (CHUNK, CHUNK), 1)
    if reverse:
        return col >= row, col > row
    return col <= row, col < row


def _chunk_order(i, n_ctx_chunks, n_chunks, reverse):
    if not reverse:
        return i
    return jnp.where(i < n_ctx_chunks, n_ctx_chunks - 1 - i, n_chunks - 1 - (i - n_ctx_chunks))


def _split3(a):
    hi = a.astype(BF16)
    r = a - hi.astype(F32)
    mid = r.astype(BF16)
    return hi, mid, (r - mid.astype(F32)).astype(BF16)


def _transpose_small(x):
    row = lax.broadcasted_iota(jnp.int32, (LANES, LANES), 0)
    col = lax.broadcasted_iota(jnp.int32, (LANES, LANES), 1)
    eye = (row == col).astype(BF16)
    nt = lambda p: lax.dot_general(eye, p, (((1,), (1,)), ((), ())), preferred_element_type=F32)
    hi, mid, lo = _split3(x)
    return nt(hi) + (nt(mid) + nt(lo))


def _chunk_cumsum(incl, x):
    m = incl.astype(BF16)
    hi, mid, lo = _split3(x)
    f = lambda p: jnp.dot(m, p, preferred_element_type=F32)
    return f(hi) + (f(mid) + f(lo))


def _select_columns(x, sel):
    c = x.shape[0]
    y = jnp.dot(jnp.concatenate(_split3(x), axis=0), sel, preferred_element_type=F32)
    return y[:c] + (y[c:2 * c] + y[2 * c:])


def _unit_tri_solve(mats, rhs, precise_levels=0):
    n = range(len(mats))
    x = [rhs[h] - _dot_x3(mats[h], rhs[h]) for h in n]
    yield
    p = mats
    for level in range(int(np.log2(CHUNK)) - 1):
        dot = _dot_x3 if level < precise_levels else _dot
        p = [dot(p[h], p[h]) for h in n]
        yield
        x = [x[h] + dot(p[h], x[h]) for h in n]
        yield
    return x


def _bidir_scan(body, tok_ins, const_ins, state_shape, *, b, t, n_ctx, lockstep=True, batch_block=1):
    nc, ncc = t // CHUNK, n_ctx // CHUNK
    nb = batch_block
    assert b % nb == 0

    def chunk_spec(width, col, reverse):
        return pl.BlockSpec((nb, CHUNK, width), lambda bi, i: (bi, _chunk_order(i, ncc, nc, reverse), col))

    def direction(reverse):
        d = int(reverse)
        specs = [chunk_spec(w, cols[d], reverse) for _, w, *cols in tok_ins]
        specs += [_const_spec(pair[d].shape) for pair in const_ins]
        return specs, [a for a, *_ in tok_ins] + [pair[d] for pair in const_ins]

    (spec_f, arg_f), (spec_b, arg_b) = direction(False), direction(True)
    n_tok, n_in = len(tok_ins), len(arg_f)

    def kern(*refs):
        o_f, o_b, s_f, s_b = refs[2 * n_in:]

        @pl.when(pl.program_id(1) == 0)
        def _():
            s_f[...] = jnp.zeros_like(s_f)
            s_b[...] = jnp.zeros_like(s_b)

        def one(j, ins, o_ref, s_ref, reverse):
            ins = [r.at[pl.ds(j, 1)] if k < n_tok else r for k, r in enumerate(ins)]
            return body(*ins, o_ref.at[pl.ds(j, 1)], s_ref.at[j], reverse=reverse)

        gens = []
        for j in range(nb):
            gens += [one(j, refs[:n_in], o_f, s_f, False), one(j, refs[n_in:2 * n_in], o_b, s_b, True)]
        if not lockstep:
            gens = [itertools.chain(*gens)]
        while gens:
            gens = [g for g in gens if next(g, _DONE) is not _DONE]

    return pl.pallas_call(
        kern,
        grid=(b // nb, nc),
        in_specs=spec_f + spec_b,
        out_specs=[chunk_spec(BRANCH, 0, False), chunk_spec(BRANCH, 0, True)],
        out_shape=[jax.ShapeDtypeStruct((b, t, BRANCH), F32)] * 2,
        scratch_shapes=[pltpu.VMEM((nb,) + tuple(state_shape), F32)] * 2,
        compiler_params=_cparams("parallel", "arbitrary"),
    )(*arg_f, *arg_b)


_DONE = object()


def _batch_block(b, pref):
    return pref if b % pref == 0 else 1


def _ssm_prep_kernel(x_ref, xp_ref, xn_ref, dt_ref, cw_ref, cb_ref, dtb_ref, xs_ref, bc_ref, sm_ref,
                     *, nct, nt):
    x = x_ref[0]
    xp, xn = _neighbours(x, xp_ref[0], xn_ref[0], nct=nct, nt=nt)
    y = _silu(xp * cw_ref[0:1, :] + x * cw_ref[1:2, :] + xn * cw_ref[2:3, :] + cb_ref[...])
    xs_ref[0] = y[:, :BRANCH]
    bc_ref[0] = y[:, BRANCH:]
    sm_ref[0] = _softplus(dt_ref[0] + dtb_ref[...])


def ssm_prep(p, lp, *, n_ctx):
    b, t, _ = p.shape
    tt = _pick_tile(n_ctx, ROW_TILE)
    dtb = jnp.pad(lp["ssm_dt_bias"].reshape(1, -1), ((0, 0), (0, LANES - 2 * SSM_HEADS)))
    specs = _halo_specs(tt, 768, 1, t) + [pl.BlockSpec((1, tt, LANES), lambda bi, i: (bi, i, 4)),
                                          _const_spec((3, 768)), _const_spec((1, 768)), _const_spec((1, LANES))]
    kern = functools.partial(_ssm_prep_kernel, nct=n_ctx // tt, nt=t // tt)
    return _prep_call(kern, (p, p, p, p, lp["ssm_conv_w"], _row(lp["ssm_conv_b"]), dtb), specs,
                      (BRANCH, 2 * SSM_GROUPS * SSM_N, LANES), b=b, t=t, tt=tt)


def _ssd_body(x_ref, bc_ref, sm_ref, na_ref, o_ref, s_ref, *, reverse):
    incl, _ = _chunk_masks(reverse)
    last = 0 if reverse else CHUNK - 1
    off = SSM_HEADS if reverse else 0
    dt_all = sm_ref[0]
    g_all = _chunk_cumsum(incl, dt_all * na_ref[...])
    yield
    expand = _expand_matrix(off, SSM_HEADS, SSM_P)
    gx = _select_columns(g_all, expand)
    dx = _select_columns(dt_all, expand)
    gt_all = _transpose_small(g_all)
    dtt_all = _transpose_small(dt_all)
    yield
    heads = range(SSM_HEADS)
    rep = SSM_HEADS // SSM_GROUPS
    gw = SSM_GROUPS * SSM_N
    pw = rep * SSM_P
    hs = [slice(h * SSM_P, (h + 1) * SSM_P) for h in heads]
    glx = gx[last:last + 1, :]
    egx = jnp.exp(gx)
    wx = dx * jnp.exp(glx - gx)
    eglx = jnp.exp(glx)
    x = x_ref[0]
    bm = [bc_ref[0, :, grp * SSM_N:(grp + 1) * SSM_N] for grp in range(SSM_GROUPS)]
    cm = [bc_ref[0, :, gw + grp * SSM_N:gw + (grp + 1) * SSM_N] for grp in range(SSM_GROUPS)]
    cb = [_dot_nt(cm[grp], bm[grp]) for grp in range(SSM_GROUPS)]
    s = [s_ref[grp] for grp in range(SSM_GROUPS)]
    yield
    scores = [cb[h // rep] * jnp.exp(jnp.where(incl, gx[:, hs[h]] - gt_all[off + h:off + h + 1, :], -jnp.inf))
              * dtt_all[off + h:off + h + 1, :] for h in heads]
    yield
    intra = [_dot(scores[h], x[:, hs[h]]) for h in heads]
    yield
    inter = [_dot(cm[grp], s[grp]) for grp in range(SSM_GROUPS)]
    yield
    upd = [_dot_tn(bm[h // rep] * wx[:, hs[h]], x[:, hs[h]]) for h in heads]
    yield
    for h in heads:
        grp, ls = h // rep, slice((h % rep) * SSM_P, (h % rep + 1) * SSM_P)
        o_ref[0, :, hs[h]] = intra[h] + egx[:, hs[h]] * inter[grp][:, ls]
        s_ref[grp, :, ls] = s[grp][:, ls] * eglx[:, hs[h]] + upd[h]


def _expand_matrix(off, n_heads, width):
    row = lax.broadcasted_iota(jnp.int32, (LANES, n_heads * width), 0)
    col = lax.broadcasted_iota(jnp.int32, (LANES, n_heads * width), 1)
    lo = row * width - off * width
    return ((col >= lo) & (col < lo + width)).astype(BF16)


def ssd_scan(xs, bc, sm, neg_a, *, n_ctx):
    b, t, _ = xs.shape
    toks = [(xs, BRANCH, 0, 0), (bc, 2 * SSM_GROUPS * SSM_N, 0, 0), (sm, LANES, 0, 0)]
    state = (SSM_GROUPS, SSM_N, (SSM_HEADS // SSM_GROUPS) * SSM_P)
    return _bidir_scan(_ssd_body, toks, [(neg_a, neg_a)], state, b=b, t=t, n_ctx=n_ctx,
                       batch_block=_batch_block(b, 2))


def _gla_body(q_ref, k_ref, v_ref, glr_ref, w2_ref, gb_ref, o_ref, s_ref, *, reverse):
    incl, _ = _chunk_masks(reverse)
    last = 0 if reverse else CHUNK - 1
    logit = _dot_x3(glr_ref[0], w2_ref[...]) + gb_ref[...]
    yield
    la = -_softplus(-logit) * (1.0 / GLA_TAU)
    g_all = _chunk_cumsum(incl, la)
    yield
    heads = range(GLA_HEADS)
    ks = [slice(h * GLA_DK, (h + 1) * GLA_DK) for h in heads]
    vs = [slice(h * GLA_DV, (h + 1) * GLA_DV) for h in heads]
    g = [g_all[:, ks[h]] for h in heads]
    gl = [g[h][last:last + 1, :] for h in heads]
    k = [k_ref[0, :, ks[h]] for h in heads]
    v = [v_ref[0, :, vs[h]] for h in heads]
    qg = [q_ref[0, :, ks[h]] * (GLA_DK ** -0.5) * jnp.exp(g[h]) for h in heads]
    st = [s_ref[h] for h in heads]
    yield
    scores = [jnp.where(incl, _dot_nt(qg[h], k[h] * jnp.exp(-g[h])), 0.0) for h in heads]
    yield
    intra = [_dot(scores[h], v[h]) for h in heads]
    yield
    inter = [_dot_nt(qg[h], st[h]) for h in heads]
    yield
    upd = [_dot_tn(v[h], k[h] * jnp.exp(gl[h] - g[h])) for h in heads]
    yield
    for h in heads:
        o_ref[0, :, vs[h]] = intra[h] + inter[h]
        s_ref[h] = st[h] * jnp.exp(gl[h]) + upd[h]


def gla_scan(p, w2_pair, gb_pair, *, n_ctx):
    b, t, _ = p.shape
    kwid = GLA_HEADS * GLA_DK
    toks = [(p, kwid, 0, 0), (p, kwid, 1, 1), (p, BRANCH, 1, 1), (p, LANES, 12, 12)]
    return _bidir_scan(_gla_body, toks, [w2_pair, gb_pair], (GLA_HEADS, GLA_DV, GLA_DK), b=b, t=t, n_ctx=n_ctx,
                       batch_block=_batch_block(b, 4))


def _rwkv_prep_kernel(x_ref, xp_ref, xn_ref, mu_ref, w2_ref, w0_ref, a2_ref, a0_ref, g2_ref, kk_ref_w,
                      ka_ref, rk_ref, bd_ref, r_ref, k_ref, v_ref, kk_ref, a_ref, lw_ref, g_ref, bo_ref,
                      *, nct, nt):
    x = x_ref[0]
    xp, xn = _neighbours(x, xp_ref[0], xn_ref[0], nct=nct, nt=nt)
    x = x + mu_ref[...] * (0.5 * (xp + xn) - x)
    r, k, v = x[:, :BRANCH], x[:, BRANCH:2 * BRANCH], x[:, 2 * BRANCH:3 * BRANCH]
    wlr = x[:, 3 * BRANCH:3 * BRANCH + LANES]
    alr = x[:, 3 * BRANCH + LANES:3 * BRANCH + 2 * LANES]
    glr = x[:, 3 * BRANCH + 2 * LANES:]
    w_raw = _dot_x3(jnp.tanh(wlr), w2_ref[...]) + w0_ref[...]
    lw_ref[0] = -jnp.exp(-_softplus(-w_raw) - 0.5)
    a = _sigmoid(_dot_x3(alr, a2_ref[...]) + a0_ref[...])
    a_ref[0] = a
    g_ref[0] = _dot_x3(_sigmoid(glr), g2_ref[...])
    kk = k * kk_ref_w[...]
    kk_ref[0] = kk * lax.rsqrt(_dot_x2(kk * kk, bd_ref[...]) + EPS)
    ksum = k * (2.0 + (a[:, :BRANCH] + a[:, BRANCH:] - 2.0) * ka_ref[...])
    bo_ref[0] = _dot_x2(r * ksum * rk_ref[...], bd_ref[...]) * v
    r_ref[0] = r
    k_ref[0] = k
    v_ref[0] = v


def rwkv_prep(p, lp, *, n_ctx):
    b, t, w = p.shape
    tt = _pick_tile(n_ctx, ROW_TILE)

    def pair(wp):
        r, c = wp.shape[1:]
        return jnp.zeros((LANES, 2 * c), F32).at[:r, :c].set(wp[0]).at[r:2 * r, c:].set(wp[1])

    consts = (_row(lp["rwkv_mu"]), pair(lp["rwkv_w2"]), _row(lp["rwkv_w0"]), pair(lp["rwkv_a2"]),
              _row(lp["rwkv_a0"]), lp["rwkv_g2"], _row(lp["rwkv_k_k"]), _row(lp["rwkv_k_a"]),
              _row(lp["rwkv_r_k"]), _block_diag_ones(RWKV_N))
    specs = _halo_specs(tt, w, 0, t) + [_const_spec(c.shape) for c in consts]
    kern = functools.partial(_rwkv_prep_kernel, nct=n_ctx // tt, nt=t // tt)
    return _prep_call(kern, (p, p, p) + consts, specs,
                      (BRANCH, BRANCH, BRANCH, BRANCH, 2 * BRANCH, 2 * BRANCH, BRANCH, BRANCH),
                      b=b, t=t, tt=tt)


def _rwkv_body(r_ref, k_ref, v_ref, kk_ref, a_ref, lw_ref, ka_ref, o_ref, s_ref, *, reverse):
    incl, strict = _chunk_masks(reverse)
    last = 0 if reverse else CHUNK - 1
    lw_all = lw_ref[0]
    g_all = _chunk_cumsum(incl, lw_all)
    a_all = a_ref[0]
    k_all = k_ref[0] * (1.0 + (a_all - 1.0) * ka_ref[...])
    yield
    heads = range(RWKV_HEADS)
    hs = [slice(h * RWKV_N, (h + 1) * RWKV_N) for h in heads]
    g = [g_all[:, hs[h]] for h in heads]
    gl = [g[h][last:last + 1, :] for h in heads]
    eneg = [jnp.exp(-g[h]) for h in heads]
    edec = [jnp.exp(gl[h] - g[h]) for h in heads]
    kk = [kk_ref[0, :, hs[h]] for h in heads]
    bvec = [kk[h] * a_all[:, hs[h]] for h in heads]
    k = [k_all[:, hs[h]] for h in heads]
    v = [v_ref[0, :, hs[h]] for h in heads]
    kkg = [kk[h] * jnp.exp(g[h] - lw_all[:, hs[h]]) for h in heads]
    rg = [r_ref[0, :, hs[h]] * jnp.exp(g[h]) for h in heads]
    bh = [bvec[h] * eneg[h] for h in heads]
    kh = [k[h] * eneg[h] for h in heads]
    s = [s_ref[h] for h in heads]
    yield
    both = [jnp.concatenate([kkg[h], rg[h]], axis=0) for h in heads]
    mask2 = jnp.concatenate([strict, incl], axis=0)
    mb = [jnp.where(mask2, _dot_nt(both[h], bh[h]), 0.0) for h in heads]
    yield
    mk = [jnp.where(mask2, _dot_nt(both[h], kh[h]), 0.0) for h in heads]
    yield
    part = [_dot(mk[h], v[h]) + _dot_nt(both[h], s[h]) for h in heads]
    yield
    x = yield from _unit_tri_solve([mb[h][:CHUNK] for h in heads], [part[h][:CHUNK] for h in heads])
    u = [-xh for xh in x]
    for h in heads:
        o_ref[0, :, hs[h]] = part[h][CHUNK:] + _dot(mb[h][CHUNK:], u[h])
    yield
    for h in heads:
        upd = _dot_tn(jnp.concatenate([u[h], v[h]], axis=0),
                      jnp.concatenate([bvec[h] * edec[h], k[h] * edec[h]], axis=0))
        s_ref[h] = s[h] * jnp.exp(gl[h]) + upd


def rwkv_scan(r, k, v, kk, a, lw, k_a, *, n_ctx):
    b, t, _ = r.shape
    toks = [(r, BRANCH, 0, 0), (k, BRANCH, 0, 0), (v, BRANCH, 0, 0), (kk, BRANCH, 0, 0),
            (a, BRANCH, 0, 1), (lw, BRANCH, 0, 1)]
    return _bidir_scan(_rwkv_body, toks, [(k_a, k_a)], (RWKV_HEADS, RWKV_N, RWKV_N), b=b, t=t, n_ctx=n_ctx,
                       batch_block=_batch_block(b, 2))


def _gdn_prep_kernel(x_ref, xp_ref, xn_ref, ab_ref, cw_ref, na_ref, dtb_ref, bd_ref,
                     q_ref, k_ref, v_ref, sm_ref, *, nct, nt):
    x = x_ref[0]
    xp, xn = _neighbours(x, xp_ref[0], xn_ref[0], nct=nct, nt=nt)
    y = _silu(xp * cw_ref[0:1, :] + x * cw_ref[1:2, :] + xn * cw_ref[2:3, :])
    q, k = y[:, :BRANCH], y[:, BRANCH:2 * BRANCH]
    q_ref[0] = q * lax.rsqrt(_dot_x2(q * q, bd_ref[...]) + EPS) * (GDN_N ** -0.5)
    k_ref[0] = k * lax.rsqrt(_dot_x2(k * k, bd_ref[...]) + EPS)
    v_ref[0] = y[:, 2 * BRANCH:]
    ab = ab_ref[0]
    lane = lax.broadcasted_iota(jnp.int32, ab.shape, 1)
    sm_ref[0] = jnp.where(lane < 2 * GDN_HEADS, na_ref[...] * _softplus(ab + dtb_ref[...]), _sigmoid(ab))


def gdn_prep(p, lp, *, n_ctx):
    b, t, _ = p.shape
    tt = _pick_tile(n_ctx, ROW_TILE)
    padrow = lambda v: jnp.pad(v.reshape(1, -1), ((0, 0), (0, LANES - 2 * GDN_HEADS)))
    consts = (lp["gdn_conv_w"], padrow(-jnp.exp(lp["gdn_a_log"])), padrow(lp["gdn_dt_bias"]),
              _block_diag_ones(GDN_N))
    specs = (_halo_specs(tt, 3 * BRANCH, 0, t) + [pl.BlockSpec((1, tt, LANES), lambda bi, i: (bi, i, 16))]
             + [_const_spec(c.shape) for c in consts])
    kern = functools.partial(_gdn_prep_kernel, nct=n_ctx // tt, nt=t // tt)
    return _prep_call(kern, (p, p, p, p) + consts, specs, (BRANCH, BRANCH, BRANCH, LANES), b=b, t=t, tt=tt)


def _gdn_body(q_ref, k_ref, v_ref, sm_ref, o_ref, s_ref, *, reverse):
    incl, strict = _chunk_masks(reverse)
    last = 0 if reverse else CHUNK - 1
    off = GDN_HEADS if reverse else 0
    sm = sm_ref[0]
    g_all = _chunk_cumsum(incl, sm)
    yield
    gt_all = _transpose_small(g_all)
    yield
    heads = range(GDN_HEADS)
    hs = [slice(h * GDN_N, (h + 1) * GDN_N) for h in heads]
    g = [g_all[:, off + h:off + h + 1] for h in heads]
    gl = [g[h][last:last + 1, :] for h in heads]
    beta = [sm[:, 2 * GDN_HEADS + off + h:2 * GDN_HEADS + off + h + 1] for h in heads]
    q = [q_ref[0, :, hs[h]] for h in heads]
    k = [k_ref[0, :, hs[h]] for h in heads]
    v = [v_ref[0, :, hs[h]] for h in heads]
    s = [s_ref[h] for h in heads]
    decay = [jnp.exp(jnp.where(incl, g[h] - gt_all[off + h:off + h + 1, :], -jnp.inf)) for h in heads]
    yield
    kq = [_dot_nt(jnp.concatenate([k[h], q[h]], axis=0), k[h]) for h in heads]
    yield
    lower = [jnp.where(strict, kq[h][:CHUNK] * decay[h] * beta[h], 0.0) for h in heads]
    attn = [kq[h][CHUNK:] * decay[h] for h in heads]
    o_part = [_dot(q[h] * jnp.exp(g[h]), s[h]) for h in heads]
    yield
    rhs = [jnp.concatenate([v[h] * beta[h], k[h] * (beta[h] * jnp.exp(g[h]))], axis=1) for h in heads]
    sol = yield from _unit_tri_solve(lower, rhs, precise_levels=2)
    v_new = [sol[h][:, :GDN_N] - _dot(sol[h][:, GDN_N:], s[h]) for h in heads]
    yield
    for h in heads:
        o_ref[0, :, hs[h]] = o_part[h] + _dot(attn[h], v_new[h])
    yield
    for h in heads:
        s_ref[h] = s[h] * jnp.exp(gl[h]) + _dot_tn(k[h] * jnp.exp(gl[h] - g[h]), v_new[h])


def gdn_scan(q, k, v, sm, *, n_ctx):
    b, t, _ = q.shape
    toks = [(q, BRANCH, 0, 0), (k, BRANCH, 0, 0), (v, BRANCH, 0, 0), (sm, LANES, 0, 0)]
    return _bidir_scan(_gdn_body, toks, [], (GDN_HEADS, GDN_N, GDN_N), b=b, t=t, n_ctx=n_ctx,
                       batch_block=_batch_block(b, 4))


def _merge_kernel(sf_ref, sb_ref, sx_ref, sz_ref, gf_ref, gb_ref, gr_ref, rf_ref, rb_ref, rg_ref, rbo_ref,
                  df_ref, db_ref, dz_ref, gate_ref, x_ref, m_ref,
                  sd_ref, sn_ref, gn_ref, lnw_ref, lnb_ref, dn_ref, bd64_ref, bd128_ref, bd256_ref,
                  wb_ref, wo_ref, o_ref):
    def group_rms(y, bd_ref, n, w_ref):
        return y * lax.rsqrt(_dot_x2(y * y, bd_ref[...]) * (1.0 / n) + EPS) * w_ref[...]

    y = (sf_ref[0] + sb_ref[0] + sd_ref[...] * sx_ref[0]) * _silu(sz_ref[0])
    ys = group_rms(y, bd256_ref, BRANCH // SSM_GROUPS, sn_ref)
    yg = group_rms(gf_ref[0] + gb_ref[0], bd128_ref, GLA_DV, gn_ref) * _silu(gr_ref[0])
    y = rf_ref[0] + rb_ref[0]
    yc = y - _dot_x2(y, bd64_ref[...]) * (1.0 / RWKV_N)
    var = _dot_x2(yc * yc, bd64_ref[...]) * (1.0 / RWKV_N)
    yr = (yc * lax.rsqrt(var + RWKV_LN_EPS) * lnw_ref[...] + lnb_ref[...] + rbo_ref[0]) * rg_ref[0]
    yd = group_rms(df_ref[0] + db_ref[0], bd128_ref, GDN_N, dn_ref) * _silu(dz_ref[0])
    acc = None
    for i, yi in enumerate((ys, yg, yr, yd)):
        term = gate_ref[0, :, i * D_MODEL:(i + 1) * D_MODEL].astype(F32) * _dot(yi, wb_ref[i])
        acc = term if acc is None else acc + term
    o_ref[0] = x_ref[0] + m_ref[0, 0] * _dot(acc, wo_ref[...])


def merge_residual(ssm, gla, rwkv, gdn, gates, x_all, gate_mod, lp, w_branch, w_out, *, n_ctx):
    b, t, d = x_all.shape
    tm = _pick_tile(n_ctx, ROW_TILE)
    tok = lambda bi, i: (bi, i, 0)
    blk = lambda c: pl.BlockSpec((1, tm, BRANCH), lambda bi, i: (bi, i, c))
    half = blk(0)
    consts = (_row(jnp.repeat(lp["ssm_d"], SSM_P)), _row(lp["ssm_norm"]),
              _row(jnp.tile(lp["gla_norm"], GLA_HEADS)), _row(lp["rwkv_ln_w"]), _row(lp["rwkv_ln_b"]),
              _row(jnp.tile(lp["gdn_norm"], GDN_HEADS)),
              _block_diag_ones(RWKV_N), _block_diag_ones(LANES), _block_diag_ones(BRANCH // SSM_GROUPS),
              w_branch, w_out)
    ins = (ssm[0], ssm[1], ssm[2], ssm[3], gla[0], gla[1], gla[2], rwkv[0], rwkv[1], rwkv[2], rwkv[3],
           gdn[0], gdn[1], gdn[2], gates, x_all, gate_mod) + consts
    specs = ([half, half, half, blk(0), half, half, blk(2), half, half, half, half, half, half, blk(3),
              pl.BlockSpec((1, tm, 4 * d), tok), pl.BlockSpec((1, tm, d), tok),
              pl.BlockSpec((1, 1, 1, d), _mod_sel(n_ctx // tm))]
             + [_const_spec(c.shape) for c in consts])
    return pl.pallas_call(
        _merge_kernel,
        grid=(b, t // tm),
        in_specs=specs,
        out_specs=pl.BlockSpec((1, tm, d), tok),
        out_shape=jax.ShapeDtypeStruct((b, t, d), F32),
        compiler_params=_cparams("parallel", "parallel"),
    )(*ins)


def _route_kernel(h_ref, rw_ref, rb_ref, u_ref, o_ref, cnt_ref, hb_ref):
    logits = lax.dot_general(rw_ref[...], h_ref[...].astype(F32), (((1,), (1,)), ((), ())),
                             precision=HI, preferred_element_type=F32)
    scores = _sigmoid(logits)
    sel = scores + rb_ref[...]
    rows = [sel[e:e + 1, :] for e in range(N_EXPERTS)]
    sc = [scores[e:e + 1, :] for e in range(N_EXPERTS)]

    def top2(vals):
        v1, i1 = vals[0], jnp.zeros(vals[0].shape, jnp.int32)
        for j in range(1, len(vals)):
            better = vals[j] > v1
            v1 = jnp.where(better, vals[j], v1)
            i1 = jnp.where(better, j, i1)
        v2 = jnp.where(i1 == 0, vals[1], vals[0])
        i2 = jnp.where(i1 == 0, 1, 0)
        for j in range(1, len(vals)):
            better = (vals[j] > v2) & (i1 != j)
            v2 = jnp.where(better, vals[j], v2)
            i2 = jnp.where(better, j, i2)
        return v1, i1, v2, i2

    gsum = []
    for grp in range(N_GROUPS):
        v1, _, v2, _ = top2(rows[grp * EXPERTS_PER_GROUP:(grp + 1) * EXPERTS_PER_GROUP])
        gsum.append(v1 + v2)
    best, gidx = gsum[0], jnp.zeros(gsum[0].shape, jnp.int32)
    for grp in range(1, N_GROUPS):
        better = gsum[grp] > best
        best = jnp.where(better, gsum[grp], best)
        gidx = jnp.where(better, grp, gidx)
    chosen, chosen_sc = [], []
    for j in range(EXPERTS_PER_GROUP):
        cj, sj = rows[j], sc[j]
        for grp in range(1, N_GROUPS):
            cj = jnp.where(gidx == grp, rows[grp * EXPERTS_PER_GROUP + j], cj)
            sj = jnp.where(gidx == grp, sc[grp * EXPERTS_PER_GROUP + j], sj)
        chosen.append(cj)
        chosen_sc.append(sj)
    _, i1, _, i2 = top2(chosen)
    w1, w2 = jnp.zeros_like(best), jnp.zeros_like(best)
    for j in range(EXPERTS_PER_GROUP):
        w1 = jnp.where(i1 == j, chosen_sc[j], w1)
        w2 = jnp.where(i2 == j, chosen_sc[j], w2)
    tot = w1 + w2
    w1, w2 = w1 / tot, w2 / tot
    tm = scores.shape[1]
    sub = lax.broadcasted_iota(jnp.int32, (SUBLANES, tm), 0)
    ind8 = jnp.zeros((SUBLANES, tm), F32)
    meta = jnp.zeros((SUBLANES, tm), F32)
    for j in range(EXPERTS_PER_GROUP):
        gate_j = jnp.where(i1 == j, w1, 0.0) + jnp.where(i2 == j, w2, 0.0)
        meta = jnp.where(sub == j, gate_j, meta)
    for grp in range(N_GROUPS):
        ind8 = jnp.where((sub == grp) & (gidx == grp), 1.0, ind8)
    before = jnp.dot(ind8.astype(BF16), u_ref[...], preferred_element_type=F32)
    rank = jnp.sum(ind8 * before, axis=0, keepdims=True)
    meta = jnp.where(sub == _META_GROUP, gidx.astype(F32), meta)
    meta = jnp.where(sub == _META_RANK, rank, meta)
    o_ref[...] = meta
    counts = jnp.sum(ind8, axis=1, keepdims=True)
    lane = lax.broadcasted_iota(jnp.int32, (SUBLANES, LANES), 1)
    row = lax.broadcasted_iota(jnp.int32, (SUBLANES, LANES), 0)
    cnt_ref[0] = jnp.broadcast_to(jnp.sum(jnp.where(lane == row, counts, 0.0), axis=0, keepdims=True),
                                  (SUBLANES, LANES)).astype(jnp.int32)
    hb_ref[...] = h_ref[...].astype(BF16)


_META_GROUP, _META_RANK = EXPERTS_PER_GROUP, EXPERTS_PER_GROUP + 1


def moe_route(h, router_w, router_b, *, tm):
    m, d = h.shape
    upper = jnp.asarray(np.triu(np.ones((tm, tm), np.float32), 1), BF16)
    return pl.pallas_call(
        _route_kernel,
        grid=(m // tm,),
        in_specs=[pl.BlockSpec((tm, d), lambda i: (i, 0)),
                  pl.BlockSpec((N_EXPERTS, d), lambda i: (0, 0)),
                  pl.BlockSpec((N_EXPERTS, 1), lambda i: (0, 0)),
                  _const_spec((tm, tm))],
        out_specs=[pl.BlockSpec((SUBLANES, tm), lambda i: (0, i)),
                   pl.BlockSpec((1, SUBLANES, LANES), lambda i: (i, 0, 0)),
                   pl.BlockSpec((tm, d), lambda i: (i, 0))],
        out_shape=[jax.ShapeDtypeStruct((SUBLANES, m), F32),
                   jax.ShapeDtypeStruct((m // tm, SUBLANES, LANES), jnp.int32),
                   jax.ShapeDtypeStruct((m, d), BF16)],
        compiler_params=_cparams("parallel"),
    )(h, router_w.T, router_b.reshape(N_EXPERTS, 1), upper)


MOE_TILE = 1024
MOE_SUB_ROWS = 256
MOE_TAIL_ROWS = 128


def _expert_kernel(cnt_ref, h_ref, mr_ref, mc_ref, wg_ref, wu_ref, wd_ref, o_ref):
    i, grp = pl.program_id(0), pl.program_id(1)

    @pl.when(grp == 0)
    def _():
        o_ref[...] = jnp.zeros_like(o_ref)

    tm = h_ref.shape[0]
    count = cnt_ref[i * N_GROUPS + grp]
    grp_f = grp.astype(F32)
    sel_row = jnp.where(mr_ref[_META_GROUP:_META_GROUP + 1, :] == grp_f, mr_ref[_META_RANK:_META_RANK + 1, :], -1.0)
    sel_col = jnp.where(mc_ref[:, _META_GROUP:_META_GROUP + 1] == grp_f, mc_ref[:, _META_RANK:_META_RANK + 1], -1.0)
    gate_parts = _split3(mc_ref[...])

    def sub_block(first, rows):
        base = first.astype(F32)
        slot_r = lax.broadcasted_iota(jnp.int32, (rows, tm), 0).astype(F32)
        slot_c = lax.broadcasted_iota(jnp.int32, (tm, rows), 1).astype(F32)
        pick = (sel_row - base == slot_r).astype(BF16)
        put = (sel_col - base == slot_c).astype(BF16)
        xg = jnp.dot(pick, h_ref[...], preferred_element_type=F32).astype(BF16)
        gates = sum(jnp.dot(pick, p, preferred_element_type=F32) for p in gate_parts)
        y = jnp.zeros((rows, o_ref.shape[1]), F32)
        for e in range(EXPERTS_PER_GROUP):
            hid = _silu(_dot(xg, wg_ref[e])) * _dot(xg, wu_ref[e])
            y = y + _dot(gates[:, e:e + 1] * hid, wd_ref[e])
        yh = y.astype(BF16)
        yl = (y - yh.astype(F32)).astype(BF16)
        o_ref[...] += (jnp.dot(put, yh, preferred_element_type=F32)
                       + jnp.dot(put, yl, preferred_element_type=F32))

    n_full = count // MOE_SUB_ROWS
    rem = count - n_full * MOE_SUB_ROWS
    n_main = n_full + (rem > MOE_TAIL_ROWS).astype(jnp.int32)

    def main_block(s, carry):
        sub_block(s * MOE_SUB_ROWS, MOE_SUB_ROWS)
        return carry

    lax.fori_loop(0, n_main, main_block, 0)

    @pl.when((rem > 0) & (rem <= MOE_TAIL_ROWS))
    def _():
        sub_block(n_full * MOE_SUB_ROWS, MOE_TAIL_ROWS)


def moe_experts(hb, meta, counts, wg, wu, wd, *, tm):
    m, d = hb.shape
    tok = lambda i, g, cnt: (i, 0)
    grid_spec = pltpu.PrefetchScalarGridSpec(
        num_scalar_prefetch=1,
        grid=(m // tm, N_GROUPS),
        in_specs=[pl.BlockSpec((tm, d), tok),
                  pl.BlockSpec((SUBLANES, tm), lambda i, g, cnt: (0, i)),
                  pl.BlockSpec((tm, SUBLANES), tok),
                  pl.BlockSpec((EXPERTS_PER_GROUP, d, EXPERT_FF), lambda i, g, cnt: (g, 0, 0)),
                  pl.BlockSpec((EXPERTS_PER_GROUP, d, EXPERT_FF), lambda i, g, cnt: (g, 0, 0)),
                  pl.BlockSpec((EXPERTS_PER_GROUP, EXPERT_FF, d), lambda i, g, cnt: (g, 0, 0))],
        out_specs=pl.BlockSpec((tm, d), tok))
    return pl.pallas_call(
        _expert_kernel,
        grid_spec=grid_spec,
        out_shape=jax.ShapeDtypeStruct((m, d), F32),
        compiler_params=_cparams("parallel", "arbitrary"),
    )(counts[:, 0, :N_GROUPS].reshape(-1), hb, meta, meta.T, wg, wu, wd)


def _moe_residual_kernel(x_ref, y_ref, m_ref, o_ref, *, bsz, t_all, n_ctx):
    tm = x_ref.shape[0]
    row = pl.program_id(0) * tm + lax.broadcasted_iota(jnp.int32, (tm, 1), 0)
    mod = jnp.zeros(x_ref.shape, F32)
    ctx = jnp.zeros((tm, 1), jnp.bool_)
    for bi in range(bsz):
        lo = bi * t_all
        ctx = ctx | ((row >= lo) & (row < lo + n_ctx))
        lat = (row >= lo + n_ctx) & (row < lo + t_all)
        mod = mod + jnp.where(lat, m_ref[bi:bi + 1, :], 0.0)
    mod = mod + jnp.where(ctx, m_ref[bsz:bsz + 1, :], 0.0)
    o_ref[...] = x_ref[...] + mod * y_ref[...]


def moe_residual(x_all, y, gate_rows, *, n_ctx, tm=1024):
    b, t, d = x_all.shape
    m = b * t
    tm = _pick_tile(m, tm)
    tok = lambda i: (i, 0)
    out = pl.pallas_call(
        functools.partial(_moe_residual_kernel, bsz=b, t_all=t, n_ctx=n_ctx),
        grid=(m // tm,),
        in_specs=[pl.BlockSpec((tm, d), tok), pl.BlockSpec((tm, d), tok), _const_spec(gate_rows.shape)],
        out_specs=pl.BlockSpec((tm, d), tok),
        out_shape=jax.ShapeDtypeStruct((m, d), F32),
        compiler_params=_cparams("parallel"),
    )(x_all.reshape(m, d), y, gate_rows)
    return out.reshape(b, t, d)


def _final_norm_kernel(x_ref, w_ref, o_ref):
    x = x_ref[...]
    o_ref[...] = x * lax.rsqrt(jnp.mean(x * x, axis=-1, keepdims=True) + EPS) * w_ref[...]


def final_rms_norm(x, w, *, tm):
    m, d = x.shape
    return pl.pallas_call(
        _final_norm_kernel,
        grid=(m // tm,),
        in_specs=[pl.BlockSpec((tm, d), lambda i: (i, 0)), pl.BlockSpec((1, d), lambda i: (0, 0))],
        out_specs=pl.BlockSpec((tm, d), lambda i: (i, 0)),
        out_shape=jax.ShapeDtypeStruct((m, d), F32),
        compiler_params=_cparams("parallel"),
    )(x, w.reshape(1, d))


def _pack_w_in(w_in, mixer):
    cols = _SRC_COLS[mixer]
    return jnp.where(jnp.asarray(cols >= 0)[None, :], w_in[:, np.maximum(cols, 0)], 0.0).astype(BF16)


def mixer_scans(ps, lp, *, n_ctx):
    p_ssm, p_gla, p_rwkv, p_gdn = ps

    xs, bc, sm = ssm_prep(p_ssm, lp, n_ctx=n_ctx)
    neg_a = jnp.pad(-jnp.exp(lp["ssm_a_log"]).reshape(1, -1), ((0, 0), (0, LANES - 2 * SSM_HEADS)))
    ssm = tuple(ssd_scan(xs, bc, sm, neg_a, n_ctx=n_ctx)) + (xs, p_ssm)

    w2 = [jnp.zeros((LANES, GLA_HEADS * GLA_DK), F32).at[d * GLA_RANK:(d + 1) * GLA_RANK].set(lp["gla_w2"][d])
          for d in range(2)]
    gb = [_row(lp["gla_b"][d]) for d in range(2)]
    gla = tuple(gla_scan(p_gla, w2, gb, n_ctx=n_ctx)) + (p_gla,)

    r, k, v, kk, a, lw, g, bonus = rwkv_prep(p_rwkv, lp, n_ctx=n_ctx)
    rwkv = tuple(rwkv_scan(r, k, v, kk, a, lw, _row(lp["rwkv_k_a"]), n_ctx=n_ctx)) + (g, bonus)

    q, kd, vd, smd = gdn_prep(p_gdn, lp, n_ctx=n_ctx)
    gdn = tuple(gdn_scan(q, kd, vd, smd, n_ctx=n_ctx)) + (p_gdn,)
    return ssm, gla, rwkv, gdn


def _to_scan_order(t):
    b, n, d = t.shape
    return t.reshape(b, n // GRID_W, GRID_W, d).transpose(0, 2, 1, 3).reshape(b, n, d)


def _from_scan_order(t):
    b, n, d = t.shape
    return t.reshape(b, GRID_W, n // GRID_W, d).transpose(0, 2, 1, 3).reshape(b, n, d)


def kernel(x, c, ctx, c_ctx, ada_w, ada_b, norm_mix, norm_ffn, w_in, w_gate, w_branch, w_out, ssm_conv_w, ssm_conv_b, ssm_a_log, ssm_dt_bias, ssm_d, ssm_norm, gla_w2, gla_b, gla_norm, rwkv_mu, rwkv_w0, rwkv_w2, rwkv_a0, rwkv_a2, rwkv_g2, rwkv_k_k, rwkv_k_a, rwkv_r_k, rwkv_ln_w, rwkv_ln_b, gdn_conv_w, gdn_a_log, gdn_dt_bias, gdn_norm, router_w, router_b, moe_w_gate, moe_w_up, moe_w_down, final_norm):
    bsz, seq, d = x.shape
    n_ctx = ctx.shape[1]
    t_all = n_ctx + seq
    m_all = bsz * t_all

    cond = jnp.concatenate([jax.nn.silu(c), jax.nn.silu(c_ctx)[None]], 0)
    cond = jnp.pad(cond, ((0, SUBLANES - cond.shape[0]), (0, 0)))
    mods, mod_rows = [], []
    for l in range(DEPTH):
        mod = pmatmul(cond, ada_w[l], tm=SUBLANES, tn=1024, precise=True) + ada_b[l]
        mod_rows.append(mod)
        lat = mod[:bsz].reshape(bsz, 6, d)
        cx = jnp.broadcast_to(mod[bsz].reshape(1, 6, d), (bsz, 6, d))
        mods.append(jnp.stack([cx, lat], axis=1))

    x_all = jnp.concatenate([ctx, x], axis=1)
    scan_order = False
    for l in range(DEPTH):
        if (l % 2 == 1) != scan_order:
            reorder = _from_scan_order if scan_order else _to_scan_order
            x_all = jnp.concatenate([x_all[:, :n_ctx], reorder(x_all[:, n_ctx:])], axis=1)
            scan_order = not scan_order
        lp = dict(ssm_conv_w=ssm_conv_w[l], ssm_conv_b=ssm_conv_b[l], ssm_a_log=ssm_a_log[l],
                  ssm_dt_bias=ssm_dt_bias[l], ssm_d=ssm_d[l], ssm_norm=ssm_norm[l],
                  gla_w2=gla_w2[l], gla_b=gla_b[l], gla_norm=gla_norm[l],
                  rwkv_mu=rwkv_mu[l], rwkv_w0=rwkv_w0[l], rwkv_w2=rwkv_w2[l], rwkv_a0=rwkv_a0[l],
                  rwkv_a2=rwkv_a2[l], rwkv_g2=rwkv_g2[l], rwkv_k_k=rwkv_k_k[l], rwkv_k_a=rwkv_k_a[l],
                  rwkv_r_k=rwkv_r_k[l], rwkv_ln_w=rwkv_ln_w[l], rwkv_ln_b=rwkv_ln_b[l],
                  gdn_conv_w=gdn_conv_w[l], gdn_a_log=gdn_a_log[l], gdn_dt_bias=gdn_dt_bias[l],
                  gdn_norm=gdn_norm[l])
        mod = mods[l]
        msel = lambda i: mod[:, :, i][:, :, None, :]

        h = norm_modulate(x_all, norm_mix[l], msel(0), msel(1), n_ctx=n_ctx)
        h2d = h.reshape(m_all, d)
        ps = []
        for mixer in ("ssm", "gla", "rwkv", "gdn"):
            wp = _pack_w_in(w_in[l], mixer)
            ps.append(pmatmul(h2d, wp, tm=512, tn=wp.shape[1]).reshape(bsz, t_all, wp.shape[1]))
        wg_cat = jnp.concatenate([w_gate[l, i] for i in range(4)], axis=1).astype(BF16)
        gates = pmatmul(h2d, wg_cat, tm=1024, tn=1024, act="sigmoid", out_dtype=BF16)
        gates = gates.reshape(bsz, t_all, 4 * d)

        ssm, gla, rwkv, gdn = mixer_scans(ps, lp, n_ctx=n_ctx)
        x_all = merge_residual(ssm, gla, rwkv, gdn, gates, x_all, msel(2), lp,
                               w_branch[l].astype(BF16), w_out[l].astype(BF16), n_ctx=n_ctx)

        h = norm_modulate(x_all, norm_ffn[l], msel(3), msel(4), n_ctx=n_ctx, out_dtype=F32)
        tm_moe = _pick_tile(m_all, MOE_TILE)
        meta, counts, hb = moe_route(h.reshape(m_all, d), router_w, router_b, tm=tm_moe)
        y = moe_experts(hb, meta, counts, moe_w_gate[l].astype(BF16), moe_w_up[l].astype(BF16),
                        moe_w_down[l].astype(BF16), tm=tm_moe)
        x_all = moe_residual(x_all, y, mod_rows[l][:, 5 * d:], n_ctx=n_ctx)

    lat = x_all[:, n_ctx:]
    if scan_order:
        lat = _from_scan_order(lat)
    return final_rms_norm(lat.reshape(bsz * seq, d), final_norm, tm=1024).reshape(bsz, seq, d)
```

```python
import functools
import itertools

import numpy as np
import jax
import jax.numpy as jnp
from jax import lax
from jax.experimental import pallas as pl
from jax.experimental.pallas import tpu as pltpu

F32 = jnp.float32
BF16 = jnp.bfloat16
HI = lax.Precision.HIGHEST

D_MODEL = 1024
DEPTH = 2
GRID_W = 64
CHUNK = 64
EPS = 1e-6
BRANCH = D_MODEL // 2
SSM_HEADS, SSM_P, SSM_GROUPS, SSM_N = 8, 64, 2, 64
GLA_HEADS, GLA_DK, GLA_DV, GLA_RANK, GLA_TAU = 4, 64, 128, 16, 16.0
RWKV_HEADS, RWKV_N, RWKV_LN_EPS = 8, 64, 64e-5
GDN_HEADS, GDN_N = 4, 128
N_EXPERTS, N_GROUPS, EXPERTS_PER_GROUP = 16, 4, 4
EXPERT_FF = D_MODEL // 2
LANES = 128
SUBLANES = 8
VMEM_LIMIT = 48 * 1024 * 1024
MOE_VMEM_LIMIT = 56 * 1024 * 1024
ROW_TILE = 256

_REF_BLOCKS = (
    ("ssm", "z", 512), ("ssm", "xbc", 768), ("ssm", "dt", 16),
    ("gla", "q", 256), ("gla", "k", 256), ("gla", "v", 512), ("gla", "r", 512), ("gla", "glr", 32),
    ("rwkv", "all", 1920),
    ("gdn", "qkv", 1536), ("gdn", "z", 512), ("gdn", "ab", 16),
)
_PACKED = {
    "ssm": (("z", 512), ("dt", 128), ("pad", 128), ("xbc", 768)),
    "gla": (("q", 256), ("k", 256), ("v", 512), ("r", 512), ("glr", 128)),
    "rwkv": (("all", 1920),),
    "gdn": (("qkv", 1536), ("z", 512), ("ab", 128)),
}


def _packed_columns():
    start, s = {}, 0
    for mixer, blk, w in _REF_BLOCKS:
        start[(mixer, blk)] = (s, w)
        s += w
    out = {}
    for mixer, blocks in _PACKED.items():
        cols = []
        for blk, wp in blocks:
            s0, w = start.get((mixer, blk), (0, 0))
            cols += list(range(s0, s0 + w)) + [-1] * (wp - w)
        out[mixer] = np.asarray(cols, np.int32)
    return out


_SRC_COLS = _packed_columns()


def _cparams(*sem):
    return pltpu.CompilerParams(dimension_semantics=sem, vmem_limit_bytes=VMEM_LIMIT)


def _dot(a, b):
    return jnp.dot(a.astype(BF16), b.astype(BF16), preferred_element_type=F32)


def _dot_nt(a, b):
    return lax.dot_general(a.astype(BF16), b.astype(BF16), (((1,), (1,)), ((), ())),
                           preferred_element_type=F32)


def _dot_tn(a, b):
    return lax.dot_general(a.astype(BF16), b.astype(BF16), (((0,), (0,)), ((), ())),
                           preferred_element_type=F32)


def _dot_hi(a, b):
    return jnp.dot(a, b, precision=HI, preferred_element_type=F32)


def _dot_x3(a, b):
    ah = a.astype(BF16)
    al = (a - ah.astype(F32)).astype(BF16)
    bh = b.astype(BF16)
    bl = (b - bh.astype(F32)).astype(BF16)
    f = lambda u, v: jnp.dot(u, v, preferred_element_type=F32)
    return f(ah, bh) + (f(ah, bl) + f(al, bh))


def _dot_x2(a, w):
    ah = a.astype(BF16)
    al = (a - ah.astype(F32)).astype(BF16)
    return jnp.dot(ah, w, preferred_element_type=F32) + jnp.dot(al, w, preferred_element_type=F32)


def _softplus(x):
    return jnp.maximum(x, 0.0) + jnp.log(1.0 + jnp.exp(-jnp.abs(x)))


def _sigmoid(x):
    return 1.0 / (1.0 + jnp.exp(-x))


def _silu(x):
    return x * _sigmoid(x)


def _pick_tile(m, pref):
    t = pref
    while m % t:
        t //= 2
    return t


def _block_diag_ones(n, width=BRANCH):
    idx = np.arange(width) // n
    return jnp.asarray(idx[:, None] == idx[None, :], BF16)


def _mm_kernel(a_ref, w_ref, o_ref, *, act, precise):
    if precise:
        r = _dot_hi(a_ref[...].astype(F32), w_ref[...].astype(F32))
    else:
        r = _dot(a_ref[...], w_ref[...])
    if act == "sigmoid":
        r = _sigmoid(r)
    o_ref[...] = r.astype(o_ref.dtype)


def pmatmul(a, w, *, tm, tn, act=None, precise=False, out_dtype=F32):
    m, k = a.shape
    n = w.shape[1]
    tm = _pick_tile(m, tm)
    assert tm % SUBLANES == 0 and n % tn == 0, (m, tm, n, tn)
    return pl.pallas_call(
        functools.partial(_mm_kernel, act=act, precise=precise),
        grid=(n // tn, m // tm),
        in_specs=[pl.BlockSpec((tm, k), lambda j, i: (i, 0)),
                  pl.BlockSpec((k, tn), lambda j, i: (0, j))],
        out_specs=pl.BlockSpec((tm, tn), lambda j, i: (i, j)),
        out_shape=jax.ShapeDtypeStruct((m, n), out_dtype),
        compiler_params=_cparams("parallel", "parallel"),
    )(a, w)


def _norm_mod_kernel(x_ref, w_ref, shift_ref, scale_ref, o_ref):
    x = x_ref[0]
    y = x * lax.rsqrt(jnp.mean(x * x, axis=-1, keepdims=True) + EPS) * w_ref[...]
    o_ref[0] = (y * (1.0 + scale_ref[0, 0]) + shift_ref[0, 0]).astype(o_ref.dtype)


def _mod_sel(nct):
    return lambda bi, i, *_: (bi, jnp.where(i < nct, 0, 1), 0, 0)


def norm_modulate(x_all, w, shift, scale, *, n_ctx, out_dtype=BF16):
    b, t, d = x_all.shape
    tm = _pick_tile(n_ctx, ROW_TILE)
    assert t % tm == 0
    tok = lambda bi, i: (bi, i, 0)
    return pl.pallas_call(
        _norm_mod_kernel,
        grid=(b, t // tm),
        in_specs=[pl.BlockSpec((1, tm, d), tok),
                  pl.BlockSpec((1, d), lambda bi, i: (0, 0)),
                  pl.BlockSpec((1, 1, 1, d), _mod_sel(n_ctx // tm)),
                  pl.BlockSpec((1, 1, 1, d), _mod_sel(n_ctx // tm))],
        out_specs=pl.BlockSpec((1, tm, d), tok),
        out_shape=jax.ShapeDtypeStruct((b, t, d), out_dtype),
        compiler_params=_cparams("parallel", "parallel"),
    )(x_all, w.reshape(1, d), shift, scale)


def _row(v):
    return v.reshape(1, -1).astype(F32)


def _const_spec(shape):
    return pl.BlockSpec(shape, lambda *_: (0,) * len(shape))


def _tile_specs(tt, width, col):
    r8 = tt // SUBLANES
    main = pl.BlockSpec((1, tt, width), lambda bi, i: (bi, i, col))
    prev = pl.BlockSpec((1, SUBLANES, width), lambda bi, i: (bi, jnp.maximum(i * r8 - 1, 0), col))
    return main, prev, r8


def _halo_specs(tt, width, col, t):
    main, prev, r8 = _tile_specs(tt, width, col)
    last8 = t // SUBLANES - 1
    nxt = pl.BlockSpec((1, SUBLANES, width), lambda bi, i: (bi, jnp.minimum((i + 1) * r8, last8), col))
    return [main, prev, nxt]


def _neighbours(x, prev8, next8, *, nct, nt):
    i = pl.program_id(1)
    tt = x.shape[0]
    row = lax.broadcasted_iota(jnp.int32, x.shape, 0)
    first = (i == 0) | (i == nct)
    last = (i == nct - 1) | (i == nt - 1)
    pr = jnp.where(first, 0.0, prev8[SUBLANES - 1:SUBLANES, :])
    nx = jnp.where(last, 0.0, next8[0:1, :])
    xp = jnp.where(row == 0, pr, pltpu.roll(x, 1, 0))
    xn = jnp.where(row == tt - 1, nx, pltpu.roll(x, tt - 1, 0))
    return xp, xn


def _prep_call(kernel, ins, in_specs, out_widths, *, b, t, tt, out_dtype=F32):
    tok = lambda bi, i: (bi, i, 0)
    return pl.pallas_call(
        kernel,
        grid=(b, t // tt),
        in_specs=in_specs,
        out_specs=[pl.BlockSpec((1, tt, w), tok) for w in out_widths],
        out_shape=[jax.ShapeDtypeStruct((b, t, w), out_dtype) for w in out_widths],
        compiler_params=_cparams("parallel", "parallel"),
    )(*ins)


def _chunk_masks(reverse):
    row = lax.broadcasted_iota(jnp.int32, (CHUNK, CHUNK), 0)
    col = lax.broadcasted_iota(jnp.int32, (CHUNK, CHUNK), 1)
    if reverse:
        return col >= row, col > row
    return col <= row, col < row


def _chunk_order(i, n_ctx_chunks, n_chunks, reverse):
    if not reverse:
        return i
    return jnp.where(i < n_ctx_chunks, n_ctx_chunks - 1 - i, n_chunks - 1 - (i - n_ctx_chunks))


def _split3(a):
    hi = a.astype(BF16)
    r = a - hi.astype(F32)
    mid = r.astype(BF16)
    return hi, mid, (r - mid.astype(F32)).astype(BF16)


def _transpose_small(x):
    row = lax.broadcasted_iota(jnp.int32, (LANES, LANES), 0)
    col = lax.broadcasted_iota(jnp.int32, (LANES, LANES), 1)
    eye = (row == col).astype(BF16)
    nt = lambda p: lax.dot_general(eye, p, (((1,), (1,)), ((), ())), preferred_element_type=F32)
    hi, mid, lo = _split3(x)
    return nt(hi) + (nt(mid) + nt(lo))


def _chunk_cumsum(incl, x):
    m = incl.astype(BF16)
    hi, mid, lo = _split3(x)
    f = lambda p: jnp.dot(m, p, preferred_element_type=F32)
    return f(hi) + (f(mid) + f(lo))


def _select_columns(x, sel):
    c = x.shape[0]
    y = jnp.dot(jnp.concatenate(_split3(x), axis=0), sel, preferred_element_type=F32)
    return y[:c] + (y[c:2 * c] + y[2 * c:])


def _unit_tri_solve(mats, rhs, precise_levels=0):
    n = range(len(mats))
    x = [rhs[h] - _dot_x3(mats[h], rhs[h]) for h in n]
    yield
    p = mats
    for level in range(int(np.log2(CHUNK)) - 1):
        dot = _dot_x3 if level < precise_levels else _dot
        p = [dot(p[h], p[h]) for h in n]
        yield
        x = [x[h] + dot(p[h], x[h]) for h in n]
        yield
    return x


def _bidir_scan(body, tok_ins, const_ins, state_shape, *, b, t, n_ctx, lockstep=True, batch_block=1):
    nc, ncc = t // CHUNK, n_ctx // CHUNK
    nb = batch_block
    assert b % nb == 0

    def chunk_spec(width, col, reverse):
        return pl.BlockSpec((nb, CHUNK, width), lambda bi, i: (bi, _chunk_order(i, ncc, nc, reverse), col))

    def direction(reverse):
        d = int(reverse)
        specs = [chunk_spec(w, cols[d], reverse) for _, w, *cols in tok_ins]
        specs += [_const_spec(pair[d].shape) for pair in const_ins]
        return specs, [a for a, *_ in tok_ins] + [pair[d] for pair in const_ins]

    (spec_f, arg_f), (spec_b, arg_b) = direction(False), direction(True)
    n_tok, n_in = len(tok_ins), len(arg_f)

    def kern(*refs):
        o_f, o_b, s_f, s_b = refs[2 * n_in:]

        @pl.when(pl.program_id(1) == 0)
        def _():
            s_f[...] = jnp.zeros_like(s_f)
            s_b[...] = jnp.zeros_like(s_b)

        def one(j, ins, o_ref, s_ref, reverse):
            ins = [r.at[pl.ds(j, 1)] if k < n_tok else r for k, r in enumerate(ins)]
            return body(*ins, o_ref.at[pl.ds(j, 1)], s_ref.at[j], reverse=reverse)

        gens = []
        for j in range(nb):
            gens += [one(j, refs[:n_in], o_f, s_f, False), one(j, refs[n_in:2 * n_in], o_b, s_b, True)]
        if not lockstep:
            gens = [itertools.chain(*gens)]
        while gens:
            gens = [g for g in gens if next(g, _DONE) is not _DONE]

    return pl.pallas_call(
        kern,
        grid=(b // nb, nc),
        in_specs=spec_f + spec_b,
        out_specs=[chunk_spec(BRANCH, 0, False), chunk_spec(BRANCH, 0, True)],
        out_shape=[jax.ShapeDtypeStruct((b, t, BRANCH), F32)] * 2,
        scratch_shapes=[pltpu.VMEM((nb,) + tuple(state_shape), F32)] * 2,
        compiler_params=_cparams("parallel", "arbitrary"),
    )(*arg_f, *arg_b)


_DONE = object()


def _batch_block(b, pref):
    return pref if b % pref == 0 else 1


def _ssm_prep_kernel(x_ref, xp_ref, xn_ref, dt_ref, cw_ref, cb_ref, dtb_ref, xs_ref, bc_ref, sm_ref,
                     *, nct, nt):
    x = x_ref[0]
    xp, xn = _neighbours(x, xp_ref[0], xn_ref[0], nct=nct, nt=nt)
    y = _silu(xp * cw_ref[0:1, :] + x * cw_ref[1:2, :] + xn * cw_ref[2:3, :] + cb_ref[...])
    xs_ref[0] = y[:, :BRANCH]
    bc_ref[0] = y[:, BRANCH:]
    sm_ref[0] = _softplus(dt_ref[0] + dtb_ref[...])


def ssm_prep(p, lp, *, n_ctx):
    b, t, _ = p.shape
    tt = _pick_tile(n_ctx, ROW_TILE)
    dtb = jnp.pad(lp["ssm_dt_bias"].reshape(1, -1), ((0, 0), (0, LANES - 2 * SSM_HEADS)))
    specs = _halo_specs(tt, 768, 1, t) + [pl.BlockSpec((1, tt, LANES), lambda bi, i: (bi, i, 4)),
                                          _const_spec((3, 768)), _const_spec((1, 768)), _const_spec((1, LANES))]
    kern = functools.partial(_ssm_prep_kernel, nct=n_ctx // tt, nt=t // tt)
    return _prep_call(kern, (p, p, p, p, lp["ssm_conv_w"], _row(lp["ssm_conv_b"]), dtb), specs,
                      (BRANCH, 2 * SSM_GROUPS * SSM_N, LANES), b=b, t=t, tt=tt)


def _ssd_body(x_ref, bc_ref, sm_ref, na_ref, o_ref, s_ref, *, reverse):
    incl, _ = _chunk_masks(reverse)
    last = 0 if reverse else CHUNK - 1
    off = SSM_HEADS if reverse else 0
    dt_all = sm_ref[0]
    g_all = _chunk_cumsum(incl, dt_all * na_ref[...])
    yield
    expand = _expand_matrix(off, SSM_HEADS, SSM_P)
    gx = _select_columns(g_all, expand)
    dx = _select_columns(dt_all, expand)
    gt_all = _transpose_small(g_all)
    dtt_all = _transpose_small(dt_all)
    yield
    heads = range(SSM_HEADS)
    rep = SSM_HEADS // SSM_GROUPS
    gw = SSM_GROUPS * SSM_N
    pw = rep * SSM_P
    hs = [slice(h * SSM_P, (h + 1) * SSM_P) for h in heads]
    glx = gx[last:last + 1, :]
    egx = jnp.exp(gx)
    wx = dx * jnp.exp(glx - gx)
    eglx = jnp.exp(glx)
    x = x_ref[0]
    bm = [bc_ref[0, :, grp * SSM_N:(grp + 1) * SSM_N] for grp in range(SSM_GROUPS)]
    cm = [bc_ref[0, :, gw + grp * SSM_N:gw + (grp + 1) * SSM_N] for grp in range(SSM_GROUPS)]
    cb = [_dot_nt(cm[grp], bm[grp]) for grp in range(SSM_GROUPS)]
    s = [s_ref[grp] for grp in range(SSM_GROUPS)]
    yield
    scores = [cb[h // rep] * jnp.exp(jnp.where(incl, gx[:, hs[h]] - gt_all[off + h:off + h + 1, :], -jnp.inf))
              * dtt_all[off + h:off + h + 1, :] for h in heads]
    yield
    intra = [_dot(scores[h], x[:, hs[h]]) for h in heads]
    yield
    inter = [_dot(cm[grp], s[grp]) for grp in range(SSM_GROUPS)]
    yield
    upd = [_dot_tn(bm[h // rep] * wx[:, hs[h]], x[:, hs[h]]) for h in heads]
    yield
    for h in heads:
        grp, ls = h // rep, slice((h % rep) * SSM_P, (h % rep + 1) * SSM_P)
        o_ref[0, :, hs[h]] = intra[h] + egx[:, hs[h]] * inter[grp][:, ls]
        s_ref[grp, :, ls] = s[grp][:, ls] * eglx[:, hs[h]] + upd[h]


def _expand_matrix(off, n_heads, width):
    row = lax.broadcasted_iota(jnp.int32, (LANES, n_heads * width), 0)
    col = lax.broadcasted_iota(jnp.int32, (LANES, n_heads * width), 1)
    lo = row * width - off * width
    return ((col >= lo) & (col < lo + width)).astype(BF16)


def ssd_scan(xs, bc, sm, neg_a, *, n_ctx):
    b, t, _ = xs.shape
    toks = [(xs, BRANCH, 0, 0), (bc, 2 * SSM_GROUPS * SSM_N, 0, 0), (sm, LANES, 0, 0)]
    state = (SSM_GROUPS, SSM_N, (SSM_HEADS // SSM_GROUPS) * SSM_P)
    return _bidir_scan(_ssd_body, toks, [(neg_a, neg_a)], state, b=b, t=t, n_ctx=n_ctx,
                       batch_block=_batch_block(b, 2))


def _gla_body(q_ref, k_ref, v_ref, glr_ref, w2_ref, gb_ref, o_ref, s_ref, *, reverse):
    incl, _ = _chunk_masks(reverse)
    last = 0 if reverse else CHUNK - 1
    logit = _dot_x3(glr_ref[0], w2_ref[...]) + gb_ref[...]
    yield
    la = -_softplus(-logit) * (1.0 / GLA_TAU)
    g_all = _chunk_cumsum(incl, la)
    yield
    heads = range(GLA_HEADS)
    ks = [slice(h * GLA_DK, (h + 1) * GLA_DK) for h in heads]
    vs = [slice(h * GLA_DV, (h + 1) * GLA_DV) for h in heads]
    g = [g_all[:, ks[h]] for h in heads]
    gl = [g[h][last:last + 1, :] for h in heads]
    k = [k_ref[0, :, ks[h]] for h in heads]
    v = [v_ref[0, :, vs[h]] for h in heads]
    qg = [q_ref[0, :, ks[h]] * (GLA_DK ** -0.5) * jnp.exp(g[h]) for h in heads]
    st = [s_ref[h] for h in heads]
    yield
    scores = [jnp.where(incl, _dot_nt(qg[h], k[h] * jnp.exp(-g[h])), 0.0) for h in heads]
    yield
    intra = [_dot(scores[h], v[h]) for h in heads]
    yield
    inter = [_dot_nt(qg[h], st[h]) for h in heads]
    yield
    upd = [_dot_tn(v[h], k[h] * jnp.exp(gl[h] - g[h])) for h in heads]
    yield
    for h in heads:
        o_ref[0, :, vs[h]] = intra[h] + inter[h]
        s_ref[h] = st[h] * jnp.exp(gl[h]) + upd[h]


def gla_scan(p, w2_pair, gb_pair, *, n_ctx):
    b, t, _ = p.shape
    kwid = GLA_HEADS * GLA_DK
    toks = [(p, kwid, 0, 0), (p, kwid, 1, 1), (p, BRANCH, 1, 1), (p, LANES, 12, 12)]
    return _bidir_scan(_gla_body, toks, [w2_pair, gb_pair], (GLA_HEADS, GLA_DV, GLA_DK), b=b, t=t, n_ctx=n_ctx,
                       batch_block=_batch_block(b, 4))


def _rwkv_prep_kernel(x_ref, xp_ref, xn_ref, mu_ref, w2_ref, w0_ref, a2_ref, a0_ref, g2_ref, kk_ref_w,
                      ka_ref, rk_ref, bd_ref, r_ref, k_ref, v_ref, kk_ref, a_ref, lw_ref, g_ref, bo_ref,
                      *, nct, nt):
    x = x_ref[0]
    xp, xn = _neighbours(x, xp_ref[0], xn_ref[0], nct=nct, nt=nt)
    x = x + mu_ref[...] * (0.5 * (xp + xn) - x)
    r, k, v = x[:, :BRANCH], x[:, BRANCH:2 * BRANCH], x[:, 2 * BRANCH:3 * BRANCH]
    wlr = x[:, 3 * BRANCH:3 * BRANCH + LANES]
    alr = x[:, 3 * BRANCH + LANES:3 * BRANCH + 2 * LANES]
    glr = x[:, 3 * BRANCH + 2 * LANES:]
    w_raw = _dot_x3(jnp.tanh(wlr), w2_ref[...]) + w0_ref[...]
    lw_ref[0] = -jnp.exp(-_softplus(-w_raw) - 0.5)
    a = _sigmoid(_dot_x3(alr, a2_ref[...]) + a0_ref[...])
    a_ref[0] = a
    g_ref[0] = _dot_x3(_sigmoid(glr), g2_ref[...])
    kk = k * kk_ref_w[...]
    kk_ref[0] = kk * lax.rsqrt(_dot_x2(kk * kk, bd_ref[...]) + EPS)
    ksum = k * (2.0 + (a[:, :BRANCH] + a[:, BRANCH:] - 2.0) * ka_ref[...])
    bo_ref[0] = _dot_x2(r * ksum * rk_ref[...], bd_ref[...]) * v
    r_ref[0] = r
    k_ref[0] = k
    v_ref[0] = v


def rwkv_prep(p, lp, *, n_ctx):
    b, t, w = p.shape
    tt = _pick_tile(n_ctx, ROW_TILE)

    def pair(wp):
        r, c = wp.shape[1:]
        return jnp.zeros((LANES, 2 * c), F32).at[:r, :c].set(wp[0]).at[r:2 * r, c:].set(wp[1])

    consts = (_row(lp["rwkv_mu"]), pair(lp["rwkv_w2"]), _row(lp["rwkv_w0"]), pair(lp["rwkv_a2"]),
              _row(lp["rwkv_a0"]), lp["rwkv_g2"], _row(lp["rwkv_k_k"]), _row(lp["rwkv_k_a"]),
              _row(lp["rwkv_r_k"]), _block_diag_ones(RWKV_N))
    specs = _halo_specs(tt, w, 0, t) + [_const_spec(c.shape) for c in consts]
    kern = functools.partial(_rwkv_prep_kernel, nct=n_ctx // tt, nt=t // tt)
    return _prep_call(kern, (p, p, p) + consts, specs,
                      (BRANCH, BRANCH, BRANCH, BRANCH, 2 * BRANCH, 2 * BRANCH, BRANCH, BRANCH),
                      b=b, t=t, tt=tt)


def _rwkv_body(r_ref, k_ref, v_ref, kk_ref, a_ref, lw_ref, ka_ref, o_ref, s_ref, *, reverse):
    incl, strict = _chunk_masks(reverse)
    last = 0 if reverse else CHUNK - 1
    lw_all = lw_ref[0]
    g_all = _chunk_cumsum(incl, lw_all)
    a_all = a_ref[0]
    k_all = k_ref[0] * (1.0 + (a_all - 1.0) * ka_ref[...])
    yield
    heads = range(RWKV_HEADS)
    hs = [slice(h * RWKV_N, (h + 1) * RWKV_N) for h in heads]
    g = [g_all[:, hs[h]] for h in heads]
    gl = [g[h][last:last + 1, :] for h in heads]
    eneg = [jnp.exp(-g[h]) for h in heads]
    edec = [jnp.exp(gl[h] - g[h]) for h in heads]
    kk = [kk_ref[0, :, hs[h]] for h in heads]
    bvec = [kk[h] * a_all[:, hs[h]] for h in heads]
    k = [k_all[:, hs[h]] for h in heads]
    v = [v_ref[0, :, hs[h]] for h in heads]
    kkg = [kk[h] * jnp.exp(g[h] - lw_all[:, hs[h]]) for h in heads]
    rg = [r_ref[0, :, hs[h]] * jnp.exp(g[h]) for h in heads]
    bh = [bvec[h] * eneg[h] for h in heads]
    kh = [k[h] * eneg[h] for h in heads]
    s = [s_ref[h] for h in heads]
    yield
    both = [jnp.concatenate([kkg[h], rg[h]], axis=0) for h in heads]
    mask2 = jnp.concatenate([strict, incl], axis=0)
    mb = [jnp.where(mask2, _dot_nt(both[h], bh[h]), 0.0) for h in heads]
    yield
    mk = [jnp.where(mask2, _dot_nt(both[h], kh[h]), 0.0) for h in heads]
    yield
    part = [_dot(mk[h], v[h]) + _dot_nt(both[h], s[h]) for h in heads]
    yield
    x = yield from _unit_tri_solve([mb[h][:CHUNK] for h in heads], [part[h][:CHUNK] for h in heads])
    u = [-xh for xh in x]
    for h in heads:
        o_ref[0, :, hs[h]] = part[h][CHUNK:] + _dot(mb[h][CHUNK:], u[h])
    yield
    for h in heads:
        upd = _dot_tn(jnp.concatenate([u[h], v[h]], axis=0),
                      jnp.concatenate([bvec[h] * edec[h], k[h] * edec[h]], axis=0))
        s_ref[h] = s[h] * jnp.exp(gl[h]) + upd


def rwkv_scan(r, k, v, kk, a, lw, k_a, *, n_ctx):
    b, t, _ = r.shape
    toks = [(r, BRANCH, 0, 0), (k, BRANCH, 0, 0), (v, BRANCH, 0, 0), (kk, BRANCH, 0, 0),
            (a, BRANCH, 0, 1), (lw, BRANCH, 0, 1)]
    return _bidir_scan(_rwkv_body, toks, [(k_a, k_a)], (RWKV_HEADS, RWKV_N, RWKV_N), b=b, t=t, n_ctx=n_ctx,
                       batch_block=_batch_block(b, 2))


def _gdn_prep_kernel(x_ref, xp_ref, xn_ref, ab_ref, cw_ref, na_ref, dtb_ref, bd_ref,
                     q_ref, k_ref, v_ref, sm_ref, *, nct, nt):
    x = x_ref[0]
    xp, xn = _neighbours(x, xp_ref[0], xn_ref[0], nct=nct, nt=nt)
    y = _silu(xp * cw_ref[0:1, :] + x * cw_ref[1:2, :] + xn * cw_ref[2:3, :])
    q, k = y[:, :BRANCH], y[:, BRANCH:2 * BRANCH]
    q_ref[0] = q * lax.rsqrt(_dot_x2(q * q, bd_ref[...]) + EPS) * (GDN_N ** -0.5)
    k_ref[0] = k * lax.rsqrt(_dot_x2(k * k, bd_ref[...]) + EPS)
    v_ref[0] = y[:, 2 * BRANCH:]
    ab = ab_ref[0]
    lane = lax.broadcasted_iota(jnp.int32, ab.shape, 1)
    sm_ref[0] = jnp.where(lane < 2 * GDN_HEADS, na_ref[...] * _softplus(ab + dtb_ref[...]), _sigmoid(ab))


def gdn_prep(p, lp, *, n_ctx):
    b, t, _ = p.shape
    tt = _pick_tile(n_ctx, ROW_TILE)
    padrow = lambda v: jnp.pad(v.reshape(1, -1), ((0, 0), (0, LANES - 2 * GDN_HEADS)))
    consts = (lp["gdn_conv_w"], padrow(-jnp.exp(lp["gdn_a_log"])), padrow(lp["gdn_dt_bias"]),
              _block_diag_ones(GDN_N))
    specs = (_halo_specs(tt, 3 * BRANCH, 0, t) + [pl.BlockSpec((1, tt, LANES), lambda bi, i: (bi, i, 16))]
             + [_const_spec(c.shape) for c in consts])
    kern = functools.partial(_gdn_prep_kernel, nct=n_ctx // tt, nt=t // tt)
    return _prep_call(kern, (p, p, p, p) + consts, specs, (BRANCH, BRANCH, BRANCH, LANES), b=b, t=t, tt=tt)


def _gdn_body(q_ref, k_ref, v_ref, sm_ref, o_ref, s_ref, *, reverse):
    incl, strict = _chunk_masks(reverse)
    last = 0 if reverse else CHUNK - 1
    off = GDN_HEADS if reverse else 0
    sm = sm_ref[0]
    g_all = _chunk_cumsum(incl, sm)
    yield
    gt_all = _transpose_small(g_all)
    yield
    heads = range(GDN_HEADS)
    hs = [slice(h * GDN_N, (h + 1) * GDN_N) for h in heads]
    g = [g_all[:, off + h:off + h + 1] for h in heads]
    gl = [g[h][last:last + 1, :] for h in heads]
    beta = [sm[:, 2 * GDN_HEADS + off + h:2 * GDN_HEADS + off + h + 1] for h in heads]
    q = [q_ref[0, :, hs[h]] for h in heads]
    k = [k_ref[0, :, hs[h]] for h in heads]
    v = [v_ref[0, :, hs[h]] for h in heads]
    s = [s_ref[h] for h in heads]
    decay = [jnp.exp(jnp.where(incl, g[h] - gt_all[off + h:off + h + 1, :], -jnp.inf)) for h in heads]
    yield
    kq = [_dot_nt(jnp.concatenate([k[h], q[h]], axis=0), k[h]) for h in heads]
    yield
    lower = [jnp.where(strict, kq[h][:CHUNK] * decay[h] * beta[h], 0.0) for h in heads]
    attn = [kq[h][CHUNK:] * decay[h] for h in heads]
    o_part = [_dot(q[h] * jnp.exp(g[h]), s[h]) for h in heads]
    yield
    rhs = [jnp.concatenate([v[h] * beta[h], k[h] * (beta[h] * jnp.exp(g[h]))], axis=1) for h in heads]
    sol = yield from _unit_tri_solve(lower, rhs, precise_levels=2)
    v_new = [sol[h][:, :GDN_N] - _dot(sol[h][:, GDN_N:], s[h]) for h in heads]
    yield
    for h in heads:
        o_ref[0, :, hs[h]] = o_part[h] + _dot(attn[h], v_new[h])
    yield
    for h in heads:
        s_ref[h] = s[h] * jnp.exp(gl[h]) + _dot_tn(k[h] * jnp.exp(gl[h] - g[h]), v_new[h])


def gdn_scan(q, k, v, sm, *, n_ctx):
    b, t, _ = q.shape
    toks = [(q, BRANCH, 0, 0), (k, BRANCH, 0, 0), (v, BRANCH, 0, 0), (sm, LANES, 0, 0)]
    return _bidir_scan(_gdn_body, toks, [], (GDN_HEADS, GDN_N, GDN_N), b=b, t=t, n_ctx=n_ctx,
                       batch_block=_batch_block(b, 4))


def _merge_kernel(sf_ref, sb_ref, sx_ref, sz_ref, gf_ref, gb_ref, gr_ref, rf_ref, rb_ref, rg_ref, rbo_ref,
                  df_ref, db_ref, dz_ref, gate_ref, x_ref, m_ref,
                  sd_ref, sn_ref, gn_ref, lnw_ref, lnb_ref, dn_ref, bd64_ref, bd128_ref, bd256_ref,
                  wb_ref, wo_ref, o_ref):
    def group_rms(y, bd_ref, n, w_ref):
        return y * lax.rsqrt(_dot_x2(y * y, bd_ref[...]) * (1.0 / n) + EPS) * w_ref[...]

    y = (sf_ref[0] + sb_ref[0] + sd_ref[...] * sx_ref[0]) * _silu(sz_ref[0])
    ys = group_rms(y, bd256_ref, BRANCH // SSM_GROUPS, sn_ref)
    yg = group_rms(gf_ref[0] + gb_ref[0], bd128_ref, GLA_DV, gn_ref) * _silu(gr_ref[0])
    y = rf_ref[0] + rb_ref[0]
    yc = y - _dot_x2(y, bd64_ref[...]) * (1.0 / RWKV_N)
    var = _dot_x2(yc * yc, bd64_ref[...]) * (1.0 / RWKV_N)
    yr = (yc * lax.rsqrt(var + RWKV_LN_EPS) * lnw_ref[...] + lnb_ref[...] + rbo_ref[0]) * rg_ref[0]
    yd = group_rms(df_ref[0] + db_ref[0], bd128_ref, GDN_N, dn_ref) * _silu(dz_ref[0])
    acc = None
    for i, yi in enumerate((ys, yg, yr, yd)):
        term = gate_ref[0, :, i * D_MODEL:(i + 1) * D_MODEL].astype(F32) * _dot(yi, wb_ref[i])
        acc = term if acc is None else acc + term
    o_ref[0] = x_ref[0] + m_ref[0, 0] * _dot(acc, wo_ref[...])


def merge_residual(ssm, gla, rwkv, gdn, gates, x_all, gate_mod, lp, w_branch, w_out, *, n_ctx):
    b, t, d = x_all.shape
    tm = _pick_tile(n_ctx, ROW_TILE)
    tok = lambda bi, i: (bi, i, 0)
    blk = lambda c: pl.BlockSpec((1, tm, BRANCH), lambda bi, i: (bi, i, c))
    half = blk(0)
    consts = (_row(jnp.repeat(lp["ssm_d"], SSM_P)), _row(lp["ssm_norm"]),
              _row(jnp.tile(lp["gla_norm"], GLA_HEADS)), _row(lp["rwkv_ln_w"]), _row(lp["rwkv_ln_b"]),
              _row(jnp.tile(lp["gdn_norm"], GDN_HEADS)),
              _block_diag_ones(RWKV_N), _block_diag_ones(LANES), _block_diag_ones(BRANCH // SSM_GROUPS),
              w_branch, w_out)
    ins = (ssm[0], ssm[1], ssm[2], ssm[3], gla[0], gla[1], gla[2], rwkv[0], rwkv[1], rwkv[2], rwkv[3],
           gdn[0], gdn[1], gdn[2], gates, x_all, gate_mod) + consts
    specs = ([half, half, half, blk(0), half, half, blk(2), half, half, half, half, half, half, blk(3),
              pl.BlockSpec((1, tm, 4 * d), tok), pl.BlockSpec((1, tm, d), tok),
              pl.BlockSpec((1, 1, 1, d), _mod_sel(n_ctx // tm))]
             + [_const_spec(c.shape) for c in consts])
    return pl.pallas_call(
        _merge_kernel,
        grid=(b, t // tm),
        in_specs=specs,
        out_specs=pl.BlockSpec((1, tm, d), tok),
        out_shape=jax.ShapeDtypeStruct((b, t, d), F32),
        compiler_params=_cparams("parallel", "parallel"),
    )(*ins)


def _route_kernel(x_ref, nw_ref, shift_ref, scale_ref, rw_ref, rb_ref, u_ref, o_ref, cnt_ref, hb_ref, *, rows_kw):
    x = x_ref[...]
    shift, scale = _token_rows([shift_ref, scale_ref], pl.program_id(0), x.shape[0], **rows_kw)
    h = x * lax.rsqrt(jnp.mean(x * x, axis=-1, keepdims=True) + EPS) * nw_ref[...] * (1.0 + scale) + shift
    hb_ref[...] = h.astype(BF16)
    logits = lax.dot_general(rw_ref[...], h, (((1,), (1,)), ((), ())),
                             precision=HI, preferred_element_type=F32)
    scores = _sigmoid(logits)
    sel = scores + rb_ref[...]
    rows = [sel[e:e + 1, :] for e in range(N_EXPERTS)]
    sc = [scores[e:e + 1, :] for e in range(N_EXPERTS)]

    def top2(vals):
        v1, i1 = vals[0], jnp.zeros(vals[0].shape, jnp.int32)
        for j in range(1, len(vals)):
            better = vals[j] > v1
            v1 = jnp.where(better, vals[j], v1)
            i1 = jnp.where(better, j, i1)
        v2 = jnp.where(i1 == 0, vals[1], vals[0])
        i2 = jnp.where(i1 == 0, 1, 0)
        for j in range(1, len(vals)):
            better = (vals[j] > v2) & (i1 != j)
            v2 = jnp.where(better, vals[j], v2)
            i2 = jnp.where(better, j, i2)
        return v1, i1, v2, i2

    gsum = []
    for grp in range(N_GROUPS):
        v1, _, v2, _ = top2(rows[grp * EXPERTS_PER_GROUP:(grp + 1) * EXPERTS_PER_GROUP])
        gsum.append(v1 + v2)
    best, gidx = gsum[0], jnp.zeros(gsum[0].shape, jnp.int32)
    for grp in range(1, N_GROUPS):
        better = gsum[grp] > best
        best = jnp.where(better, gsum[grp], best)
        gidx = jnp.where(better, grp, gidx)
    chosen, chosen_sc = [], []
    for j in range(EXPERTS_PER_GROUP):
        cj, sj = rows[j], sc[j]
        for grp in range(1, N_GROUPS):
            cj = jnp.where(gidx == grp, rows[grp * EXPERTS_PER_GROUP + j], cj)
            sj = jnp.where(gidx == grp, sc[grp * EXPERTS_PER_GROUP + j], sj)
        chosen.append(cj)
        chosen_sc.append(sj)
    _, i1, _, i2 = top2(chosen)
    w1, w2 = jnp.zeros_like(best), jnp.zeros_like(best)
    for j in range(EXPERTS_PER_GROUP):
        w1 = jnp.where(i1 == j, chosen_sc[j], w1)
        w2 = jnp.where(i2 == j, chosen_sc[j], w2)
    tot = w1 + w2
    w1, w2 = w1 / tot, w2 / tot
    tm = scores.shape[1]
    sub = lax.broadcasted_iota(jnp.int32, (SUBLANES, tm), 0)
    ind8 = jnp.zeros((SUBLANES, tm), F32)
    meta = jnp.zeros((SUBLANES, tm), F32)
    for j in range(EXPERTS_PER_GROUP):
        gate_j = jnp.where(i1 == j, w1, 0.0) + jnp.where(i2 == j, w2, 0.0)
        meta = jnp.where(sub == j, gate_j, meta)
    for grp in range(N_GROUPS):
        ind8 = jnp.where((sub == grp) & (gidx == grp), 1.0, ind8)
    before = jnp.dot(ind8.astype(BF16), u_ref[...], preferred_element_type=F32)
    rank = jnp.sum(ind8 * before, axis=0, keepdims=True)
    meta = jnp.where(sub == _META_GROUP, gidx.astype(F32), meta)
    meta = jnp.where(sub == _META_RANK, rank, meta)
    o_ref[...] = meta
    counts = jnp.sum(ind8, axis=1, keepdims=True)
    lane = lax.broadcasted_iota(jnp.int32, (SUBLANES, LANES), 1)
    row = lax.broadcasted_iota(jnp.int32, (SUBLANES, LANES), 0)
    cnt_ref[0] = jnp.broadcast_to(jnp.sum(jnp.where(lane == row, counts, 0.0), axis=0, keepdims=True),
                                  (SUBLANES, LANES)).astype(jnp.int32)


_META_GROUP, _META_RANK = EXPERTS_PER_GROUP, EXPERTS_PER_GROUP + 1


def moe_route(x_all, norm_w, shift_rows, scale_rows, router_w, router_b, *, n_ctx, tm):
    b, t, d = x_all.shape
    m = b * t
    upper = jnp.asarray(np.triu(np.ones((tm, tm), np.float32), 1), BF16)
    return pl.pallas_call(
        functools.partial(_route_kernel, rows_kw=dict(bsz=b, t_all=t, n_ctx=n_ctx)),
        grid=(m // tm,),
        in_specs=[pl.BlockSpec((tm, d), lambda i: (i, 0)),
                  _const_spec((1, d)), _const_spec(shift_rows.shape), _const_spec(scale_rows.shape),
                  pl.BlockSpec((N_EXPERTS, d), lambda i: (0, 0)),
                  pl.BlockSpec((N_EXPERTS, 1), lambda i: (0, 0)),
                  _const_spec((tm, tm))],
        out_specs=[pl.BlockSpec((SUBLANES, tm), lambda i: (0, i)),
                   pl.BlockSpec((1, SUBLANES, LANES), lambda i: (i, 0, 0)),
                   pl.BlockSpec((tm, d), lambda i: (i, 0))],
        out_shape=[jax.ShapeDtypeStruct((SUBLANES, m), F32),
                   jax.ShapeDtypeStruct((m // tm, SUBLANES, LANES), jnp.int32),
                   jax.ShapeDtypeStruct((m, d), BF16)],
        compiler_params=_cparams("parallel"),
    )(x_all.reshape(m, d), norm_w.reshape(1, d), shift_rows, scale_rows,
      router_w.T, router_b.reshape(N_EXPERTS, 1), upper)


MOE_TILE = 1024
MOE_SUB_ROWS = 256
MOE_TAIL_ROWS = 128


def _expert_kernel(cnt_ref, h_ref, mr_ref, mc_ref, wg_ref, wu_ref, wd_ref, x_ref, gate_ref, o_ref, *, rows_kw):
    i, grp = pl.program_id(0), pl.program_id(1)

    @pl.when(grp == 0)
    def _():
        o_ref[...] = jnp.zeros_like(o_ref)

    tm = h_ref.shape[0]
    count = cnt_ref[i * N_GROUPS + grp]
    grp_f = grp.astype(F32)
    sel_row = jnp.where(mr_ref[_META_GROUP:_META_GROUP + 1, :] == grp_f, mr_ref[_META_RANK:_META_RANK + 1, :], -1.0)
    sel_col = jnp.where(mc_ref[:, _META_GROUP:_META_GROUP + 1] == grp_f, mc_ref[:, _META_RANK:_META_RANK + 1], -1.0)
    gate_parts = _split3(mc_ref[...])

    def sub_block(first, rows):
        base = first.astype(F32)
        slot_r = lax.broadcasted_iota(jnp.int32, (rows, tm), 0).astype(F32)
        slot_c = lax.broadcasted_iota(jnp.int32, (tm, rows), 1).astype(F32)
        pick = (sel_row - base == slot_r).astype(BF16)
        put = (sel_col - base == slot_c).astype(BF16)
        xg = jnp.dot(pick, h_ref[...], preferred_element_type=F32).astype(BF16)
        gates = sum(jnp.dot(pick, p, preferred_element_type=F32) for p in gate_parts)
        y = jnp.zeros((rows, o_ref.shape[1]), F32)
        for e in range(EXPERTS_PER_GROUP):
            hid = _silu(_dot(xg, wg_ref[e])) * _dot(xg, wu_ref[e])
            y = y + _dot(gates[:, e:e + 1] * hid, wd_ref[e])
        yh = y.astype(BF16)
        yl = (y - yh.astype(F32)).astype(BF16)
        o_ref[...] += (jnp.dot(put, yh, preferred_element_type=F32)
                       + jnp.dot(put, yl, preferred_element_type=F32))

    n_full = count // MOE_SUB_ROWS
    rem = count - n_full * MOE_SUB_ROWS
    n_main = n_full + (rem > MOE_TAIL_ROWS).astype(jnp.int32)

    def main_block(s, carry):
        sub_block(s * MOE_SUB_ROWS, MOE_SUB_ROWS)
        return carry

    lax.fori_loop(0, n_main, main_block, 0)

    @pl.when((rem > 0) & (rem <= MOE_TAIL_ROWS))
    def _():
        sub_block(n_full * MOE_SUB_ROWS, MOE_TAIL_ROWS)

    @pl.when(grp == N_GROUPS - 1)
    def _():
        (gate,) = _token_rows([gate_ref], i, tm, **rows_kw)
        o_ref[...] = x_ref[...] + gate * o_ref[...]


def moe_experts(hb, meta, counts, wg, wu, wd, x_all, gate_rows, *, n_ctx, tm):
    b, t, d = x_all.shape
    m = b * t
    tok = lambda i, g, cnt: (i, 0)
    grid_spec = pltpu.PrefetchScalarGridSpec(
        num_scalar_prefetch=1,
        grid=(m // tm, N_GROUPS),
        in_specs=[pl.BlockSpec((tm, d), tok),
                  pl.BlockSpec((SUBLANES, tm), lambda i, g, cnt: (0, i)),
                  pl.BlockSpec((tm, SUBLANES), tok),
                  pl.BlockSpec((EXPERTS_PER_GROUP, d, EXPERT_FF), lambda i, g, cnt: (g, 0, 0)),
                  pl.BlockSpec((EXPERTS_PER_GROUP, d, EXPERT_FF), lambda i, g, cnt: (g, 0, 0)),
                  pl.BlockSpec((EXPERTS_PER_GROUP, EXPERT_FF, d), lambda i, g, cnt: (g, 0, 0)),
                  pl.BlockSpec((tm, d), tok),
                  pl.BlockSpec(gate_rows.shape, lambda i, g, cnt: (0, 0))],
        out_specs=pl.BlockSpec((tm, d), tok))
    out = pl.pallas_call(
        functools.partial(_expert_kernel, rows_kw=dict(bsz=b, t_all=t, n_ctx=n_ctx)),
        grid_spec=grid_spec,
        out_shape=jax.ShapeDtypeStruct((m, d), F32),
        compiler_params=pltpu.CompilerParams(dimension_semantics=("parallel", "arbitrary"),
                                             vmem_limit_bytes=MOE_VMEM_LIMIT),
    )(counts[:, 0, :N_GROUPS].reshape(-1), hb, meta, meta.T, wg, wu, wd, x_all.reshape(m, d), gate_rows)
    return out.reshape(b, t, d)


def _token_rows(m_refs, tile, tm, *, bsz, t_all, n_ctx):
    row = tile * tm + lax.broadcasted_iota(jnp.int32, (tm, 1), 0)
    ctx = jnp.zeros((tm, 1), jnp.bool_)
    lat = []
    for bi in range(bsz):
        lo = bi * t_all
        ctx = ctx | ((row >= lo) & (row < lo + n_ctx))
        lat.append((row >= lo + n_ctx) & (row < lo + t_all))
    out = []
    for m_ref in m_refs:
        v = jnp.where(ctx, m_ref[bsz:bsz + 1, :], 0.0)
        for bi in range(bsz):
            v = v + jnp.where(lat[bi], m_ref[bi:bi + 1, :], 0.0)
        out.append(v)
    return out


def _final_norm_kernel(x_ref, w_ref, o_ref):
    x = x_ref[...]
    o_ref[...] = x * lax.rsqrt(jnp.mean(x * x, axis=-1, keepdims=True) + EPS) * w_ref[...]


def final_rms_norm(x, w, *, tm):
    m, d = x.shape
    return pl.pallas_call(
        _final_norm_kernel,
        grid=(m // tm,),
        in_specs=[pl.BlockSpec((tm, d), lambda i: (i, 0)), pl.BlockSpec((1, d), lambda i: (0, 0))],
        out_specs=pl.BlockSpec((tm, d), lambda i: (i, 0)),
        out_shape=jax.ShapeDtypeStruct((m, d), F32),
        compiler_params=_cparams("parallel"),
    )(x, w.reshape(1, d))


def _pack_w_in(w_in, mixer):
    cols = _SRC_COLS[mixer]
    pieces, i = [], 0
    while i < len(cols):
        j = i
        if cols[i] < 0:
            while j < len(cols) and cols[j] < 0:
                j += 1
            pieces.append(jnp.zeros((w_in.shape[0], j - i), w_in.dtype))
        else:
            while j < len(cols) and cols[j] == cols[i] + (j - i):
                j += 1
            pieces.append(w_in[:, int(cols[i]):int(cols[i]) + (j - i)])
        i = j
    return jnp.concatenate(pieces, axis=1).astype(BF16)


def mixer_scans(ps, lp, *, n_ctx):
    p_ssm, p_gla, p_rwkv, p_gdn = ps

    xs, bc, sm = ssm_prep(p_ssm, lp, n_ctx=n_ctx)
    neg_a = jnp.pad(-jnp.exp(lp["ssm_a_log"]).reshape(1, -1), ((0, 0), (0, LANES - 2 * SSM_HEADS)))
    ssm = tuple(ssd_scan(xs, bc, sm, neg_a, n_ctx=n_ctx)) + (xs, p_ssm)

    w2 = [jnp.zeros((LANES, GLA_HEADS * GLA_DK), F32).at[d * GLA_RANK:(d + 1) * GLA_RANK].set(lp["gla_w2"][d])
          for d in range(2)]
    gb = [_row(lp["gla_b"][d]) for d in range(2)]
    gla = tuple(gla_scan(p_gla, w2, gb, n_ctx=n_ctx)) + (p_gla,)

    r, k, v, kk, a, lw, g, bonus = rwkv_prep(p_rwkv, lp, n_ctx=n_ctx)
    rwkv = tuple(rwkv_scan(r, k, v, kk, a, lw, _row(lp["rwkv_k_a"]), n_ctx=n_ctx)) + (g, bonus)

    q, kd, vd, smd = gdn_prep(p_gdn, lp, n_ctx=n_ctx)
    gdn = tuple(gdn_scan(q, kd, vd, smd, n_ctx=n_ctx)) + (p_gdn,)
    return ssm, gla, rwkv, gdn


def _to_scan_order(t):
    b, n, d = t.shape
    return t.reshape(b, n // GRID_W, GRID_W, d).transpose(0, 2, 1, 3).reshape(b, n, d)


def _from_scan_order(t):
    b, n, d = t.shape
    return t.reshape(b, GRID_W, n // GRID_W, d).transpose(0, 2, 1, 3).reshape(b, n, d)


def kernel(x, c, ctx, c_ctx, ada_w, ada_b, norm_mix, norm_ffn, w_in, w_gate, w_branch, w_out, ssm_conv_w, ssm_conv_b, ssm_a_log, ssm_dt_bias, ssm_d, ssm_norm, gla_w2, gla_b, gla_norm, rwkv_mu, rwkv_w0, rwkv_w2, rwkv_a0, rwkv_a2, rwkv_g2, rwkv_k_k, rwkv_k_a, rwkv_r_k, rwkv_ln_w, rwkv_ln_b, gdn_conv_w, gdn_a_log, gdn_dt_bias, gdn_norm, router_w, router_b, moe_w_gate, moe_w_up, moe_w_down, final_norm):
    bsz, seq, d = x.shape
    n_ctx = ctx.shape[1]
    t_all = n_ctx + seq
    m_all = bsz * t_all

    cond = jnp.concatenate([jax.nn.silu(c), jax.nn.silu(c_ctx)[None]], 0)
    cond = jnp.pad(cond, ((0, SUBLANES - cond.shape[0]), (0, 0)))
    mods, mod_rows = [], []
    for l in range(DEPTH):
        mod = pmatmul(cond, ada_w[l], tm=SUBLANES, tn=1024, precise=True) + ada_b[l]
        mod_rows.append(mod)
        lat = mod[:bsz].reshape(bsz, 6, d)
        cx = jnp.broadcast_to(mod[bsz].reshape(1, 6, d), (bsz, 6, d))
        mods.append(jnp.stack([cx, lat], axis=1))

    x_all = jnp.concatenate([ctx, x], axis=1)
    scan_order = False
    for l in range(DEPTH):
        if (l % 2 == 1) != scan_order:
            reorder = _from_scan_order if scan_order else _to_scan_order
            x_all = jnp.concatenate([x_all[:, :n_ctx], reorder(x_all[:, n_ctx:])], axis=1)
            scan_order = not scan_order
        lp = dict(ssm_conv_w=ssm_conv_w[l], ssm_conv_b=ssm_conv_b[l], ssm_a_log=ssm_a_log[l],
                  ssm_dt_bias=ssm_dt_bias[l], ssm_d=ssm_d[l], ssm_norm=ssm_norm[l],
                  gla_w2=gla_w2[l], gla_b=gla_b[l], gla_norm=gla_norm[l],
                  rwkv_mu=rwkv_mu[l], rwkv_w0=rwkv_w0[l], rwkv_w2=rwkv_w2[l], rwkv_a0=rwkv_a0[l],
                  rwkv_a2=rwkv_a2[l], rwkv_g2=rwkv_g2[l], rwkv_k_k=rwkv_k_k[l], rwkv_k_a=rwkv_k_a[l],
                  rwkv_r_k=rwkv_r_k[l], rwkv_ln_w=rwkv_ln_w[l], rwkv_ln_b=rwkv_ln_b[l],
                  gdn_conv_w=gdn_conv_w[l], gdn_a_log=gdn_a_log[l], gdn_dt_bias=gdn_dt_bias[l],
                  gdn_norm=gdn_norm[l])
        mod = mods[l]
        msel = lambda i: mod[:, :, i][:, :, None, :]

        h = norm_modulate(x_all, norm_mix[l], msel(0), msel(1), n_ctx=n_ctx)
        h2d = h.reshape(m_all, d)
        ps = []
        for mixer in ("ssm", "gla", "rwkv", "gdn"):
            wp = _pack_w_in(w_in[l], mixer)
            ps.append(pmatmul(h2d, wp, tm=512, tn=wp.shape[1]).reshape(bsz, t_all, wp.shape[1]))
        wg_cat = jnp.concatenate([w_gate[l, i] for i in range(4)], axis=1).astype(BF16)
        gates = pmatmul(h2d, wg_cat, tm=1024, tn=1024, act="sigmoid", out_dtype=BF16)
        gates = gates.reshape(bsz, t_all, 4 * d)

        ssm, gla, rwkv, gdn = mixer_scans(ps, lp, n_ctx=n_ctx)
        x_all = merge_residual(ssm, gla, rwkv, gdn, gates, x_all, msel(2), lp,
                               w_branch[l].astype(BF16), w_out[l].astype(BF16), n_ctx=n_ctx)

        tm_moe = _pick_tile(m_all, MOE_TILE)
        rows = lambda i: mod_rows[l][:, i * d:(i + 1) * d]
        meta, counts, hb = moe_route(x_all, norm_ffn[l], rows(3), rows(4), router_w, router_b,
                                     n_ctx=n_ctx, tm=tm_moe)
        x_all = moe_experts(hb, meta, counts, moe_w_gate[l].astype(BF16), moe_w_up[l].astype(BF16),
                            moe_w_down[l].astype(BF16), x_all, rows(5), n_ctx=n_ctx, tm=tm_moe)

    lat = x_all[:, n_ctx:]
    if scan_order:
        lat = _from_scan_order(lat)
    return final_rms_norm(lat.reshape(bsz * seq, d), final_norm, tm=1024).reshape(bsz, seq, d)
```

```python
import functools
import itertools

import numpy as np
import jax
import jax.numpy as jnp
from jax import lax
from jax.experimental import pallas as pl
from jax.experimental.pallas import tpu as pltpu

F32 = jnp.float32
BF16 = jnp.bfloat16
HI = lax.Precision.HIGHEST

D_MODEL = 1024
DEPTH = 2
GRID_W = 64
CHUNK = 64
EPS = 1e-6
BRANCH = D_MODEL // 2
SSM_HEADS, SSM_P, SSM_GROUPS, SSM_N = 8, 64, 2, 64
GLA_HEADS, GLA_DK, GLA_DV, GLA_RANK, GLA_TAU = 4, 64, 128, 16, 16.0
RWKV_HEADS, RWKV_N, RWKV_LN_EPS = 8, 64, 64e-5
GDN_HEADS, GDN_N = 4, 128
N_EXPERTS, N_GROUPS, EXPERTS_PER_GROUP = 16, 4, 4
EXPERT_FF = D_MODEL // 2
LANES = 128
SUBLANES = 8
VMEM_LIMIT = 48 * 1024 * 1024
MOE_VMEM_LIMIT = 56 * 1024 * 1024
ROW_TILE = 256

_REF_BLOCKS = (
    ("ssm", "z", 512), ("ssm", "xbc", 768), ("ssm", "dt", 16),
    ("gla", "q", 256), ("gla", "k", 256), ("gla", "v", 512), ("gla", "r", 512), ("gla", "glr", 32),
    ("rwkv", "all", 1920),
    ("gdn", "qkv", 1536), ("gdn", "z", 512), ("gdn", "ab", 16),
)
_PACKED = {
    "ssm": (("z", 512), ("dt", 128), ("pad", 128), ("xbc", 768)),
    "gla": (("q", 256), ("k", 256), ("v", 512), ("r", 512), ("glr", 128)),
    "rwkv": (("all", 1920),),
    "gdn": (("qkv", 1536), ("z", 512), ("ab", 128)),
}


def _packed_columns():
    start, s = {}, 0
    for mixer, blk, w in _REF_BLOCKS:
        start[(mixer, blk)] = (s, w)
        s += w
    out = {}
    for mixer, blocks in _PACKED.items():
        cols = []
        for blk, wp in blocks:
            s0, w = start.get((mixer, blk), (0, 0))
            cols += list(range(s0, s0 + w)) + [-1] * (wp - w)
        out[mixer] = np.asarray(cols, np.int32)
    return out


_SRC_COLS = _packed_columns()


def _cparams(*sem):
    return pltpu.CompilerParams(dimension_semantics=sem, vmem_limit_bytes=VMEM_LIMIT)


def _dot(a, b):
    return jnp.dot(a.astype(BF16), b.astype(BF16), preferred_element_type=F32)


def _dot_nt(a, b):
    return lax.dot_general(a.astype(BF16), b.astype(BF16), (((1,), (1,)), ((), ())),
                           preferred_element_type=F32)


def _dot_tn(a, b):
    return lax.dot_general(a.astype(BF16), b.astype(BF16), (((0,), (0,)), ((), ())),
                           preferred_element_type=F32)


def _dot_hi(a, b):
    return jnp.dot(a, b, precision=HI, preferred_element_type=F32)


def _dot_x3(a, b):
    ah = a.astype(BF16)
    al = (a - ah.astype(F32)).astype(BF16)
    bh = b.astype(BF16)
    bl = (b - bh.astype(F32)).astype(BF16)
    f = lambda u, v: jnp.dot(u, v, preferred_element_type=F32)
    return f(ah, bh) + (f(ah, bl) + f(al, bh))


def _dot_x2(a, w):
    ah = a.astype(BF16)
    al = (a - ah.astype(F32)).astype(BF16)
    return jnp.dot(ah, w, preferred_element_type=F32) + jnp.dot(al, w, preferred_element_type=F32)


def _softplus(x):
    return jnp.maximum(x, 0.0) + jnp.log(1.0 + jnp.exp(-jnp.abs(x)))


def _sigmoid(x):
    return 1.0 / (1.0 + jnp.exp(-x))


def _silu(x):
    return x * _sigmoid(x)


def _pick_tile(m, pref):
    t = pref
    while m % t:
        t //= 2
    return t


def _block_diag_ones(n, width=BRANCH):
    idx = np.arange(width) // n
    return jnp.asarray(idx[:, None] == idx[None, :], BF16)


def _mm_kernel(a_ref, w_ref, o_ref, *, act, precise):
    if precise:
        r = _dot_hi(a_ref[...].astype(F32), w_ref[...].astype(F32))
    else:
        r = _dot(a_ref[...], w_ref[...])
    if act == "sigmoid":
        r = _sigmoid(r)
    o_ref[...] = r.astype(o_ref.dtype)


def pmatmul(a, w, *, tm, tn, act=None, precise=False, out_dtype=F32):
    m, k = a.shape
    n = w.shape[1]
    tm = _pick_tile(m, tm)
    assert tm % SUBLANES == 0 and n % tn == 0, (m, tm, n, tn)
    return pl.pallas_call(
        functools.partial(_mm_kernel, act=act, precise=precise),
        grid=(n // tn, m // tm),
        in_specs=[pl.BlockSpec((tm, k), lambda j, i: (i, 0)),
                  pl.BlockSpec((k, tn), lambda j, i: (0, j))],
        out_specs=pl.BlockSpec((tm, tn), lambda j, i: (i, j)),
        out_shape=jax.ShapeDtypeStruct((m, n), out_dtype),
        compiler_params=_cparams("parallel", "parallel"),
    )(a, w)


def _norm_mod_kernel(x_ref, w_ref, shift_ref, scale_ref, o_ref):
    x = x_ref[0]
    y = x * lax.rsqrt(jnp.mean(x * x, axis=-1, keepdims=True) + EPS) * w_ref[...]
    o_ref[0] = (y * (1.0 + scale_ref[0, 0]) + shift_ref[0, 0]).astype(o_ref.dtype)


def _mod_sel(n_lat_tiles):
    return lambda bi, i, *_: (bi, jnp.where(i < n_lat_tiles, 1, 0), 0, 0)


def norm_modulate(x_all, w, shift, scale, *, n_ctx, out_dtype=BF16):
    b, t, d = x_all.shape
    tm = _pick_tile(n_ctx, ROW_TILE)
    assert t % tm == 0
    tok = lambda bi, i: (bi, i, 0)
    return pl.pallas_call(
        _norm_mod_kernel,
        grid=(b, t // tm),
        in_specs=[pl.BlockSpec((1, tm, d), tok),
                  pl.BlockSpec((1, d), lambda bi, i: (0, 0)),
                  pl.BlockSpec((1, 1, 1, d), _mod_sel((t - n_ctx) // tm)),
                  pl.BlockSpec((1, 1, 1, d), _mod_sel((t - n_ctx) // tm))],
        out_specs=pl.BlockSpec((1, tm, d), tok),
        out_shape=jax.ShapeDtypeStruct((b, t, d), out_dtype),
        compiler_params=_cparams("parallel", "parallel"),
    )(x_all, w.reshape(1, d), shift, scale)


def _row(v):
    return v.reshape(1, -1).astype(F32)


def _const_spec(shape):
    return pl.BlockSpec(shape, lambda *_: (0,) * len(shape))


def _tile_specs(tt, width, col):
    r8 = tt // SUBLANES
    main = pl.BlockSpec((1, tt, width), lambda bi, i: (bi, i, col))
    prev = pl.BlockSpec((1, SUBLANES, width), lambda bi, i: (bi, jnp.maximum(i * r8 - 1, 0), col))
    return main, prev, r8


def _halo_specs(tt, width, col, t):
    main, prev, r8 = _tile_specs(tt, width, col)
    last8 = t // SUBLANES - 1
    nxt = pl.BlockSpec((1, SUBLANES, width), lambda bi, i: (bi, jnp.minimum((i + 1) * r8, last8), col))
    return [main, prev, nxt]


def _neighbours(x, prev8, next8, *, nct, nt):
    i = pl.program_id(1)
    tt = x.shape[0]
    row = lax.broadcasted_iota(jnp.int32, x.shape, 0)
    first = (i == 0) | (i == nct)
    last = (i == nct - 1) | (i == nt - 1)
    pr = jnp.where(first, 0.0, prev8[SUBLANES - 1:SUBLANES, :])
    nx = jnp.where(last, 0.0, next8[0:1, :])
    xp = jnp.where(row == 0, pr, pltpu.roll(x, 1, 0))
    xn = jnp.where(row == tt - 1, nx, pltpu.roll(x, tt - 1, 0))
    return xp, xn


def _prep_call(kernel, ins, in_specs, out_widths, *, b, t, tt, out_dtype=F32):
    tok = lambda bi, i: (bi, i, 0)
    return pl.pallas_call(
        kernel,
        grid=(b, t // tt),
        in_specs=in_specs,
        out_specs=[pl.BlockSpec((1, tt, w), tok) for w in out_widths],
        out_shape=[jax.ShapeDtypeStruct((b, t, w), out_dtype) for w in out_widths],
        compiler_params=_cparams("parallel", "parallel"),
    )(*ins)


def _chunk_masks(reverse):
    row = lax.broadcasted_iota(jnp.int32, (CHUNK, CHUNK), 0)
    col = lax.broadcasted_iota(jnp.int32, (CHUNK, CHUNK), 1)
    if reverse:
        return col >= row, col > row
    return col <= row, col < row


def _chunk_order(i, n_ctx_chunks, n_chunks, reverse):
    n_lat_chunks = n_chunks - n_ctx_chunks
    if not reverse:
        return jnp.where(i < n_ctx_chunks, n_lat_chunks + i, i - n_ctx_chunks)
    return jnp.where(i < n_ctx_chunks, n_chunks - 1 - i, n_lat_chunks - 1 - (i - n_ctx_chunks))


def _split3(a):
    hi = a.astype(BF16)
    r = a - hi.astype(F32)
    mid = r.astype(BF16)
    return hi, mid, (r - mid.astype(F32)).astype(BF16)


def _transpose_small(x):
    row = lax.broadcasted_iota(jnp.int32, (LANES, LANES), 0)
    col = lax.broadcasted_iota(jnp.int32, (LANES, LANES), 1)
    eye = (row == col).astype(BF16)
    nt = lambda p: lax.dot_general(eye, p, (((1,), (1,)), ((), ())), preferred_element_type=F32)
    hi, mid, lo = _split3(x)
    return nt(hi) + (nt(mid) + nt(lo))


def _chunk_cumsum(incl, x):
    m = incl.astype(BF16)
    hi, mid, lo = _split3(x)
    f = lambda p: jnp.dot(m, p, preferred_element_type=F32)
    return f(hi) + (f(mid) + f(lo))


def _select_columns(x, sel):
    c = x.shape[0]
    y = jnp.dot(jnp.concatenate(_split3(x), axis=0), sel, preferred_element_type=F32)
    return y[:c] + (y[c:2 * c] + y[2 * c:])


def _unit_tri_solve(mats, rhs, precise_levels=0):
    n = range(len(mats))
    x = [rhs[h] - _dot_x3(mats[h], rhs[h]) for h in n]
    yield
    p = mats
    for level in range(int(np.log2(CHUNK)) - 1):
        dot = _dot_x3 if level < precise_levels else _dot
        p = [dot(p[h], p[h]) for h in n]
        yield
        x = [x[h] + dot(p[h], x[h]) for h in n]
        yield
    return x


def _bidir_scan(body, tok_ins, const_ins, state_shape, *, b, t, n_ctx, lockstep=True, batch_block=1):
    nc, ncc = t // CHUNK, n_ctx // CHUNK
    nb = batch_block
    assert b % nb == 0

    def chunk_spec(width, col, reverse):
        return pl.BlockSpec((nb, CHUNK, width), lambda bi, i: (bi, _chunk_order(i, ncc, nc, reverse), col))

    def direction(reverse):
        d = int(reverse)
        specs = [chunk_spec(w, cols[d], reverse) for _, w, *cols in tok_ins]
        specs += [_const_spec(pair[d].shape) for pair in const_ins]
        return specs, [a for a, *_ in tok_ins] + [pair[d] for pair in const_ins]

    (spec_f, arg_f), (spec_b, arg_b) = direction(False), direction(True)
    n_tok, n_in = len(tok_ins), len(arg_f)

    def kern(*refs):
        o_f, o_b, s_f, s_b = refs[2 * n_in:]

        @pl.when(pl.program_id(1) == 0)
        def _():
            s_f[...] = jnp.zeros_like(s_f)
            s_b[...] = jnp.zeros_like(s_b)

        def one(j, ins, o_ref, s_ref, reverse):
            ins = [r.at[pl.ds(j, 1)] if k < n_tok else r for k, r in enumerate(ins)]
            return body(*ins, o_ref.at[pl.ds(j, 1)], s_ref.at[j], reverse=reverse)

        gens = []
        for j in range(nb):
            gens += [one(j, refs[:n_in], o_f, s_f, False), one(j, refs[n_in:2 * n_in], o_b, s_b, True)]
        if not lockstep:
            gens = [itertools.chain(*gens)]
        while gens:
            gens = [g for g in gens if next(g, _DONE) is not _DONE]

    return pl.pallas_call(
        kern,
        grid=(b // nb, nc),
        in_specs=spec_f + spec_b,
        out_specs=[chunk_spec(BRANCH, 0, False), chunk_spec(BRANCH, 0, True)],
        out_shape=[jax.ShapeDtypeStruct((b, t, BRANCH), F32)] * 2,
        scratch_shapes=[pltpu.VMEM((nb,) + tuple(state_shape), F32)] * 2,
        compiler_params=_cparams("parallel", "arbitrary"),
    )(*arg_f, *arg_b)


_DONE = object()


def _batch_block(b, pref):
    return pref if b % pref == 0 else 1


def _ssm_prep_kernel(x_ref, xp_ref, xn_ref, dt_ref, cw_ref, cb_ref, dtb_ref, xs_ref, bc_ref, sm_ref,
                     *, nct, nt):
    x = x_ref[0]
    xp, xn = _neighbours(x, xp_ref[0], xn_ref[0], nct=nct, nt=nt)
    y = _silu(xp * cw_ref[0:1, :] + x * cw_ref[1:2, :] + xn * cw_ref[2:3, :] + cb_ref[...])
    xs_ref[0] = y[:, :BRANCH]
    bc_ref[0] = y[:, BRANCH:]
    sm_ref[0] = _softplus(dt_ref[0] + dtb_ref[...])


def ssm_prep(p, lp, *, n_ctx):
    b, t, _ = p.shape
    tt = _pick_tile(n_ctx, ROW_TILE)
    dtb = jnp.pad(lp["ssm_dt_bias"].reshape(1, -1), ((0, 0), (0, LANES - 2 * SSM_HEADS)))
    specs = _halo_specs(tt, 768, 1, t) + [pl.BlockSpec((1, tt, LANES), lambda bi, i: (bi, i, 4)),
                                          _const_spec((3, 768)), _const_spec((1, 768)), _const_spec((1, LANES))]
    kern = functools.partial(_ssm_prep_kernel, nct=(t - n_ctx) // tt, nt=t // tt)
    return _prep_call(kern, (p, p, p, p, lp["ssm_conv_w"], _row(lp["ssm_conv_b"]), dtb), specs,
                      (BRANCH, 2 * SSM_GROUPS * SSM_N, LANES), b=b, t=t, tt=tt)


def _ssd_body(x_ref, bc_ref, sm_ref, na_ref, o_ref, s_ref, *, reverse):
    incl, _ = _chunk_masks(reverse)
    last = 0 if reverse else CHUNK - 1
    off = SSM_HEADS if reverse else 0
    dt_all = sm_ref[0]
    g_all = _chunk_cumsum(incl, dt_all * na_ref[...])
    yield
    expand = _expand_matrix(off, SSM_HEADS, SSM_P)
    gx = _select_columns(g_all, expand)
    dx = _select_columns(dt_all, expand)
    gt_all = _transpose_small(g_all)
    dtt_all = _transpose_small(dt_all)
    yield
    heads = range(SSM_HEADS)
    rep = SSM_HEADS // SSM_GROUPS
    gw = SSM_GROUPS * SSM_N
    pw = rep * SSM_P
    hs = [slice(h * SSM_P, (h + 1) * SSM_P) for h in heads]
    glx = gx[last:last + 1, :]
    egx = jnp.exp(gx)
    wx = dx * jnp.exp(glx - gx)
    eglx = jnp.exp(glx)
    x = x_ref[0]
    bm = [bc_ref[0, :, grp * SSM_N:(grp + 1) * SSM_N] for grp in range(SSM_GROUPS)]
    cm = [bc_ref[0, :, gw + grp * SSM_N:gw + (grp + 1) * SSM_N] for grp in range(SSM_GROUPS)]
    cb = [_dot_nt(cm[grp], bm[grp]) for grp in range(SSM_GROUPS)]
    s = [s_ref[grp] for grp in range(SSM_GROUPS)]
    yield
    scores = [cb[h // rep] * jnp.exp(jnp.where(incl, gx[:, hs[h]] - gt_all[off + h:off + h + 1, :], -jnp.inf))
              * dtt_all[off + h:off + h + 1, :] for h in heads]
    yield
    intra = [_dot(scores[h], x[:, hs[h]]) for h in heads]
    yield
    inter = [_dot(cm[grp], s[grp]) for grp in range(SSM_GROUPS)]
    yield
    upd = [_dot_tn(bm[h // rep] * wx[:, hs[h]], x[:, hs[h]]) for h in heads]
    yield
    for h in heads:
        grp, ls = h // rep, slice((h % rep) * SSM_P, (h % rep + 1) * SSM_P)
        o_ref[0, :, hs[h]] = intra[h] + egx[:, hs[h]] * inter[grp][:, ls]
        s_ref[grp, :, ls] = s[grp][:, ls] * eglx[:, hs[h]] + upd[h]


def _expand_matrix(off, n_heads, width):
    row = lax.broadcasted_iota(jnp.int32, (LANES, n_heads * width), 0)
    col = lax.broadcasted_iota(jnp.int32, (LANES, n_heads * width), 1)
    lo = row * width - off * width
    return ((col >= lo) & (col < lo + width)).astype(BF16)


def ssd_scan(xs, bc, sm, neg_a, *, n_ctx):
    b, t, _ = xs.shape
    toks = [(xs, BRANCH, 0, 0), (bc, 2 * SSM_GROUPS * SSM_N, 0, 0), (sm, LANES, 0, 0)]
    state = (SSM_GROUPS, SSM_N, (SSM_HEADS // SSM_GROUPS) * SSM_P)
    return _bidir_scan(_ssd_body, toks, [(neg_a, neg_a)], state, b=b, t=t, n_ctx=n_ctx,
                       batch_block=_batch_block(b, 2))


def _gla_body(q_ref, k_ref, v_ref, glr_ref, w2_ref, gb_ref, o_ref, s_ref, *, reverse):
    incl, _ = _chunk_masks(reverse)
    last = 0 if reverse else CHUNK - 1
    logit = _dot_x3(glr_ref[0], w2_ref[...]) + gb_ref[...]
    yield
    la = -_softplus(-logit) * (1.0 / GLA_TAU)
    g_all = _chunk_cumsum(incl, la)
    yield
    heads = range(GLA_HEADS)
    ks = [slice(h * GLA_DK, (h + 1) * GLA_DK) for h in heads]
    vs = [slice(h * GLA_DV, (h + 1) * GLA_DV) for h in heads]
    g = [g_all[:, ks[h]] for h in heads]
    gl = [g[h][last:last + 1, :] for h in heads]
    k = [k_ref[0, :, ks[h]] for h in heads]
    v = [v_ref[0, :, vs[h]] for h in heads]
    qg = [q_ref[0, :, ks[h]] * (GLA_DK ** -0.5) * jnp.exp(g[h]) for h in heads]
    st = [s_ref[h] for h in heads]
    yield
    scores = [jnp.where(incl, _dot_nt(qg[h], k[h] * jnp.exp(-g[h])), 0.0) for h in heads]
    yield
    intra = [_dot(scores[h], v[h]) for h in heads]
    yield
    inter = [_dot_nt(qg[h], st[h]) for h in heads]
    yield
    upd = [_dot_tn(v[h], k[h] * jnp.exp(gl[h] - g[h])) for h in heads]
    yield
    for h in heads:
        o_ref[0, :, vs[h]] = intra[h] + inter[h]
        s_ref[h] = st[h] * jnp.exp(gl[h]) + upd[h]


def gla_scan(p, w2_pair, gb_pair, *, n_ctx):
    b, t, _ = p.shape
    kwid = GLA_HEADS * GLA_DK
    toks = [(p, kwid, 0, 0), (p, kwid, 1, 1), (p, BRANCH, 1, 1), (p, LANES, 12, 12)]
    return _bidir_scan(_gla_body, toks, [w2_pair, gb_pair], (GLA_HEADS, GLA_DV, GLA_DK), b=b, t=t, n_ctx=n_ctx,
                       batch_block=_batch_block(b, 4))


def _rwkv_prep_kernel(x_ref, xp_ref, xn_ref, mu_ref, w2_ref, w0_ref, a2_ref, a0_ref, g2_ref, kk_ref_w,
                      ka_ref, rk_ref, bd_ref, r_ref, k_ref, v_ref, kk_ref, a_ref, lw_ref, g_ref, bo_ref,
                      *, nct, nt):
    x = x_ref[0]
    xp, xn = _neighbours(x, xp_ref[0], xn_ref[0], nct=nct, nt=nt)
    x = x + mu_ref[...] * (0.5 * (xp + xn) - x)
    r, k, v = x[:, :BRANCH], x[:, BRANCH:2 * BRANCH], x[:, 2 * BRANCH:3 * BRANCH]
    wlr = x[:, 3 * BRANCH:3 * BRANCH + LANES]
    alr = x[:, 3 * BRANCH + LANES:3 * BRANCH + 2 * LANES]
    glr = x[:, 3 * BRANCH + 2 * LANES:]
    w_raw = _dot_x3(jnp.tanh(wlr), w2_ref[...]) + w0_ref[...]
    lw_ref[0] = -jnp.exp(-_softplus(-w_raw) - 0.5)
    a = _sigmoid(_dot_x3(alr, a2_ref[...]) + a0_ref[...])
    a_ref[0] = a
    g_ref[0] = _dot_x3(_sigmoid(glr), g2_ref[...])
    kk = k * kk_ref_w[...]
    kk_ref[0] = kk * lax.rsqrt(_dot_x2(kk * kk, bd_ref[...]) + EPS)
    ksum = k * (2.0 + (a[:, :BRANCH] + a[:, BRANCH:] - 2.0) * ka_ref[...])
    bo_ref[0] = _dot_x2(r * ksum * rk_ref[...], bd_ref[...]) * v
    r_ref[0] = r
    k_ref[0] = k
    v_ref[0] = v


def rwkv_prep(p, lp, *, n_ctx):
    b, t, w = p.shape
    tt = _pick_tile(n_ctx, ROW_TILE)

    def pair(wp):
        r, c = wp.shape[1:]
        return jnp.zeros((LANES, 2 * c), F32).at[:r, :c].set(wp[0]).at[r:2 * r, c:].set(wp[1])

    consts = (_row(lp["rwkv_mu"]), pair(lp["rwkv_w2"]), _row(lp["rwkv_w0"]), pair(lp["rwkv_a2"]),
              _row(lp["rwkv_a0"]), lp["rwkv_g2"], _row(lp["rwkv_k_k"]), _row(lp["rwkv_k_a"]),
              _row(lp["rwkv_r_k"]), _block_diag_ones(RWKV_N))
    specs = _halo_specs(tt, w, 0, t) + [_const_spec(c.shape) for c in consts]
    kern = functools.partial(_rwkv_prep_kernel, nct=(t - n_ctx) // tt, nt=t // tt)
    return _prep_call(kern, (p, p, p) + consts, specs,
                      (BRANCH, BRANCH, BRANCH, BRANCH, 2 * BRANCH, 2 * BRANCH, BRANCH, BRANCH),
                      b=b, t=t, tt=tt)


def _rwkv_body(r_ref, k_ref, v_ref, kk_ref, a_ref, lw_ref, ka_ref, o_ref, s_ref, *, reverse):
    incl, strict = _chunk_masks(reverse)
    last = 0 if reverse else CHUNK - 1
    lw_all = lw_ref[0]
    g_all = _chunk_cumsum(incl, lw_all)
    a_all = a_ref[0]
    k_all = k_ref[0] * (1.0 + (a_all - 1.0) * ka_ref[...])
    yield
    heads = range(RWKV_HEADS)
    hs = [slice(h * RWKV_N, (h + 1) * RWKV_N) for h in heads]
    g = [g_all[:, hs[h]] for h in heads]
    gl = [g[h][last:last + 1, :] for h in heads]
    eneg = [jnp.exp(-g[h]) for h in heads]
    edec = [jnp.exp(gl[h] - g[h]) for h in heads]
    kk = [kk_ref[0, :, hs[h]] for h in heads]
    bvec = [kk[h] * a_all[:, hs[h]] for h in heads]
    k = [k_all[:, hs[h]] for h in heads]
    v = [v_ref[0, :, hs[h]] for h in heads]
    kkg = [kk[h] * jnp.exp(g[h] - lw_all[:, hs[h]]) for h in heads]
    rg = [r_ref[0, :, hs[h]] * jnp.exp(g[h]) for h in heads]
    bh = [bvec[h] * eneg[h] for h in heads]
    kh = [k[h] * eneg[h] for h in heads]
    s = [s_ref[h] for h in heads]
    yield
    both = [jnp.concatenate([kkg[h], rg[h]], axis=0) for h in heads]
    mask2 = jnp.concatenate([strict, incl], axis=0)
    mb = [jnp.where(mask2, _dot_nt(both[h], bh[h]), 0.0) for h in heads]
    yield
    mk = [jnp.where(mask2, _dot_nt(both[h], kh[h]), 0.0) for h in heads]
    yield
    part = [_dot(mk[h], v[h]) + _dot_nt(both[h], s[h]) for h in heads]
    yield
    x = yield from _unit_tri_solve([mb[h][:CHUNK] for h in heads], [part[h][:CHUNK] for h in heads])
    u = [-xh for xh in x]
    for h in heads:
        o_ref[0, :, hs[h]] = part[h][CHUNK:] + _dot(mb[h][CHUNK:], u[h])
    yield
    for h in heads:
        upd = _dot_tn(jnp.concatenate([u[h], v[h]], axis=0),
                      jnp.concatenate([bvec[h] * edec[h], k[h] * edec[h]], axis=0))
        s_ref[h] = s[h] * jnp.exp(gl[h]) + upd


def rwkv_scan(r, k, v, kk, a, lw, k_a, *, n_ctx):
    b, t, _ = r.shape
    toks = [(r, BRANCH, 0, 0), (k, BRANCH, 0, 0), (v, BRANCH, 0, 0), (kk, BRANCH, 0, 0),
            (a, BRANCH, 0, 1), (lw, BRANCH, 0, 1)]
    return _bidir_scan(_rwkv_body, toks, [(k_a, k_a)], (RWKV_HEADS, RWKV_N, RWKV_N), b=b, t=t, n_ctx=n_ctx,
                       batch_block=_batch_block(b, 2))


def _gdn_prep_kernel(x_ref, xp_ref, xn_ref, ab_ref, cw_ref, na_ref, dtb_ref, bd_ref,
                     q_ref, k_ref, v_ref, sm_ref, *, nct, nt):
    x = x_ref[0]
    xp, xn = _neighbours(x, xp_ref[0], xn_ref[0], nct=nct, nt=nt)
    y = _silu(xp * cw_ref[0:1, :] + x * cw_ref[1:2, :] + xn * cw_ref[2:3, :])
    q, k = y[:, :BRANCH], y[:, BRANCH:2 * BRANCH]
    q_ref[0] = q * lax.rsqrt(_dot_x2(q * q, bd_ref[...]) + EPS) * (GDN_N ** -0.5)
    k_ref[0] = k * lax.rsqrt(_dot_x2(k * k, bd_ref[...]) + EPS)
    v_ref[0] = y[:, 2 * BRANCH:]
    ab = ab_ref[0]
    lane = lax.broadcasted_iota(jnp.int32, ab.shape, 1)
    sm_ref[0] = jnp.where(lane < 2 * GDN_HEADS, na_ref[...] * _softplus(ab + dtb_ref[...]), _sigmoid(ab))


def gdn_prep(p, lp, *, n_ctx):
    b, t, _ = p.shape
    tt = _pick_tile(n_ctx, ROW_TILE)
    padrow = lambda v: jnp.pad(v.reshape(1, -1), ((0, 0), (0, LANES - 2 * GDN_HEADS)))
    consts = (lp["gdn_conv_w"], padrow(-jnp.exp(lp["gdn_a_log"])), padrow(lp["gdn_dt_bias"]),
              _block_diag_ones(GDN_N))
    specs = (_halo_specs(tt, 3 * BRANCH, 0, t) + [pl.BlockSpec((1, tt, LANES), lambda bi, i: (bi, i, 16))]
             + [_const_spec(c.shape) for c in consts])
    kern = functools.partial(_gdn_prep_kernel, nct=(t - n_ctx) // tt, nt=t // tt)
    return _prep_call(kern, (p, p, p, p) + consts, specs, (BRANCH, BRANCH, BRANCH, LANES), b=b, t=t, tt=tt)


def _gdn_body(q_ref, k_ref, v_ref, sm_ref, o_ref, s_ref, *, reverse):
    incl, strict = _chunk_masks(reverse)
    last = 0 if reverse else CHUNK - 1
    off = GDN_HEADS if reverse else 0
    sm = sm_ref[0]
    g_all = _chunk_cumsum(incl, sm)
    yield
    gt_all = _transpose_small(g_all)
    yield
    heads = range(GDN_HEADS)
    hs = [slice(h * GDN_N, (h + 1) * GDN_N) for h in heads]
    g = [g_all[:, off + h:off + h + 1] for h in heads]
    gl = [g[h][last:last + 1, :] for h in heads]
    beta = [sm[:, 2 * GDN_HEADS + off + h:2 * GDN_HEADS + off + h + 1] for h in heads]
    q = [q_ref[0, :, hs[h]] for h in heads]
    k = [k_ref[0, :, hs[h]] for h in heads]
    v = [v_ref[0, :, hs[h]] for h in heads]
    s = [s_ref[h] for h in heads]
    decay = [jnp.exp(jnp.where(incl, g[h] - gt_all[off + h:off + h + 1, :], -jnp.inf)) for h in heads]
    yield
    kq = [_dot_nt(jnp.concatenate([k[h], q[h]], axis=0), k[h]) for h in heads]
    yield
    lower = [jnp.where(strict, kq[h][:CHUNK] * decay[h] * beta[h], 0.0) for h in heads]
    attn = [kq[h][CHUNK:] * decay[h] for h in heads]
    o_part = [_dot(q[h] * jnp.exp(g[h]), s[h]) for h in heads]
    yield
    rhs = [jnp.concatenate([v[h] * beta[h], k[h] * (beta[h] * jnp.exp(g[h]))], axis=1) for h in heads]
    sol = yield from _unit_tri_solve(lower, rhs, precise_levels=2)
    v_new = [sol[h][:, :GDN_N] - _dot(sol[h][:, GDN_N:], s[h]) for h in heads]
    yield
    for h in heads:
        o_ref[0, :, hs[h]] = o_part[h] + _dot(attn[h], v_new[h])
    yield
    for h in heads:
        s_ref[h] = s[h] * jnp.exp(gl[h]) + _dot_tn(k[h] * jnp.exp(gl[h] - g[h]), v_new[h])


def gdn_scan(q, k, v, sm, *, n_ctx):
    b, t, _ = q.shape
    toks = [(q, BRANCH, 0, 0), (k, BRANCH, 0, 0), (v, BRANCH, 0, 0), (sm, LANES, 0, 0)]
    return _bidir_scan(_gdn_body, toks, [], (GDN_HEADS, GDN_N, GDN_N), b=b, t=t, n_ctx=n_ctx,
                       batch_block=_batch_block(b, 4))


def _merge_kernel(sf_ref, sb_ref, sx_ref, sz_ref, gf_ref, gb_ref, gr_ref, rf_ref, rb_ref, rg_ref, rbo_ref,
                  df_ref, db_ref, dz_ref, gate_ref, x_ref, m_ref,
                  sd_ref, sn_ref, gn_ref, lnw_ref, lnb_ref, dn_ref, bd64_ref, bd128_ref, bd256_ref,
                  wb_ref, wo_ref, o_ref):
    def group_rms(y, bd_ref, n, w_ref):
        return y * lax.rsqrt(_dot_x2(y * y, bd_ref[...]) * (1.0 / n) + EPS) * w_ref[...]

    y = (sf_ref[0] + sb_ref[0] + sd_ref[...] * sx_ref[0]) * _silu(sz_ref[0])
    ys = group_rms(y, bd256_ref, BRANCH // SSM_GROUPS, sn_ref)
    yg = group_rms(gf_ref[0] + gb_ref[0], bd128_ref, GLA_DV, gn_ref) * _silu(gr_ref[0])
    y = rf_ref[0] + rb_ref[0]
    yc = y - _dot_x2(y, bd64_ref[...]) * (1.0 / RWKV_N)
    var = _dot_x2(yc * yc, bd64_ref[...]) * (1.0 / RWKV_N)
    yr = (yc * lax.rsqrt(var + RWKV_LN_EPS) * lnw_ref[...] + lnb_ref[...] + rbo_ref[0]) * rg_ref[0]
    yd = group_rms(df_ref[0] + db_ref[0], bd128_ref, GDN_N, dn_ref) * _silu(dz_ref[0])
    acc = None
    for i, yi in enumerate((ys, yg, yr, yd)):
        term = gate_ref[0, :, i * D_MODEL:(i + 1) * D_MODEL].astype(F32) * _dot(yi, wb_ref[i])
        acc = term if acc is None else acc + term
    o_ref[0] = x_ref[0] + m_ref[0, 0] * _dot(acc, wo_ref[...])


def merge_residual(ssm, gla, rwkv, gdn, gates, x_all, gate_mod, lp, w_branch, w_out, *, n_ctx):
    b, t, d = x_all.shape
    tm = _pick_tile(n_ctx, ROW_TILE)
    tok = lambda bi, i: (bi, i, 0)
    blk = lambda c: pl.BlockSpec((1, tm, BRANCH), lambda bi, i: (bi, i, c))
    half = blk(0)
    consts = (_row(jnp.repeat(lp["ssm_d"], SSM_P)), _row(lp["ssm_norm"]),
              _row(jnp.tile(lp["gla_norm"], GLA_HEADS)), _row(lp["rwkv_ln_w"]), _row(lp["rwkv_ln_b"]),
              _row(jnp.tile(lp["gdn_norm"], GDN_HEADS)),
              _block_diag_ones(RWKV_N), _block_diag_ones(LANES), _block_diag_ones(BRANCH // SSM_GROUPS),
              w_branch, w_out)
    ins = (ssm[0], ssm[1], ssm[2], ssm[3], gla[0], gla[1], gla[2], rwkv[0], rwkv[1], rwkv[2], rwkv[3],
           gdn[0], gdn[1], gdn[2], gates, x_all, gate_mod) + consts
    specs = ([half, half, half, blk(0), half, half, blk(2), half, half, half, half, half, half, blk(3),
              pl.BlockSpec((1, tm, 4 * d), tok), pl.BlockSpec((1, tm, d), tok),
              pl.BlockSpec((1, 1, 1, d), _mod_sel((t - n_ctx) // tm))]
             + [_const_spec(c.shape) for c in consts])
    return pl.pallas_call(
        _merge_kernel,
        grid=(b, t // tm),
        in_specs=specs,
        out_specs=pl.BlockSpec((1, tm, d), tok),
        out_shape=jax.ShapeDtypeStruct((b, t, d), F32),
        compiler_params=_cparams("parallel", "parallel"),
    )(*ins)


def _route_kernel(x_ref, nw_ref, shift_ref, scale_ref, rw_ref, rb_ref, u_ref, o_ref, cnt_ref, hb_ref, *, rows_kw):
    x = x_ref[...]
    shift, scale = _token_rows([shift_ref, scale_ref], pl.program_id(0), x.shape[0], **rows_kw)
    h = x * lax.rsqrt(jnp.mean(x * x, axis=-1, keepdims=True) + EPS) * nw_ref[...] * (1.0 + scale) + shift
    hb_ref[...] = h.astype(BF16)
    logits = lax.dot_general(rw_ref[...], h, (((1,), (1,)), ((), ())),
                             precision=HI, preferred_element_type=F32)
    scores = _sigmoid(logits)
    sel = scores + rb_ref[...]
    rows = [sel[e:e + 1, :] for e in range(N_EXPERTS)]
    sc = [scores[e:e + 1, :] for e in range(N_EXPERTS)]

    def top2(vals):
        v1, i1 = vals[0], jnp.zeros(vals[0].shape, jnp.int32)
        for j in range(1, len(vals)):
            better = vals[j] > v1
            v1 = jnp.where(better, vals[j], v1)
            i1 = jnp.where(better, j, i1)
        v2 = jnp.where(i1 == 0, vals[1], vals[0])
        i2 = jnp.where(i1 == 0, 1, 0)
        for j in range(1, len(vals)):
            better = (vals[j] > v2) & (i1 != j)
            v2 = jnp.where(better, vals[j], v2)
            i2 = jnp.where(better, j, i2)
        return v1, i1, v2, i2

    gsum = []
    for grp in range(N_GROUPS):
        v1, _, v2, _ = top2(rows[grp * EXPERTS_PER_GROUP:(grp + 1) * EXPERTS_PER_GROUP])
        gsum.append(v1 + v2)
    best, gidx = gsum[0], jnp.zeros(gsum[0].shape, jnp.int32)
    for grp in range(1, N_GROUPS):
        better = gsum[grp] > best
        best = jnp.where(better, gsum[grp], best)
        gidx = jnp.where(better, grp, gidx)
    chosen, chosen_sc = [], []
    for j in range(EXPERTS_PER_GROUP):
        cj, sj = rows[j], sc[j]
        for grp in range(1, N_GROUPS):
            cj = jnp.where(gidx == grp, rows[grp * EXPERTS_PER_GROUP + j], cj)
            sj = jnp.where(gidx == grp, sc[grp * EXPERTS_PER_GROUP + j], sj)
        chosen.append(cj)
        chosen_sc.append(sj)
    _, i1, _, i2 = top2(chosen)
    w1, w2 = jnp.zeros_like(best), jnp.zeros_like(best)
    for j in range(EXPERTS_PER_GROUP):
        w1 = jnp.where(i1 == j, chosen_sc[j], w1)
        w2 = jnp.where(i2 == j, chosen_sc[j], w2)
    tot = w1 + w2
    w1, w2 = w1 / tot, w2 / tot
    tm = scores.shape[1]
    sub = lax.broadcasted_iota(jnp.int32, (SUBLANES, tm), 0)
    ind8 = jnp.zeros((SUBLANES, tm), F32)
    meta = jnp.zeros((SUBLANES, tm), F32)
    for j in range(EXPERTS_PER_GROUP):
        gate_j = jnp.where(i1 == j, w1, 0.0) + jnp.where(i2 == j, w2, 0.0)
        meta = jnp.where(sub == j, gate_j, meta)
    for grp in range(N_GROUPS):
        ind8 = jnp.where((sub == grp) & (gidx == grp), 1.0, ind8)
    before = jnp.dot(ind8.astype(BF16), u_ref[...], preferred_element_type=F32)
    rank = jnp.sum(ind8 * before, axis=0, keepdims=True)
    meta = jnp.where(sub == _META_GROUP, gidx.astype(F32), meta)
    meta = jnp.where(sub == _META_RANK, rank, meta)
    o_ref[...] = meta
    counts = jnp.sum(ind8, axis=1, keepdims=True)
    lane = lax.broadcasted_iota(jnp.int32, (SUBLANES, LANES), 1)
    row = lax.broadcasted_iota(jnp.int32, (SUBLANES, LANES), 0)
    cnt_ref[0] = jnp.broadcast_to(jnp.sum(jnp.where(lane == row, counts, 0.0), axis=0, keepdims=True),
                                  (SUBLANES, LANES)).astype(jnp.int32)


_META_GROUP, _META_RANK = EXPERTS_PER_GROUP, EXPERTS_PER_GROUP + 1


def moe_route(x_all, norm_w, shift_rows, scale_rows, router_w, router_b, *, n_ctx, tm):
    b, t, d = x_all.shape
    m = b * t
    upper = jnp.asarray(np.triu(np.ones((tm, tm), np.float32), 1), BF16)
    return pl.pallas_call(
        functools.partial(_route_kernel, rows_kw=dict(bsz=b, t_all=t, n_ctx=n_ctx)),
        grid=(m // tm,),
        in_specs=[pl.BlockSpec((tm, d), lambda i: (i, 0)),
                  _const_spec((1, d)), _const_spec(shift_rows.shape), _const_spec(scale_rows.shape),
                  pl.BlockSpec((N_EXPERTS, d), lambda i: (0, 0)),
                  pl.BlockSpec((N_EXPERTS, 1), lambda i: (0, 0)),
                  _const_spec((tm, tm))],
        out_specs=[pl.BlockSpec((SUBLANES, tm), lambda i: (0, i)),
                   pl.BlockSpec((1, SUBLANES, LANES), lambda i: (i, 0, 0)),
                   pl.BlockSpec((tm, d), lambda i: (i, 0))],
        out_shape=[jax.ShapeDtypeStruct((SUBLANES, m), F32),
                   jax.ShapeDtypeStruct((m // tm, SUBLANES, LANES), jnp.int32),
                   jax.ShapeDtypeStruct((m, d), BF16)],
        compiler_params=_cparams("parallel"),
    )(x_all.reshape(m, d), norm_w.reshape(1, d), shift_rows, scale_rows,
      router_w.T, router_b.reshape(N_EXPERTS, 1), upper)


MOE_TILE = 1024
MOE_SUB_ROWS = 256
MOE_TAIL_ROWS = 128


def _expert_kernel(cnt_ref, h_ref, mr_ref, mc_ref, wg_ref, wu_ref, wd_ref, x_ref, gate_ref, o_ref, *, rows_kw):
    i, grp = pl.program_id(0), pl.program_id(1)

    @pl.when(grp == 0)
    def _():
        o_ref[...] = jnp.zeros_like(o_ref)

    tm = h_ref.shape[0]
    count = cnt_ref[i * N_GROUPS + grp]
    grp_f = grp.astype(F32)
    sel_row = jnp.where(mr_ref[_META_GROUP:_META_GROUP + 1, :] == grp_f, mr_ref[_META_RANK:_META_RANK + 1, :], -1.0)
    sel_col = jnp.where(mc_ref[:, _META_GROUP:_META_GROUP + 1] == grp_f, mc_ref[:, _META_RANK:_META_RANK + 1], -1.0)
    gate_parts = _split3(mc_ref[...])

    def sub_block(first, rows):
        base = first.astype(F32)
        slot_r = lax.broadcasted_iota(jnp.int32, (rows, tm), 0).astype(F32)
        slot_c = lax.broadcasted_iota(jnp.int32, (tm, rows), 1).astype(F32)
        pick = (sel_row - base == slot_r).astype(BF16)
        put = (sel_col - base == slot_c).astype(BF16)
        xg = jnp.dot(pick, h_ref[...], preferred_element_type=F32).astype(BF16)
        gates = sum(jnp.dot(pick, p, preferred_element_type=F32) for p in gate_parts)
        y = jnp.zeros((rows, o_ref.shape[1]), F32)
        for e in range(EXPERTS_PER_GROUP):
            hid = _silu(_dot(xg, wg_ref[e])) * _dot(xg, wu_ref[e])
            y = y + _dot(gates[:, e:e + 1] * hid, wd_ref[e])
        yh = y.astype(BF16)
        yl = (y - yh.astype(F32)).astype(BF16)
        o_ref[...] += (jnp.dot(put, yh, preferred_element_type=F32)
                       + jnp.dot(put, yl, preferred_element_type=F32))

    n_full = count // MOE_SUB_ROWS
    rem = count - n_full * MOE_SUB_ROWS
    n_main = n_full + (rem > MOE_TAIL_ROWS).astype(jnp.int32)

    def main_block(s, carry):
        sub_block(s * MOE_SUB_ROWS, MOE_SUB_ROWS)
        return carry

    lax.fori_loop(0, n_main, main_block, 0)

    @pl.when((rem > 0) & (rem <= MOE_TAIL_ROWS))
    def _():
        sub_block(n_full * MOE_SUB_ROWS, MOE_TAIL_ROWS)

    @pl.when(grp == N_GROUPS - 1)
    def _():
        (gate,) = _token_rows([gate_ref], i, tm, **rows_kw)
        o_ref[...] = x_ref[...] + gate * o_ref[...]


def moe_experts(hb, meta, counts, wg, wu, wd, x_all, gate_rows, *, n_ctx, tm):
    b, t, d = x_all.shape
    m = b * t
    tok = lambda i, g, cnt: (i, 0)
    grid_spec = pltpu.PrefetchScalarGridSpec(
        num_scalar_prefetch=1,
        grid=(m // tm, N_GROUPS),
        in_specs=[pl.BlockSpec((tm, d), tok),
                  pl.BlockSpec((SUBLANES, tm), lambda i, g, cnt: (0, i)),
                  pl.BlockSpec((tm, SUBLANES), tok),
                  pl.BlockSpec((EXPERTS_PER_GROUP, d, EXPERT_FF), lambda i, g, cnt: (g, 0, 0)),
                  pl.BlockSpec((EXPERTS_PER_GROUP, d, EXPERT_FF), lambda i, g, cnt: (g, 0, 0)),
                  pl.BlockSpec((EXPERTS_PER_GROUP, EXPERT_FF, d), lambda i, g, cnt: (g, 0, 0)),
                  pl.BlockSpec((tm, d), tok),
                  pl.BlockSpec(gate_rows.shape, lambda i, g, cnt: (0, 0))],
        out_specs=pl.BlockSpec((tm, d), tok))
    out = pl.pallas_call(
        functools.partial(_expert_kernel, rows_kw=dict(bsz=b, t_all=t, n_ctx=n_ctx)),
        grid_spec=grid_spec,
        out_shape=jax.ShapeDtypeStruct((m, d), F32),
        compiler_params=pltpu.CompilerParams(dimension_semantics=("parallel", "arbitrary"),
                                             vmem_limit_bytes=MOE_VMEM_LIMIT),
    )(counts[:, 0, :N_GROUPS].reshape(-1), hb, meta, meta.T, wg, wu, wd, x_all.reshape(m, d), gate_rows)
    return out.reshape(b, t, d)


def _token_rows(m_refs, tile, tm, *, bsz, t_all, n_ctx):
    row = tile * tm + lax.broadcasted_iota(jnp.int32, (tm, 1), 0)
    ctx = jnp.zeros((tm, 1), jnp.bool_)
    lat = []
    for bi in range(bsz):
        lo, split = bi * t_all, bi * t_all + t_all - n_ctx
        lat.append((row >= lo) & (row < split))
        ctx = ctx | ((row >= split) & (row < lo + t_all))
    out = []
    for m_ref in m_refs:
        v = jnp.where(ctx, m_ref[bsz:bsz + 1, :], 0.0)
        for bi in range(bsz):
            v = v + jnp.where(lat[bi], m_ref[bi:bi + 1, :], 0.0)
        out.append(v)
    return out


def _grid_view(x_all, n_lat, tt):
    b, t, d = x_all.shape
    rows = n_lat // GRID_W
    assert rows == GRID_W and t % GRID_W == 0 and tt % rows == 0
    last = n_lat // tt - 1
    spec = pl.BlockSpec((1, rows, (tt // rows) * d), lambda bi, i: (bi, 0, jnp.minimum(i, last)))
    return x_all.reshape(b, t // GRID_W, GRID_W * d), spec


def _grid_tile(xg_ref, d):
    return jnp.concatenate([xg_ref[0, :, j * d:(j + 1) * d] for j in range(xg_ref.shape[2] // d)], axis=0)


def _transpose_grid_kernel(xn_ref, xg_ref, o_ref, *, n_lat_tiles):
    i = pl.program_id(1)

    @pl.when(i < n_lat_tiles)
    def _():
        o_ref[0] = _grid_tile(xg_ref, o_ref.shape[2])

    @pl.when(i >= n_lat_tiles)
    def _():
        o_ref[0] = xn_ref[0]


def transpose_grid(x_all, *, n_ctx):
    b, t, d = x_all.shape
    tt = _pick_tile(n_ctx, ROW_TILE)
    xg, gspec = _grid_view(x_all, t - n_ctx, tt)
    tok = pl.BlockSpec((1, tt, d), lambda bi, i: (bi, i, 0))
    return pl.pallas_call(
        functools.partial(_transpose_grid_kernel, n_lat_tiles=(t - n_ctx) // tt),
        grid=(b, t // tt),
        in_specs=[tok, gspec],
        out_specs=tok,
        out_shape=jax.ShapeDtypeStruct((b, t, d), x_all.dtype),
        compiler_params=_cparams("parallel", "parallel"),
    )(x_all, xg)


def _final_norm_kernel(x_ref, w_ref, o_ref, *, from_grid):
    x = _grid_tile(x_ref, o_ref.shape[2]) if from_grid else x_ref[0]
    o_ref[0] = x * lax.rsqrt(jnp.mean(x * x, axis=-1, keepdims=True) + EPS) * w_ref[...]


def final_rms_norm(x_all, w, *, n_ctx, from_grid):
    b, t, d = x_all.shape
    n_lat = t - n_ctx
    tt = _pick_tile(n_ctx, ROW_TILE)
    tok = pl.BlockSpec((1, tt, d), lambda bi, i: (bi, i, 0))
    if from_grid:
        x_in, spec = _grid_view(x_all, n_lat, tt)
    else:
        x_in, spec = x_all, tok
    return pl.pallas_call(
        functools.partial(_final_norm_kernel, from_grid=from_grid),
        grid=(b, n_lat // tt),
        in_specs=[spec, _const_spec((1, d))],
        out_specs=tok,
        out_shape=jax.ShapeDtypeStruct((b, n_lat, d), F32),
        compiler_params=_cparams("parallel", "parallel"),
    )(x_in, w.reshape(1, d))


def _pack_w_in(w_in, mixer):
    cols = _SRC_COLS[mixer]
    pieces, i = [], 0
    while i < len(cols):
        j = i
        if cols[i] < 0:
            while j < len(cols) and cols[j] < 0:
                j += 1
            pieces.append(jnp.zeros((w_in.shape[0], j - i), w_in.dtype))
        else:
            while j < len(cols) and cols[j] == cols[i] + (j - i):
                j += 1
            pieces.append(w_in[:, int(cols[i]):int(cols[i]) + (j - i)])
        i = j
    return jnp.concatenate(pieces, axis=1).astype(BF16)


def mixer_scans(ps, lp, *, n_ctx):
    p_ssm, p_gla, p_rwkv, p_gdn = ps

    xs, bc, sm = ssm_prep(p_ssm, lp, n_ctx=n_ctx)
    neg_a = jnp.pad(-jnp.exp(lp["ssm_a_log"]).reshape(1, -1), ((0, 0), (0, LANES - 2 * SSM_HEADS)))
    ssm = tuple(ssd_scan(xs, bc, sm, neg_a, n_ctx=n_ctx)) + (xs, p_ssm)

    w2 = [jnp.zeros((LANES, GLA_HEADS * GLA_DK), F32).at[d * GLA_RANK:(d + 1) * GLA_RANK].set(lp["gla_w2"][d])
          for d in range(2)]
    gb = [_row(lp["gla_b"][d]) for d in range(2)]
    gla = tuple(gla_scan(p_gla, w2, gb, n_ctx=n_ctx)) + (p_gla,)

    r, k, v, kk, a, lw, g, bonus = rwkv_prep(p_rwkv, lp, n_ctx=n_ctx)
    rwkv = tuple(rwkv_scan(r, k, v, kk, a, lw, _row(lp["rwkv_k_a"]), n_ctx=n_ctx)) + (g, bonus)

    q, kd, vd, smd = gdn_prep(p_gdn, lp, n_ctx=n_ctx)
    gdn = tuple(gdn_scan(q, kd, vd, smd, n_ctx=n_ctx)) + (p_gdn,)
    return ssm, gla, rwkv, gdn


def kernel(x, c, ctx, c_ctx, ada_w, ada_b, norm_mix, norm_ffn, w_in, w_gate, w_branch, w_out, ssm_conv_w, ssm_conv_b, ssm_a_log, ssm_dt_bias, ssm_d, ssm_norm, gla_w2, gla_b, gla_norm, rwkv_mu, rwkv_w0, rwkv_w2, rwkv_a0, rwkv_a2, rwkv_g2, rwkv_k_k, rwkv_k_a, rwkv_r_k, rwkv_ln_w, rwkv_ln_b, gdn_conv_w, gdn_a_log, gdn_dt_bias, gdn_norm, router_w, router_b, moe_w_gate, moe_w_up, moe_w_down, final_norm):
    bsz, seq, d = x.shape
    n_ctx = ctx.shape[1]
    t_all = n_ctx + seq
    m_all = bsz * t_all

    cond = jnp.concatenate([jax.nn.silu(c), jax.nn.silu(c_ctx)[None]], 0)
    cond = jnp.pad(cond, ((0, SUBLANES - cond.shape[0]), (0, 0)))
    mods, mod_rows = [], []
    for l in range(DEPTH):
        mod = pmatmul(cond, ada_w[l], tm=SUBLANES, tn=1024, precise=True) + ada_b[l]
        mod_rows.append(mod)
        lat = mod[:bsz].reshape(bsz, 6, d)
        cx = jnp.broadcast_to(mod[bsz].reshape(1, 6, d), (bsz, 6, d))
        mods.append(jnp.stack([cx, lat], axis=1))

    x_all = jnp.concatenate([x, ctx], axis=1)
    scan_order = False
    for l in range(DEPTH):
        if (l % 2 == 1) != scan_order:
            x_all = transpose_grid(x_all, n_ctx=n_ctx)
            scan_order = not scan_order
        lp = dict(ssm_conv_w=ssm_conv_w[l], ssm_conv_b=ssm_conv_b[l], ssm_a_log=ssm_a_log[l],
                  ssm_dt_bias=ssm_dt_bias[l], ssm_d=ssm_d[l], ssm_norm=ssm_norm[l],
                  gla_w2=gla_w2[l], gla_b=gla_b[l], gla_norm=gla_norm[l],
                  rwkv_mu=rwkv_mu[l], rwkv_w0=rwkv_w0[l], rwkv_w2=rwkv_w2[l], rwkv_a0=rwkv_a0[l],
                  rwkv_a2=rwkv_a2[l], rwkv_g2=rwkv_g2[l], rwkv_k_k=rwkv_k_k[l], rwkv_k_a=rwkv_k_a[l],
                  rwkv_r_k=rwkv_r_k[l], rwkv_ln_w=rwkv_ln_w[l], rwkv_ln_b=rwkv_ln_b[l],
                  gdn_conv_w=gdn_conv_w[l], gdn_a_log=gdn_a_log[l], gdn_dt_bias=gdn_dt_bias[l],
                  gdn_norm=gdn_norm[l])
        mod = mods[l]
        msel = lambda i: mod[:, :, i][:, :, None, :]

        h = norm_modulate(x_all, norm_mix[l], msel(0), msel(1), n_ctx=n_ctx)
        h2d = h.reshape(m_all, d)
        ps = []
        for mixer in ("ssm", "gla", "rwkv", "gdn"):
            wp = _pack_w_in(w_in[l], mixer)
            ps.append(pmatmul(h2d, wp, tm=512, tn=wp.shape[1]).reshape(bsz, t_all, wp.shape[1]))
        wg_cat = jnp.concatenate([w_gate[l, i] for i in range(4)], axis=1).astype(BF16)
        gates = pmatmul(h2d, wg_cat, tm=1024, tn=1024, act="sigmoid", out_dtype=BF16)
        gates = gates.reshape(bsz, t_all, 4 * d)

        ssm, gla, rwkv, gdn = mixer_scans(ps, lp, n_ctx=n_ctx)
        x_all = merge_residual(ssm, gla, rwkv, gdn, gates, x_all, msel(2), lp,
                               w_branch[l].astype(BF16), w_out[l].astype(BF16), n_ctx=n_ctx)

        tm_moe = _pick_tile(m_all, MOE_TILE)
        rows = lambda i: mod_rows[l][:, i * d:(i + 1) * d]
        meta, counts, hb = moe_route(x_all, norm_ffn[l], rows(3), rows(4), router_w, router_b,
                                     n_ctx=n_ctx, tm=tm_moe)
        x_all = moe_experts(hb, meta, counts, moe_w_gate[l].astype(BF16), moe_w_up[l].astype(BF16),
                            moe_w_down[l].astype(BF16), x_all, rows(5), n_ctx=n_ctx, tm=tm_moe)

    return final_rms_norm(x_all, final_norm, n_ctx=n_ctx, from_grid=scan_order)
```

```python
import functools
import itertools

import numpy as np
import jax
import jax.numpy as jnp
from jax import lax
from jax.experimental import pallas as pl
from jax.experimental.pallas import tpu as pltpu

F32 = jnp.float32
BF16 = jnp.bfloat16
HI = lax.Precision.HIGHEST

D_MODEL = 1024
DEPTH = 2
GRID_W = 64
CHUNK = 64
EPS = 1e-6
BRANCH = D_MODEL // 2
SSM_HEADS, SSM_P, SSM_GROUPS, SSM_N = 8, 64, 2, 64
GLA_HEADS, GLA_DK, GLA_DV, GLA_RANK, GLA_TAU = 4, 64, 128, 16, 16.0
RWKV_HEADS, RWKV_N, RWKV_LN_EPS = 8, 64, 64e-5
GDN_HEADS, GDN_N = 4, 128
N_EXPERTS, N_GROUPS, EXPERTS_PER_GROUP = 16, 4, 4
EXPERT_FF = D_MODEL // 2
LANES = 128
SUBLANES = 8
VMEM_LIMIT = 48 * 1024 * 1024
MOE_VMEM_LIMIT = 56 * 1024 * 1024
ROW_TILE = 256

_REF_BLOCKS = (
    ("ssm", "z", 512), ("ssm", "xbc", 768), ("ssm", "dt", 16),
    ("gla", "q", 256), ("gla", "k", 256), ("gla", "v", 512), ("gla", "r", 512), ("gla", "glr", 32),
    ("rwkv", "all", 1920),
    ("gdn", "qkv", 1536), ("gdn", "z", 512), ("gdn", "ab", 16),
)
_PACKED = {
    "ssm": (("z", 512), ("dt", 128), ("pad", 128), ("xbc", 768)),
    "gla": (("q", 256), ("k", 256), ("v", 512), ("r", 512), ("glr", 128)),
    "rwkv": (("all", 1920),),
    "gdn": (("qkv", 1536), ("z", 512), ("ab", 128)),
}


def _packed_columns():
    start, s = {}, 0
    for mixer, blk, w in _REF_BLOCKS:
        start[(mixer, blk)] = (s, w)
        s += w
    out = {}
    for mixer, blocks in _PACKED.items():
        cols = []
        for blk, wp in blocks:
            s0, w = start.get((mixer, blk), (0, 0))
            cols += list(range(s0, s0 + w)) + [-1] * (wp - w)
        out[mixer] = np.asarray(cols, np.int32)
    return out


_SRC_COLS = _packed_columns()


def _cparams(*sem):
    return pltpu.CompilerParams(dimension_semantics=sem, vmem_limit_bytes=VMEM_LIMIT)


def _dot(a, b):
    return jnp.dot(a.astype(BF16), b.astype(BF16), preferred_element_type=F32)


def _dot_nt(a, b):
    return lax.dot_general(a.astype(BF16), b.astype(BF16), (((1,), (1,)), ((), ())),
                           preferred_element_type=F32)


def _dot_tn(a, b):
    return lax.dot_general(a.astype(BF16), b.astype(BF16), (((0,), (0,)), ((), ())),
                           preferred_element_type=F32)


def _dot_hi(a, b):
    return jnp.dot(a, b, precision=HI, preferred_element_type=F32)


def _dot_x3(a, b):
    ah = a.astype(BF16)
    al = (a - ah.astype(F32)).astype(BF16)
    bh = b.astype(BF16)
    bl = (b - bh.astype(F32)).astype(BF16)
    f = lambda u, v: jnp.dot(u, v, preferred_element_type=F32)
    return f(ah, bh) + (f(ah, bl) + f(al, bh))


def _dot_x2(a, w):
    ah = a.astype(BF16)
    al = (a - ah.astype(F32)).astype(BF16)
    return jnp.dot(ah, w, preferred_element_type=F32) + jnp.dot(al, w, preferred_element_type=F32)


def _softplus(x):
    return jnp.maximum(x, 0.0) + jnp.log(1.0 + jnp.exp(-jnp.abs(x)))


def _sigmoid(x):
    return 1.0 / (1.0 + jnp.exp(-x))


def _silu(x):
    return x * _sigmoid(x)


def _pick_tile(m, pref):
    t = pref
    while m % t:
        t //= 2
    return t


def _block_diag_ones(n, width=BRANCH):
    idx = np.arange(width) // n
    return jnp.asarray(idx[:, None] == idx[None, :], BF16)


def _mm_kernel(a_ref, w_ref, o_ref, *, act, precise):
    if precise:
        r = _dot_hi(a_ref[...].astype(F32), w_ref[...].astype(F32))
    else:
        r = _dot(a_ref[...], w_ref[...])
    if act == "sigmoid":
        r = _sigmoid(r)
    o_ref[...] = r.astype(o_ref.dtype)


def pmatmul(a, w, *, tm, tn, act=None, precise=False, out_dtype=F32):
    m, k = a.shape
    n = w.shape[1]
    tm = _pick_tile(m, tm)
    assert tm % SUBLANES == 0 and n % tn == 0, (m, tm, n, tn)
    return pl.pallas_call(
        functools.partial(_mm_kernel, act=act, precise=precise),
        grid=(n // tn, m // tm),
        in_specs=[pl.BlockSpec((tm, k), lambda j, i: (i, 0)),
                  pl.BlockSpec((k, tn), lambda j, i: (0, j))],
        out_specs=pl.BlockSpec((tm, tn), lambda j, i: (i, j)),
        out_shape=jax.ShapeDtypeStruct((m, n), out_dtype),
        compiler_params=_cparams("parallel", "parallel"),
    )(a, w)


def _norm_mod_kernel(x_ref, w_ref, shift_ref, scale_ref, o_ref):
    x = x_ref[0]
    y = x * lax.rsqrt(jnp.mean(x * x, axis=-1, keepdims=True) + EPS) * w_ref[...]
    o_ref[0] = (y * (1.0 + scale_ref[0, 0]) + shift_ref[0, 0]).astype(o_ref.dtype)


def _mod_sel(n_lat_tiles):
    return lambda bi, i, *_: (bi, jnp.where(i < n_lat_tiles, 1, 0), 0, 0)


def norm_modulate(x_all, w, shift, scale, *, n_ctx, out_dtype=BF16):
    b, t, d = x_all.shape
    tm = _pick_tile(n_ctx, ROW_TILE)
    assert t % tm == 0
    tok = lambda bi, i: (bi, i, 0)
    return pl.pallas_call(
        _norm_mod_kernel,
        grid=(b, t // tm),
        in_specs=[pl.BlockSpec((1, tm, d), tok),
                  pl.BlockSpec((1, d), lambda bi, i: (0, 0)),
                  pl.BlockSpec((1, 1, 1, d), _mod_sel((t - n_ctx) // tm)),
                  pl.BlockSpec((1, 1, 1, d), _mod_sel((t - n_ctx) // tm))],
        out_specs=pl.BlockSpec((1, tm, d), tok),
        out_shape=jax.ShapeDtypeStruct((b, t, d), out_dtype),
        compiler_params=_cparams("parallel", "parallel"),
    )(x_all, w.reshape(1, d), shift, scale)


def _row(v):
    return v.reshape(1, -1).astype(F32)


def _const_spec(shape):
    return pl.BlockSpec(shape, lambda *_: (0,) * len(shape))


def _tile_specs(tt, width, col):
    r8 = tt // SUBLANES
    main = pl.BlockSpec((1, tt, width), lambda bi, i: (bi, i, col))
    prev = pl.BlockSpec((1, SUBLANES, width), lambda bi, i: (bi, jnp.maximum(i * r8 - 1, 0), col))
    return main, prev, r8


def _halo_specs(tt, width, col, t):
    main, prev, r8 = _tile_specs(tt, width, col)
    last8 = t // SUBLANES - 1
    nxt = pl.BlockSpec((1, SUBLANES, width), lambda bi, i: (bi, jnp.minimum((i + 1) * r8, last8), col))
    return [main, prev, nxt]


def _neighbours(x, prev8, next8, *, nct, nt):
    i = pl.program_id(1)
    tt = x.shape[0]
    row = lax.broadcasted_iota(jnp.int32, x.shape, 0)
    first = (i == 0) | (i == nct)
    last = (i == nct - 1) | (i == nt - 1)
    pr = jnp.where(first, 0.0, prev8[SUBLANES - 1:SUBLANES, :])
    nx = jnp.where(last, 0.0, next8[0:1, :])
    xp = jnp.where(row == 0, pr, pltpu.roll(x, 1, 0))
    xn = jnp.where(row == tt - 1, nx, pltpu.roll(x, tt - 1, 0))
    return xp, xn


def _prep_call(kernel, ins, in_specs, out_widths, *, b, t, tt, out_dtype=F32):
    tok = lambda bi, i: (bi, i, 0)
    return pl.pallas_call(
        kernel,
        grid=(b, t // tt),
        in_specs=in_specs,
        out_specs=[pl.BlockSpec((1, tt, w), tok) for w in out_widths],
        out_shape=[jax.ShapeDtypeStruct((b, t, w), out_dtype) for w in out_widths],
        compiler_params=_cparams("parallel", "parallel"),
    )(*ins)


def _chunk_masks(reverse):
    row = lax.broadcasted_iota(jnp.int32, (CHUNK, CHUNK), 0)
    col = lax.broadcasted_iota(jnp.int32, (CHUNK, CHUNK), 1)
    if reverse:
        return col >= row, col > row
    return col <= row, col < row


def _chunk_order(i, n_ctx_chunks, n_chunks, reverse):
    n_lat_chunks = n_chunks - n_ctx_chunks
    if not reverse:
        return jnp.where(i < n_ctx_chunks, n_lat_chunks + i, i - n_ctx_chunks)
    return jnp.where(i < n_ctx_chunks, n_chunks - 1 - i, n_lat_chunks - 1 - (i - n_ctx_chunks))


def _split3(a):
    hi = a.astype(BF16)
    r = a - hi.astype(F32)
    mid = r.astype(BF16)
    return hi, mid, (r - mid.astype(F32)).astype(BF16)


def _transpose_small(x):
    row = lax.broadcasted_iota(jnp.int32, (LANES, LANES), 0)
    col = lax.broadcasted_iota(jnp.int32, (LANES, LANES), 1)
    eye = (row == col).astype(BF16)
    nt = lambda p: lax.dot_general(eye, p, (((1,), (1,)), ((), ())), preferred_element_type=F32)
    hi, mid, lo = _split3(x)
    return nt(hi) + (nt(mid) + nt(lo))


def _chunk_cumsum(incl, x):
    m = incl.astype(BF16)
    hi, mid, lo = _split3(x)
    f = lambda p: jnp.dot(m, p, preferred_element_type=F32)
    return f(hi) + (f(mid) + f(lo))


def _select_columns(x, sel):
    c = x.shape[0]
    y = jnp.dot(jnp.concatenate(_split3(x), axis=0), sel, preferred_element_type=F32)
    return y[:c] + (y[c:2 * c] + y[2 * c:])


def _unit_tri_solve(mats, rhs, precise_levels=0):
    n = range(len(mats))
    x = [rhs[h] - _dot_x3(mats[h], rhs[h]) for h in n]
    yield
    p = mats
    for level in range(int(np.log2(CHUNK)) - 1):
        dot = _dot_x3 if level < precise_levels else _dot
        p = [dot(p[h], p[h]) for h in n]
        yield
        x = [x[h] + dot(p[h], x[h]) for h in n]
        yield
    return x


def _bidir_scan(body, tok_ins, const_ins, state_shape, *, b, t, n_ctx, lockstep=True, batch_block=1):
    nc, ncc = t // CHUNK, n_ctx // CHUNK
    nb = batch_block
    assert b % nb == 0

    def chunk_spec(width, col, reverse):
        return pl.BlockSpec((nb, CHUNK, width), lambda bi, i: (bi, _chunk_order(i, ncc, nc, reverse), col))

    def direction(reverse):
        d = int(reverse)
        specs = [chunk_spec(w, cols[d], reverse) for _, w, *cols in tok_ins]
        specs += [_const_spec(pair[d].shape) for pair in const_ins]
        return specs, [a for a, *_ in tok_ins] + [pair[d] for pair in const_ins]

    (spec_f, arg_f), (spec_b, arg_b) = direction(False), direction(True)
    n_tok, n_in = len(tok_ins), len(arg_f)

    def kern(*refs):
        o_f, o_b, s_f, s_b = refs[2 * n_in:]

        @pl.when(pl.program_id(1) == 0)
        def _():
            s_f[...] = jnp.zeros_like(s_f)
            s_b[...] = jnp.zeros_like(s_b)

        def one(j, ins, o_ref, s_ref, reverse):
            ins = [r.at[pl.ds(j, 1)] if k < n_tok else r for k, r in enumerate(ins)]
            return body(*ins, o_ref.at[pl.ds(j, 1)], s_ref.at[j], reverse=reverse)

        gens = []
        for j in range(nb):
            gens += [one(j, refs[:n_in], o_f, s_f, False), one(j, refs[n_in:2 * n_in], o_b, s_b, True)]
        if not lockstep:
            gens = [itertools.chain(*gens)]
        while gens:
            gens = [g for g in gens if next(g, _DONE) is not _DONE]

    return pl.pallas_call(
        kern,
        grid=(b // nb, nc),
        in_specs=spec_f + spec_b,
        out_specs=[chunk_spec(BRANCH, 0, False), chunk_spec(BRANCH, 0, True)],
        out_shape=[jax.ShapeDtypeStruct((b, t, BRANCH), F32)] * 2,
        scratch_shapes=[pltpu.VMEM((nb,) + tuple(state_shape), F32)] * 2,
        compiler_params=_cparams("parallel", "arbitrary"),
    )(*arg_f, *arg_b)


_DONE = object()


def _batch_block(b, pref):
    return pref if b % pref == 0 else 1


def _ssm_prep_kernel(x_ref, xp_ref, xn_ref, dt_ref, cw_ref, cb_ref, dtb_ref, xs_ref, bc_ref, sm_ref,
                     *, nct, nt):
    x = x_ref[0]
    xp, xn = _neighbours(x, xp_ref[0], xn_ref[0], nct=nct, nt=nt)
    y = _silu(xp * cw_ref[0:1, :] + x * cw_ref[1:2, :] + xn * cw_ref[2:3, :] + cb_ref[...])
    xs_ref[0] = y[:, :BRANCH]
    bc_ref[0] = y[:, BRANCH:]
    sm_ref[0] = _softplus(dt_ref[0] + dtb_ref[...])


def ssm_prep(p, lp, *, n_ctx):
    b, t, _ = p.shape
    tt = _pick_tile(n_ctx, ROW_TILE)
    dtb = jnp.pad(lp["ssm_dt_bias"].reshape(1, -1), ((0, 0), (0, LANES - 2 * SSM_HEADS)))
    specs = _halo_specs(tt, 768, 1, t) + [pl.BlockSpec((1, tt, LANES), lambda bi, i: (bi, i, 4)),
                                          _const_spec((3, 768)), _const_spec((1, 768)), _const_spec((1, LANES))]
    kern = functools.partial(_ssm_prep_kernel, nct=(t - n_ctx) // tt, nt=t // tt)
    return _prep_call(kern, (p, p, p, p, lp["ssm_conv_w"], _row(lp["ssm_conv_b"]), dtb), specs,
                      (BRANCH, 2 * SSM_GROUPS * SSM_N, LANES), b=b, t=t, tt=tt)


def _ssd_body(x_ref, bc_ref, sm_ref, na_ref, o_ref, s_ref, *, reverse):
    incl, _ = _chunk_masks(reverse)
    last = 0 if reverse else CHUNK - 1
    off = SSM_HEADS if reverse else 0
    dt_all = sm_ref[0]
    g_all = _chunk_cumsum(incl, dt_all * na_ref[...])
    yield
    expand = _expand_matrix(off, SSM_HEADS, SSM_P)
    gx = _select_columns(g_all, expand)
    dx = _select_columns(dt_all, expand)
    gt_all = _transpose_small(g_all)
    dtt_all = _transpose_small(dt_all)
    yield
    heads = range(SSM_HEADS)
    rep = SSM_HEADS // SSM_GROUPS
    gw = SSM_GROUPS * SSM_N
    pw = rep * SSM_P
    hs = [slice(h * SSM_P, (h + 1) * SSM_P) for h in heads]
    glx = gx[last:last + 1, :]
    egx = jnp.exp(gx)
    wx = dx * jnp.exp(glx - gx)
    eglx = jnp.exp(glx)
    x = x_ref[0]
    bm = [bc_ref[0, :, grp * SSM_N:(grp + 1) * SSM_N] for grp in range(SSM_GROUPS)]
    cm = [bc_ref[0, :, gw + grp * SSM_N:gw + (grp + 1) * SSM_N] for grp in range(SSM_GROUPS)]
    cb = [_dot_nt(cm[grp], bm[grp]) for grp in range(SSM_GROUPS)]
    s = [s_ref[grp] for grp in range(SSM_GROUPS)]
    yield
    scores = [cb[h // rep] * jnp.exp(jnp.where(incl, gx[:, hs[h]] - gt_all[off + h:off + h + 1, :], -jnp.inf))
              * dtt_all[off + h:off + h + 1, :] for h in heads]
    yield
    intra = [_dot(scores[h], x[:, hs[h]]) for h in heads]
    yield
    inter = [_dot(cm[grp], s[grp]) for grp in range(SSM_GROUPS)]
    yield
    upd = [_dot_tn(bm[h // rep] * wx[:, hs[h]], x[:, hs[h]]) for h in heads]
    yield
    for h in heads:
        grp, ls = h // rep, slice((h % rep) * SSM_P, (h % rep + 1) * SSM_P)
        o_ref[0, :, hs[h]] = intra[h] + egx[:, hs[h]] * inter[grp][:, ls]
        s_ref[grp, :, ls] = s[grp][:, ls] * eglx[:, hs[h]] + upd[h]


def _expand_matrix(off, n_heads, width):
    row = lax.broadcasted_iota(jnp.int32, (LANES, n_heads * width), 0)
    col = lax.broadcasted_iota(jnp.int32, (LANES, n_heads * width), 1)
    lo = row * width - off * width
    return ((col >= lo) & (col < lo + width)).astype(BF16)


def ssd_scan(xs, bc, sm, neg_a, *, n_ctx):
    b, t, _ = xs.shape
    toks = [(xs, BRANCH, 0, 0), (bc, 2 * SSM_GROUPS * SSM_N, 0, 0), (sm, LANES, 0, 0)]
    state = (SSM_GROUPS, SSM_N, (SSM_HEADS // SSM_GROUPS) * SSM_P)
    return _bidir_scan(_ssd_body, toks, [(neg_a, neg_a)], state, b=b, t=t, n_ctx=n_ctx,
                       batch_block=_batch_block(b, 2))


def _gla_body(q_ref, k_ref, v_ref, glr_ref, w2_ref, gb_ref, o_ref, s_ref, *, reverse):
    incl, _ = _chunk_masks(reverse)
    last = 0 if reverse else CHUNK - 1
    logit = _dot_x3(glr_ref[0], w2_ref[...]) + gb_ref[...]
    yield
    la = -_softplus(-logit) * (1.0 / GLA_TAU)
    g_all = _chunk_cumsum(incl, la)
    yield
    heads = range(GLA_HEADS)
    ks = [slice(h * GLA_DK, (h + 1) * GLA_DK) for h in heads]
    vs = [slice(h * GLA_DV, (h + 1) * GLA_DV) for h in heads]
    g = [g_all[:, ks[h]] for h in heads]
    gl = [g[h][last:last + 1, :] for h in heads]
    k = [k_ref[0, :, ks[h]] for h in heads]
    v = [v_ref[0, :, vs[h]] for h in heads]
    qg = [q_ref[0, :, ks[h]] * (GLA_DK ** -0.5) * jnp.exp(g[h]) for h in heads]
    st = [s_ref[h] for h in heads]
    yield
    scores = [jnp.where(incl, _dot_nt(qg[h], k[h] * jnp.exp(-g[h])), 0.0) for h in heads]
    yield
    intra = [_dot(scores[h], v[h]) for h in heads]
    yield
    inter = [_dot_nt(qg[h], st[h]) for h in heads]
    yield
    upd = [_dot_tn(v[h], k[h] * jnp.exp(gl[h] - g[h])) for h in heads]
    yield
    for h in heads:
        o_ref[0, :, vs[h]] = intra[h] + inter[h]
        s_ref[h] = st[h] * jnp.exp(gl[h]) + upd[h]


def gla_scan(p, w2_pair, gb_pair, *, n_ctx):
    b, t, _ = p.shape
    kwid = GLA_HEADS * GLA_DK
    toks = [(p, kwid, 0, 0), (p, kwid, 1, 1), (p, BRANCH, 1, 1), (p, LANES, 12, 12)]
    return _bidir_scan(_gla_body, toks, [w2_pair, gb_pair], (GLA_HEADS, GLA_DV, GLA_DK), b=b, t=t, n_ctx=n_ctx,
                       batch_block=_batch_block(b, 4))


def _rwkv_prep_kernel(x_ref, xp_ref, xn_ref, mu_ref, w2_ref, w0_ref, a2_ref, a0_ref, g2_ref, kk_ref_w,
                      ka_ref, rk_ref, bd_ref, r_ref, k_ref, v_ref, kk_ref, a_ref, lw_ref, g_ref, bo_ref,
                      *, nct, nt):
    x = x_ref[0]
    xp, xn = _neighbours(x, xp_ref[0], xn_ref[0], nct=nct, nt=nt)
    x = x + mu_ref[...] * (0.5 * (xp + xn) - x)
    r, k, v = x[:, :BRANCH], x[:, BRANCH:2 * BRANCH], x[:, 2 * BRANCH:3 * BRANCH]
    wlr = x[:, 3 * BRANCH:3 * BRANCH + LANES]
    alr = x[:, 3 * BRANCH + LANES:3 * BRANCH + 2 * LANES]
    glr = x[:, 3 * BRANCH + 2 * LANES:]
    w_raw = _dot_x3(jnp.tanh(wlr), w2_ref[...]) + w0_ref[...]
    lw_ref[0] = -jnp.exp(-_softplus(-w_raw) - 0.5)
    a = _sigmoid(_dot_x3(alr, a2_ref[...]) + a0_ref[...])
    a_ref[0] = a
    g_ref[0] = _dot_x3(_sigmoid(glr), g2_ref[...])
    kk = k * kk_ref_w[...]
    kk_ref[0] = kk * lax.rsqrt(_dot_x2(kk * kk, bd_ref[...]) + EPS)
    ksum = k * (2.0 + (a[:, :BRANCH] + a[:, BRANCH:] - 2.0) * ka_ref[...])
    bo_ref[0] = _dot_x2(r * ksum * rk_ref[...], bd_ref[...]) * v
    r_ref[0] = r
    k_ref[0] = k
    v_ref[0] = v


def rwkv_prep(p, lp, *, n_ctx):
    b, t, w = p.shape
    tt = _pick_tile(n_ctx, ROW_TILE)

    def pair(wp):
        r, c = wp.shape[1:]
        return jnp.zeros((LANES, 2 * c), F32).at[:r, :c].set(wp[0]).at[r:2 * r, c:].set(wp[1])

    consts = (_row(lp["rwkv_mu"]), pair(lp["rwkv_w2"]), _row(lp["rwkv_w0"]), pair(lp["rwkv_a2"]),
              _row(lp["rwkv_a0"]), lp["rwkv_g2"], _row(lp["rwkv_k_k"]), _row(lp["rwkv_k_a"]),
              _row(lp["rwkv_r_k"]), _block_diag_ones(RWKV_N))
    specs = _halo_specs(tt, w, 0, t) + [_const_spec(c.shape) for c in consts]
    kern = functools.partial(_rwkv_prep_kernel, nct=(t - n_ctx) // tt, nt=t // tt)
    return _prep_call(kern, (p, p, p) + consts, specs,
                      (BRANCH, BRANCH, BRANCH, BRANCH, 2 * BRANCH, 2 * BRANCH, BRANCH, BRANCH),
                      b=b, t=t, tt=tt)


def _rwkv_body(r_ref, k_ref, v_ref, kk_ref, a_ref, lw_ref, ka_ref, o_ref, s_ref, *, reverse):
    incl, strict = _chunk_masks(reverse)
    last = 0 if reverse else CHUNK - 1
    lw_all = lw_ref[0]
    g_all = _chunk_cumsum(incl, lw_all)
    a_all = a_ref[0]
    k_all = k_ref[0] * (1.0 + (a_all - 1.0) * ka_ref[...])
    yield
    heads = range(RWKV_HEADS)
    hs = [slice(h * RWKV_N, (h + 1) * RWKV_N) for h in heads]
    g = [g_all[:, hs[h]] for h in heads]
    gl = [g[h][last:last + 1, :] for h in heads]
    eneg = [jnp.exp(-g[h]) for h in heads]
    edec = [jnp.exp(gl[h] - g[h]) for h in heads]
    kk = [kk_ref[0, :, hs[h]] for h in heads]
    bvec = [kk[h] * a_all[:, hs[h]] for h in heads]
    k = [k_all[:, hs[h]] for h in heads]
    v = [v_ref[0, :, hs[h]] for h in heads]
    kkg = [kk[h] * jnp.exp(g[h] - lw_all[:, hs[h]]) for h in heads]
    rg = [r_ref[0, :, hs[h]] * jnp.exp(g[h]) for h in heads]
    bh = [bvec[h] * eneg[h] for h in heads]
    kh = [k[h] * eneg[h] for h in heads]
    s = [s_ref[h] for h in heads]
    yield
    both = [jnp.concatenate([kkg[h], rg[h]], axis=0) for h in heads]
    mask2 = jnp.concatenate([strict, incl], axis=0)
    mb = [jnp.where(mask2, _dot_nt(both[h], bh[h]), 0.0) for h in heads]
    yield
    mk = [jnp.where(mask2, _dot_nt(both[h], kh[h]), 0.0) for h in heads]
    yield
    part = [_dot(mk[h], v[h]) + _dot_nt(both[h], s[h]) for h in heads]
    yield
    x = yield from _unit_tri_solve([mb[h][:CHUNK] for h in heads], [part[h][:CHUNK] for h in heads])
    u = [-xh for xh in x]
    for h in heads:
        o_ref[0, :, hs[h]] = part[h][CHUNK:] + _dot(mb[h][CHUNK:], u[h])
    yield
    for h in heads:
        upd = _dot_tn(jnp.concatenate([u[h], v[h]], axis=0),
                      jnp.concatenate([bvec[h] * edec[h], k[h] * edec[h]], axis=0))
        s_ref[h] = s[h] * jnp.exp(gl[h]) + upd


def rwkv_scan(r, k, v, kk, a, lw, k_a, *, n_ctx):
    b, t, _ = r.shape
    toks = [(r, BRANCH, 0, 0), (k, BRANCH, 0, 0), (v, BRANCH, 0, 0), (kk, BRANCH, 0, 0),
            (a, BRANCH, 0, 1), (lw, BRANCH, 0, 1)]
    return _bidir_scan(_rwkv_body, toks, [(k_a, k_a)], (RWKV_HEADS, RWKV_N, RWKV_N), b=b, t=t, n_ctx=n_ctx,
                       batch_block=_batch_block(b, 2))


def _gdn_prep_kernel(x_ref, xp_ref, xn_ref, ab_ref, cw_ref, na_ref, dtb_ref, bd_ref,
                     q_ref, k_ref, v_ref, sm_ref, *, nct, nt):
    x = x_ref[0]
    xp, xn = _neighbours(x, xp_ref[0], xn_ref[0], nct=nct, nt=nt)
    y = _silu(xp * cw_ref[0:1, :] + x * cw_ref[1:2, :] + xn * cw_ref[2:3, :])
    q, k = y[:, :BRANCH], y[:, BRANCH:2 * BRANCH]
    q_ref[0] = q * lax.rsqrt(_dot_x2(q * q, bd_ref[...]) + EPS) * (GDN_N ** -0.5)
    k_ref[0] = k * lax.rsqrt(_dot_x2(k * k, bd_ref[...]) + EPS)
    v_ref[0] = y[:, 2 * BRANCH:]
    ab = ab_ref[0]
    lane = lax.broadcasted_iota(jnp.int32, ab.shape, 1)
    sm_ref[0] = jnp.where(lane < 2 * GDN_HEADS, na_ref[...] * _softplus(ab + dtb_ref[...]), _sigmoid(ab))


def gdn_prep(p, lp, *, n_ctx):
    b, t, _ = p.shape
    tt = _pick_tile(n_ctx, ROW_TILE)
    padrow = lambda v: jnp.pad(v.reshape(1, -1), ((0, 0), (0, LANES - 2 * GDN_HEADS)))
    consts = (lp["gdn_conv_w"], padrow(-jnp.exp(lp["gdn_a_log"])), padrow(lp["gdn_dt_bias"]),
              _block_diag_ones(GDN_N))
    specs = (_halo_specs(tt, 3 * BRANCH, 0, t) + [pl.BlockSpec((1, tt, LANES), lambda bi, i: (bi, i, 16))]
             + [_const_spec(c.shape) for c in consts])
    kern = functools.partial(_gdn_prep_kernel, nct=(t - n_ctx) // tt, nt=t // tt)
    return _prep_call(kern, (p, p, p, p) + consts, specs, (BRANCH, BRANCH, BRANCH, LANES), b=b, t=t, tt=tt)


def _gdn_body(q_ref, k_ref, v_ref, sm_ref, o_ref, s_ref, *, reverse):
    incl, strict = _chunk_masks(reverse)
    last = 0 if reverse else CHUNK - 1
    off = GDN_HEADS if reverse else 0
    sm = sm_ref[0]
    g_all = _chunk_cumsum(incl, sm)
    yield
    gt_all = _transpose_small(g_all)
    yield
    heads = range(GDN_HEADS)
    hs = [slice(h * GDN_N, (h + 1) * GDN_N) for h in heads]
    g = [g_all[:, off + h:off + h + 1] for h in heads]
    gl = [g[h][last:last + 1, :] for h in heads]
    beta = [sm[:, 2 * GDN_HEADS + off + h:2 * GDN_HEADS + off + h + 1] for h in heads]
    q = [q_ref[0, :, hs[h]] for h in heads]
    k = [k_ref[0, :, hs[h]] for h in heads]
    v = [v_ref[0, :, hs[h]] for h in heads]
    s = [s_ref[h] for h in heads]
    decay = [jnp.exp(jnp.where(incl, g[h] - gt_all[off + h:off + h + 1, :], -jnp.inf)) for h in heads]
    yield
    kq = [_dot_nt(jnp.concatenate([k[h], q[h]], axis=0), k[h]) for h in heads]
    yield
    lower = [jnp.where(strict, kq[h][:CHUNK] * decay[h] * beta[h], 0.0) for h in heads]
    attn = [kq[h][CHUNK:] * decay[h] for h in heads]
    o_part = [_dot(q[h] * jnp.exp(g[h]), s[h]) for h in heads]
    yield
    rhs = [jnp.concatenate([v[h] * beta[h], k[h] * (beta[h] * jnp.exp(g[h]))], axis=1) for h in heads]
    sol = yield from _unit_tri_solve(lower, rhs, precise_levels=2)
    v_new = [sol[h][:, :GDN_N] - _dot(sol[h][:, GDN_N:], s[h]) for h in heads]
    yield
    for h in heads:
        o_ref[0, :, hs[h]] = o_part[h] + _dot(attn[h], v_new[h])
    yield
    for h in heads:
        s_ref[h] = s[h] * jnp.exp(gl[h]) + _dot_tn(k[h] * jnp.exp(gl[h] - g[h]), v_new[h])


def gdn_scan(q, k, v, sm, *, n_ctx):
    b, t, _ = q.shape
    toks = [(q, BRANCH, 0, 0), (k, BRANCH, 0, 0), (v, BRANCH, 0, 0), (sm, LANES, 0, 0)]
    return _bidir_scan(_gdn_body, toks, [], (GDN_HEADS, GDN_N, GDN_N), b=b, t=t, n_ctx=n_ctx,
                       batch_block=_batch_block(b, 4))


def _merge_kernel(sf_ref, sb_ref, sx_ref, sz_ref, gf_ref, gb_ref, gr_ref, rf_ref, rb_ref, rg_ref, rbo_ref,
                  df_ref, db_ref, dz_ref, gate_ref, x_ref, m_ref,
                  sd_ref, sn_ref, gn_ref, lnw_ref, lnb_ref, dn_ref, bd64_ref, bd128_ref, bd256_ref,
                  wb_ref, wo_ref, o_ref):
    def group_rms(y, bd_ref, n, w_ref):
        return y * lax.rsqrt(_dot_x2(y * y, bd_ref[...]) * (1.0 / n) + EPS) * w_ref[...]

    y = (sf_ref[0] + sb_ref[0] + sd_ref[...] * sx_ref[0]) * _silu(sz_ref[0])
    ys = group_rms(y, bd256_ref, BRANCH // SSM_GROUPS, sn_ref)
    yg = group_rms(gf_ref[0] + gb_ref[0], bd128_ref, GLA_DV, gn_ref) * _silu(gr_ref[0])
    y = rf_ref[0] + rb_ref[0]
    yc = y - _dot_x2(y, bd64_ref[...]) * (1.0 / RWKV_N)
    var = _dot_x2(yc * yc, bd64_ref[...]) * (1.0 / RWKV_N)
    yr = (yc * lax.rsqrt(var + RWKV_LN_EPS) * lnw_ref[...] + lnb_ref[...] + rbo_ref[0]) * rg_ref[0]
    yd = group_rms(df_ref[0] + db_ref[0], bd128_ref, GDN_N, dn_ref) * _silu(dz_ref[0])
    acc = None
    for i, yi in enumerate((ys, yg, yr, yd)):
        term = gate_ref[0, :, i * D_MODEL:(i + 1) * D_MODEL].astype(F32) * _dot(yi, wb_ref[i])
        acc = term if acc is None else acc + term
    o_ref[0] = x_ref[0] + m_ref[0, 0] * _dot(acc, wo_ref[...])


def merge_residual(ssm, gla, rwkv, gdn, gates, x_all, gate_mod, lp, w_branch, w_out, *, n_ctx):
    b, t, d = x_all.shape
    tm = _pick_tile(n_ctx, ROW_TILE)
    tok = lambda bi, i: (bi, i, 0)
    blk = lambda c: pl.BlockSpec((1, tm, BRANCH), lambda bi, i: (bi, i, c))
    half = blk(0)
    consts = (_row(jnp.repeat(lp["ssm_d"], SSM_P)), _row(lp["ssm_norm"]),
              _row(jnp.tile(lp["gla_norm"], GLA_HEADS)), _row(lp["rwkv_ln_w"]), _row(lp["rwkv_ln_b"]),
              _row(jnp.tile(lp["gdn_norm"], GDN_HEADS)),
              _block_diag_ones(RWKV_N), _block_diag_ones(LANES), _block_diag_ones(BRANCH // SSM_GROUPS),
              w_branch, w_out)
    ins = (ssm[0], ssm[1], ssm[2], ssm[3], gla[0], gla[1], gla[2], rwkv[0], rwkv[1], rwkv[2], rwkv[3],
           gdn[0], gdn[1], gdn[2], gates, x_all, gate_mod) + consts
    specs = ([half, half, half, blk(0), half, half, blk(2), half, half, half, half, half, half, blk(3),
              pl.BlockSpec((1, tm, 4 * d), tok), pl.BlockSpec((1, tm, d), tok),
              pl.BlockSpec((1, 1, 1, d), _mod_sel((t - n_ctx) // tm))]
             + [_const_spec(c.shape) for c in consts])
    return pl.pallas_call(
        _merge_kernel,
        grid=(b, t // tm),
        in_specs=specs,
        out_specs=pl.BlockSpec((1, tm, d), tok),
        out_shape=jax.ShapeDtypeStruct((b, t, d), F32),
        compiler_params=_cparams("parallel", "parallel"),
    )(*ins)


def _route_kernel(x_ref, nw_ref, shift_ref, scale_ref, rw_ref, rb_ref, u_ref, o_ref, cnt_ref, hb_ref, *, rows_kw):
    x = x_ref[...]
    shift, scale = _token_rows([shift_ref, scale_ref], pl.program_id(0), x.shape[0], **rows_kw)
    h = x * lax.rsqrt(jnp.mean(x * x, axis=-1, keepdims=True) + EPS) * nw_ref[...] * (1.0 + scale) + shift
    hb_ref[...] = h.astype(BF16)
    logits = lax.dot_general(rw_ref[...], h, (((1,), (1,)), ((), ())),
                             precision=HI, preferred_element_type=F32)
    scores = _sigmoid(logits)
    sel = scores + rb_ref[...]
    rows = [sel[e:e + 1, :] for e in range(N_EXPERTS)]
    sc = [scores[e:e + 1, :] for e in range(N_EXPERTS)]

    def top2(vals):
        v1, i1 = vals[0], jnp.zeros(vals[0].shape, jnp.int32)
        for j in range(1, len(vals)):
            better = vals[j] > v1
            v1 = jnp.where(better, vals[j], v1)
            i1 = jnp.where(better, j, i1)
        v2 = jnp.where(i1 == 0, vals[1], vals[0])
        i2 = jnp.where(i1 == 0, 1, 0)
        for j in range(1, len(vals)):
            better = (vals[j] > v2) & (i1 != j)
            v2 = jnp.where(better, vals[j], v2)
            i2 = jnp.where(better, j, i2)
        return v1, i1, v2, i2

    gsum = []
    for grp in range(N_GROUPS):
        v1, _, v2, _ = top2(rows[grp * EXPERTS_PER_GROUP:(grp + 1) * EXPERTS_PER_GROUP])
        gsum.append(v1 + v2)
    best, gidx = gsum[0], jnp.zeros(gsum[0].shape, jnp.int32)
    for grp in range(1, N_GROUPS):
        better = gsum[grp] > best
        best = jnp.where(better, gsum[grp], best)
        gidx = jnp.where(better, grp, gidx)
    chosen, chosen_sc = [], []
    for j in range(EXPERTS_PER_GROUP):
        cj, sj = rows[j], sc[j]
        for grp in range(1, N_GROUPS):
            cj = jnp.where(gidx == grp, rows[grp * EXPERTS_PER_GROUP + j], cj)
            sj = jnp.where(gidx == grp, sc[grp * EXPERTS_PER_GROUP + j], sj)
        chosen.append(cj)
        chosen_sc.append(sj)
    _, i1, _, i2 = top2(chosen)
    w1, w2 = jnp.zeros_like(best), jnp.zeros_like(best)
    for j in range(EXPERTS_PER_GROUP):
        w1 = jnp.where(i1 == j, chosen_sc[j], w1)
        w2 = jnp.where(i2 == j, chosen_sc[j], w2)
    tot = w1 + w2
    w1, w2 = w1 / tot, w2 / tot
    tm = scores.shape[1]
    sub = lax.broadcasted_iota(jnp.int32, (SUBLANES, tm), 0)
    ind8 = jnp.zeros((SUBLANES, tm), F32)
    meta = jnp.zeros((SUBLANES, tm), F32)
    for j in range(EXPERTS_PER_GROUP):
        gate_j = jnp.where(i1 == j, w1, 0.0) + jnp.where(i2 == j, w2, 0.0)
        meta = jnp.where(sub == j, gate_j, meta)
    for grp in range(N_GROUPS):
        ind8 = jnp.where((sub == grp) & (gidx == grp), 1.0, ind8)
    before = jnp.dot(ind8.astype(BF16), u_ref[...], preferred_element_type=F32)
    rank = jnp.sum(ind8 * before, axis=0, keepdims=True)
    meta = jnp.where(sub == _META_GROUP, gidx.astype(F32), meta)
    meta = jnp.where(sub == _META_RANK, rank, meta)
    o_ref[...] = meta
    counts = jnp.sum(ind8, axis=1, keepdims=True)
    lane = lax.broadcasted_iota(jnp.int32, (SUBLANES, LANES), 1)
    row = lax.broadcasted_iota(jnp.int32, (SUBLANES, LANES), 0)
    cnt_ref[0] = jnp.broadcast_to(jnp.sum(jnp.where(lane == row, counts, 0.0), axis=0, keepdims=True),
                                  (SUBLANES, LANES)).astype(jnp.int32)


_META_GROUP, _META_RANK = EXPERTS_PER_GROUP, EXPERTS_PER_GROUP + 1


def moe_route(x_all, norm_w, shift_rows, scale_rows, router_w, router_b, *, n_ctx, tm):
    b, t, d = x_all.shape
    m = b * t
    upper = jnp.asarray(np.triu(np.ones((tm, tm), np.float32), 1), BF16)
    return pl.pallas_call(
        functools.partial(_route_kernel, rows_kw=dict(bsz=b, t_all=t, n_ctx=n_ctx)),
        grid=(m // tm,),
        in_specs=[pl.BlockSpec((tm, d), lambda i: (i, 0)),
                  _const_spec((1, d)), _const_spec(shift_rows.shape), _const_spec(scale_rows.shape),
                  pl.BlockSpec((N_EXPERTS, d), lambda i: (0, 0)),
                  pl.BlockSpec((N_EXPERTS, 1), lambda i: (0, 0)),
                  _const_spec((tm, tm))],
        out_specs=[pl.BlockSpec((SUBLANES, tm), lambda i: (0, i)),
                   pl.BlockSpec((1, SUBLANES, LANES), lambda i: (i, 0, 0)),
                   pl.BlockSpec((tm, d), lambda i: (i, 0))],
        out_shape=[jax.ShapeDtypeStruct((SUBLANES, m), F32),
                   jax.ShapeDtypeStruct((m // tm, SUBLANES, LANES), jnp.int32),
                   jax.ShapeDtypeStruct((m, d), BF16)],
        compiler_params=_cparams("parallel"),
    )(x_all.reshape(m, d), norm_w.reshape(1, d), shift_rows, scale_rows,
      router_w.T, router_b.reshape(N_EXPERTS, 1), upper)


MOE_TILE = 1024
MOE_SUB_ROWS = 256
MOE_TAIL_ROWS = 128


def _expert_kernel(cnt_ref, h_ref, mr_ref, mc_ref, wg_ref, wu_ref, wd_ref, x_ref, gate_ref, o_ref, *, rows_kw):
    i, grp = pl.program_id(0), pl.program_id(1)

    @pl.when(grp == 0)
    def _():
        o_ref[...] = jnp.zeros_like(o_ref)

    tm = h_ref.shape[0]
    count = cnt_ref[i * N_GROUPS + grp]
    grp_f = grp.astype(F32)
    sel_row = jnp.where(mr_ref[_META_GROUP:_META_GROUP + 1, :] == grp_f, mr_ref[_META_RANK:_META_RANK + 1, :], -1.0)
    sel_col = jnp.where(mc_ref[:, _META_GROUP:_META_GROUP + 1] == grp_f, mc_ref[:, _META_RANK:_META_RANK + 1], -1.0)
    gate_parts = _split3(mc_ref[...])

    def sub_block(first, rows):
        base = first.astype(F32)
        slot_r = lax.broadcasted_iota(jnp.int32, (rows, tm), 0).astype(F32)
        slot_c = lax.broadcasted_iota(jnp.int32, (tm, rows), 1).astype(F32)
        pick = (sel_row - base == slot_r).astype(BF16)
        put = (sel_col - base == slot_c).astype(BF16)
        xg = jnp.dot(pick, h_ref[...], preferred_element_type=F32).astype(BF16)
        gates = sum(jnp.dot(pick, p, preferred_element_type=F32) for p in gate_parts)
        y = jnp.zeros((rows, o_ref.shape[1]), F32)
        for e in range(EXPERTS_PER_GROUP):
            hid = _silu(_dot(xg, wg_ref[e])) * _dot(xg, wu_ref[e])
            y = y + _dot(gates[:, e:e + 1] * hid, wd_ref[e])
        yh = y.astype(BF16)
        yl = (y - yh.astype(F32)).astype(BF16)
        o_ref[...] += (jnp.dot(put, yh, preferred_element_type=F32)
                       + jnp.dot(put, yl, preferred_element_type=F32))

    n_full = count // MOE_SUB_ROWS
    rem = count - n_full * MOE_SUB_ROWS
    n_main = n_full + (rem > MOE_TAIL_ROWS).astype(jnp.int32)

    def main_block(s, carry):
        sub_block(s * MOE_SUB_ROWS, MOE_SUB_ROWS)
        return carry

    lax.fori_loop(0, n_main, main_block, 0)

    @pl.when((rem > 0) & (rem <= MOE_TAIL_ROWS))
    def _():
        sub_block(n_full * MOE_SUB_ROWS, MOE_TAIL_ROWS)

    @pl.when(grp == N_GROUPS - 1)
    def _():
        (gate,) = _token_rows([gate_ref], i, tm, **rows_kw)
        o_ref[...] = x_ref[...] + gate * o_ref[...]


def moe_experts(hb, meta, counts, wg, wu, wd, x_all, gate_rows, *, n_ctx, tm):
    b, t, d = x_all.shape
    m = b * t
    tok = lambda i, g, cnt: (i, 0)
    grid_spec = pltpu.PrefetchScalarGridSpec(
        num_scalar_prefetch=1,
        grid=(m // tm, N_GROUPS),
        in_specs=[pl.BlockSpec((tm, d), tok),
                  pl.BlockSpec((SUBLANES, tm), lambda i, g, cnt: (0, i)),
                  pl.BlockSpec((tm, SUBLANES), tok),
                  pl.BlockSpec((EXPERTS_PER_GROUP, d, EXPERT_FF), lambda i, g, cnt: (g, 0, 0)),
                  pl.BlockSpec((EXPERTS_PER_GROUP, d, EXPERT_FF), lambda i, g, cnt: (g, 0, 0)),
                  pl.BlockSpec((EXPERTS_PER_GROUP, EXPERT_FF, d), lambda i, g, cnt: (g, 0, 0)),
                  pl.BlockSpec((tm, d), tok),
                  pl.BlockSpec(gate_rows.shape, lambda i, g, cnt: (0, 0))],
        out_specs=pl.BlockSpec((tm, d), tok))
    out = pl.pallas_call(
        functools.partial(_expert_kernel, rows_kw=dict(bsz=b, t_all=t, n_ctx=n_ctx)),
        grid_spec=grid_spec,
        out_shape=jax.ShapeDtypeStruct((m, d), F32),
        compiler_params=pltpu.CompilerParams(dimension_semantics=("parallel", "arbitrary"),
                                             vmem_limit_bytes=MOE_VMEM_LIMIT),
    )(counts[:, 0, :N_GROUPS].reshape(-1), hb, meta, meta.T, wg, wu, wd, x_all.reshape(m, d), gate_rows)
    return out.reshape(b, t, d)


def _token_rows(m_refs, tile, tm, *, bsz, t_all, n_ctx):
    row = tile * tm + lax.broadcasted_iota(jnp.int32, (tm, 1), 0)
    ctx = jnp.zeros((tm, 1), jnp.bool_)
    lat = []
    for bi in range(bsz):
        lo, split = bi * t_all, bi * t_all + t_all - n_ctx
        lat.append((row >= lo) & (row < split))
        ctx = ctx | ((row >= split) & (row < lo + t_all))
    out = []
    for m_ref in m_refs:
        v = jnp.where(ctx, m_ref[bsz:bsz + 1, :], 0.0)
        for bi in range(bsz):
            v = v + jnp.where(lat[bi], m_ref[bi:bi + 1, :], 0.0)
        out.append(v)
    return out


GRID_TILE_COLS = SUBLANES
GRID_TILE = GRID_W * GRID_TILE_COLS


def _grid_view(x_all, n_lat):
    b, t, d = x_all.shape
    rows = n_lat // GRID_W
    assert rows == GRID_W and t % GRID_W == 0
    last = GRID_W // GRID_TILE_COLS - 1
    spec = pl.BlockSpec((1, rows, GRID_TILE_COLS, d), lambda bi, i: (bi, 0, jnp.minimum(i, last), 0))
    return x_all.reshape(b, t // GRID_W, GRID_W, d), spec


def _grid_tile(xg_ref):
    return jnp.concatenate([xg_ref[0, :, j, :] for j in range(xg_ref.shape[2])], axis=0)


def _transpose_grid_kernel(xn_ref, xg_ref, o_ref, *, n_lat_tiles):
    i = pl.program_id(1)

    @pl.when(i < n_lat_tiles)
    def _():
        o_ref[0] = _grid_tile(xg_ref)

    @pl.when(i >= n_lat_tiles)
    def _():
        o_ref[0] = xn_ref[0]


def transpose_grid(x_all, *, n_ctx):
    b, t, d = x_all.shape
    xg, gspec = _grid_view(x_all, t - n_ctx)
    tok = pl.BlockSpec((1, GRID_TILE, d), lambda bi, i: (bi, i, 0))
    return pl.pallas_call(
        functools.partial(_transpose_grid_kernel, n_lat_tiles=(t - n_ctx) // GRID_TILE),
        grid=(b, pl.cdiv(t, GRID_TILE)),
        in_specs=[tok, gspec],
        out_specs=tok,
        out_shape=jax.ShapeDtypeStruct((b, t, d), x_all.dtype),
        compiler_params=_cparams("parallel", "parallel"),
    )(x_all, xg)


def _final_norm_kernel(x_ref, w_ref, o_ref, *, from_grid):
    x = _grid_tile(x_ref) if from_grid else x_ref[0]
    o_ref[0] = x * lax.rsqrt(jnp.mean(x * x, axis=-1, keepdims=True) + EPS) * w_ref[...]


def final_rms_norm(x_all, w, *, n_ctx, from_grid):
    b, t, d = x_all.shape
    n_lat = t - n_ctx
    tt = GRID_TILE
    tok = pl.BlockSpec((1, tt, d), lambda bi, i: (bi, i, 0))
    if from_grid:
        x_in, spec = _grid_view(x_all, n_lat)
    else:
        x_in, spec = x_all, tok
    return pl.pallas_call(
        functools.partial(_final_norm_kernel, from_grid=from_grid),
        grid=(b, n_lat // tt),
        in_specs=[spec, _const_spec((1, d))],
        out_specs=tok,
        out_shape=jax.ShapeDtypeStruct((b, n_lat, d), F32),
        compiler_params=_cparams("parallel", "parallel"),
    )(x_in, w.reshape(1, d))


def _pack_w_in(w_in, mixer):
    cols = _SRC_COLS[mixer]
    pieces, i = [], 0
    while i < len(cols):
        j = i
        if cols[i] < 0:
            while j < len(cols) and cols[j] < 0:
                j += 1
            pieces.append(jnp.zeros((w_in.shape[0], j - i), w_in.dtype))
        else:
            while j < len(cols) and cols[j] == cols[i] + (j - i):
                j += 1
            pieces.append(w_in[:, int(cols[i]):int(cols[i]) + (j - i)])
        i = j
    return jnp.concatenate(pieces, axis=1).astype(BF16)


def mixer_scans(ps, lp, *, n_ctx):
    p_ssm, p_gla, p_rwkv, p_gdn = ps

    xs, bc, sm = ssm_prep(p_ssm, lp, n_ctx=n_ctx)
    neg_a = jnp.pad(-jnp.exp(lp["ssm_a_log"]).reshape(1, -1), ((0, 0), (0, LANES - 2 * SSM_HEADS)))
    ssm = tuple(ssd_scan(xs, bc, sm, neg_a, n_ctx=n_ctx)) + (xs, p_ssm)

    w2 = [jnp.zeros((LANES, GLA_HEADS * GLA_DK), F32).at[d * GLA_RANK:(d + 1) * GLA_RANK].set(lp["gla_w2"][d])
          for d in range(2)]
    gb = [_row(lp["gla_b"][d]) for d in range(2)]
    gla = tuple(gla_scan(p_gla, w2, gb, n_ctx=n_ctx)) + (p_gla,)

    r, k, v, kk, a, lw, g, bonus = rwkv_prep(p_rwkv, lp, n_ctx=n_ctx)
    rwkv = tuple(rwkv_scan(r, k, v, kk, a, lw, _row(lp["rwkv_k_a"]), n_ctx=n_ctx)) + (g, bonus)

    q, kd, vd, smd = gdn_prep(p_gdn, lp, n_ctx=n_ctx)
    gdn = tuple(gdn_scan(q, kd, vd, smd, n_ctx=n_ctx)) + (p_gdn,)
    return ssm, gla, rwkv, gdn


def kernel(x, c, ctx, c_ctx, ada_w, ada_b, norm_mix, norm_ffn, w_in, w_gate, w_branch, w_out, ssm_conv_w, ssm_conv_b, ssm_a_log, ssm_dt_bias, ssm_d, ssm_norm, gla_w2, gla_b, gla_norm, rwkv_mu, rwkv_w0, rwkv_w2, rwkv_a0, rwkv_a2, rwkv_g2, rwkv_k_k, rwkv_k_a, rwkv_r_k, rwkv_ln_w, rwkv_ln_b, gdn_conv_w, gdn_a_log, gdn_dt_bias, gdn_norm, router_w, router_b, moe_w_gate, moe_w_up, moe_w_down, final_norm):
    bsz, seq, d = x.shape
    n_ctx = ctx.shape[1]
    t_all = n_ctx + seq
    m_all = bsz * t_all

    cond = jnp.concatenate([jax.nn.silu(c), jax.nn.silu(c_ctx)[None]], 0)
    cond = jnp.pad(cond, ((0, SUBLANES - cond.shape[0]), (0, 0)))
    mods, mod_rows = [], []
    for l in range(DEPTH):
        mod = pmatmul(cond, ada_w[l], tm=SUBLANES, tn=1024, precise=True) + ada_b[l]
        mod_rows.append(mod)
        lat = mod[:bsz].reshape(bsz, 6, d)
        cx = jnp.broadcast_to(mod[bsz].reshape(1, 6, d), (bsz, 6, d))
        mods.append(jnp.stack([cx, lat], axis=1))

    x_all = jnp.concatenate([x, ctx], axis=1)
    scan_order = False
    for l in range(DEPTH):
        if (l % 2 == 1) != scan_order:
            x_all = transpose_grid(x_all, n_ctx=n_ctx)
            scan_order = not scan_order
        lp = dict(ssm_conv_w=ssm_conv_w[l], ssm_conv_b=ssm_conv_b[l], ssm_a_log=ssm_a_log[l],
                  ssm_dt_bias=ssm_dt_bias[l], ssm_d=ssm_d[l], ssm_norm=ssm_norm[l],
                  gla_w2=gla_w2[l], gla_b=gla_b[l], gla_norm=gla_norm[l],
                  rwkv_mu=rwkv_mu[l], rwkv_w0=rwkv_w0[l], rwkv_w2=rwkv_w2[l], rwkv_a0=rwkv_a0[l],
                  rwkv_a2=rwkv_a2[l], rwkv_g2=rwkv_g2[l], rwkv_k_k=rwkv_k_k[l], rwkv_k_a=rwkv_k_a[l],
                  rwkv_r_k=rwkv_r_k[l], rwkv_ln_w=rwkv_ln_w[l], rwkv_ln_b=rwkv_ln_b[l],
                  gdn_conv_w=gdn_conv_w[l], gdn_a_log=gdn_a_log[l], gdn_dt_bias=gdn_dt_bias[l],
                  gdn_norm=gdn_norm[l])
        mod = mods[l]
        msel = lambda i: mod[:, :, i][:, :, None, :]

        h = norm_modulate(x_all, norm_mix[l], msel(0), msel(1), n_ctx=n_ctx)
        h2d = h.reshape(m_all, d)
        ps = []
        for mixer in ("ssm", "gla", "rwkv", "gdn"):
            wp = _pack_w_in(w_in[l], mixer)
            ps.append(pmatmul(h2d, wp, tm=512, tn=wp.shape[1]).reshape(bsz, t_all, wp.shape[1]))
        wg_cat = jnp.concatenate([w_gate[l, i] for i in range(4)], axis=1).astype(BF16)
        gates = pmatmul(h2d, wg_cat, tm=1024, tn=1024, act="sigmoid", out_dtype=BF16)
        gates = gates.reshape(bsz, t_all, 4 * d)

        ssm, gla, rwkv, gdn = mixer_scans(ps, lp, n_ctx=n_ctx)
        x_all = merge_residual(ssm, gla, rwkv, gdn, gates, x_all, msel(2), lp,
                               w_branch[l].astype(BF16), w_out[l].astype(BF16), n_ctx=n_ctx)

        tm_moe = _pick_tile(m_all, MOE_TILE)
        rows = lambda i: mod_rows[l][:, i * d:(i + 1) * d]
        meta, counts, hb = moe_route(x_all, norm_ffn[l], rows(3), rows(4), router_w, router_b,
                                     n_ctx=n_ctx, tm=tm_moe)
        x_all = moe_experts(hb, meta, counts, moe_w_gate[l].astype(BF16), moe_w_up[l].astype(BF16),
                            moe_w_down[l].astype(BF16), x_all, rows(5), n_ctx=n_ctx, tm=tm_moe)

    return final_rms_norm(x_all, final_norm, n_ctx=n_ctx, from_grid=scan_order)
```

```python
import functools
import itertools

import numpy as np
import jax
import jax.numpy as jnp
from jax import lax
from jax.experimental import pallas as pl
from jax.experimental.pallas import tpu as pltpu

F32 = jnp.float32
BF16 = jnp.bfloat16
HI = lax.Precision.HIGHEST

D_MODEL = 1024
DEPTH = 2
GRID_W = 64
CHUNK = 64
EPS = 1e-6
BRANCH = D_MODEL // 2
SSM_HEADS, SSM_P, SSM_GROUPS, SSM_N = 8, 64, 2, 64
GLA_HEADS, GLA_DK, GLA_DV, GLA_RANK, GLA_TAU = 4, 64, 128, 16, 16.0
RWKV_HEADS, RWKV_N, RWKV_LN_EPS = 8, 64, 64e-5
GDN_HEADS, GDN_N = 4, 128
N_EXPERTS, N_GROUPS, EXPERTS_PER_GROUP = 16, 4, 4
EXPERT_FF = D_MODEL // 2
LANES = 128
SUBLANES = 8
VMEM_LIMIT = 48 * 1024 * 1024
MOE_VMEM_LIMIT = 56 * 1024 * 1024
ROW_TILE = 256

_REF_BLOCKS = (
    ("ssm", "z", 512), ("ssm", "xbc", 768), ("ssm", "dt", 16),
    ("gla", "q", 256), ("gla", "k", 256), ("gla", "v", 512), ("gla", "r", 512), ("gla", "glr", 32),
    ("rwkv", "all", 1920),
    ("gdn", "qkv", 1536), ("gdn", "z", 512), ("gdn", "ab", 16),
)
_PACKED = {
    "ssm": (("z", 512), ("dt", 128), ("pad", 128), ("xbc", 768)),
    "gla": (("q", 256), ("k", 256), ("v", 512), ("r", 512), ("glr", 128)),
    "rwkv": (("all", 1920),),
    "gdn": (("qkv", 1536), ("z", 512), ("ab", 128)),
}


def _packed_columns():
    start, s = {}, 0
    for mixer, blk, w in _REF_BLOCKS:
        start[(mixer, blk)] = (s, w)
        s += w
    out = {}
    for mixer, blocks in _PACKED.items():
        cols = []
        for blk, wp in blocks:
            s0, w = start.get((mixer, blk), (0, 0))
            cols += list(range(s0, s0 + w)) + [-1] * (wp - w)
        out[mixer] = np.asarray(cols, np.int32)
    return out


_SRC_COLS = _packed_columns()


def _cparams(*sem):
    return pltpu.CompilerParams(dimension_semantics=sem, vmem_limit_bytes=VMEM_LIMIT)


def _dot(a, b):
    return jnp.dot(a.astype(BF16), b.astype(BF16), preferred_element_type=F32)


def _dot_nt(a, b):
    return lax.dot_general(a.astype(BF16), b.astype(BF16), (((1,), (1,)), ((), ())),
                           preferred_element_type=F32)


def _dot_tn(a, b):
    return lax.dot_general(a.astype(BF16), b.astype(BF16), (((0,), (0,)), ((), ())),
                           preferred_element_type=F32)


def _dot_hi(a, b):
    return jnp.dot(a, b, precision=HI, preferred_element_type=F32)


def _dot_x3(a, b):
    ah = a.astype(BF16)
    al = (a - ah.astype(F32)).astype(BF16)
    bh = b.astype(BF16)
    bl = (b - bh.astype(F32)).astype(BF16)
    f = lambda u, v: jnp.dot(u, v, preferred_element_type=F32)
    return f(ah, bh) + (f(ah, bl) + f(al, bh))


def _dot_x2(a, w):
    ah = a.astype(BF16)
    al = (a - ah.astype(F32)).astype(BF16)
    return jnp.dot(ah, w, preferred_element_type=F32) + jnp.dot(al, w, preferred_element_type=F32)


def _softplus(x):
    return jnp.maximum(x, 0.0) + jnp.log(1.0 + jnp.exp(-jnp.abs(x)))


def _sigmoid(x):
    return 1.0 / (1.0 + jnp.exp(-x))


def _silu(x):
    return x * _sigmoid(x)


def _pick_tile(m, pref):
    t = pref
    while m % t:
        t //= 2
    return t


def _group_sums(y, n):
    m = min(n, LANES)
    row = lax.broadcasted_iota(jnp.int32, (LANES, LANES), 0)
    col = lax.broadcasted_iota(jnp.int32, (LANES, LANES), 1)
    same = ((row // m) == (col // m)).astype(BF16)
    parts = [_dot_x2(y[:, j:j + LANES], same) for j in range(0, y.shape[1], LANES)]
    k = n // m
    if k > 1:
        parts = [sum(parts[g * k:(g + 1) * k]) for g in range(len(parts) // k) for _ in range(k)]
    return jnp.concatenate(parts, axis=1)


def _mm_kernel(a_ref, w_ref, o_ref, *, act, precise):
    if precise:
        r = _dot_hi(a_ref[...].astype(F32), w_ref[...].astype(F32))
    else:
        r = _dot(a_ref[...], w_ref[...])
    if act == "sigmoid":
        r = _sigmoid(r)
    o_ref[...] = r.astype(o_ref.dtype)


def pmatmul(a, w, *, tm, tn, act=None, precise=False, out_dtype=F32):
    m, k = a.shape
    n = w.shape[1]
    tm = _pick_tile(m, tm)
    assert tm % SUBLANES == 0 and n % tn == 0, (m, tm, n, tn)
    return pl.pallas_call(
        functools.partial(_mm_kernel, act=act, precise=precise),
        grid=(n // tn, m // tm),
        in_specs=[pl.BlockSpec((tm, k), lambda j, i: (i, 0)),
                  pl.BlockSpec((k, tn), lambda j, i: (0, j))],
        out_specs=pl.BlockSpec((tm, tn), lambda j, i: (i, j)),
        out_shape=jax.ShapeDtypeStruct((m, n), out_dtype),
        compiler_params=_cparams("parallel", "parallel"),
    )(a, w)


def _norm_mod_kernel(x_ref, w_ref, shift_ref, scale_ref, o_ref):
    x = x_ref[0]
    y = x * lax.rsqrt(jnp.mean(x * x, axis=-1, keepdims=True) + EPS) * w_ref[...]
    o_ref[0] = (y * (1.0 + scale_ref[0, 0]) + shift_ref[0, 0]).astype(o_ref.dtype)


def _mod_sel(n_lat_tiles):
    return lambda bi, i, *_: (bi, jnp.where(i < n_lat_tiles, 1, 0), 0, 0)


def norm_modulate(x_all, w, shift, scale, *, n_ctx, out_dtype=BF16):
    b, t, d = x_all.shape
    tm = _pick_tile(n_ctx, ROW_TILE)
    assert t % tm == 0
    tok = lambda bi, i: (bi, i, 0)
    return pl.pallas_call(
        _norm_mod_kernel,
        grid=(b, t // tm),
        in_specs=[pl.BlockSpec((1, tm, d), tok),
                  pl.BlockSpec((1, d), lambda bi, i: (0, 0)),
                  pl.BlockSpec((1, 1, 1, d), _mod_sel((t - n_ctx) // tm)),
                  pl.BlockSpec((1, 1, 1, d), _mod_sel((t - n_ctx) // tm))],
        out_specs=pl.BlockSpec((1, tm, d), tok),
        out_shape=jax.ShapeDtypeStruct((b, t, d), out_dtype),
        compiler_params=_cparams("parallel", "parallel"),
    )(x_all, w.reshape(1, d), shift, scale)


def _row(v):
    return v.reshape(1, -1).astype(F32)


def _const_spec(shape):
    return pl.BlockSpec(shape, lambda *_: (0,) * len(shape))


def _tile_specs(tt, width, col):
    r8 = tt // SUBLANES
    main = pl.BlockSpec((1, tt, width), lambda bi, i: (bi, i, col))
    prev = pl.BlockSpec((1, SUBLANES, width), lambda bi, i: (bi, jnp.maximum(i * r8 - 1, 0), col))
    return main, prev, r8


def _halo_specs(tt, width, col, t):
    main, prev, r8 = _tile_specs(tt, width, col)
    last8 = t // SUBLANES - 1
    nxt = pl.BlockSpec((1, SUBLANES, width), lambda bi, i: (bi, jnp.minimum((i + 1) * r8, last8), col))
    return [main, prev, nxt]


def _neighbours(x, prev8, next8, *, nct, nt):
    i = pl.program_id(1)
    tt = x.shape[0]
    row = lax.broadcasted_iota(jnp.int32, x.shape, 0)
    first = (i == 0) | (i == nct)
    last = (i == nct - 1) | (i == nt - 1)
    pr = jnp.where(first, 0.0, prev8[SUBLANES - 1:SUBLANES, :])
    nx = jnp.where(last, 0.0, next8[0:1, :])
    xp = jnp.where(row == 0, pr, pltpu.roll(x, 1, 0))
    xn = jnp.where(row == tt - 1, nx, pltpu.roll(x, tt - 1, 0))
    return xp, xn


def _prep_call(kernel, ins, in_specs, out_widths, *, b, t, tt, out_dtype=F32):
    tok = lambda bi, i: (bi, i, 0)
    return pl.pallas_call(
        kernel,
        grid=(b, t // tt),
        in_specs=in_specs,
        out_specs=[pl.BlockSpec((1, tt, w), tok) for w in out_widths],
        out_shape=[jax.ShapeDtypeStruct((b, t, w), out_dtype) for w in out_widths],
        compiler_params=_cparams("parallel", "parallel"),
    )(*ins)


def _chunk_masks(reverse):
    row = lax.broadcasted_iota(jnp.int32, (CHUNK, CHUNK), 0)
    col = lax.broadcasted_iota(jnp.int32, (CHUNK, CHUNK), 1)
    if reverse:
        return col >= row, col > row
    return col <= row, col < row


def _chunk_order(i, n_ctx_chunks, n_chunks, reverse):
    n_lat_chunks = n_chunks - n_ctx_chunks
    if not reverse:
        return jnp.where(i < n_ctx_chunks, n_lat_chunks + i, i - n_ctx_chunks)
    return jnp.where(i < n_ctx_chunks, n_chunks - 1 - i, n_lat_chunks - 1 - (i - n_ctx_chunks))


def _split3(a):
    hi = a.astype(BF16)
    r = a - hi.astype(F32)
    mid = r.astype(BF16)
    return hi, mid, (r - mid.astype(F32)).astype(BF16)


def _transpose_small(x):
    row = lax.broadcasted_iota(jnp.int32, (LANES, LANES), 0)
    col = lax.broadcasted_iota(jnp.int32, (LANES, LANES), 1)
    eye = (row == col).astype(BF16)
    nt = lambda p: lax.dot_general(eye, p, (((1,), (1,)), ((), ())), preferred_element_type=F32)
    hi, mid, lo = _split3(x)
    return nt(hi) + (nt(mid) + nt(lo))


def _chunk_cumsum(incl, x):
    m = incl.astype(BF16)
    hi, mid, lo = _split3(x)
    f = lambda p: jnp.dot(m, p, preferred_element_type=F32)
    return f(hi) + (f(mid) + f(lo))


def _select_columns(x, sel):
    c = x.shape[0]
    y = jnp.dot(jnp.concatenate(_split3(x), axis=0), sel, preferred_element_type=F32)
    return y[:c] + (y[c:2 * c] + y[2 * c:])


def _unit_tri_solve(mats, rhs, precise_levels=0):
    n = range(len(mats))
    x = [rhs[h] - _dot_x3(mats[h], rhs[h]) for h in n]
    yield
    p = mats
    for level in range(int(np.log2(CHUNK)) - 1):
        dot = _dot_x3 if level < precise_levels else _dot
        p = [dot(p[h], p[h]) for h in n]
        yield
        x = [x[h] + dot(p[h], x[h]) for h in n]
        yield
    return x


def _bidir_scan(body, tok_ins, const_ins, state_shape, *, b, t, n_ctx, lockstep=True, batch_block=1):
    nc, ncc = t // CHUNK, n_ctx // CHUNK
    nb = batch_block
    assert b % nb == 0

    def chunk_spec(width, col, reverse):
        return pl.BlockSpec((nb, CHUNK, width), lambda bi, i: (bi, _chunk_order(i, ncc, nc, reverse), col))

    def direction(reverse):
        d = int(reverse)
        specs = [chunk_spec(w, cols[d], reverse) for _, w, *cols in tok_ins]
        specs += [_const_spec(pair[d].shape) for pair in const_ins]
        return specs, [a for a, *_ in tok_ins] + [pair[d] for pair in const_ins]

    (spec_f, arg_f), (spec_b, arg_b) = direction(False), direction(True)
    n_tok, n_in = len(tok_ins), len(arg_f)

    def kern(*refs):
        o_f, o_b, s_f, s_b = refs[2 * n_in:]

        @pl.when(pl.program_id(1) == 0)
        def _():
            s_f[...] = jnp.zeros_like(s_f)
            s_b[...] = jnp.zeros_like(s_b)

        def one(j, ins, o_ref, s_ref, reverse):
            ins = [r.at[pl.ds(j, 1)] if k < n_tok else r for k, r in enumerate(ins)]
            return body(*ins, o_ref.at[pl.ds(j, 1)], s_ref.at[j], reverse=reverse)

        gens = []
        for j in range(nb):
            gens += [one(j, refs[:n_in], o_f, s_f, False), one(j, refs[n_in:2 * n_in], o_b, s_b, True)]
        if not lockstep:
            gens = [itertools.chain(*gens)]
        while gens:
            gens = [g for g in gens if next(g, _DONE) is not _DONE]

    return pl.pallas_call(
        kern,
        grid=(b // nb, nc),
        in_specs=spec_f + spec_b,
        out_specs=[chunk_spec(BRANCH, 0, False), chunk_spec(BRANCH, 0, True)],
        out_shape=[jax.ShapeDtypeStruct((b, t, BRANCH), F32)] * 2,
        scratch_shapes=[pltpu.VMEM((nb,) + tuple(state_shape), F32)] * 2,
        compiler_params=_cparams("parallel", "arbitrary"),
    )(*arg_f, *arg_b)


_DONE = object()


def _batch_block(b, pref):
    return pref if b % pref == 0 else 1


def _ssm_prep_kernel(x_ref, xp_ref, xn_ref, dt_ref, cw_ref, cb_ref, dtb_ref, xs_ref, bc_ref, sm_ref,
                     *, nct, nt):
    x = x_ref[0]
    xp, xn = _neighbours(x, xp_ref[0], xn_ref[0], nct=nct, nt=nt)
    y = _silu(xp * cw_ref[0:1, :] + x * cw_ref[1:2, :] + xn * cw_ref[2:3, :] + cb_ref[...])
    xs_ref[0] = y[:, :BRANCH]
    bc_ref[0] = y[:, BRANCH:]
    sm_ref[0] = _softplus(dt_ref[0] + dtb_ref[...])


def ssm_prep(p, lp, *, n_ctx):
    b, t, _ = p.shape
    tt = _pick_tile(n_ctx, ROW_TILE)
    dtb = jnp.pad(lp["ssm_dt_bias"].reshape(1, -1), ((0, 0), (0, LANES - 2 * SSM_HEADS)))
    specs = _halo_specs(tt, 768, 1, t) + [pl.BlockSpec((1, tt, LANES), lambda bi, i: (bi, i, 4)),
                                          _const_spec((3, 768)), _const_spec((1, 768)), _const_spec((1, LANES))]
    kern = functools.partial(_ssm_prep_kernel, nct=(t - n_ctx) // tt, nt=t // tt)
    return _prep_call(kern, (p, p, p, p, lp["ssm_conv_w"], _row(lp["ssm_conv_b"]), dtb), specs,
                      (BRANCH, 2 * SSM_GROUPS * SSM_N, LANES), b=b, t=t, tt=tt)


def _ssd_body(x_ref, bc_ref, sm_ref, na_ref, o_ref, s_ref, *, reverse):
    incl, _ = _chunk_masks(reverse)
    last = 0 if reverse else CHUNK - 1
    off = SSM_HEADS if reverse else 0
    dt_all = sm_ref[0]
    g_all = _chunk_cumsum(incl, dt_all * na_ref[...])
    yield
    expand = _expand_matrix(off, SSM_HEADS, SSM_P)
    gx = _select_columns(g_all, expand)
    dx = _select_columns(dt_all, expand)
    gt_all = _transpose_small(g_all)
    dtt_all = _transpose_small(dt_all)
    yield
    heads = range(SSM_HEADS)
    rep = SSM_HEADS // SSM_GROUPS
    gw = SSM_GROUPS * SSM_N
    hs = [slice(h * SSM_P, (h + 1) * SSM_P) for h in heads]
    glx = gx[last:last + 1, :]
    egx = jnp.exp(gx)
    wx = dx * jnp.exp(glx - gx)
    eglx = jnp.exp(glx)
    x = x_ref[0]
    bm = [bc_ref[0, :, grp * SSM_N:(grp + 1) * SSM_N] for grp in range(SSM_GROUPS)]
    cm = [bc_ref[0, :, gw + grp * SSM_N:gw + (grp + 1) * SSM_N] for grp in range(SSM_GROUPS)]
    cb = [_dot_nt(cm[grp], bm[grp]) for grp in range(SSM_GROUPS)]
    s = [s_ref[grp] for grp in range(SSM_GROUPS)]
    yield
    scores = [cb[h // rep] * jnp.exp(jnp.where(incl, gx[:, hs[h]] - gt_all[off + h:off + h + 1, :], -jnp.inf))
              * dtt_all[off + h:off + h + 1, :] for h in heads]
    yield
    intra = [_dot(scores[h], x[:, hs[h]]) for h in heads]
    yield
    inter = [_dot(cm[grp], s[grp]) for grp in range(SSM_GROUPS)]
    yield
    upd = [_dot_tn(bm[h // rep] * wx[:, hs[h]], x[:, hs[h]]) for h in heads]
    yield
    for h in heads:
        grp, ls = h // rep, slice((h % rep) * SSM_P, (h % rep + 1) * SSM_P)
        o_ref[0, :, hs[h]] = intra[h] + egx[:, hs[h]] * inter[grp][:, ls]
        s_ref[grp, :, ls] = s[grp][:, ls] * eglx[:, hs[h]] + upd[h]


def _expand_matrix(off, n_heads, width):
    row = lax.broadcasted_iota(jnp.int32, (LANES, n_heads * width), 0)
    col = lax.broadcasted_iota(jnp.int32, (LANES, n_heads * width), 1)
    lo = row * width - off * width
    return ((col >= lo) & (col < lo + width)).astype(BF16)


def ssd_scan(xs, bc, sm, neg_a, *, n_ctx):
    b, t, _ = xs.shape
    toks = [(xs, BRANCH, 0, 0), (bc, 2 * SSM_GROUPS * SSM_N, 0, 0), (sm, LANES, 0, 0)]
    state = (SSM_GROUPS, SSM_N, (SSM_HEADS // SSM_GROUPS) * SSM_P)
    return _bidir_scan(_ssd_body, toks, [(neg_a, neg_a)], state, b=b, t=t, n_ctx=n_ctx,
                       batch_block=_batch_block(b, 2))


def _gla_body(q_ref, k_ref, v_ref, glr_ref, w2_ref, gb_ref, o_ref, s_ref, *, reverse):
    incl, _ = _chunk_masks(reverse)
    last = 0 if reverse else CHUNK - 1
    logit = _dot_x3(glr_ref[0], w2_ref[...]) + gb_ref[...]
    yield
    la = -_softplus(-logit) * (1.0 / GLA_TAU)
    g_all = _chunk_cumsum(incl, la)
    yield
    heads = range(GLA_HEADS)
    ks = [slice(h * GLA_DK, (h + 1) * GLA_DK) for h in heads]
    vs = [slice(h * GLA_DV, (h + 1) * GLA_DV) for h in heads]
    g = [g_all[:, ks[h]] for h in heads]
    gl = [g[h][last:last + 1, :] for h in heads]
    k = [k_ref[0, :, ks[h]] for h in heads]
    v = [v_ref[0, :, vs[h]] for h in heads]
    qg = [q_ref[0, :, ks[h]] * (GLA_DK ** -0.5) * jnp.exp(g[h]) for h in heads]
    st = [s_ref[h] for h in heads]
    yield
    scores = [jnp.where(incl, _dot_nt(qg[h], k[h] * jnp.exp(-g[h])), 0.0) for h in heads]
    yield
    intra = [_dot(scores[h], v[h]) for h in heads]
    yield
    inter = [_dot_nt(qg[h], st[h]) for h in heads]
    yield
    upd = [_dot_tn(v[h], k[h] * jnp.exp(gl[h] - g[h])) for h in heads]
    yield
    for h in heads:
        o_ref[0, :, vs[h]] = intra[h] + inter[h]
        s_ref[h] = st[h] * jnp.exp(gl[h]) + upd[h]


def gla_scan(p, w2_pair, gb_pair, *, n_ctx):
    b, t, _ = p.shape
    kwid = GLA_HEADS * GLA_DK
    toks = [(p, kwid, 0, 0), (p, kwid, 1, 1), (p, BRANCH, 1, 1), (p, LANES, 12, 12)]
    return _bidir_scan(_gla_body, toks, [w2_pair, gb_pair], (GLA_HEADS, GLA_DV, GLA_DK), b=b, t=t, n_ctx=n_ctx,
                       batch_block=_batch_block(b, 4))


def _rwkv_prep_kernel(x_ref, xp_ref, xn_ref, mu_ref, w2_ref, w0_ref, a2_ref, a0_ref, g2_ref, kk_ref_w,
                      ka_ref, rk_ref, r_ref, k_ref, v_ref, kk_ref, a_ref, lw_ref, g_ref, bo_ref,
                      *, nct, nt):
    x = x_ref[0]
    xp, xn = _neighbours(x, xp_ref[0], xn_ref[0], nct=nct, nt=nt)
    x = x + mu_ref[...] * (0.5 * (xp + xn) - x)
    r, k, v = x[:, :BRANCH], x[:, BRANCH:2 * BRANCH], x[:, 2 * BRANCH:3 * BRANCH]
    wlr = x[:, 3 * BRANCH:3 * BRANCH + LANES]
    alr = x[:, 3 * BRANCH + LANES:3 * BRANCH + 2 * LANES]
    glr = x[:, 3 * BRANCH + 2 * LANES:]
    w_raw = _dot_x3(jnp.tanh(wlr), w2_ref[...]) + w0_ref[...]
    lw_ref[0] = -jnp.exp(-_softplus(-w_raw) - 0.5)
    a = _sigmoid(_dot_x3(alr, a2_ref[...]) + a0_ref[...])
    a_ref[0] = a
    g_ref[0] = _dot_x3(_sigmoid(glr), g2_ref[...])
    kk = k * kk_ref_w[...]
    kk_ref[0] = kk * lax.rsqrt(_group_sums(kk * kk, RWKV_N) + EPS)
    ksum = k * (2.0 + (a[:, :BRANCH] + a[:, BRANCH:] - 2.0) * ka_ref[...])
    bo_ref[0] = _group_sums(r * ksum * rk_ref[...], RWKV_N) * v
    r_ref[0] = r
    k_ref[0] = k
    v_ref[0] = v


def rwkv_prep(p, lp, *, n_ctx):
    b, t, w = p.shape
    tt = _pick_tile(n_ctx, ROW_TILE)

    def pair(wp):
        r, c = wp.shape[1:]
        return jnp.zeros((LANES, 2 * c), F32).at[:r, :c].set(wp[0]).at[r:2 * r, c:].set(wp[1])

    consts = (_row(lp["rwkv_mu"]), pair(lp["rwkv_w2"]), _row(lp["rwkv_w0"]), pair(lp["rwkv_a2"]),
              _row(lp["rwkv_a0"]), lp["rwkv_g2"], _row(lp["rwkv_k_k"]), _row(lp["rwkv_k_a"]),
              _row(lp["rwkv_r_k"]))
    specs = _halo_specs(tt, w, 0, t) + [_const_spec(c.shape) for c in consts]
    kern = functools.partial(_rwkv_prep_kernel, nct=(t - n_ctx) // tt, nt=t // tt)
    return _prep_call(kern, (p, p, p) + consts, specs,
                      (BRANCH, BRANCH, BRANCH, BRANCH, 2 * BRANCH, 2 * BRANCH, BRANCH, BRANCH),
                      b=b, t=t, tt=tt)


def _rwkv_body(r_ref, k_ref, v_ref, kk_ref, a_ref, lw_ref, ka_ref, o_ref, s_ref, *, reverse):
    incl, strict = _chunk_masks(reverse)
    last = 0 if reverse else CHUNK - 1
    lw_all = lw_ref[0]
    g_all = _chunk_cumsum(incl, lw_all)
    a_all = a_ref[0]
    k_all = k_ref[0] * (1.0 + (a_all - 1.0) * ka_ref[...])
    yield
    heads = range(RWKV_HEADS)
    hs = [slice(h * RWKV_N, (h + 1) * RWKV_N) for h in heads]
    g = [g_all[:, hs[h]] for h in heads]
    gl = [g[h][last:last + 1, :] for h in heads]
    eneg = [jnp.exp(-g[h]) for h in heads]
    edec = [jnp.exp(gl[h] - g[h]) for h in heads]
    kk = [kk_ref[0, :, hs[h]] for h in heads]
    bvec = [kk[h] * a_all[:, hs[h]] for h in heads]
    k = [k_all[:, hs[h]] for h in heads]
    v = [v_ref[0, :, hs[h]] for h in heads]
    kkg = [kk[h] * jnp.exp(g[h] - lw_all[:, hs[h]]) for h in heads]
    rg = [r_ref[0, :, hs[h]] * jnp.exp(g[h]) for h in heads]
    bh = [bvec[h] * eneg[h] for h in heads]
    kh = [k[h] * eneg[h] for h in heads]
    s = [s_ref[h] for h in heads]
    yield
    both = [jnp.concatenate([kkg[h], rg[h]], axis=0) for h in heads]
    mask2 = jnp.concatenate([strict, incl], axis=0)
    mb = [jnp.where(mask2, _dot_nt(both[h], bh[h]), 0.0) for h in heads]
    yield
    mk = [jnp.where(mask2, _dot_nt(both[h], kh[h]), 0.0) for h in heads]
    yield
    part = [_dot(mk[h], v[h]) + _dot_nt(both[h], s[h]) for h in heads]
    yield
    x = yield from _unit_tri_solve([mb[h][:CHUNK] for h in heads], [part[h][:CHUNK] for h in heads])
    u = [-xh for xh in x]
    for h in heads:
        o_ref[0, :, hs[h]] = part[h][CHUNK:] + _dot(mb[h][CHUNK:], u[h])
    yield
    for h in heads:
        upd = _dot_tn(jnp.concatenate([u[h], v[h]], axis=0),
                      jnp.concatenate([bvec[h] * edec[h], k[h] * edec[h]], axis=0))
        s_ref[h] = s[h] * jnp.exp(gl[h]) + upd


def rwkv_scan(r, k, v, kk, a, lw, k_a, *, n_ctx):
    b, t, _ = r.shape
    toks = [(r, BRANCH, 0, 0), (k, BRANCH, 0, 0), (v, BRANCH, 0, 0), (kk, BRANCH, 0, 0),
            (a, BRANCH, 0, 1), (lw, BRANCH, 0, 1)]
    return _bidir_scan(_rwkv_body, toks, [(k_a, k_a)], (RWKV_HEADS, RWKV_N, RWKV_N), b=b, t=t, n_ctx=n_ctx,
                       batch_block=_batch_block(b, 2))


def _gdn_prep_kernel(x_ref, xp_ref, xn_ref, ab_ref, cw_ref, na_ref, dtb_ref,
                     q_ref, k_ref, v_ref, sm_ref, *, nct, nt):
    x = x_ref[0]
    xp, xn = _neighbours(x, xp_ref[0], xn_ref[0], nct=nct, nt=nt)
    y = _silu(xp * cw_ref[0:1, :] + x * cw_ref[1:2, :] + xn * cw_ref[2:3, :])
    q, k = y[:, :BRANCH], y[:, BRANCH:2 * BRANCH]
    q_ref[0] = q * lax.rsqrt(_group_sums(q * q, GDN_N) + EPS) * (GDN_N ** -0.5)
    k_ref[0] = k * lax.rsqrt(_group_sums(k * k, GDN_N) + EPS)
    v_ref[0] = y[:, 2 * BRANCH:]
    ab = ab_ref[0]
    lane = lax.broadcasted_iota(jnp.int32, ab.shape, 1)
    sm_ref[0] = jnp.where(lane < 2 * GDN_HEADS, na_ref[...] * _softplus(ab + dtb_ref[...]), _sigmoid(ab))


def gdn_prep(p, lp, *, n_ctx):
    b, t, _ = p.shape
    tt = _pick_tile(n_ctx, ROW_TILE)
    padrow = lambda v: jnp.pad(v.reshape(1, -1), ((0, 0), (0, LANES - 2 * GDN_HEADS)))
    consts = (lp["gdn_conv_w"], padrow(-jnp.exp(lp["gdn_a_log"])), padrow(lp["gdn_dt_bias"]))
    specs = (_halo_specs(tt, 3 * BRANCH, 0, t) + [pl.BlockSpec((1, tt, LANES), lambda bi, i: (bi, i, 16))]
             + [_const_spec(c.shape) for c in consts])
    kern = functools.partial(_gdn_prep_kernel, nct=(t - n_ctx) // tt, nt=t // tt)
    return _prep_call(kern, (p, p, p, p) + consts, specs, (BRANCH, BRANCH, BRANCH, LANES), b=b, t=t, tt=tt)


def _gdn_body(q_ref, k_ref, v_ref, sm_ref, o_ref, s_ref, *, reverse):
    incl, strict = _chunk_masks(reverse)
    last = 0 if reverse else CHUNK - 1
    off = GDN_HEADS if reverse else 0
    sm = sm_ref[0]
    g_all = _chunk_cumsum(incl, sm)
    yield
    gt_all = _transpose_small(g_all)
    yield
    heads = range(GDN_HEADS)
    hs = [slice(h * GDN_N, (h + 1) * GDN_N) for h in heads]
    g = [g_all[:, off + h:off + h + 1] for h in heads]
    gl = [g[h][last:last + 1, :] for h in heads]
    beta = [sm[:, 2 * GDN_HEADS + off + h:2 * GDN_HEADS + off + h + 1] for h in heads]
    q = [q_ref[0, :, hs[h]] for h in heads]
    k = [k_ref[0, :, hs[h]] for h in heads]
    v = [v_ref[0, :, hs[h]] for h in heads]
    s = [s_ref[h] for h in heads]
    decay = [jnp.exp(jnp.where(incl, g[h] - gt_all[off + h:off + h + 1, :], -jnp.inf)) for h in heads]
    yield
    kq = [_dot_nt(jnp.concatenate([k[h], q[h]], axis=0), k[h]) for h in heads]
    yield
    lower = [jnp.where(strict, kq[h][:CHUNK] * decay[h] * beta[h], 0.0) for h in heads]
    attn = [kq[h][CHUNK:] * decay[h] for h in heads]
    o_part = [_dot(q[h] * jnp.exp(g[h]), s[h]) for h in heads]
    yield
    rhs = [jnp.concatenate([v[h] * beta[h], k[h] * (beta[h] * jnp.exp(g[h]))], axis=1) for h in heads]
    sol = yield from _unit_tri_solve(lower, rhs, precise_levels=2)
    v_new = [sol[h][:, :GDN_N] - _dot(sol[h][:, GDN_N:], s[h]) for h in heads]
    yield
    for h in heads:
        o_ref[0, :, hs[h]] = o_part[h] + _dot(attn[h], v_new[h])
    yield
    for h in heads:
        s_ref[h] = s[h] * jnp.exp(gl[h]) + _dot_tn(k[h] * jnp.exp(gl[h] - g[h]), v_new[h])


def gdn_scan(q, k, v, sm, *, n_ctx):
    b, t, _ = q.shape
    toks = [(q, BRANCH, 0, 0), (k, BRANCH, 0, 0), (v, BRANCH, 0, 0), (sm, LANES, 0, 0)]
    return _bidir_scan(_gdn_body, toks, [], (GDN_HEADS, GDN_N, GDN_N), b=b, t=t, n_ctx=n_ctx,
                       batch_block=_batch_block(b, 4))


def _merge_kernel(sf_ref, sb_ref, sx_ref, sz_ref, gf_ref, gb_ref, gr_ref, rf_ref, rb_ref, rg_ref, rbo_ref,
                  df_ref, db_ref, dz_ref, gate_ref, x_ref, m_ref,
                  sd_ref, sn_ref, gn_ref, lnw_ref, lnb_ref, dn_ref,
                  wb_ref, wo_ref, o_ref):
    def group_rms(y, n, w_ref):
        return y * lax.rsqrt(_group_sums(y * y, n) * (1.0 / n) + EPS) * w_ref[...]

    y = (sf_ref[0] + sb_ref[0] + sd_ref[...] * sx_ref[0]) * _silu(sz_ref[0])
    ys = group_rms(y, BRANCH // SSM_GROUPS, sn_ref)
    yg = group_rms(gf_ref[0] + gb_ref[0], GLA_DV, gn_ref) * _silu(gr_ref[0])
    y = rf_ref[0] + rb_ref[0]
    yc = y - _group_sums(y, RWKV_N) * (1.0 / RWKV_N)
    var = _group_sums(yc * yc, RWKV_N) * (1.0 / RWKV_N)
    yr = (yc * lax.rsqrt(var + RWKV_LN_EPS) * lnw_ref[...] + lnb_ref[...] + rbo_ref[0]) * rg_ref[0]
    yd = group_rms(df_ref[0] + db_ref[0], GDN_N, dn_ref) * _silu(dz_ref[0])
    acc = None
    for i, yi in enumerate((ys, yg, yr, yd)):
        term = gate_ref[0, :, i * D_MODEL:(i + 1) * D_MODEL].astype(F32) * _dot(yi, wb_ref[i])
        acc = term if acc is None else acc + term
    o_ref[0] = x_ref[0] + m_ref[0, 0] * _dot(acc, wo_ref[...])


def merge_residual(ssm, gla, rwkv, gdn, gates, x_all, gate_mod, lp, w_branch, w_out, *, n_ctx):
    b, t, d = x_all.shape
    tm = _pick_tile(n_ctx, ROW_TILE)
    tok = lambda bi, i: (bi, i, 0)
    blk = lambda c: pl.BlockSpec((1, tm, BRANCH), lambda bi, i: (bi, i, c))
    half = blk(0)
    consts = (_row(jnp.repeat(lp["ssm_d"], SSM_P)), _row(lp["ssm_norm"]),
              _row(jnp.tile(lp["gla_norm"], GLA_HEADS)), _row(lp["rwkv_ln_w"]), _row(lp["rwkv_ln_b"]),
              _row(jnp.tile(lp["gdn_norm"], GDN_HEADS)), w_branch, w_out)
    ins = (ssm[0], ssm[1], ssm[2], ssm[3], gla[0], gla[1], gla[2], rwkv[0], rwkv[1], rwkv[2], rwkv[3],
           gdn[0], gdn[1], gdn[2], gates, x_all, gate_mod) + consts
    specs = ([half, half, half, blk(0), half, half, blk(2), half, half, half, half, half, half, blk(3),
              pl.BlockSpec((1, tm, 4 * d), tok), pl.BlockSpec((1, tm, d), tok),
              pl.BlockSpec((1, 1, 1, d), _mod_sel((t - n_ctx) // tm))]
             + [_const_spec(c.shape) for c in consts])
    return pl.pallas_call(
        _merge_kernel,
        grid=(b, t // tm),
        in_specs=specs,
        out_specs=pl.BlockSpec((1, tm, d), tok),
        out_shape=jax.ShapeDtypeStruct((b, t, d), F32),
        compiler_params=_cparams("parallel", "parallel"),
    )(*ins)


def _route_kernel(x_ref, nw_ref, shift_ref, scale_ref, rw_ref, rb_ref, u_ref, o_ref, cnt_ref, hb_ref, *, rows_kw):
    x = x_ref[...]
    shift, scale = _token_rows([shift_ref, scale_ref], pl.program_id(0), x.shape[0], **rows_kw)
    h = x * lax.rsqrt(jnp.mean(x * x, axis=-1, keepdims=True) + EPS) * nw_ref[...] * (1.0 + scale) + shift
    hb_ref[...] = h.astype(BF16)
    logits = lax.dot_general(rw_ref[...], h, (((1,), (1,)), ((), ())),
                             precision=HI, preferred_element_type=F32)
    scores = _sigmoid(logits)
    sel = scores + rb_ref[...]
    rows = [sel[e:e + 1, :] for e in range(N_EXPERTS)]
    sc = [scores[e:e + 1, :] for e in range(N_EXPERTS)]

    def top2(vals):
        v1, i1 = vals[0], jnp.zeros(vals[0].shape, jnp.int32)
        for j in range(1, len(vals)):
            better = vals[j] > v1
            v1 = jnp.where(better, vals[j], v1)
            i1 = jnp.where(better, j, i1)
        v2 = jnp.where(i1 == 0, vals[1], vals[0])
        i2 = jnp.where(i1 == 0, 1, 0)
        for j in range(1, len(vals)):
            better = (vals[j] > v2) & (i1 != j)
            v2 = jnp.where(better, vals[j], v2)
            i2 = jnp.where(better, j, i2)
        return v1, i1, v2, i2

    gsum = []
    for grp in range(N_GROUPS):
        v1, _, v2, _ = top2(rows[grp * EXPERTS_PER_GROUP:(grp + 1) * EXPERTS_PER_GROUP])
        gsum.append(v1 + v2)
    best, gidx = gsum[0], jnp.zeros(gsum[0].shape, jnp.int32)
    for grp in range(1, N_GROUPS):
        better = gsum[grp] > best
        best = jnp.where(better, gsum[grp], best)
        gidx = jnp.where(better, grp, gidx)
    chosen, chosen_sc = [], []
    for j in range(EXPERTS_PER_GROUP):
        cj, sj = rows[j], sc[j]
        for grp in range(1, N_GROUPS):
            cj = jnp.where(gidx == grp, rows[grp * EXPERTS_PER_GROUP + j], cj)
            sj = jnp.where(gidx == grp, sc[grp * EXPERTS_PER_GROUP + j], sj)
        chosen.append(cj)
        chosen_sc.append(sj)
    _, i1, _, i2 = top2(chosen)
    w1, w2 = jnp.zeros_like(best), jnp.zeros_like(best)
    for j in range(EXPERTS_PER_GROUP):
        w1 = jnp.where(i1 == j, chosen_sc[j], w1)
        w2 = jnp.where(i2 == j, chosen_sc[j], w2)
    tot = w1 + w2
    w1, w2 = w1 / tot, w2 / tot
    tm = scores.shape[1]
    sub = lax.broadcasted_iota(jnp.int32, (SUBLANES, tm), 0)
    ind8 = jnp.zeros((SUBLANES, tm), F32)
    meta = jnp.zeros((SUBLANES, tm), F32)
    for j in range(EXPERTS_PER_GROUP):
        gate_j = jnp.where(i1 == j, w1, 0.0) + jnp.where(i2 == j, w2, 0.0)
        meta = jnp.where(sub == j, gate_j, meta)
    for grp in range(N_GROUPS):
        ind8 = jnp.where((sub == grp) & (gidx == grp), 1.0, ind8)
    before = jnp.dot(ind8.astype(BF16), u_ref[...], preferred_element_type=F32)
    rank = jnp.sum(ind8 * before, axis=0, keepdims=True)
    meta = jnp.where(sub == _META_GROUP, gidx.astype(F32), meta)
    meta = jnp.where(sub == _META_RANK, rank, meta)
    o_ref[...] = meta
    counts = jnp.sum(ind8, axis=1, keepdims=True)
    lane = lax.broadcasted_iota(jnp.int32, (SUBLANES, LANES), 1)
    row = lax.broadcasted_iota(jnp.int32, (SUBLANES, LANES), 0)
    cnt_ref[0] = jnp.broadcast_to(jnp.sum(jnp.where(lane == row, counts, 0.0), axis=0, keepdims=True),
                                  (SUBLANES, LANES)).astype(jnp.int32)


_META_GROUP, _META_RANK = EXPERTS_PER_GROUP, EXPERTS_PER_GROUP + 1


def moe_route(x_all, norm_w, shift_rows, scale_rows, router_w, router_b, *, n_ctx, tm):
    b, t, d = x_all.shape
    m = b * t
    upper = jnp.asarray(np.triu(np.ones((tm, tm), np.float32), 1), BF16)
    return pl.pallas_call(
        functools.partial(_route_kernel, rows_kw=dict(bsz=b, t_all=t, n_ctx=n_ctx)),
        grid=(m // tm,),
        in_specs=[pl.BlockSpec((tm, d), lambda i: (i, 0)),
                  _const_spec((1, d)), _const_spec(shift_rows.shape), _const_spec(scale_rows.shape),
                  pl.BlockSpec((N_EXPERTS, d), lambda i: (0, 0)),
                  pl.BlockSpec((N_EXPERTS, 1), lambda i: (0, 0)),
                  _const_spec((tm, tm))],
        out_specs=[pl.BlockSpec((SUBLANES, tm), lambda i: (0, i)),
                   pl.BlockSpec((1, SUBLANES, LANES), lambda i: (i, 0, 0)),
                   pl.BlockSpec((tm, d), lambda i: (i, 0))],
        out_shape=[jax.ShapeDtypeStruct((SUBLANES, m), F32),
                   jax.ShapeDtypeStruct((m // tm, SUBLANES, LANES), jnp.int32),
                   jax.ShapeDtypeStruct((m, d), BF16)],
        compiler_params=_cparams("parallel"),
    )(x_all.reshape(m, d), norm_w.reshape(1, d), shift_rows, scale_rows,
      router_w.T, router_b.reshape(N_EXPERTS, 1), upper)


MOE_TILE = 1024
MOE_SUB_ROWS = 256
MOE_TAIL_ROWS = 128


def _expert_kernel(cnt_ref, h_ref, mr_ref, mc_ref, wg_ref, wu_ref, wd_ref, x_ref, gate_ref, o_ref, *, rows_kw):
    i, grp = pl.program_id(0), pl.program_id(1)

    @pl.when(grp == 0)
    def _():
        o_ref[...] = jnp.zeros_like(o_ref)

    tm = h_ref.shape[0]
    count = cnt_ref[i * N_GROUPS + grp]
    grp_f = grp.astype(F32)
    sel_row = jnp.where(mr_ref[_META_GROUP:_META_GROUP + 1, :] == grp_f, mr_ref[_META_RANK:_META_RANK + 1, :], -1.0)
    sel_col = jnp.where(mc_ref[:, _META_GROUP:_META_GROUP + 1] == grp_f, mc_ref[:, _META_RANK:_META_RANK + 1], -1.0)
    gate_parts = _split3(mc_ref[...])

    def sub_block(first, rows):
        base = first.astype(F32)
        slot_r = lax.broadcasted_iota(jnp.int32, (rows, tm), 0).astype(F32)
        slot_c = lax.broadcasted_iota(jnp.int32, (tm, rows), 1).astype(F32)
        pick = (sel_row - base == slot_r).astype(BF16)
        put = (sel_col - base == slot_c).astype(BF16)
        xg = jnp.dot(pick, h_ref[...], preferred_element_type=F32).astype(BF16)
        gates = sum(jnp.dot(pick, p, preferred_element_type=F32) for p in gate_parts)
        y = jnp.zeros((rows, o_ref.shape[1]), F32)
        for e in range(EXPERTS_PER_GROUP):
            hid = _silu(_dot(xg, wg_ref[e])) * _dot(xg, wu_ref[e])
            y = y + _dot(gates[:, e:e + 1] * hid, wd_ref[e])
        yh = y.astype(BF16)
        yl = (y - yh.astype(F32)).astype(BF16)
        o_ref[...] += (jnp.dot(put, yh, preferred_element_type=F32)
                       + jnp.dot(put, yl, preferred_element_type=F32))

    n_full = count // MOE_SUB_ROWS
    rem = count - n_full * MOE_SUB_ROWS
    n_main = n_full + (rem > MOE_TAIL_ROWS).astype(jnp.int32)

    def main_block(s, carry):
        sub_block(s * MOE_SUB_ROWS, MOE_SUB_ROWS)
        return carry

    lax.fori_loop(0, n_main, main_block, 0)

    @pl.when((rem > 0) & (rem <= MOE_TAIL_ROWS))
    def _():
        sub_block(n_full * MOE_SUB_ROWS, MOE_TAIL_ROWS)

    @pl.when(grp == N_GROUPS - 1)
    def _():
        (gate,) = _token_rows([gate_ref], i, tm, **rows_kw)
        o_ref[...] = x_ref[...] + gate * o_ref[...]


def moe_experts(hb, meta, counts, wg, wu, wd, x_all, gate_rows, *, n_ctx, tm):
    b, t, d = x_all.shape
    m = b * t
    tok = lambda i, g, cnt: (i, 0)
    grid_spec = pltpu.PrefetchScalarGridSpec(
        num_scalar_prefetch=1,
        grid=(m // tm, N_GROUPS),
        in_specs=[pl.BlockSpec((tm, d), tok),
                  pl.BlockSpec((SUBLANES, tm), lambda i, g, cnt: (0, i)),
                  pl.BlockSpec((tm, SUBLANES), tok),
                  pl.BlockSpec((EXPERTS_PER_GROUP, d, EXPERT_FF), lambda i, g, cnt: (g, 0, 0)),
                  pl.BlockSpec((EXPERTS_PER_GROUP, d, EXPERT_FF), lambda i, g, cnt: (g, 0, 0)),
                  pl.BlockSpec((EXPERTS_PER_GROUP, EXPERT_FF, d), lambda i, g, cnt: (g, 0, 0)),
                  pl.BlockSpec((tm, d), tok),
                  pl.BlockSpec(gate_rows.shape, lambda i, g, cnt: (0, 0))],
        out_specs=pl.BlockSpec((tm, d), tok))
    out = pl.pallas_call(
        functools.partial(_expert_kernel, rows_kw=dict(bsz=b, t_all=t, n_ctx=n_ctx)),
        grid_spec=grid_spec,
        out_shape=jax.ShapeDtypeStruct((m, d), F32),
        compiler_params=pltpu.CompilerParams(dimension_semantics=("parallel", "arbitrary"),
                                             vmem_limit_bytes=MOE_VMEM_LIMIT),
    )(counts[:, 0, :N_GROUPS].reshape(-1), hb, meta, meta.T, wg, wu, wd, x_all.reshape(m, d), gate_rows)
    return out.reshape(b, t, d)


def _token_rows(m_refs, tile, tm, *, bsz, t_all, n_ctx):
    row = tile * tm + lax.broadcasted_iota(jnp.int32, (tm, 1), 0)
    ctx = jnp.zeros((tm, 1), jnp.bool_)
    lat = []
    for bi in range(bsz):
        lo, split = bi * t_all, bi * t_all + t_all - n_ctx
        lat.append((row >= lo) & (row < split))
        ctx = ctx | ((row >= split) & (row < lo + t_all))
    out = []
    for m_ref in m_refs:
        v = jnp.where(ctx, m_ref[bsz:bsz + 1, :], 0.0)
        for bi in range(bsz):
            v = v + jnp.where(lat[bi], m_ref[bi:bi + 1, :], 0.0)
        out.append(v)
    return out


GRID_TILE_COLS = SUBLANES
GRID_TILE = GRID_W * GRID_TILE_COLS


def _grid_view(x_all, n_lat):
    b, t, d = x_all.shape
    rows = n_lat // GRID_W
    assert rows == GRID_W and t % GRID_W == 0
    last = GRID_W // GRID_TILE_COLS - 1
    spec = pl.BlockSpec((1, rows, GRID_TILE_COLS, d), lambda bi, i: (bi, 0, jnp.minimum(i, last), 0))
    return x_all.reshape(b, t // GRID_W, GRID_W, d), spec


def _grid_tile(xg_ref):
    return jnp.concatenate([xg_ref[0, :, j, :] for j in range(xg_ref.shape[2])], axis=0)


def _transpose_grid_kernel(xn_ref, xg_ref, o_ref, *, n_lat_tiles):
    i = pl.program_id(1)

    @pl.when(i < n_lat_tiles)
    def _():
        o_ref[0] = _grid_tile(xg_ref)

    @pl.when(i >= n_lat_tiles)
    def _():
        o_ref[0] = xn_ref[0]


def transpose_grid(x_all, *, n_ctx):
    b, t, d = x_all.shape
    xg, gspec = _grid_view(x_all, t - n_ctx)
    tok = pl.BlockSpec((1, GRID_TILE, d), lambda bi, i: (bi, i, 0))
    return pl.pallas_call(
        functools.partial(_transpose_grid_kernel, n_lat_tiles=(t - n_ctx) // GRID_TILE),
        grid=(b, pl.cdiv(t, GRID_TILE)),
        in_specs=[tok, gspec],
        out_specs=tok,
        out_shape=jax.ShapeDtypeStruct((b, t, d), x_all.dtype),
        compiler_params=_cparams("parallel", "parallel"),
    )(x_all, xg)


def _final_norm_kernel(x_ref, w_ref, o_ref, *, from_grid):
    x = _grid_tile(x_ref) if from_grid else x_ref[0]
    o_ref[0] = x * lax.rsqrt(jnp.mean(x * x, axis=-1, keepdims=True) + EPS) * w_ref[...]


def final_rms_norm(x_all, w, *, n_ctx, from_grid):
    b, t, d = x_all.shape
    n_lat = t - n_ctx
    tt = GRID_TILE
    tok = pl.BlockSpec((1, tt, d), lambda bi, i: (bi, i, 0))
    if from_grid:
        x_in, spec = _grid_view(x_all, n_lat)
    else:
        x_in, spec = x_all, tok
    return pl.pallas_call(
        functools.partial(_final_norm_kernel, from_grid=from_grid),
        grid=(b, n_lat // tt),
        in_specs=[spec, _const_spec((1, d))],
        out_specs=tok,
        out_shape=jax.ShapeDtypeStruct((b, n_lat, d), F32),
        compiler_params=_cparams("parallel", "parallel"),
    )(x_in, w.reshape(1, d))


def _pack_w_in(w_in, mixer):
    cols = _SRC_COLS[mixer]
    pieces, i = [], 0
    while i < len(cols):
        j = i
        if cols[i] < 0:
            while j < len(cols) and cols[j] < 0:
                j += 1
            pieces.append(jnp.zeros((w_in.shape[0], j - i), w_in.dtype))
        else:
            while j < len(cols) and cols[j] == cols[i] + (j - i):
                j += 1
            pieces.append(w_in[:, int(cols[i]):int(cols[i]) + (j - i)])
        i = j
    return jnp.concatenate(pieces, axis=1).astype(BF16)


def mixer_scans(ps, lp, *, n_ctx):
    p_ssm, p_gla, p_rwkv, p_gdn = ps

    xs, bc, sm = ssm_prep(p_ssm, lp, n_ctx=n_ctx)
    neg_a = jnp.pad(-jnp.exp(lp["ssm_a_log"]).reshape(1, -1), ((0, 0), (0, LANES - 2 * SSM_HEADS)))
    ssm = tuple(ssd_scan(xs, bc, sm, neg_a, n_ctx=n_ctx)) + (xs, p_ssm)

    w2 = [jnp.zeros((LANES, GLA_HEADS * GLA_DK), F32).at[d * GLA_RANK:(d + 1) * GLA_RANK].set(lp["gla_w2"][d])
          for d in range(2)]
    gb = [_row(lp["gla_b"][d]) for d in range(2)]
    gla = tuple(gla_scan(p_gla, w2, gb, n_ctx=n_ctx)) + (p_gla,)

    r, k, v, kk, a, lw, g, bonus = rwkv_prep(p_rwkv, lp, n_ctx=n_ctx)
    rwkv = tuple(rwkv_scan(r, k, v, kk, a, lw, _row(lp["rwkv_k_a"]), n_ctx=n_ctx)) + (g, bonus)

    q, kd, vd, smd = gdn_prep(p_gdn, lp, n_ctx=n_ctx)
    gdn = tuple(gdn_scan(q, kd, vd, smd, n_ctx=n_ctx)) + (p_gdn,)
    return ssm, gla, rwkv, gdn


def kernel(x, c, ctx, c_ctx, ada_w, ada_b, norm_mix, norm_ffn, w_in, w_gate, w_branch, w_out, ssm_conv_w, ssm_conv_b, ssm_a_log, ssm_dt_bias, ssm_d, ssm_norm, gla_w2, gla_b, gla_norm, rwkv_mu, rwkv_w0, rwkv_w2, rwkv_a0, rwkv_a2, rwkv_g2, rwkv_k_k, rwkv_k_a, rwkv_r_k, rwkv_ln_w, rwkv_ln_b, gdn_conv_w, gdn_a_log, gdn_dt_bias, gdn_norm, router_w, router_b, moe_w_gate, moe_w_up, moe_w_down, final_norm):
    bsz, seq, d = x.shape
    n_ctx = ctx.shape[1]
    t_all = n_ctx + seq
    m_all = bsz * t_all

    cond = jnp.concatenate([jax.nn.silu(c), jax.nn.silu(c_ctx)[None]], 0)
    cond = jnp.pad(cond, ((0, SUBLANES - cond.shape[0]), (0, 0)))
    mods, mod_rows = [], []
    for l in range(DEPTH):
        mod = pmatmul(cond, ada_w[l], tm=SUBLANES, tn=1024, precise=True) + ada_b[l]
        mod_rows.append(mod)
        lat = mod[:bsz].reshape(bsz, 6, d)
        cx = jnp.broadcast_to(mod[bsz].reshape(1, 6, d), (bsz, 6, d))
        mods.append(jnp.stack([cx, lat], axis=1))

    x_all = jnp.concatenate([x, ctx], axis=1)
    scan_order = False
    for l in range(DEPTH):
        if (l % 2 == 1) != scan_order:
            x_all = transpose_grid(x_all, n_ctx=n_ctx)
            scan_order = not scan_order
        lp = dict(ssm_conv_w=ssm_conv_w[l], ssm_conv_b=ssm_conv_b[l], ssm_a_log=ssm_a_log[l],
                  ssm_dt_bias=ssm_dt_bias[l], ssm_d=ssm_d[l], ssm_norm=ssm_norm[l],
                  gla_w2=gla_w2[l], gla_b=gla_b[l], gla_norm=gla_norm[l],
                  rwkv_mu=rwkv_mu[l], rwkv_w0=rwkv_w0[l], rwkv_w2=rwkv_w2[l], rwkv_a0=rwkv_a0[l],
                  rwkv_a2=rwkv_a2[l], rwkv_g2=rwkv_g2[l], rwkv_k_k=rwkv_k_k[l], rwkv_k_a=rwkv_k_a[l],
                  rwkv_r_k=rwkv_r_k[l], rwkv_ln_w=rwkv_ln_w[l], rwkv_ln_b=rwkv_ln_b[l],
                  gdn_conv_w=gdn_conv_w[l], gdn_a_log=gdn_a_log[l], gdn_dt_bias=gdn_dt_bias[l],
                  gdn_norm=gdn_norm[l])
        mod = mods[l]
        msel = lambda i: mod[:, :, i][:, :, None, :]

        h = norm_modulate(x_all, norm_mix[l], msel(0), msel(1), n_ctx=n_ctx)
        h2d = h.reshape(m_all, d)
        ps = []
        for mixer in ("ssm", "gla", "rwkv", "gdn"):
            wp = _pack_w_in(w_in[l], mixer)
            ps.append(pmatmul(h2d, wp, tm=512, tn=wp.shape[1]).reshape(bsz, t_all, wp.shape[1]))
        wg_cat = jnp.concatenate([w_gate[l, i] for i in range(4)], axis=1).astype(BF16)
        gates = pmatmul(h2d, wg_cat, tm=1024, tn=1024, act="sigmoid", out_dtype=BF16)
        gates = gates.reshape(bsz, t_all, 4 * d)

        ssm, gla, rwkv, gdn = mixer_scans(ps, lp, n_ctx=n_ctx)
        x_all = merge_residual(ssm, gla, rwkv, gdn, gates, x_all, msel(2), lp,
                               w_branch[l].astype(BF16), w_out[l].astype(BF16), n_ctx=n_ctx)

        tm_moe = _pick_tile(m_all, MOE_TILE)
        rows = lambda i: mod_rows[l][:, i * d:(i + 1) * d]
        meta, counts, hb = moe_route(x_all, norm_ffn[l], rows(3), rows(4), router_w, router_b,
                                     n_ctx=n_ctx, tm=tm_moe)
        x_all = moe_experts(hb, meta, counts, moe_w_gate[l].astype(BF16), moe_w_up[l].astype(BF16),
                            moe_w_down[l].astype(BF16), x_all, rows(5), n_ctx=n_ctx, tm=tm_moe)

    return final_rms_norm(x_all, final_norm, n_ctx=n_ctx, from_grid=scan_order)
```

```python
import functools
import itertools

import numpy as np
import jax
import jax.numpy as jnp
from jax import lax
from jax.experimental import pallas as pl
from jax.experimental.pallas import tpu as pltpu

F32 = jnp.float32
BF16 = jnp.bfloat16
HI = lax.Precision.HIGHEST

D_MODEL = 1024
DEPTH = 2
GRID_W = 64
CHUNK = 64
EPS = 1e-6
BRANCH = D_MODEL // 2
SSM_HEADS, SSM_P, SSM_GROUPS, SSM_N = 8, 64, 2, 64
GLA_HEADS, GLA_DK, GLA_DV, GLA_RANK, GLA_TAU = 4, 64, 128, 16, 16.0
RWKV_HEADS, RWKV_N, RWKV_LN_EPS = 8, 64, 64e-5
GDN_HEADS, GDN_N = 4, 128
N_EXPERTS, N_GROUPS, EXPERTS_PER_GROUP = 16, 4, 4
EXPERT_FF = D_MODEL // 2
LANES = 128
SUBLANES = 8
VMEM_LIMIT = 48 * 1024 * 1024
MOE_VMEM_LIMIT = 56 * 1024 * 1024
ROW_TILE = 256

_REF_BLOCKS = (
    ("ssm", "z", 512), ("ssm", "xbc", 768), ("ssm", "dt", 16),
    ("gla", "q", 256), ("gla", "k", 256), ("gla", "v", 512), ("gla", "r", 512), ("gla", "glr", 32),
    ("rwkv", "all", 1920),
    ("gdn", "qkv", 1536), ("gdn", "z", 512), ("gdn", "ab", 16),
)
_PACKED = {
    "ssm": (("z", 512), ("dt", 128), ("pad", 128), ("xbc", 768)),
    "gla": (("q", 256), ("k", 256), ("v", 512), ("r", 512), ("glr", 128)),
    "rwkv": (("all", 1920),),
    "gdn": (("qkv", 1536), ("z", 512), ("ab", 128)),
}


def _packed_columns():
    start, s = {}, 0
    for mixer, blk, w in _REF_BLOCKS:
        start[(mixer, blk)] = (s, w)
        s += w
    out = {}
    for mixer, blocks in _PACKED.items():
        cols = []
        for blk, wp in blocks:
            s0, w = start.get((mixer, blk), (0, 0))
            cols += list(range(s0, s0 + w)) + [-1] * (wp - w)
        out[mixer] = np.asarray(cols, np.int32)
    return out


_SRC_COLS = _packed_columns()


def _cparams(*sem):
    return pltpu.CompilerParams(dimension_semantics=sem, vmem_limit_bytes=VMEM_LIMIT)


def _dot(a, b):
    return jnp.dot(a.astype(BF16), b.astype(BF16), preferred_element_type=F32)


def _dot_nt(a, b):
    return lax.dot_general(a.astype(BF16), b.astype(BF16), (((1,), (1,)), ((), ())),
                           preferred_element_type=F32)


def _dot_tn(a, b):
    return lax.dot_general(a.astype(BF16), b.astype(BF16), (((0,), (0,)), ((), ())),
                           preferred_element_type=F32)


def _dot_hi(a, b):
    return jnp.dot(a, b, precision=HI, preferred_element_type=F32)


def _dot_x3(a, b):
    ah = a.astype(BF16)
    al = (a - ah.astype(F32)).astype(BF16)
    bh = b.astype(BF16)
    bl = (b - bh.astype(F32)).astype(BF16)
    f = lambda u, v: jnp.dot(u, v, preferred_element_type=F32)
    return f(ah, bh) + (f(ah, bl) + f(al, bh))


def _dot_x2(a, w):
    ah = a.astype(BF16)
    al = (a - ah.astype(F32)).astype(BF16)
    return jnp.dot(ah, w, preferred_element_type=F32) + jnp.dot(al, w, preferred_element_type=F32)


def _softplus(x):
    return jnp.maximum(x, 0.0) + jnp.log(1.0 + jnp.exp(-jnp.abs(x)))


def _sigmoid(x):
    return 1.0 / (1.0 + jnp.exp(-x))


def _silu(x):
    return x * _sigmoid(x)


def _pick_tile(m, pref):
    t = pref
    while m % t:
        t //= 2
    return t


def _group_sums(y, n):
    m = min(n, LANES)
    row = lax.broadcasted_iota(jnp.int32, (LANES, LANES), 0)
    col = lax.broadcasted_iota(jnp.int32, (LANES, LANES), 1)
    same = ((row // m) == (col // m)).astype(BF16)
    parts = [_dot_x2(y[:, j:j + LANES], same) for j in range(0, y.shape[1], LANES)]
    k = n // m
    if k > 1:
        parts = [sum(parts[g * k:(g + 1) * k]) for g in range(len(parts) // k) for _ in range(k)]
    return jnp.concatenate(parts, axis=1)


def _mm_kernel(a_ref, w_ref, o_ref, *, act, precise):
    if precise:
        r = _dot_hi(a_ref[...].astype(F32), w_ref[...].astype(F32))
    else:
        r = _dot(a_ref[...], w_ref[...])
    if act == "sigmoid":
        r = _sigmoid(r)
    o_ref[...] = r.astype(o_ref.dtype)


def pmatmul(a, w, *, tm, tn, act=None, precise=False, out_dtype=F32):
    m, k = a.shape
    n = w.shape[1]
    tm = _pick_tile(m, tm)
    assert tm % SUBLANES == 0 and n % tn == 0, (m, tm, n, tn)
    return pl.pallas_call(
        functools.partial(_mm_kernel, act=act, precise=precise),
        grid=(n // tn, m // tm),
        in_specs=[pl.BlockSpec((tm, k), lambda j, i: (i, 0)),
                  pl.BlockSpec((k, tn), lambda j, i: (0, j))],
        out_specs=pl.BlockSpec((tm, tn), lambda j, i: (i, j)),
        out_shape=jax.ShapeDtypeStruct((m, n), out_dtype),
        compiler_params=_cparams("parallel", "parallel"),
    )(a, w)


def _norm_mod_kernel(x_ref, w_ref, shift_ref, scale_ref, o_ref):
    x = x_ref[0]
    y = x * lax.rsqrt(jnp.mean(x * x, axis=-1, keepdims=True) + EPS) * w_ref[...]
    o_ref[0] = (y * (1.0 + scale_ref[0, 0]) + shift_ref[0, 0]).astype(o_ref.dtype)


def _mod_sel(n_lat_tiles):
    return lambda bi, i, *_: (bi, jnp.where(i < n_lat_tiles, 1, 0), 0, 0)


def norm_modulate(x_all, w, shift, scale, *, n_ctx, out_dtype=BF16):
    b, t, d = x_all.shape
    tm = _pick_tile(n_ctx, ROW_TILE)
    assert t % tm == 0
    tok = lambda bi, i: (bi, i, 0)
    return pl.pallas_call(
        _norm_mod_kernel,
        grid=(b, t // tm),
        in_specs=[pl.BlockSpec((1, tm, d), tok),
                  pl.BlockSpec((1, d), lambda bi, i: (0, 0)),
                  pl.BlockSpec((1, 1, 1, d), _mod_sel((t - n_ctx) // tm)),
                  pl.BlockSpec((1, 1, 1, d), _mod_sel((t - n_ctx) // tm))],
        out_specs=pl.BlockSpec((1, tm, d), tok),
        out_shape=jax.ShapeDtypeStruct((b, t, d), out_dtype),
        compiler_params=_cparams("parallel", "parallel"),
    )(x_all, w.reshape(1, d), shift, scale)


def _row(v):
    return v.reshape(1, -1).astype(F32)


def _const_spec(shape):
    return pl.BlockSpec(shape, lambda *_: (0,) * len(shape))


def _tile_specs(tt, width, col):
    r8 = tt // SUBLANES
    main = pl.BlockSpec((1, tt, width), lambda bi, i: (bi, i, col))
    prev = pl.BlockSpec((1, SUBLANES, width), lambda bi, i: (bi, jnp.maximum(i * r8 - 1, 0), col))
    return main, prev, r8


def _halo_specs(tt, width, col, t):
    main, prev, r8 = _tile_specs(tt, width, col)
    last8 = t // SUBLANES - 1
    nxt = pl.BlockSpec((1, SUBLANES, width), lambda bi, i: (bi, jnp.minimum((i + 1) * r8, last8), col))
    return [main, prev, nxt]


def _neighbours(x, prev8, next8, *, nct, nt):
    i = pl.program_id(1)
    tt = x.shape[0]
    row = lax.broadcasted_iota(jnp.int32, x.shape, 0)
    first = (i == 0) | (i == nct)
    last = (i == nct - 1) | (i == nt - 1)
    pr = jnp.where(first, 0.0, prev8[SUBLANES - 1:SUBLANES, :])
    nx = jnp.where(last, 0.0, next8[0:1, :])
    xp = jnp.where(row == 0, pr, pltpu.roll(x, 1, 0))
    xn = jnp.where(row == tt - 1, nx, pltpu.roll(x, tt - 1, 0))
    return xp, xn


def _prep_call(kernel, ins, in_specs, out_widths, *, b, t, tt, out_dtype=F32):
    tok = lambda bi, i: (bi, i, 0)
    return pl.pallas_call(
        kernel,
        grid=(b, t // tt),
        in_specs=in_specs,
        out_specs=[pl.BlockSpec((1, tt, w), tok) for w in out_widths],
        out_shape=[jax.ShapeDtypeStruct((b, t, w), out_dtype) for w in out_widths],
        compiler_params=_cparams("parallel", "parallel"),
    )(*ins)


def _chunk_masks(reverse):
    row = lax.broadcasted_iota(jnp.int32, (CHUNK, CHUNK), 0)
    col = lax.broadcasted_iota(jnp.int32, (CHUNK, CHUNK), 1)
    if reverse:
        return col >= row, col > row
    return col <= row, col < row


def _chunk_order(i, n_ctx_chunks, n_chunks, reverse):
    n_lat_chunks = n_chunks - n_ctx_chunks
    if not reverse:
        return jnp.where(i < n_ctx_chunks, n_lat_chunks + i, i - n_ctx_chunks)
    return jnp.where(i < n_ctx_chunks, n_chunks - 1 - i, n_lat_chunks - 1 - (i - n_ctx_chunks))


def _split3(a):
    hi = a.astype(BF16)
    r = a - hi.astype(F32)
    mid = r.astype(BF16)
    return hi, mid, (r - mid.astype(F32)).astype(BF16)


def _transpose_small(x):
    row = lax.broadcasted_iota(jnp.int32, (LANES, LANES), 0)
    col = lax.broadcasted_iota(jnp.int32, (LANES, LANES), 1)
    eye = (row == col).astype(BF16)
    nt = lambda p: lax.dot_general(eye, p, (((1,), (1,)), ((), ())), preferred_element_type=F32)
    hi, mid, lo = _split3(x)
    return nt(hi) + (nt(mid) + nt(lo))


def _chunk_cumsum(incl, x):
    m = incl.astype(BF16)
    hi, mid, lo = _split3(x)
    f = lambda p: jnp.dot(m, p, preferred_element_type=F32)
    return f(hi) + (f(mid) + f(lo))


def _select_columns(x, sel):
    c = x.shape[0]
    y = jnp.dot(jnp.concatenate(_split3(x), axis=0), sel, preferred_element_type=F32)
    return y[:c] + (y[c:2 * c] + y[2 * c:])


def _unit_tri_solve(mats, rhs, precise_levels=0):
    n = range(len(mats))
    x = [rhs[h] - _dot_x3(mats[h], rhs[h]) for h in n]
    yield
    p = mats
    for level in range(int(np.log2(CHUNK)) - 1):
        dot = _dot_x3 if level < precise_levels else _dot
        p = [dot(p[h], p[h]) for h in n]
        yield
        x = [x[h] + dot(p[h], x[h]) for h in n]
        yield
    return x


def _bidir_scan(body, tok_ins, const_ins, state_shape, *, b, t, n_ctx, lockstep=True, batch_block=1):
    nc, ncc = t // CHUNK, n_ctx // CHUNK
    nb = batch_block
    assert b % nb == 0

    def chunk_spec(width, col, reverse):
        return pl.BlockSpec((nb, CHUNK, width), lambda bi, i: (bi, _chunk_order(i, ncc, nc, reverse), col))

    def direction(reverse):
        d = int(reverse)
        specs = [chunk_spec(w, cols[d], reverse) for _, w, *cols in tok_ins]
        specs += [_const_spec(pair[d].shape) for pair in const_ins]
        return specs, [a for a, *_ in tok_ins] + [pair[d] for pair in const_ins]

    (spec_f, arg_f), (spec_b, arg_b) = direction(False), direction(True)
    n_tok, n_in = len(tok_ins), len(arg_f)

    def kern(*refs):
        o_f, o_b, s_f, s_b = refs[2 * n_in:]

        @pl.when(pl.program_id(1) == 0)
        def _():
            s_f[...] = jnp.zeros_like(s_f)
            s_b[...] = jnp.zeros_like(s_b)

        def one(j, ins, o_ref, s_ref, reverse):
            ins = [r.at[pl.ds(j, 1)] if k < n_tok else r for k, r in enumerate(ins)]
            return body(*ins, o_ref.at[pl.ds(j, 1)], s_ref.at[j], reverse=reverse)

        gens = []
        for j in range(nb):
            gens += [one(j, refs[:n_in], o_f, s_f, False), one(j, refs[n_in:2 * n_in], o_b, s_b, True)]
        if not lockstep:
            gens = [itertools.chain(*gens)]
        while gens:
            gens = [g for g in gens if next(g, _DONE) is not _DONE]

    return pl.pallas_call(
        kern,
        grid=(b // nb, nc),
        in_specs=spec_f + spec_b,
        out_specs=[chunk_spec(BRANCH, 0, False), chunk_spec(BRANCH, 0, True)],
        out_shape=[jax.ShapeDtypeStruct((b, t, BRANCH), F32)] * 2,
        scratch_shapes=[pltpu.VMEM((nb,) + tuple(state_shape), F32)] * 2,
        compiler_params=_cparams("parallel", "arbitrary"),
    )(*arg_f, *arg_b)


_DONE = object()


def _batch_block(b, pref):
    return pref if b % pref == 0 else 1


def _ssm_prep_kernel(x_ref, xp_ref, xn_ref, dt_ref, cw_ref, cb_ref, dtb_ref, xs_ref, bc_ref, sm_ref,
                     *, nct, nt):
    x = x_ref[0]
    xp, xn = _neighbours(x, xp_ref[0], xn_ref[0], nct=nct, nt=nt)
    y = _silu(xp * cw_ref[0:1, :] + x * cw_ref[1:2, :] + xn * cw_ref[2:3, :] + cb_ref[...])
    xs_ref[0] = y[:, :BRANCH]
    bc_ref[0] = y[:, BRANCH:]
    sm_ref[0] = _softplus(dt_ref[0] + dtb_ref[...])


def ssm_prep(p, lp, *, n_ctx):
    b, t, _ = p.shape
    tt = _pick_tile(n_ctx, ROW_TILE)
    dtb = jnp.pad(lp["ssm_dt_bias"].reshape(1, -1), ((0, 0), (0, LANES - 2 * SSM_HEADS)))
    specs = _halo_specs(tt, 768, 1, t) + [pl.BlockSpec((1, tt, LANES), lambda bi, i: (bi, i, 4)),
                                          _const_spec((3, 768)), _const_spec((1, 768)), _const_spec((1, LANES))]
    kern = functools.partial(_ssm_prep_kernel, nct=(t - n_ctx) // tt, nt=t // tt)
    return _prep_call(kern, (p, p, p, p, lp["ssm_conv_w"], _row(lp["ssm_conv_b"]), dtb), specs,
                      (BRANCH, 2 * SSM_GROUPS * SSM_N, LANES), b=b, t=t, tt=tt)


def _ssd_body(x_ref, bc_ref, sm_ref, na_ref, o_ref, s_ref, *, reverse):
    incl, _ = _chunk_masks(reverse)
    last = 0 if reverse else CHUNK - 1
    off = SSM_HEADS if reverse else 0
    dt_all = sm_ref[0]
    g_all = _chunk_cumsum(incl, dt_all * na_ref[...])
    yield
    expand = _expand_matrix(off, SSM_HEADS, SSM_P)
    gx = _select_columns(g_all, expand)
    dx = _select_columns(dt_all, expand)
    gt_all = _transpose_small(g_all)
    dtt_all = _transpose_small(dt_all)
    yield
    heads = range(SSM_HEADS)
    rep = SSM_HEADS // SSM_GROUPS
    gw = SSM_GROUPS * SSM_N
    hs = [slice(h * SSM_P, (h + 1) * SSM_P) for h in heads]
    glx = gx[last:last + 1, :]
    egx = jnp.exp(gx)
    wx = dx * jnp.exp(glx - gx)
    eglx = jnp.exp(glx)
    x = x_ref[0]
    bm = [bc_ref[0, :, grp * SSM_N:(grp + 1) * SSM_N] for grp in range(SSM_GROUPS)]
    cm = [bc_ref[0, :, gw + grp * SSM_N:gw + (grp + 1) * SSM_N] for grp in range(SSM_GROUPS)]
    cb = [_dot_nt(cm[grp], bm[grp]) for grp in range(SSM_GROUPS)]
    s = [s_ref[grp] for grp in range(SSM_GROUPS)]
    yield
    scores = [cb[h // rep] * jnp.exp(jnp.where(incl, gx[:, hs[h]] - gt_all[off + h:off + h + 1, :], -jnp.inf))
              * dtt_all[off + h:off + h + 1, :] for h in heads]
    yield
    intra = [_dot(scores[h], x[:, hs[h]]) for h in heads]
    yield
    inter = [_dot(cm[grp], s[grp]) for grp in range(SSM_GROUPS)]
    yield
    upd = [_dot_tn(bm[h // rep] * wx[:, hs[h]], x[:, hs[h]]) for h in heads]
    yield
    for h in heads:
        grp, ls = h // rep, slice((h % rep) * SSM_P, (h % rep + 1) * SSM_P)
        o_ref[0, :, hs[h]] = intra[h] + egx[:, hs[h]] * inter[grp][:, ls]
        s_ref[grp, :, ls] = s[grp][:, ls] * eglx[:, hs[h]] + upd[h]


def _expand_matrix(off, n_heads, width):
    row = lax.broadcasted_iota(jnp.int32, (LANES, n_heads * width), 0)
    col = lax.broadcasted_iota(jnp.int32, (LANES, n_heads * width), 1)
    lo = row * width - off * width
    return ((col >= lo) & (col < lo + width)).astype(BF16)


def ssd_scan(xs, bc, sm, neg_a, *, n_ctx):
    b, t, _ = xs.shape
    toks = [(xs, BRANCH, 0, 0), (bc, 2 * SSM_GROUPS * SSM_N, 0, 0), (sm, LANES, 0, 0)]
    state = (SSM_GROUPS, SSM_N, (SSM_HEADS // SSM_GROUPS) * SSM_P)
    return _bidir_scan(_ssd_body, toks, [(neg_a, neg_a)], state, b=b, t=t, n_ctx=n_ctx,
                       batch_block=_batch_block(b, 2))


def _gla_body(q_ref, k_ref, v_ref, glr_ref, w2_ref, gb_ref, o_ref, s_ref, *, reverse):
    incl, _ = _chunk_masks(reverse)
    last = 0 if reverse else CHUNK - 1
    logit = _dot_x3(glr_ref[0], w2_ref[...]) + gb_ref[...]
    yield
    la = -_softplus(-logit) * (1.0 / GLA_TAU)
    g_all = _chunk_cumsum(incl, la)
    yield
    heads = range(GLA_HEADS)
    ks = [slice(h * GLA_DK, (h + 1) * GLA_DK) for h in heads]
    vs = [slice(h * GLA_DV, (h + 1) * GLA_DV) for h in heads]
    g = [g_all[:, ks[h]] for h in heads]
    gl = [g[h][last:last + 1, :] for h in heads]
    k = [k_ref[0, :, ks[h]] for h in heads]
    v = [v_ref[0, :, vs[h]] for h in heads]
    qg = [q_ref[0, :, ks[h]] * (GLA_DK ** -0.5) * jnp.exp(g[h]) for h in heads]
    st = [s_ref[h] for h in heads]
    yield
    scores = [jnp.where(incl, _dot_nt(qg[h], k[h] * jnp.exp(-g[h])), 0.0) for h in heads]
    yield
    intra = [_dot(scores[h], v[h]) for h in heads]
    yield
    inter = [_dot_nt(qg[h], st[h]) for h in heads]
    yield
    upd = [_dot_tn(v[h], k[h] * jnp.exp(gl[h] - g[h])) for h in heads]
    yield
    for h in heads:
        o_ref[0, :, vs[h]] = intra[h] + inter[h]
        s_ref[h] = st[h] * jnp.exp(gl[h]) + upd[h]


def gla_scan(p, w2_pair, gb_pair, *, n_ctx):
    b, t, _ = p.shape
    kwid = GLA_HEADS * GLA_DK
    toks = [(p, kwid, 0, 0), (p, kwid, 1, 1), (p, BRANCH, 1, 1), (p, LANES, 12, 12)]
    return _bidir_scan(_gla_body, toks, [w2_pair, gb_pair], (GLA_HEADS, GLA_DV, GLA_DK), b=b, t=t, n_ctx=n_ctx,
                       batch_block=_batch_block(b, 4))


def _rwkv_prep_kernel(x_ref, xp_ref, xn_ref, mu_ref, w2_ref, w0_ref, a2_ref, a0_ref, g2_ref, kk_ref_w,
                      ka_ref, rk_ref, r_ref, k_ref, v_ref, kk_ref, a_ref, lw_ref, g_ref, bo_ref,
                      *, nct, nt):
    x = x_ref[0]
    xp, xn = _neighbours(x, xp_ref[0], xn_ref[0], nct=nct, nt=nt)
    x = x + mu_ref[...] * (0.5 * (xp + xn) - x)
    r, k, v = x[:, :BRANCH], x[:, BRANCH:2 * BRANCH], x[:, 2 * BRANCH:3 * BRANCH]
    wlr = x[:, 3 * BRANCH:3 * BRANCH + LANES]
    alr = x[:, 3 * BRANCH + LANES:3 * BRANCH + 2 * LANES]
    glr = x[:, 3 * BRANCH + 2 * LANES:]
    w_raw = _dot_x3(jnp.tanh(wlr), w2_ref[...]) + w0_ref[...]
    lw_ref[0] = -jnp.exp(-_softplus(-w_raw) - 0.5)
    a = _sigmoid(_dot_x3(alr, a2_ref[...]) + a0_ref[...])
    a_ref[0] = a
    g_ref[0] = _dot_x3(_sigmoid(glr), g2_ref[...])
    kk = k * kk_ref_w[...]
    kk_ref[0] = kk * lax.rsqrt(_group_sums(kk * kk, RWKV_N) + EPS)
    ksum = k * (2.0 + (a[:, :BRANCH] + a[:, BRANCH:] - 2.0) * ka_ref[...])
    bo_ref[0] = _group_sums(r * ksum * rk_ref[...], RWKV_N) * v
    r_ref[0] = r
    k_ref[0] = k
    v_ref[0] = v


def rwkv_prep(p, lp, *, n_ctx):
    b, t, w = p.shape
    tt = _pick_tile(n_ctx, ROW_TILE)

    def pair(wp):
        r, c = wp.shape[1:]
        return jnp.zeros((LANES, 2 * c), F32).at[:r, :c].set(wp[0]).at[r:2 * r, c:].set(wp[1])

    consts = (_row(lp["rwkv_mu"]), pair(lp["rwkv_w2"]), _row(lp["rwkv_w0"]), pair(lp["rwkv_a2"]),
              _row(lp["rwkv_a0"]), lp["rwkv_g2"], _row(lp["rwkv_k_k"]), _row(lp["rwkv_k_a"]),
              _row(lp["rwkv_r_k"]))
    specs = _halo_specs(tt, w, 0, t) + [_const_spec(c.shape) for c in consts]
    kern = functools.partial(_rwkv_prep_kernel, nct=(t - n_ctx) // tt, nt=t // tt)
    return _prep_call(kern, (p, p, p) + consts, specs,
                      (BRANCH, BRANCH, BRANCH, BRANCH, 2 * BRANCH, 2 * BRANCH, BRANCH, BRANCH),
                      b=b, t=t, tt=tt)


def _rwkv_body(r_ref, k_ref, v_ref, kk_ref, a_ref, lw_ref, ka_ref, o_ref, s_ref, *, reverse):
    incl, strict = _chunk_masks(reverse)
    last = 0 if reverse else CHUNK - 1
    lw_all = lw_ref[0]
    g_all = _chunk_cumsum(incl, lw_all)
    a_all = a_ref[0]
    k_all = k_ref[0] * (1.0 + (a_all - 1.0) * ka_ref[...])
    yield
    heads = range(RWKV_HEADS)
    hs = [slice(h * RWKV_N, (h + 1) * RWKV_N) for h in heads]
    g = [g_all[:, hs[h]] for h in heads]
    gl = [g[h][last:last + 1, :] for h in heads]
    eneg = [jnp.exp(-g[h]) for h in heads]
    edec = [jnp.exp(gl[h] - g[h]) for h in heads]
    kk = [kk_ref[0, :, hs[h]] for h in heads]
    bvec = [kk[h] * a_all[:, hs[h]] for h in heads]
    k = [k_all[:, hs[h]] for h in heads]
    v = [v_ref[0, :, hs[h]] for h in heads]
    kkg = [kk[h] * jnp.exp(g[h] - lw_all[:, hs[h]]) for h in heads]
    rg = [r_ref[0, :, hs[h]] * jnp.exp(g[h]) for h in heads]
    bh = [bvec[h] * eneg[h] for h in heads]
    kh = [k[h] * eneg[h] for h in heads]
    s = [s_ref[h] for h in heads]
    yield
    both = [jnp.concatenate([kkg[h], rg[h]], axis=0) for h in heads]
    mask2 = jnp.concatenate([strict, incl], axis=0)
    mb = [jnp.where(mask2, _dot_nt(both[h], bh[h]), 0.0) for h in heads]
    yield
    mk = [jnp.where(mask2, _dot_nt(both[h], kh[h]), 0.0) for h in heads]
    yield
    part = [_dot(mk[h], v[h]) + _dot_nt(both[h], s[h]) for h in heads]
    yield
    x = yield from _unit_tri_solve([mb[h][:CHUNK] for h in heads], [part[h][:CHUNK] for h in heads])
    u = [-xh for xh in x]
    for h in heads:
        o_ref[0, :, hs[h]] = part[h][CHUNK:] + _dot(mb[h][CHUNK:], u[h])
    yield
    for h in heads:
        upd = _dot_tn(jnp.concatenate([u[h], v[h]], axis=0),
                      jnp.concatenate([bvec[h] * edec[h], k[h] * edec[h]], axis=0))
        s_ref[h] = s[h] * jnp.exp(gl[h]) + upd


def rwkv_scan(r, k, v, kk, a, lw, k_a, *, n_ctx):
    b, t, _ = r.shape
    toks = [(r, BRANCH, 0, 0), (k, BRANCH, 0, 0), (v, BRANCH, 0, 0), (kk, BRANCH, 0, 0),
            (a, BRANCH, 0, 1), (lw, BRANCH, 0, 1)]
    return _bidir_scan(_rwkv_body, toks, [(k_a, k_a)], (RWKV_HEADS, RWKV_N, RWKV_N), b=b, t=t, n_ctx=n_ctx,
                       batch_block=_batch_block(b, 2))


def _gdn_prep_kernel(x_ref, xp_ref, xn_ref, ab_ref, cw_ref, na_ref, dtb_ref,
                     q_ref, k_ref, v_ref, sm_ref, *, nct, nt):
    x = x_ref[0]
    xp, xn = _neighbours(x, xp_ref[0], xn_ref[0], nct=nct, nt=nt)
    y = _silu(xp * cw_ref[0:1, :] + x * cw_ref[1:2, :] + xn * cw_ref[2:3, :])
    q, k = y[:, :BRANCH], y[:, BRANCH:2 * BRANCH]
    q_ref[0] = q * lax.rsqrt(_group_sums(q * q, GDN_N) + EPS) * (GDN_N ** -0.5)
    k_ref[0] = k * lax.rsqrt(_group_sums(k * k, GDN_N) + EPS)
    v_ref[0] = y[:, 2 * BRANCH:]
    ab = ab_ref[0]
    lane = lax.broadcasted_iota(jnp.int32, ab.shape, 1)
    sm_ref[0] = jnp.where(lane < 2 * GDN_HEADS, na_ref[...] * _softplus(ab + dtb_ref[...]), _sigmoid(ab))


def gdn_prep(p, lp, *, n_ctx):
    b, t, _ = p.shape
    tt = _pick_tile(n_ctx, ROW_TILE)
    padrow = lambda v: jnp.pad(v.reshape(1, -1), ((0, 0), (0, LANES - 2 * GDN_HEADS)))
    consts = (lp["gdn_conv_w"], padrow(-jnp.exp(lp["gdn_a_log"])), padrow(lp["gdn_dt_bias"]))
    specs = (_halo_specs(tt, 3 * BRANCH, 0, t) + [pl.BlockSpec((1, tt, LANES), lambda bi, i: (bi, i, 16))]
             + [_const_spec(c.shape) for c in consts])
    kern = functools.partial(_gdn_prep_kernel, nct=(t - n_ctx) // tt, nt=t // tt)
    return _prep_call(kern, (p, p, p, p) + consts, specs, (BRANCH, BRANCH, BRANCH, LANES), b=b, t=t, tt=tt)


def _gdn_body(q_ref, k_ref, v_ref, sm_ref, o_ref, s_ref, *, reverse):
    incl, strict = _chunk_masks(reverse)
    last = 0 if reverse else CHUNK - 1
    off = GDN_HEADS if reverse else 0
    sm = sm_ref[0]
    g_all = _chunk_cumsum(incl, sm)
    yield
    gt_all = _transpose_small(g_all)
    yield
    heads = range(GDN_HEADS)
    hs = [slice(h * GDN_N, (h + 1) * GDN_N) for h in heads]
    g = [g_all[:, off + h:off + h + 1] for h in heads]
    gl = [g[h][last:last + 1, :] for h in heads]
    beta = [sm[:, 2 * GDN_HEADS + off + h:2 * GDN_HEADS + off + h + 1] for h in heads]
    q = [q_ref[0, :, hs[h]] for h in heads]
    k = [k_ref[0, :, hs[h]] for h in heads]
    v = [v_ref[0, :, hs[h]] for h in heads]
    s = [s_ref[h] for h in heads]
    decay = [jnp.exp(jnp.where(incl, g[h] - gt_all[off + h:off + h + 1, :], -jnp.inf)) for h in heads]
    yield
    kq = [_dot_nt(jnp.concatenate([k[h], q[h]], axis=0), k[h]) for h in heads]
    yield
    lower = [jnp.where(strict, kq[h][:CHUNK] * decay[h] * beta[h], 0.0) for h in heads]
    attn = [kq[h][CHUNK:] * decay[h] for h in heads]
    o_part = [_dot(q[h] * jnp.exp(g[h]), s[h]) for h in heads]
    yield
    rhs = [jnp.concatenate([v[h] * beta[h], k[h] * (beta[h] * jnp.exp(g[h]))], axis=1) for h in heads]
    sol = yield from _unit_tri_solve(lower, rhs, precise_levels=2)
    v_new = [sol[h][:, :GDN_N] - _dot(sol[h][:, GDN_N:], s[h]) for h in heads]
    yield
    for h in heads:
        o_ref[0, :, hs[h]] = o_part[h] + _dot(attn[h], v_new[h])
    yield
    for h in heads:
        s_ref[h] = s[h] * jnp.exp(gl[h]) + _dot_tn(k[h] * jnp.exp(gl[h] - g[h]), v_new[h])


def gdn_scan(q, k, v, sm, *, n_ctx):
    b, t, _ = q.shape
    toks = [(q, BRANCH, 0, 0), (k, BRANCH, 0, 0), (v, BRANCH, 0, 0), (sm, LANES, 0, 0)]
    return _bidir_scan(_gdn_body, toks, [], (GDN_HEADS, GDN_N, GDN_N), b=b, t=t, n_ctx=n_ctx,
                       batch_block=_batch_block(b, 4))


def _merge_kernel(sf_ref, sb_ref, sx_ref, sz_ref, gf_ref, gb_ref, gr_ref, rf_ref, rb_ref, rg_ref, rbo_ref,
                  df_ref, db_ref, dz_ref, gate_ref, x_ref, m_ref,
                  sd_ref, sn_ref, gn_ref, lnw_ref, lnb_ref, dn_ref,
                  wb_ref, wo_ref, o_ref):
    def group_rms(y, n, w_ref):
        return y * lax.rsqrt(_group_sums(y * y, n) * (1.0 / n) + EPS) * w_ref[...]

    y = (sf_ref[0] + sb_ref[0] + sd_ref[...] * sx_ref[0]) * _silu(sz_ref[0])
    ys = group_rms(y, BRANCH // SSM_GROUPS, sn_ref)
    yg = group_rms(gf_ref[0] + gb_ref[0], GLA_DV, gn_ref) * _silu(gr_ref[0])
    y = rf_ref[0] + rb_ref[0]
    yc = y - _group_sums(y, RWKV_N) * (1.0 / RWKV_N)
    var = _group_sums(yc * yc, RWKV_N) * (1.0 / RWKV_N)
    yr = (yc * lax.rsqrt(var + RWKV_LN_EPS) * lnw_ref[...] + lnb_ref[...] + rbo_ref[0]) * rg_ref[0]
    yd = group_rms(df_ref[0] + db_ref[0], GDN_N, dn_ref) * _silu(dz_ref[0])
    acc = None
    for i, yi in enumerate((ys, yg, yr, yd)):
        term = gate_ref[0, :, i * D_MODEL:(i + 1) * D_MODEL].astype(F32) * _dot(yi, wb_ref[i])
        acc = term if acc is None else acc + term
    o_ref[0] = x_ref[0] + m_ref[0, 0] * _dot(acc, wo_ref[...])


def merge_residual(ssm, gla, rwkv, gdn, gates, x_all, gate_mod, lp, w_branch, w_out, *, n_ctx):
    b, t, d = x_all.shape
    tm = _pick_tile(n_ctx, ROW_TILE)
    tok = lambda bi, i: (bi, i, 0)
    blk = lambda c: pl.BlockSpec((1, tm, BRANCH), lambda bi, i: (bi, i, c))
    half = blk(0)
    consts = (_row(jnp.repeat(lp["ssm_d"], SSM_P)), _row(lp["ssm_norm"]),
              _row(jnp.tile(lp["gla_norm"], GLA_HEADS)), _row(lp["rwkv_ln_w"]), _row(lp["rwkv_ln_b"]),
              _row(jnp.tile(lp["gdn_norm"], GDN_HEADS)), w_branch, w_out)
    ins = (ssm[0], ssm[1], ssm[2], ssm[3], gla[0], gla[1], gla[2], rwkv[0], rwkv[1], rwkv[2], rwkv[3],
           gdn[0], gdn[1], gdn[2], gates, x_all, gate_mod) + consts
    specs = ([half, half, half, blk(0), half, half, blk(2), half, half, half, half, half, half, blk(3),
              pl.BlockSpec((1, tm, 4 * d), tok), pl.BlockSpec((1, tm, d), tok),
              pl.BlockSpec((1, 1, 1, d), _mod_sel((t - n_ctx) // tm))]
             + [_const_spec(c.shape) for c in consts])
    return pl.pallas_call(
        _merge_kernel,
        grid=(b, t // tm),
        in_specs=specs,
        out_specs=pl.BlockSpec((1, tm, d), tok),
        out_shape=jax.ShapeDtypeStruct((b, t, d), F32),
        compiler_params=_cparams("parallel", "parallel"),
    )(*ins)


def _route_kernel(x_ref, nw_ref, shift_ref, scale_ref, rw_ref, rb_ref, u_ref, o_ref, cnt_ref, hb_ref, *, rows_kw):
    x = x_ref[...]
    shift, scale = _token_rows([shift_ref, scale_ref], pl.program_id(0), x.shape[0], **rows_kw)
    h = x * lax.rsqrt(jnp.mean(x * x, axis=-1, keepdims=True) + EPS) * nw_ref[...] * (1.0 + scale) + shift
    hb_ref[...] = h.astype(BF16)
    logits = lax.dot_general(rw_ref[...], h, (((1,), (1,)), ((), ())),
                             precision=HI, preferred_element_type=F32)
    scores = _sigmoid(logits)
    sel = scores + rb_ref[...]
    rows = [sel[e:e + 1, :] for e in range(N_EXPERTS)]
    sc = [scores[e:e + 1, :] for e in range(N_EXPERTS)]

    def top2(vals):
        v1, i1 = vals[0], jnp.zeros(vals[0].shape, jnp.int32)
        for j in range(1, len(vals)):
            better = vals[j] > v1
            v1 = jnp.where(better, vals[j], v1)
            i1 = jnp.where(better, j, i1)
        v2 = jnp.where(i1 == 0, vals[1], vals[0])
        i2 = jnp.where(i1 == 0, 1, 0)
        for j in range(1, len(vals)):
            better = (vals[j] > v2) & (i1 != j)
            v2 = jnp.where(better, vals[j], v2)
            i2 = jnp.where(better, j, i2)
        return v1, i1, v2, i2

    gsum = []
    for grp in range(N_GROUPS):
        v1, _, v2, _ = top2(rows[grp * EXPERTS_PER_GROUP:(grp + 1) * EXPERTS_PER_GROUP])
        gsum.append(v1 + v2)
    best, gidx = gsum[0], jnp.zeros(gsum[0].shape, jnp.int32)
    for grp in range(1, N_GROUPS):
        better = gsum[grp] > best
        best = jnp.where(better, gsum[grp], best)
        gidx = jnp.where(better, grp, gidx)
    chosen, chosen_sc = [], []
    for j in range(EXPERTS_PER_GROUP):
        cj, sj = rows[j], sc[j]
        for grp in range(1, N_GROUPS):
            cj = jnp.where(gidx == grp, rows[grp * EXPERTS_PER_GROUP + j], cj)
            sj = jnp.where(gidx == grp, sc[grp * EXPERTS_PER_GROUP + j], sj)
        chosen.append(cj)
        chosen_sc.append(sj)
    _, i1, _, i2 = top2(chosen)
    w1, w2 = jnp.zeros_like(best), jnp.zeros_like(best)
    for j in range(EXPERTS_PER_GROUP):
        w1 = jnp.where(i1 == j, chosen_sc[j], w1)
        w2 = jnp.where(i2 == j, chosen_sc[j], w2)
    tot = w1 + w2
    w1, w2 = w1 / tot, w2 / tot
    tm = scores.shape[1]
    sub = lax.broadcasted_iota(jnp.int32, (SUBLANES, tm), 0)
    ind8 = jnp.zeros((SUBLANES, tm), F32)
    meta = jnp.zeros((SUBLANES, tm), F32)
    for j in range(EXPERTS_PER_GROUP):
        gate_j = jnp.where(i1 == j, w1, 0.0) + jnp.where(i2 == j, w2, 0.0)
        meta = jnp.where(sub == j, gate_j, meta)
    for grp in range(N_GROUPS):
        ind8 = jnp.where((sub == grp) & (gidx == grp), 1.0, ind8)
    before = jnp.dot(ind8.astype(BF16), u_ref[...], preferred_element_type=F32)
    rank = jnp.sum(ind8 * before, axis=0, keepdims=True)
    meta = jnp.where(sub == _META_GROUP, gidx.astype(F32), meta)
    meta = jnp.where(sub == _META_RANK, rank, meta)
    o_ref[...] = meta
    counts = jnp.sum(ind8, axis=1, keepdims=True)
    lane = lax.broadcasted_iota(jnp.int32, (SUBLANES, LANES), 1)
    row = lax.broadcasted_iota(jnp.int32, (SUBLANES, LANES), 0)
    cnt_ref[0] = jnp.broadcast_to(jnp.sum(jnp.where(lane == row, counts, 0.0), axis=0, keepdims=True),
                                  (SUBLANES, LANES)).astype(jnp.int32)


_META_GROUP, _META_RANK = EXPERTS_PER_GROUP, EXPERTS_PER_GROUP + 1


def moe_route(x_all, norm_w, shift_rows, scale_rows, router_w, router_b, *, n_ctx, tm):
    b, t, d = x_all.shape
    m = b * t
    upper = jnp.asarray(np.triu(np.ones((tm, tm), np.float32), 1), BF16)
    return pl.pallas_call(
        functools.partial(_route_kernel, rows_kw=dict(bsz=b, t_all=t, n_ctx=n_ctx)),
        grid=(m // tm,),
        in_specs=[pl.BlockSpec((tm, d), lambda i: (i, 0)),
                  _const_spec((1, d)), _const_spec(shift_rows.shape), _const_spec(scale_rows.shape),
                  pl.BlockSpec((N_EXPERTS, d), lambda i: (0, 0)),
                  pl.BlockSpec((N_EXPERTS, 1), lambda i: (0, 0)),
                  _const_spec((tm, tm))],
        out_specs=[pl.BlockSpec((SUBLANES, tm), lambda i: (0, i)),
                   pl.BlockSpec((1, SUBLANES, LANES), lambda i: (i, 0, 0)),
                   pl.BlockSpec((tm, d), lambda i: (i, 0))],
        out_shape=[jax.ShapeDtypeStruct((SUBLANES, m), F32),
                   jax.ShapeDtypeStruct((m // tm, SUBLANES, LANES), jnp.int32),
                   jax.ShapeDtypeStruct((m, d), BF16)],
        compiler_params=_cparams("parallel"),
    )(x_all.reshape(m, d), norm_w.reshape(1, d), shift_rows, scale_rows,
      router_w.T, router_b.reshape(N_EXPERTS, 1), upper)


MOE_TILE = 1024
MOE_SUB_ROWS = 256
MOE_TAIL_ROWS = 128


def _expert_kernel(cnt_ref, h_ref, mr_ref, mc_ref, wg_ref, wu_ref, wd_ref, x_ref, gate_ref, o_ref, *, rows_kw):
    i, grp = pl.program_id(0), pl.program_id(1)

    @pl.when(grp == 0)
    def _():
        o_ref[...] = jnp.zeros_like(o_ref)

    tm = h_ref.shape[0]
    count = cnt_ref[i * N_GROUPS + grp]
    grp_f = grp.astype(F32)
    sel_row = jnp.where(mr_ref[_META_GROUP:_META_GROUP + 1, :] == grp_f, mr_ref[_META_RANK:_META_RANK + 1, :], -1.0)
    sel_col = jnp.where(mc_ref[:, _META_GROUP:_META_GROUP + 1] == grp_f, mc_ref[:, _META_RANK:_META_RANK + 1], -1.0)
    gate_parts = _split3(mc_ref[...])

    def sub_block(first, rows):
        base = first.astype(F32)
        slot_r = lax.broadcasted_iota(jnp.int32, (rows, tm), 0).astype(F32)
        slot_c = lax.broadcasted_iota(jnp.int32, (tm, rows), 1).astype(F32)
        pick = (sel_row - base == slot_r).astype(BF16)
        put = (sel_col - base == slot_c).astype(BF16)
        xg = jnp.dot(pick, h_ref[...], preferred_element_type=F32).astype(BF16)
        gates = sum(jnp.dot(pick, p, preferred_element_type=F32) for p in gate_parts)
        y = jnp.zeros((rows, o_ref.shape[1]), F32)
        for e in range(EXPERTS_PER_GROUP):
            hid = _silu(_dot(xg, wg_ref[e])) * _dot(xg, wu_ref[e])
            y = y + _dot(gates[:, e:e + 1] * hid, wd_ref[e])
        o_ref[...] += jnp.dot(put, y.astype(BF16), preferred_element_type=F32)

    n_full = count // MOE_SUB_ROWS
    rem = count - n_full * MOE_SUB_ROWS
    n_main = n_full + (rem > MOE_TAIL_ROWS).astype(jnp.int32)

    def main_block(s, carry):
        sub_block(s * MOE_SUB_ROWS, MOE_SUB_ROWS)
        return carry

    lax.fori_loop(0, n_main, main_block, 0)

    @pl.when((rem > 0) & (rem <= MOE_TAIL_ROWS))
    def _():
        sub_block(n_full * MOE_SUB_ROWS, MOE_TAIL_ROWS)

    @pl.when(grp == N_GROUPS - 1)
    def _():
        (gate,) = _token_rows([gate_ref], i, tm, **rows_kw)
        o_ref[...] = x_ref[...] + gate * o_ref[...]


def moe_experts(hb, meta, counts, wg, wu, wd, x_all, gate_rows, *, n_ctx, tm):
    b, t, d = x_all.shape
    m = b * t
    tok = lambda i, g, cnt: (i, 0)
    grid_spec = pltpu.PrefetchScalarGridSpec(
        num_scalar_prefetch=1,
        grid=(m // tm, N_GROUPS),
        in_specs=[pl.BlockSpec((tm, d), tok),
                  pl.BlockSpec((SUBLANES, tm), lambda i, g, cnt: (0, i)),
                  pl.BlockSpec((tm, SUBLANES), tok),
                  pl.BlockSpec((EXPERTS_PER_GROUP, d, EXPERT_FF), lambda i, g, cnt: (g, 0, 0)),
                  pl.BlockSpec((EXPERTS_PER_GROUP, d, EXPERT_FF), lambda i, g, cnt: (g, 0, 0)),
                  pl.BlockSpec((EXPERTS_PER_GROUP, EXPERT_FF, d), lambda i, g, cnt: (g, 0, 0)),
                  pl.BlockSpec((tm, d), tok),
                  pl.BlockSpec(gate_rows.shape, lambda i, g, cnt: (0, 0))],
        out_specs=pl.BlockSpec((tm, d), tok))
    out = pl.pallas_call(
        functools.partial(_expert_kernel, rows_kw=dict(bsz=b, t_all=t, n_ctx=n_ctx)),
        grid_spec=grid_spec,
        out_shape=jax.ShapeDtypeStruct((m, d), F32),
        compiler_params=pltpu.CompilerParams(dimension_semantics=("parallel", "arbitrary"),
                                             vmem_limit_bytes=MOE_VMEM_LIMIT),
    )(counts[:, 0, :N_GROUPS].reshape(-1), hb, meta, meta.T, wg, wu, wd, x_all.reshape(m, d), gate_rows)
    return out.reshape(b, t, d)


def _token_rows(m_refs, tile, tm, *, bsz, t_all, n_ctx):
    row = tile * tm + lax.broadcasted_iota(jnp.int32, (tm, 1), 0)
    ctx = jnp.zeros((tm, 1), jnp.bool_)
    lat = []
    for bi in range(bsz):
        lo, split = bi * t_all, bi * t_all + t_all - n_ctx
        lat.append((row >= lo) & (row < split))
        ctx = ctx | ((row >= split) & (row < lo + t_all))
    out = []
    for m_ref in m_refs:
        v = jnp.where(ctx, m_ref[bsz:bsz + 1, :], 0.0)
        for bi in range(bsz):
            v = v + jnp.where(lat[bi], m_ref[bi:bi + 1, :], 0.0)
        out.append(v)
    return out


GRID_TILE_COLS = SUBLANES
GRID_TILE = GRID_W * GRID_TILE_COLS


def _grid_view(x_all, n_lat):
    b, t, d = x_all.shape
    rows = n_lat // GRID_W
    assert rows == GRID_W and t % GRID_W == 0
    last = GRID_W // GRID_TILE_COLS - 1
    spec = pl.BlockSpec((1, rows, GRID_TILE_COLS, d), lambda bi, i: (bi, 0, jnp.minimum(i, last), 0))
    return x_all.reshape(b, t // GRID_W, GRID_W, d), spec


def _grid_tile(xg_ref):
    return jnp.concatenate([xg_ref[0, :, j, :] for j in range(xg_ref.shape[2])], axis=0)


def _transpose_grid_kernel(xn_ref, xg_ref, o_ref, *, n_lat_tiles):
    i = pl.program_id(1)

    @pl.when(i < n_lat_tiles)
    def _():
        o_ref[0] = _grid_tile(xg_ref)

    @pl.when(i >= n_lat_tiles)
    def _():
        o_ref[0] = xn_ref[0]


def transpose_grid(x_all, *, n_ctx):
    b, t, d = x_all.shape
    xg, gspec = _grid_view(x_all, t - n_ctx)
    tok = pl.BlockSpec((1, GRID_TILE, d), lambda bi, i: (bi, i, 0))
    return pl.pallas_call(
        functools.partial(_transpose_grid_kernel, n_lat_tiles=(t - n_ctx) // GRID_TILE),
        grid=(b, pl.cdiv(t, GRID_TILE)),
        in_specs=[tok, gspec],
        out_specs=tok,
        out_shape=jax.ShapeDtypeStruct((b, t, d), x_all.dtype),
        compiler_params=_cparams("parallel", "parallel"),
    )(x_all, xg)


def _final_norm_kernel(x_ref, w_ref, o_ref, *, from_grid):
    x = _grid_tile(x_ref) if from_grid else x_ref[0]
    o_ref[0] = x * lax.rsqrt(jnp.mean(x * x, axis=-1, keepdims=True) + EPS) * w_ref[...]


def final_rms_norm(x_all, w, *, n_ctx, from_grid):
    b, t, d = x_all.shape
    n_lat = t - n_ctx
    tt = GRID_TILE
    tok = pl.BlockSpec((1, tt, d), lambda bi, i: (bi, i, 0))
    if from_grid:
        x_in, spec = _grid_view(x_all, n_lat)
    else:
        x_in, spec = x_all, tok
    return pl.pallas_call(
        functools.partial(_final_norm_kernel, from_grid=from_grid),
        grid=(b, n_lat // tt),
        in_specs=[spec, _const_spec((1, d))],
        out_specs=tok,
        out_shape=jax.ShapeDtypeStruct((b, n_lat, d), F32),
        compiler_params=_cparams("parallel", "parallel"),
    )(x_in, w.reshape(1, d))


def _pack_w_in(w_in, mixer):
    cols = _SRC_COLS[mixer]
    pieces, i = [], 0
    while i < len(cols):
        j = i
        if cols[i] < 0:
            while j < len(cols) and cols[j] < 0:
                j += 1
            pieces.append(jnp.zeros((w_in.shape[0], j - i), w_in.dtype))
        else:
            while j < len(cols) and cols[j] == cols[i] + (j - i):
                j += 1
            pieces.append(w_in[:, int(cols[i]):int(cols[i]) + (j - i)])
        i = j
    return jnp.concatenate(pieces, axis=1).astype(BF16)


def mixer_scans(ps, lp, *, n_ctx):
    p_ssm, p_gla, p_rwkv, p_gdn = ps

    xs, bc, sm = ssm_prep(p_ssm, lp, n_ctx=n_ctx)
    neg_a = jnp.pad(-jnp.exp(lp["ssm_a_log"]).reshape(1, -1), ((0, 0), (0, LANES - 2 * SSM_HEADS)))
    ssm = tuple(ssd_scan(xs, bc, sm, neg_a, n_ctx=n_ctx)) + (xs, p_ssm)

    w2 = [jnp.zeros((LANES, GLA_HEADS * GLA_DK), F32).at[d * GLA_RANK:(d + 1) * GLA_RANK].set(lp["gla_w2"][d])
          for d in range(2)]
    gb = [_row(lp["gla_b"][d]) for d in range(2)]
    gla = tuple(gla_scan(p_gla, w2, gb, n_ctx=n_ctx)) + (p_gla,)

    r, k, v, kk, a, lw, g, bonus = rwkv_prep(p_rwkv, lp, n_ctx=n_ctx)
    rwkv = tuple(rwkv_scan(r, k, v, kk, a, lw, _row(lp["rwkv_k_a"]), n_ctx=n_ctx)) + (g, bonus)

    q, kd, vd, smd = gdn_prep(p_gdn, lp, n_ctx=n_ctx)
    gdn = tuple(gdn_scan(q, kd, vd, smd, n_ctx=n_ctx)) + (p_gdn,)
    return ssm, gla, rwkv, gdn


def kernel(x, c, ctx, c_ctx, ada_w, ada_b, norm_mix, norm_ffn, w_in, w_gate, w_branch, w_out, ssm_conv_w, ssm_conv_b, ssm_a_log, ssm_dt_bias, ssm_d, ssm_norm, gla_w2, gla_b, gla_norm, rwkv_mu, rwkv_w0, rwkv_w2, rwkv_a0, rwkv_a2, rwkv_g2, rwkv_k_k, rwkv_k_a, rwkv_r_k, rwkv_ln_w, rwkv_ln_b, gdn_conv_w, gdn_a_log, gdn_dt_bias, gdn_norm, router_w, router_b, moe_w_gate, moe_w_up, moe_w_down, final_norm):
    bsz, seq, d = x.shape
    n_ctx = ctx.shape[1]
    t_all = n_ctx + seq
    m_all = bsz * t_all

    cond = jnp.concatenate([jax.nn.silu(c), jax.nn.silu(c_ctx)[None]], 0)
    cond = jnp.pad(cond, ((0, SUBLANES - cond.shape[0]), (0, 0)))
    mods, mod_rows = [], []
    for l in range(DEPTH):
        mod = pmatmul(cond, ada_w[l], tm=SUBLANES, tn=1024, precise=True) + ada_b[l]
        mod_rows.append(mod)
        lat = mod[:bsz].reshape(bsz, 6, d)
        cx = jnp.broadcast_to(mod[bsz].reshape(1, 6, d), (bsz, 6, d))
        mods.append(jnp.stack([cx, lat], axis=1))

    x_all = jnp.concatenate([x, ctx], axis=1)
    scan_order = False
    for l in range(DEPTH):
        if (l % 2 == 1) != scan_order:
            x_all = transpose_grid(x_all, n_ctx=n_ctx)
            scan_order = not scan_order
        lp = dict(ssm_conv_w=ssm_conv_w[l], ssm_conv_b=ssm_conv_b[l], ssm_a_log=ssm_a_log[l],
                  ssm_dt_bias=ssm_dt_bias[l], ssm_d=ssm_d[l], ssm_norm=ssm_norm[l],
                  gla_w2=gla_w2[l], gla_b=gla_b[l], gla_norm=gla_norm[l],
                  rwkv_mu=rwkv_mu[l], rwkv_w0=rwkv_w0[l], rwkv_w2=rwkv_w2[l], rwkv_a0=rwkv_a0[l],
                  rwkv_a2=rwkv_a2[l], rwkv_g2=rwkv_g2[l], rwkv_k_k=rwkv_k_k[l], rwkv_k_a=rwkv_k_a[l],
                  rwkv_r_k=rwkv_r_k[l], rwkv_ln_w=rwkv_ln_w[l], rwkv_ln_b=rwkv_ln_b[l],
                  gdn_conv_w=gdn_conv_w[l], gdn_a_log=gdn_a_log[l], gdn_dt_bias=gdn_dt_bias[l],
                  gdn_norm=gdn_norm[l])
        mod = mods[l]
        msel = lambda i: mod[:, :, i][:, :, None, :]

        h = norm_modulate(x_all, norm_mix[l], msel(0), msel(1), n_ctx=n_ctx)
        h2d = h.reshape(m_all, d)
        ps = []
        for mixer in ("ssm", "gla", "rwkv", "gdn"):
            wp = _pack_w_in(w_in[l], mixer)
            ps.append(pmatmul(h2d, wp, tm=512, tn=wp.shape[1]).reshape(bsz, t_all, wp.shape[1]))
        wg_cat = jnp.concatenate([w_gate[l, i] for i in range(4)], axis=1).astype(BF16)
        gates = pmatmul(h2d, wg_cat, tm=1024, tn=1024, act="sigmoid", out_dtype=BF16)
        gates = gates.reshape(bsz, t_all, 4 * d)

        ssm, gla, rwkv, gdn = mixer_scans(ps, lp, n_ctx=n_ctx)
        x_all = merge_residual(ssm, gla, rwkv, gdn, gates, x_all, msel(2), lp,
                               w_branch[l].astype(BF16), w_out[l].astype(BF16), n_ctx=n_ctx)

        tm_moe = _pick_tile(m_all, MOE_TILE)
        rows = lambda i: mod_rows[l][:, i * d:(i + 1) * d]
        meta, counts, hb = moe_route(x_all, norm_ffn[l], rows(3), rows(4), router_w, router_b,
                                     n_ctx=n_ctx, tm=tm_moe)
        x_all = moe_experts(hb, meta, counts, moe_w_gate[l].astype(BF16), moe_w_up[l].astype(BF16),
                            moe_w_down[l].astype(BF16), x_all, rows(5), n_ctx=n_ctx, tm=tm_moe)

    return final_rms_norm(x_all, final_norm, n_ctx=n_ctx, from_grid=scan_order)
```

```python
import functools
import itertools

import numpy as np
import jax
import jax.numpy as jnp
from jax import lax
from jax.experimental import pallas as pl
from jax.experimental.pallas import tpu as pltpu

F32 = jnp.float32
BF16 = jnp.bfloat16
HI = lax.Precision.HIGHEST

D_MODEL = 1024
DEPTH = 2
GRID_W = 64
CHUNK = 64
EPS = 1e-6
BRANCH = D_MODEL // 2
SSM_HEADS, SSM_P, SSM_GROUPS, SSM_N = 8, 64, 2, 64
GLA_HEADS, GLA_DK, GLA_DV, GLA_RANK, GLA_TAU = 4, 64, 128, 16, 16.0
RWKV_HEADS, RWKV_N, RWKV_LN_EPS = 8, 64, 64e-5
GDN_HEADS, GDN_N = 4, 128
N_EXPERTS, N_GROUPS, EXPERTS_PER_GROUP = 16, 4, 4
EXPERT_FF = D_MODEL // 2
LANES = 128
SUBLANES = 8
VMEM_LIMIT = 48 * 1024 * 1024
MOE_VMEM_LIMIT = 56 * 1024 * 1024
ROW_TILE = 256

_REF_BLOCKS = (
    ("ssm", "z", 512), ("ssm", "xbc", 768), ("ssm", "dt", 16),
    ("gla", "q", 256), ("gla", "k", 256), ("gla", "v", 512), ("gla", "r", 512), ("gla", "glr", 32),
    ("rwkv", "all", 1920),
    ("gdn", "qkv", 1536), ("gdn", "z", 512), ("gdn", "ab", 16),
)
_PACKED = {
    "ssm": (("z", 512), ("dt", 128), ("pad", 128), ("xbc", 768)),
    "gla": (("q", 256), ("k", 256), ("v", 512), ("r", 512), ("glr", 128)),
    "rwkv": (("all", 1920),),
    "gdn": (("qkv", 1536), ("z", 512), ("ab", 128)),
}


def _packed_columns():
    start, s = {}, 0
    for mixer, blk, w in _REF_BLOCKS:
        start[(mixer, blk)] = (s, w)
        s += w
    out = {}
    for mixer, blocks in _PACKED.items():
        cols = []
        for blk, wp in blocks:
            s0, w = start.get((mixer, blk), (0, 0))
            cols += list(range(s0, s0 + w)) + [-1] * (wp - w)
        out[mixer] = np.asarray(cols, np.int32)
    return out


_SRC_COLS = _packed_columns()


def _cparams(*sem):
    return pltpu.CompilerParams(dimension_semantics=sem, vmem_limit_bytes=VMEM_LIMIT)


def _dot(a, b):
    return jnp.dot(a.astype(BF16), b.astype(BF16), preferred_element_type=F32)


def _dot_nt(a, b):
    return lax.dot_general(a.astype(BF16), b.astype(BF16), (((1,), (1,)), ((), ())),
                           preferred_element_type=F32)


def _dot_tn(a, b):
    return lax.dot_general(a.astype(BF16), b.astype(BF16), (((0,), (0,)), ((), ())),
                           preferred_element_type=F32)


def _dot_hi(a, b):
    return jnp.dot(a, b, precision=HI, preferred_element_type=F32)


def _dot_x3(a, b):
    ah = a.astype(BF16)
    al = (a - ah.astype(F32)).astype(BF16)
    bh = b.astype(BF16)
    bl = (b - bh.astype(F32)).astype(BF16)
    f = lambda u, v: jnp.dot(u, v, preferred_element_type=F32)
    return f(ah, bh) + (f(ah, bl) + f(al, bh))


def _dot_x2(a, w):
    ah = a.astype(BF16)
    al = (a - ah.astype(F32)).astype(BF16)
    return jnp.dot(ah, w, preferred_element_type=F32) + jnp.dot(al, w, preferred_element_type=F32)


def _softplus(x):
    return jnp.maximum(x, 0.0) + jnp.log(1.0 + jnp.exp(-jnp.abs(x)))


def _sigmoid(x):
    return 1.0 / (1.0 + jnp.exp(-x))


def _silu(x):
    return x * _sigmoid(x)


def _pick_tile(m, pref):
    t = pref
    while m % t:
        t //= 2
    return t


def _group_sums(y, n):
    m = min(n, LANES)
    row = lax.broadcasted_iota(jnp.int32, (LANES, LANES), 0)
    col = lax.broadcasted_iota(jnp.int32, (LANES, LANES), 1)
    same = ((row // m) == (col // m)).astype(BF16)
    parts = [_dot_x2(y[:, j:j + LANES], same) for j in range(0, y.shape[1], LANES)]
    k = n // m
    if k > 1:
        parts = [sum(parts[g * k:(g + 1) * k]) for g in range(len(parts) // k) for _ in range(k)]
    return jnp.concatenate(parts, axis=1)


def _mm_kernel(a_ref, w_ref, o_ref, *, act, precise):
    if precise:
        r = _dot_hi(a_ref[...].astype(F32), w_ref[...].astype(F32))
    else:
        r = _dot(a_ref[...], w_ref[...])
    if act == "sigmoid":
        r = _sigmoid(r)
    o_ref[...] = r.astype(o_ref.dtype)


def pmatmul(a, w, *, tm, tn, act=None, precise=False, out_dtype=F32):
    m, k = a.shape
    tm = _pick_tile(m, tm)
    if w.ndim == 3:
        per = w.shape[2] // tn
        n = w.shape[0] * w.shape[2]
        w_spec = pl.BlockSpec((None, k, tn), lambda j, i: (j // per, 0, j % per))
    else:
        n = w.shape[1]
        w_spec = pl.BlockSpec((k, tn), lambda j, i: (0, j))
    assert tm % SUBLANES == 0 and n % tn == 0 and w.shape[-1] % tn == 0, (m, tm, w.shape, tn)
    return pl.pallas_call(
        functools.partial(_mm_kernel, act=act, precise=precise),
        grid=(n // tn, m // tm),
        in_specs=[pl.BlockSpec((tm, k), lambda j, i: (i, 0)), w_spec],
        out_specs=pl.BlockSpec((tm, tn), lambda j, i: (i, j)),
        out_shape=jax.ShapeDtypeStruct((m, n), out_dtype),
        compiler_params=_cparams("parallel", "parallel"),
    )(a, w)


def _norm_mod_kernel(x_ref, w_ref, shift_ref, scale_ref, o_ref):
    x = x_ref[0]
    y = x * lax.rsqrt(jnp.mean(x * x, axis=-1, keepdims=True) + EPS) * w_ref[...]
    o_ref[0] = (y * (1.0 + scale_ref[0, 0]) + shift_ref[0, 0]).astype(o_ref.dtype)


def _mod_sel(n_lat_tiles):
    return lambda bi, i, *_: (bi, jnp.where(i < n_lat_tiles, 1, 0), 0, 0)


def norm_modulate(x_all, w, shift, scale, *, n_ctx, out_dtype=BF16):
    b, t, d = x_all.shape
    tm = _pick_tile(n_ctx, ROW_TILE)
    assert t % tm == 0
    tok = lambda bi, i: (bi, i, 0)
    return pl.pallas_call(
        _norm_mod_kernel,
        grid=(b, t // tm),
        in_specs=[pl.BlockSpec((1, tm, d), tok),
                  pl.BlockSpec((1, d), lambda bi, i: (0, 0)),
                  pl.BlockSpec((1, 1, 1, d), _mod_sel((t - n_ctx) // tm)),
                  pl.BlockSpec((1, 1, 1, d), _mod_sel((t - n_ctx) // tm))],
        out_specs=pl.BlockSpec((1, tm, d), tok),
        out_shape=jax.ShapeDtypeStruct((b, t, d), out_dtype),
        compiler_params=_cparams("parallel", "parallel"),
    )(x_all, w.reshape(1, d), shift, scale)


def _row(v):
    return v.reshape(1, -1).astype(F32)


def _const_spec(shape):
    return pl.BlockSpec(shape, lambda *_: (0,) * len(shape))


def _tile_specs(tt, width, col):
    r8 = tt // SUBLANES
    main = pl.BlockSpec((1, tt, width), lambda bi, i: (bi, i, col))
    prev = pl.BlockSpec((1, SUBLANES, width), lambda bi, i: (bi, jnp.maximum(i * r8 - 1, 0), col))
    return main, prev, r8


def _halo_specs(tt, width, col, t):
    main, prev, r8 = _tile_specs(tt, width, col)
    last8 = t // SUBLANES - 1
    nxt = pl.BlockSpec((1, SUBLANES, width), lambda bi, i: (bi, jnp.minimum((i + 1) * r8, last8), col))
    return [main, prev, nxt]


def _neighbours(x, prev8, next8, *, nct, nt):
    i = pl.program_id(1)
    tt = x.shape[0]
    row = lax.broadcasted_iota(jnp.int32, x.shape, 0)
    first = (i == 0) | (i == nct)
    last = (i == nct - 1) | (i == nt - 1)
    pr = jnp.where(first, 0.0, prev8[SUBLANES - 1:SUBLANES, :])
    nx = jnp.where(last, 0.0, next8[0:1, :])
    xp = jnp.where(row == 0, pr, pltpu.roll(x, 1, 0))
    xn = jnp.where(row == tt - 1, nx, pltpu.roll(x, tt - 1, 0))
    return xp, xn


def _prep_call(kernel, ins, in_specs, out_widths, *, b, t, tt, out_dtype=F32):
    tok = lambda bi, i: (bi, i, 0)
    return pl.pallas_call(
        kernel,
        grid=(b, t // tt),
        in_specs=in_specs,
        out_specs=[pl.BlockSpec((1, tt, w), tok) for w in out_widths],
        out_shape=[jax.ShapeDtypeStruct((b, t, w), out_dtype) for w in out_widths],
        compiler_params=_cparams("parallel", "parallel"),
    )(*ins)


def _chunk_masks(reverse):
    row = lax.broadcasted_iota(jnp.int32, (CHUNK, CHUNK), 0)
    col = lax.broadcasted_iota(jnp.int32, (CHUNK, CHUNK), 1)
    if reverse:
        return col >= row, col > row
    return col <= row, col < row


def _chunk_order(i, n_ctx_chunks, n_chunks, reverse):
    n_lat_chunks = n_chunks - n_ctx_chunks
    if not reverse:
        return jnp.where(i < n_ctx_chunks, n_lat_chunks + i, i - n_ctx_chunks)
    return jnp.where(i < n_ctx_chunks, n_chunks - 1 - i, n_lat_chunks - 1 - (i - n_ctx_chunks))


def _split3(a):
    hi = a.astype(BF16)
    r = a - hi.astype(F32)
    mid = r.astype(BF16)
    return hi, mid, (r - mid.astype(F32)).astype(BF16)


def _transpose_small(x):
    row = lax.broadcasted_iota(jnp.int32, (LANES, LANES), 0)
    col = lax.broadcasted_iota(jnp.int32, (LANES, LANES), 1)
    eye = (row == col).astype(BF16)
    nt = lambda p: lax.dot_general(eye, p, (((1,), (1,)), ((), ())), preferred_element_type=F32)
    hi, mid, lo = _split3(x)
    return nt(hi) + (nt(mid) + nt(lo))


def _chunk_cumsum(incl, x):
    m = incl.astype(BF16)
    hi, mid, lo = _split3(x)
    f = lambda p: jnp.dot(m, p, preferred_element_type=F32)
    return f(hi) + (f(mid) + f(lo))


def _select_columns(x, sel):
    c = x.shape[0]
    y = jnp.dot(jnp.concatenate(_split3(x), axis=0), sel, preferred_element_type=F32)
    return y[:c] + (y[c:2 * c] + y[2 * c:])


def _unit_tri_solve(mats, rhs, precise_levels=0, explicit=False):
    n = range(len(mats))
    if explicit:
        row = lax.broadcasted_iota(jnp.int32, (CHUNK, CHUNK), 0)
        col = lax.broadcasted_iota(jnp.int32, (CHUNK, CHUNK), 1)
        x = [(row == col).astype(F32) - mats[h] for h in n]
    else:
        x = [rhs[h] - _dot_x3(mats[h], rhs[h]) for h in n]
    yield
    p = mats
    for level in range(int(np.log2(CHUNK)) - 1):
        dot = _dot_x3 if level < precise_levels else _dot
        p = [dot(p[h], p[h]) for h in n]
        yield
        x = [x[h] + (dot(x[h], p[h]) if explicit else dot(p[h], x[h])) for h in n]
        yield
    if explicit:
        x = [_dot_x3(x[h], rhs[h]) for h in n]
        yield
    return x


def _bidir_scan(body, tok_ins, const_ins, state_shape, *, b, t, n_ctx, lockstep=True, batch_block=1):
    nc, ncc = t // CHUNK, n_ctx // CHUNK
    nb = batch_block
    assert b % nb == 0

    def chunk_spec(width, col, reverse):
        return pl.BlockSpec((nb, CHUNK, width), lambda bi, i: (bi, _chunk_order(i, ncc, nc, reverse), col))

    def direction(reverse):
        d = int(reverse)
        specs = [chunk_spec(w, cols[d], reverse) for _, w, *cols in tok_ins]
        specs += [_const_spec(pair[d].shape) for pair in const_ins]
        return specs, [a for a, *_ in tok_ins] + [pair[d] for pair in const_ins]

    (spec_f, arg_f), (spec_b, arg_b) = direction(False), direction(True)
    n_tok, n_in = len(tok_ins), len(arg_f)

    def kern(*refs):
        o_f, o_b, s_f, s_b = refs[2 * n_in:]

        @pl.when(pl.program_id(1) == 0)
        def _():
            s_f[...] = jnp.zeros_like(s_f)
            s_b[...] = jnp.zeros_like(s_b)

        def one(j, ins, o_ref, s_ref, reverse):
            ins = [r.at[pl.ds(j, 1)] if k < n_tok else r for k, r in enumerate(ins)]
            return body(*ins, o_ref.at[pl.ds(j, 1)], s_ref.at[j], reverse=reverse)

        gens = []
        for j in range(nb):
            gens += [one(j, refs[:n_in], o_f, s_f, False), one(j, refs[n_in:2 * n_in], o_b, s_b, True)]
        if not lockstep:
            gens = [itertools.chain(*gens)]
        while gens:
            gens = [g for g in gens if next(g, _DONE) is not _DONE]

    return pl.pallas_call(
        kern,
        grid=(b // nb, nc),
        in_specs=spec_f + spec_b,
        out_specs=[chunk_spec(BRANCH, 0, False), chunk_spec(BRANCH, 0, True)],
        out_shape=[jax.ShapeDtypeStruct((b, t, BRANCH), F32)] * 2,
        scratch_shapes=[pltpu.VMEM((nb,) + tuple(state_shape), F32)] * 2,
        compiler_params=_cparams("parallel", "arbitrary"),
    )(*arg_f, *arg_b)


_DONE = object()


def _batch_block(b, pref):
    return pref if b % pref == 0 else 1


def _ssm_prep_kernel(x_ref, xp_ref, xn_ref, dt_ref, cw_ref, cb_ref, dtb_ref, xs_ref, bc_ref, sm_ref,
                     *, nct, nt):
    x = x_ref[0]
    xp, xn = _neighbours(x, xp_ref[0], xn_ref[0], nct=nct, nt=nt)
    y = _silu(xp * cw_ref[0:1, :] + x * cw_ref[1:2, :] + xn * cw_ref[2:3, :] + cb_ref[...])
    xs_ref[0] = y[:, :BRANCH]
    bc_ref[0] = y[:, BRANCH:]
    sm_ref[0] = _softplus(dt_ref[0] + dtb_ref[...])


def ssm_prep(p, lp, *, n_ctx):
    b, t, _ = p.shape
    tt = _pick_tile(n_ctx, ROW_TILE)
    dtb = jnp.pad(lp["ssm_dt_bias"].reshape(1, -1), ((0, 0), (0, LANES - 2 * SSM_HEADS)))
    specs = _halo_specs(tt, 768, 1, t) + [pl.BlockSpec((1, tt, LANES), lambda bi, i: (bi, i, 4)),
                                          _const_spec((3, 768)), _const_spec((1, 768)), _const_spec((1, LANES))]
    kern = functools.partial(_ssm_prep_kernel, nct=(t - n_ctx) // tt, nt=t // tt)
    return _prep_call(kern, (p, p, p, p, lp["ssm_conv_w"], _row(lp["ssm_conv_b"]), dtb), specs,
                      (BRANCH, 2 * SSM_GROUPS * SSM_N, LANES), b=b, t=t, tt=tt)


def _ssd_body(x_ref, bc_ref, sm_ref, na_ref, o_ref, s_ref, *, reverse):
    incl, _ = _chunk_masks(reverse)
    last = 0 if reverse else CHUNK - 1
    off = SSM_HEADS if reverse else 0
    dt_all = sm_ref[0]
    g_all = _chunk_cumsum(incl, dt_all * na_ref[...])
    yield
    expand = _expand_matrix(off, SSM_HEADS, SSM_P)
    gx = _select_columns(g_all, expand)
    dx = _select_columns(dt_all, expand)
    gt_all = _transpose_small(g_all)
    dtt_all = _transpose_small(dt_all)
    yield
    heads = range(SSM_HEADS)
    rep = SSM_HEADS // SSM_GROUPS
    gw = SSM_GROUPS * SSM_N
    hs = [slice(h * SSM_P, (h + 1) * SSM_P) for h in heads]
    glx = gx[last:last + 1, :]
    egx = jnp.exp(gx)
    wx = dx * jnp.exp(glx - gx)
    eglx = jnp.exp(glx)
    x = x_ref[0]
    bm = [bc_ref[0, :, grp * SSM_N:(grp + 1) * SSM_N] for grp in range(SSM_GROUPS)]
    cm = [bc_ref[0, :, gw + grp * SSM_N:gw + (grp + 1) * SSM_N] for grp in range(SSM_GROUPS)]
    cb = [_dot_nt(cm[grp], bm[grp]) for grp in range(SSM_GROUPS)]
    s = [s_ref[grp] for grp in range(SSM_GROUPS)]
    yield
    scores = [cb[h // rep] * jnp.exp(jnp.where(incl, gx[:, hs[h]] - gt_all[off + h:off + h + 1, :], -jnp.inf))
              * dtt_all[off + h:off + h + 1, :] for h in heads]
    yield
    intra = [_dot(scores[h], x[:, hs[h]]) for h in heads]
    yield
    inter = [_dot(cm[grp], s[grp]) for grp in range(SSM_GROUPS)]
    yield
    upd = [_dot_tn(bm[h // rep] * wx[:, hs[h]], x[:, hs[h]]) for h in heads]
    yield
    for h in heads:
        grp, ls = h // rep, slice((h % rep) * SSM_P, (h % rep + 1) * SSM_P)
        o_ref[0, :, hs[h]] = intra[h] + egx[:, hs[h]] * inter[grp][:, ls]
        s_ref[grp, :, ls] = s[grp][:, ls] * eglx[:, hs[h]] + upd[h]


def _expand_matrix(off, n_heads, width):
    row = lax.broadcasted_iota(jnp.int32, (LANES, n_heads * width), 0)
    col = lax.broadcasted_iota(jnp.int32, (LANES, n_heads * width), 1)
    lo = row * width - off * width
    return ((col >= lo) & (col < lo + width)).astype(BF16)


def ssd_scan(xs, bc, sm, neg_a, *, n_ctx):
    b, t, _ = xs.shape
    toks = [(xs, BRANCH, 0, 0), (bc, 2 * SSM_GROUPS * SSM_N, 0, 0), (sm, LANES, 0, 0)]
    state = (SSM_GROUPS, SSM_N, (SSM_HEADS // SSM_GROUPS) * SSM_P)
    return _bidir_scan(_ssd_body, toks, [(neg_a, neg_a)], state, b=b, t=t, n_ctx=n_ctx,
                       batch_block=_batch_block(b, 2))


def _gla_body(q_ref, k_ref, v_ref, glr_ref, w2_ref, gb_ref, o_ref, s_ref, *, reverse):
    incl, _ = _chunk_masks(reverse)
    last = 0 if reverse else CHUNK - 1
    logit = _dot_x3(glr_ref[0], w2_ref[...]) + gb_ref[...]
    yield
    la = -_softplus(-logit) * (1.0 / GLA_TAU)
    g_all = _chunk_cumsum(incl, la)
    yield
    heads = range(GLA_HEADS)
    ks = [slice(h * GLA_DK, (h + 1) * GLA_DK) for h in heads]
    vs = [slice(h * GLA_DV, (h + 1) * GLA_DV) for h in heads]
    g = [g_all[:, ks[h]] for h in heads]
    gl = [g[h][last:last + 1, :] for h in heads]
    k = [k_ref[0, :, ks[h]] for h in heads]
    v = [v_ref[0, :, vs[h]] for h in heads]
    qg = [q_ref[0, :, ks[h]] * (GLA_DK ** -0.5) * jnp.exp(g[h]) for h in heads]
    st = [s_ref[h] for h in heads]
    yield
    scores = [jnp.where(incl, _dot_nt(qg[h], k[h] * jnp.exp(-g[h])), 0.0) for h in heads]
    yield
    intra = [_dot(scores[h], v[h]) for h in heads]
    yield
    inter = [_dot_nt(qg[h], st[h]) for h in heads]
    yield
    upd = [_dot_tn(v[h], k[h] * jnp.exp(gl[h] - g[h])) for h in heads]
    yield
    for h in heads:
        o_ref[0, :, vs[h]] = intra[h] + inter[h]
        s_ref[h] = st[h] * jnp.exp(gl[h]) + upd[h]


def gla_scan(p, w2_pair, gb_pair, *, n_ctx):
    b, t, _ = p.shape
    kwid = GLA_HEADS * GLA_DK
    toks = [(p, kwid, 0, 0), (p, kwid, 1, 1), (p, BRANCH, 1, 1), (p, LANES, 12, 12)]
    return _bidir_scan(_gla_body, toks, [w2_pair, gb_pair], (GLA_HEADS, GLA_DV, GLA_DK), b=b, t=t, n_ctx=n_ctx,
                       batch_block=_batch_block(b, 4))


def _rwkv_prep_kernel(x_ref, xp_ref, xn_ref, mu_ref, w2_ref, w0_ref, a2_ref, a0_ref, g2_ref, kk_ref_w,
                      ka_ref, rk_ref, r_ref, k_ref, v_ref, kk_ref, a_ref, lw_ref, g_ref, bo_ref,
                      *, nct, nt):
    x = x_ref[0]
    xp, xn = _neighbours(x, xp_ref[0], xn_ref[0], nct=nct, nt=nt)
    x = x + mu_ref[...] * (0.5 * (xp + xn) - x)
    r, k, v = x[:, :BRANCH], x[:, BRANCH:2 * BRANCH], x[:, 2 * BRANCH:3 * BRANCH]
    wlr = x[:, 3 * BRANCH:3 * BRANCH + LANES]
    alr = x[:, 3 * BRANCH + LANES:3 * BRANCH + 2 * LANES]
    glr = x[:, 3 * BRANCH + 2 * LANES:]
    w_raw = _dot_x3(jnp.tanh(wlr), w2_ref[...]) + w0_ref[...]
    lw_ref[0] = _sigmoid(w_raw) * (-float(np.exp(-0.5)))
    a = _sigmoid(_dot_x3(alr, a2_ref[...]) + a0_ref[...])
    a_ref[0] = a
    g_ref[0] = _dot_x3(_sigmoid(glr), g2_ref[...])
    kk = k * kk_ref_w[...]
    kk_ref[0] = kk * lax.rsqrt(_group_sums(kk * kk, RWKV_N) + EPS)
    ksum = k * (2.0 + (a[:, :BRANCH] + a[:, BRANCH:] - 2.0) * ka_ref[...])
    bo_ref[0] = _group_sums(r * ksum * rk_ref[...], RWKV_N) * v
    r_ref[0] = r
    k_ref[0] = k
    v_ref[0] = v


def rwkv_prep(p, lp, *, n_ctx):
    b, t, w = p.shape
    tt = _pick_tile(n_ctx, ROW_TILE)

    def pair(wp):
        r, c = wp.shape[1:]
        return jnp.zeros((LANES, 2 * c), F32).at[:r, :c].set(wp[0]).at[r:2 * r, c:].set(wp[1])

    consts = (_row(lp["rwkv_mu"]), pair(lp["rwkv_w2"]), _row(lp["rwkv_w0"]), pair(lp["rwkv_a2"]),
              _row(lp["rwkv_a0"]), lp["rwkv_g2"], _row(lp["rwkv_k_k"]), _row(lp["rwkv_k_a"]),
              _row(lp["rwkv_r_k"]))
    specs = _halo_specs(tt, w, 0, t) + [_const_spec(c.shape) for c in consts]
    kern = functools.partial(_rwkv_prep_kernel, nct=(t - n_ctx) // tt, nt=t // tt)
    return _prep_call(kern, (p, p, p) + consts, specs,
                      (BRANCH, BRANCH, BRANCH, BRANCH, 2 * BRANCH, 2 * BRANCH, BRANCH, BRANCH),
                      b=b, t=t, tt=tt)


def _rwkv_body(r_ref, k_ref, v_ref, kk_ref, a_ref, lw_ref, ka_ref, o_ref, s_ref, *, reverse):
    incl, strict = _chunk_masks(reverse)
    last = 0 if reverse else CHUNK - 1
    lw_all = lw_ref[0]
    g_all = _chunk_cumsum(incl, lw_all)
    a_all = a_ref[0]
    k_all = k_ref[0] * (1.0 + (a_all - 1.0) * ka_ref[...])
    yield
    heads = range(RWKV_HEADS)
    hs = [slice(h * RWKV_N, (h + 1) * RWKV_N) for h in heads]
    g = [g_all[:, hs[h]] for h in heads]
    gl = [g[h][last:last + 1, :] for h in heads]
    eneg = [jnp.exp(-g[h]) for h in heads]
    edec = [jnp.exp(gl[h] - g[h]) for h in heads]
    kk = [kk_ref[0, :, hs[h]] for h in heads]
    bvec = [kk[h] * a_all[:, hs[h]] for h in heads]
    k = [k_all[:, hs[h]] for h in heads]
    v = [v_ref[0, :, hs[h]] for h in heads]
    kkg = [kk[h] * jnp.exp(g[h] - lw_all[:, hs[h]]) for h in heads]
    rg = [r_ref[0, :, hs[h]] * jnp.exp(g[h]) for h in heads]
    bh = [bvec[h] * eneg[h] for h in heads]
    kh = [k[h] * eneg[h] for h in heads]
    s = [s_ref[h] for h in heads]
    yield
    both = [jnp.concatenate([kkg[h], rg[h]], axis=0) for h in heads]
    mask2 = jnp.concatenate([strict, incl], axis=0)
    mb = [jnp.where(mask2, _dot_nt(both[h], bh[h]), 0.0) for h in heads]
    yield
    mk = [jnp.where(mask2, _dot_nt(both[h], kh[h]), 0.0) for h in heads]
    yield
    part = [_dot(mk[h], v[h]) + _dot_nt(both[h], s[h]) for h in heads]
    yield
    x = yield from _unit_tri_solve([mb[h][:CHUNK] for h in heads], [part[h][:CHUNK] for h in heads])
    u = [-xh for xh in x]
    for h in heads:
        o_ref[0, :, hs[h]] = part[h][CHUNK:] + _dot(mb[h][CHUNK:], u[h])
    yield
    for h in heads:
        upd = _dot_tn(jnp.concatenate([u[h], v[h]], axis=0),
                      jnp.concatenate([bvec[h] * edec[h], k[h] * edec[h]], axis=0))
        s_ref[h] = s[h] * jnp.exp(gl[h]) + upd


def rwkv_scan(r, k, v, kk, a, lw, k_a, *, n_ctx):
    b, t, _ = r.shape
    toks = [(r, BRANCH, 0, 0), (k, BRANCH, 0, 0), (v, BRANCH, 0, 0), (kk, BRANCH, 0, 0),
            (a, BRANCH, 0, 1), (lw, BRANCH, 0, 1)]
    return _bidir_scan(_rwkv_body, toks, [(k_a, k_a)], (RWKV_HEADS, RWKV_N, RWKV_N), b=b, t=t, n_ctx=n_ctx,
                       batch_block=_batch_block(b, 2))


def _gdn_prep_kernel(x_ref, xp_ref, xn_ref, ab_ref, cw_ref, na_ref, dtb_ref,
                     q_ref, k_ref, v_ref, sm_ref, *, nct, nt):
    x = x_ref[0]
    xp, xn = _neighbours(x, xp_ref[0], xn_ref[0], nct=nct, nt=nt)
    y = _silu(xp * cw_ref[0:1, :] + x * cw_ref[1:2, :] + xn * cw_ref[2:3, :])
    q, k = y[:, :BRANCH], y[:, BRANCH:2 * BRANCH]
    q_ref[0] = q * lax.rsqrt(_group_sums(q * q, GDN_N) + EPS) * (GDN_N ** -0.5)
    k_ref[0] = k * lax.rsqrt(_group_sums(k * k, GDN_N) + EPS)
    v_ref[0] = y[:, 2 * BRANCH:]
    ab = ab_ref[0]
    lane = lax.broadcasted_iota(jnp.int32, ab.shape, 1)
    sm_ref[0] = jnp.where(lane < 2 * GDN_HEADS, na_ref[...] * _softplus(ab + dtb_ref[...]), _sigmoid(ab))


def gdn_prep(p, lp, *, n_ctx):
    b, t, _ = p.shape
    tt = _pick_tile(n_ctx, ROW_TILE)
    padrow = lambda v: jnp.pad(v.reshape(1, -1), ((0, 0), (0, LANES - 2 * GDN_HEADS)))
    consts = (lp["gdn_conv_w"], padrow(-jnp.exp(lp["gdn_a_log"])), padrow(lp["gdn_dt_bias"]))
    specs = (_halo_specs(tt, 3 * BRANCH, 0, t) + [pl.BlockSpec((1, tt, LANES), lambda bi, i: (bi, i, 16))]
             + [_const_spec(c.shape) for c in consts])
    kern = functools.partial(_gdn_prep_kernel, nct=(t - n_ctx) // tt, nt=t // tt)
    return _prep_call(kern, (p, p, p, p) + consts, specs, (BRANCH, BRANCH, BRANCH, LANES), b=b, t=t, tt=tt)


def _gdn_body(q_ref, k_ref, v_ref, sm_ref, o_ref, s_ref, *, reverse):
    incl, strict = _chunk_masks(reverse)
    last = 0 if reverse else CHUNK - 1
    off = GDN_HEADS if reverse else 0
    sm = sm_ref[0]
    g_all = _chunk_cumsum(incl, sm)
    yield
    gt_all = _transpose_small(g_all)
    yield
    heads = range(GDN_HEADS)
    hs = [slice(h * GDN_N, (h + 1) * GDN_N) for h in heads]
    g = [g_all[:, off + h:off + h + 1] for h in heads]
    gl = [g[h][last:last + 1, :] for h in heads]
    beta = [sm[:, 2 * GDN_HEADS + off + h:2 * GDN_HEADS + off + h + 1] for h in heads]
    q = [q_ref[0, :, hs[h]] for h in heads]
    k = [k_ref[0, :, hs[h]] for h in heads]
    v = [v_ref[0, :, hs[h]] for h in heads]
    s = [s_ref[h] for h in heads]
    decay = [jnp.exp(jnp.where(incl, g[h] - gt_all[off + h:off + h + 1, :], -jnp.inf)) for h in heads]
    yield
    kq = [_dot_nt(jnp.concatenate([k[h], q[h]], axis=0), k[h]) for h in heads]
    yield
    lower = [jnp.where(strict, kq[h][:CHUNK] * decay[h] * beta[h], 0.0) for h in heads]
    attn = [kq[h][CHUNK:] * decay[h] for h in heads]
    o_part = [_dot(q[h] * jnp.exp(g[h]), s[h]) for h in heads]
    yield
    rhs = [jnp.concatenate([v[h] * beta[h], k[h] * (beta[h] * jnp.exp(g[h]))], axis=1) for h in heads]
    sol = yield from _unit_tri_solve(lower, rhs, precise_levels=2, explicit=True)
    v_new = [sol[h][:, :GDN_N] - _dot(sol[h][:, GDN_N:], s[h]) for h in heads]
    yield
    for h in heads:
        o_ref[0, :, hs[h]] = o_part[h] + _dot(attn[h], v_new[h])
    yield
    for h in heads:
        s_ref[h] = s[h] * jnp.exp(gl[h]) + _dot_tn(k[h] * jnp.exp(gl[h] - g[h]), v_new[h])


def gdn_scan(q, k, v, sm, *, n_ctx):
    b, t, _ = q.shape
    toks = [(q, BRANCH, 0, 0), (k, BRANCH, 0, 0), (v, BRANCH, 0, 0), (sm, LANES, 0, 0)]
    return _bidir_scan(_gdn_body, toks, [], (GDN_HEADS, GDN_N, GDN_N), b=b, t=t, n_ctx=n_ctx,
                       batch_block=_batch_block(b, 4))


def _merge_kernel(sf_ref, sb_ref, sx_ref, sz_ref, gf_ref, gb_ref, gr_ref, rf_ref, rb_ref, rg_ref, rbo_ref,
                  df_ref, db_ref, dz_ref, gate_ref, x_ref, m_ref,
                  sd_ref, sn_ref, gn_ref, lnw_ref, lnb_ref, dn_ref,
                  wb_ref, wo_ref, o_ref):
    def group_rms(y, n, w_ref):
        return y * lax.rsqrt(_group_sums(y * y, n) * (1.0 / n) + EPS) * w_ref[...]

    y = (sf_ref[0] + sb_ref[0] + sd_ref[...] * sx_ref[0]) * _silu(sz_ref[0])
    ys = group_rms(y, BRANCH // SSM_GROUPS, sn_ref)
    yg = group_rms(gf_ref[0] + gb_ref[0], GLA_DV, gn_ref) * _silu(gr_ref[0])
    y = rf_ref[0] + rb_ref[0]
    yc = y - _group_sums(y, RWKV_N) * (1.0 / RWKV_N)
    var = _group_sums(yc * yc, RWKV_N) * (1.0 / RWKV_N)
    yr = (yc * lax.rsqrt(var + RWKV_LN_EPS) * lnw_ref[...] + lnb_ref[...] + rbo_ref[0]) * rg_ref[0]
    yd = group_rms(df_ref[0] + db_ref[0], GDN_N, dn_ref) * _silu(dz_ref[0])
    acc = None
    for i, yi in enumerate((ys, yg, yr, yd)):
        term = gate_ref[0, :, i * D_MODEL:(i + 1) * D_MODEL].astype(F32) * _dot(yi, wb_ref[i])
        acc = term if acc is None else acc + term
    o_ref[0] = x_ref[0] + m_ref[0, 0] * _dot(acc, wo_ref[...])


def merge_residual(ssm, gla, rwkv, gdn, gates, x_all, gate_mod, lp, w_branch, w_out, *, n_ctx):
    b, t, d = x_all.shape
    tm = _pick_tile(n_ctx, ROW_TILE)
    tok = lambda bi, i: (bi, i, 0)
    blk = lambda c: pl.BlockSpec((1, tm, BRANCH), lambda bi, i: (bi, i, c))
    half = blk(0)
    consts = (_row(jnp.repeat(lp["ssm_d"], SSM_P)), _row(lp["ssm_norm"]),
              _row(jnp.tile(lp["gla_norm"], GLA_HEADS)), _row(lp["rwkv_ln_w"]), _row(lp["rwkv_ln_b"]),
              _row(jnp.tile(lp["gdn_norm"], GDN_HEADS)), w_branch, w_out)
    ins = (ssm[0], ssm[1], ssm[2], ssm[3], gla[0], gla[1], gla[2], rwkv[0], rwkv[1], rwkv[2], rwkv[3],
           gdn[0], gdn[1], gdn[2], gates, x_all, gate_mod) + consts
    specs = ([half, half, half, blk(0), half, half, blk(2), half, half, half, half, half, half, blk(3),
              pl.BlockSpec((1, tm, 4 * d), tok), pl.BlockSpec((1, tm, d), tok),
              pl.BlockSpec((1, 1, 1, d), _mod_sel((t - n_ctx) // tm))]
             + [_const_spec(c.shape) for c in consts])
    return pl.pallas_call(
        _merge_kernel,
        grid=(b, t // tm),
        in_specs=specs,
        out_specs=pl.BlockSpec((1, tm, d), tok),
        out_shape=jax.ShapeDtypeStruct((b, t, d), F32),
        compiler_params=_cparams("parallel", "parallel"),
    )(*ins)


def _route_kernel(x_ref, nw_ref, shift_ref, scale_ref, rw_ref, rb_ref, u_ref, o_ref, cnt_ref, hb_ref, *, rows_kw):
    x = x_ref[...]
    shift, scale = _token_rows([shift_ref, scale_ref], pl.program_id(0), x.shape[0], **rows_kw)
    h = x * lax.rsqrt(jnp.mean(x * x, axis=-1, keepdims=True) + EPS) * nw_ref[...] * (1.0 + scale) + shift
    hb_ref[...] = h.astype(BF16)
    logits = lax.dot_general(rw_ref[...], h, (((1,), (1,)), ((), ())),
                             precision=HI, preferred_element_type=F32)
    scores = _sigmoid(logits)
    sel = scores + rb_ref[...]
    rows = [sel[e:e + 1, :] for e in range(N_EXPERTS)]
    sc = [scores[e:e + 1, :] for e in range(N_EXPERTS)]

    def top2(vals):
        v1, i1 = vals[0], jnp.zeros(vals[0].shape, jnp.int32)
        for j in range(1, len(vals)):
            better = vals[j] > v1
            v1 = jnp.where(better, vals[j], v1)
            i1 = jnp.where(better, j, i1)
        v2 = jnp.where(i1 == 0, vals[1], vals[0])
        i2 = jnp.where(i1 == 0, 1, 0)
        for j in range(1, len(vals)):
            better = (vals[j] > v2) & (i1 != j)
            v2 = jnp.where(better, vals[j], v2)
            i2 = jnp.where(better, j, i2)
        return v1, i1, v2, i2

    gsum = []
    for grp in range(N_GROUPS):
        v1, _, v2, _ = top2(rows[grp * EXPERTS_PER_GROUP:(grp + 1) * EXPERTS_PER_GROUP])
        gsum.append(v1 + v2)
    best, gidx = gsum[0], jnp.zeros(gsum[0].shape, jnp.int32)
    for grp in range(1, N_GROUPS):
        better = gsum[grp] > best
        best = jnp.where(better, gsum[grp], best)
        gidx = jnp.where(better, grp, gidx)
    chosen, chosen_sc = [], []
    for j in range(EXPERTS_PER_GROUP):
        cj, sj = rows[j], sc[j]
        for grp in range(1, N_GROUPS):
            cj = jnp.where(gidx == grp, rows[grp * EXPERTS_PER_GROUP + j], cj)
            sj = jnp.where(gidx == grp, sc[grp * EXPERTS_PER_GROUP + j], sj)
        chosen.append(cj)
        chosen_sc.append(sj)
    _, i1, _, i2 = top2(chosen)
    w1, w2 = jnp.zeros_like(best), jnp.zeros_like(best)
    for j in range(EXPERTS_PER_GROUP):
        w1 = jnp.where(i1 == j, chosen_sc[j], w1)
        w2 = jnp.where(i2 == j, chosen_sc[j], w2)
    tot = w1 + w2
    w1, w2 = w1 / tot, w2 / tot
    tm = scores.shape[1]
    sub = lax.broadcasted_iota(jnp.int32, (SUBLANES, tm), 0)
    ind8 = jnp.zeros((SUBLANES, tm), F32)
    meta = jnp.zeros((SUBLANES, tm), F32)
    for j in range(EXPERTS_PER_GROUP):
        gate_j = jnp.where(i1 == j, w1, 0.0) + jnp.where(i2 == j, w2, 0.0)
        meta = jnp.where(sub == j, gate_j, meta)
    for grp in range(N_GROUPS):
        ind8 = jnp.where((sub == grp) & (gidx == grp), 1.0, ind8)
    before = jnp.dot(ind8.astype(BF16), u_ref[...], preferred_element_type=F32)
    rank = jnp.sum(ind8 * before, axis=0, keepdims=True)
    meta = jnp.where(sub == _META_GROUP, gidx.astype(F32), meta)
    meta = jnp.where(sub == _META_RANK, rank, meta)
    o_ref[...] = meta
    counts = jnp.sum(ind8, axis=1, keepdims=True)
    lane = lax.broadcasted_iota(jnp.int32, (SUBLANES, LANES), 1)
    row = lax.broadcasted_iota(jnp.int32, (SUBLANES, LANES), 0)
    cnt_ref[0] = jnp.broadcast_to(jnp.sum(jnp.where(lane == row, counts, 0.0), axis=0, keepdims=True),
                                  (SUBLANES, LANES)).astype(jnp.int32)


_META_GROUP, _META_RANK = EXPERTS_PER_GROUP, EXPERTS_PER_GROUP + 1


def moe_route(x_all, norm_w, shift_rows, scale_rows, router_w, router_b, *, n_ctx, tm):
    b, t, d = x_all.shape
    m = b * t
    upper = jnp.asarray(np.triu(np.ones((tm, tm), np.float32), 1), BF16)
    return pl.pallas_call(
        functools.partial(_route_kernel, rows_kw=dict(bsz=b, t_all=t, n_ctx=n_ctx)),
        grid=(m // tm,),
        in_specs=[pl.BlockSpec((tm, d), lambda i: (i, 0)),
                  _const_spec((1, d)), _const_spec(shift_rows.shape), _const_spec(scale_rows.shape),
                  pl.BlockSpec((N_EXPERTS, d), lambda i: (0, 0)),
                  pl.BlockSpec((N_EXPERTS, 1), lambda i: (0, 0)),
                  _const_spec((tm, tm))],
        out_specs=[pl.BlockSpec((SUBLANES, tm), lambda i: (0, i)),
                   pl.BlockSpec((1, SUBLANES, LANES), lambda i: (i, 0, 0)),
                   pl.BlockSpec((tm, d), lambda i: (i, 0))],
        out_shape=[jax.ShapeDtypeStruct((SUBLANES, m), F32),
                   jax.ShapeDtypeStruct((m // tm, SUBLANES, LANES), jnp.int32),
                   jax.ShapeDtypeStruct((m, d), BF16)],
        compiler_params=_cparams("parallel"),
    )(x_all.reshape(m, d), norm_w.reshape(1, d), shift_rows, scale_rows,
      router_w.T, router_b.reshape(N_EXPERTS, 1), upper)


MOE_TILE = 1024
MOE_SUB_ROWS = 256
MOE_TAIL_ROWS = 128


def _expert_kernel(cnt_ref, h_ref, mr_ref, mc_ref, wg_ref, wu_ref, wd_ref, x_ref, gate_ref, o_ref, *, rows_kw):
    i, grp = pl.program_id(0), pl.program_id(1)

    @pl.when(grp == 0)
    def _():
        o_ref[...] = jnp.zeros_like(o_ref)

    tm = h_ref.shape[0]
    count = cnt_ref[i * N_GROUPS + grp]
    grp_f = grp.astype(F32)
    sel_row = jnp.where(mr_ref[_META_GROUP:_META_GROUP + 1, :] == grp_f, mr_ref[_META_RANK:_META_RANK + 1, :], -1.0)
    sel_col = jnp.where(mc_ref[:, _META_GROUP:_META_GROUP + 1] == grp_f, mc_ref[:, _META_RANK:_META_RANK + 1], -1.0)
    gate_parts = _split3(mc_ref[...])

    def sub_block(first, rows):
        base = first.astype(F32)
        slot_r = lax.broadcasted_iota(jnp.int32, (rows, tm), 0).astype(F32)
        slot_c = lax.broadcasted_iota(jnp.int32, (tm, rows), 1).astype(F32)
        pick = (sel_row - base == slot_r).astype(BF16)
        put = (sel_col - base == slot_c).astype(BF16)
        xg = jnp.dot(pick, h_ref[...], preferred_element_type=F32).astype(BF16)
        gates = sum(jnp.dot(pick, p, preferred_element_type=F32) for p in gate_parts)
        y = jnp.zeros((rows, o_ref.shape[1]), F32)
        for e in range(EXPERTS_PER_GROUP):
            hid = _silu(_dot(xg, wg_ref[e])) * _dot(xg, wu_ref[e])
            y = y + _dot(gates[:, e:e + 1] * hid, wd_ref[e])
        o_ref[...] += jnp.dot(put, y.astype(BF16), preferred_element_type=F32)

    n_full = count // MOE_SUB_ROWS
    rem = count - n_full * MOE_SUB_ROWS
    n_main = n_full + (rem > MOE_TAIL_ROWS).astype(jnp.int32)

    def main_block(s, carry):
        sub_block(s * MOE_SUB_ROWS, MOE_SUB_ROWS)
        return carry

    lax.fori_loop(0, n_main, main_block, 0)

    @pl.when((rem > 0) & (rem <= MOE_TAIL_ROWS))
    def _():
        sub_block(n_full * MOE_SUB_ROWS, MOE_TAIL_ROWS)

    @pl.when(grp == N_GROUPS - 1)
    def _():
        (gate,) = _token_rows([gate_ref], i, tm, **rows_kw)
        o_ref[...] = x_ref[...] + gate * o_ref[...]


def moe_experts(hb, meta, counts, wg, wu, wd, x_all, gate_rows, *, n_ctx, tm):
    b, t, d = x_all.shape
    m = b * t
    tok = lambda i, g, cnt: (i, 0)
    grid_spec = pltpu.PrefetchScalarGridSpec(
        num_scalar_prefetch=1,
        grid=(m // tm, N_GROUPS),
        in_specs=[pl.BlockSpec((tm, d), tok),
                  pl.BlockSpec((SUBLANES, tm), lambda i, g, cnt: (0, i)),
                  pl.BlockSpec((tm, SUBLANES), tok),
                  pl.BlockSpec((EXPERTS_PER_GROUP, d, EXPERT_FF), lambda i, g, cnt: (g, 0, 0)),
                  pl.BlockSpec((EXPERTS_PER_GROUP, d, EXPERT_FF), lambda i, g, cnt: (g, 0, 0)),
                  pl.BlockSpec((EXPERTS_PER_GROUP, EXPERT_FF, d), lambda i, g, cnt: (g, 0, 0)),
                  pl.BlockSpec((tm, d), tok),
                  pl.BlockSpec(gate_rows.shape, lambda i, g, cnt: (0, 0))],
        out_specs=pl.BlockSpec((tm, d), tok))
    out = pl.pallas_call(
        functools.partial(_expert_kernel, rows_kw=dict(bsz=b, t_all=t, n_ctx=n_ctx)),
        grid_spec=grid_spec,
        out_shape=jax.ShapeDtypeStruct((m, d), F32),
        compiler_params=pltpu.CompilerParams(dimension_semantics=("parallel", "arbitrary"),
                                             vmem_limit_bytes=MOE_VMEM_LIMIT),
    )(counts[:, 0, :N_GROUPS].reshape(-1), hb, meta, meta.T, wg, wu, wd, x_all.reshape(m, d), gate_rows)
    return out.reshape(b, t, d)


def _token_rows(m_refs, tile, tm, *, bsz, t_all, n_ctx):
    row = tile * tm + lax.broadcasted_iota(jnp.int32, (tm, 1), 0)
    ctx = jnp.zeros((tm, 1), jnp.bool_)
    lat = []
    for bi in range(bsz):
        lo, split = bi * t_all, bi * t_all + t_all - n_ctx
        lat.append((row >= lo) & (row < split))
        ctx = ctx | ((row >= split) & (row < lo + t_all))
    out = []
    for m_ref in m_refs:
        v = jnp.where(ctx, m_ref[bsz:bsz + 1, :], 0.0)
        for bi in range(bsz):
            v = v + jnp.where(lat[bi], m_ref[bi:bi + 1, :], 0.0)
        out.append(v)
    return out


GRID_TILE_COLS = SUBLANES
GRID_TILE = GRID_W * GRID_TILE_COLS


def _grid_view(x_all, n_lat):
    b, t, d = x_all.shape
    rows = n_lat // GRID_W
    assert rows == GRID_W and t % GRID_W == 0
    last = GRID_W // GRID_TILE_COLS - 1
    spec = pl.BlockSpec((1, rows, GRID_TILE_COLS, d), lambda bi, i: (bi, 0, jnp.minimum(i, last), 0))
    return x_all.reshape(b, t // GRID_W, GRID_W, d), spec


def _grid_tile(xg_ref):
    return jnp.concatenate([xg_ref[0, :, j, :] for j in range(xg_ref.shape[2])], axis=0)


def _transpose_grid_kernel(xn_ref, xg_ref, o_ref, *, n_lat_tiles):
    i = pl.program_id(1)

    @pl.when(i < n_lat_tiles)
    def _():
        o_ref[0] = _grid_tile(xg_ref)

    @pl.when(i >= n_lat_tiles)
    def _():
        o_ref[0] = xn_ref[0]


def transpose_grid(x_all, *, n_ctx):
    b, t, d = x_all.shape
    xg, gspec = _grid_view(x_all, t - n_ctx)
    tok = pl.BlockSpec((1, GRID_TILE, d), lambda bi, i: (bi, i, 0))
    return pl.pallas_call(
        functools.partial(_transpose_grid_kernel, n_lat_tiles=(t - n_ctx) // GRID_TILE),
        grid=(b, pl.cdiv(t, GRID_TILE)),
        in_specs=[tok, gspec],
        out_specs=tok,
        out_shape=jax.ShapeDtypeStruct((b, t, d), x_all.dtype),
        compiler_params=_cparams("parallel", "parallel"),
    )(x_all, xg)


def _final_norm_kernel(x_ref, w_ref, o_ref, *, from_grid):
    x = _grid_tile(x_ref) if from_grid else x_ref[0]
    o_ref[0] = x * lax.rsqrt(jnp.mean(x * x, axis=-1, keepdims=True) + EPS) * w_ref[...]


def final_rms_norm(x_all, w, *, n_ctx, from_grid):
    b, t, d = x_all.shape
    n_lat = t - n_ctx
    tt = GRID_TILE
    tok = pl.BlockSpec((1, tt, d), lambda bi, i: (bi, i, 0))
    if from_grid:
        x_in, spec = _grid_view(x_all, n_lat)
    else:
        x_in, spec = x_all, tok
    return pl.pallas_call(
        functools.partial(_final_norm_kernel, from_grid=from_grid),
        grid=(b, n_lat // tt),
        in_specs=[spec, _const_spec((1, d))],
        out_specs=tok,
        out_shape=jax.ShapeDtypeStruct((b, n_lat, d), F32),
        compiler_params=_cparams("parallel", "parallel"),
    )(x_in, w.reshape(1, d))


def _pack_w_in(w_in, mixer):
    cols = _SRC_COLS[mixer]
    pieces, i = [], 0
    while i < len(cols):
        j = i
        if cols[i] < 0:
            while j < len(cols) and cols[j] < 0:
                j += 1
            pieces.append(jnp.zeros((w_in.shape[0], j - i), w_in.dtype))
        else:
            while j < len(cols) and cols[j] == cols[i] + (j - i):
                j += 1
            pieces.append(w_in[:, int(cols[i]):int(cols[i]) + (j - i)])
        i = j
    return jnp.concatenate(pieces, axis=1).astype(BF16)


def mixer_scans(ps, lp, *, n_ctx):
    p_ssm, p_gla, p_rwkv, p_gdn = ps

    xs, bc, sm = ssm_prep(p_ssm, lp, n_ctx=n_ctx)
    neg_a = jnp.pad(-jnp.exp(lp["ssm_a_log"]).reshape(1, -1), ((0, 0), (0, LANES - 2 * SSM_HEADS)))
    ssm = tuple(ssd_scan(xs, bc, sm, neg_a, n_ctx=n_ctx)) + (xs, p_ssm)

    w2 = [jnp.zeros((LANES, GLA_HEADS * GLA_DK), F32).at[d * GLA_RANK:(d + 1) * GLA_RANK].set(lp["gla_w2"][d])
          for d in range(2)]
    gb = [_row(lp["gla_b"][d]) for d in range(2)]
    gla = tuple(gla_scan(p_gla, w2, gb, n_ctx=n_ctx)) + (p_gla,)

    r, k, v, kk, a, lw, g, bonus = rwkv_prep(p_rwkv, lp, n_ctx=n_ctx)
    rwkv = tuple(rwkv_scan(r, k, v, kk, a, lw, _row(lp["rwkv_k_a"]), n_ctx=n_ctx)) + (g, bonus)

    q, kd, vd, smd = gdn_prep(p_gdn, lp, n_ctx=n_ctx)
    gdn = tuple(gdn_scan(q, kd, vd, smd, n_ctx=n_ctx)) + (p_gdn,)
    return ssm, gla, rwkv, gdn


def kernel(x, c, ctx, c_ctx, ada_w, ada_b, norm_mix, norm_ffn, w_in, w_gate, w_branch, w_out, ssm_conv_w, ssm_conv_b, ssm_a_log, ssm_dt_bias, ssm_d, ssm_norm, gla_w2, gla_b, gla_norm, rwkv_mu, rwkv_w0, rwkv_w2, rwkv_a0, rwkv_a2, rwkv_g2, rwkv_k_k, rwkv_k_a, rwkv_r_k, rwkv_ln_w, rwkv_ln_b, gdn_conv_w, gdn_a_log, gdn_dt_bias, gdn_norm, router_w, router_b, moe_w_gate, moe_w_up, moe_w_down, final_norm):
    bsz, seq, d = x.shape
    n_ctx = ctx.shape[1]
    t_all = n_ctx + seq
    m_all = bsz * t_all

    cond = jnp.concatenate([jax.nn.silu(c), jax.nn.silu(c_ctx)[None]], 0)
    cond = jnp.pad(cond, ((0, SUBLANES - cond.shape[0]), (0, 0)))
    mods, mod_rows = [], []
    for l in range(DEPTH):
        mod = pmatmul(cond, ada_w[l], tm=SUBLANES, tn=1024, precise=True) + ada_b[l]
        mod_rows.append(mod)
        lat = mod[:bsz].reshape(bsz, 6, d)
        cx = jnp.broadcast_to(mod[bsz].reshape(1, 6, d), (bsz, 6, d))
        mods.append(jnp.stack([cx, lat], axis=1))

    x_all = jnp.concatenate([x, ctx], axis=1)
    scan_order = False
    for l in range(DEPTH):
        if (l % 2 == 1) != scan_order:
            x_all = transpose_grid(x_all, n_ctx=n_ctx)
            scan_order = not scan_order
        lp = dict(ssm_conv_w=ssm_conv_w[l], ssm_conv_b=ssm_conv_b[l], ssm_a_log=ssm_a_log[l],
                  ssm_dt_bias=ssm_dt_bias[l], ssm_d=ssm_d[l], ssm_norm=ssm_norm[l],
                  gla_w2=gla_w2[l], gla_b=gla_b[l], gla_norm=gla_norm[l],
                  rwkv_mu=rwkv_mu[l], rwkv_w0=rwkv_w0[l], rwkv_w2=rwkv_w2[l], rwkv_a0=rwkv_a0[l],
                  rwkv_a2=rwkv_a2[l], rwkv_g2=rwkv_g2[l], rwkv_k_k=rwkv_k_k[l], rwkv_k_a=rwkv_k_a[l],
                  rwkv_r_k=rwkv_r_k[l], rwkv_ln_w=rwkv_ln_w[l], rwkv_ln_b=rwkv_ln_b[l],
                  gdn_conv_w=gdn_conv_w[l], gdn_a_log=gdn_a_log[l], gdn_dt_bias=gdn_dt_bias[l],
                  gdn_norm=gdn_norm[l])
        mod = mods[l]
        msel = lambda i: mod[:, :, i][:, :, None, :]

        h = norm_modulate(x_all, norm_mix[l], msel(0), msel(1), n_ctx=n_ctx)
        h2d = h.reshape(m_all, d)
        ps = []
        for mixer in ("ssm", "gla", "rwkv", "gdn"):
            wp = _pack_w_in(w_in[l], mixer)
            ps.append(pmatmul(h2d, wp, tm=512, tn=wp.shape[1]).reshape(bsz, t_all, wp.shape[1]))
        gates = pmatmul(h2d, w_gate[l].astype(BF16), tm=1024, tn=1024, act="sigmoid", out_dtype=BF16)
        gates = gates.reshape(bsz, t_all, 4 * d)

        ssm, gla, rwkv, gdn = mixer_scans(ps, lp, n_ctx=n_ctx)
        x_all = merge_residual(ssm, gla, rwkv, gdn, gates, x_all, msel(2), lp,
                               w_branch[l].astype(BF16), w_out[l].astype(BF16), n_ctx=n_ctx)

        tm_moe = _pick_tile(m_all, MOE_TILE)
        rows = lambda i: mod_rows[l][:, i * d:(i + 1) * d]
        meta, counts, hb = moe_route(x_all, norm_ffn[l], rows(3), rows(4), router_w, router_b,
                                     n_ctx=n_ctx, tm=tm_moe)
        x_all = moe_experts(hb, meta, counts, moe_w_gate[l].astype(BF16), moe_w_up[l].astype(BF16),
                            moe_w_down[l].astype(BF16), x_all, rows(5), n_ctx=n_ctx, tm=tm_moe)

    return final_rms_norm(x_all, final_norm, n_ctx=n_ctx, from_grid=scan_order)
```

```python
import functools
import itertools

import numpy as np
import jax
import jax.numpy as jnp
from jax import lax
from jax.experimental import pallas as pl
from jax.experimental.pallas import tpu as pltpu

F32 = jnp.float32
BF16 = jnp.bfloat16
HI = lax.Precision.HIGHEST

D_MODEL = 1024
DEPTH = 2
GRID_W = 64
CHUNK = 64
EPS = 1e-6
BRANCH = D_MODEL // 2
SSM_HEADS, SSM_P, SSM_GROUPS, SSM_N = 8, 64, 2, 64
GLA_HEADS, GLA_DK, GLA_DV, GLA_RANK, GLA_TAU = 4, 64, 128, 16, 16.0
RWKV_HEADS, RWKV_N, RWKV_LN_EPS = 8, 64, 64e-5
GDN_HEADS, GDN_N = 4, 128
N_EXPERTS, N_GROUPS, EXPERTS_PER_GROUP = 16, 4, 4
EXPERT_FF = D_MODEL // 2
LANES = 128
SUBLANES = 8
VMEM_LIMIT = 48 * 1024 * 1024
MOE_VMEM_LIMIT = 56 * 1024 * 1024
ROW_TILE = 256

_REF_BLOCKS = (
    ("ssm", "z", 512), ("ssm", "xbc", 768), ("ssm", "dt", 16),
    ("gla", "q", 256), ("gla", "k", 256), ("gla", "v", 512), ("gla", "r", 512), ("gla", "glr", 32),
    ("rwkv", "all", 1920),
    ("gdn", "qkv", 1536), ("gdn", "z", 512), ("gdn", "ab", 16),
)
_PACKED = {
    "ssm": (("z", 512), ("dt", 128), ("pad", 128), ("xbc", 768)),
    "gla": (("q", 256), ("k", 256), ("v", 512), ("r", 512), ("glr", 128)),
    "rwkv": (("all", 1920),),
    "gdn": (("qkv", 1536), ("z", 512), ("ab", 128)),
}


def _packed_columns():
    start, s = {}, 0
    for mixer, blk, w in _REF_BLOCKS:
        start[(mixer, blk)] = (s, w)
        s += w
    out = {}
    for mixer, blocks in _PACKED.items():
        cols = []
        for blk, wp in blocks:
            s0, w = start.get((mixer, blk), (0, 0))
            cols += list(range(s0, s0 + w)) + [-1] * (wp - w)
        out[mixer] = np.asarray(cols, np.int32)
    return out


_SRC_COLS = _packed_columns()


def _cparams(*sem):
    return pltpu.CompilerParams(dimension_semantics=sem, vmem_limit_bytes=VMEM_LIMIT)


def _dot(a, b):
    return jnp.dot(a.astype(BF16), b.astype(BF16), preferred_element_type=F32)


def _dot_nt(a, b):
    return lax.dot_general(a.astype(BF16), b.astype(BF16), (((1,), (1,)), ((), ())),
                           preferred_element_type=F32)


def _dot_tn(a, b):
    return lax.dot_general(a.astype(BF16), b.astype(BF16), (((0,), (0,)), ((), ())),
                           preferred_element_type=F32)


def _dot_hi(a, b):
    return jnp.dot(a, b, precision=HI, preferred_element_type=F32)


def _dot_x3(a, b):
    ah = a.astype(BF16)
    al = (a - ah.astype(F32)).astype(BF16)
    bh = b.astype(BF16)
    bl = (b - bh.astype(F32)).astype(BF16)
    f = lambda u, v: jnp.dot(u, v, preferred_element_type=F32)
    return f(ah, bh) + (f(ah, bl) + f(al, bh))


def _dot_x2(a, w):
    ah = a.astype(BF16)
    al = (a - ah.astype(F32)).astype(BF16)
    return jnp.dot(ah, w, preferred_element_type=F32) + jnp.dot(al, w, preferred_element_type=F32)


def _softplus(x):
    return jnp.maximum(x, 0.0) + jnp.log(1.0 + jnp.exp(-jnp.abs(x)))


def _sigmoid(x):
    return 1.0 / (1.0 + jnp.exp(-x))


def _silu(x):
    return x * _sigmoid(x)


def _pick_tile(m, pref):
    t = pref
    while m % t:
        t //= 2
    return t


def _group_sums(y, n):
    m = min(n, LANES)
    row = lax.broadcasted_iota(jnp.int32, (LANES, LANES), 0)
    col = lax.broadcasted_iota(jnp.int32, (LANES, LANES), 1)
    same = ((row // m) == (col // m)).astype(BF16)
    parts = [_dot_x2(y[:, j:j + LANES], same) for j in range(0, y.shape[1], LANES)]
    k = n // m
    if k > 1:
        parts = [sum(parts[g * k:(g + 1) * k]) for g in range(len(parts) // k) for _ in range(k)]
    return jnp.concatenate(parts, axis=1)


def _mm_kernel(a_ref, w_ref, o_ref, *, act, precise):
    if precise:
        r = _dot_hi(a_ref[...].astype(F32), w_ref[...].astype(F32))
    else:
        r = _dot(a_ref[...], w_ref[...])
    if act == "sigmoid":
        r = _sigmoid(r)
    o_ref[...] = r.astype(o_ref.dtype)


def pmatmul(a, w, *, tm, tn, act=None, precise=False, out_dtype=F32):
    m, k = a.shape
    tm = _pick_tile(m, tm)
    if w.ndim == 3:
        per = w.shape[2] // tn
        n = w.shape[0] * w.shape[2]
        w_spec = pl.BlockSpec((None, k, tn), lambda j, i: (j // per, 0, j % per))
    else:
        n = w.shape[1]
        w_spec = pl.BlockSpec((k, tn), lambda j, i: (0, j))
    assert tm % SUBLANES == 0 and n % tn == 0 and w.shape[-1] % tn == 0, (m, tm, w.shape, tn)
    return pl.pallas_call(
        functools.partial(_mm_kernel, act=act, precise=precise),
        grid=(n // tn, m // tm),
        in_specs=[pl.BlockSpec((tm, k), lambda j, i: (i, 0)), w_spec],
        out_specs=pl.BlockSpec((tm, tn), lambda j, i: (i, j)),
        out_shape=jax.ShapeDtypeStruct((m, n), out_dtype),
        compiler_params=_cparams("parallel", "parallel"),
    )(a, w)


def _norm_mod_kernel(x_ref, w_ref, shift_ref, scale_ref, o_ref):
    x = x_ref[0]
    y = x * lax.rsqrt(jnp.mean(x * x, axis=-1, keepdims=True) + EPS) * w_ref[...]
    o_ref[0] = (y * (1.0 + scale_ref[0, 0]) + shift_ref[0, 0]).astype(o_ref.dtype)


def _mod_sel(n_lat_tiles):
    return lambda bi, i, *_: (bi, jnp.where(i < n_lat_tiles, 1, 0), 0, 0)


def norm_modulate(x_all, w, shift, scale, *, n_ctx, out_dtype=BF16):
    b, t, d = x_all.shape
    tm = _pick_tile(n_ctx, ROW_TILE)
    assert t % tm == 0
    tok = lambda bi, i: (bi, i, 0)
    return pl.pallas_call(
        _norm_mod_kernel,
        grid=(b, t // tm),
        in_specs=[pl.BlockSpec((1, tm, d), tok),
                  pl.BlockSpec((1, d), lambda bi, i: (0, 0)),
                  pl.BlockSpec((1, 1, 1, d), _mod_sel((t - n_ctx) // tm)),
                  pl.BlockSpec((1, 1, 1, d), _mod_sel((t - n_ctx) // tm))],
        out_specs=pl.BlockSpec((1, tm, d), tok),
        out_shape=jax.ShapeDtypeStruct((b, t, d), out_dtype),
        compiler_params=_cparams("parallel", "parallel"),
    )(x_all, w.reshape(1, d), shift, scale)


def _row(v):
    return v.reshape(1, -1).astype(F32)


def _const_spec(shape):
    return pl.BlockSpec(shape, lambda *_: (0,) * len(shape))


def _tile_specs(tt, width, col):
    r8 = tt // SUBLANES
    main = pl.BlockSpec((1, tt, width), lambda bi, i: (bi, i, col))
    prev = pl.BlockSpec((1, SUBLANES, width), lambda bi, i: (bi, jnp.maximum(i * r8 - 1, 0), col))
    return main, prev, r8


def _halo_specs(tt, width, col, t):
    main, prev, r8 = _tile_specs(tt, width, col)
    last8 = t // SUBLANES - 1
    nxt = pl.BlockSpec((1, SUBLANES, width), lambda bi, i: (bi, jnp.minimum((i + 1) * r8, last8), col))
    return [main, prev, nxt]


def _neighbours(x, prev8, next8, *, nct, nt):
    i = pl.program_id(1)
    tt = x.shape[0]
    row = lax.broadcasted_iota(jnp.int32, x.shape, 0)
    first = (i == 0) | (i == nct)
    last = (i == nct - 1) | (i == nt - 1)
    pr = jnp.where(first, 0.0, prev8[SUBLANES - 1:SUBLANES, :])
    nx = jnp.where(last, 0.0, next8[0:1, :])
    xp = jnp.where(row == 0, pr, pltpu.roll(x, 1, 0))
    xn = jnp.where(row == tt - 1, nx, pltpu.roll(x, tt - 1, 0))
    return xp, xn


def _prep_call(kernel, ins, in_specs, out_widths, *, b, t, tt, out_dtype=F32):
    tok = lambda bi, i: (bi, i, 0)
    return pl.pallas_call(
        kernel,
        grid=(b, t // tt),
        in_specs=in_specs,
        out_specs=[pl.BlockSpec((1, tt, w), tok) for w in out_widths],
        out_shape=[jax.ShapeDtypeStruct((b, t, w), out_dtype) for w in out_widths],
        compiler_params=_cparams("parallel", "parallel"),
    )(*ins)


def _chunk_masks(reverse):
    row = lax.broadcasted_iota(jnp.int32, (CHUNK, CHUNK), 0)
    col = lax.broadcasted_iota(jnp.int32, (CHUNK, CHUNK), 1)
    if reverse:
        return col >= row, col > row
    return col <= row, col < row


def _chunk_order(i, n_ctx_chunks, n_chunks, reverse):
    n_lat_chunks = n_chunks - n_ctx_chunks
    if not reverse:
        return jnp.where(i < n_ctx_chunks, n_lat_chunks + i, i - n_ctx_chunks)
    return jnp.where(i < n_ctx_chunks, n_chunks - 1 - i, n_lat_chunks - 1 - (i - n_ctx_chunks))


def _split3(a):
    hi = a.astype(BF16)
    r = a - hi.astype(F32)
    mid = r.astype(BF16)
    return hi, mid, (r - mid.astype(F32)).astype(BF16)


def _transpose_small(x):
    row = lax.broadcasted_iota(jnp.int32, (LANES, LANES), 0)
    col = lax.broadcasted_iota(jnp.int32, (LANES, LANES), 1)
    eye = (row == col).astype(BF16)
    nt = lambda p: lax.dot_general(eye, p, (((1,), (1,)), ((), ())), preferred_element_type=F32)
    hi, mid, lo = _split3(x)
    return nt(hi) + (nt(mid) + nt(lo))


def _chunk_cumsum(incl, x):
    m = incl.astype(BF16)
    hi, mid, lo = _split3(x)
    f = lambda p: jnp.dot(m, p, preferred_element_type=F32)
    return f(hi) + (f(mid) + f(lo))


def _select_columns(x, sel):
    c = x.shape[0]
    y = jnp.dot(jnp.concatenate(_split3(x), axis=0), sel, preferred_element_type=F32)
    return y[:c] + (y[c:2 * c] + y[2 * c:])


def _unit_tri_solve(mats, rhs, precise_levels=-1, explicit=False):
    n = range(len(mats))
    if explicit:
        row = lax.broadcasted_iota(jnp.int32, (CHUNK, CHUNK), 0)
        col = lax.broadcasted_iota(jnp.int32, (CHUNK, CHUNK), 1)
        x = [(row == col).astype(F32) - mats[h] for h in n]
    else:
        first = _dot_x3 if precise_levels >= 0 else _dot
        x = [rhs[h] - first(mats[h], rhs[h]) for h in n]
    yield
    p = mats
    for level in range(int(np.log2(CHUNK)) - 1):
        dot = _dot_x3 if level < precise_levels else _dot
        p = [dot(p[h], p[h]) for h in n]
        yield
        x = [x[h] + (dot(x[h], p[h]) if explicit else dot(p[h], x[h])) for h in n]
        yield
    if explicit:
        x = [_dot_x3(x[h], rhs[h]) for h in n]
        yield
    return x


def _bidir_scan(body, tok_ins, const_ins, state_shape, *, b, t, n_ctx, lockstep=True, batch_block=1):
    nc, ncc = t // CHUNK, n_ctx // CHUNK
    nb = batch_block
    assert b % nb == 0

    def chunk_spec(width, col, reverse):
        return pl.BlockSpec((nb, CHUNK, width), lambda bi, i: (bi, _chunk_order(i, ncc, nc, reverse), col))

    def direction(reverse):
        d = int(reverse)
        specs = [chunk_spec(w, cols[d], reverse) for _, w, *cols in tok_ins]
        specs += [_const_spec(pair[d].shape) for pair in const_ins]
        return specs, [a for a, *_ in tok_ins] + [pair[d] for pair in const_ins]

    (spec_f, arg_f), (spec_b, arg_b) = direction(False), direction(True)
    n_tok, n_in = len(tok_ins), len(arg_f)

    def kern(*refs):
        o_f, o_b, s_f, s_b = refs[2 * n_in:]

        @pl.when(pl.program_id(1) == 0)
        def _():
            s_f[...] = jnp.zeros_like(s_f)
            s_b[...] = jnp.zeros_like(s_b)

        def one(j, ins, o_ref, s_ref, reverse):
            ins = [r.at[pl.ds(j, 1)] if k < n_tok else r for k, r in enumerate(ins)]
            return body(*ins, o_ref.at[pl.ds(j, 1)], s_ref.at[j], reverse=reverse)

        gens = []
        for j in range(nb):
            gens += [one(j, refs[:n_in], o_f, s_f, False), one(j, refs[n_in:2 * n_in], o_b, s_b, True)]
        if not lockstep:
            gens = [itertools.chain(*gens)]
        while gens:
            gens = [g for g in gens if next(g, _DONE) is not _DONE]

    return pl.pallas_call(
        kern,
        grid=(b // nb, nc),
        in_specs=spec_f + spec_b,
        out_specs=[chunk_spec(BRANCH, 0, False), chunk_spec(BRANCH, 0, True)],
        out_shape=[jax.ShapeDtypeStruct((b, t, BRANCH), F32)] * 2,
        scratch_shapes=[pltpu.VMEM((nb,) + tuple(state_shape), F32)] * 2,
        compiler_params=_cparams("parallel", "arbitrary"),
    )(*arg_f, *arg_b)


_DONE = object()


def _batch_block(b, pref):
    return pref if b % pref == 0 else 1


def _ssm_prep_kernel(x_ref, xp_ref, xn_ref, dt_ref, cw_ref, cb_ref, dtb_ref, xs_ref, bc_ref, sm_ref,
                     *, nct, nt):
    x = x_ref[0]
    xp, xn = _neighbours(x, xp_ref[0], xn_ref[0], nct=nct, nt=nt)
    y = _silu(xp * cw_ref[0:1, :] + x * cw_ref[1:2, :] + xn * cw_ref[2:3, :] + cb_ref[...])
    xs_ref[0] = y[:, :BRANCH]
    bc_ref[0] = y[:, BRANCH:]
    sm_ref[0] = _softplus(dt_ref[0] + dtb_ref[...])


def ssm_prep(p, lp, *, n_ctx):
    b, t, _ = p.shape
    tt = _pick_tile(n_ctx, ROW_TILE)
    dtb = jnp.pad(lp["ssm_dt_bias"].reshape(1, -1), ((0, 0), (0, LANES - 2 * SSM_HEADS)))
    specs = _halo_specs(tt, 768, 1, t) + [pl.BlockSpec((1, tt, LANES), lambda bi, i: (bi, i, 4)),
                                          _const_spec((3, 768)), _const_spec((1, 768)), _const_spec((1, LANES))]
    kern = functools.partial(_ssm_prep_kernel, nct=(t - n_ctx) // tt, nt=t // tt)
    return _prep_call(kern, (p, p, p, p, lp["ssm_conv_w"], _row(lp["ssm_conv_b"]), dtb), specs,
                      (BRANCH, 2 * SSM_GROUPS * SSM_N, LANES), b=b, t=t, tt=tt)


def _ssd_body(x_ref, bc_ref, sm_ref, na_ref, o_ref, s_ref, *, reverse):
    incl, _ = _chunk_masks(reverse)
    last = 0 if reverse else CHUNK - 1
    off = SSM_HEADS if reverse else 0
    dt_all = sm_ref[0]
    g_all = _chunk_cumsum(incl, dt_all * na_ref[...])
    yield
    expand = _expand_matrix(off, SSM_HEADS, SSM_P)
    gx = _select_columns(g_all, expand)
    dx = _select_columns(dt_all, expand)
    gt_all = _transpose_small(g_all)
    dtt_all = _transpose_small(dt_all)
    yield
    heads = range(SSM_HEADS)
    rep = SSM_HEADS // SSM_GROUPS
    gw = SSM_GROUPS * SSM_N
    hs = [slice(h * SSM_P, (h + 1) * SSM_P) for h in heads]
    glx = gx[last:last + 1, :]
    egx = jnp.exp(gx)
    wx = dx * jnp.exp(glx - gx)
    eglx = jnp.exp(glx)
    x = x_ref[0]
    bm = [bc_ref[0, :, grp * SSM_N:(grp + 1) * SSM_N] for grp in range(SSM_GROUPS)]
    cm = [bc_ref[0, :, gw + grp * SSM_N:gw + (grp + 1) * SSM_N] for grp in range(SSM_GROUPS)]
    cb = [_dot_nt(cm[grp], bm[grp]) for grp in range(SSM_GROUPS)]
    s = [s_ref[grp] for grp in range(SSM_GROUPS)]
    yield
    scores = [cb[h // rep] * jnp.exp(jnp.where(incl, gx[:, hs[h]] - gt_all[off + h:off + h + 1, :], -jnp.inf))
              * dtt_all[off + h:off + h + 1, :] for h in heads]
    yield
    intra = [_dot(scores[h], x[:, hs[h]]) for h in heads]
    yield
    inter = [_dot(cm[grp], s[grp]) for grp in range(SSM_GROUPS)]
    yield
    upd = [_dot_tn(bm[h // rep] * wx[:, hs[h]], x[:, hs[h]]) for h in heads]
    yield
    for h in heads:
        grp, ls = h // rep, slice((h % rep) * SSM_P, (h % rep + 1) * SSM_P)
        o_ref[0, :, hs[h]] = intra[h] + egx[:, hs[h]] * inter[grp][:, ls]
        s_ref[grp, :, ls] = s[grp][:, ls] * eglx[:, hs[h]] + upd[h]


def _expand_matrix(off, n_heads, width):
    row = lax.broadcasted_iota(jnp.int32, (LANES, n_heads * width), 0)
    col = lax.broadcasted_iota(jnp.int32, (LANES, n_heads * width), 1)
    lo = row * width - off * width
    return ((col >= lo) & (col < lo + width)).astype(BF16)


def ssd_scan(xs, bc, sm, neg_a, *, n_ctx):
    b, t, _ = xs.shape
    toks = [(xs, BRANCH, 0, 0), (bc, 2 * SSM_GROUPS * SSM_N, 0, 0), (sm, LANES, 0, 0)]
    state = (SSM_GROUPS, SSM_N, (SSM_HEADS // SSM_GROUPS) * SSM_P)
    return _bidir_scan(_ssd_body, toks, [(neg_a, neg_a)], state, b=b, t=t, n_ctx=n_ctx,
                       batch_block=_batch_block(b, 2))


def _gla_body(q_ref, k_ref, v_ref, glr_ref, w2_ref, gb_ref, o_ref, s_ref, *, reverse):
    incl, _ = _chunk_masks(reverse)
    last = 0 if reverse else CHUNK - 1
    logit = _dot_x3(glr_ref[0], w2_ref[...]) + gb_ref[...]
    yield
    la = -_softplus(-logit) * (1.0 / GLA_TAU)
    g_all = _chunk_cumsum(incl, la)
    yield
    heads = range(GLA_HEADS)
    ks = [slice(h * GLA_DK, (h + 1) * GLA_DK) for h in heads]
    vs = [slice(h * GLA_DV, (h + 1) * GLA_DV) for h in heads]
    g = [g_all[:, ks[h]] for h in heads]
    gl = [g[h][last:last + 1, :] for h in heads]
    k = [k_ref[0, :, ks[h]] for h in heads]
    v = [v_ref[0, :, vs[h]] for h in heads]
    qg = [q_ref[0, :, ks[h]] * (GLA_DK ** -0.5) * jnp.exp(g[h]) for h in heads]
    st = [s_ref[h] for h in heads]
    yield
    scores = [jnp.where(incl, _dot_nt(qg[h], k[h] * jnp.exp(-g[h])), 0.0) for h in heads]
    yield
    intra = [_dot(scores[h], v[h]) for h in heads]
    yield
    inter = [_dot_nt(qg[h], st[h]) for h in heads]
    yield
    upd = [_dot_tn(v[h], k[h] * jnp.exp(gl[h] - g[h])) for h in heads]
    yield
    for h in heads:
        o_ref[0, :, vs[h]] = intra[h] + inter[h]
        s_ref[h] = st[h] * jnp.exp(gl[h]) + upd[h]


def gla_scan(p, w2_pair, gb_pair, *, n_ctx):
    b, t, _ = p.shape
    kwid = GLA_HEADS * GLA_DK
    toks = [(p, kwid, 0, 0), (p, kwid, 1, 1), (p, BRANCH, 1, 1), (p, LANES, 12, 12)]
    return _bidir_scan(_gla_body, toks, [w2_pair, gb_pair], (GLA_HEADS, GLA_DV, GLA_DK), b=b, t=t, n_ctx=n_ctx,
                       batch_block=_batch_block(b, 4))


def _rwkv_prep_kernel(x_ref, xp_ref, xn_ref, mu_ref, w2_ref, w0_ref, a2_ref, a0_ref, g2_ref, kk_ref_w,
                      ka_ref, rk_ref, r_ref, k_ref, v_ref, kk_ref, a_ref, lw_ref, g_ref, bo_ref,
                      *, nct, nt):
    x = x_ref[0]
    xp, xn = _neighbours(x, xp_ref[0], xn_ref[0], nct=nct, nt=nt)
    x = x + mu_ref[...] * (0.5 * (xp + xn) - x)
    r, k, v = x[:, :BRANCH], x[:, BRANCH:2 * BRANCH], x[:, 2 * BRANCH:3 * BRANCH]
    wlr = x[:, 3 * BRANCH:3 * BRANCH + LANES]
    alr = x[:, 3 * BRANCH + LANES:3 * BRANCH + 2 * LANES]
    glr = x[:, 3 * BRANCH + 2 * LANES:]
    w_raw = _dot_x3(jnp.tanh(wlr), w2_ref[...]) + w0_ref[...]
    lw_ref[0] = _sigmoid(w_raw) * (-float(np.exp(-0.5)))
    a = _sigmoid(_dot_x3(alr, a2_ref[...]) + a0_ref[...])
    a_ref[0] = a
    g_ref[0] = _dot_x3(_sigmoid(glr), g2_ref[...])
    kk = k * kk_ref_w[...]
    kk_ref[0] = kk * lax.rsqrt(_group_sums(kk * kk, RWKV_N) + EPS)
    ksum = k * (2.0 + (a[:, :BRANCH] + a[:, BRANCH:] - 2.0) * ka_ref[...])
    bo_ref[0] = _group_sums(r * ksum * rk_ref[...], RWKV_N) * v
    r_ref[0] = r
    k_ref[0] = k
    v_ref[0] = v


def rwkv_prep(p, lp, *, n_ctx):
    b, t, w = p.shape
    tt = _pick_tile(n_ctx, ROW_TILE)

    def pair(wp):
        r, c = wp.shape[1:]
        return jnp.zeros((LANES, 2 * c), F32).at[:r, :c].set(wp[0]).at[r:2 * r, c:].set(wp[1])

    consts = (_row(lp["rwkv_mu"]), pair(lp["rwkv_w2"]), _row(lp["rwkv_w0"]), pair(lp["rwkv_a2"]),
              _row(lp["rwkv_a0"]), lp["rwkv_g2"], _row(lp["rwkv_k_k"]), _row(lp["rwkv_k_a"]),
              _row(lp["rwkv_r_k"]))
    specs = _halo_specs(tt, w, 0, t) + [_const_spec(c.shape) for c in consts]
    kern = functools.partial(_rwkv_prep_kernel, nct=(t - n_ctx) // tt, nt=t // tt)
    return _prep_call(kern, (p, p, p) + consts, specs,
                      (BRANCH, BRANCH, BRANCH, BRANCH, 2 * BRANCH, 2 * BRANCH, BRANCH, BRANCH),
                      b=b, t=t, tt=tt)


def _rwkv_body(r_ref, k_ref, v_ref, kk_ref, a_ref, lw_ref, ka_ref, o_ref, s_ref, *, reverse):
    incl, strict = _chunk_masks(reverse)
    last = 0 if reverse else CHUNK - 1
    lw_all = lw_ref[0]
    g_all = _chunk_cumsum(incl, lw_all)
    a_all = a_ref[0]
    k_all = k_ref[0] * (1.0 + (a_all - 1.0) * ka_ref[...])
    yield
    heads = range(RWKV_HEADS)
    hs = [slice(h * RWKV_N, (h + 1) * RWKV_N) for h in heads]
    g = [g_all[:, hs[h]] for h in heads]
    gl = [g[h][last:last + 1, :] for h in heads]
    eneg = [jnp.exp(-g[h]) for h in heads]
    edec = [jnp.exp(gl[h] - g[h]) for h in heads]
    kk = [kk_ref[0, :, hs[h]] for h in heads]
    bvec = [kk[h] * a_all[:, hs[h]] for h in heads]
    k = [k_all[:, hs[h]] for h in heads]
    v = [v_ref[0, :, hs[h]] for h in heads]
    kkg = [kk[h] * jnp.exp(g[h] - lw_all[:, hs[h]]) for h in heads]
    rg = [r_ref[0, :, hs[h]] * jnp.exp(g[h]) for h in heads]
    bh = [bvec[h] * eneg[h] for h in heads]
    kh = [k[h] * eneg[h] for h in heads]
    s = [s_ref[h] for h in heads]
    yield
    both = [jnp.concatenate([kkg[h], rg[h]], axis=0) for h in heads]
    mask2 = jnp.concatenate([strict, incl], axis=0)
    mb = [jnp.where(mask2, _dot_nt(both[h], bh[h]), 0.0) for h in heads]
    yield
    mk = [jnp.where(mask2, _dot_nt(both[h], kh[h]), 0.0) for h in heads]
    yield
    part = [_dot(mk[h], v[h]) + _dot_nt(both[h], s[h]) for h in heads]
    yield
    x = yield from _unit_tri_solve([mb[h][:CHUNK] for h in heads], [part[h][:CHUNK] for h in heads])
    u = [-xh for xh in x]
    for h in heads:
        o_ref[0, :, hs[h]] = part[h][CHUNK:] + _dot(mb[h][CHUNK:], u[h])
    yield
    for h in heads:
        upd = _dot_tn(jnp.concatenate([u[h], v[h]], axis=0),
                      jnp.concatenate([bvec[h] * edec[h], k[h] * edec[h]], axis=0))
        s_ref[h] = s[h] * jnp.exp(gl[h]) + upd


def rwkv_scan(r, k, v, kk, a, lw, k_a, *, n_ctx):
    b, t, _ = r.shape
    toks = [(r, BRANCH, 0, 0), (k, BRANCH, 0, 0), (v, BRANCH, 0, 0), (kk, BRANCH, 0, 0),
            (a, BRANCH, 0, 1), (lw, BRANCH, 0, 1)]
    return _bidir_scan(_rwkv_body, toks, [(k_a, k_a)], (RWKV_HEADS, RWKV_N, RWKV_N), b=b, t=t, n_ctx=n_ctx,
                       batch_block=_batch_block(b, 2))


def _gdn_prep_kernel(x_ref, xp_ref, xn_ref, ab_ref, cw_ref, na_ref, dtb_ref,
                     q_ref, k_ref, v_ref, sm_ref, *, nct, nt):
    x = x_ref[0]
    xp, xn = _neighbours(x, xp_ref[0], xn_ref[0], nct=nct, nt=nt)
    y = _silu(xp * cw_ref[0:1, :] + x * cw_ref[1:2, :] + xn * cw_ref[2:3, :])
    q, k = y[:, :BRANCH], y[:, BRANCH:2 * BRANCH]
    q_ref[0] = q * lax.rsqrt(_group_sums(q * q, GDN_N) + EPS) * (GDN_N ** -0.5)
    k_ref[0] = k * lax.rsqrt(_group_sums(k * k, GDN_N) + EPS)
    v_ref[0] = y[:, 2 * BRANCH:]
    ab = ab_ref[0]
    lane = lax.broadcasted_iota(jnp.int32, ab.shape, 1)
    sm_ref[0] = jnp.where(lane < 2 * GDN_HEADS, na_ref[...] * _softplus(ab + dtb_ref[...]), _sigmoid(ab))


def gdn_prep(p, lp, *, n_ctx):
    b, t, _ = p.shape
    tt = _pick_tile(n_ctx, ROW_TILE)
    padrow = lambda v: jnp.pad(v.reshape(1, -1), ((0, 0), (0, LANES - 2 * GDN_HEADS)))
    consts = (lp["gdn_conv_w"], padrow(-jnp.exp(lp["gdn_a_log"])), padrow(lp["gdn_dt_bias"]))
    specs = (_halo_specs(tt, 3 * BRANCH, 0, t) + [pl.BlockSpec((1, tt, LANES), lambda bi, i: (bi, i, 16))]
             + [_const_spec(c.shape) for c in consts])
    kern = functools.partial(_gdn_prep_kernel, nct=(t - n_ctx) // tt, nt=t // tt)
    return _prep_call(kern, (p, p, p, p) + consts, specs, (BRANCH, BRANCH, BRANCH, LANES), b=b, t=t, tt=tt)


def _gdn_body(q_ref, k_ref, v_ref, sm_ref, o_ref, s_ref, *, reverse):
    incl, strict = _chunk_masks(reverse)
    last = 0 if reverse else CHUNK - 1
    off = GDN_HEADS if reverse else 0
    sm = sm_ref[0]
    g_all = _chunk_cumsum(incl, sm)
    yield
    gt_all = _transpose_small(g_all)
    yield
    heads = range(GDN_HEADS)
    hs = [slice(h * GDN_N, (h + 1) * GDN_N) for h in heads]
    g = [g_all[:, off + h:off + h + 1] for h in heads]
    gl = [g[h][last:last + 1, :] for h in heads]
    beta = [sm[:, 2 * GDN_HEADS + off + h:2 * GDN_HEADS + off + h + 1] for h in heads]
    q = [q_ref[0, :, hs[h]] for h in heads]
    k = [k_ref[0, :, hs[h]] for h in heads]
    v = [v_ref[0, :, hs[h]] for h in heads]
    s = [s_ref[h] for h in heads]
    decay = [jnp.exp(jnp.where(incl, g[h] - gt_all[off + h:off + h + 1, :], -jnp.inf)) for h in heads]
    yield
    kq = [_dot_nt(jnp.concatenate([k[h], q[h]], axis=0), k[h]) for h in heads]
    yield
    lower = [jnp.where(strict, kq[h][:CHUNK] * decay[h] * beta[h], 0.0) for h in heads]
    attn = [kq[h][CHUNK:] * decay[h] for h in heads]
    o_part = [_dot(q[h] * jnp.exp(g[h]), s[h]) for h in heads]
    yield
    rhs = [jnp.concatenate([v[h] * beta[h], k[h] * (beta[h] * jnp.exp(g[h]))], axis=1) for h in heads]
    sol = yield from _unit_tri_solve(lower, rhs, precise_levels=2, explicit=True)
    v_new = [sol[h][:, :GDN_N] - _dot(sol[h][:, GDN_N:], s[h]) for h in heads]
    yield
    for h in heads:
        o_ref[0, :, hs[h]] = o_part[h] + _dot(attn[h], v_new[h])
    yield
    for h in heads:
        s_ref[h] = s[h] * jnp.exp(gl[h]) + _dot_tn(k[h] * jnp.exp(gl[h] - g[h]), v_new[h])


def gdn_scan(q, k, v, sm, *, n_ctx):
    b, t, _ = q.shape
    toks = [(q, BRANCH, 0, 0), (k, BRANCH, 0, 0), (v, BRANCH, 0, 0), (sm, LANES, 0, 0)]
    return _bidir_scan(_gdn_body, toks, [], (GDN_HEADS, GDN_N, GDN_N), b=b, t=t, n_ctx=n_ctx,
                       batch_block=_batch_block(b, 4))


def _merge_kernel(sf_ref, sb_ref, sx_ref, sz_ref, gf_ref, gb_ref, gr_ref, rf_ref, rb_ref, rg_ref, rbo_ref,
                  df_ref, db_ref, dz_ref, gate_ref, x_ref, m_ref,
                  sd_ref, sn_ref, gn_ref, lnw_ref, lnb_ref, dn_ref,
                  wb_ref, wo_ref, o_ref):
    def group_rms(y, n, w_ref):
        return y * lax.rsqrt(_group_sums(y * y, n) * (1.0 / n) + EPS) * w_ref[...]

    y = (sf_ref[0] + sb_ref[0] + sd_ref[...] * sx_ref[0]) * _silu(sz_ref[0])
    ys = group_rms(y, BRANCH // SSM_GROUPS, sn_ref)
    yg = group_rms(gf_ref[0] + gb_ref[0], GLA_DV, gn_ref) * _silu(gr_ref[0])
    y = rf_ref[0] + rb_ref[0]
    yc = y - _group_sums(y, RWKV_N) * (1.0 / RWKV_N)
    var = _group_sums(yc * yc, RWKV_N) * (1.0 / RWKV_N)
    yr = (yc * lax.rsqrt(var + RWKV_LN_EPS) * lnw_ref[...] + lnb_ref[...] + rbo_ref[0]) * rg_ref[0]
    yd = group_rms(df_ref[0] + db_ref[0], GDN_N, dn_ref) * _silu(dz_ref[0])
    acc = None
    for i, yi in enumerate((ys, yg, yr, yd)):
        term = gate_ref[0, :, i * D_MODEL:(i + 1) * D_MODEL].astype(F32) * _dot(yi, wb_ref[i])
        acc = term if acc is None else acc + term
    o_ref[0] = x_ref[0] + m_ref[0, 0] * _dot(acc, wo_ref[...])


def merge_residual(ssm, gla, rwkv, gdn, gates, x_all, gate_mod, lp, w_branch, w_out, *, n_ctx):
    b, t, d = x_all.shape
    tm = _pick_tile(n_ctx, ROW_TILE)
    tok = lambda bi, i: (bi, i, 0)
    blk = lambda c: pl.BlockSpec((1, tm, BRANCH), lambda bi, i: (bi, i, c))
    half = blk(0)
    consts = (_row(jnp.repeat(lp["ssm_d"], SSM_P)), _row(lp["ssm_norm"]),
              _row(jnp.tile(lp["gla_norm"], GLA_HEADS)), _row(lp["rwkv_ln_w"]), _row(lp["rwkv_ln_b"]),
              _row(jnp.tile(lp["gdn_norm"], GDN_HEADS)), w_branch, w_out)
    ins = (ssm[0], ssm[1], ssm[2], ssm[3], gla[0], gla[1], gla[2], rwkv[0], rwkv[1], rwkv[2], rwkv[3],
           gdn[0], gdn[1], gdn[2], gates, x_all, gate_mod) + consts
    specs = ([half, half, half, blk(0), half, half, blk(2), half, half, half, half, half, half, blk(3),
              pl.BlockSpec((1, tm, 4 * d), tok), pl.BlockSpec((1, tm, d), tok),
              pl.BlockSpec((1, 1, 1, d), _mod_sel((t - n_ctx) // tm))]
             + [_const_spec(c.shape) for c in consts])
    return pl.pallas_call(
        _merge_kernel,
        grid=(b, t // tm),
        in_specs=specs,
        out_specs=pl.BlockSpec((1, tm, d), tok),
        out_shape=jax.ShapeDtypeStruct((b, t, d), F32),
        compiler_params=_cparams("parallel", "parallel"),
    )(*ins)


def _route_kernel(x_ref, nw_ref, shift_ref, scale_ref, rw_ref, rb_ref, u_ref, o_ref, cnt_ref, hb_ref, *, rows_kw):
    x = x_ref[...]
    shift, scale = _token_rows([shift_ref, scale_ref], pl.program_id(0), x.shape[0], **rows_kw)
    h = x * lax.rsqrt(jnp.mean(x * x, axis=-1, keepdims=True) + EPS) * nw_ref[...] * (1.0 + scale) + shift
    hb_ref[...] = h.astype(BF16)
    logits = lax.dot_general(rw_ref[...], h, (((1,), (1,)), ((), ())),
                             precision=HI, preferred_element_type=F32)
    scores = _sigmoid(logits)
    sel = scores + rb_ref[...]
    rows = [sel[e:e + 1, :] for e in range(N_EXPERTS)]
    sc = [scores[e:e + 1, :] for e in range(N_EXPERTS)]

    def top2(vals):
        v1, i1 = vals[0], jnp.zeros(vals[0].shape, jnp.int32)
        for j in range(1, len(vals)):
            better = vals[j] > v1
            v1 = jnp.where(better, vals[j], v1)
            i1 = jnp.where(better, j, i1)
        v2 = jnp.where(i1 == 0, vals[1], vals[0])
        i2 = jnp.where(i1 == 0, 1, 0)
        for j in range(1, len(vals)):
            better = (vals[j] > v2) & (i1 != j)
            v2 = jnp.where(better, vals[j], v2)
            i2 = jnp.where(better, j, i2)
        return v1, i1, v2, i2

    gsum = []
    for grp in range(N_GROUPS):
        v1, _, v2, _ = top2(rows[grp * EXPERTS_PER_GROUP:(grp + 1) * EXPERTS_PER_GROUP])
        gsum.append(v1 + v2)
    best, gidx = gsum[0], jnp.zeros(gsum[0].shape, jnp.int32)
    for grp in range(1, N_GROUPS):
        better = gsum[grp] > best
        best = jnp.where(better, gsum[grp], best)
        gidx = jnp.where(better, grp, gidx)
    chosen, chosen_sc = [], []
    for j in range(EXPERTS_PER_GROUP):
        cj, sj = rows[j], sc[j]
        for grp in range(1, N_GROUPS):
            cj = jnp.where(gidx == grp, rows[grp * EXPERTS_PER_GROUP + j], cj)
            sj = jnp.where(gidx == grp, sc[grp * EXPERTS_PER_GROUP + j], sj)
        chosen.append(cj)
        chosen_sc.append(sj)
    _, i1, _, i2 = top2(chosen)
    w1, w2 = jnp.zeros_like(best), jnp.zeros_like(best)
    for j in range(EXPERTS_PER_GROUP):
        w1 = jnp.where(i1 == j, chosen_sc[j], w1)
        w2 = jnp.where(i2 == j, chosen_sc[j], w2)
    tot = w1 + w2
    w1, w2 = w1 / tot, w2 / tot
    tm = scores.shape[1]
    sub = lax.broadcasted_iota(jnp.int32, (SUBLANES, tm), 0)
    ind8 = jnp.zeros((SUBLANES, tm), F32)
    meta = jnp.zeros((SUBLANES, tm), F32)
    for j in range(EXPERTS_PER_GROUP):
        gate_j = jnp.where(i1 == j, w1, 0.0) + jnp.where(i2 == j, w2, 0.0)
        meta = jnp.where(sub == j, gate_j, meta)
    for grp in range(N_GROUPS):
        ind8 = jnp.where((sub == grp) & (gidx == grp), 1.0, ind8)
    before = jnp.dot(ind8.astype(BF16), u_ref[...], preferred_element_type=F32)
    rank = jnp.sum(ind8 * before, axis=0, keepdims=True)
    meta = jnp.where(sub == _META_GROUP, gidx.astype(F32), meta)
    meta = jnp.where(sub == _META_RANK, rank, meta)
    o_ref[...] = meta
    counts = jnp.sum(ind8, axis=1, keepdims=True)
    lane = lax.broadcasted_iota(jnp.int32, (SUBLANES, LANES), 1)
    row = lax.broadcasted_iota(jnp.int32, (SUBLANES, LANES), 0)
    cnt_ref[0] = jnp.broadcast_to(jnp.sum(jnp.where(lane == row, counts, 0.0), axis=0, keepdims=True),
                                  (SUBLANES, LANES)).astype(jnp.int32)


_META_GROUP, _META_RANK = EXPERTS_PER_GROUP, EXPERTS_PER_GROUP + 1


def moe_route(x_all, norm_w, shift_rows, scale_rows, router_w, router_b, *, n_ctx, tm):
    b, t, d = x_all.shape
    m = b * t
    upper = jnp.asarray(np.triu(np.ones((tm, tm), np.float32), 1), BF16)
    return pl.pallas_call(
        functools.partial(_route_kernel, rows_kw=dict(bsz=b, t_all=t, n_ctx=n_ctx)),
        grid=(m // tm,),
        in_specs=[pl.BlockSpec((tm, d), lambda i: (i, 0)),
                  _const_spec((1, d)), _const_spec(shift_rows.shape), _const_spec(scale_rows.shape),
                  pl.BlockSpec((N_EXPERTS, d), lambda i: (0, 0)),
                  pl.BlockSpec((N_EXPERTS, 1), lambda i: (0, 0)),
                  _const_spec((tm, tm))],
        out_specs=[pl.BlockSpec((SUBLANES, tm), lambda i: (0, i)),
                   pl.BlockSpec((1, SUBLANES, LANES), lambda i: (i, 0, 0)),
                   pl.BlockSpec((tm, d), lambda i: (i, 0))],
        out_shape=[jax.ShapeDtypeStruct((SUBLANES, m), F32),
                   jax.ShapeDtypeStruct((m // tm, SUBLANES, LANES), jnp.int32),
                   jax.ShapeDtypeStruct((m, d), BF16)],
        compiler_params=_cparams("parallel"),
    )(x_all.reshape(m, d), norm_w.reshape(1, d), shift_rows, scale_rows,
      router_w.T, router_b.reshape(N_EXPERTS, 1), upper)


MOE_TILE = 1024
MOE_SUB_ROWS = 256
MOE_TAIL_ROWS = 128


def _expert_kernel(cnt_ref, h_ref, mr_ref, mc_ref, wg_ref, wu_ref, wd_ref, x_ref, gate_ref, o_ref, *, rows_kw):
    i, grp = pl.program_id(0), pl.program_id(1)

    @pl.when(grp == 0)
    def _():
        o_ref[...] = jnp.zeros_like(o_ref)

    tm = h_ref.shape[0]
    count = cnt_ref[i * N_GROUPS + grp]
    grp_f = grp.astype(F32)
    sel_row = jnp.where(mr_ref[_META_GROUP:_META_GROUP + 1, :] == grp_f, mr_ref[_META_RANK:_META_RANK + 1, :], -1.0)
    sel_col = jnp.where(mc_ref[:, _META_GROUP:_META_GROUP + 1] == grp_f, mc_ref[:, _META_RANK:_META_RANK + 1], -1.0)
    gate_parts = _split3(mc_ref[...])

    def sub_block(first, rows):
        base = first.astype(F32)
        slot_r = lax.broadcasted_iota(jnp.int32, (rows, tm), 0).astype(F32)
        slot_c = lax.broadcasted_iota(jnp.int32, (tm, rows), 1).astype(F32)
        pick = (sel_row - base == slot_r).astype(BF16)
        put = (sel_col - base == slot_c).astype(BF16)
        xg = jnp.dot(pick, h_ref[...], preferred_element_type=F32).astype(BF16)
        gates = sum(jnp.dot(pick, p, preferred_element_type=F32) for p in gate_parts)
        y = jnp.zeros((rows, o_ref.shape[1]), F32)
        for e in range(EXPERTS_PER_GROUP):
            hid = _silu(_dot(xg, wg_ref[e])) * _dot(xg, wu_ref[e])
            y = y + _dot(gates[:, e:e + 1] * hid, wd_ref[e])
        o_ref[...] += jnp.dot(put, y.astype(BF16), preferred_element_type=F32)

    n_full = count // MOE_SUB_ROWS
    rem = count - n_full * MOE_SUB_ROWS
    n_main = n_full + (rem > MOE_TAIL_ROWS).astype(jnp.int32)

    def main_block(s, carry):
        sub_block(s * MOE_SUB_ROWS, MOE_SUB_ROWS)
        return carry

    lax.fori_loop(0, n_main, main_block, 0)

    @pl.when((rem > 0) & (rem <= MOE_TAIL_ROWS))
    def _():
        sub_block(n_full * MOE_SUB_ROWS, MOE_TAIL_ROWS)

    @pl.when(grp == N_GROUPS - 1)
    def _():
        (gate,) = _token_rows([gate_ref], i, tm, **rows_kw)
        o_ref[...] = x_ref[...] + gate * o_ref[...]


def moe_experts(hb, meta, counts, wg, wu, wd, x_all, gate_rows, *, n_ctx, tm):
    b, t, d = x_all.shape
    m = b * t
    tok = lambda i, g, cnt: (i, 0)
    grid_spec = pltpu.PrefetchScalarGridSpec(
        num_scalar_prefetch=1,
        grid=(m // tm, N_GROUPS),
        in_specs=[pl.BlockSpec((tm, d), tok),
                  pl.BlockSpec((SUBLANES, tm), lambda i, g, cnt: (0, i)),
                  pl.BlockSpec((tm, SUBLANES), tok),
                  pl.BlockSpec((EXPERTS_PER_GROUP, d, EXPERT_FF), lambda i, g, cnt: (g, 0, 0)),
                  pl.BlockSpec((EXPERTS_PER_GROUP, d, EXPERT_FF), lambda i, g, cnt: (g, 0, 0)),
                  pl.BlockSpec((EXPERTS_PER_GROUP, EXPERT_FF, d), lambda i, g, cnt: (g, 0, 0)),
                  pl.BlockSpec((tm, d), tok),
                  pl.BlockSpec(gate_rows.shape, lambda i, g, cnt: (0, 0))],
        out_specs=pl.BlockSpec((tm, d), tok))
    out = pl.pallas_call(
        functools.partial(_expert_kernel, rows_kw=dict(bsz=b, t_all=t, n_ctx=n_ctx)),
        grid_spec=grid_spec,
        out_shape=jax.ShapeDtypeStruct((m, d), F32),
        compiler_params=pltpu.CompilerParams(dimension_semantics=("parallel", "arbitrary"),
                                             vmem_limit_bytes=MOE_VMEM_LIMIT),
    )(counts[:, 0, :N_GROUPS].reshape(-1), hb, meta, meta.T, wg, wu, wd, x_all.reshape(m, d), gate_rows)
    return out.reshape(b, t, d)


def _token_rows(m_refs, tile, tm, *, bsz, t_all, n_ctx):
    row = tile * tm + lax.broadcasted_iota(jnp.int32, (tm, 1), 0)
    ctx = jnp.zeros((tm, 1), jnp.bool_)
    lat = []
    for bi in range(bsz):
        lo, split = bi * t_all, bi * t_all + t_all - n_ctx
        lat.append((row >= lo) & (row < split))
        ctx = ctx | ((row >= split) & (row < lo + t_all))
    out = []
    for m_ref in m_refs:
        v = jnp.where(ctx, m_ref[bsz:bsz + 1, :], 0.0)
        for bi in range(bsz):
            v = v + jnp.where(lat[bi], m_ref[bi:bi + 1, :], 0.0)
        out.append(v)
    return out


GRID_TILE_COLS = SUBLANES
GRID_TILE = GRID_W * GRID_TILE_COLS


def _grid_view(x_all, n_lat):
    b, t, d = x_all.shape
    rows = n_lat // GRID_W
    assert rows == GRID_W and t % GRID_W == 0
    last = GRID_W // GRID_TILE_COLS - 1
    spec = pl.BlockSpec((1, rows, GRID_TILE_COLS, d), lambda bi, i: (bi, 0, jnp.minimum(i, last), 0))
    return x_all.reshape(b, t // GRID_W, GRID_W, d), spec


def _grid_tile(xg_ref):
    return jnp.concatenate([xg_ref[0, :, j, :] for j in range(xg_ref.shape[2])], axis=0)


def _transpose_grid_kernel(xn_ref, xg_ref, o_ref, *, n_lat_tiles):
    i = pl.program_id(1)

    @pl.when(i < n_lat_tiles)
    def _():
        o_ref[0] = _grid_tile(xg_ref)

    @pl.when(i >= n_lat_tiles)
    def _():
        o_ref[0] = xn_ref[0]


def transpose_grid(x_all, *, n_ctx):
    b, t, d = x_all.shape
    xg, gspec = _grid_view(x_all, t - n_ctx)
    tok = pl.BlockSpec((1, GRID_TILE, d), lambda bi, i: (bi, i, 0))
    return pl.pallas_call(
        functools.partial(_transpose_grid_kernel, n_lat_tiles=(t - n_ctx) // GRID_TILE),
        grid=(b, pl.cdiv(t, GRID_TILE)),
        in_specs=[tok, gspec],
        out_specs=tok,
        out_shape=jax.ShapeDtypeStruct((b, t, d), x_all.dtype),
        compiler_params=_cparams("parallel", "parallel"),
    )(x_all, xg)


def _final_norm_kernel(x_ref, w_ref, o_ref, *, from_grid):
    x = _grid_tile(x_ref) if from_grid else x_ref[0]
    o_ref[0] = x * lax.rsqrt(jnp.mean(x * x, axis=-1, keepdims=True) + EPS) * w_ref[...]


def final_rms_norm(x_all, w, *, n_ctx, from_grid):
    b, t, d = x_all.shape
    n_lat = t - n_ctx
    tt = GRID_TILE
    tok = pl.BlockSpec((1, tt, d), lambda bi, i: (bi, i, 0))
    if from_grid:
        x_in, spec = _grid_view(x_all, n_lat)
    else:
        x_in, spec = x_all, tok
    return pl.pallas_call(
        functools.partial(_final_norm_kernel, from_grid=from_grid),
        grid=(b, n_lat // tt),
        in_specs=[spec, _const_spec((1, d))],
        out_specs=tok,
        out_shape=jax.ShapeDtypeStruct((b, n_lat, d), F32),
        compiler_params=_cparams("parallel", "parallel"),
    )(x_in, w.reshape(1, d))


def _pack_w_in(w_in, mixer):
    cols = _SRC_COLS[mixer]
    pieces, i = [], 0
    while i < len(cols):
        j = i
        if cols[i] < 0:
            while j < len(cols) and cols[j] < 0:
                j += 1
            pieces.append(jnp.zeros((w_in.shape[0], j - i), w_in.dtype))
        else:
            while j < len(cols) and cols[j] == cols[i] + (j - i):
                j += 1
            pieces.append(w_in[:, int(cols[i]):int(cols[i]) + (j - i)])
        i = j
    return jnp.concatenate(pieces, axis=1).astype(BF16)


def mixer_scans(ps, lp, *, n_ctx):
    p_ssm, p_gla, p_rwkv, p_gdn = ps

    xs, bc, sm = ssm_prep(p_ssm, lp, n_ctx=n_ctx)
    neg_a = jnp.pad(-jnp.exp(lp["ssm_a_log"]).reshape(1, -1), ((0, 0), (0, LANES - 2 * SSM_HEADS)))
    ssm = tuple(ssd_scan(xs, bc, sm, neg_a, n_ctx=n_ctx)) + (xs, p_ssm)

    w2 = [jnp.zeros((LANES, GLA_HEADS * GLA_DK), F32).at[d * GLA_RANK:(d + 1) * GLA_RANK].set(lp["gla_w2"][d])
          for d in range(2)]
    gb = [_row(lp["gla_b"][d]) for d in range(2)]
    gla = tuple(gla_scan(p_gla, w2, gb, n_ctx=n_ctx)) + (p_gla,)

    r, k, v, kk, a, lw, g, bonus = rwkv_prep(p_rwkv, lp, n_ctx=n_ctx)
    rwkv = tuple(rwkv_scan(r, k, v, kk, a, lw, _row(lp["rwkv_k_a"]), n_ctx=n_ctx)) + (g, bonus)

    q, kd, vd, smd = gdn_prep(p_gdn, lp, n_ctx=n_ctx)
    gdn = tuple(gdn_scan(q, kd, vd, smd, n_ctx=n_ctx)) + (p_gdn,)
    return ssm, gla, rwkv, gdn


def kernel(x, c, ctx, c_ctx, ada_w, ada_b, norm_mix, norm_ffn, w_in, w_gate, w_branch, w_out, ssm_conv_w, ssm_conv_b, ssm_a_log, ssm_dt_bias, ssm_d, ssm_norm, gla_w2, gla_b, gla_norm, rwkv_mu, rwkv_w0, rwkv_w2, rwkv_a0, rwkv_a2, rwkv_g2, rwkv_k_k, rwkv_k_a, rwkv_r_k, rwkv_ln_w, rwkv_ln_b, gdn_conv_w, gdn_a_log, gdn_dt_bias, gdn_norm, router_w, router_b, moe_w_gate, moe_w_up, moe_w_down, final_norm):
    bsz, seq, d = x.shape
    n_ctx = ctx.shape[1]
    t_all = n_ctx + seq
    m_all = bsz * t_all

    cond = jnp.concatenate([jax.nn.silu(c), jax.nn.silu(c_ctx)[None]], 0)
    cond = jnp.pad(cond, ((0, SUBLANES - cond.shape[0]), (0, 0)))
    mods, mod_rows = [], []
    for l in range(DEPTH):
        mod = pmatmul(cond, ada_w[l], tm=SUBLANES, tn=1024, precise=True) + ada_b[l]
        mod_rows.append(mod)
        lat = mod[:bsz].reshape(bsz, 6, d)
        cx = jnp.broadcast_to(mod[bsz].reshape(1, 6, d), (bsz, 6, d))
        mods.append(jnp.stack([cx, lat], axis=1))

    x_all = jnp.concatenate([x, ctx], axis=1)
    scan_order = False
    for l in range(DEPTH):
        if (l % 2 == 1) != scan_order:
            x_all = transpose_grid(x_all, n_ctx=n_ctx)
            scan_order = not scan_order
        lp = dict(ssm_conv_w=ssm_conv_w[l], ssm_conv_b=ssm_conv_b[l], ssm_a_log=ssm_a_log[l],
                  ssm_dt_bias=ssm_dt_bias[l], ssm_d=ssm_d[l], ssm_norm=ssm_norm[l],
                  gla_w2=gla_w2[l], gla_b=gla_b[l], gla_norm=gla_norm[l],
                  rwkv_mu=rwkv_mu[l], rwkv_w0=rwkv_w0[l], rwkv_w2=rwkv_w2[l], rwkv_a0=rwkv_a0[l],
                  rwkv_a2=rwkv_a2[l], rwkv_g2=rwkv_g2[l], rwkv_k_k=rwkv_k_k[l], rwkv_k_a=rwkv_k_a[l],
                  rwkv_r_k=rwkv_r_k[l], rwkv_ln_w=rwkv_ln_w[l], rwkv_ln_b=rwkv_ln_b[l],
                  gdn_conv_w=gdn_conv_w[l], gdn_a_log=gdn_a_log[l], gdn_dt_bias=gdn_dt_bias[l],
                  gdn_norm=gdn_norm[l])
        mod = mods[l]
        msel = lambda i: mod[:, :, i][:, :, None, :]

        h = norm_modulate(x_all, norm_mix[l], msel(0), msel(1), n_ctx=n_ctx)
        h2d = h.reshape(m_all, d)
        ps = []
        for mixer in ("ssm", "gla", "rwkv", "gdn"):
            wp = _pack_w_in(w_in[l], mixer)
            ps.append(pmatmul(h2d, wp, tm=1024, tn=wp.shape[1]).reshape(bsz, t_all, wp.shape[1]))
        gates = pmatmul(h2d, w_gate[l].astype(BF16), tm=1024, tn=1024, act="sigmoid", out_dtype=BF16)
        gates = gates.reshape(bsz, t_all, 4 * d)

        ssm, gla, rwkv, gdn = mixer_scans(ps, lp, n_ctx=n_ctx)
        x_all = merge_residual(ssm, gla, rwkv, gdn, gates, x_all, msel(2), lp,
                               w_branch[l].astype(BF16), w_out[l].astype(BF16), n_ctx=n_ctx)

        tm_moe = _pick_tile(m_all, MOE_TILE)
        rows = lambda i: mod_rows[l][:, i * d:(i + 1) * d]
        meta, counts, hb = moe_route(x_all, norm_ffn[l], rows(3), rows(4), router_w, router_b,
                                     n_ctx=n_ctx, tm=tm_moe)
        x_all = moe_experts(hb, meta, counts, moe_w_gate[l].astype(BF16), moe_w_up[l].astype(BF16),
                            moe_w_down[l].astype(BF16), x_all, rows(5), n_ctx=n_ctx, tm=tm_moe)

    return final_rms_norm(x_all, final_norm, n_ctx=n_ctx, from_grid=scan_order)
```

```python
import functools
import itertools

import numpy as np
import jax
import jax.numpy as jnp
from jax import lax
from jax.experimental import pallas as pl
from jax.experimental.pallas import tpu as pltpu

F32 = jnp.float32
BF16 = jnp.bfloat16
HI = lax.Precision.HIGHEST

D_MODEL = 1024
DEPTH = 2
GRID_W = 64
CHUNK = 64
EPS = 1e-6
BRANCH = D_MODEL // 2
SSM_HEADS, SSM_P, SSM_GROUPS, SSM_N = 8, 64, 2, 64
GLA_HEADS, GLA_DK, GLA_DV, GLA_RANK, GLA_TAU = 4, 64, 128, 16, 16.0
RWKV_HEADS, RWKV_N, RWKV_LN_EPS = 8, 64, 64e-5
GDN_HEADS, GDN_N = 4, 128
N_EXPERTS, N_GROUPS, EXPERTS_PER_GROUP = 16, 4, 4
EXPERT_FF = D_MODEL // 2
LANES = 128
SUBLANES = 8
VMEM_LIMIT = 48 * 1024 * 1024
MOE_VMEM_LIMIT = 56 * 1024 * 1024
ROW_TILE = 256

_REF_BLOCKS = (
    ("ssm", "z", 512), ("ssm", "xbc", 768), ("ssm", "dt", 16),
    ("gla", "q", 256), ("gla", "k", 256), ("gla", "v", 512), ("gla", "r", 512), ("gla", "glr", 32),
    ("rwkv", "all", 1920),
    ("gdn", "qkv", 1536), ("gdn", "z", 512), ("gdn", "ab", 16),
)
_PACKED = {
    "ssm": (("z", 512), ("dt", 128), ("pad", 128), ("xbc", 768)),
    "gla": (("q", 256), ("k", 256), ("v", 512), ("r", 512), ("glr", 128)),
    "rwkv": (("all", 1920),),
    "gdn": (("qkv", 1536), ("z", 512), ("ab", 128)),
}


def _packed_columns():
    start, s = {}, 0
    for mixer, blk, w in _REF_BLOCKS:
        start[(mixer, blk)] = (s, w)
        s += w
    out = {}
    for mixer, blocks in _PACKED.items():
        cols = []
        for blk, wp in blocks:
            s0, w = start.get((mixer, blk), (0, 0))
            cols += list(range(s0, s0 + w)) + [-1] * (wp - w)
        out[mixer] = np.asarray(cols, np.int32)
    return out


_SRC_COLS = _packed_columns()


def _cparams(*sem):
    return pltpu.CompilerParams(dimension_semantics=sem, vmem_limit_bytes=VMEM_LIMIT)


def _dot(a, b):
    return jnp.dot(a.astype(BF16), b.astype(BF16), preferred_element_type=F32)


def _dot_nt(a, b):
    return lax.dot_general(a.astype(BF16), b.astype(BF16), (((1,), (1,)), ((), ())),
                           preferred_element_type=F32)


def _dot_tn(a, b):
    return lax.dot_general(a.astype(BF16), b.astype(BF16), (((0,), (0,)), ((), ())),
                           preferred_element_type=F32)


def _dot_hi(a, b):
    return jnp.dot(a, b, precision=HI, preferred_element_type=F32)


def _dot_x3(a, b):
    ah = a.astype(BF16)
    al = (a - ah.astype(F32)).astype(BF16)
    bh = b.astype(BF16)
    bl = (b - bh.astype(F32)).astype(BF16)
    f = lambda u, v: jnp.dot(u, v, preferred_element_type=F32)
    return f(ah, bh) + (f(ah, bl) + f(al, bh))


def _dot_x2(a, w):
    ah = a.astype(BF16)
    al = (a - ah.astype(F32)).astype(BF16)
    return jnp.dot(ah, w, preferred_element_type=F32) + jnp.dot(al, w, preferred_element_type=F32)


def _softplus(x):
    return jnp.maximum(x, 0.0) + jnp.log(1.0 + jnp.exp(-jnp.abs(x)))


def _sigmoid(x):
    return 1.0 / (1.0 + jnp.exp(-x))


def _silu(x):
    return x * _sigmoid(x)


def _pick_tile(m, pref):
    t = pref
    while m % t:
        t //= 2
    return t


def _group_sums(y, n):
    m = min(n, LANES)
    row = lax.broadcasted_iota(jnp.int32, (LANES, LANES), 0)
    col = lax.broadcasted_iota(jnp.int32, (LANES, LANES), 1)
    same = ((row // m) == (col // m)).astype(BF16)
    parts = [_dot_x2(y[:, j:j + LANES], same) for j in range(0, y.shape[1], LANES)]
    k = n // m
    if k > 1:
        parts = [sum(parts[g * k:(g + 1) * k]) for g in range(len(parts) // k) for _ in range(k)]
    return jnp.concatenate(parts, axis=1)


def _mm_kernel(a_ref, w_ref, o_ref, *, act, precise):
    if precise:
        r = _dot_hi(a_ref[...].astype(F32), w_ref[...].astype(F32))
    else:
        r = _dot(a_ref[...], w_ref[...])
    if act == "sigmoid":
        r = _sigmoid(r)
    o_ref[...] = r.astype(o_ref.dtype)


def pmatmul(a, w, *, tm, tn, act=None, precise=False, out_dtype=F32):
    m, k = a.shape
    tm = _pick_tile(m, tm)
    if w.ndim == 3:
        per = w.shape[2] // tn
        n = w.shape[0] * w.shape[2]
        w_spec = pl.BlockSpec((None, k, tn), lambda j, i: (j // per, 0, j % per))
    else:
        n = w.shape[1]
        w_spec = pl.BlockSpec((k, tn), lambda j, i: (0, j))
    assert tm % SUBLANES == 0 and n % tn == 0 and w.shape[-1] % tn == 0, (m, tm, w.shape, tn)
    return pl.pallas_call(
        functools.partial(_mm_kernel, act=act, precise=precise),
        grid=(n // tn, m // tm),
        in_specs=[pl.BlockSpec((tm, k), lambda j, i: (i, 0)), w_spec],
        out_specs=pl.BlockSpec((tm, tn), lambda j, i: (i, j)),
        out_shape=jax.ShapeDtypeStruct((m, n), out_dtype),
        compiler_params=_cparams("parallel", "parallel"),
    )(a, w)


def _norm_mod_kernel(x_ref, w_ref, shift_ref, scale_ref, o_ref):
    x = x_ref[0]
    y = x * lax.rsqrt(jnp.mean(x * x, axis=-1, keepdims=True) + EPS) * w_ref[...]
    o_ref[0] = (y * (1.0 + scale_ref[0, 0]) + shift_ref[0, 0]).astype(o_ref.dtype)


def _mod_sel(n_lat_tiles):
    return lambda bi, i, *_: (bi, jnp.where(i < n_lat_tiles, 1, 0), 0, 0)


def norm_modulate(x_all, w, shift, scale, *, n_ctx, out_dtype=BF16):
    b, t, d = x_all.shape
    tm = _pick_tile(n_ctx, ROW_TILE)
    assert t % tm == 0
    tok = lambda bi, i: (bi, i, 0)
    return pl.pallas_call(
        _norm_mod_kernel,
        grid=(b, t // tm),
        in_specs=[pl.BlockSpec((1, tm, d), tok),
                  pl.BlockSpec((1, d), lambda bi, i: (0, 0)),
                  pl.BlockSpec((1, 1, 1, d), _mod_sel((t - n_ctx) // tm)),
                  pl.BlockSpec((1, 1, 1, d), _mod_sel((t - n_ctx) // tm))],
        out_specs=pl.BlockSpec((1, tm, d), tok),
        out_shape=jax.ShapeDtypeStruct((b, t, d), out_dtype),
        compiler_params=_cparams("parallel", "parallel"),
    )(x_all, w.reshape(1, d), shift, scale)


def _row(v):
    return v.reshape(1, -1).astype(F32)


def _const_spec(shape):
    return pl.BlockSpec(shape, lambda *_: (0,) * len(shape))


def _tile_specs(tt, width, col):
    r8 = tt // SUBLANES
    main = pl.BlockSpec((1, tt, width), lambda bi, i: (bi, i, col))
    prev = pl.BlockSpec((1, SUBLANES, width), lambda bi, i: (bi, jnp.maximum(i * r8 - 1, 0), col))
    return main, prev, r8


def _halo_specs(tt, width, col, t):
    main, prev, r8 = _tile_specs(tt, width, col)
    last8 = t // SUBLANES - 1
    nxt = pl.BlockSpec((1, SUBLANES, width), lambda bi, i: (bi, jnp.minimum((i + 1) * r8, last8), col))
    return [main, prev, nxt]


def _neighbours(x, prev8, next8, *, nct, nt):
    i = pl.program_id(1)
    tt = x.shape[0]
    row = lax.broadcasted_iota(jnp.int32, x.shape, 0)
    first = (i == 0) | (i == nct)
    last = (i == nct - 1) | (i == nt - 1)
    pr = jnp.where(first, 0.0, prev8[SUBLANES - 1:SUBLANES, :])
    nx = jnp.where(last, 0.0, next8[0:1, :])
    xp = jnp.where(row == 0, pr, pltpu.roll(x, 1, 0))
    xn = jnp.where(row == tt - 1, nx, pltpu.roll(x, tt - 1, 0))
    return xp, xn


def _prep_call(kernel, ins, in_specs, out_widths, *, b, t, tt, out_dtype=F32):
    tok = lambda bi, i: (bi, i, 0)
    return pl.pallas_call(
        kernel,
        grid=(b, t // tt),
        in_specs=in_specs,
        out_specs=[pl.BlockSpec((1, tt, w), tok) for w in out_widths],
        out_shape=[jax.ShapeDtypeStruct((b, t, w), out_dtype) for w in out_widths],
        compiler_params=_cparams("parallel", "parallel"),
    )(*ins)


def _chunk_masks(reverse):
    row = lax.broadcasted_iota(jnp.int32, (CHUNK, CHUNK), 0)
    col = lax.broadcasted_iota(jnp.int32, (CHUNK, CHUNK), 1)
    if reverse:
        return col >= row, col > row
    return col <= row, col < row


def _chunk_order(i, n_ctx_chunks, n_chunks, reverse):
    n_lat_chunks = n_chunks - n_ctx_chunks
    if not reverse:
        return jnp.where(i < n_ctx_chunks, n_lat_chunks + i, i - n_ctx_chunks)
    return jnp.where(i < n_ctx_chunks, n_chunks - 1 - i, n_lat_chunks - 1 - (i - n_ctx_chunks))


def _split3(a):
    hi = a.astype(BF16)
    r = a - hi.astype(F32)
    mid = r.astype(BF16)
    return hi, mid, (r - mid.astype(F32)).astype(BF16)


def _transpose_small(x):
    row = lax.broadcasted_iota(jnp.int32, (LANES, LANES), 0)
    col = lax.broadcasted_iota(jnp.int32, (LANES, LANES), 1)
    eye = (row == col).astype(BF16)
    nt = lambda p: lax.dot_general(eye, p, (((1,), (1,)), ((), ())), preferred_element_type=F32)
    hi, mid, lo = _split3(x)
    return nt(hi) + (nt(mid) + nt(lo))


def _chunk_cumsum(incl, x):
    m = incl.astype(BF16)
    hi, mid, lo = _split3(x)
    f = lambda p: jnp.dot(m, p, preferred_element_type=F32)
    return f(hi) + (f(mid) + f(lo))


def _select_columns(x, sel):
    c = x.shape[0]
    y = jnp.dot(jnp.concatenate(_split3(x), axis=0), sel, preferred_element_type=F32)
    return y[:c] + (y[c:2 * c] + y[2 * c:])


def _unit_tri_solve(mats, rhs, precise_levels=-1, explicit=False):
    n = range(len(mats))
    if explicit:
        row = lax.broadcasted_iota(jnp.int32, (CHUNK, CHUNK), 0)
        col = lax.broadcasted_iota(jnp.int32, (CHUNK, CHUNK), 1)
        x = [(row == col).astype(F32) - mats[h] for h in n]
    else:
        first = _dot_x3 if precise_levels >= 0 else _dot
        x = [rhs[h] - first(mats[h], rhs[h]) for h in n]
    yield
    p = mats
    for level in range(int(np.log2(CHUNK)) - 1):
        dot = _dot_x3 if level < precise_levels else _dot
        p = [dot(p[h], p[h]) for h in n]
        yield
        x = [x[h] + (dot(x[h], p[h]) if explicit else dot(p[h], x[h])) for h in n]
        yield
    if explicit:
        x = [_dot_x3(x[h], rhs[h]) for h in n]
        yield
    return x


def _bidir_scan(body, tok_ins, const_ins, state_shape, *, b, t, n_ctx, lockstep=True, batch_block=1):
    nc, ncc = t // CHUNK, n_ctx // CHUNK
    nb = batch_block
    assert b % nb == 0

    def chunk_spec(width, col, reverse):
        return pl.BlockSpec((nb, CHUNK, width), lambda bi, i: (bi, _chunk_order(i, ncc, nc, reverse), col))

    def direction(reverse):
        d = int(reverse)
        specs = [chunk_spec(w, cols[d], reverse) for _, w, *cols in tok_ins]
        specs += [_const_spec(pair[d].shape) for pair in const_ins]
        return specs, [a for a, *_ in tok_ins] + [pair[d] for pair in const_ins]

    (spec_f, arg_f), (spec_b, arg_b) = direction(False), direction(True)
    n_tok, n_in = len(tok_ins), len(arg_f)

    def kern(*refs):
        o_f, o_b, s_f, s_b = refs[2 * n_in:]

        @pl.when(pl.program_id(1) == 0)
        def _():
            s_f[...] = jnp.zeros_like(s_f)
            s_b[...] = jnp.zeros_like(s_b)

        def one(j, ins, o_ref, s_ref, reverse):
            ins = [r.at[pl.ds(j, 1)] if k < n_tok else r for k, r in enumerate(ins)]
            return body(*ins, o_ref.at[pl.ds(j, 1)], s_ref.at[j], reverse=reverse)

        gens = []
        for j in range(nb):
            gens += [one(j, refs[:n_in], o_f, s_f, False), one(j, refs[n_in:2 * n_in], o_b, s_b, True)]
        if not lockstep:
            gens = [itertools.chain(*gens)]
        while gens:
            gens = [g for g in gens if next(g, _DONE) is not _DONE]

    return pl.pallas_call(
        kern,
        grid=(b // nb, nc),
        in_specs=spec_f + spec_b,
        out_specs=[chunk_spec(BRANCH, 0, False), chunk_spec(BRANCH, 0, True)],
        out_shape=[jax.ShapeDtypeStruct((b, t, BRANCH), F32)] * 2,
        scratch_shapes=[pltpu.VMEM((nb,) + tuple(state_shape), F32)] * 2,
        compiler_params=_cparams("parallel", "arbitrary"),
    )(*arg_f, *arg_b)


_DONE = object()


def _batch_block(b, pref):
    return pref if b % pref == 0 else 1


def _ssm_prep_kernel(x_ref, xp_ref, xn_ref, dt_ref, cw_ref, cb_ref, dtb_ref, xs_ref, bc_ref, sm_ref,
                     *, nct, nt):
    x = x_ref[0]
    xp, xn = _neighbours(x, xp_ref[0], xn_ref[0], nct=nct, nt=nt)
    y = _silu(xp * cw_ref[0:1, :] + x * cw_ref[1:2, :] + xn * cw_ref[2:3, :] + cb_ref[...])
    xs_ref[0] = y[:, :BRANCH]
    bc_ref[0] = y[:, BRANCH:]
    sm_ref[0] = _softplus(dt_ref[0] + dtb_ref[...])


def ssm_prep(p, lp, *, n_ctx):
    b, t, _ = p.shape
    tt = _pick_tile(n_ctx, ROW_TILE)
    dtb = jnp.pad(lp["ssm_dt_bias"].reshape(1, -1), ((0, 0), (0, LANES - 2 * SSM_HEADS)))
    specs = _halo_specs(tt, 768, 1, t) + [pl.BlockSpec((1, tt, LANES), lambda bi, i: (bi, i, 4)),
                                          _const_spec((3, 768)), _const_spec((1, 768)), _const_spec((1, LANES))]
    kern = functools.partial(_ssm_prep_kernel, nct=(t - n_ctx) // tt, nt=t // tt)
    return _prep_call(kern, (p, p, p, p, lp["ssm_conv_w"], _row(lp["ssm_conv_b"]), dtb), specs,
                      (BRANCH, 2 * SSM_GROUPS * SSM_N, LANES), b=b, t=t, tt=tt)


def _ssd_body(x_ref, bc_ref, sm_ref, na_ref, o_ref, s_ref, *, reverse):
    incl, _ = _chunk_masks(reverse)
    last = 0 if reverse else CHUNK - 1
    off = SSM_HEADS if reverse else 0
    dt_all = sm_ref[0]
    g_all = _chunk_cumsum(incl, dt_all * na_ref[...])
    yield
    expand = _expand_matrix(off, SSM_HEADS, SSM_P)
    gx = _select_columns(g_all, expand)
    dx = _select_columns(dt_all, expand)
    gt_all = _transpose_small(g_all)
    dtt_all = _transpose_small(dt_all)
    yield
    heads = range(SSM_HEADS)
    rep = SSM_HEADS // SSM_GROUPS
    gw = SSM_GROUPS * SSM_N
    hs = [slice(h * SSM_P, (h + 1) * SSM_P) for h in heads]
    glx = gx[last:last + 1, :]
    egx = jnp.exp(gx)
    wx = dx * jnp.exp(glx - gx)
    eglx = jnp.exp(glx)
    x = x_ref[0]
    bm = [bc_ref[0, :, grp * SSM_N:(grp + 1) * SSM_N] for grp in range(SSM_GROUPS)]
    cm = [bc_ref[0, :, gw + grp * SSM_N:gw + (grp + 1) * SSM_N] for grp in range(SSM_GROUPS)]
    cb = [_dot_nt(cm[grp], bm[grp]) for grp in range(SSM_GROUPS)]
    s = [s_ref[grp] for grp in range(SSM_GROUPS)]
    yield
    scores = [cb[h // rep] * jnp.exp(jnp.where(incl, gx[:, hs[h]] - gt_all[off + h:off + h + 1, :], -jnp.inf))
              * dtt_all[off + h:off + h + 1, :] for h in heads]
    yield
    intra = [_dot(scores[h], x[:, hs[h]]) for h in heads]
    yield
    inter = [_dot(cm[grp], s[grp]) for grp in range(SSM_GROUPS)]
    yield
    upd = [_dot_tn(bm[h // rep] * wx[:, hs[h]], x[:, hs[h]]) for h in heads]
    yield
    for h in heads:
        grp, ls = h // rep, slice((h % rep) * SSM_P, (h % rep + 1) * SSM_P)
        o_ref[0, :, hs[h]] = intra[h] + egx[:, hs[h]] * inter[grp][:, ls]
        s_ref[grp, :, ls] = s[grp][:, ls] * eglx[:, hs[h]] + upd[h]


def _expand_matrix(off, n_heads, width):
    row = lax.broadcasted_iota(jnp.int32, (LANES, n_heads * width), 0)
    col = lax.broadcasted_iota(jnp.int32, (LANES, n_heads * width), 1)
    lo = row * width - off * width
    return ((col >= lo) & (col < lo + width)).astype(BF16)


def ssd_scan(xs, bc, sm, neg_a, *, n_ctx):
    b, t, _ = xs.shape
    toks = [(xs, BRANCH, 0, 0), (bc, 2 * SSM_GROUPS * SSM_N, 0, 0), (sm, LANES, 0, 0)]
    state = (SSM_GROUPS, SSM_N, (SSM_HEADS // SSM_GROUPS) * SSM_P)
    return _bidir_scan(_ssd_body, toks, [(neg_a, neg_a)], state, b=b, t=t, n_ctx=n_ctx,
                       batch_block=_batch_block(b, 2))


def _gla_body(q_ref, k_ref, v_ref, glr_ref, w2_ref, gb_ref, o_ref, s_ref, *, reverse):
    incl, _ = _chunk_masks(reverse)
    last = 0 if reverse else CHUNK - 1
    logit = _dot_x3(glr_ref[0], w2_ref[...]) + gb_ref[...]
    yield
    la = -_softplus(-logit) * (1.0 / GLA_TAU)
    g_all = _chunk_cumsum(incl, la)
    yield
    heads = range(GLA_HEADS)
    ks = [slice(h * GLA_DK, (h + 1) * GLA_DK) for h in heads]
    vs = [slice(h * GLA_DV, (h + 1) * GLA_DV) for h in heads]
    g = [g_all[:, ks[h]] for h in heads]
    gl = [g[h][last:last + 1, :] for h in heads]
    k = [k_ref[0, :, ks[h]] for h in heads]
    v = [v_ref[0, :, vs[h]] for h in heads]
    qg = [q_ref[0, :, ks[h]] * (GLA_DK ** -0.5) * jnp.exp(g[h]) for h in heads]
    st = [s_ref[h] for h in heads]
    yield
    scores = [jnp.where(incl, _dot_nt(qg[h], k[h] * jnp.exp(-g[h])), 0.0) for h in heads]
    yield
    intra = [_dot(scores[h], v[h]) for h in heads]
    yield
    inter = [_dot_nt(qg[h], st[h]) for h in heads]
    yield
    upd = [_dot_tn(v[h], k[h] * jnp.exp(gl[h] - g[h])) for h in heads]
    yield
    for h in heads:
        o_ref[0, :, vs[h]] = intra[h] + inter[h]
        s_ref[h] = st[h] * jnp.exp(gl[h]) + upd[h]


def gla_scan(p, w2_pair, gb_pair, *, n_ctx):
    b, t, _ = p.shape
    kwid = GLA_HEADS * GLA_DK
    toks = [(p, kwid, 0, 0), (p, kwid, 1, 1), (p, BRANCH, 1, 1), (p, LANES, 12, 12)]
    return _bidir_scan(_gla_body, toks, [w2_pair, gb_pair], (GLA_HEADS, GLA_DV, GLA_DK), b=b, t=t, n_ctx=n_ctx,
                       batch_block=_batch_block(b, 4))


def _rwkv_prep_kernel(x_ref, xp_ref, xn_ref, mu_ref, w2_ref, w0_ref, a2_ref, a0_ref, g2_ref, kk_ref_w,
                      ka_ref, rk_ref, r_ref, k_ref, v_ref, kk_ref, a_ref, lw_ref, g_ref, bo_ref,
                      *, nct, nt):
    x = x_ref[0]
    xp, xn = _neighbours(x, xp_ref[0], xn_ref[0], nct=nct, nt=nt)
    x = x + mu_ref[...] * (0.5 * (xp + xn) - x)
    r, k, v = x[:, :BRANCH], x[:, BRANCH:2 * BRANCH], x[:, 2 * BRANCH:3 * BRANCH]
    wlr = x[:, 3 * BRANCH:3 * BRANCH + LANES]
    alr = x[:, 3 * BRANCH + LANES:3 * BRANCH + 2 * LANES]
    glr = x[:, 3 * BRANCH + 2 * LANES:]
    w_raw = _dot_x3(jnp.tanh(wlr), w2_ref[...]) + w0_ref[...]
    lw_ref[0] = _sigmoid(w_raw) * (-float(np.exp(-0.5)))
    a = _sigmoid(_dot_x3(alr, a2_ref[...]) + a0_ref[...])
    a_ref[0] = a
    g_ref[0] = _dot_x3(_sigmoid(glr), g2_ref[...])
    kk = k * kk_ref_w[...]
    kk_ref[0] = kk * lax.rsqrt(_group_sums(kk * kk, RWKV_N) + EPS)
    ksum = k * (2.0 + (a[:, :BRANCH] + a[:, BRANCH:] - 2.0) * ka_ref[...])
    bo_ref[0] = _group_sums(r * ksum * rk_ref[...], RWKV_N) * v
    r_ref[0] = r
    k_ref[0] = k
    v_ref[0] = v


def rwkv_prep(p, lp, *, n_ctx):
    b, t, w = p.shape
    tt = _pick_tile(n_ctx, ROW_TILE)

    def pair(wp):
        r, c = wp.shape[1:]
        return jnp.zeros((LANES, 2 * c), F32).at[:r, :c].set(wp[0]).at[r:2 * r, c:].set(wp[1])

    consts = (_row(lp["rwkv_mu"]), pair(lp["rwkv_w2"]), _row(lp["rwkv_w0"]), pair(lp["rwkv_a2"]),
              _row(lp["rwkv_a0"]), lp["rwkv_g2"], _row(lp["rwkv_k_k"]), _row(lp["rwkv_k_a"]),
              _row(lp["rwkv_r_k"]))
    specs = _halo_specs(tt, w, 0, t) + [_const_spec(c.shape) for c in consts]
    kern = functools.partial(_rwkv_prep_kernel, nct=(t - n_ctx) // tt, nt=t // tt)
    return _prep_call(kern, (p, p, p) + consts, specs,
                      (BRANCH, BRANCH, BRANCH, BRANCH, 2 * BRANCH, 2 * BRANCH, BRANCH, BRANCH),
                      b=b, t=t, tt=tt)


def _rwkv_body(r_ref, k_ref, v_ref, kk_ref, a_ref, lw_ref, ka_ref, o_ref, s_ref, *, reverse):
    incl, strict = _chunk_masks(reverse)
    last = 0 if reverse else CHUNK - 1
    lw_all = lw_ref[0]
    g_all = _chunk_cumsum(incl, lw_all)
    a_all = a_ref[0]
    k_all = k_ref[0] * (1.0 + (a_all - 1.0) * ka_ref[...])
    yield
    heads = range(RWKV_HEADS)
    hs = [slice(h * RWKV_N, (h + 1) * RWKV_N) for h in heads]
    g = [g_all[:, hs[h]] for h in heads]
    gl = [g[h][last:last + 1, :] for h in heads]
    eneg = [jnp.exp(-g[h]) for h in heads]
    edec = [jnp.exp(gl[h] - g[h]) for h in heads]
    kk = [kk_ref[0, :, hs[h]] for h in heads]
    bvec = [kk[h] * a_all[:, hs[h]] for h in heads]
    k = [k_all[:, hs[h]] for h in heads]
    v = [v_ref[0, :, hs[h]] for h in heads]
    kkg = [kk[h] * jnp.exp(g[h] - lw_all[:, hs[h]]) for h in heads]
    rg = [r_ref[0, :, hs[h]] * jnp.exp(g[h]) for h in heads]
    bh = [bvec[h] * eneg[h] for h in heads]
    kh = [k[h] * eneg[h] for h in heads]
    s = [s_ref[h] for h in heads]
    yield
    both = [jnp.concatenate([kkg[h], rg[h]], axis=0) for h in heads]
    mask2 = jnp.concatenate([strict, incl], axis=0)
    mb = [jnp.where(mask2, _dot_nt(both[h], bh[h]), 0.0) for h in heads]
    yield
    mk = [jnp.where(mask2, _dot_nt(both[h], kh[h]), 0.0) for h in heads]
    yield
    part = [_dot(mk[h], v[h]) + _dot_nt(both[h], s[h]) for h in heads]
    yield
    x = yield from _unit_tri_solve([mb[h][:CHUNK] for h in heads], [part[h][:CHUNK] for h in heads])
    u = [-xh for xh in x]
    for h in heads:
        o_ref[0, :, hs[h]] = part[h][CHUNK:] + _dot(mb[h][CHUNK:], u[h])
    yield
    for h in heads:
        upd = _dot_tn(jnp.concatenate([u[h], v[h]], axis=0),
                      jnp.concatenate([bvec[h] * edec[h], k[h] * edec[h]], axis=0))
        s_ref[h] = s[h] * jnp.exp(gl[h]) + upd


def rwkv_scan(r, k, v, kk, a, lw, k_a, *, n_ctx):
    b, t, _ = r.shape
    toks = [(r, BRANCH, 0, 0), (k, BRANCH, 0, 0), (v, BRANCH, 0, 0), (kk, BRANCH, 0, 0),
            (a, BRANCH, 0, 1), (lw, BRANCH, 0, 1)]
    return _bidir_scan(_rwkv_body, toks, [(k_a, k_a)], (RWKV_HEADS, RWKV_N, RWKV_N), b=b, t=t, n_ctx=n_ctx,
                       batch_block=_batch_block(b, 2))


def _gdn_prep_kernel(x_ref, xp_ref, xn_ref, ab_ref, cw_ref, na_ref, dtb_ref,
                     q_ref, k_ref, v_ref, sm_ref, *, nct, nt):
    x = x_ref[0]
    xp, xn = _neighbours(x, xp_ref[0], xn_ref[0], nct=nct, nt=nt)
    y = _silu(xp * cw_ref[0:1, :] + x * cw_ref[1:2, :] + xn * cw_ref[2:3, :])
    q, k = y[:, :BRANCH], y[:, BRANCH:2 * BRANCH]
    q_ref[0] = q * lax.rsqrt(_group_sums(q * q, GDN_N) + EPS) * (GDN_N ** -0.5)
    k_ref[0] = k * lax.rsqrt(_group_sums(k * k, GDN_N) + EPS)
    v_ref[0] = y[:, 2 * BRANCH:]
    ab = ab_ref[0]
    lane = lax.broadcasted_iota(jnp.int32, ab.shape, 1)
    sm_ref[0] = jnp.where(lane < 2 * GDN_HEADS, na_ref[...] * _softplus(ab + dtb_ref[...]), _sigmoid(ab))


def gdn_prep(p, lp, *, n_ctx):
    b, t, _ = p.shape
    tt = _pick_tile(n_ctx, ROW_TILE)
    padrow = lambda v: jnp.pad(v.reshape(1, -1), ((0, 0), (0, LANES - 2 * GDN_HEADS)))
    consts = (lp["gdn_conv_w"], padrow(-jnp.exp(lp["gdn_a_log"])), padrow(lp["gdn_dt_bias"]))
    specs = (_halo_specs(tt, 3 * BRANCH, 0, t) + [pl.BlockSpec((1, tt, LANES), lambda bi, i: (bi, i, 16))]
             + [_const_spec(c.shape) for c in consts])
    kern = functools.partial(_gdn_prep_kernel, nct=(t - n_ctx) // tt, nt=t // tt)
    return _prep_call(kern, (p, p, p, p) + consts, specs, (BRANCH, BRANCH, BRANCH, LANES), b=b, t=t, tt=tt)


def _gdn_body(q_ref, k_ref, v_ref, sm_ref, o_ref, s_ref, *, reverse):
    incl, strict = _chunk_masks(reverse)
    last = 0 if reverse else CHUNK - 1
    off = GDN_HEADS if reverse else 0
    sm = sm_ref[0]
    g_all = _chunk_cumsum(incl, sm)
    yield
    gt_all = _transpose_small(g_all)
    yield
    heads = range(GDN_HEADS)
    hs = [slice(h * GDN_N, (h + 1) * GDN_N) for h in heads]
    g = [g_all[:, off + h:off + h + 1] for h in heads]
    gl = [g[h][last:last + 1, :] for h in heads]
    beta = [sm[:, 2 * GDN_HEADS + off + h:2 * GDN_HEADS + off + h + 1] for h in heads]
    q = [q_ref[0, :, hs[h]] for h in heads]
    k = [k_ref[0, :, hs[h]] for h in heads]
    v = [v_ref[0, :, hs[h]] for h in heads]
    s = [s_ref[h] for h in heads]
    decay = [jnp.exp(jnp.where(incl, g[h] - gt_all[off + h:off + h + 1, :], -jnp.inf)) for h in heads]
    yield
    kq = [_dot_nt(jnp.concatenate([k[h], q[h]], axis=0), k[h]) for h in heads]
    yield
    lower = [jnp.where(strict, kq[h][:CHUNK] * decay[h] * beta[h], 0.0) for h in heads]
    attn = [kq[h][CHUNK:] * decay[h] for h in heads]
    o_part = [_dot(q[h] * jnp.exp(g[h]), s[h]) for h in heads]
    yield
    rhs = [jnp.concatenate([v[h] * beta[h], k[h] * (beta[h] * jnp.exp(g[h]))], axis=1) for h in heads]
    sol = yield from _unit_tri_solve(lower, rhs, precise_levels=2, explicit=True)
    v_new = [sol[h][:, :GDN_N] - _dot(sol[h][:, GDN_N:], s[h]) for h in heads]
    yield
    for h in heads:
        o_ref[0, :, hs[h]] = o_part[h] + _dot(attn[h], v_new[h])
    yield
    for h in heads:
        s_ref[h] = s[h] * jnp.exp(gl[h]) + _dot_tn(k[h] * jnp.exp(gl[h] - g[h]), v_new[h])


def gdn_scan(q, k, v, sm, *, n_ctx):
    b, t, _ = q.shape
    toks = [(q, BRANCH, 0, 0), (k, BRANCH, 0, 0), (v, BRANCH, 0, 0), (sm, LANES, 0, 0)]
    return _bidir_scan(_gdn_body, toks, [], (GDN_HEADS, GDN_N, GDN_N), b=b, t=t, n_ctx=n_ctx,
                       batch_block=_batch_block(b, 4))


def _merge_kernel(sf_ref, sb_ref, sx_ref, sz_ref, gf_ref, gb_ref, gr_ref, rf_ref, rb_ref, rg_ref, rbo_ref,
                  df_ref, db_ref, dz_ref, gate_ref, x_ref, m_ref,
                  sd_ref, sn_ref, gn_ref, lnw_ref, lnb_ref, dn_ref,
                  wb_ref, wo_ref, o_ref):
    def group_rms(y, n, w_ref):
        return y * lax.rsqrt(_group_sums(y * y, n) * (1.0 / n) + EPS) * w_ref[...]

    y = (sf_ref[0] + sb_ref[0] + sd_ref[...] * sx_ref[0]) * _silu(sz_ref[0])
    ys = group_rms(y, BRANCH // SSM_GROUPS, sn_ref)
    yg = group_rms(gf_ref[0] + gb_ref[0], GLA_DV, gn_ref) * _silu(gr_ref[0])
    y = rf_ref[0] + rb_ref[0]
    yc = y - _group_sums(y, RWKV_N) * (1.0 / RWKV_N)
    var = _group_sums(yc * yc, RWKV_N) * (1.0 / RWKV_N)
    yr = (yc * lax.rsqrt(var + RWKV_LN_EPS) * lnw_ref[...] + lnb_ref[...] + rbo_ref[0]) * rg_ref[0]
    yd = group_rms(df_ref[0] + db_ref[0], GDN_N, dn_ref) * _silu(dz_ref[0])
    acc = None
    for i, yi in enumerate((ys, yg, yr, yd)):
        term = gate_ref[0, :, i * D_MODEL:(i + 1) * D_MODEL].astype(F32) * _dot(yi, wb_ref[i])
        acc = term if acc is None else acc + term
    o_ref[0] = x_ref[0] + m_ref[0, 0] * _dot(acc, wo_ref[...])


def merge_residual(ssm, gla, rwkv, gdn, gates, x_all, gate_mod, lp, w_branch, w_out, *, n_ctx):
    b, t, d = x_all.shape
    tm = _pick_tile(n_ctx, ROW_TILE)
    tok = lambda bi, i: (bi, i, 0)
    blk = lambda c: pl.BlockSpec((1, tm, BRANCH), lambda bi, i: (bi, i, c))
    half = blk(0)
    consts = (_row(jnp.repeat(lp["ssm_d"], SSM_P)), _row(lp["ssm_norm"]),
              _row(jnp.tile(lp["gla_norm"], GLA_HEADS)), _row(lp["rwkv_ln_w"]), _row(lp["rwkv_ln_b"]),
              _row(jnp.tile(lp["gdn_norm"], GDN_HEADS)), w_branch, w_out)
    ins = (ssm[0], ssm[1], ssm[2], ssm[3], gla[0], gla[1], gla[2], rwkv[0], rwkv[1], rwkv[2], rwkv[3],
           gdn[0], gdn[1], gdn[2], gates, x_all, gate_mod) + consts
    specs = ([half, half, half, blk(0), half, half, blk(2), half, half, half, half, half, half, blk(3),
              pl.BlockSpec((1, tm, 4 * d), tok), pl.BlockSpec((1, tm, d), tok),
              pl.BlockSpec((1, 1, 1, d), _mod_sel((t - n_ctx) // tm))]
             + [_const_spec(c.shape) for c in consts])
    return pl.pallas_call(
        _merge_kernel,
        grid=(b, t // tm),
        in_specs=specs,
        out_specs=pl.BlockSpec((1, tm, d), tok),
        out_shape=jax.ShapeDtypeStruct((b, t, d), F32),
        compiler_params=_cparams("parallel", "parallel"),
    )(*ins)


def _route_kernel(x_ref, nw_ref, shift_ref, scale_ref, rw_ref, rb_ref, u_ref, o_ref, cnt_ref, hb_ref, *, rows_kw):
    x = x_ref[...]
    shift, scale = _token_rows([shift_ref, scale_ref], pl.program_id(0), x.shape[0], **rows_kw)
    h = x * lax.rsqrt(jnp.mean(x * x, axis=-1, keepdims=True) + EPS) * nw_ref[...] * (1.0 + scale) + shift
    hb_ref[...] = h.astype(BF16)
    logits = lax.dot_general(rw_ref[...], h, (((1,), (1,)), ((), ())),
                             precision=HI, preferred_element_type=F32)
    scores = _sigmoid(logits)
    sel = scores + rb_ref[...]
    rows = [sel[e:e + 1, :] for e in range(N_EXPERTS)]
    sc = [scores[e:e + 1, :] for e in range(N_EXPERTS)]

    def top2(vals):
        v1, i1 = vals[0], jnp.zeros(vals[0].shape, jnp.int32)
        for j in range(1, len(vals)):
            better = vals[j] > v1
            v1 = jnp.where(better, vals[j], v1)
            i1 = jnp.where(better, j, i1)
        v2 = jnp.where(i1 == 0, vals[1], vals[0])
        i2 = jnp.where(i1 == 0, 1, 0)
        for j in range(1, len(vals)):
            better = (vals[j] > v2) & (i1 != j)
            v2 = jnp.where(better, vals[j], v2)
            i2 = jnp.where(better, j, i2)
        return v1, i1, v2, i2

    gsum = []
    for grp in range(N_GROUPS):
        v1, _, v2, _ = top2(rows[grp * EXPERTS_PER_GROUP:(grp + 1) * EXPERTS_PER_GROUP])
        gsum.append(v1 + v2)
    best, gidx = gsum[0], jnp.zeros(gsum[0].shape, jnp.int32)
    for grp in range(1, N_GROUPS):
        better = gsum[grp] > best
        best = jnp.where(better, gsum[grp], best)
        gidx = jnp.where(better, grp, gidx)
    chosen, chosen_sc = [], []
    for j in range(EXPERTS_PER_GROUP):
        cj, sj = rows[j], sc[j]
        for grp in range(1, N_GROUPS):
            cj = jnp.where(gidx == grp, rows[grp * EXPERTS_PER_GROUP + j], cj)
            sj = jnp.where(gidx == grp, sc[grp * EXPERTS_PER_GROUP + j], sj)
        chosen.append(cj)
        chosen_sc.append(sj)
    _, i1, _, i2 = top2(chosen)
    w1, w2 = jnp.zeros_like(best), jnp.zeros_like(best)
    for j in range(EXPERTS_PER_GROUP):
        w1 = jnp.where(i1 == j, chosen_sc[j], w1)
        w2 = jnp.where(i2 == j, chosen_sc[j], w2)
    tot = w1 + w2
    w1, w2 = w1 / tot, w2 / tot
    tm = scores.shape[1]
    sub = lax.broadcasted_iota(jnp.int32, (SUBLANES, tm), 0)
    ind8 = jnp.zeros((SUBLANES, tm), F32)
    meta = jnp.zeros((SUBLANES, tm), F32)
    for j in range(EXPERTS_PER_GROUP):
        gate_j = jnp.where(i1 == j, w1, 0.0) + jnp.where(i2 == j, w2, 0.0)
        meta = jnp.where(sub == j, gate_j, meta)
    for grp in range(N_GROUPS):
        ind8 = jnp.where((sub == grp) & (gidx == grp), 1.0, ind8)
    before = jnp.dot(ind8.astype(BF16), u_ref[...], preferred_element_type=F32)
    rank = jnp.sum(ind8 * before, axis=0, keepdims=True)
    meta = jnp.where(sub == _META_GROUP, gidx.astype(F32), meta)
    meta = jnp.where(sub == _META_RANK, rank, meta)
    o_ref[...] = meta
    counts = jnp.sum(ind8, axis=1, keepdims=True)
    lane = lax.broadcasted_iota(jnp.int32, (SUBLANES, LANES), 1)
    row = lax.broadcasted_iota(jnp.int32, (SUBLANES, LANES), 0)
    cnt_ref[0] = jnp.broadcast_to(jnp.sum(jnp.where(lane == row, counts, 0.0), axis=0, keepdims=True),
                                  (SUBLANES, LANES)).astype(jnp.int32)


_META_GROUP, _META_RANK = EXPERTS_PER_GROUP, EXPERTS_PER_GROUP + 1


def moe_route(x_all, norm_w, shift_rows, scale_rows, router_w, router_b, *, n_ctx, tm):
    b, t, d = x_all.shape
    m = b * t
    upper = jnp.asarray(np.triu(np.ones((tm, tm), np.float32), 1), BF16)
    return pl.pallas_call(
        functools.partial(_route_kernel, rows_kw=dict(bsz=b, t_all=t, n_ctx=n_ctx)),
        grid=(m // tm,),
        in_specs=[pl.BlockSpec((tm, d), lambda i: (i, 0)),
                  _const_spec((1, d)), _const_spec(shift_rows.shape), _const_spec(scale_rows.shape),
                  pl.BlockSpec((N_EXPERTS, d), lambda i: (0, 0)),
                  pl.BlockSpec((N_EXPERTS, 1), lambda i: (0, 0)),
                  _const_spec((tm, tm))],
        out_specs=[pl.BlockSpec((SUBLANES, tm), lambda i: (0, i)),
                   pl.BlockSpec((1, SUBLANES, LANES), lambda i: (i, 0, 0)),
                   pl.BlockSpec((tm, d), lambda i: (i, 0))],
        out_shape=[jax.ShapeDtypeStruct((SUBLANES, m), F32),
                   jax.ShapeDtypeStruct((m // tm, SUBLANES, LANES), jnp.int32),
                   jax.ShapeDtypeStruct((m, d), BF16)],
        compiler_params=_cparams("parallel"),
    )(x_all.reshape(m, d), norm_w.reshape(1, d), shift_rows, scale_rows,
      router_w.T, router_b.reshape(N_EXPERTS, 1), upper)


MOE_TILE = 1024
MOE_SUB_ROWS = 256


def _expert_kernel(cnt_ref, h_ref, mr_ref, mc_ref, wg_ref, wu_ref, wd_ref, x_ref, gate_ref, o_ref, *, rows_kw):
    i, grp = pl.program_id(0), pl.program_id(1)

    @pl.when(grp == 0)
    def _():
        o_ref[...] = jnp.zeros_like(o_ref)

    tm = h_ref.shape[0]
    count = cnt_ref[i * N_GROUPS + grp]
    grp_f = grp.astype(F32)
    sel_row = jnp.where(mr_ref[_META_GROUP:_META_GROUP + 1, :] == grp_f, mr_ref[_META_RANK:_META_RANK + 1, :], -1.0)
    sel_col = jnp.where(mc_ref[:, _META_GROUP:_META_GROUP + 1] == grp_f, mc_ref[:, _META_RANK:_META_RANK + 1], -1.0)
    gate_parts = _split3(mc_ref[...])

    def sub_block(first, rows):
        base = first.astype(F32)
        slot_r = lax.broadcasted_iota(jnp.int32, (rows, tm), 0).astype(F32)
        slot_c = lax.broadcasted_iota(jnp.int32, (tm, rows), 1).astype(F32)
        pick = (sel_row - base == slot_r).astype(BF16)
        put = (sel_col - base == slot_c).astype(BF16)
        xg = jnp.dot(pick, h_ref[...], preferred_element_type=F32).astype(BF16)
        gates = sum(jnp.dot(pick, p, preferred_element_type=F32) for p in gate_parts)
        y = jnp.zeros((rows, o_ref.shape[1]), F32)
        for e in range(EXPERTS_PER_GROUP):
            hid = _silu(_dot(xg, wg_ref[e])) * _dot(xg, wu_ref[e])
            y = y + _dot(gates[:, e:e + 1] * hid, wd_ref[e])
        o_ref[...] += jnp.dot(put, y.astype(BF16), preferred_element_type=F32)

    half, quarter = MOE_SUB_ROWS // 2, MOE_SUB_ROWS // 4
    n_full = count // MOE_SUB_ROWS
    quarters = (count - n_full * MOE_SUB_ROWS + quarter - 1) // quarter
    n_main = n_full + (quarters == 4).astype(jnp.int32)

    def main_block(s, carry):
        sub_block(s * MOE_SUB_ROWS, MOE_SUB_ROWS)
        return carry

    lax.fori_loop(0, n_main, main_block, 0)
    has_half = (quarters == 2) | (quarters == 3)

    @pl.when(has_half)
    def _():
        sub_block(n_full * MOE_SUB_ROWS, half)

    @pl.when((quarters == 1) | (quarters == 3))
    def _():
        sub_block(n_full * MOE_SUB_ROWS + half * has_half.astype(jnp.int32), quarter)

    @pl.when(grp == N_GROUPS - 1)
    def _():
        (gate,) = _token_rows([gate_ref], i, tm, **rows_kw)
        o_ref[...] = x_ref[...] + gate * o_ref[...]


def moe_experts(hb, meta, counts, wg, wu, wd, x_all, gate_rows, *, n_ctx, tm):
    b, t, d = x_all.shape
    m = b * t
    tok = lambda i, g, cnt: (i, 0)
    grid_spec = pltpu.PrefetchScalarGridSpec(
        num_scalar_prefetch=1,
        grid=(m // tm, N_GROUPS),
        in_specs=[pl.BlockSpec((tm, d), tok),
                  pl.BlockSpec((SUBLANES, tm), lambda i, g, cnt: (0, i)),
                  pl.BlockSpec((tm, SUBLANES), tok),
                  pl.BlockSpec((EXPERTS_PER_GROUP, d, EXPERT_FF), lambda i, g, cnt: (g, 0, 0)),
                  pl.BlockSpec((EXPERTS_PER_GROUP, d, EXPERT_FF), lambda i, g, cnt: (g, 0, 0)),
                  pl.BlockSpec((EXPERTS_PER_GROUP, EXPERT_FF, d), lambda i, g, cnt: (g, 0, 0)),
                  pl.BlockSpec((tm, d), tok),
                  pl.BlockSpec(gate_rows.shape, lambda i, g, cnt: (0, 0))],
        out_specs=pl.BlockSpec((tm, d), tok))
    out = pl.pallas_call(
        functools.partial(_expert_kernel, rows_kw=dict(bsz=b, t_all=t, n_ctx=n_ctx)),
        grid_spec=grid_spec,
        out_shape=jax.ShapeDtypeStruct((m, d), F32),
        compiler_params=pltpu.CompilerParams(dimension_semantics=("parallel", "arbitrary"),
                                             vmem_limit_bytes=MOE_VMEM_LIMIT),
    )(counts[:, 0, :N_GROUPS].reshape(-1), hb, meta, meta.T, wg, wu, wd, x_all.reshape(m, d), gate_rows)
    return out.reshape(b, t, d)


def _token_rows(m_refs, tile, tm, *, bsz, t_all, n_ctx):
    row = tile * tm + lax.broadcasted_iota(jnp.int32, (tm, 1), 0)
    ctx = jnp.zeros((tm, 1), jnp.bool_)
    lat = []
    for bi in range(bsz):
        lo, split = bi * t_all, bi * t_all + t_all - n_ctx
        lat.append((row >= lo) & (row < split))
        ctx = ctx | ((row >= split) & (row < lo + t_all))
    out = []
    for m_ref in m_refs:
        v = jnp.where(ctx, m_ref[bsz:bsz + 1, :], 0.0)
        for bi in range(bsz):
            v = v + jnp.where(lat[bi], m_ref[bi:bi + 1, :], 0.0)
        out.append(v)
    return out


GRID_TILE_COLS = SUBLANES
GRID_TILE = GRID_W * GRID_TILE_COLS


def _grid_view(x_all, n_lat):
    b, t, d = x_all.shape
    rows = n_lat // GRID_W
    assert rows == GRID_W and t % GRID_W == 0
    last = GRID_W // GRID_TILE_COLS - 1
    spec = pl.BlockSpec((1, rows, GRID_TILE_COLS, d), lambda bi, i: (bi, 0, jnp.minimum(i, last), 0))
    return x_all.reshape(b, t // GRID_W, GRID_W, d), spec


def _grid_tile(xg_ref):
    return jnp.concatenate([xg_ref[0, :, j, :] for j in range(xg_ref.shape[2])], axis=0)


def _transpose_grid_kernel(xn_ref, xg_ref, o_ref, *, n_lat_tiles):
    i = pl.program_id(1)

    @pl.when(i < n_lat_tiles)
    def _():
        o_ref[0] = _grid_tile(xg_ref)

    @pl.when(i >= n_lat_tiles)
    def _():
        o_ref[0] = xn_ref[0]


def transpose_grid(x_all, *, n_ctx):
    b, t, d = x_all.shape
    xg, gspec = _grid_view(x_all, t - n_ctx)
    tok = pl.BlockSpec((1, GRID_TILE, d), lambda bi, i: (bi, i, 0))
    return pl.pallas_call(
        functools.partial(_transpose_grid_kernel, n_lat_tiles=(t - n_ctx) // GRID_TILE),
        grid=(b, pl.cdiv(t, GRID_TILE)),
        in_specs=[tok, gspec],
        out_specs=tok,
        out_shape=jax.ShapeDtypeStruct((b, t, d), x_all.dtype),
        compiler_params=_cparams("parallel", "parallel"),
    )(x_all, xg)


def _final_norm_kernel(x_ref, w_ref, o_ref, *, from_grid):
    x = _grid_tile(x_ref) if from_grid else x_ref[0]
    o_ref[0] = x * lax.rsqrt(jnp.mean(x * x, axis=-1, keepdims=True) + EPS) * w_ref[...]


def final_rms_norm(x_all, w, *, n_ctx, from_grid):
    b, t, d = x_all.shape
    n_lat = t - n_ctx
    tt = GRID_TILE
    tok = pl.BlockSpec((1, tt, d), lambda bi, i: (bi, i, 0))
    if from_grid:
        x_in, spec = _grid_view(x_all, n_lat)
    else:
        x_in, spec = x_all, tok
    return pl.pallas_call(
        functools.partial(_final_norm_kernel, from_grid=from_grid),
        grid=(b, n_lat // tt),
        in_specs=[spec, _const_spec((1, d))],
        out_specs=tok,
        out_shape=jax.ShapeDtypeStruct((b, n_lat, d), F32),
        compiler_params=_cparams("parallel", "parallel"),
    )(x_in, w.reshape(1, d))


def _pack_w_in(w_in, mixer):
    cols = _SRC_COLS[mixer]
    pieces, i = [], 0
    while i < len(cols):
        j = i
        if cols[i] < 0:
            while j < len(cols) and cols[j] < 0:
                j += 1
            pieces.append(jnp.zeros((w_in.shape[0], j - i), w_in.dtype))
        else:
            while j < len(cols) and cols[j] == cols[i] + (j - i):
                j += 1
            pieces.append(w_in[:, int(cols[i]):int(cols[i]) + (j - i)])
        i = j
    return jnp.concatenate(pieces, axis=1).astype(BF16)


def mixer_scans(ps, lp, *, n_ctx):
    p_ssm, p_gla, p_rwkv, p_gdn = ps

    xs, bc, sm = ssm_prep(p_ssm, lp, n_ctx=n_ctx)
    neg_a = jnp.pad(-jnp.exp(lp["ssm_a_log"]).reshape(1, -1), ((0, 0), (0, LANES - 2 * SSM_HEADS)))
    ssm = tuple(ssd_scan(xs, bc, sm, neg_a, n_ctx=n_ctx)) + (xs, p_ssm)

    w2 = [jnp.zeros((LANES, GLA_HEADS * GLA_DK), F32).at[d * GLA_RANK:(d + 1) * GLA_RANK].set(lp["gla_w2"][d])
          for d in range(2)]
    gb = [_row(lp["gla_b"][d]) for d in range(2)]
    gla = tuple(gla_scan(p_gla, w2, gb, n_ctx=n_ctx)) + (p_gla,)

    r, k, v, kk, a, lw, g, bonus = rwkv_prep(p_rwkv, lp, n_ctx=n_ctx)
    rwkv = tuple(rwkv_scan(r, k, v, kk, a, lw, _row(lp["rwkv_k_a"]), n_ctx=n_ctx)) + (g, bonus)

    q, kd, vd, smd = gdn_prep(p_gdn, lp, n_ctx=n_ctx)
    gdn = tuple(gdn_scan(q, kd, vd, smd, n_ctx=n_ctx)) + (p_gdn,)
    return ssm, gla, rwkv, gdn


def kernel(x, c, ctx, c_ctx, ada_w, ada_b, norm_mix, norm_ffn, w_in, w_gate, w_branch, w_out, ssm_conv_w, ssm_conv_b, ssm_a_log, ssm_dt_bias, ssm_d, ssm_norm, gla_w2, gla_b, gla_norm, rwkv_mu, rwkv_w0, rwkv_w2, rwkv_a0, rwkv_a2, rwkv_g2, rwkv_k_k, rwkv_k_a, rwkv_r_k, rwkv_ln_w, rwkv_ln_b, gdn_conv_w, gdn_a_log, gdn_dt_bias, gdn_norm, router_w, router_b, moe_w_gate, moe_w_up, moe_w_down, final_norm):
    bsz, seq, d = x.shape
    n_ctx = ctx.shape[1]
    t_all = n_ctx + seq
    m_all = bsz * t_all

    cond = jnp.concatenate([jax.nn.silu(c), jax.nn.silu(c_ctx)[None]], 0)
    cond = jnp.pad(cond, ((0, SUBLANES - cond.shape[0]), (0, 0)))
    mods, mod_rows = [], []
    for l in range(DEPTH):
        mod = pmatmul(cond, ada_w[l], tm=SUBLANES, tn=1024, precise=True) + ada_b[l]
        mod_rows.append(mod)
        lat = mod[:bsz].reshape(bsz, 6, d)
        cx = jnp.broadcast_to(mod[bsz].reshape(1, 6, d), (bsz, 6, d))
        mods.append(jnp.stack([cx, lat], axis=1))

    x_all = jnp.concatenate([x, ctx], axis=1)
    scan_order = False
    for l in range(DEPTH):
        if (l % 2 == 1) != scan_order:
            x_all = transpose_grid(x_all, n_ctx=n_ctx)
            scan_order = not scan_order
        lp = dict(ssm_conv_w=ssm_conv_w[l], ssm_conv_b=ssm_conv_b[l], ssm_a_log=ssm_a_log[l],
                  ssm_dt_bias=ssm_dt_bias[l], ssm_d=ssm_d[l], ssm_norm=ssm_norm[l],
                  gla_w2=gla_w2[l], gla_b=gla_b[l], gla_norm=gla_norm[l],
                  rwkv_mu=rwkv_mu[l], rwkv_w0=rwkv_w0[l], rwkv_w2=rwkv_w2[l], rwkv_a0=rwkv_a0[l],
                  rwkv_a2=rwkv_a2[l], rwkv_g2=rwkv_g2[l], rwkv_k_k=rwkv_k_k[l], rwkv_k_a=rwkv_k_a[l],
                  rwkv_r_k=rwkv_r_k[l], rwkv_ln_w=rwkv_ln_w[l], rwkv_ln_b=rwkv_ln_b[l],
                  gdn_conv_w=gdn_conv_w[l], gdn_a_log=gdn_a_log[l], gdn_dt_bias=gdn_dt_bias[l],
                  gdn_norm=gdn_norm[l])
        mod = mods[l]
        msel = lambda i: mod[:, :, i][:, :, None, :]

        h = norm_modulate(x_all, norm_mix[l], msel(0), msel(1), n_ctx=n_ctx)
        h2d = h.reshape(m_all, d)
        ps = []
        for mixer in ("ssm", "gla", "rwkv", "gdn"):
            wp = _pack_w_in(w_in[l], mixer)
            ps.append(pmatmul(h2d, wp, tm=1024, tn=wp.shape[1]).reshape(bsz, t_all, wp.shape[1]))
        gates = pmatmul(h2d, w_gate[l].astype(BF16), tm=1024, tn=1024, act="sigmoid", out_dtype=BF16)
        gates = gates.reshape(bsz, t_all, 4 * d)

        ssm, gla, rwkv, gdn = mixer_scans(ps, lp, n_ctx=n_ctx)
        x_all = merge_residual(ssm, gla, rwkv, gdn, gates, x_all, msel(2), lp,
                               w_branch[l].astype(BF16), w_out[l].astype(BF16), n_ctx=n_ctx)

        tm_moe = _pick_tile(m_all, MOE_TILE)
        rows = lambda i: mod_rows[l][:, i * d:(i + 1) * d]
        meta, counts, hb = moe_route(x_all, norm_ffn[l], rows(3), rows(4), router_w, router_b,
                                     n_ctx=n_ctx, tm=tm_moe)
        x_all = moe_experts(hb, meta, counts, moe_w_gate[l].astype(BF16), moe_w_up[l].astype(BF16),
                            moe_w_down[l].astype(BF16), x_all, rows(5), n_ctx=n_ctx, tm=tm_moe)

    return final_rms_norm(x_all, final_norm, n_ctx=n_ctx, from_grid=scan_order)
```

```python
import functools
import itertools

import numpy as np
import jax
import jax.numpy as jnp
from jax import lax
from jax.experimental import pallas as pl
from jax.experimental.pallas import tpu as pltpu

F32 = jnp.float32
BF16 = jnp.bfloat16
HI = lax.Precision.HIGHEST

D_MODEL = 1024
DEPTH = 2
GRID_W = 64
CHUNK = 64
EPS = 1e-6
BRANCH = D_MODEL // 2
SSM_HEADS, SSM_P, SSM_GROUPS, SSM_N = 8, 64, 2, 64
GLA_HEADS, GLA_DK, GLA_DV, GLA_RANK, GLA_TAU = 4, 64, 128, 16, 16.0
RWKV_HEADS, RWKV_N, RWKV_LN_EPS = 8, 64, 64e-5
GDN_HEADS, GDN_N = 4, 128
N_EXPERTS, N_GROUPS, EXPERTS_PER_GROUP = 16, 4, 4
EXPERT_FF = D_MODEL // 2
LANES = 128
SUBLANES = 8
VMEM_LIMIT = 48 * 1024 * 1024
MOE_VMEM_LIMIT = 56 * 1024 * 1024
ROW_TILE = 256

_REF_BLOCKS = (
    ("ssm", "z", 512), ("ssm", "xbc", 768), ("ssm", "dt", 16),
    ("gla", "q", 256), ("gla", "k", 256), ("gla", "v", 512), ("gla", "r", 512), ("gla", "glr", 32),
    ("rwkv", "all", 1920),
    ("gdn", "qkv", 1536), ("gdn", "z", 512), ("gdn", "ab", 16),
)
_PACKED = {
    "ssm": (("z", 512), ("dt", 128), ("pad", 128), ("xbc", 768)),
    "gla": (("q", 256), ("k", 256), ("v", 512), ("r", 512), ("glr", 128)),
    "rwkv": (("all", 1920),),
    "gdn": (("qkv", 1536), ("z", 512), ("ab", 128)),
}


def _packed_columns():
    start, s = {}, 0
    for mixer, blk, w in _REF_BLOCKS:
        start[(mixer, blk)] = (s, w)
        s += w
    out = {}
    for mixer, blocks in _PACKED.items():
        cols = []
        for blk, wp in blocks:
            s0, w = start.get((mixer, blk), (0, 0))
            cols += list(range(s0, s0 + w)) + [-1] * (wp - w)
        out[mixer] = np.asarray(cols, np.int32)
    return out


_SRC_COLS = _packed_columns()


def _cparams(*sem):
    return pltpu.CompilerParams(dimension_semantics=sem, vmem_limit_bytes=VMEM_LIMIT)


def _dot(a, b):
    return jnp.dot(a.astype(BF16), b.astype(BF16), preferred_element_type=F32)


def _dot_nt(a, b):
    return lax.dot_general(a.astype(BF16), b.astype(BF16), (((1,), (1,)), ((), ())),
                           preferred_element_type=F32)


def _dot_tn(a, b):
    return lax.dot_general(a.astype(BF16), b.astype(BF16), (((0,), (0,)), ((), ())),
                           preferred_element_type=F32)


def _dot_hi(a, b):
    return jnp.dot(a, b, precision=HI, preferred_element_type=F32)


def _dot_x3(a, b):
    ah = a.astype(BF16)
    al = (a - ah.astype(F32)).astype(BF16)
    bh = b.astype(BF16)
    bl = (b - bh.astype(F32)).astype(BF16)
    f = lambda u, v: jnp.dot(u, v, preferred_element_type=F32)
    return f(ah, bh) + (f(ah, bl) + f(al, bh))


def _dot_x2(a, w):
    ah = a.astype(BF16)
    al = (a - ah.astype(F32)).astype(BF16)
    return jnp.dot(ah, w, preferred_element_type=F32) + jnp.dot(al, w, preferred_element_type=F32)


def _softplus(x):
    return jnp.maximum(x, 0.0) + jnp.log(1.0 + jnp.exp(-jnp.abs(x)))


def _sigmoid(x):
    return 1.0 / (1.0 + jnp.exp(-x))


def _silu(x):
    return x * _sigmoid(x)


def _pick_tile(m, pref):
    t = pref
    while m % t:
        t //= 2
    return t


def _group_sums(y, n):
    m = min(n, LANES)
    row = lax.broadcasted_iota(jnp.int32, (LANES, LANES), 0)
    col = lax.broadcasted_iota(jnp.int32, (LANES, LANES), 1)
    same = ((row // m) == (col // m)).astype(BF16)
    parts = [_dot_x2(y[:, j:j + LANES], same) for j in range(0, y.shape[1], LANES)]
    k = n // m
    if k > 1:
        parts = [sum(parts[g * k:(g + 1) * k]) for g in range(len(parts) // k) for _ in range(k)]
    return jnp.concatenate(parts, axis=1)


def _mm_kernel(a_ref, w_ref, o_ref, *, act, precise):
    if precise:
        r = _dot_hi(a_ref[...].astype(F32), w_ref[...].astype(F32))
    else:
        r = _dot(a_ref[...], w_ref[...])
    if act == "sigmoid":
        r = _sigmoid(r)
    o_ref[...] = r.astype(o_ref.dtype)


def pmatmul(a, w, *, tm, tn, act=None, precise=False, out_dtype=F32):
    m, k = a.shape
    tm = _pick_tile(m, tm)
    if w.ndim == 3:
        per = w.shape[2] // tn
        n = w.shape[0] * w.shape[2]
        w_spec = pl.BlockSpec((None, k, tn), lambda j, i: (j // per, 0, j % per))
    else:
        n = w.shape[1]
        w_spec = pl.BlockSpec((k, tn), lambda j, i: (0, j))
    assert tm % SUBLANES == 0 and n % tn == 0 and w.shape[-1] % tn == 0, (m, tm, w.shape, tn)
    return pl.pallas_call(
        functools.partial(_mm_kernel, act=act, precise=precise),
        grid=(n // tn, m // tm),
        in_specs=[pl.BlockSpec((tm, k), lambda j, i: (i, 0)), w_spec],
        out_specs=pl.BlockSpec((tm, tn), lambda j, i: (i, j)),
        out_shape=jax.ShapeDtypeStruct((m, n), out_dtype),
        compiler_params=_cparams("parallel", "parallel"),
    )(a, w)


def _norm_mod_kernel(x_ref, w_ref, shift_ref, scale_ref, o_ref):
    x = x_ref[0]
    y = x * lax.rsqrt(jnp.mean(x * x, axis=-1, keepdims=True) + EPS) * w_ref[...]
    o_ref[0] = (y * (1.0 + scale_ref[0, 0]) + shift_ref[0, 0]).astype(o_ref.dtype)


def _mod_sel(n_lat_tiles):
    return lambda bi, i, *_: (bi, jnp.where(i < n_lat_tiles, 1, 0), 0, 0)


def norm_modulate(x_all, w, shift, scale, *, n_ctx, out_dtype=BF16):
    b, t, d = x_all.shape
    tm = _pick_tile(n_ctx, ROW_TILE)
    assert t % tm == 0
    tok = lambda bi, i: (bi, i, 0)
    return pl.pallas_call(
        _norm_mod_kernel,
        grid=(b, t // tm),
        in_specs=[pl.BlockSpec((1, tm, d), tok),
                  pl.BlockSpec((1, d), lambda bi, i: (0, 0)),
                  pl.BlockSpec((1, 1, 1, d), _mod_sel((t - n_ctx) // tm)),
                  pl.BlockSpec((1, 1, 1, d), _mod_sel((t - n_ctx) // tm))],
        out_specs=pl.BlockSpec((1, tm, d), tok),
        out_shape=jax.ShapeDtypeStruct((b, t, d), out_dtype),
        compiler_params=_cparams("parallel", "parallel"),
    )(x_all, w.reshape(1, d), shift, scale)


def _row(v):
    return v.reshape(1, -1).astype(F32)


def _const_spec(shape):
    return pl.BlockSpec(shape, lambda *_: (0,) * len(shape))


def _tile_specs(tt, width, col):
    r8 = tt // SUBLANES
    main = pl.BlockSpec((1, tt, width), lambda bi, i: (bi, i, col))
    prev = pl.BlockSpec((1, SUBLANES, width), lambda bi, i: (bi, jnp.maximum(i * r8 - 1, 0), col))
    return main, prev, r8


def _halo_specs(tt, width, col, t):
    main, prev, r8 = _tile_specs(tt, width, col)
    last8 = t // SUBLANES - 1
    nxt = pl.BlockSpec((1, SUBLANES, width), lambda bi, i: (bi, jnp.minimum((i + 1) * r8, last8), col))
    return [main, prev, nxt]


def _neighbours(x, prev8, next8, *, nct, nt):
    i = pl.program_id(1)
    tt = x.shape[0]
    row = lax.broadcasted_iota(jnp.int32, x.shape, 0)
    first = (i == 0) | (i == nct)
    last = (i == nct - 1) | (i == nt - 1)
    pr = jnp.where(first, 0.0, prev8[SUBLANES - 1:SUBLANES, :])
    nx = jnp.where(last, 0.0, next8[0:1, :])
    xp = jnp.where(row == 0, pr, pltpu.roll(x, 1, 0))
    xn = jnp.where(row == tt - 1, nx, pltpu.roll(x, tt - 1, 0))
    return xp, xn


def _prep_call(kernel, ins, in_specs, out_widths, *, b, t, tt, out_dtype=F32):
    tok = lambda bi, i: (bi, i, 0)
    return pl.pallas_call(
        kernel,
        grid=(b, t // tt),
        in_specs=in_specs,
        out_specs=[pl.BlockSpec((1, tt, w), tok) for w in out_widths],
        out_shape=[jax.ShapeDtypeStruct((b, t, w), out_dtype) for w in out_widths],
        compiler_params=_cparams("parallel", "parallel"),
    )(*ins)


def _chunk_masks(reverse):
    row = lax.broadcasted_iota(jnp.int32, (CHUNK, CHUNK), 0)
    col = lax.broadcasted_iota(jnp.int32, (CHUNK, CHUNK), 1)
    if reverse:
        return col >= row, col > row
    return col <= row, col < row


def _chunk_order(i, n_ctx_chunks, n_chunks, reverse):
    n_lat_chunks = n_chunks - n_ctx_chunks
    if not reverse:
        return jnp.where(i < n_ctx_chunks, n_lat_chunks + i, i - n_ctx_chunks)
    return jnp.where(i < n_ctx_chunks, n_chunks - 1 - i, n_lat_chunks - 1 - (i - n_ctx_chunks))


def _split3(a):
    hi = a.astype(BF16)
    r = a - hi.astype(F32)
    mid = r.astype(BF16)
    return hi, mid, (r - mid.astype(F32)).astype(BF16)


def _transpose_small(x):
    row = lax.broadcasted_iota(jnp.int32, (LANES, LANES), 0)
    col = lax.broadcasted_iota(jnp.int32, (LANES, LANES), 1)
    eye = (row == col).astype(BF16)
    nt = lambda p: lax.dot_general(eye, p, (((1,), (1,)), ((), ())), preferred_element_type=F32)
    hi, mid, lo = _split3(x)
    return nt(hi) + (nt(mid) + nt(lo))


def _chunk_cumsum(incl, x):
    m = incl.astype(BF16)
    hi, mid, lo = _split3(x)
    f = lambda p: jnp.dot(m, p, preferred_element_type=F32)
    return f(hi) + (f(mid) + f(lo))


def _select_columns(x, sel):
    c = x.shape[0]
    y = jnp.dot(jnp.concatenate(_split3(x), axis=0), sel, preferred_element_type=F32)
    return y[:c] + (y[c:2 * c] + y[2 * c:])


def _unit_tri_solve(mats, rhs, precise_levels=-1, explicit=False):
    n = range(len(mats))
    if explicit:
        row = lax.broadcasted_iota(jnp.int32, (CHUNK, CHUNK), 0)
        col = lax.broadcasted_iota(jnp.int32, (CHUNK, CHUNK), 1)
        x = [(row == col).astype(F32) - mats[h] for h in n]
    else:
        first = _dot_x3 if precise_levels >= 0 else _dot
        x = [rhs[h] - first(mats[h], rhs[h]) for h in n]
    yield
    p = mats
    for level in range(int(np.log2(CHUNK)) - 1):
        dot = _dot_x3 if level < precise_levels else _dot
        p = [dot(p[h], p[h]) for h in n]
        yield
        x = [x[h] + (dot(x[h], p[h]) if explicit else dot(p[h], x[h])) for h in n]
        yield
    if explicit:
        x = [_dot_x3(x[h], rhs[h]) for h in n]
        yield
    return x


def _bidir_scan(body, tok_ins, const_ins, state_shape, *, b, t, n_ctx, lockstep=True, batch_block=1):
    nc, ncc = t // CHUNK, n_ctx // CHUNK
    nb = batch_block
    assert b % nb == 0

    def chunk_spec(width, col, reverse):
        return pl.BlockSpec((nb, CHUNK, width), lambda bi, i: (bi, _chunk_order(i, ncc, nc, reverse), col))

    def direction(reverse):
        d = int(reverse)
        specs = [chunk_spec(w, cols[d], reverse) for _, w, *cols in tok_ins]
        specs += [_const_spec(pair[d].shape) for pair in const_ins]
        return specs, [a for a, *_ in tok_ins] + [pair[d] for pair in const_ins]

    (spec_f, arg_f), (spec_b, arg_b) = direction(False), direction(True)
    n_tok, n_in = len(tok_ins), len(arg_f)

    def kern(*refs):
        o_f, o_b, s_f, s_b = refs[2 * n_in:]

        @pl.when(pl.program_id(1) == 0)
        def _():
            s_f[...] = jnp.zeros_like(s_f)
            s_b[...] = jnp.zeros_like(s_b)

        def one(j, ins, o_ref, s_ref, reverse):
            ins = [r.at[pl.ds(j, 1)] if k < n_tok else r for k, r in enumerate(ins)]
            return body(*ins, o_ref.at[pl.ds(j, 1)], s_ref.at[j], reverse=reverse)

        gens = []
        for j in range(nb):
            gens += [one(j, refs[:n_in], o_f, s_f, False), one(j, refs[n_in:2 * n_in], o_b, s_b, True)]
        if not lockstep:
            gens = [itertools.chain(*gens)]
        while gens:
            gens = [g for g in gens if next(g, _DONE) is not _DONE]

    return pl.pallas_call(
        kern,
        grid=(b // nb, nc),
        in_specs=spec_f + spec_b,
        out_specs=[chunk_spec(BRANCH, 0, False), chunk_spec(BRANCH, 0, True)],
        out_shape=[jax.ShapeDtypeStruct((b, t, BRANCH), BF16)] * 2,
        scratch_shapes=[pltpu.VMEM((nb,) + tuple(state_shape), F32)] * 2,
        compiler_params=_cparams("parallel", "arbitrary"),
    )(*arg_f, *arg_b)


_DONE = object()


def _batch_block(b, pref):
    return pref if b % pref == 0 else 1


def _ssm_prep_kernel(x_ref, xp_ref, xn_ref, dt_ref, cw_ref, cb_ref, dtb_ref, xs_ref, bc_ref, sm_ref,
                     *, nct, nt):
    x = x_ref[0]
    xp, xn = _neighbours(x, xp_ref[0], xn_ref[0], nct=nct, nt=nt)
    y = _silu(xp * cw_ref[0:1, :] + x * cw_ref[1:2, :] + xn * cw_ref[2:3, :] + cb_ref[...])
    xs_ref[0] = y[:, :BRANCH]
    bc_ref[0] = y[:, BRANCH:]
    sm_ref[0] = _softplus(dt_ref[0] + dtb_ref[...])


def ssm_prep(p, lp, *, n_ctx):
    b, t, _ = p.shape
    tt = _pick_tile(n_ctx, ROW_TILE)
    dtb = jnp.pad(lp["ssm_dt_bias"].reshape(1, -1), ((0, 0), (0, LANES - 2 * SSM_HEADS)))
    specs = _halo_specs(tt, 768, 1, t) + [pl.BlockSpec((1, tt, LANES), lambda bi, i: (bi, i, 4)),
                                          _const_spec((3, 768)), _const_spec((1, 768)), _const_spec((1, LANES))]
    kern = functools.partial(_ssm_prep_kernel, nct=(t - n_ctx) // tt, nt=t // tt)
    return _prep_call(kern, (p, p, p, p, lp["ssm_conv_w"], _row(lp["ssm_conv_b"]), dtb), specs,
                      (BRANCH, 2 * SSM_GROUPS * SSM_N, LANES), b=b, t=t, tt=tt)


def _ssd_body(x_ref, bc_ref, sm_ref, na_ref, o_ref, s_ref, *, reverse):
    incl, _ = _chunk_masks(reverse)
    last = 0 if reverse else CHUNK - 1
    off = SSM_HEADS if reverse else 0
    dt_all = sm_ref[0]
    g_all = _chunk_cumsum(incl, dt_all * na_ref[...])
    yield
    expand = _expand_matrix(off, SSM_HEADS, SSM_P)
    gx = _select_columns(g_all, expand)
    dx = _select_columns(dt_all, expand)
    gt_all = _transpose_small(g_all)
    dtt_all = _transpose_small(dt_all)
    yield
    heads = range(SSM_HEADS)
    rep = SSM_HEADS // SSM_GROUPS
    gw = SSM_GROUPS * SSM_N
    hs = [slice(h * SSM_P, (h + 1) * SSM_P) for h in heads]
    glx = gx[last:last + 1, :]
    egx = jnp.exp(gx)
    wx = dx * jnp.exp(glx - gx)
    eglx = jnp.exp(glx)
    x = x_ref[0]
    bm = [bc_ref[0, :, grp * SSM_N:(grp + 1) * SSM_N] for grp in range(SSM_GROUPS)]
    cm = [bc_ref[0, :, gw + grp * SSM_N:gw + (grp + 1) * SSM_N] for grp in range(SSM_GROUPS)]
    cb = [_dot_nt(cm[grp], bm[grp]) for grp in range(SSM_GROUPS)]
    s = [s_ref[grp] for grp in range(SSM_GROUPS)]
    yield
    scores = [cb[h // rep] * jnp.exp(jnp.where(incl, gx[:, hs[h]] - gt_all[off + h:off + h + 1, :], -jnp.inf))
              * dtt_all[off + h:off + h + 1, :] for h in heads]
    yield
    intra = [_dot(scores[h], x[:, hs[h]]) for h in heads]
    yield
    inter = [_dot(cm[grp], s[grp]) for grp in range(SSM_GROUPS)]
    yield
    upd = [_dot_tn(bm[h // rep] * wx[:, hs[h]], x[:, hs[h]]) for h in heads]
    yield
    for h in heads:
        grp, ls = h // rep, slice((h % rep) * SSM_P, (h % rep + 1) * SSM_P)
        o_ref[0, :, hs[h]] = (intra[h] + egx[:, hs[h]] * inter[grp][:, ls]).astype(o_ref.dtype)
        s_ref[grp, :, ls] = s[grp][:, ls] * eglx[:, hs[h]] + upd[h]


def _expand_matrix(off, n_heads, width):
    row = lax.broadcasted_iota(jnp.int32, (LANES, n_heads * width), 0)
    col = lax.broadcasted_iota(jnp.int32, (LANES, n_heads * width), 1)
    lo = row * width - off * width
    return ((col >= lo) & (col < lo + width)).astype(BF16)


def ssd_scan(xs, bc, sm, neg_a, *, n_ctx):
    b, t, _ = xs.shape
    toks = [(xs, BRANCH, 0, 0), (bc, 2 * SSM_GROUPS * SSM_N, 0, 0), (sm, LANES, 0, 0)]
    state = (SSM_GROUPS, SSM_N, (SSM_HEADS // SSM_GROUPS) * SSM_P)
    return _bidir_scan(_ssd_body, toks, [(neg_a, neg_a)], state, b=b, t=t, n_ctx=n_ctx,
                       batch_block=_batch_block(b, 2))


def _gla_body(q_ref, k_ref, v_ref, glr_ref, w2_ref, gb_ref, o_ref, s_ref, *, reverse):
    incl, _ = _chunk_masks(reverse)
    last = 0 if reverse else CHUNK - 1
    logit = _dot_x3(glr_ref[0], w2_ref[...]) + gb_ref[...]
    yield
    la = -_softplus(-logit) * (1.0 / GLA_TAU)
    g_all = _chunk_cumsum(incl, la)
    yield
    heads = range(GLA_HEADS)
    ks = [slice(h * GLA_DK, (h + 1) * GLA_DK) for h in heads]
    vs = [slice(h * GLA_DV, (h + 1) * GLA_DV) for h in heads]
    g = [g_all[:, ks[h]] for h in heads]
    gl = [g[h][last:last + 1, :] for h in heads]
    k = [k_ref[0, :, ks[h]] for h in heads]
    v = [v_ref[0, :, vs[h]] for h in heads]
    qg = [q_ref[0, :, ks[h]] * (GLA_DK ** -0.5) * jnp.exp(g[h]) for h in heads]
    st = [s_ref[h] for h in heads]
    yield
    scores = [jnp.where(incl, _dot_nt(qg[h], k[h] * jnp.exp(-g[h])), 0.0) for h in heads]
    yield
    intra = [_dot(scores[h], v[h]) for h in heads]
    yield
    inter = [_dot_nt(qg[h], st[h]) for h in heads]
    yield
    upd = [_dot_tn(v[h], k[h] * jnp.exp(gl[h] - g[h])) for h in heads]
    yield
    for h in heads:
        o_ref[0, :, vs[h]] = (intra[h] + inter[h]).astype(o_ref.dtype)
        s_ref[h] = st[h] * jnp.exp(gl[h]) + upd[h]


def gla_scan(p, w2_pair, gb_pair, *, n_ctx):
    b, t, _ = p.shape
    kwid = GLA_HEADS * GLA_DK
    toks = [(p, kwid, 0, 0), (p, kwid, 1, 1), (p, BRANCH, 1, 1), (p, LANES, 12, 12)]
    return _bidir_scan(_gla_body, toks, [w2_pair, gb_pair], (GLA_HEADS, GLA_DV, GLA_DK), b=b, t=t, n_ctx=n_ctx,
                       batch_block=_batch_block(b, 4))


def _rwkv_prep_kernel(x_ref, xp_ref, xn_ref, mu_ref, w2_ref, w0_ref, a2_ref, a0_ref, g2_ref, kk_ref_w,
                      ka_ref, rk_ref, r_ref, k_ref, v_ref, kk_ref, a_ref, lw_ref, g_ref, bo_ref,
                      *, nct, nt):
    x = x_ref[0]
    xp, xn = _neighbours(x, xp_ref[0], xn_ref[0], nct=nct, nt=nt)
    x = x + mu_ref[...] * (0.5 * (xp + xn) - x)
    r, k, v = x[:, :BRANCH], x[:, BRANCH:2 * BRANCH], x[:, 2 * BRANCH:3 * BRANCH]
    wlr = x[:, 3 * BRANCH:3 * BRANCH + LANES]
    alr = x[:, 3 * BRANCH + LANES:3 * BRANCH + 2 * LANES]
    glr = x[:, 3 * BRANCH + 2 * LANES:]
    w_raw = _dot_x3(jnp.tanh(wlr), w2_ref[...]) + w0_ref[...]
    lw_ref[0] = _sigmoid(w_raw) * (-float(np.exp(-0.5)))
    a = _sigmoid(_dot_x3(alr, a2_ref[...]) + a0_ref[...])
    a_ref[0] = a
    g_ref[0] = _dot_x3(_sigmoid(glr), g2_ref[...])
    kk = k * kk_ref_w[...]
    kk_ref[0] = kk * lax.rsqrt(_group_sums(kk * kk, RWKV_N) + EPS)
    ksum = k * (2.0 + (a[:, :BRANCH] + a[:, BRANCH:] - 2.0) * ka_ref[...])
    bo_ref[0] = _group_sums(r * ksum * rk_ref[...], RWKV_N) * v
    r_ref[0] = r
    k_ref[0] = k
    v_ref[0] = v


def rwkv_prep(p, lp, *, n_ctx):
    b, t, w = p.shape
    tt = _pick_tile(n_ctx, ROW_TILE)

    def pair(wp):
        r, c = wp.shape[1:]
        return jnp.zeros((LANES, 2 * c), F32).at[:r, :c].set(wp[0]).at[r:2 * r, c:].set(wp[1])

    consts = (_row(lp["rwkv_mu"]), pair(lp["rwkv_w2"]), _row(lp["rwkv_w0"]), pair(lp["rwkv_a2"]),
              _row(lp["rwkv_a0"]), lp["rwkv_g2"], _row(lp["rwkv_k_k"]), _row(lp["rwkv_k_a"]),
              _row(lp["rwkv_r_k"]))
    specs = _halo_specs(tt, w, 0, t) + [_const_spec(c.shape) for c in consts]
    kern = functools.partial(_rwkv_prep_kernel, nct=(t - n_ctx) // tt, nt=t // tt)
    return _prep_call(kern, (p, p, p) + consts, specs,
                      (BRANCH, BRANCH, BRANCH, BRANCH, 2 * BRANCH, 2 * BRANCH, BRANCH, BRANCH),
                      b=b, t=t, tt=tt)


def _rwkv_body(r_ref, k_ref, v_ref, kk_ref, a_ref, lw_ref, ka_ref, o_ref, s_ref, *, reverse):
    incl, strict = _chunk_masks(reverse)
    last = 0 if reverse else CHUNK - 1
    lw_all = lw_ref[0]
    g_all = _chunk_cumsum(incl, lw_all)
    a_all = a_ref[0]
    k_all = k_ref[0] * (1.0 + (a_all - 1.0) * ka_ref[...])
    yield
    heads = range(RWKV_HEADS)
    hs = [slice(h * RWKV_N, (h + 1) * RWKV_N) for h in heads]
    g = [g_all[:, hs[h]] for h in heads]
    gl = [g[h][last:last + 1, :] for h in heads]
    eneg = [jnp.exp(-g[h]) for h in heads]
    edec = [jnp.exp(gl[h] - g[h]) for h in heads]
    kk = [kk_ref[0, :, hs[h]] for h in heads]
    bvec = [kk[h] * a_all[:, hs[h]] for h in heads]
    k = [k_all[:, hs[h]] for h in heads]
    v = [v_ref[0, :, hs[h]] for h in heads]
    kkg = [kk[h] * jnp.exp(g[h] - lw_all[:, hs[h]]) for h in heads]
    rg = [r_ref[0, :, hs[h]] * jnp.exp(g[h]) for h in heads]
    bh = [bvec[h] * eneg[h] for h in heads]
    kh = [k[h] * eneg[h] for h in heads]
    s = [s_ref[h] for h in heads]
    yield
    both = [jnp.concatenate([kkg[h], rg[h]], axis=0) for h in heads]
    mask2 = jnp.concatenate([strict, incl], axis=0)
    mb = [jnp.where(mask2, _dot_nt(both[h], bh[h]), 0.0) for h in heads]
    yield
    mk = [jnp.where(mask2, _dot_nt(both[h], kh[h]), 0.0) for h in heads]
    yield
    part = [_dot(mk[h], v[h]) + _dot_nt(both[h], s[h]) for h in heads]
    yield
    x = yield from _unit_tri_solve([mb[h][:CHUNK] for h in heads], [part[h][:CHUNK] for h in heads])
    u = [-xh for xh in x]
    for h in heads:
        o_ref[0, :, hs[h]] = (part[h][CHUNK:] + _dot(mb[h][CHUNK:], u[h])).astype(o_ref.dtype)
    yield
    for h in heads:
        upd = _dot_tn(jnp.concatenate([u[h], v[h]], axis=0),
                      jnp.concatenate([bvec[h] * edec[h], k[h] * edec[h]], axis=0))
        s_ref[h] = s[h] * jnp.exp(gl[h]) + upd


def rwkv_scan(r, k, v, kk, a, lw, k_a, *, n_ctx):
    b, t, _ = r.shape
    toks = [(r, BRANCH, 0, 0), (k, BRANCH, 0, 0), (v, BRANCH, 0, 0), (kk, BRANCH, 0, 0),
            (a, BRANCH, 0, 1), (lw, BRANCH, 0, 1)]
    return _bidir_scan(_rwkv_body, toks, [(k_a, k_a)], (RWKV_HEADS, RWKV_N, RWKV_N), b=b, t=t, n_ctx=n_ctx,
                       batch_block=_batch_block(b, 2))


def _gdn_prep_kernel(x_ref, xp_ref, xn_ref, ab_ref, cw_ref, na_ref, dtb_ref,
                     q_ref, k_ref, v_ref, sm_ref, *, nct, nt):
    x = x_ref[0]
    xp, xn = _neighbours(x, xp_ref[0], xn_ref[0], nct=nct, nt=nt)
    y = _silu(xp * cw_ref[0:1, :] + x * cw_ref[1:2, :] + xn * cw_ref[2:3, :])
    q, k = y[:, :BRANCH], y[:, BRANCH:2 * BRANCH]
    q_ref[0] = q * lax.rsqrt(_group_sums(q * q, GDN_N) + EPS) * (GDN_N ** -0.5)
    k_ref[0] = k * lax.rsqrt(_group_sums(k * k, GDN_N) + EPS)
    v_ref[0] = y[:, 2 * BRANCH:]
    ab = ab_ref[0]
    lane = lax.broadcasted_iota(jnp.int32, ab.shape, 1)
    sm_ref[0] = jnp.where(lane < 2 * GDN_HEADS, na_ref[...] * _softplus(ab + dtb_ref[...]), _sigmoid(ab))


def gdn_prep(p, lp, *, n_ctx):
    b, t, _ = p.shape
    tt = _pick_tile(n_ctx, ROW_TILE)
    padrow = lambda v: jnp.pad(v.reshape(1, -1), ((0, 0), (0, LANES - 2 * GDN_HEADS)))
    consts = (lp["gdn_conv_w"], padrow(-jnp.exp(lp["gdn_a_log"])), padrow(lp["gdn_dt_bias"]))
    specs = (_halo_specs(tt, 3 * BRANCH, 0, t) + [pl.BlockSpec((1, tt, LANES), lambda bi, i: (bi, i, 16))]
             + [_const_spec(c.shape) for c in consts])
    kern = functools.partial(_gdn_prep_kernel, nct=(t - n_ctx) // tt, nt=t // tt)
    return _prep_call(kern, (p, p, p, p) + consts, specs, (BRANCH, BRANCH, BRANCH, LANES), b=b, t=t, tt=tt)


def _gdn_body(q_ref, k_ref, v_ref, sm_ref, o_ref, s_ref, *, reverse):
    incl, strict = _chunk_masks(reverse)
    last = 0 if reverse else CHUNK - 1
    off = GDN_HEADS if reverse else 0
    sm = sm_ref[0]
    g_all = _chunk_cumsum(incl, sm)
    yield
    gt_all = _transpose_small(g_all)
    yield
    heads = range(GDN_HEADS)
    hs = [slice(h * GDN_N, (h + 1) * GDN_N) for h in heads]
    g = [g_all[:, off + h:off + h + 1] for h in heads]
    gl = [g[h][last:last + 1, :] for h in heads]
    beta = [sm[:, 2 * GDN_HEADS + off + h:2 * GDN_HEADS + off + h + 1] for h in heads]
    q = [q_ref[0, :, hs[h]] for h in heads]
    k = [k_ref[0, :, hs[h]] for h in heads]
    v = [v_ref[0, :, hs[h]] for h in heads]
    s = [s_ref[h] for h in heads]
    decay = [jnp.exp(jnp.where(incl, g[h] - gt_all[off + h:off + h + 1, :], -jnp.inf)) for h in heads]
    yield
    kq = [_dot_nt(jnp.concatenate([k[h], q[h]], axis=0), k[h]) for h in heads]
    yield
    lower = [jnp.where(strict, kq[h][:CHUNK] * decay[h] * beta[h], 0.0) for h in heads]
    attn = [kq[h][CHUNK:] * decay[h] for h in heads]
    o_part = [_dot(q[h] * jnp.exp(g[h]), s[h]) for h in heads]
    yield
    rhs = [jnp.concatenate([v[h] * beta[h], k[h] * (beta[h] * jnp.exp(g[h]))], axis=1) for h in heads]
    sol = yield from _unit_tri_solve(lower, rhs, precise_levels=2, explicit=True)
    v_new = [sol[h][:, :GDN_N] - _dot(sol[h][:, GDN_N:], s[h]) for h in heads]
    yield
    for h in heads:
        o_ref[0, :, hs[h]] = (o_part[h] + _dot(attn[h], v_new[h])).astype(o_ref.dtype)
    yield
    for h in heads:
        s_ref[h] = s[h] * jnp.exp(gl[h]) + _dot_tn(k[h] * jnp.exp(gl[h] - g[h]), v_new[h])


def gdn_scan(q, k, v, sm, *, n_ctx):
    b, t, _ = q.shape
    toks = [(q, BRANCH, 0, 0), (k, BRANCH, 0, 0), (v, BRANCH, 0, 0), (sm, LANES, 0, 0)]
    return _bidir_scan(_gdn_body, toks, [], (GDN_HEADS, GDN_N, GDN_N), b=b, t=t, n_ctx=n_ctx,
                       batch_block=_batch_block(b, 4))


def _merge_kernel(sf_ref, sb_ref, sx_ref, sz_ref, gf_ref, gb_ref, gr_ref, rf_ref, rb_ref, rg_ref, rbo_ref,
                  df_ref, db_ref, dz_ref, gate_ref, x_ref, m_ref,
                  sd_ref, sn_ref, gn_ref, lnw_ref, lnb_ref, dn_ref,
                  wb_ref, wo_ref, o_ref):
    def group_rms(y, n, w_ref):
        return y * lax.rsqrt(_group_sums(y * y, n) * (1.0 / n) + EPS) * w_ref[...]

    both = lambda f_ref, b_ref: f_ref[0].astype(F32) + b_ref[0].astype(F32)
    y = (both(sf_ref, sb_ref) + sd_ref[...] * sx_ref[0]) * _silu(sz_ref[0])
    ys = group_rms(y, BRANCH // SSM_GROUPS, sn_ref)
    yg = group_rms(both(gf_ref, gb_ref), GLA_DV, gn_ref) * _silu(gr_ref[0])
    y = both(rf_ref, rb_ref)
    yc = y - _group_sums(y, RWKV_N) * (1.0 / RWKV_N)
    var = _group_sums(yc * yc, RWKV_N) * (1.0 / RWKV_N)
    yr = (yc * lax.rsqrt(var + RWKV_LN_EPS) * lnw_ref[...] + lnb_ref[...] + rbo_ref[0]) * rg_ref[0]
    yd = group_rms(both(df_ref, db_ref), GDN_N, dn_ref) * _silu(dz_ref[0])
    acc = None
    for i, yi in enumerate((ys, yg, yr, yd)):
        term = gate_ref[0, :, i * D_MODEL:(i + 1) * D_MODEL].astype(F32) * _dot(yi, wb_ref[i])
        acc = term if acc is None else acc + term
    o_ref[0] = x_ref[0] + m_ref[0, 0] * _dot(acc, wo_ref[...])


def merge_residual(ssm, gla, rwkv, gdn, gates, x_all, gate_mod, lp, w_branch, w_out, *, n_ctx):
    b, t, d = x_all.shape
    tm = _pick_tile(n_ctx, ROW_TILE)
    tok = lambda bi, i: (bi, i, 0)
    blk = lambda c: pl.BlockSpec((1, tm, BRANCH), lambda bi, i: (bi, i, c))
    half = blk(0)
    consts = (_row(jnp.repeat(lp["ssm_d"], SSM_P)), _row(lp["ssm_norm"]),
              _row(jnp.tile(lp["gla_norm"], GLA_HEADS)), _row(lp["rwkv_ln_w"]), _row(lp["rwkv_ln_b"]),
              _row(jnp.tile(lp["gdn_norm"], GDN_HEADS)), w_branch, w_out)
    ins = (ssm[0], ssm[1], ssm[2], ssm[3], gla[0], gla[1], gla[2], rwkv[0], rwkv[1], rwkv[2], rwkv[3],
           gdn[0], gdn[1], gdn[2], gates, x_all, gate_mod) + consts
    specs = ([half, half, half, blk(0), half, half, blk(2), half, half, half, half, half, half, blk(3),
              pl.BlockSpec((1, tm, 4 * d), tok), pl.BlockSpec((1, tm, d), tok),
              pl.BlockSpec((1, 1, 1, d), _mod_sel((t - n_ctx) // tm))]
             + [_const_spec(c.shape) for c in consts])
    return pl.pallas_call(
        _merge_kernel,
        grid=(b, t // tm),
        in_specs=specs,
        out_specs=pl.BlockSpec((1, tm, d), tok),
        out_shape=jax.ShapeDtypeStruct((b, t, d), F32),
        compiler_params=_cparams("parallel", "parallel"),
    )(*ins)


def _route_kernel(x_ref, nw_ref, shift_ref, scale_ref, rw_ref, rb_ref, u_ref, o_ref, cnt_ref, hb_ref, *, rows_kw):
    x = x_ref[...]
    shift, scale = _token_rows([shift_ref, scale_ref], pl.program_id(0), x.shape[0], **rows_kw)
    h = x * lax.rsqrt(jnp.mean(x * x, axis=-1, keepdims=True) + EPS) * nw_ref[...] * (1.0 + scale) + shift
    hb_ref[...] = h.astype(BF16)
    logits = lax.dot_general(rw_ref[...], h, (((1,), (1,)), ((), ())),
                             precision=HI, preferred_element_type=F32)
    scores = _sigmoid(logits)
    sel = scores + rb_ref[...]
    rows = [sel[e:e + 1, :] for e in range(N_EXPERTS)]
    sc = [scores[e:e + 1, :] for e in range(N_EXPERTS)]

    def top2(vals):
        v1, i1 = vals[0], jnp.zeros(vals[0].shape, jnp.int32)
        for j in range(1, len(vals)):
            better = vals[j] > v1
            v1 = jnp.where(better, vals[j], v1)
            i1 = jnp.where(better, j, i1)
        v2 = jnp.where(i1 == 0, vals[1], vals[0])
        i2 = jnp.where(i1 == 0, 1, 0)
        for j in range(1, len(vals)):
            better = (vals[j] > v2) & (i1 != j)
            v2 = jnp.where(better, vals[j], v2)
            i2 = jnp.where(better, j, i2)
        return v1, i1, v2, i2

    gsum = []
    for grp in range(N_GROUPS):
        v1, _, v2, _ = top2(rows[grp * EXPERTS_PER_GROUP:(grp + 1) * EXPERTS_PER_GROUP])
        gsum.append(v1 + v2)
    best, gidx = gsum[0], jnp.zeros(gsum[0].shape, jnp.int32)
    for grp in range(1, N_GROUPS):
        better = gsum[grp] > best
        best = jnp.where(better, gsum[grp], best)
        gidx = jnp.where(better, grp, gidx)
    chosen, chosen_sc = [], []
    for j in range(EXPERTS_PER_GROUP):
        cj, sj = rows[j], sc[j]
        for grp in range(1, N_GROUPS):
            cj = jnp.where(gidx == grp, rows[grp * EXPERTS_PER_GROUP + j], cj)
            sj = jnp.where(gidx == grp, sc[grp * EXPERTS_PER_GROUP + j], sj)
        chosen.append(cj)
        chosen_sc.append(sj)
    _, i1, _, i2 = top2(chosen)
    w1, w2 = jnp.zeros_like(best), jnp.zeros_like(best)
    for j in range(EXPERTS_PER_GROUP):
        w1 = jnp.where(i1 == j, chosen_sc[j], w1)
        w2 = jnp.where(i2 == j, chosen_sc[j], w2)
    tot = w1 + w2
    w1, w2 = w1 / tot, w2 / tot
    tm = scores.shape[1]
    sub = lax.broadcasted_iota(jnp.int32, (SUBLANES, tm), 0)
    ind8 = jnp.zeros((SUBLANES, tm), F32)
    meta = jnp.zeros((SUBLANES, tm), F32)
    for j in range(EXPERTS_PER_GROUP):
        gate_j = jnp.where(i1 == j, w1, 0.0) + jnp.where(i2 == j, w2, 0.0)
        meta = jnp.where(sub == j, gate_j, meta)
    for grp in range(N_GROUPS):
        ind8 = jnp.where((sub == grp) & (gidx == grp), 1.0, ind8)
    before = jnp.dot(ind8.astype(BF16), u_ref[...], preferred_element_type=F32)
    rank = jnp.sum(ind8 * before, axis=0, keepdims=True)
    meta = jnp.where(sub == _META_GROUP, gidx.astype(F32), meta)
    meta = jnp.where(sub == _META_RANK, rank, meta)
    o_ref[...] = meta
    counts = jnp.sum(ind8, axis=1, keepdims=True)
    lane = lax.broadcasted_iota(jnp.int32, (SUBLANES, LANES), 1)
    row = lax.broadcasted_iota(jnp.int32, (SUBLANES, LANES), 0)
    cnt_ref[0] = jnp.broadcast_to(jnp.sum(jnp.where(lane == row, counts, 0.0), axis=0, keepdims=True),
                                  (SUBLANES, LANES)).astype(jnp.int32)


_META_GROUP, _META_RANK = EXPERTS_PER_GROUP, EXPERTS_PER_GROUP + 1


def moe_route(x_all, norm_w, shift_rows, scale_rows, router_w, router_b, *, n_ctx, tm):
    b, t, d = x_all.shape
    m = b * t
    upper = jnp.asarray(np.triu(np.ones((tm, tm), np.float32), 1), BF16)
    return pl.pallas_call(
        functools.partial(_route_kernel, rows_kw=dict(bsz=b, t_all=t, n_ctx=n_ctx)),
        grid=(m // tm,),
        in_specs=[pl.BlockSpec((tm, d), lambda i: (i, 0)),
                  _const_spec((1, d)), _const_spec(shift_rows.shape), _const_spec(scale_rows.shape),
                  pl.BlockSpec((N_EXPERTS, d), lambda i: (0, 0)),
                  pl.BlockSpec((N_EXPERTS, 1), lambda i: (0, 0)),
                  _const_spec((tm, tm))],
        out_specs=[pl.BlockSpec((SUBLANES, tm), lambda i: (0, i)),
                   pl.BlockSpec((1, SUBLANES, LANES), lambda i: (i, 0, 0)),
                   pl.BlockSpec((tm, d), lambda i: (i, 0))],
        out_shape=[jax.ShapeDtypeStruct((SUBLANES, m), F32),
                   jax.ShapeDtypeStruct((m // tm, SUBLANES, LANES), jnp.int32),
                   jax.ShapeDtypeStruct((m, d), BF16)],
        compiler_params=_cparams("parallel"),
    )(x_all.reshape(m, d), norm_w.reshape(1, d), shift_rows, scale_rows,
      router_w.T, router_b.reshape(N_EXPERTS, 1), upper)


MOE_TILE = 1024
MOE_SUB_ROWS = 256
MOE_TAIL_ROWS = 128


def _expert_kernel(cnt_ref, h_ref, mr_ref, mc_ref, wg_ref, wu_ref, wd_ref, x_ref, gate_ref, o_ref, *, rows_kw):
    i, grp = pl.program_id(0), pl.program_id(1)

    @pl.when(grp == 0)
    def _():
        o_ref[...] = jnp.zeros_like(o_ref)

    tm = h_ref.shape[0]
    count = cnt_ref[i * N_GROUPS + grp]
    grp_f = grp.astype(F32)
    sel_row = jnp.where(mr_ref[_META_GROUP:_META_GROUP + 1, :] == grp_f, mr_ref[_META_RANK:_META_RANK + 1, :], -1.0)
    sel_col = jnp.where(mc_ref[:, _META_GROUP:_META_GROUP + 1] == grp_f, mc_ref[:, _META_RANK:_META_RANK + 1], -1.0)
    gate_parts = _split3(mc_ref[...])

    def sub_block(first, rows):
        base = first.astype(F32)
        slot_r = lax.broadcasted_iota(jnp.int32, (rows, tm), 0).astype(F32)
        slot_c = lax.broadcasted_iota(jnp.int32, (tm, rows), 1).astype(F32)
        pick = (sel_row - base == slot_r).astype(BF16)
        put = (sel_col - base == slot_c).astype(BF16)
        xg = jnp.dot(pick, h_ref[...], preferred_element_type=F32).astype(BF16)
        gates = sum(jnp.dot(pick, p, preferred_element_type=F32) for p in gate_parts)
        y = jnp.zeros((rows, o_ref.shape[1]), F32)
        for e in range(EXPERTS_PER_GROUP):
            hid = _silu(_dot(xg, wg_ref[e])) * _dot(xg, wu_ref[e])
            y = y + _dot(gates[:, e:e + 1] * hid, wd_ref[e])
        o_ref[...] += jnp.dot(put, y.astype(BF16), preferred_element_type=F32)

    n_full = count // MOE_SUB_ROWS
    rem = count - n_full * MOE_SUB_ROWS
    n_main = n_full + (rem > MOE_TAIL_ROWS).astype(jnp.int32)

    def main_block(s, carry):
        sub_block(s * MOE_SUB_ROWS, MOE_SUB_ROWS)
        return carry

    lax.fori_loop(0, n_main, main_block, 0)

    @pl.when((rem > 0) & (rem <= MOE_TAIL_ROWS))
    def _():
        sub_block(n_full * MOE_SUB_ROWS, MOE_TAIL_ROWS)

    @pl.when(grp == N_GROUPS - 1)
    def _():
        (gate,) = _token_rows([gate_ref], i, tm, **rows_kw)
        o_ref[...] = x_ref[...] + gate * o_ref[...]


def moe_experts(hb, meta, counts, wg, wu, wd, x_all, gate_rows, *, n_ctx, tm):
    b, t, d = x_all.shape
    m = b * t
    tok = lambda i, g, cnt: (i, 0)
    grid_spec = pltpu.PrefetchScalarGridSpec(
        num_scalar_prefetch=1,
        grid=(m // tm, N_GROUPS),
        in_specs=[pl.BlockSpec((tm, d), tok),
                  pl.BlockSpec((SUBLANES, tm), lambda i, g, cnt: (0, i)),
                  pl.BlockSpec((tm, SUBLANES), tok),
                  pl.BlockSpec((EXPERTS_PER_GROUP, d, EXPERT_FF), lambda i, g, cnt: (g, 0, 0)),
                  pl.BlockSpec((EXPERTS_PER_GROUP, d, EXPERT_FF), lambda i, g, cnt: (g, 0, 0)),
                  pl.BlockSpec((EXPERTS_PER_GROUP, EXPERT_FF, d), lambda i, g, cnt: (g, 0, 0)),
                  pl.BlockSpec((tm, d), tok),
                  pl.BlockSpec(gate_rows.shape, lambda i, g, cnt: (0, 0))],
        out_specs=pl.BlockSpec((tm, d), tok))
    out = pl.pallas_call(
        functools.partial(_expert_kernel, rows_kw=dict(bsz=b, t_all=t, n_ctx=n_ctx)),
        grid_spec=grid_spec,
        out_shape=jax.ShapeDtypeStruct((m, d), F32),
        compiler_params=pltpu.CompilerParams(dimension_semantics=("parallel", "arbitrary"),
                                             vmem_limit_bytes=MOE_VMEM_LIMIT),
    )(counts[:, 0, :N_GROUPS].reshape(-1), hb, meta, meta.T, wg, wu, wd, x_all.reshape(m, d), gate_rows)
    return out.reshape(b, t, d)


def _token_rows(m_refs, tile, tm, *, bsz, t_all, n_ctx):
    row = tile * tm + lax.broadcasted_iota(jnp.int32, (tm, 1), 0)
    ctx = jnp.zeros((tm, 1), jnp.bool_)
    lat = []
    for bi in range(bsz):
        lo, split = bi * t_all, bi * t_all + t_all - n_ctx
        lat.append((row >= lo) & (row < split))
        ctx = ctx | ((row >= split) & (row < lo + t_all))
    out = []
    for m_ref in m_refs:
        v = jnp.where(ctx, m_ref[bsz:bsz + 1, :], 0.0)
        for bi in range(bsz):
            v = v + jnp.where(lat[bi], m_ref[bi:bi + 1, :], 0.0)
        out.append(v)
    return out


GRID_TILE_COLS = SUBLANES
GRID_TILE = GRID_W * GRID_TILE_COLS


def _grid_view(x_all, n_lat):
    b, t, d = x_all.shape
    rows = n_lat // GRID_W
    assert rows == GRID_W and t % GRID_W == 0
    last = GRID_W // GRID_TILE_COLS - 1
    spec = pl.BlockSpec((1, rows, GRID_TILE_COLS, d), lambda bi, i: (bi, 0, jnp.minimum(i, last), 0))
    return x_all.reshape(b, t // GRID_W, GRID_W, d), spec


def _grid_tile(xg_ref):
    return jnp.concatenate([xg_ref[0, :, j, :] for j in range(xg_ref.shape[2])], axis=0)


def _transpose_grid_kernel(xn_ref, xg_ref, o_ref, *, n_lat_tiles):
    i = pl.program_id(1)

    @pl.when(i < n_lat_tiles)
    def _():
        o_ref[0] = _grid_tile(xg_ref)

    @pl.when(i >= n_lat_tiles)
    def _():
        o_ref[0] = xn_ref[0]


def transpose_grid(x_all, *, n_ctx):
    b, t, d = x_all.shape
    xg, gspec = _grid_view(x_all, t - n_ctx)
    tok = pl.BlockSpec((1, GRID_TILE, d), lambda bi, i: (bi, i, 0))
    return pl.pallas_call(
        functools.partial(_transpose_grid_kernel, n_lat_tiles=(t - n_ctx) // GRID_TILE),
        grid=(b, pl.cdiv(t, GRID_TILE)),
        in_specs=[tok, gspec],
        out_specs=tok,
        out_shape=jax.ShapeDtypeStruct((b, t, d), x_all.dtype),
        compiler_params=_cparams("parallel", "parallel"),
    )(x_all, xg)


def _final_norm_kernel(x_ref, w_ref, o_ref, *, from_grid):
    x = _grid_tile(x_ref) if from_grid else x_ref[0]
    o_ref[0] = x * lax.rsqrt(jnp.mean(x * x, axis=-1, keepdims=True) + EPS) * w_ref[...]


def final_rms_norm(x_all, w, *, n_ctx, from_grid):
    b, t, d = x_all.shape
    n_lat = t - n_ctx
    tt = GRID_TILE
    tok = pl.BlockSpec((1, tt, d), lambda bi, i: (bi, i, 0))
    if from_grid:
        x_in, spec = _grid_view(x_all, n_lat)
    else:
        x_in, spec = x_all, tok
    return pl.pallas_call(
        functools.partial(_final_norm_kernel, from_grid=from_grid),
        grid=(b, n_lat // tt),
        in_specs=[spec, _const_spec((1, d))],
        out_specs=tok,
        out_shape=jax.ShapeDtypeStruct((b, n_lat, d), F32),
        compiler_params=_cparams("parallel", "parallel"),
    )(x_in, w.reshape(1, d))


def _pack_w_in(w_in, mixer):
    cols = _SRC_COLS[mixer]
    pieces, i = [], 0
    while i < len(cols):
        j = i
        if cols[i] < 0:
            while j < len(cols) and cols[j] < 0:
                j += 1
            pieces.append(jnp.zeros((w_in.shape[0], j - i), w_in.dtype))
        else:
            while j < len(cols) and cols[j] == cols[i] + (j - i):
                j += 1
            pieces.append(w_in[:, int(cols[i]):int(cols[i]) + (j - i)])
        i = j
    return jnp.concatenate(pieces, axis=1).astype(BF16)


def mixer_scans(ps, lp, *, n_ctx):
    p_ssm, p_gla, p_rwkv, p_gdn = ps

    xs, bc, sm = ssm_prep(p_ssm, lp, n_ctx=n_ctx)
    neg_a = jnp.pad(-jnp.exp(lp["ssm_a_log"]).reshape(1, -1), ((0, 0), (0, LANES - 2 * SSM_HEADS)))
    ssm = tuple(ssd_scan(xs, bc, sm, neg_a, n_ctx=n_ctx)) + (xs, p_ssm)

    w2 = [jnp.zeros((LANES, GLA_HEADS * GLA_DK), F32).at[d * GLA_RANK:(d + 1) * GLA_RANK].set(lp["gla_w2"][d])
          for d in range(2)]
    gb = [_row(lp["gla_b"][d]) for d in range(2)]
    gla = tuple(gla_scan(p_gla, w2, gb, n_ctx=n_ctx)) + (p_gla,)

    r, k, v, kk, a, lw, g, bonus = rwkv_prep(p_rwkv, lp, n_ctx=n_ctx)
    rwkv = tuple(rwkv_scan(r, k, v, kk, a, lw, _row(lp["rwkv_k_a"]), n_ctx=n_ctx)) + (g, bonus)

    q, kd, vd, smd = gdn_prep(p_gdn, lp, n_ctx=n_ctx)
    gdn = tuple(gdn_scan(q, kd, vd, smd, n_ctx=n_ctx)) + (p_gdn,)
    return ssm, gla, rwkv, gdn


def kernel(x, c, ctx, c_ctx, ada_w, ada_b, norm_mix, norm_ffn, w_in, w_gate, w_branch, w_out, ssm_conv_w, ssm_conv_b, ssm_a_log, ssm_dt_bias, ssm_d, ssm_norm, gla_w2, gla_b, gla_norm, rwkv_mu, rwkv_w0, rwkv_w2, rwkv_a0, rwkv_a2, rwkv_g2, rwkv_k_k, rwkv_k_a, rwkv_r_k, rwkv_ln_w, rwkv_ln_b, gdn_conv_w, gdn_a_log, gdn_dt_bias, gdn_norm, router_w, router_b, moe_w_gate, moe_w_up, moe_w_down, final_norm):
    bsz, seq, d = x.shape
    n_ctx = ctx.shape[1]
    t_all = n_ctx + seq
    m_all = bsz * t_all

    cond = jnp.concatenate([jax.nn.silu(c), jax.nn.silu(c_ctx)[None]], 0)
    cond = jnp.pad(cond, ((0, SUBLANES - cond.shape[0]), (0, 0)))
    mods, mod_rows = [], []
    for l in range(DEPTH):
        mod = pmatmul(cond, ada_w[l], tm=SUBLANES, tn=1024, precise=True) + ada_b[l]
        mod_rows.append(mod)
        lat = mod[:bsz].reshape(bsz, 6, d)
        cx = jnp.broadcast_to(mod[bsz].reshape(1, 6, d), (bsz, 6, d))
        mods.append(jnp.stack([cx, lat], axis=1))

    x_all = jnp.concatenate([x, ctx], axis=1)
    scan_order = False
    for l in range(DEPTH):
        if (l % 2 == 1) != scan_order:
            x_all = transpose_grid(x_all, n_ctx=n_ctx)
            scan_order = not scan_order
        lp = dict(ssm_conv_w=ssm_conv_w[l], ssm_conv_b=ssm_conv_b[l], ssm_a_log=ssm_a_log[l],
                  ssm_dt_bias=ssm_dt_bias[l], ssm_d=ssm_d[l], ssm_norm=ssm_norm[l],
                  gla_w2=gla_w2[l], gla_b=gla_b[l], gla_norm=gla_norm[l],
                  rwkv_mu=rwkv_mu[l], rwkv_w0=rwkv_w0[l], rwkv_w2=rwkv_w2[l], rwkv_a0=rwkv_a0[l],
                  rwkv_a2=rwkv_a2[l], rwkv_g2=rwkv_g2[l], rwkv_k_k=rwkv_k_k[l], rwkv_k_a=rwkv_k_a[l],
                  rwkv_r_k=rwkv_r_k[l], rwkv_ln_w=rwkv_ln_w[l], rwkv_ln_b=rwkv_ln_b[l],
                  gdn_conv_w=gdn_conv_w[l], gdn_a_log=gdn_a_log[l], gdn_dt_bias=gdn_dt_bias[l],
                  gdn_norm=gdn_norm[l])
        mod = mods[l]
        msel = lambda i: mod[:, :, i][:, :, None, :]

        h = norm_modulate(x_all, norm_mix[l], msel(0), msel(1), n_ctx=n_ctx)
        h2d = h.reshape(m_all, d)
        ps = []
        for mixer in ("ssm", "gla", "rwkv", "gdn"):
            wp = _pack_w_in(w_in[l], mixer)
            ps.append(pmatmul(h2d, wp, tm=1024, tn=wp.shape[1]).reshape(bsz, t_all, wp.shape[1]))
        gates = pmatmul(h2d, w_gate[l].astype(BF16), tm=1024, tn=1024, act="sigmoid", out_dtype=BF16)
        gates = gates.reshape(bsz, t_all, 4 * d)

        ssm, gla, rwkv, gdn = mixer_scans(ps, lp, n_ctx=n_ctx)
        x_all = merge_residual(ssm, gla, rwkv, gdn, gates, x_all, msel(2), lp,
                               w_branch[l].astype(BF16), w_out[l].astype(BF16), n_ctx=n_ctx)

        tm_moe = _pick_tile(m_all, MOE_TILE)
        rows = lambda i: mod_rows[l][:, i * d:(i + 1) * d]
        meta, counts, hb = moe_route(x_all, norm_ffn[l], rows(3), rows(4), router_w, router_b,
                                     n_ctx=n_ctx, tm=tm_moe)
        x_all = moe_experts(hb, meta, counts, moe_w_gate[l].astype(BF16), moe_w_up[l].astype(BF16),
                            moe_w_down[l].astype(BF16), x_all, rows(5), n_ctx=n_ctx, tm=tm_moe)

    return final_rms_norm(x_all, final_norm, n_ctx=n_ctx, from_grid=scan_order)
```

```python
import functools
import itertools

import numpy as np
import jax
import jax.numpy as jnp
from jax import lax
from jax.experimental import pallas as pl
from jax.experimental.pallas import tpu as pltpu

F32 = jnp.float32
BF16 = jnp.bfloat16
HI = lax.Precision.HIGHEST

D_MODEL = 1024
DEPTH = 2
GRID_W = 64
CHUNK = 64
EPS = 1e-6
BRANCH = D_MODEL // 2
SSM_HEADS, SSM_P, SSM_GROUPS, SSM_N = 8, 64, 2, 64
GLA_HEADS, GLA_DK, GLA_DV, GLA_RANK, GLA_TAU = 4, 64, 128, 16, 16.0
RWKV_HEADS, RWKV_N, RWKV_LN_EPS = 8, 64, 64e-5
GDN_HEADS, GDN_N = 4, 128
N_EXPERTS, N_GROUPS, EXPERTS_PER_GROUP = 16, 4, 4
EXPERT_FF = D_MODEL // 2
LANES = 128
SUBLANES = 8
VMEM_LIMIT = 48 * 1024 * 1024
MOE_VMEM_LIMIT = 56 * 1024 * 1024
ROW_TILE = 256

_REF_BLOCKS = (
    ("ssm", "z", 512), ("ssm", "xbc", 768), ("ssm", "dt", 16),
    ("gla", "q", 256), ("gla", "k", 256), ("gla", "v", 512), ("gla", "r", 512), ("gla", "glr", 32),
    ("rwkv", "all", 1920),
    ("gdn", "qkv", 1536), ("gdn", "z", 512), ("gdn", "ab", 16),
)
_PACKED = {
    "ssm": (("z", 512), ("dt", 128), ("pad", 128), ("xbc", 768)),
    "gla": (("q", 256), ("k", 256), ("v", 512), ("r", 512), ("glr", 128)),
    "rwkv": (("all", 1920),),
    "gdn": (("qkv", 1536), ("z", 512), ("ab", 128)),
}


def _packed_columns():
    start, s = {}, 0
    for mixer, blk, w in _REF_BLOCKS:
        start[(mixer, blk)] = (s, w)
        s += w
    out = {}
    for mixer, blocks in _PACKED.items():
        cols = []
        for blk, wp in blocks:
            s0, w = start.get((mixer, blk), (0, 0))
            cols += list(range(s0, s0 + w)) + [-1] * (wp - w)
        out[mixer] = np.asarray(cols, np.int32)
    return out


_SRC_COLS = _packed_columns()


def _cparams(*sem):
    return pltpu.CompilerParams(dimension_semantics=sem, vmem_limit_bytes=VMEM_LIMIT)


def _dot(a, b):
    return jnp.dot(a.astype(BF16), b.astype(BF16), preferred_element_type=F32)


def _dot_nt(a, b):
    return lax.dot_general(a.astype(BF16), b.astype(BF16), (((1,), (1,)), ((), ())),
                           preferred_element_type=F32)


def _dot_tn(a, b):
    return lax.dot_general(a.astype(BF16), b.astype(BF16), (((0,), (0,)), ((), ())),
                           preferred_element_type=F32)


def _dot_hi(a, b):
    return jnp.dot(a, b, precision=HI, preferred_element_type=F32)


def _dot_x3(a, b):
    ah = a.astype(BF16)
    al = (a - ah.astype(F32)).astype(BF16)
    bh = b.astype(BF16)
    bl = (b - bh.astype(F32)).astype(BF16)
    f = lambda u, v: jnp.dot(u, v, preferred_element_type=F32)
    return f(ah, bh) + (f(ah, bl) + f(al, bh))


def _dot_x2(a, w):
    ah = a.astype(BF16)
    al = (a - ah.astype(F32)).astype(BF16)
    return jnp.dot(ah, w, preferred_element_type=F32) + jnp.dot(al, w, preferred_element_type=F32)


def _softplus(x):
    return jnp.maximum(x, 0.0) + jnp.log(1.0 + jnp.exp(-jnp.abs(x)))


def _sigmoid(x):
    return 1.0 / (1.0 + jnp.exp(-x))


def _silu(x):
    return x * _sigmoid(x)


def _pick_tile(m, pref):
    t = pref
    while m % t:
        t //= 2
    return t


def _group_sums(y, n):
    m = min(n, LANES)
    row = lax.broadcasted_iota(jnp.int32, (LANES, LANES), 0)
    col = lax.broadcasted_iota(jnp.int32, (LANES, LANES), 1)
    same = ((row // m) == (col // m)).astype(BF16)
    parts = [_dot_x2(y[:, j:j + LANES], same) for j in range(0, y.shape[1], LANES)]
    k = n // m
    if k > 1:
        parts = [sum(parts[g * k:(g + 1) * k]) for g in range(len(parts) // k) for _ in range(k)]
    return jnp.concatenate(parts, axis=1)


def _mm_kernel(a_ref, w_ref, o_ref, *, act, precise):
    if precise:
        r = _dot_hi(a_ref[...].astype(F32), w_ref[...].astype(F32))
    else:
        r = _dot(a_ref[...], w_ref[...])
    if act == "sigmoid":
        r = _sigmoid(r)
    o_ref[...] = r.astype(o_ref.dtype)


def pmatmul(a, w, *, tm, tn, act=None, precise=False, out_dtype=F32):
    m, k = a.shape
    tm = _pick_tile(m, tm)
    if w.ndim == 3:
        per = w.shape[2] // tn
        n = w.shape[0] * w.shape[2]
        w_spec = pl.BlockSpec((None, k, tn), lambda j, i: (j // per, 0, j % per))
    else:
        n = w.shape[1]
        w_spec = pl.BlockSpec((k, tn), lambda j, i: (0, j))
    assert tm % SUBLANES == 0 and n % tn == 0 and w.shape[-1] % tn == 0, (m, tm, w.shape, tn)
    return pl.pallas_call(
        functools.partial(_mm_kernel, act=act, precise=precise),
        grid=(n // tn, m // tm),
        in_specs=[pl.BlockSpec((tm, k), lambda j, i: (i, 0)), w_spec],
        out_specs=pl.BlockSpec((tm, tn), lambda j, i: (i, j)),
        out_shape=jax.ShapeDtypeStruct((m, n), out_dtype),
        compiler_params=_cparams("parallel", "parallel"),
    )(a, w)


def _norm_mod_kernel(x_ref, w_ref, shift_ref, scale_ref, o_ref):
    x = x_ref[0]
    y = x * lax.rsqrt(jnp.mean(x * x, axis=-1, keepdims=True) + EPS) * w_ref[...]
    o_ref[0] = (y * (1.0 + scale_ref[0, 0]) + shift_ref[0, 0]).astype(o_ref.dtype)


def _mod_sel(n_lat_tiles):
    return lambda bi, i, *_: (bi, jnp.where(i < n_lat_tiles, 1, 0), 0, 0)


def norm_modulate(x_all, w, shift, scale, *, n_ctx, out_dtype=BF16):
    b, t, d = x_all.shape
    tm = _pick_tile(n_ctx, ROW_TILE)
    assert t % tm == 0
    tok = lambda bi, i: (bi, i, 0)
    return pl.pallas_call(
        _norm_mod_kernel,
        grid=(b, t // tm),
        in_specs=[pl.BlockSpec((1, tm, d), tok),
                  pl.BlockSpec((1, d), lambda bi, i: (0, 0)),
                  pl.BlockSpec((1, 1, 1, d), _mod_sel((t - n_ctx) // tm)),
                  pl.BlockSpec((1, 1, 1, d), _mod_sel((t - n_ctx) // tm))],
        out_specs=pl.BlockSpec((1, tm, d), tok),
        out_shape=jax.ShapeDtypeStruct((b, t, d), out_dtype),
        compiler_params=_cparams("parallel", "parallel"),
    )(x_all, w.reshape(1, d), shift, scale)


def _row(v):
    return v.reshape(1, -1).astype(F32)


def _const_spec(shape):
    return pl.BlockSpec(shape, lambda *_: (0,) * len(shape))


def _tile_specs(tt, width, col):
    r8 = tt // SUBLANES
    main = pl.BlockSpec((1, tt, width), lambda bi, i: (bi, i, col))
    prev = pl.BlockSpec((1, SUBLANES, width), lambda bi, i: (bi, jnp.maximum(i * r8 - 1, 0), col))
    return main, prev, r8


def _halo_specs(tt, width, col, t):
    main, prev, r8 = _tile_specs(tt, width, col)
    last8 = t // SUBLANES - 1
    nxt = pl.BlockSpec((1, SUBLANES, width), lambda bi, i: (bi, jnp.minimum((i + 1) * r8, last8), col))
    return [main, prev, nxt]


def _neighbours(x, prev8, next8, *, nct, nt):
    i = pl.program_id(1)
    tt = x.shape[0]
    row = lax.broadcasted_iota(jnp.int32, x.shape, 0)
    first = (i == 0) | (i == nct)
    last = (i == nct - 1) | (i == nt - 1)
    pr = jnp.where(first, 0.0, prev8[SUBLANES - 1:SUBLANES, :])
    nx = jnp.where(last, 0.0, next8[0:1, :])
    xp = jnp.where(row == 0, pr, pltpu.roll(x, 1, 0))
    xn = jnp.where(row == tt - 1, nx, pltpu.roll(x, tt - 1, 0))
    return xp, xn


def _prep_call(kernel, ins, in_specs, out_widths, *, b, t, tt, out_dtype=F32):
    tok = lambda bi, i: (bi, i, 0)
    return pl.pallas_call(
        kernel,
        grid=(b, t // tt),
        in_specs=in_specs,
        out_specs=[pl.BlockSpec((1, tt, w), tok) for w in out_widths],
        out_shape=[jax.ShapeDtypeStruct((b, t, w), out_dtype) for w in out_widths],
        compiler_params=_cparams("parallel", "parallel"),
    )(*ins)


def _chunk_masks(reverse):
    row = lax.broadcasted_iota(jnp.int32, (CHUNK, CHUNK), 0)
    col = lax.broadcasted_iota(jnp.int32, (CHUNK, CHUNK), 1)
    if reverse:
        return col >= row, col > row
    return col <= row, col < row


def _chunk_order(i, n_ctx_chunks, n_chunks, reverse):
    n_lat_chunks = n_chunks - n_ctx_chunks
    if not reverse:
        return jnp.where(i < n_ctx_chunks, n_lat_chunks + i, i - n_ctx_chunks)
    return jnp.where(i < n_ctx_chunks, n_chunks - 1 - i, n_lat_chunks - 1 - (i - n_ctx_chunks))


def _split3(a):
    hi = a.astype(BF16)
    r = a - hi.astype(F32)
    mid = r.astype(BF16)
    return hi, mid, (r - mid.astype(F32)).astype(BF16)


def _transpose_small(x):
    row = lax.broadcasted_iota(jnp.int32, (LANES, LANES), 0)
    col = lax.broadcasted_iota(jnp.int32, (LANES, LANES), 1)
    eye = (row == col).astype(BF16)
    nt = lambda p: lax.dot_general(eye, p, (((1,), (1,)), ((), ())), preferred_element_type=F32)
    hi, mid, lo = _split3(x)
    return nt(hi) + (nt(mid) + nt(lo))


def _chunk_cumsum(incl, x):
    m = incl.astype(BF16)
    hi, mid, lo = _split3(x)
    f = lambda p: jnp.dot(m, p, preferred_element_type=F32)
    return f(hi) + (f(mid) + f(lo))


def _select_columns(x, sel):
    c = x.shape[0]
    y = jnp.dot(jnp.concatenate(_split3(x), axis=0), sel, preferred_element_type=F32)
    return y[:c] + (y[c:2 * c] + y[2 * c:])


def _unit_tri_solve(mats, rhs, precise_levels=-1, explicit=False):
    n = range(len(mats))
    if explicit:
        row = lax.broadcasted_iota(jnp.int32, (CHUNK, CHUNK), 0)
        col = lax.broadcasted_iota(jnp.int32, (CHUNK, CHUNK), 1)
        x = [(row == col).astype(F32) - mats[h] for h in n]
    else:
        first = _dot_x3 if precise_levels >= 0 else _dot
        x = [rhs[h] - first(mats[h], rhs[h]) for h in n]
    yield
    p = mats
    for level in range(int(np.log2(CHUNK)) - 1):
        dot = _dot_x3 if level < precise_levels else _dot
        p = [dot(p[h], p[h]) for h in n]
        yield
        x = [x[h] + (dot(x[h], p[h]) if explicit else dot(p[h], x[h])) for h in n]
        yield
    if explicit:
        x = [_dot_x3(x[h], rhs[h]) for h in n]
        yield
    return x


def _bidir_scan(body, tok_ins, const_ins, state_shape, *, b, t, n_ctx, lockstep=True, batch_block=1):
    nc, ncc = t // CHUNK, n_ctx // CHUNK
    nb = batch_block
    assert b % nb == 0

    def chunk_spec(width, col, reverse):
        return pl.BlockSpec((nb, CHUNK, width), lambda bi, i: (bi, _chunk_order(i, ncc, nc, reverse), col))

    def direction(reverse):
        d = int(reverse)
        specs = [chunk_spec(w, cols[d], reverse) for _, w, *cols in tok_ins]
        specs += [_const_spec(pair[d].shape) for pair in const_ins]
        return specs, [a for a, *_ in tok_ins] + [pair[d] for pair in const_ins]

    (spec_f, arg_f), (spec_b, arg_b) = direction(False), direction(True)
    n_tok, n_in = len(tok_ins), len(arg_f)

    def kern(*refs):
        o_f, o_b, s_f, s_b = refs[2 * n_in:]

        @pl.when(pl.program_id(1) == 0)
        def _():
            s_f[...] = jnp.zeros_like(s_f)
            s_b[...] = jnp.zeros_like(s_b)

        def one(j, ins, o_ref, s_ref, reverse):
            ins = [r.at[pl.ds(j, 1)] if k < n_tok else r for k, r in enumerate(ins)]
            return body(*ins, o_ref.at[pl.ds(j, 1)], s_ref.at[j], reverse=reverse)

        gens = []
        for j in range(nb):
            gens += [one(j, refs[:n_in], o_f, s_f, False), one(j, refs[n_in:2 * n_in], o_b, s_b, True)]
        if not lockstep:
            gens = [itertools.chain(*gens)]
        while gens:
            gens = [g for g in gens if next(g, _DONE) is not _DONE]

    return pl.pallas_call(
        kern,
        grid=(b // nb, nc),
        in_specs=spec_f + spec_b,
        out_specs=[chunk_spec(BRANCH, 0, False), chunk_spec(BRANCH, 0, True)],
        out_shape=[jax.ShapeDtypeStruct((b, t, BRANCH), F32)] * 2,
        scratch_shapes=[pltpu.VMEM((nb,) + tuple(state_shape), F32)] * 2,
        compiler_params=_cparams("parallel", "arbitrary"),
    )(*arg_f, *arg_b)


_DONE = object()


def _batch_block(b, pref):
    return pref if b % pref == 0 else 1


def _ssm_prep_kernel(x_ref, xp_ref, xn_ref, dt_ref, cw_ref, cb_ref, dtb_ref, xs_ref, bc_ref, sm_ref,
                     *, nct, nt):
    x = x_ref[0]
    xp, xn = _neighbours(x, xp_ref[0], xn_ref[0], nct=nct, nt=nt)
    y = _silu(xp * cw_ref[0:1, :] + x * cw_ref[1:2, :] + xn * cw_ref[2:3, :] + cb_ref[...])
    xs_ref[0] = y[:, :BRANCH]
    bc_ref[0] = y[:, BRANCH:]
    sm_ref[0] = _softplus(dt_ref[0] + dtb_ref[...])


def ssm_prep(p, lp, *, n_ctx):
    b, t, _ = p.shape
    tt = _pick_tile(n_ctx, ROW_TILE)
    dtb = jnp.pad(lp["ssm_dt_bias"].reshape(1, -1), ((0, 0), (0, LANES - 2 * SSM_HEADS)))
    specs = _halo_specs(tt, 768, 1, t) + [pl.BlockSpec((1, tt, LANES), lambda bi, i: (bi, i, 4)),
                                          _const_spec((3, 768)), _const_spec((1, 768)), _const_spec((1, LANES))]
    kern = functools.partial(_ssm_prep_kernel, nct=(t - n_ctx) // tt, nt=t // tt)
    return _prep_call(kern, (p, p, p, p, lp["ssm_conv_w"], _row(lp["ssm_conv_b"]), dtb), specs,
                      (BRANCH, 2 * SSM_GROUPS * SSM_N, LANES), b=b, t=t, tt=tt)


def _ssd_body(x_ref, bc_ref, sm_ref, na_ref, o_ref, s_ref, *, reverse):
    incl, _ = _chunk_masks(reverse)
    last = 0 if reverse else CHUNK - 1
    off = SSM_HEADS if reverse else 0
    dt_all = sm_ref[0]
    g_all = _chunk_cumsum(incl, dt_all * na_ref[...])
    yield
    expand = _expand_matrix(off, SSM_HEADS, SSM_P)
    gx = _select_columns(g_all, expand)
    dx = _select_columns(dt_all, expand)
    gt_all = _transpose_small(g_all)
    dtt_all = _transpose_small(dt_all)
    yield
    assert SSM_P == CHUNK and 2 * SSM_P == LANES
    groups, pairs = range(SSM_GROUPS), range(SSM_HEADS // 2)
    rep = SSM_HEADS // SSM_GROUPS
    gw = SSM_GROUPS * SSM_N
    gs = [slice(grp * rep * SSM_P, (grp + 1) * rep * SSM_P) for grp in groups]
    ps = [slice(p * LANES, (p + 1) * LANES) for p in pairs]
    glx = gx[last:last + 1, :]
    egx = jnp.exp(gx)
    wx = dx * jnp.exp(glx - gx)
    eglx = jnp.exp(glx)
    x = x_ref[0]
    bm = [bc_ref[0, :, grp * SSM_N:(grp + 1) * SSM_N] for grp in groups]
    cm = [bc_ref[0, :, gw + grp * SSM_N:gw + (grp + 1) * SSM_N] for grp in groups]
    cb = [_dot_nt(cm[grp], bm[grp]) for grp in groups]
    cb2 = [jnp.concatenate([cb[grp], cb[grp]], axis=1) for grp in groups]
    s = [s_ref[grp] for grp in groups]
    row2 = lax.broadcasted_iota(jnp.int32, (CHUNK, LANES), 0)
    lane2 = lax.broadcasted_iota(jnp.int32, (CHUNK, LANES), 1)
    col2 = lane2 % CHUNK
    incl2 = (col2 >= row2) if reverse else (col2 <= row2)
    pair_row = lambda t, p: jnp.concatenate([t[off + 2 * p:off + 2 * p + 1, :], t[off + 2 * p + 1:off + 2 * p + 2, :]],
                                            axis=1)
    yield
    scores = [cb2[2 * p // rep] * jnp.exp(jnp.where(incl2, gx[:, ps[p]] - pair_row(gt_all, p), -jnp.inf))
              * pair_row(dtt_all, p) for p in pairs]
    x_bd = [jnp.concatenate([jnp.where(lane2 < SSM_P, x[:, ps[p]], 0.0), jnp.where(lane2 >= SSM_P, x[:, ps[p]], 0.0)],
                            axis=0) for p in pairs]
    yield
    intra = [_dot(scores[p], x_bd[p]) for p in pairs]
    yield
    inter = [_dot(cm[grp], s[grp]) for grp in groups]
    yield
    upd = [_dot_tn(bm[grp], wx[:, gs[grp]] * x[:, gs[grp]]) for grp in groups]
    yield
    for p in pairs:
        grp = 2 * p // rep
        ls = slice((2 * p % rep) * SSM_P, (2 * p % rep + 2) * SSM_P)
        o_ref[0, :, ps[p]] = intra[p] + egx[:, ps[p]] * inter[grp][:, ls]
    for grp in groups:
        s_ref[grp] = s[grp] * eglx[:, gs[grp]] + upd[grp]


def _expand_matrix(off, n_heads, width):
    row = lax.broadcasted_iota(jnp.int32, (LANES, n_heads * width), 0)
    col = lax.broadcasted_iota(jnp.int32, (LANES, n_heads * width), 1)
    lo = row * width - off * width
    return ((col >= lo) & (col < lo + width)).astype(BF16)


def ssd_scan(xs, bc, sm, neg_a, *, n_ctx):
    b, t, _ = xs.shape
    toks = [(xs, BRANCH, 0, 0), (bc, 2 * SSM_GROUPS * SSM_N, 0, 0), (sm, LANES, 0, 0)]
    state = (SSM_GROUPS, SSM_N, (SSM_HEADS // SSM_GROUPS) * SSM_P)
    return _bidir_scan(_ssd_body, toks, [(neg_a, neg_a)], state, b=b, t=t, n_ctx=n_ctx,
                       batch_block=_batch_block(b, 4))


def _gla_body(q_ref, k_ref, v_ref, glr_ref, w2_ref, gb_ref, o_ref, s_ref, *, reverse):
    incl, _ = _chunk_masks(reverse)
    last = 0 if reverse else CHUNK - 1
    logit = _dot_x3(glr_ref[0], w2_ref[...]) + gb_ref[...]
    yield
    la = -_softplus(-logit) * (1.0 / GLA_TAU)
    g_all = _chunk_cumsum(incl, la)
    yield
    heads = range(GLA_HEADS)
    ks = [slice(h * GLA_DK, (h + 1) * GLA_DK) for h in heads]
    vs = [slice(h * GLA_DV, (h + 1) * GLA_DV) for h in heads]
    g = [g_all[:, ks[h]] for h in heads]
    gl = [g[h][last:last + 1, :] for h in heads]
    k = [k_ref[0, :, ks[h]] for h in heads]
    v = [v_ref[0, :, vs[h]] for h in heads]
    qg = [q_ref[0, :, ks[h]] * (GLA_DK ** -0.5) * jnp.exp(g[h]) for h in heads]
    st = [s_ref[h] for h in heads]
    yield
    scores = [jnp.where(incl, _dot_nt(qg[h], k[h] * jnp.exp(-g[h])), 0.0) for h in heads]
    yield
    intra = [_dot(scores[h], v[h]) for h in heads]
    yield
    inter = [_dot_nt(qg[h], st[h]) for h in heads]
    yield
    upd = [_dot_tn(v[h], k[h] * jnp.exp(gl[h] - g[h])) for h in heads]
    yield
    for h in heads:
        o_ref[0, :, vs[h]] = intra[h] + inter[h]
        s_ref[h] = st[h] * jnp.exp(gl[h]) + upd[h]


def gla_scan(p, w2_pair, gb_pair, *, n_ctx):
    b, t, _ = p.shape
    kwid = GLA_HEADS * GLA_DK
    toks = [(p, kwid, 0, 0), (p, kwid, 1, 1), (p, BRANCH, 1, 1), (p, LANES, 12, 12)]
    return _bidir_scan(_gla_body, toks, [w2_pair, gb_pair], (GLA_HEADS, GLA_DV, GLA_DK), b=b, t=t, n_ctx=n_ctx,
                       batch_block=_batch_block(b, 4))


def _rwkv_prep_kernel(x_ref, xp_ref, xn_ref, mu_ref, w2_ref, w0_ref, a2_ref, a0_ref, g2_ref, kk_ref_w,
                      ka_ref, rk_ref, r_ref, k_ref, v_ref, kk_ref, a_ref, lw_ref, g_ref, bo_ref,
                      *, nct, nt):
    x = x_ref[0]
    xp, xn = _neighbours(x, xp_ref[0], xn_ref[0], nct=nct, nt=nt)
    x = x + mu_ref[...] * (0.5 * (xp + xn) - x)
    r, k, v = x[:, :BRANCH], x[:, BRANCH:2 * BRANCH], x[:, 2 * BRANCH:3 * BRANCH]
    wlr = x[:, 3 * BRANCH:3 * BRANCH + LANES]
    alr = x[:, 3 * BRANCH + LANES:3 * BRANCH + 2 * LANES]
    glr = x[:, 3 * BRANCH + 2 * LANES:]
    w_raw = _dot_x3(jnp.tanh(wlr), w2_ref[...]) + w0_ref[...]
    lw_ref[0] = _sigmoid(w_raw) * (-float(np.exp(-0.5)))
    a = _sigmoid(_dot_x3(alr, a2_ref[...]) + a0_ref[...])
    a_ref[0] = a
    g_ref[0] = _dot_x3(_sigmoid(glr), g2_ref[...])
    kk = k * kk_ref_w[...]
    kk_ref[0] = kk * lax.rsqrt(_group_sums(kk * kk, RWKV_N) + EPS)
    ksum = k * (2.0 + (a[:, :BRANCH] + a[:, BRANCH:] - 2.0) * ka_ref[...])
    bo_ref[0] = _group_sums(r * ksum * rk_ref[...], RWKV_N) * v
    r_ref[0] = r
    k_ref[0] = k
    v_ref[0] = v


def rwkv_prep(p, lp, *, n_ctx):
    b, t, w = p.shape
    tt = _pick_tile(n_ctx, ROW_TILE)

    def pair(wp):
        r, c = wp.shape[1:]
        return jnp.zeros((LANES, 2 * c), F32).at[:r, :c].set(wp[0]).at[r:2 * r, c:].set(wp[1])

    consts = (_row(lp["rwkv_mu"]), pair(lp["rwkv_w2"]), _row(lp["rwkv_w0"]), pair(lp["rwkv_a2"]),
              _row(lp["rwkv_a0"]), lp["rwkv_g2"], _row(lp["rwkv_k_k"]), _row(lp["rwkv_k_a"]),
              _row(lp["rwkv_r_k"]))
    specs = _halo_specs(tt, w, 0, t) + [_const_spec(c.shape) for c in consts]
    kern = functools.partial(_rwkv_prep_kernel, nct=(t - n_ctx) // tt, nt=t // tt)
    return _prep_call(kern, (p, p, p) + consts, specs,
                      (BRANCH, BRANCH, BRANCH, BRANCH, 2 * BRANCH, 2 * BRANCH, BRANCH, BRANCH),
                      b=b, t=t, tt=tt)


def _rwkv_body(r_ref, k_ref, v_ref, kk_ref, a_ref, lw_ref, ka_ref, o_ref, s_ref, *, reverse):
    incl, strict = _chunk_masks(reverse)
    last = 0 if reverse else CHUNK - 1
    lw_all = lw_ref[0]
    g_all = _chunk_cumsum(incl, lw_all)
    a_all = a_ref[0]
    k_all = k_ref[0] * (1.0 + (a_all - 1.0) * ka_ref[...])
    yield
    heads = range(RWKV_HEADS)
    hs = [slice(h * RWKV_N, (h + 1) * RWKV_N) for h in heads]
    g = [g_all[:, hs[h]] for h in heads]
    gl = [g[h][last:last + 1, :] for h in heads]
    eneg = [jnp.exp(-g[h]) for h in heads]
    edec = [jnp.exp(gl[h] - g[h]) for h in heads]
    kk = [kk_ref[0, :, hs[h]] for h in heads]
    bvec = [kk[h] * a_all[:, hs[h]] for h in heads]
    k = [k_all[:, hs[h]] for h in heads]
    v = [v_ref[0, :, hs[h]] for h in heads]
    kkg = [kk[h] * jnp.exp(g[h] - lw_all[:, hs[h]]) for h in heads]
    rg = [r_ref[0, :, hs[h]] * jnp.exp(g[h]) for h in heads]
    bh = [bvec[h] * eneg[h] for h in heads]
    kh = [k[h] * eneg[h] for h in heads]
    s = [s_ref[h] for h in heads]
    yield
    both = [jnp.concatenate([kkg[h], rg[h]], axis=0) for h in heads]
    mask2 = jnp.concatenate([strict, incl], axis=0)
    mb = [jnp.where(mask2, _dot_nt(both[h], bh[h]), 0.0) for h in heads]
    yield
    mk = [jnp.where(mask2, _dot_nt(both[h], kh[h]), 0.0) for h in heads]
    yield
    part = [_dot(mk[h], v[h]) + _dot_nt(both[h], s[h]) for h in heads]
    yield
    x = yield from _unit_tri_solve([mb[h][:CHUNK] for h in heads], [part[h][:CHUNK] for h in heads])
    u = [-xh for xh in x]
    for h in heads:
        o_ref[0, :, hs[h]] = part[h][CHUNK:] + _dot(mb[h][CHUNK:], u[h])
    yield
    for h in heads:
        upd = _dot_tn(jnp.concatenate([u[h], v[h]], axis=0),
                      jnp.concatenate([bvec[h] * edec[h], k[h] * edec[h]], axis=0))
        s_ref[h] = s[h] * jnp.exp(gl[h]) + upd


def rwkv_scan(r, k, v, kk, a, lw, k_a, *, n_ctx):
    b, t, _ = r.shape
    toks = [(r, BRANCH, 0, 0), (k, BRANCH, 0, 0), (v, BRANCH, 0, 0), (kk, BRANCH, 0, 0),
            (a, BRANCH, 0, 1), (lw, BRANCH, 0, 1)]
    return _bidir_scan(_rwkv_body, toks, [(k_a, k_a)], (RWKV_HEADS, RWKV_N, RWKV_N), b=b, t=t, n_ctx=n_ctx,
                       batch_block=_batch_block(b, 2))


def _gdn_prep_kernel(x_ref, xp_ref, xn_ref, ab_ref, cw_ref, na_ref, dtb_ref,
                     q_ref, k_ref, v_ref, sm_ref, *, nct, nt):
    x = x_ref[0]
    xp, xn = _neighbours(x, xp_ref[0], xn_ref[0], nct=nct, nt=nt)
    y = _silu(xp * cw_ref[0:1, :] + x * cw_ref[1:2, :] + xn * cw_ref[2:3, :])
    q, k = y[:, :BRANCH], y[:, BRANCH:2 * BRANCH]
    q_ref[0] = q * lax.rsqrt(_group_sums(q * q, GDN_N) + EPS) * (GDN_N ** -0.5)
    k_ref[0] = k * lax.rsqrt(_group_sums(k * k, GDN_N) + EPS)
    v_ref[0] = y[:, 2 * BRANCH:]
    ab = ab_ref[0]
    lane = lax.broadcasted_iota(jnp.int32, ab.shape, 1)
    sm_ref[0] = jnp.where(lane < 2 * GDN_HEADS, na_ref[...] * _softplus(ab + dtb_ref[...]), _sigmoid(ab))


def gdn_prep(p, lp, *, n_ctx):
    b, t, _ = p.shape
    tt = _pick_tile(n_ctx, ROW_TILE)
    padrow = lambda v: jnp.pad(v.reshape(1, -1), ((0, 0), (0, LANES - 2 * GDN_HEADS)))
    consts = (lp["gdn_conv_w"], padrow(-jnp.exp(lp["gdn_a_log"])), padrow(lp["gdn_dt_bias"]))
    specs = (_halo_specs(tt, 3 * BRANCH, 0, t) + [pl.BlockSpec((1, tt, LANES), lambda bi, i: (bi, i, 16))]
             + [_const_spec(c.shape) for c in consts])
    kern = functools.partial(_gdn_prep_kernel, nct=(t - n_ctx) // tt, nt=t // tt)
    return _prep_call(kern, (p, p, p, p) + consts, specs, (BRANCH, BRANCH, BRANCH, LANES), b=b, t=t, tt=tt)


def _gdn_body(q_ref, k_ref, v_ref, sm_ref, o_ref, s_ref, *, reverse):
    incl, strict = _chunk_masks(reverse)
    last = 0 if reverse else CHUNK - 1
    off = GDN_HEADS if reverse else 0
    sm = sm_ref[0]
    g_all = _chunk_cumsum(incl, sm)
    yield
    gt_all = _transpose_small(g_all)
    yield
    heads = range(GDN_HEADS)
    hs = [slice(h * GDN_N, (h + 1) * GDN_N) for h in heads]
    g = [g_all[:, off + h:off + h + 1] for h in heads]
    gl = [g[h][last:last + 1, :] for h in heads]
    beta = [sm[:, 2 * GDN_HEADS + off + h:2 * GDN_HEADS + off + h + 1] for h in heads]
    q = [q_ref[0, :, hs[h]] for h in heads]
    k = [k_ref[0, :, hs[h]] for h in heads]
    v = [v_ref[0, :, hs[h]] for h in heads]
    s = [s_ref[h] for h in heads]
    decay = [jnp.exp(jnp.where(incl, g[h] - gt_all[off + h:off + h + 1, :], -jnp.inf)) for h in heads]
    yield
    kq = [_dot_nt(jnp.concatenate([k[h], q[h]], axis=0), k[h]) for h in heads]
    yield
    lower = [jnp.where(strict, kq[h][:CHUNK] * decay[h] * beta[h], 0.0) for h in heads]
    attn = [kq[h][CHUNK:] * decay[h] for h in heads]
    o_part = [_dot(q[h] * jnp.exp(g[h]), s[h]) for h in heads]
    yield
    rhs = [jnp.concatenate([v[h] * beta[h], k[h] * (beta[h] * jnp.exp(g[h]))], axis=1) for h in heads]
    sol = yield from _unit_tri_solve(lower, rhs, precise_levels=2, explicit=True)
    v_new = [sol[h][:, :GDN_N] - _dot(sol[h][:, GDN_N:], s[h]) for h in heads]
    yield
    for h in heads:
        o_ref[0, :, hs[h]] = o_part[h] + _dot(attn[h], v_new[h])
    yield
    for h in heads:
        s_ref[h] = s[h] * jnp.exp(gl[h]) + _dot_tn(k[h] * jnp.exp(gl[h] - g[h]), v_new[h])


def gdn_scan(q, k, v, sm, *, n_ctx):
    b, t, _ = q.shape
    toks = [(q, BRANCH, 0, 0), (k, BRANCH, 0, 0), (v, BRANCH, 0, 0), (sm, LANES, 0, 0)]
    return _bidir_scan(_gdn_body, toks, [], (GDN_HEADS, GDN_N, GDN_N), b=b, t=t, n_ctx=n_ctx,
                       batch_block=_batch_block(b, 4))


def _merge_kernel(sf_ref, sb_ref, sx_ref, sz_ref, gf_ref, gb_ref, gr_ref, rf_ref, rb_ref, rg_ref, rbo_ref,
                  df_ref, db_ref, dz_ref, gate_ref, x_ref, m_ref,
                  sd_ref, sn_ref, gn_ref, lnw_ref, lnb_ref, dn_ref,
                  wb_ref, wo_ref, o_ref):
    def group_rms(y, n, w_ref):
        return y * lax.rsqrt(_group_sums(y * y, n) * (1.0 / n) + EPS) * w_ref[...]

    y = (sf_ref[0] + sb_ref[0] + sd_ref[...] * sx_ref[0]) * _silu(sz_ref[0])
    ys = group_rms(y, BRANCH // SSM_GROUPS, sn_ref)
    yg = group_rms(gf_ref[0] + gb_ref[0], GLA_DV, gn_ref) * _silu(gr_ref[0])
    y = rf_ref[0] + rb_ref[0]
    yc = y - _group_sums(y, RWKV_N) * (1.0 / RWKV_N)
    var = _group_sums(yc * yc, RWKV_N) * (1.0 / RWKV_N)
    yr = (yc * lax.rsqrt(var + RWKV_LN_EPS) * lnw_ref[...] + lnb_ref[...] + rbo_ref[0]) * rg_ref[0]
    yd = group_rms(df_ref[0] + db_ref[0], GDN_N, dn_ref) * _silu(dz_ref[0])
    acc = None
    for i, yi in enumerate((ys, yg, yr, yd)):
        term = gate_ref[0, :, i * D_MODEL:(i + 1) * D_MODEL].astype(F32) * _dot(yi, wb_ref[i])
        acc = term if acc is None else acc + term
    o_ref[0] = x_ref[0] + m_ref[0, 0] * _dot(acc, wo_ref[...])


def merge_residual(ssm, gla, rwkv, gdn, gates, x_all, gate_mod, lp, w_branch, w_out, *, n_ctx):
    b, t, d = x_all.shape
    tm = _pick_tile(n_ctx, ROW_TILE)
    tok = lambda bi, i: (bi, i, 0)
    blk = lambda c: pl.BlockSpec((1, tm, BRANCH), lambda bi, i: (bi, i, c))
    half = blk(0)
    consts = (_row(jnp.repeat(lp["ssm_d"], SSM_P)), _row(lp["ssm_norm"]),
              _row(jnp.tile(lp["gla_norm"], GLA_HEADS)), _row(lp["rwkv_ln_w"]), _row(lp["rwkv_ln_b"]),
              _row(jnp.tile(lp["gdn_norm"], GDN_HEADS)), w_branch, w_out)
    ins = (ssm[0], ssm[1], ssm[2], ssm[3], gla[0], gla[1], gla[2], rwkv[0], rwkv[1], rwkv[2], rwkv[3],
           gdn[0], gdn[1], gdn[2], gates, x_all, gate_mod) + consts
    specs = ([half, half, half, blk(0), half, half, blk(2), half, half, half, half, half, half, blk(3),
              pl.BlockSpec((1, tm, 4 * d), tok), pl.BlockSpec((1, tm, d), tok),
              pl.BlockSpec((1, 1, 1, d), _mod_sel((t - n_ctx) // tm))]
             + [_const_spec(c.shape) for c in consts])
    return pl.pallas_call(
        _merge_kernel,
        grid=(b, t // tm),
        in_specs=specs,
        out_specs=pl.BlockSpec((1, tm, d), tok),
        out_shape=jax.ShapeDtypeStruct((b, t, d), F32),
        compiler_params=_cparams("parallel", "parallel"),
    )(*ins)


def _route_kernel(x_ref, nw_ref, shift_ref, scale_ref, rw_ref, rb_ref, u_ref, o_ref, cnt_ref, hb_ref, *, rows_kw):
    x = x_ref[...]
    shift, scale = _token_rows([shift_ref, scale_ref], pl.program_id(0), x.shape[0], **rows_kw)
    h = x * lax.rsqrt(jnp.mean(x * x, axis=-1, keepdims=True) + EPS) * nw_ref[...] * (1.0 + scale) + shift
    hb_ref[...] = h.astype(BF16)
    logits = lax.dot_general(rw_ref[...], h, (((1,), (1,)), ((), ())),
                             precision=HI, preferred_element_type=F32)
    scores = _sigmoid(logits)
    sel = scores + rb_ref[...]
    rows = [sel[e:e + 1, :] for e in range(N_EXPERTS)]
    sc = [scores[e:e + 1, :] for e in range(N_EXPERTS)]

    def top2(vals):
        v1, i1 = vals[0], jnp.zeros(vals[0].shape, jnp.int32)
        for j in range(1, len(vals)):
            better = vals[j] > v1
            v1 = jnp.where(better, vals[j], v1)
            i1 = jnp.where(better, j, i1)
        v2 = jnp.where(i1 == 0, vals[1], vals[0])
        i2 = jnp.where(i1 == 0, 1, 0)
        for j in range(1, len(vals)):
            better = (vals[j] > v2) & (i1 != j)
            v2 = jnp.where(better, vals[j], v2)
            i2 = jnp.where(better, j, i2)
        return v1, i1, v2, i2

    gsum = []
    for grp in range(N_GROUPS):
        v1, _, v2, _ = top2(rows[grp * EXPERTS_PER_GROUP:(grp + 1) * EXPERTS_PER_GROUP])
        gsum.append(v1 + v2)
    best, gidx = gsum[0], jnp.zeros(gsum[0].shape, jnp.int32)
    for grp in range(1, N_GROUPS):
        better = gsum[grp] > best
        best = jnp.where(better, gsum[grp], best)
        gidx = jnp.where(better, grp, gidx)
    chosen, chosen_sc = [], []
    for j in range(EXPERTS_PER_GROUP):
        cj, sj = rows[j], sc[j]
        for grp in range(1, N_GROUPS):
            cj = jnp.where(gidx == grp, rows[grp * EXPERTS_PER_GROUP + j], cj)
            sj = jnp.where(gidx == grp, sc[grp * EXPERTS_PER_GROUP + j], sj)
        chosen.append(cj)
        chosen_sc.append(sj)
    _, i1, _, i2 = top2(chosen)
    w1, w2 = jnp.zeros_like(best), jnp.zeros_like(best)
    for j in range(EXPERTS_PER_GROUP):
        w1 = jnp.where(i1 == j, chosen_sc[j], w1)
        w2 = jnp.where(i2 == j, chosen_sc[j], w2)
    tot = w1 + w2
    w1, w2 = w1 / tot, w2 / tot
    tm = scores.shape[1]
    sub = lax.broadcasted_iota(jnp.int32, (SUBLANES, tm), 0)
    ind8 = jnp.zeros((SUBLANES, tm), F32)
    meta = jnp.zeros((SUBLANES, tm), F32)
    for j in range(EXPERTS_PER_GROUP):
        gate_j = jnp.where(i1 == j, w1, 0.0) + jnp.where(i2 == j, w2, 0.0)
        meta = jnp.where(sub == j, gate_j, meta)
    for grp in range(N_GROUPS):
        ind8 = jnp.where((sub == grp) & (gidx == grp), 1.0, ind8)
    before = jnp.dot(ind8.astype(BF16), u_ref[...], preferred_element_type=F32)
    rank = jnp.sum(ind8 * before, axis=0, keepdims=True)
    meta = jnp.where(sub == _META_GROUP, gidx.astype(F32), meta)
    meta = jnp.where(sub == _META_RANK, rank, meta)
    o_ref[...] = meta
    counts = jnp.sum(ind8, axis=1, keepdims=True)
    lane = lax.broadcasted_iota(jnp.int32, (SUBLANES, LANES), 1)
    row = lax.broadcasted_iota(jnp.int32, (SUBLANES, LANES), 0)
    cnt_ref[0] = jnp.broadcast_to(jnp.sum(jnp.where(lane == row, counts, 0.0), axis=0, keepdims=True),
                                  (SUBLANES, LANES)).astype(jnp.int32)


_META_GROUP, _META_RANK = EXPERTS_PER_GROUP, EXPERTS_PER_GROUP + 1


def moe_route(x_all, norm_w, shift_rows, scale_rows, router_w, router_b, *, n_ctx, tm):
    b, t, d = x_all.shape
    m = b * t
    upper = jnp.asarray(np.triu(np.ones((tm, tm), np.float32), 1), BF16)
    return pl.pallas_call(
        functools.partial(_route_kernel, rows_kw=dict(bsz=b, t_all=t, n_ctx=n_ctx)),
        grid=(m // tm,),
        in_specs=[pl.BlockSpec((tm, d), lambda i: (i, 0)),
                  _const_spec((1, d)), _const_spec(shift_rows.shape), _const_spec(scale_rows.shape),
                  pl.BlockSpec((N_EXPERTS, d), lambda i: (0, 0)),
                  pl.BlockSpec((N_EXPERTS, 1), lambda i: (0, 0)),
                  _const_spec((tm, tm))],
        out_specs=[pl.BlockSpec((SUBLANES, tm), lambda i: (0, i)),
                   pl.BlockSpec((1, SUBLANES, LANES), lambda i: (i, 0, 0)),
                   pl.BlockSpec((tm, d), lambda i: (i, 0))],
        out_shape=[jax.ShapeDtypeStruct((SUBLANES, m), F32),
                   jax.ShapeDtypeStruct((m // tm, SUBLANES, LANES), jnp.int32),
                   jax.ShapeDtypeStruct((m, d), BF16)],
        compiler_params=_cparams("parallel"),
    )(x_all.reshape(m, d), norm_w.reshape(1, d), shift_rows, scale_rows,
      router_w.T, router_b.reshape(N_EXPERTS, 1), upper)


MOE_TILE = 1024
MOE_SUB_ROWS = 256
MOE_TAIL_ROWS = 128


def _expert_kernel(cnt_ref, h_ref, mr_ref, mc_ref, wg_ref, wu_ref, wd_ref, x_ref, gate_ref, o_ref, *, rows_kw):
    i, grp = pl.program_id(0), pl.program_id(1)

    @pl.when(grp == 0)
    def _():
        o_ref[...] = jnp.zeros_like(o_ref)

    tm = h_ref.shape[0]
    count = cnt_ref[i * N_GROUPS + grp]
    grp_f = grp.astype(F32)
    sel_row = jnp.where(mr_ref[_META_GROUP:_META_GROUP + 1, :] == grp_f, mr_ref[_META_RANK:_META_RANK + 1, :], -1.0)
    sel_col = jnp.where(mc_ref[:, _META_GROUP:_META_GROUP + 1] == grp_f, mc_ref[:, _META_RANK:_META_RANK + 1], -1.0)
    gate_parts = _split3(mc_ref[...])

    def sub_block(first, rows):
        base = first.astype(F32)
        slot_r = lax.broadcasted_iota(jnp.int32, (rows, tm), 0).astype(F32)
        slot_c = lax.broadcasted_iota(jnp.int32, (tm, rows), 1).astype(F32)
        pick = (sel_row - base == slot_r).astype(BF16)
        put = (sel_col - base == slot_c).astype(BF16)
        xg = jnp.dot(pick, h_ref[...], preferred_element_type=F32).astype(BF16)
        gates = sum(jnp.dot(pick, p, preferred_element_type=F32) for p in gate_parts)
        y = jnp.zeros((rows, o_ref.shape[1]), F32)
        for e in range(EXPERTS_PER_GROUP):
            hid = _silu(_dot(xg, wg_ref[e])) * _dot(xg, wu_ref[e])
            y = y + _dot(gates[:, e:e + 1] * hid, wd_ref[e])
        o_ref[...] += jnp.dot(put, y.astype(BF16), preferred_element_type=F32)

    n_full = count // MOE_SUB_ROWS
    rem = count - n_full * MOE_SUB_ROWS
    n_main = n_full + (rem > MOE_TAIL_ROWS).astype(jnp.int32)

    def main_block(s, carry):
        sub_block(s * MOE_SUB_ROWS, MOE_SUB_ROWS)
        return carry

    lax.fori_loop(0, n_main, main_block, 0)

    @pl.when((rem > 0) & (rem <= MOE_TAIL_ROWS))
    def _():
        sub_block(n_full * MOE_SUB_ROWS, MOE_TAIL_ROWS)

    @pl.when(grp == N_GROUPS - 1)
    def _():
        (gate,) = _token_rows([gate_ref], i, tm, **rows_kw)
        o_ref[...] = x_ref[...] + gate * o_ref[...]


def moe_experts(hb, meta, counts, wg, wu, wd, x_all, gate_rows, *, n_ctx, tm):
    b, t, d = x_all.shape
    m = b * t
    tok = lambda i, g, cnt: (i, 0)
    grid_spec = pltpu.PrefetchScalarGridSpec(
        num_scalar_prefetch=1,
        grid=(m // tm, N_GROUPS),
        in_specs=[pl.BlockSpec((tm, d), tok),
                  pl.BlockSpec((SUBLANES, tm), lambda i, g, cnt: (0, i)),
                  pl.BlockSpec((tm, SUBLANES), tok),
                  pl.BlockSpec((EXPERTS_PER_GROUP, d, EXPERT_FF), lambda i, g, cnt: (g, 0, 0)),
                  pl.BlockSpec((EXPERTS_PER_GROUP, d, EXPERT_FF), lambda i, g, cnt: (g, 0, 0)),
                  pl.BlockSpec((EXPERTS_PER_GROUP, EXPERT_FF, d), lambda i, g, cnt: (g, 0, 0)),
                  pl.BlockSpec((tm, d), tok),
                  pl.BlockSpec(gate_rows.shape, lambda i, g, cnt: (0, 0))],
        out_specs=pl.BlockSpec((tm, d), tok))
    out = pl.pallas_call(
        functools.partial(_expert_kernel, rows_kw=dict(bsz=b, t_all=t, n_ctx=n_ctx)),
        grid_spec=grid_spec,
        out_shape=jax.ShapeDtypeStruct((m, d), F32),
        compiler_params=pltpu.CompilerParams(dimension_semantics=("parallel", "arbitrary"),
                                             vmem_limit_bytes=MOE_VMEM_LIMIT),
    )(counts[:, 0, :N_GROUPS].reshape(-1), hb, meta, meta.T, wg, wu, wd, x_all.reshape(m, d), gate_rows)
    return out.reshape(b, t, d)


def _token_rows(m_refs, tile, tm, *, bsz, t_all, n_ctx):
    row = tile * tm + lax.broadcasted_iota(jnp.int32, (tm, 1), 0)
    ctx = jnp.zeros((tm, 1), jnp.bool_)
    lat = []
    for bi in range(bsz):
        lo, split = bi * t_all, bi * t_all + t_all - n_ctx
        lat.append((row >= lo) & (row < split))
        ctx = ctx | ((row >= split) & (row < lo + t_all))
    out = []
    for m_ref in m_refs:
        v = jnp.where(ctx, m_ref[bsz:bsz + 1, :], 0.0)
        for bi in range(bsz):
            v = v + jnp.where(lat[bi], m_ref[bi:bi + 1, :], 0.0)
        out.append(v)
    return out


GRID_TILE_COLS = SUBLANES
GRID_TILE = GRID_W * GRID_TILE_COLS


def _grid_view(x_all, n_lat):
    b, t, d = x_all.shape
    rows = n_lat // GRID_W
    assert rows == GRID_W and t % GRID_W == 0
    last = GRID_W // GRID_TILE_COLS - 1
    spec = pl.BlockSpec((1, rows, GRID_TILE_COLS, d), lambda bi, i: (bi, 0, jnp.minimum(i, last), 0))
    return x_all.reshape(b, t // GRID_W, GRID_W, d), spec


def _grid_tile(xg_ref):
    return jnp.concatenate([xg_ref[0, :, j, :] for j in range(xg_ref.shape[2])], axis=0)


def _transpose_grid_kernel(xn_ref, xg_ref, o_ref, *, n_lat_tiles):
    i = pl.program_id(1)

    @pl.when(i < n_lat_tiles)
    def _():
        o_ref[0] = _grid_tile(xg_ref)

    @pl.when(i >= n_lat_tiles)
    def _():
        o_ref[0] = xn_ref[0]


def transpose_grid(x_all, *, n_ctx):
    b, t, d = x_all.shape
    xg, gspec = _grid_view(x_all, t - n_ctx)
    tok = pl.BlockSpec((1, GRID_TILE, d), lambda bi, i: (bi, i, 0))
    return pl.pallas_call(
        functools.partial(_transpose_grid_kernel, n_lat_tiles=(t - n_ctx) // GRID_TILE),
        grid=(b, pl.cdiv(t, GRID_TILE)),
        in_specs=[tok, gspec],
        out_specs=tok,
        out_shape=jax.ShapeDtypeStruct((b, t, d), x_all.dtype),
        compiler_params=_cparams("parallel", "parallel"),
    )(x_all, xg)


def _final_norm_kernel(x_ref, w_ref, o_ref, *, from_grid):
    x = _grid_tile(x_ref) if from_grid else x_ref[0]
    o_ref[0] = x * lax.rsqrt(jnp.mean(x * x, axis=-1, keepdims=True) + EPS) * w_ref[...]


def final_rms_norm(x_all, w, *, n_ctx, from_grid):
    b, t, d = x_all.shape
    n_lat = t - n_ctx
    tt = GRID_TILE
    tok = pl.BlockSpec((1, tt, d), lambda bi, i: (bi, i, 0))
    if from_grid:
        x_in, spec = _grid_view(x_all, n_lat)
    else:
        x_in, spec = x_all, tok
    return pl.pallas_call(
        functools.partial(_final_norm_kernel, from_grid=from_grid),
        grid=(b, n_lat // tt),
        in_specs=[spec, _const_spec((1, d))],
        out_specs=tok,
        out_shape=jax.ShapeDtypeStruct((b, n_lat, d), F32),
        compiler_params=_cparams("parallel", "parallel"),
    )(x_in, w.reshape(1, d))


def _pack_w_in(w_in, mixer):
    cols = _SRC_COLS[mixer]
    pieces, i = [], 0
    while i < len(cols):
        j = i
        if cols[i] < 0:
            while j < len(cols) and cols[j] < 0:
                j += 1
            pieces.append(jnp.zeros((w_in.shape[0], j - i), w_in.dtype))
        else:
            while j < len(cols) and cols[j] == cols[i] + (j - i):
                j += 1
            pieces.append(w_in[:, int(cols[i]):int(cols[i]) + (j - i)])
        i = j
    return jnp.concatenate(pieces, axis=1).astype(BF16)


def mixer_scans(ps, lp, *, n_ctx):
    p_ssm, p_gla, p_rwkv, p_gdn = ps

    xs, bc, sm = ssm_prep(p_ssm, lp, n_ctx=n_ctx)
    neg_a = jnp.pad(-jnp.exp(lp["ssm_a_log"]).reshape(1, -1), ((0, 0), (0, LANES - 2 * SSM_HEADS)))
    ssm = tuple(ssd_scan(xs, bc, sm, neg_a, n_ctx=n_ctx)) + (xs, p_ssm)

    w2 = [jnp.zeros((LANES, GLA_HEADS * GLA_DK), F32).at[d * GLA_RANK:(d + 1) * GLA_RANK].set(lp["gla_w2"][d])
          for d in range(2)]
    gb = [_row(lp["gla_b"][d]) for d in range(2)]
    gla = tuple(gla_scan(p_gla, w2, gb, n_ctx=n_ctx)) + (p_gla,)

    r, k, v, kk, a, lw, g, bonus = rwkv_prep(p_rwkv, lp, n_ctx=n_ctx)
    rwkv = tuple(rwkv_scan(r, k, v, kk, a, lw, _row(lp["rwkv_k_a"]), n_ctx=n_ctx)) + (g, bonus)

    q, kd, vd, smd = gdn_prep(p_gdn, lp, n_ctx=n_ctx)
    gdn = tuple(gdn_scan(q, kd, vd, smd, n_ctx=n_ctx)) + (p_gdn,)
    return ssm, gla, rwkv, gdn


def kernel(x, c, ctx, c_ctx, ada_w, ada_b, norm_mix, norm_ffn, w_in, w_gate, w_branch, w_out, ssm_conv_w, ssm_conv_b, ssm_a_log, ssm_dt_bias, ssm_d, ssm_norm, gla_w2, gla_b, gla_norm, rwkv_mu, rwkv_w0, rwkv_w2, rwkv_a0, rwkv_a2, rwkv_g2, rwkv_k_k, rwkv_k_a, rwkv_r_k, rwkv_ln_w, rwkv_ln_b, gdn_conv_w, gdn_a_log, gdn_dt_bias, gdn_norm, router_w, router_b, moe_w_gate, moe_w_up, moe_w_down, final_norm):
    bsz, seq, d = x.shape
    n_ctx = ctx.shape[1]
    t_all = n_ctx + seq
    m_all = bsz * t_all

    cond = jnp.concatenate([jax.nn.silu(c), jax.nn.silu(c_ctx)[None]], 0)
    cond = jnp.pad(cond, ((0, SUBLANES - cond.shape[0]), (0, 0)))
    mods, mod_rows = [], []
    for l in range(DEPTH):
        mod = pmatmul(cond, ada_w[l], tm=SUBLANES, tn=1024, precise=True) + ada_b[l]
        mod_rows.append(mod)
        lat = mod[:bsz].reshape(bsz, 6, d)
        cx = jnp.broadcast_to(mod[bsz].reshape(1, 6, d), (bsz, 6, d))
        mods.append(jnp.stack([cx, lat], axis=1))

    x_all = jnp.concatenate([x, ctx], axis=1)
    scan_order = False
    for l in range(DEPTH):
        if (l % 2 == 1) != scan_order:
            x_all = transpose_grid(x_all, n_ctx=n_ctx)
            scan_order = not scan_order
        lp = dict(ssm_conv_w=ssm_conv_w[l], ssm_conv_b=ssm_conv_b[l], ssm_a_log=ssm_a_log[l],
                  ssm_dt_bias=ssm_dt_bias[l], ssm_d=ssm_d[l], ssm_norm=ssm_norm[l],
                  gla_w2=gla_w2[l], gla_b=gla_b[l], gla_norm=gla_norm[l],
                  rwkv_mu=rwkv_mu[l], rwkv_w0=rwkv_w0[l], rwkv_w2=rwkv_w2[l], rwkv_a0=rwkv_a0[l],
                  rwkv_a2=rwkv_a2[l], rwkv_g2=rwkv_g2[l], rwkv_k_k=rwkv_k_k[l], rwkv_k_a=rwkv_k_a[l],
                  rwkv_r_k=rwkv_r_k[l], rwkv_ln_w=rwkv_ln_w[l], rwkv_ln_b=rwkv_ln_b[l],
                  gdn_conv_w=gdn_conv_w[l], gdn_a_log=gdn_a_log[l], gdn_dt_bias=gdn_dt_bias[l],
                  gdn_norm=gdn_norm[l])
        mod = mods[l]
        msel = lambda i: mod[:, :, i][:, :, None, :]

        h = norm_modulate(x_all, norm_mix[l], msel(0), msel(1), n_ctx=n_ctx)
        h2d = h.reshape(m_all, d)
        ps = []
        for mixer in ("ssm", "gla", "rwkv", "gdn"):
            wp = _pack_w_in(w_in[l], mixer)
            ps.append(pmatmul(h2d, wp, tm=1024, tn=wp.shape[1]).reshape(bsz, t_all, wp.shape[1]))
        gates = pmatmul(h2d, w_gate[l].astype(BF16), tm=1024, tn=1024, act="sigmoid", out_dtype=BF16)
        gates = gates.reshape(bsz, t_all, 4 * d)

        ssm, gla, rwkv, gdn = mixer_scans(ps, lp, n_ctx=n_ctx)
        x_all = merge_residual(ssm, gla, rwkv, gdn, gates, x_all, msel(2), lp,
                               w_branch[l].astype(BF16), w_out[l].astype(BF16), n_ctx=n_ctx)

        tm_moe = _pick_tile(m_all, MOE_TILE)
        rows = lambda i: mod_rows[l][:, i * d:(i + 1) * d]
        meta, counts, hb = moe_route(x_all, norm_ffn[l], rows(3), rows(4), router_w, router_b,
                                     n_ctx=n_ctx, tm=tm_moe)
        x_all = moe_experts(hb, meta, counts, moe_w_gate[l].astype(BF16), moe_w_up[l].astype(BF16),
                            moe_w_down[l].astype(BF16), x_all, rows(5), n_ctx=n_ctx, tm=tm_moe)

    return final_rms_norm(x_all, final_norm, n_ctx=n_ctx, from_grid=scan_order)
```

```python
import functools
import itertools

import numpy as np
import jax
import jax.numpy as jnp
from jax import lax
from jax.experimental import pallas as pl
from jax.experimental.pallas import tpu as pltpu

F32 = jnp.float32
BF16 = jnp.bfloat16
HI = lax.Precision.HIGHEST

D_MODEL = 1024
DEPTH = 2
GRID_W = 64
CHUNK = 64
EPS = 1e-6
BRANCH = D_MODEL // 2
SSM_HEADS, SSM_P, SSM_GROUPS, SSM_N = 8, 64, 2, 64
GLA_HEADS, GLA_DK, GLA_DV, GLA_RANK, GLA_TAU = 4, 64, 128, 16, 16.0
RWKV_HEADS, RWKV_N, RWKV_LN_EPS = 8, 64, 64e-5
GDN_HEADS, GDN_N = 4, 128
N_EXPERTS, N_GROUPS, EXPERTS_PER_GROUP = 16, 4, 4
EXPERT_FF = D_MODEL // 2
LANES = 128
SUBLANES = 8
VMEM_LIMIT = 48 * 1024 * 1024
MOE_VMEM_LIMIT = 56 * 1024 * 1024
ROW_TILE = 256

_REF_BLOCKS = (
    ("ssm", "z", 512), ("ssm", "xbc", 768), ("ssm", "dt", 16),
    ("gla", "q", 256), ("gla", "k", 256), ("gla", "v", 512), ("gla", "r", 512), ("gla", "glr", 32),
    ("rwkv", "all", 1920),
    ("gdn", "qkv", 1536), ("gdn", "z", 512), ("gdn", "ab", 16),
)
_PACKED = {
    "ssm": (("z", 512), ("dt", 128), ("pad", 128), ("xbc", 768)),
    "gla": (("q", 256), ("k", 256), ("v", 512), ("r", 512), ("glr", 128)),
    "rwkv": (("all", 1920),),
    "gdn": (("qkv", 1536), ("z", 512), ("ab", 128)),
}


def _packed_columns():
    start, s = {}, 0
    for mixer, blk, w in _REF_BLOCKS:
        start[(mixer, blk)] = (s, w)
        s += w
    out = {}
    for mixer, blocks in _PACKED.items():
        cols = []
        for blk, wp in blocks:
            s0, w = start.get((mixer, blk), (0, 0))
            cols += list(range(s0, s0 + w)) + [-1] * (wp - w)
        out[mixer] = np.asarray(cols, np.int32)
    return out


_SRC_COLS = _packed_columns()


def _cparams(*sem):
    return pltpu.CompilerParams(dimension_semantics=sem, vmem_limit_bytes=VMEM_LIMIT)


def _dot(a, b):
    return jnp.dot(a.astype(BF16), b.astype(BF16), preferred_element_type=F32)


def _dot_nt(a, b):
    return lax.dot_general(a.astype(BF16), b.astype(BF16), (((1,), (1,)), ((), ())),
                           preferred_element_type=F32)


def _dot_tn(a, b):
    return lax.dot_general(a.astype(BF16), b.astype(BF16), (((0,), (0,)), ((), ())),
                           preferred_element_type=F32)


def _dot_hi(a, b):
    return jnp.dot(a, b, precision=HI, preferred_element_type=F32)


def _dot_x3(a, b):
    ah = a.astype(BF16)
    al = (a - ah.astype(F32)).astype(BF16)
    bh = b.astype(BF16)
    bl = (b - bh.astype(F32)).astype(BF16)
    f = lambda u, v: jnp.dot(u, v, preferred_element_type=F32)
    return f(ah, bh) + (f(ah, bl) + f(al, bh))


def _dot_x2(a, w):
    ah = a.astype(BF16)
    al = (a - ah.astype(F32)).astype(BF16)
    return jnp.dot(ah, w, preferred_element_type=F32) + jnp.dot(al, w, preferred_element_type=F32)


def _softplus(x):
    return jnp.maximum(x, 0.0) + jnp.log(1.0 + jnp.exp(-jnp.abs(x)))


def _sigmoid(x):
    return 1.0 / (1.0 + jnp.exp(-x))


def _silu(x):
    return x * _sigmoid(x)


def _pick_tile(m, pref):
    t = pref
    while m % t:
        t //= 2
    return t


def _group_sums(y, n):
    m = min(n, LANES)
    row = lax.broadcasted_iota(jnp.int32, (LANES, LANES), 0)
    col = lax.broadcasted_iota(jnp.int32, (LANES, LANES), 1)
    same = ((row // m) == (col // m)).astype(BF16)
    parts = [_dot_x2(y[:, j:j + LANES], same) for j in range(0, y.shape[1], LANES)]
    k = n // m
    if k > 1:
        parts = [sum(parts[g * k:(g + 1) * k]) for g in range(len(parts) // k) for _ in range(k)]
    return jnp.concatenate(parts, axis=1)


def _mm_kernel(a_ref, w_ref, o_ref, *, act, precise):
    if precise:
        r = _dot_hi(a_ref[...].astype(F32), w_ref[...].astype(F32))
    else:
        r = _dot(a_ref[...], w_ref[...])
    if act == "sigmoid":
        r = _sigmoid(r)
    o_ref[...] = r.astype(o_ref.dtype)


def pmatmul(a, w, *, tm, tn, act=None, precise=False, out_dtype=F32):
    m, k = a.shape
    tm = _pick_tile(m, tm)
    if w.ndim == 3:
        per = w.shape[2] // tn
        n = w.shape[0] * w.shape[2]
        w_spec = pl.BlockSpec((None, k, tn), lambda j, i: (j // per, 0, j % per))
    else:
        n = w.shape[1]
        w_spec = pl.BlockSpec((k, tn), lambda j, i: (0, j))
    assert tm % SUBLANES == 0 and n % tn == 0 and w.shape[-1] % tn == 0, (m, tm, w.shape, tn)
    return pl.pallas_call(
        functools.partial(_mm_kernel, act=act, precise=precise),
        grid=(n // tn, m // tm),
        in_specs=[pl.BlockSpec((tm, k), lambda j, i: (i, 0)), w_spec],
        out_specs=pl.BlockSpec((tm, tn), lambda j, i: (i, j)),
        out_shape=jax.ShapeDtypeStruct((m, n), out_dtype),
        compiler_params=_cparams("parallel", "parallel"),
    )(a, w)


def _norm_mod_kernel(x_ref, w_ref, shift_ref, scale_ref, o_ref):
    x = x_ref[0]
    y = x * lax.rsqrt(jnp.mean(x * x, axis=-1, keepdims=True) + EPS) * w_ref[...]
    o_ref[0] = (y * (1.0 + scale_ref[0, 0]) + shift_ref[0, 0]).astype(o_ref.dtype)


def _mod_sel(n_lat_tiles):
    return lambda bi, i, *_: (bi, jnp.where(i < n_lat_tiles, 1, 0), 0, 0)


def norm_modulate(x_all, w, shift, scale, *, n_ctx, out_dtype=BF16):
    b, t, d = x_all.shape
    tm = _pick_tile(n_ctx, ROW_TILE)
    assert t % tm == 0
    tok = lambda bi, i: (bi, i, 0)
    return pl.pallas_call(
        _norm_mod_kernel,
        grid=(b, t // tm),
        in_specs=[pl.BlockSpec((1, tm, d), tok),
                  pl.BlockSpec((1, d), lambda bi, i: (0, 0)),
                  pl.BlockSpec((1, 1, 1, d), _mod_sel((t - n_ctx) // tm)),
                  pl.BlockSpec((1, 1, 1, d), _mod_sel((t - n_ctx) // tm))],
        out_specs=pl.BlockSpec((1, tm, d), tok),
        out_shape=jax.ShapeDtypeStruct((b, t, d), out_dtype),
        compiler_params=_cparams("parallel", "parallel"),
    )(x_all, w.reshape(1, d), shift, scale)


def _row(v):
    return v.reshape(1, -1).astype(F32)


def _const_spec(shape):
    return pl.BlockSpec(shape, lambda *_: (0,) * len(shape))


def _tile_specs(tt, width, col):
    r8 = tt // SUBLANES
    main = pl.BlockSpec((1, tt, width), lambda bi, i: (bi, i, col))
    prev = pl.BlockSpec((1, SUBLANES, width), lambda bi, i: (bi, jnp.maximum(i * r8 - 1, 0), col))
    return main, prev, r8


def _halo_specs(tt, width, col, t):
    main, prev, r8 = _tile_specs(tt, width, col)
    last8 = t // SUBLANES - 1
    nxt = pl.BlockSpec((1, SUBLANES, width), lambda bi, i: (bi, jnp.minimum((i + 1) * r8, last8), col))
    return [main, prev, nxt]


def _neighbours(x, prev8, next8, *, nct, nt):
    i = pl.program_id(1)
    tt = x.shape[0]
    row = lax.broadcasted_iota(jnp.int32, x.shape, 0)
    first = (i == 0) | (i == nct)
    last = (i == nct - 1) | (i == nt - 1)
    pr = jnp.where(first, 0.0, prev8[SUBLANES - 1:SUBLANES, :])
    nx = jnp.where(last, 0.0, next8[0:1, :])
    xp = jnp.where(row == 0, pr, pltpu.roll(x, 1, 0))
    xn = jnp.where(row == tt - 1, nx, pltpu.roll(x, tt - 1, 0))
    return xp, xn


def _prep_call(kernel, ins, in_specs, out_widths, *, b, t, tt, out_dtype=F32):
    tok = lambda bi, i: (bi, i, 0)
    return pl.pallas_call(
        kernel,
        grid=(b, t // tt),
        in_specs=in_specs,
        out_specs=[pl.BlockSpec((1, tt, w), tok) for w in out_widths],
        out_shape=[jax.ShapeDtypeStruct((b, t, w), out_dtype) for w in out_widths],
        compiler_params=_cparams("parallel", "parallel"),
    )(*ins)


def _chunk_masks(reverse):
    row = lax.broadcasted_iota(jnp.int32, (CHUNK, CHUNK), 0)
    col = lax.broadcasted_iota(jnp.int32, (CHUNK, CHUNK), 1)
    if reverse:
        return col >= row, col > row
    return col <= row, col < row


def _chunk_order(i, n_ctx_chunks, n_chunks, reverse):
    n_lat_chunks = n_chunks - n_ctx_chunks
    if not reverse:
        return jnp.where(i < n_ctx_chunks, n_lat_chunks + i, i - n_ctx_chunks)
    return jnp.where(i < n_ctx_chunks, n_chunks - 1 - i, n_lat_chunks - 1 - (i - n_ctx_chunks))


def _split3(a):
    hi = a.astype(BF16)
    r = a - hi.astype(F32)
    mid = r.astype(BF16)
    return hi, mid, (r - mid.astype(F32)).astype(BF16)


def _transpose_small(x):
    row = lax.broadcasted_iota(jnp.int32, (LANES, LANES), 0)
    col = lax.broadcasted_iota(jnp.int32, (LANES, LANES), 1)
    eye = (row == col).astype(BF16)
    nt = lambda p: lax.dot_general(eye, p, (((1,), (1,)), ((), ())), preferred_element_type=F32)
    hi, mid, lo = _split3(x)
    return nt(hi) + (nt(mid) + nt(lo))


def _chunk_cumsum(incl, x):
    m = incl.astype(BF16)
    hi, mid, lo = _split3(x)
    f = lambda p: jnp.dot(m, p, preferred_element_type=F32)
    return f(hi) + (f(mid) + f(lo))


def _select_columns(x, sel):
    c = x.shape[0]
    y = jnp.dot(jnp.concatenate(_split3(x), axis=0), sel, preferred_element_type=F32)
    return y[:c] + (y[c:2 * c] + y[2 * c:])


def _unit_tri_solve(mats, rhs, precise_levels=-1, explicit=False):
    n = range(len(mats))
    if explicit:
        row = lax.broadcasted_iota(jnp.int32, (CHUNK, CHUNK), 0)
        col = lax.broadcasted_iota(jnp.int32, (CHUNK, CHUNK), 1)
        x = [(row == col).astype(F32) - mats[h] for h in n]
    else:
        first = _dot_x3 if precise_levels >= 0 else _dot
        x = [rhs[h] - first(mats[h], rhs[h]) for h in n]
    yield
    p = mats
    for level in range(int(np.log2(CHUNK)) - 1):
        dot = _dot_x3 if level < precise_levels else _dot
        p = [dot(p[h], p[h]) for h in n]
        yield
        x = [x[h] + (dot(x[h], p[h]) if explicit else dot(p[h], x[h])) for h in n]
        yield
    if explicit:
        x = [_dot_x3(x[h], rhs[h]) for h in n]
        yield
    return x


def _bidir_scan(body, tok_ins, const_ins, state_shape, *, b, t, n_ctx, lockstep=True, batch_block=1):
    nc, ncc = t // CHUNK, n_ctx // CHUNK
    nb = batch_block
    assert b % nb == 0

    def chunk_spec(width, col, reverse):
        return pl.BlockSpec((nb, CHUNK, width), lambda bi, i: (bi, _chunk_order(i, ncc, nc, reverse), col))

    def direction(reverse):
        d = int(reverse)
        specs = [chunk_spec(w, cols[d], reverse) for _, w, *cols in tok_ins]
        specs += [_const_spec(pair[d].shape) for pair in const_ins]
        return specs, [a for a, *_ in tok_ins] + [pair[d] for pair in const_ins]

    (spec_f, arg_f), (spec_b, arg_b) = direction(False), direction(True)
    n_tok, n_in = len(tok_ins), len(arg_f)

    def kern(*refs):
        o_f, o_b, s_f, s_b = refs[2 * n_in:]

        @pl.when(pl.program_id(1) == 0)
        def _():
            s_f[...] = jnp.zeros_like(s_f)
            s_b[...] = jnp.zeros_like(s_b)

        def one(j, ins, o_ref, s_ref, reverse):
            ins = [r.at[pl.ds(j, 1)] if k < n_tok else r for k, r in enumerate(ins)]
            return body(*ins, o_ref.at[pl.ds(j, 1)], s_ref.at[j], reverse=reverse)

        gens = []
        for j in range(nb):
            gens += [one(j, refs[:n_in], o_f, s_f, False), one(j, refs[n_in:2 * n_in], o_b, s_b, True)]
        if not lockstep:
            gens = [itertools.chain(*gens)]
        while gens:
            gens = [g for g in gens if next(g, _DONE) is not _DONE]

    return pl.pallas_call(
        kern,
        grid=(b // nb, nc),
        in_specs=spec_f + spec_b,
        out_specs=[chunk_spec(BRANCH, 0, False), chunk_spec(BRANCH, 0, True)],
        out_shape=[jax.ShapeDtypeStruct((b, t, BRANCH), F32)] * 2,
        scratch_shapes=[pltpu.VMEM((nb,) + tuple(state_shape), F32)] * 2,
        compiler_params=_cparams("parallel", "arbitrary"),
    )(*arg_f, *arg_b)


_DONE = object()


def _batch_block(b, pref):
    return pref if b % pref == 0 else 1


def _ssm_prep_kernel(x_ref, xp_ref, xn_ref, dt_ref, cw_ref, cb_ref, dtb_ref, xs_ref, bc_ref, sm_ref,
                     *, nct, nt):
    x = x_ref[0]
    xp, xn = _neighbours(x, xp_ref[0], xn_ref[0], nct=nct, nt=nt)
    y = _silu(xp * cw_ref[0:1, :] + x * cw_ref[1:2, :] + xn * cw_ref[2:3, :] + cb_ref[...])
    xs_ref[0] = y[:, :BRANCH]
    bc_ref[0] = y[:, BRANCH:]
    sm_ref[0] = _softplus(dt_ref[0] + dtb_ref[...])


def ssm_prep(p, lp, *, n_ctx):
    b, t, _ = p.shape
    tt = _pick_tile(n_ctx, ROW_TILE)
    dtb = jnp.pad(lp["ssm_dt_bias"].reshape(1, -1), ((0, 0), (0, LANES - 2 * SSM_HEADS)))
    specs = _halo_specs(tt, 768, 1, t) + [pl.BlockSpec((1, tt, LANES), lambda bi, i: (bi, i, 4)),
                                          _const_spec((3, 768)), _const_spec((1, 768)), _const_spec((1, LANES))]
    kern = functools.partial(_ssm_prep_kernel, nct=(t - n_ctx) // tt, nt=t // tt)
    return _prep_call(kern, (p, p, p, p, lp["ssm_conv_w"], _row(lp["ssm_conv_b"]), dtb), specs,
                      (BRANCH, 2 * SSM_GROUPS * SSM_N, LANES), b=b, t=t, tt=tt)


def _ssd_body(x_ref, bc_ref, sm_ref, na_ref, o_ref, s_ref, *, reverse):
    incl, _ = _chunk_masks(reverse)
    last = 0 if reverse else CHUNK - 1
    off = SSM_HEADS if reverse else 0
    dt_all = sm_ref[0]
    g_all = _chunk_cumsum(incl, dt_all * na_ref[...])
    yield
    expand = _expand_matrix(off, SSM_HEADS, SSM_P)
    gx = _select_columns(g_all, expand)
    dx = _select_columns(dt_all, expand)
    gt_all = _transpose_small(g_all)
    dtt_all = _transpose_small(dt_all)
    yield
    assert SSM_P == CHUNK and 2 * SSM_P == LANES
    groups, pairs = range(SSM_GROUPS), range(SSM_HEADS // 2)
    rep = SSM_HEADS // SSM_GROUPS
    gw = SSM_GROUPS * SSM_N
    gs = [slice(grp * rep * SSM_P, (grp + 1) * rep * SSM_P) for grp in groups]
    ps = [slice(p * LANES, (p + 1) * LANES) for p in pairs]
    glx = gx[last:last + 1, :]
    egx = jnp.exp(gx)
    wx = dx * jnp.exp(glx - gx)
    eglx = jnp.exp(glx)
    x = x_ref[0]
    bm = [bc_ref[0, :, grp * SSM_N:(grp + 1) * SSM_N] for grp in groups]
    cm = [bc_ref[0, :, gw + grp * SSM_N:gw + (grp + 1) * SSM_N] for grp in groups]
    cb = [_dot_nt(cm[grp], bm[grp]) for grp in groups]
    cb2 = [jnp.concatenate([cb[grp], cb[grp]], axis=1) for grp in groups]
    s = [s_ref[grp] for grp in groups]
    row2 = lax.broadcasted_iota(jnp.int32, (CHUNK, LANES), 0)
    lane2 = lax.broadcasted_iota(jnp.int32, (CHUNK, LANES), 1)
    col2 = lane2 % CHUNK
    incl2 = (col2 >= row2) if reverse else (col2 <= row2)
    pair_row = lambda t, p: jnp.concatenate([t[off + 2 * p:off + 2 * p + 1, :], t[off + 2 * p + 1:off + 2 * p + 2, :]],
                                            axis=1)
    yield
    scores = [cb2[2 * p // rep] * jnp.exp(jnp.where(incl2, gx[:, ps[p]] - pair_row(gt_all, p), -jnp.inf))
              * pair_row(dtt_all, p) for p in pairs]
    x_bd = [jnp.concatenate([jnp.where(lane2 < SSM_P, x[:, ps[p]], 0.0), jnp.where(lane2 >= SSM_P, x[:, ps[p]], 0.0)],
                            axis=0) for p in pairs]
    yield
    intra = [_dot(scores[p], x_bd[p]) for p in pairs]
    yield
    inter = [_dot(cm[grp], s[grp]) for grp in groups]
    yield
    upd = [_dot_tn(bm[grp], wx[:, gs[grp]] * x[:, gs[grp]]) for grp in groups]
    yield
    for p in pairs:
        grp = 2 * p // rep
        ls = slice((2 * p % rep) * SSM_P, (2 * p % rep + 2) * SSM_P)
        o_ref[0, :, ps[p]] = intra[p] + egx[:, ps[p]] * inter[grp][:, ls]
    for grp in groups:
        s_ref[grp] = s[grp] * eglx[:, gs[grp]] + upd[grp]


def _expand_matrix(off, n_heads, width):
    row = lax.broadcasted_iota(jnp.int32, (LANES, n_heads * width), 0)
    col = lax.broadcasted_iota(jnp.int32, (LANES, n_heads * width), 1)
    lo = row * width - off * width
    return ((col >= lo) & (col < lo + width)).astype(BF16)


def ssd_scan(xs, bc, sm, neg_a, *, n_ctx):
    b, t, _ = xs.shape
    toks = [(xs, BRANCH, 0, 0), (bc, 2 * SSM_GROUPS * SSM_N, 0, 0), (sm, LANES, 0, 0)]
    state = (SSM_GROUPS, SSM_N, (SSM_HEADS // SSM_GROUPS) * SSM_P)
    return _bidir_scan(_ssd_body, toks, [(neg_a, neg_a)], state, b=b, t=t, n_ctx=n_ctx,
                       batch_block=_batch_block(b, 4))


def _gla_body(q_ref, k_ref, v_ref, glr_ref, w2_ref, gb_ref, o_ref, s_ref, *, reverse):
    incl, _ = _chunk_masks(reverse)
    last = 0 if reverse else CHUNK - 1
    logit = _dot_x3(glr_ref[0], w2_ref[...]) + gb_ref[...]
    yield
    la = -_softplus(-logit) * (1.0 / GLA_TAU)
    g_all = _chunk_cumsum(incl, la)
    yield
    heads = range(GLA_HEADS)
    ks = [slice(h * GLA_DK, (h + 1) * GLA_DK) for h in heads]
    vs = [slice(h * GLA_DV, (h + 1) * GLA_DV) for h in heads]
    g = [g_all[:, ks[h]] for h in heads]
    gl = [g[h][last:last + 1, :] for h in heads]
    k = [k_ref[0, :, ks[h]] for h in heads]
    v = [v_ref[0, :, vs[h]] for h in heads]
    qg = [q_ref[0, :, ks[h]] * (GLA_DK ** -0.5) * jnp.exp(g[h]) for h in heads]
    st = [s_ref[h] for h in heads]
    yield
    scores = [jnp.where(incl, _dot_nt(qg[h], k[h] * jnp.exp(-g[h])), 0.0) for h in heads]
    yield
    intra = [_dot(scores[h], v[h]) for h in heads]
    yield
    inter = [_dot_nt(qg[h], st[h]) for h in heads]
    yield
    upd = [_dot_tn(v[h], k[h] * jnp.exp(gl[h] - g[h])) for h in heads]
    yield
    for h in heads:
        o_ref[0, :, vs[h]] = intra[h] + inter[h]
        s_ref[h] = st[h] * jnp.exp(gl[h]) + upd[h]


def gla_scan(p, w2_pair, gb_pair, *, n_ctx):
    b, t, _ = p.shape
    kwid = GLA_HEADS * GLA_DK
    toks = [(p, kwid, 0, 0), (p, kwid, 1, 1), (p, BRANCH, 1, 1), (p, LANES, 12, 12)]
    return _bidir_scan(_gla_body, toks, [w2_pair, gb_pair], (GLA_HEADS, GLA_DV, GLA_DK), b=b, t=t, n_ctx=n_ctx,
                       batch_block=_batch_block(b, 4))


def _rwkv_prep_kernel(x_ref, xp_ref, xn_ref, mu_ref, w2_ref, w0_ref, a2_ref, a0_ref, g2_ref, kk_ref_w,
                      ka_ref, rk_ref, r_ref, k_ref, v_ref, kk_ref, a_ref, lw_ref, g_ref, bo_ref,
                      *, nct, nt):
    x = x_ref[0]
    xp, xn = _neighbours(x, xp_ref[0], xn_ref[0], nct=nct, nt=nt)
    x = x + mu_ref[...] * (0.5 * (xp + xn) - x)
    r, k, v = x[:, :BRANCH], x[:, BRANCH:2 * BRANCH], x[:, 2 * BRANCH:3 * BRANCH]
    wlr = x[:, 3 * BRANCH:3 * BRANCH + LANES]
    alr = x[:, 3 * BRANCH + LANES:3 * BRANCH + 2 * LANES]
    glr = x[:, 3 * BRANCH + 2 * LANES:]
    w_raw = _dot_x3(jnp.tanh(wlr), w2_ref[...]) + w0_ref[...]
    lw_ref[0] = _sigmoid(w_raw) * (-float(np.exp(-0.5)))
    a = _sigmoid(_dot_x3(alr, a2_ref[...]) + a0_ref[...])
    a_ref[0] = a
    g_ref[0] = _dot_x3(_sigmoid(glr), g2_ref[...])
    kk = k * kk_ref_w[...]
    kk_ref[0] = kk * lax.rsqrt(_group_sums(kk * kk, RWKV_N) + EPS)
    ksum = k * (2.0 + (a[:, :BRANCH] + a[:, BRANCH:] - 2.0) * ka_ref[...])
    bo_ref[0] = _group_sums(r * ksum * rk_ref[...], RWKV_N) * v
    r_ref[0] = r
    k_ref[0] = k
    v_ref[0] = v


def rwkv_prep(p, lp, *, n_ctx):
    b, t, w = p.shape
    tt = _pick_tile(n_ctx, ROW_TILE)

    def pair(wp):
        r, c = wp.shape[1:]
        return jnp.zeros((LANES, 2 * c), F32).at[:r, :c].set(wp[0]).at[r:2 * r, c:].set(wp[1])

    consts = (_row(lp["rwkv_mu"]), pair(lp["rwkv_w2"]), _row(lp["rwkv_w0"]), pair(lp["rwkv_a2"]),
              _row(lp["rwkv_a0"]), lp["rwkv_g2"], _row(lp["rwkv_k_k"]), _row(lp["rwkv_k_a"]),
              _row(lp["rwkv_r_k"]))
    specs = _halo_specs(tt, w, 0, t) + [_const_spec(c.shape) for c in consts]
    kern = functools.partial(_rwkv_prep_kernel, nct=(t - n_ctx) // tt, nt=t // tt)
    return _prep_call(kern, (p, p, p) + consts, specs,
                      (BRANCH, BRANCH, BRANCH, BRANCH, 2 * BRANCH, 2 * BRANCH, BRANCH, BRANCH),
                      b=b, t=t, tt=tt)


def _rwkv_body(r_ref, k_ref, v_ref, kk_ref, a_ref, lw_ref, ka_ref, o_ref, s_ref, *, reverse):
    incl, strict = _chunk_masks(reverse)
    last = 0 if reverse else CHUNK - 1
    lw_all = lw_ref[0]
    g_all = _chunk_cumsum(incl, lw_all)
    a_all = a_ref[0]
    k_all = k_ref[0] * (1.0 + (a_all - 1.0) * ka_ref[...])
    yield
    assert RWKV_N == CHUNK and 2 * RWKV_N == LANES
    del strict
    pairs = range(RWKV_HEADS // 2)
    ps = [slice(p * LANES, (p + 1) * LANES) for p in pairs]
    row2 = lax.broadcasted_iota(jnp.int32, (CHUNK, LANES), 0)
    col2 = lax.broadcasted_iota(jnp.int32, (CHUNK, LANES), 1) % CHUNK
    incl2, strict2 = ((col2 >= row2), (col2 > row2)) if reverse else ((col2 <= row2), (col2 < row2))
    mask2 = jnp.concatenate([strict2, incl2], axis=0)
    bd = _block_diag_pair
    g = [g_all[:, ps[p]] for p in pairs]
    gl = [g[p][last:last + 1, :] for p in pairs]
    eneg = [jnp.exp(-g[p]) for p in pairs]
    edec = [jnp.exp(gl[p] - g[p]) for p in pairs]
    kk = [kk_ref[0, :, ps[p]] for p in pairs]
    bvec = [kk[p] * a_all[:, ps[p]] for p in pairs]
    k = [k_all[:, ps[p]] for p in pairs]
    v = [v_ref[0, :, ps[p]] for p in pairs]
    kkg = [kk[p] * jnp.exp(g[p] - lw_all[:, ps[p]]) for p in pairs]
    rg = [r_ref[0, :, ps[p]] * jnp.exp(g[p]) for p in pairs]
    bh = [bvec[p] * eneg[p] for p in pairs]
    kh = [k[p] * eneg[p] for p in pairs]
    s = [s_ref[p] for p in pairs]
    yield
    both = [jnp.concatenate([kkg[p], rg[p]], axis=0) for p in pairs]
    mb = [jnp.where(mask2, _dot_nt(both[p], bd(bh[p])), 0.0) for p in pairs]
    yield
    mk = [jnp.where(mask2, _dot_nt(both[p], bd(kh[p])), 0.0) for p in pairs]
    yield
    part = [_dot(mk[p], bd(v[p])) + _dot_nt(both[p], s[p]) for p in pairs]
    yield
    pw = [mb[p][:CHUNK] for p in pairs]
    x = [part[p][:CHUNK] - _dot(pw[p], bd(part[p][:CHUNK])) for p in pairs]
    yield
    for _ in range(int(np.log2(CHUNK)) - 1):
        pw = [_dot(pw[p], bd(pw[p])) for p in pairs]
        yield
        x = [x[p] + _dot(pw[p], bd(x[p])) for p in pairs]
        yield
    u = [-x[p] for p in pairs]
    for p in pairs:
        o_ref[0, :, ps[p]] = part[p][CHUNK:] + _dot(mb[p][CHUNK:], bd(u[p]))
    yield
    for p in pairs:
        upd = _dot_tn(jnp.concatenate([bd(u[p]), bd(v[p])], axis=0),
                      jnp.concatenate([bd(bvec[p] * edec[p]), bd(k[p] * edec[p])], axis=0))
        s_ref[p] = s[p] * jnp.exp(gl[p]) + upd


def _block_diag_pair(m):
    lane = lax.broadcasted_iota(jnp.int32, m.shape, 1)
    w = m.shape[1] // 2
    return jnp.concatenate([jnp.where(lane < w, m, 0.0), jnp.where(lane >= w, m, 0.0)], axis=0)


def rwkv_scan(r, k, v, kk, a, lw, k_a, *, n_ctx):
    b, t, _ = r.shape
    toks = [(r, BRANCH, 0, 0), (k, BRANCH, 0, 0), (v, BRANCH, 0, 0), (kk, BRANCH, 0, 0),
            (a, BRANCH, 0, 1), (lw, BRANCH, 0, 1)]
    return _bidir_scan(_rwkv_body, toks, [(k_a, k_a)], (RWKV_HEADS // 2, LANES, LANES), b=b, t=t, n_ctx=n_ctx,
                       batch_block=_batch_block(b, 4))


def _gdn_prep_kernel(x_ref, xp_ref, xn_ref, ab_ref, cw_ref, na_ref, dtb_ref,
                     q_ref, k_ref, v_ref, sm_ref, *, nct, nt):
    x = x_ref[0]
    xp, xn = _neighbours(x, xp_ref[0], xn_ref[0], nct=nct, nt=nt)
    y = _silu(xp * cw_ref[0:1, :] + x * cw_ref[1:2, :] + xn * cw_ref[2:3, :])
    q, k = y[:, :BRANCH], y[:, BRANCH:2 * BRANCH]
    q_ref[0] = q * lax.rsqrt(_group_sums(q * q, GDN_N) + EPS) * (GDN_N ** -0.5)
    k_ref[0] = k * lax.rsqrt(_group_sums(k * k, GDN_N) + EPS)
    v_ref[0] = y[:, 2 * BRANCH:]
    ab = ab_ref[0]
    lane = lax.broadcasted_iota(jnp.int32, ab.shape, 1)
    sm_ref[0] = jnp.where(lane < 2 * GDN_HEADS, na_ref[...] * _softplus(ab + dtb_ref[...]), _sigmoid(ab))


def gdn_prep(p, lp, *, n_ctx):
    b, t, _ = p.shape
    tt = _pick_tile(n_ctx, ROW_TILE)
    padrow = lambda v: jnp.pad(v.reshape(1, -1), ((0, 0), (0, LANES - 2 * GDN_HEADS)))
    consts = (lp["gdn_conv_w"], padrow(-jnp.exp(lp["gdn_a_log"])), padrow(lp["gdn_dt_bias"]))
    specs = (_halo_specs(tt, 3 * BRANCH, 0, t) + [pl.BlockSpec((1, tt, LANES), lambda bi, i: (bi, i, 16))]
             + [_const_spec(c.shape) for c in consts])
    kern = functools.partial(_gdn_prep_kernel, nct=(t - n_ctx) // tt, nt=t // tt)
    return _prep_call(kern, (p, p, p, p) + consts, specs, (BRANCH, BRANCH, BRANCH, LANES), b=b, t=t, tt=tt)


def _gdn_body(q_ref, k_ref, v_ref, sm_ref, o_ref, s_ref, *, reverse):
    incl, strict = _chunk_masks(reverse)
    last = 0 if reverse else CHUNK - 1
    off = GDN_HEADS if reverse else 0
    sm = sm_ref[0]
    g_all = _chunk_cumsum(incl, sm)
    yield
    gt_all = _transpose_small(g_all)
    yield
    heads = range(GDN_HEADS)
    hs = [slice(h * GDN_N, (h + 1) * GDN_N) for h in heads]
    g = [g_all[:, off + h:off + h + 1] for h in heads]
    gl = [g[h][last:last + 1, :] for h in heads]
    beta = [sm[:, 2 * GDN_HEADS + off + h:2 * GDN_HEADS + off + h + 1] for h in heads]
    q = [q_ref[0, :, hs[h]] for h in heads]
    k = [k_ref[0, :, hs[h]] for h in heads]
    v = [v_ref[0, :, hs[h]] for h in heads]
    s = [s_ref[h] for h in heads]
    decay = [jnp.exp(jnp.where(incl, g[h] - gt_all[off + h:off + h + 1, :], -jnp.inf)) for h in heads]
    yield
    kq = [_dot_nt(jnp.concatenate([k[h], q[h]], axis=0), k[h]) for h in heads]
    yield
    lower = [jnp.where(strict, kq[h][:CHUNK] * decay[h] * beta[h], 0.0) for h in heads]
    attn = [kq[h][CHUNK:] * decay[h] for h in heads]
    o_part = [_dot(q[h] * jnp.exp(g[h]), s[h]) for h in heads]
    yield
    rhs = [jnp.concatenate([v[h] * beta[h], k[h] * (beta[h] * jnp.exp(g[h]))], axis=1) for h in heads]
    sol = yield from _unit_tri_solve(lower, rhs, precise_levels=2, explicit=True)
    v_new = [sol[h][:, :GDN_N] - _dot(sol[h][:, GDN_N:], s[h]) for h in heads]
    yield
    for h in heads:
        o_ref[0, :, hs[h]] = o_part[h] + _dot(attn[h], v_new[h])
    yield
    for h in heads:
        s_ref[h] = s[h] * jnp.exp(gl[h]) + _dot_tn(k[h] * jnp.exp(gl[h] - g[h]), v_new[h])


def gdn_scan(q, k, v, sm, *, n_ctx):
    b, t, _ = q.shape
    toks = [(q, BRANCH, 0, 0), (k, BRANCH, 0, 0), (v, BRANCH, 0, 0), (sm, LANES, 0, 0)]
    return _bidir_scan(_gdn_body, toks, [], (GDN_HEADS, GDN_N, GDN_N), b=b, t=t, n_ctx=n_ctx,
                       batch_block=_batch_block(b, 4))


def _merge_kernel(sf_ref, sb_ref, sx_ref, sz_ref, gf_ref, gb_ref, gr_ref, rf_ref, rb_ref, rg_ref, rbo_ref,
                  df_ref, db_ref, dz_ref, gate_ref, x_ref, m_ref,
                  sd_ref, sn_ref, gn_ref, lnw_ref, lnb_ref, dn_ref,
                  wb_ref, wo_ref, o_ref):
    def group_rms(y, n, w_ref):
        return y * lax.rsqrt(_group_sums(y * y, n) * (1.0 / n) + EPS) * w_ref[...]

    y = (sf_ref[0] + sb_ref[0] + sd_ref[...] * sx_ref[0]) * _silu(sz_ref[0])
    ys = group_rms(y, BRANCH // SSM_GROUPS, sn_ref)
    yg = group_rms(gf_ref[0] + gb_ref[0], GLA_DV, gn_ref) * _silu(gr_ref[0])
    y = rf_ref[0] + rb_ref[0]
    yc = y - _group_sums(y, RWKV_N) * (1.0 / RWKV_N)
    var = _group_sums(yc * yc, RWKV_N) * (1.0 / RWKV_N)
    yr = (yc * lax.rsqrt(var + RWKV_LN_EPS) * lnw_ref[...] + lnb_ref[...] + rbo_ref[0]) * rg_ref[0]
    yd = group_rms(df_ref[0] + db_ref[0], GDN_N, dn_ref) * _silu(dz_ref[0])
    acc = None
    for i, yi in enumerate((ys, yg, yr, yd)):
        term = gate_ref[0, :, i * D_MODEL:(i + 1) * D_MODEL].astype(F32) * _dot(yi, wb_ref[i])
        acc = term if acc is None else acc + term
    o_ref[0] = x_ref[0] + m_ref[0, 0] * _dot(acc, wo_ref[...])


def merge_residual(ssm, gla, rwkv, gdn, gates, x_all, gate_mod, lp, w_branch, w_out, *, n_ctx):
    b, t, d = x_all.shape
    tm = _pick_tile(n_ctx, ROW_TILE)
    tok = lambda bi, i: (bi, i, 0)
    blk = lambda c: pl.BlockSpec((1, tm, BRANCH), lambda bi, i: (bi, i, c))
    half = blk(0)
    consts = (_row(jnp.repeat(lp["ssm_d"], SSM_P)), _row(lp["ssm_norm"]),
              _row(jnp.tile(lp["gla_norm"], GLA_HEADS)), _row(lp["rwkv_ln_w"]), _row(lp["rwkv_ln_b"]),
              _row(jnp.tile(lp["gdn_norm"], GDN_HEADS)), w_branch, w_out)
    ins = (ssm[0], ssm[1], ssm[2], ssm[3], gla[0], gla[1], gla[2], rwkv[0], rwkv[1], rwkv[2], rwkv[3],
           gdn[0], gdn[1], gdn[2], gates, x_all, gate_mod) + consts
    specs = ([half, half, half, blk(0), half, half, blk(2), half, half, half, half, half, half, blk(3),
              pl.BlockSpec((1, tm, 4 * d), tok), pl.BlockSpec((1, tm, d), tok),
              pl.BlockSpec((1, 1, 1, d), _mod_sel((t - n_ctx) // tm))]
             + [_const_spec(c.shape) for c in consts])
    return pl.pallas_call(
        _merge_kernel,
        grid=(b, t // tm),
        in_specs=specs,
        out_specs=pl.BlockSpec((1, tm, d), tok),
        out_shape=jax.ShapeDtypeStruct((b, t, d), F32),
        compiler_params=_cparams("parallel", "parallel"),
    )(*ins)


def _route_kernel(x_ref, nw_ref, shift_ref, scale_ref, rw_ref, rb_ref, u_ref, o_ref, cnt_ref, hb_ref, *, rows_kw):
    x = x_ref[...]
    shift, scale = _token_rows([shift_ref, scale_ref], pl.program_id(0), x.shape[0], **rows_kw)
    h = x * lax.rsqrt(jnp.mean(x * x, axis=-1, keepdims=True) + EPS) * nw_ref[...] * (1.0 + scale) + shift
    hb_ref[...] = h.astype(BF16)
    logits = lax.dot_general(rw_ref[...], h, (((1,), (1,)), ((), ())),
                             precision=HI, preferred_element_type=F32)
    scores = _sigmoid(logits)
    sel = scores + rb_ref[...]
    rows = [sel[e:e + 1, :] for e in range(N_EXPERTS)]
    sc = [scores[e:e + 1, :] for e in range(N_EXPERTS)]

    def top2(vals):
        v1, i1 = vals[0], jnp.zeros(vals[0].shape, jnp.int32)
        for j in range(1, len(vals)):
            better = vals[j] > v1
            v1 = jnp.where(better, vals[j], v1)
            i1 = jnp.where(better, j, i1)
        v2 = jnp.where(i1 == 0, vals[1], vals[0])
        i2 = jnp.where(i1 == 0, 1, 0)
        for j in range(1, len(vals)):
            better = (vals[j] > v2) & (i1 != j)
            v2 = jnp.where(better, vals[j], v2)
            i2 = jnp.where(better, j, i2)
        return v1, i1, v2, i2

    gsum = []
    for grp in range(N_GROUPS):
        v1, _, v2, _ = top2(rows[grp * EXPERTS_PER_GROUP:(grp + 1) * EXPERTS_PER_GROUP])
        gsum.append(v1 + v2)
    best, gidx = gsum[0], jnp.zeros(gsum[0].shape, jnp.int32)
    for grp in range(1, N_GROUPS):
        better = gsum[grp] > best
        best = jnp.where(better, gsum[grp], best)
        gidx = jnp.where(better, grp, gidx)
    chosen, chosen_sc = [], []
    for j in range(EXPERTS_PER_GROUP):
        cj, sj = rows[j], sc[j]
        for grp in range(1, N_GROUPS):
            cj = jnp.where(gidx == grp, rows[grp * EXPERTS_PER_GROUP + j], cj)
            sj = jnp.where(gidx == grp, sc[grp * EXPERTS_PER_GROUP + j], sj)
        chosen.append(cj)
        chosen_sc.append(sj)
    _, i1, _, i2 = top2(chosen)
    w1, w2 = jnp.zeros_like(best), jnp.zeros_like(best)
    for j in range(EXPERTS_PER_GROUP):
        w1 = jnp.where(i1 == j, chosen_sc[j], w1)
        w2 = jnp.where(i2 == j, chosen_sc[j], w2)
    tot = w1 + w2
    w1, w2 = w1 / tot, w2 / tot
    tm = scores.shape[1]
    sub = lax.broadcasted_iota(jnp.int32, (SUBLANES, tm), 0)
    ind8 = jnp.zeros((SUBLANES, tm), F32)
    meta = jnp.zeros((SUBLANES, tm), F32)
    for j in range(EXPERTS_PER_GROUP):
        gate_j = jnp.where(i1 == j, w1, 0.0) + jnp.where(i2 == j, w2, 0.0)
        meta = jnp.where(sub == j, gate_j, meta)
    for grp in range(N_GROUPS):
        ind8 = jnp.where((sub == grp) & (gidx == grp), 1.0, ind8)
    before = jnp.dot(ind8.astype(BF16), u_ref[...], preferred_element_type=F32)
    rank = jnp.sum(ind8 * before, axis=0, keepdims=True)
    meta = jnp.where(sub == _META_GROUP, gidx.astype(F32), meta)
    meta = jnp.where(sub == _META_RANK, rank, meta)
    o_ref[...] = meta
    counts = jnp.sum(ind8, axis=1, keepdims=True)
    lane = lax.broadcasted_iota(jnp.int32, (SUBLANES, LANES), 1)
    row = lax.broadcasted_iota(jnp.int32, (SUBLANES, LANES), 0)
    cnt_ref[0] = jnp.broadcast_to(jnp.sum(jnp.where(lane == row, counts, 0.0), axis=0, keepdims=True),
                                  (SUBLANES, LANES)).astype(jnp.int32)


_META_GROUP, _META_RANK = EXPERTS_PER_GROUP, EXPERTS_PER_GROUP + 1


def moe_route(x_all, norm_w, shift_rows, scale_rows, router_w, router_b, *, n_ctx, tm):
    b, t, d = x_all.shape
    m = b * t
    upper = jnp.asarray(np.triu(np.ones((tm, tm), np.float32), 1), BF16)
    return pl.pallas_call(
        functools.partial(_route_kernel, rows_kw=dict(bsz=b, t_all=t, n_ctx=n_ctx)),
        grid=(m // tm,),
        in_specs=[pl.BlockSpec((tm, d), lambda i: (i, 0)),
                  _const_spec((1, d)), _const_spec(shift_rows.shape), _const_spec(scale_rows.shape),
                  pl.BlockSpec((N_EXPERTS, d), lambda i: (0, 0)),
                  pl.BlockSpec((N_EXPERTS, 1), lambda i: (0, 0)),
                  _const_spec((tm, tm))],
        out_specs=[pl.BlockSpec((SUBLANES, tm), lambda i: (0, i)),
                   pl.BlockSpec((1, SUBLANES, LANES), lambda i: (i, 0, 0)),
                   pl.BlockSpec((tm, d), lambda i: (i, 0))],
        out_shape=[jax.ShapeDtypeStruct((SUBLANES, m), F32),
                   jax.ShapeDtypeStruct((m // tm, SUBLANES, LANES), jnp.int32),
                   jax.ShapeDtypeStruct((m, d), BF16)],
        compiler_params=_cparams("parallel"),
    )(x_all.reshape(m, d), norm_w.reshape(1, d), shift_rows, scale_rows,
      router_w.T, router_b.reshape(N_EXPERTS, 1), upper)


MOE_TILE = 1024
MOE_SUB_ROWS = 256
MOE_TAIL_ROWS = 128


def _expert_kernel(cnt_ref, h_ref, mr_ref, mc_ref, wg_ref, wu_ref, wd_ref, x_ref, gate_ref, o_ref, *, rows_kw):
    i, grp = pl.program_id(0), pl.program_id(1)

    @pl.when(grp == 0)
    def _():
        o_ref[...] = jnp.zeros_like(o_ref)

    tm = h_ref.shape[0]
    count = cnt_ref[i * N_GROUPS + grp]
    grp_f = grp.astype(F32)
    sel_row = jnp.where(mr_ref[_META_GROUP:_META_GROUP + 1, :] == grp_f, mr_ref[_META_RANK:_META_RANK + 1, :], -1.0)
    sel_col = jnp.where(mc_ref[:, _META_GROUP:_META_GROUP + 1] == grp_f, mc_ref[:, _META_RANK:_META_RANK + 1], -1.0)
    gate_parts = _split3(mc_ref[...])

    def sub_block(first, rows):
        base = first.astype(F32)
        slot_r = lax.broadcasted_iota(jnp.int32, (rows, tm), 0).astype(F32)
        slot_c = lax.broadcasted_iota(jnp.int32, (tm, rows), 1).astype(F32)
        pick = (sel_row - base == slot_r).astype(BF16)
        put = (sel_col - base == slot_c).astype(BF16)
        xg = jnp.dot(pick, h_ref[...], preferred_element_type=F32).astype(BF16)
        gates = sum(jnp.dot(pick, p, preferred_element_type=F32) for p in gate_parts)
        y = jnp.zeros((rows, o_ref.shape[1]), F32)
        for e in range(EXPERTS_PER_GROUP):
            hid = _silu(_dot(xg, wg_ref[e])) * _dot(xg, wu_ref[e])
            y = y + _dot(gates[:, e:e + 1] * hid, wd_ref[e])
        o_ref[...] += jnp.dot(put, y.astype(BF16), preferred_element_type=F32)

    n_full = count // MOE_SUB_ROWS
    rem = count - n_full * MOE_SUB_ROWS
    n_main = n_full + (rem > MOE_TAIL_ROWS).astype(jnp.int32)

    def main_block(s, carry):
        sub_block(s * MOE_SUB_ROWS, MOE_SUB_ROWS)
        return carry

    lax.fori_loop(0, n_main, main_block, 0)

    @pl.when((rem > 0) & (rem <= MOE_TAIL_ROWS))
    def _():
        sub_block(n_full * MOE_SUB_ROWS, MOE_TAIL_ROWS)

    @pl.when(grp == N_GROUPS - 1)
    def _():
        (gate,) = _token_rows([gate_ref], i, tm, **rows_kw)
        o_ref[...] = x_ref[...] + gate * o_ref[...]


def moe_experts(hb, meta, counts, wg, wu, wd, x_all, gate_rows, *, n_ctx, tm):
    b, t, d = x_all.shape
    m = b * t
    tok = lambda i, g, cnt: (i, 0)
    grid_spec = pltpu.PrefetchScalarGridSpec(
        num_scalar_prefetch=1,
        grid=(m // tm, N_GROUPS),
        in_specs=[pl.BlockSpec((tm, d), tok),
                  pl.BlockSpec((SUBLANES, tm), lambda i, g, cnt: (0, i)),
                  pl.BlockSpec((tm, SUBLANES), tok),
                  pl.BlockSpec((EXPERTS_PER_GROUP, d, EXPERT_FF), lambda i, g, cnt: (g, 0, 0)),
                  pl.BlockSpec((EXPERTS_PER_GROUP, d, EXPERT_FF), lambda i, g, cnt: (g, 0, 0)),
                  pl.BlockSpec((EXPERTS_PER_GROUP, EXPERT_FF, d), lambda i, g, cnt: (g, 0, 0)),
                  pl.BlockSpec((tm, d), tok),
                  pl.BlockSpec(gate_rows.shape, lambda i, g, cnt: (0, 0))],
        out_specs=pl.BlockSpec((tm, d), tok))
    out = pl.pallas_call(
        functools.partial(_expert_kernel, rows_kw=dict(bsz=b, t_all=t, n_ctx=n_ctx)),
        grid_spec=grid_spec,
        out_shape=jax.ShapeDtypeStruct((m, d), F32),
        compiler_params=pltpu.CompilerParams(dimension_semantics=("parallel", "arbitrary"),
                                             vmem_limit_bytes=MOE_VMEM_LIMIT),
    )(counts[:, 0, :N_GROUPS].reshape(-1), hb, meta, meta.T, wg, wu, wd, x_all.reshape(m, d), gate_rows)
    return out.reshape(b, t, d)


def _token_rows(m_refs, tile, tm, *, bsz, t_all, n_ctx):
    row = tile * tm + lax.broadcasted_iota(jnp.int32, (tm, 1), 0)
    ctx = jnp.zeros((tm, 1), jnp.bool_)
    lat = []
    for bi in range(bsz):
        lo, split = bi * t_all, bi * t_all + t_all - n_ctx
        lat.append((row >= lo) & (row < split))
        ctx = ctx | ((row >= split) & (row < lo + t_all))
    out = []
    for m_ref in m_refs:
        v = jnp.where(ctx, m_ref[bsz:bsz + 1, :], 0.0)
        for bi in range(bsz):
            v = v + jnp.where(lat[bi], m_ref[bi:bi + 1, :], 0.0)
        out.append(v)
    return out


GRID_TILE_COLS = SUBLANES
GRID_TILE = GRID_W * GRID_TILE_COLS


def _grid_view(x_all, n_lat):
    b, t, d = x_all.shape
    rows = n_lat // GRID_W
    assert rows == GRID_W and t % GRID_W == 0
    last = GRID_W // GRID_TILE_COLS - 1
    spec = pl.BlockSpec((1, rows, GRID_TILE_COLS, d), lambda bi, i: (bi, 0, jnp.minimum(i, last), 0))
    return x_all.reshape(b, t // GRID_W, GRID_W, d), spec


def _grid_tile(xg_ref):
    return jnp.concatenate([xg_ref[0, :, j, :] for j in range(xg_ref.shape[2])], axis=0)


def _transpose_grid_kernel(xn_ref, xg_ref, o_ref, *, n_lat_tiles):
    i = pl.program_id(1)

    @pl.when(i < n_lat_tiles)
    def _():
        o_ref[0] = _grid_tile(xg_ref)

    @pl.when(i >= n_lat_tiles)
    def _():
        o_ref[0] = xn_ref[0]


def transpose_grid(x_all, *, n_ctx):
    b, t, d = x_all.shape
    xg, gspec = _grid_view(x_all, t - n_ctx)
    tok = pl.BlockSpec((1, GRID_TILE, d), lambda bi, i: (bi, i, 0))
    return pl.pallas_call(
        functools.partial(_transpose_grid_kernel, n_lat_tiles=(t - n_ctx) // GRID_TILE),
        grid=(b, pl.cdiv(t, GRID_TILE)),
        in_specs=[tok, gspec],
        out_specs=tok,
        out_shape=jax.ShapeDtypeStruct((b, t, d), x_all.dtype),
        compiler_params=_cparams("parallel", "parallel"),
    )(x_all, xg)


def _final_norm_kernel(x_ref, w_ref, o_ref, *, from_grid):
    x = _grid_tile(x_ref) if from_grid else x_ref[0]
    o_ref[0] = x * lax.rsqrt(jnp.mean(x * x, axis=-1, keepdims=True) + EPS) * w_ref[...]


def final_rms_norm(x_all, w, *, n_ctx, from_grid):
    b, t, d = x_all.shape
    n_lat = t - n_ctx
    tt = GRID_TILE
    tok = pl.BlockSpec((1, tt, d), lambda bi, i: (bi, i, 0))
    if from_grid:
        x_in, spec = _grid_view(x_all, n_lat)
    else:
        x_in, spec = x_all, tok
    return pl.pallas_call(
        functools.partial(_final_norm_kernel, from_grid=from_grid),
        grid=(b, n_lat // tt),
        in_specs=[spec, _const_spec((1, d))],
        out_specs=tok,
        out_shape=jax.ShapeDtypeStruct((b, n_lat, d), F32),
        compiler_params=_cparams("parallel", "parallel"),
    )(x_in, w.reshape(1, d))


def _pack_w_in(w_in, mixer):
    cols = _SRC_COLS[mixer]
    pieces, i = [], 0
    while i < len(cols):
        j = i
        if cols[i] < 0:
            while j < len(cols) and cols[j] < 0:
                j += 1
            pieces.append(jnp.zeros((w_in.shape[0], j - i), w_in.dtype))
        else:
            while j < len(cols) and cols[j] == cols[i] + (j - i):
                j += 1
            pieces.append(w_in[:, int(cols[i]):int(cols[i]) + (j - i)])
        i = j
    return jnp.concatenate(pieces, axis=1).astype(BF16)


def mixer_scans(ps, lp, *, n_ctx):
    p_ssm, p_gla, p_rwkv, p_gdn = ps

    xs, bc, sm = ssm_prep(p_ssm, lp, n_ctx=n_ctx)
    neg_a = jnp.pad(-jnp.exp(lp["ssm_a_log"]).reshape(1, -1), ((0, 0), (0, LANES - 2 * SSM_HEADS)))
    ssm = tuple(ssd_scan(xs, bc, sm, neg_a, n_ctx=n_ctx)) + (xs, p_ssm)

    w2 = [jnp.zeros((LANES, GLA_HEADS * GLA_DK), F32).at[d * GLA_RANK:(d + 1) * GLA_RANK].set(lp["gla_w2"][d])
          for d in range(2)]
    gb = [_row(lp["gla_b"][d]) for d in range(2)]
    gla = tuple(gla_scan(p_gla, w2, gb, n_ctx=n_ctx)) + (p_gla,)

    r, k, v, kk, a, lw, g, bonus = rwkv_prep(p_rwkv, lp, n_ctx=n_ctx)
    rwkv = tuple(rwkv_scan(r, k, v, kk, a, lw, _row(lp["rwkv_k_a"]), n_ctx=n_ctx)) + (g, bonus)

    q, kd, vd, smd = gdn_prep(p_gdn, lp, n_ctx=n_ctx)
    gdn = tuple(gdn_scan(q, kd, vd, smd, n_ctx=n_ctx)) + (p_gdn,)
    return ssm, gla, rwkv, gdn


def kernel(x, c, ctx, c_ctx, ada_w, ada_b, norm_mix, norm_ffn, w_in, w_gate, w_branch, w_out, ssm_conv_w, ssm_conv_b, ssm_a_log, ssm_dt_bias, ssm_d, ssm_norm, gla_w2, gla_b, gla_norm, rwkv_mu, rwkv_w0, rwkv_w2, rwkv_a0, rwkv_a2, rwkv_g2, rwkv_k_k, rwkv_k_a, rwkv_r_k, rwkv_ln_w, rwkv_ln_b, gdn_conv_w, gdn_a_log, gdn_dt_bias, gdn_norm, router_w, router_b, moe_w_gate, moe_w_up, moe_w_down, final_norm):
    bsz, seq, d = x.shape
    n_ctx = ctx.shape[1]
    t_all = n_ctx + seq
    m_all = bsz * t_all

    cond = jnp.concatenate([jax.nn.silu(c), jax.nn.silu(c_ctx)[None]], 0)
    cond = jnp.pad(cond, ((0, SUBLANES - cond.shape[0]), (0, 0)))
    mods, mod_rows = [], []
    for l in range(DEPTH):
        mod = pmatmul(cond, ada_w[l], tm=SUBLANES, tn=1024, precise=True) + ada_b[l]
        mod_rows.append(mod)
        lat = mod[:bsz].reshape(bsz, 6, d)
        cx = jnp.broadcast_to(mod[bsz].reshape(1, 6, d), (bsz, 6, d))
        mods.append(jnp.stack([cx, lat], axis=1))

    x_all = jnp.concatenate([x, ctx], axis=1)
    scan_order = False
    for l in range(DEPTH):
        if (l % 2 == 1) != scan_order:
            x_all = transpose_grid(x_all, n_ctx=n_ctx)
            scan_order = not scan_order
        lp = dict(ssm_conv_w=ssm_conv_w[l], ssm_conv_b=ssm_conv_b[l], ssm_a_log=ssm_a_log[l],
                  ssm_dt_bias=ssm_dt_bias[l], ssm_d=ssm_d[l], ssm_norm=ssm_norm[l],
                  gla_w2=gla_w2[l], gla_b=gla_b[l], gla_norm=gla_norm[l],
                  rwkv_mu=rwkv_mu[l], rwkv_w0=rwkv_w0[l], rwkv_w2=rwkv_w2[l], rwkv_a0=rwkv_a0[l],
                  rwkv_a2=rwkv_a2[l], rwkv_g2=rwkv_g2[l], rwkv_k_k=rwkv_k_k[l], rwkv_k_a=rwkv_k_a[l],
                  rwkv_r_k=rwkv_r_k[l], rwkv_ln_w=rwkv_ln_w[l], rwkv_ln_b=rwkv_ln_b[l],
                  gdn_conv_w=gdn_conv_w[l], gdn_a_log=gdn_a_log[l], gdn_dt_bias=gdn_dt_bias[l],
                  gdn_norm=gdn_norm[l])
        mod = mods[l]
        msel = lambda i: mod[:, :, i][:, :, None, :]

        h = norm_modulate(x_all, norm_mix[l], msel(0), msel(1), n_ctx=n_ctx)
        h2d = h.reshape(m_all, d)
        ps = []
        for mixer in ("ssm", "gla", "rwkv", "gdn"):
            wp = _pack_w_in(w_in[l], mixer)
            ps.append(pmatmul(h2d, wp, tm=1024, tn=wp.shape[1]).reshape(bsz, t_all, wp.shape[1]))
        gates = pmatmul(h2d, w_gate[l].astype(BF16), tm=1024, tn=1024, act="sigmoid", out_dtype=BF16)
        gates = gates.reshape(bsz, t_all, 4 * d)

        ssm, gla, rwkv, gdn = mixer_scans(ps, lp, n_ctx=n_ctx)
        x_all = merge_residual(ssm, gla, rwkv, gdn, gates, x_all, msel(2), lp,
                               w_branch[l].astype(BF16), w_out[l].astype(BF16), n_ctx=n_ctx)

        tm_moe = _pick_tile(m_all, MOE_TILE)
        rows = lambda i: mod_rows[l][:, i * d:(i + 1) * d]
        meta, counts, hb = moe_route(x_all, norm_ffn[l], rows(3), rows(4), router_w, router_b,
                                     n_ctx=n_ctx, tm=tm_moe)
        x_all = moe_experts(hb, meta, counts, moe_w_gate[l].astype(BF16), moe_w_up[l].astype(BF16),
                            moe_w_down[l].astype(BF16), x_all, rows(5), n_ctx=n_ctx, tm=tm_moe)

    return final_rms_norm(x_all, final_norm, n_ctx=n_ctx, from_grid=scan_order)
```

```python
import functools
import itertools

import numpy as np
import jax
import jax.numpy as jnp
from jax import lax
from jax.experimental import pallas as pl
from jax.experimental.pallas import tpu as pltpu

F32 = jnp.float32
BF16 = jnp.bfloat16
HI = lax.Precision.HIGHEST

D_MODEL = 1024
DEPTH = 2
GRID_W = 64
CHUNK = 64
EPS = 1e-6
BRANCH = D_MODEL // 2
SSM_HEADS, SSM_P, SSM_GROUPS, SSM_N = 8, 64, 2, 64
GLA_HEADS, GLA_DK, GLA_DV, GLA_RANK, GLA_TAU = 4, 64, 128, 16, 16.0
RWKV_HEADS, RWKV_N, RWKV_LN_EPS = 8, 64, 64e-5
GDN_HEADS, GDN_N = 4, 128
N_EXPERTS, N_GROUPS, EXPERTS_PER_GROUP = 16, 4, 4
EXPERT_FF = D_MODEL // 2
LANES = 128
SUBLANES = 8
VMEM_LIMIT = 48 * 1024 * 1024
MOE_VMEM_LIMIT = 56 * 1024 * 1024
ROW_TILE = 256

_REF_BLOCKS = (
    ("ssm", "z", 512), ("ssm", "xbc", 768), ("ssm", "dt", 16),
    ("gla", "q", 256), ("gla", "k", 256), ("gla", "v", 512), ("gla", "r", 512), ("gla", "glr", 32),
    ("rwkv", "all", 1920),
    ("gdn", "qkv", 1536), ("gdn", "z", 512), ("gdn", "ab", 16),
)
_PACKED = {
    "ssm": (("z", 512), ("dt", 128), ("pad", 128), ("xbc", 768)),
    "gla": (("q", 256), ("k", 256), ("v", 512), ("r", 512), ("glr", 128)),
    "rwkv": (("all", 1920),),
    "gdn": (("qkv", 1536), ("z", 512), ("ab", 128)),
}


def _packed_columns():
    start, s = {}, 0
    for mixer, blk, w in _REF_BLOCKS:
        start[(mixer, blk)] = (s, w)
        s += w
    out = {}
    for mixer, blocks in _PACKED.items():
        cols = []
        for blk, wp in blocks:
            s0, w = start.get((mixer, blk), (0, 0))
            cols += list(range(s0, s0 + w)) + [-1] * (wp - w)
        out[mixer] = np.asarray(cols, np.int32)
    return out


_SRC_COLS = _packed_columns()


def _cparams(*sem):
    return pltpu.CompilerParams(dimension_semantics=sem, vmem_limit_bytes=VMEM_LIMIT)


def _dot(a, b):
    return jnp.dot(a.astype(BF16), b.astype(BF16), preferred_element_type=F32)


def _dot_nt(a, b):
    return lax.dot_general(a.astype(BF16), b.astype(BF16), (((1,), (1,)), ((), ())),
                           preferred_element_type=F32)


def _dot_tn(a, b):
    return lax.dot_general(a.astype(BF16), b.astype(BF16), (((0,), (0,)), ((), ())),
                           preferred_element_type=F32)


def _dot_hi(a, b):
    return jnp.dot(a, b, precision=HI, preferred_element_type=F32)


def _dot_x3(a, b):
    ah = a.astype(BF16)
    al = (a - ah.astype(F32)).astype(BF16)
    bh = b.astype(BF16)
    bl = (b - bh.astype(F32)).astype(BF16)
    f = lambda u, v: jnp.dot(u, v, preferred_element_type=F32)
    return f(ah, bh) + (f(ah, bl) + f(al, bh))


def _dot_x2(a, w):
    ah = a.astype(BF16)
    al = (a - ah.astype(F32)).astype(BF16)
    return jnp.dot(ah, w, preferred_element_type=F32) + jnp.dot(al, w, preferred_element_type=F32)


def _softplus(x):
    return jnp.maximum(x, 0.0) + jnp.log(1.0 + jnp.exp(-jnp.abs(x)))


def _sigmoid(x):
    return 1.0 / (1.0 + jnp.exp(-x))


def _silu(x):
    return x * _sigmoid(x)


def _pick_tile(m, pref):
    t = pref
    while m % t:
        t //= 2
    return t


def _group_sums(y, n):
    m = min(n, LANES)
    row = lax.broadcasted_iota(jnp.int32, (LANES, LANES), 0)
    col = lax.broadcasted_iota(jnp.int32, (LANES, LANES), 1)
    same = ((row // m) == (col // m)).astype(BF16)
    parts = [_dot_x2(y[:, j:j + LANES], same) for j in range(0, y.shape[1], LANES)]
    k = n // m
    if k > 1:
        parts = [sum(parts[g * k:(g + 1) * k]) for g in range(len(parts) // k) for _ in range(k)]
    return jnp.concatenate(parts, axis=1)


def _mm_kernel(a_ref, w_ref, o_ref, *, act, precise):
    if precise:
        r = _dot_hi(a_ref[...].astype(F32), w_ref[...].astype(F32))
    else:
        r = _dot(a_ref[...], w_ref[...])
    if act == "sigmoid":
        r = _sigmoid(r)
    o_ref[...] = r.astype(o_ref.dtype)


def pmatmul(a, w, *, tm, tn, act=None, precise=False, out_dtype=F32):
    m, k = a.shape
    tm = _pick_tile(m, tm)
    if w.ndim == 3:
        per = w.shape[2] // tn
        n = w.shape[0] * w.shape[2]
        w_spec = pl.BlockSpec((None, k, tn), lambda j, i: (j // per, 0, j % per))
    else:
        n = w.shape[1]
        w_spec = pl.BlockSpec((k, tn), lambda j, i: (0, j))
    assert tm % SUBLANES == 0 and n % tn == 0 and w.shape[-1] % tn == 0, (m, tm, w.shape, tn)
    return pl.pallas_call(
        functools.partial(_mm_kernel, act=act, precise=precise),
        grid=(n // tn, m // tm),
        in_specs=[pl.BlockSpec((tm, k), lambda j, i: (i, 0)), w_spec],
        out_specs=pl.BlockSpec((tm, tn), lambda j, i: (i, j)),
        out_shape=jax.ShapeDtypeStruct((m, n), out_dtype),
        compiler_params=_cparams("parallel", "parallel"),
    )(a, w)


def _norm_mod_kernel(x_ref, w_ref, shift_ref, scale_ref, o_ref):
    x = x_ref[0]
    y = x * lax.rsqrt(jnp.mean(x * x, axis=-1, keepdims=True) + EPS) * w_ref[...]
    o_ref[0] = (y * (1.0 + scale_ref[0, 0]) + shift_ref[0, 0]).astype(o_ref.dtype)


def _mod_sel(n_lat_tiles):
    return lambda bi, i, *_: (bi, jnp.where(i < n_lat_tiles, 1, 0), 0, 0)


def norm_modulate(x_all, w, shift, scale, *, n_ctx, out_dtype=BF16):
    b, t, d = x_all.shape
    tm = _pick_tile(n_ctx, ROW_TILE)
    assert t % tm == 0
    tok = lambda bi, i: (bi, i, 0)
    return pl.pallas_call(
        _norm_mod_kernel,
        grid=(b, t // tm),
        in_specs=[pl.BlockSpec((1, tm, d), tok),
                  pl.BlockSpec((1, d), lambda bi, i: (0, 0)),
                  pl.BlockSpec((1, 1, 1, d), _mod_sel((t - n_ctx) // tm)),
                  pl.BlockSpec((1, 1, 1, d), _mod_sel((t - n_ctx) // tm))],
        out_specs=pl.BlockSpec((1, tm, d), tok),
        out_shape=jax.ShapeDtypeStruct((b, t, d), out_dtype),
        compiler_params=_cparams("parallel", "parallel"),
    )(x_all, w.reshape(1, d), shift, scale)


def _row(v):
    return v.reshape(1, -1).astype(F32)


def _const_spec(shape):
    return pl.BlockSpec(shape, lambda *_: (0,) * len(shape))


def _tile_specs(tt, width, col):
    r8 = tt // SUBLANES
    main = pl.BlockSpec((1, tt, width), lambda bi, i: (bi, i, col))
    prev = pl.BlockSpec((1, SUBLANES, width), lambda bi, i: (bi, jnp.maximum(i * r8 - 1, 0), col))
    return main, prev, r8


def _halo_specs(tt, width, col, t):
    main, prev, r8 = _tile_specs(tt, width, col)
    last8 = t // SUBLANES - 1
    nxt = pl.BlockSpec((1, SUBLANES, width), lambda bi, i: (bi, jnp.minimum((i + 1) * r8, last8), col))
    return [main, prev, nxt]


def _neighbours(x, prev8, next8, *, nct, nt):
    i = pl.program_id(1)
    tt = x.shape[0]
    row = lax.broadcasted_iota(jnp.int32, x.shape, 0)
    first = (i == 0) | (i == nct)
    last = (i == nct - 1) | (i == nt - 1)
    pr = jnp.where(first, 0.0, prev8[SUBLANES - 1:SUBLANES, :])
    nx = jnp.where(last, 0.0, next8[0:1, :])
    xp = jnp.where(row == 0, pr, pltpu.roll(x, 1, 0))
    xn = jnp.where(row == tt - 1, nx, pltpu.roll(x, tt - 1, 0))
    return xp, xn


def _prep_call(kernel, ins, in_specs, out_widths, *, b, t, tt, out_dtype=F32):
    tok = lambda bi, i: (bi, i, 0)
    return pl.pallas_call(
        kernel,
        grid=(b, t // tt),
        in_specs=in_specs,
        out_specs=[pl.BlockSpec((1, tt, w), tok) for w in out_widths],
        out_shape=[jax.ShapeDtypeStruct((b, t, w), out_dtype) for w in out_widths],
        compiler_params=_cparams("parallel", "parallel"),
    )(*ins)


def _chunk_masks(reverse):
    row = lax.broadcasted_iota(jnp.int32, (CHUNK, CHUNK), 0)
    col = lax.broadcasted_iota(jnp.int32, (CHUNK, CHUNK), 1)
    if reverse:
        return col >= row, col > row
    return col <= row, col < row


def _chunk_order(i, n_ctx_chunks, n_chunks, reverse):
    n_lat_chunks = n_chunks - n_ctx_chunks
    if not reverse:
        return jnp.where(i < n_ctx_chunks, n_lat_chunks + i, i - n_ctx_chunks)
    return jnp.where(i < n_ctx_chunks, n_chunks - 1 - i, n_lat_chunks - 1 - (i - n_ctx_chunks))


def _split3(a):
    hi = a.astype(BF16)
    r = a - hi.astype(F32)
    mid = r.astype(BF16)
    return hi, mid, (r - mid.astype(F32)).astype(BF16)


def _transpose_small(x):
    row = lax.broadcasted_iota(jnp.int32, (LANES, LANES), 0)
    col = lax.broadcasted_iota(jnp.int32, (LANES, LANES), 1)
    eye = (row == col).astype(BF16)
    nt = lambda p: lax.dot_general(eye, p, (((1,), (1,)), ((), ())), preferred_element_type=F32)
    hi, mid, lo = _split3(x)
    return nt(hi) + (nt(mid) + nt(lo))


def _chunk_cumsum(incl, x):
    m = incl.astype(BF16)
    hi, mid, lo = _split3(x)
    f = lambda p: jnp.dot(m, p, preferred_element_type=F32)
    return f(hi) + (f(mid) + f(lo))


def _select_columns(x, sel):
    c = x.shape[0]
    y = jnp.dot(jnp.concatenate(_split3(x), axis=0), sel, preferred_element_type=F32)
    return y[:c] + (y[c:2 * c] + y[2 * c:])


def _unit_tri_solve(mats, rhs, precise_levels=-1, explicit=False):
    n = range(len(mats))
    if explicit:
        row = lax.broadcasted_iota(jnp.int32, (CHUNK, CHUNK), 0)
        col = lax.broadcasted_iota(jnp.int32, (CHUNK, CHUNK), 1)
        x = [(row == col).astype(F32) - mats[h] for h in n]
    else:
        first = _dot_x3 if precise_levels >= 0 else _dot
        x = [rhs[h] - first(mats[h], rhs[h]) for h in n]
    yield
    p = mats
    for level in range(int(np.log2(CHUNK)) - 1):
        dot = _dot_x3 if level < precise_levels else _dot
        p = [dot(p[h], p[h]) for h in n]
        yield
        x = [x[h] + (dot(x[h], p[h]) if explicit else dot(p[h], x[h])) for h in n]
        yield
    if explicit:
        x = [_dot_x3(x[h], rhs[h]) for h in n]
        yield
    return x


def _bidir_scan(body, tok_ins, const_ins, state_shape, *, b, t, n_ctx, lockstep=True, batch_block=1):
    nc, ncc = t // CHUNK, n_ctx // CHUNK
    nb = batch_block
    assert b % nb == 0

    def chunk_spec(width, col, reverse):
        return pl.BlockSpec((nb, CHUNK, width), lambda bi, i: (bi, _chunk_order(i, ncc, nc, reverse), col))

    def direction(reverse):
        d = int(reverse)
        specs = [chunk_spec(w, cols[d], reverse) for _, w, *cols in tok_ins]
        specs += [_const_spec(pair[d].shape) for pair in const_ins]
        return specs, [a for a, *_ in tok_ins] + [pair[d] for pair in const_ins]

    (spec_f, arg_f), (spec_b, arg_b) = direction(False), direction(True)
    n_tok, n_in = len(tok_ins), len(arg_f)

    def kern(*refs):
        o_f, o_b, s_f, s_b = refs[2 * n_in:]

        @pl.when(pl.program_id(1) == 0)
        def _():
            s_f[...] = jnp.zeros_like(s_f)
            s_b[...] = jnp.zeros_like(s_b)

        def one(j, ins, o_ref, s_ref, reverse):
            ins = [r.at[pl.ds(j, 1)] if k < n_tok else r for k, r in enumerate(ins)]
            return body(*ins, o_ref.at[pl.ds(j, 1)], s_ref.at[j], reverse=reverse)

        gens = []
        for j in range(nb):
            gens += [one(j, refs[:n_in], o_f, s_f, False), one(j, refs[n_in:2 * n_in], o_b, s_b, True)]
        if not lockstep:
            gens = [itertools.chain(*gens)]
        while gens:
            gens = [g for g in gens if next(g, _DONE) is not _DONE]

    return pl.pallas_call(
        kern,
        grid=(b // nb, nc),
        in_specs=spec_f + spec_b,
        out_specs=[chunk_spec(BRANCH, 0, False), chunk_spec(BRANCH, 0, True)],
        out_shape=[jax.ShapeDtypeStruct((b, t, BRANCH), F32)] * 2,
        scratch_shapes=[pltpu.VMEM((nb,) + tuple(state_shape), F32)] * 2,
        compiler_params=_cparams("parallel", "arbitrary"),
    )(*arg_f, *arg_b)


_DONE = object()


def _batch_block(b, pref):
    return pref if b % pref == 0 else 1


def _ssm_prep_kernel(x_ref, xp_ref, xn_ref, dt_ref, cw_ref, cb_ref, dtb_ref, xs_ref, bc_ref, sm_ref,
                     *, nct, nt):
    x = x_ref[0]
    xp, xn = _neighbours(x, xp_ref[0], xn_ref[0], nct=nct, nt=nt)
    y = _silu(xp * cw_ref[0:1, :] + x * cw_ref[1:2, :] + xn * cw_ref[2:3, :] + cb_ref[...])
    xs_ref[0] = y[:, :BRANCH]
    bc_ref[0] = y[:, BRANCH:]
    sm_ref[0] = _softplus(dt_ref[0] + dtb_ref[...])


def ssm_prep(p, lp, *, n_ctx):
    b, t, _ = p.shape
    tt = _pick_tile(n_ctx, ROW_TILE)
    dtb = jnp.pad(lp["ssm_dt_bias"].reshape(1, -1), ((0, 0), (0, LANES - 2 * SSM_HEADS)))
    specs = _halo_specs(tt, 768, 1, t) + [pl.BlockSpec((1, tt, LANES), lambda bi, i: (bi, i, 4)),
                                          _const_spec((3, 768)), _const_spec((1, 768)), _const_spec((1, LANES))]
    kern = functools.partial(_ssm_prep_kernel, nct=(t - n_ctx) // tt, nt=t // tt)
    return _prep_call(kern, (p, p, p, p, lp["ssm_conv_w"], _row(lp["ssm_conv_b"]), dtb), specs,
                      (BRANCH, 2 * SSM_GROUPS * SSM_N, LANES), b=b, t=t, tt=tt)


def _ssd_body(x_ref, bc_ref, sm_ref, na_ref, o_ref, s_ref, *, reverse):
    incl, _ = _chunk_masks(reverse)
    last = 0 if reverse else CHUNK - 1
    off = SSM_HEADS if reverse else 0
    dt_all = sm_ref[0]
    g_all = _chunk_cumsum(incl, dt_all * na_ref[...])
    yield
    expand = _expand_matrix(off, SSM_HEADS, SSM_P)
    gx = _select_columns(g_all, expand)
    dx = _select_columns(dt_all, expand)
    gt_all = _transpose_small(g_all)
    dtt_all = _transpose_small(dt_all)
    yield
    assert SSM_P == CHUNK and 2 * SSM_P == LANES
    groups, pairs = range(SSM_GROUPS), range(SSM_HEADS // 2)
    rep = SSM_HEADS // SSM_GROUPS
    gw = SSM_GROUPS * SSM_N
    gs = [slice(grp * rep * SSM_P, (grp + 1) * rep * SSM_P) for grp in groups]
    ps = [slice(p * LANES, (p + 1) * LANES) for p in pairs]
    glx = gx[last:last + 1, :]
    egx = jnp.exp(gx)
    wx = dx * jnp.exp(glx - gx)
    eglx = jnp.exp(glx)
    x = x_ref[0]
    bm = [bc_ref[0, :, grp * SSM_N:(grp + 1) * SSM_N] for grp in groups]
    cm = [bc_ref[0, :, gw + grp * SSM_N:gw + (grp + 1) * SSM_N] for grp in groups]
    cb = [_dot_nt(cm[grp], bm[grp]) for grp in groups]
    cb2 = [jnp.concatenate([cb[grp], cb[grp]], axis=1) for grp in groups]
    s = [s_ref[grp] for grp in groups]
    row2 = lax.broadcasted_iota(jnp.int32, (CHUNK, LANES), 0)
    lane2 = lax.broadcasted_iota(jnp.int32, (CHUNK, LANES), 1)
    col2 = lane2 % CHUNK
    incl2 = (col2 >= row2) if reverse else (col2 <= row2)
    pair_row = lambda t, p: jnp.concatenate([t[off + 2 * p:off + 2 * p + 1, :], t[off + 2 * p + 1:off + 2 * p + 2, :]],
                                            axis=1)
    yield
    scores = [cb2[2 * p // rep] * jnp.exp(jnp.where(incl2, gx[:, ps[p]] - pair_row(gt_all, p), -jnp.inf))
              * pair_row(dtt_all, p) for p in pairs]
    x_bd = [jnp.concatenate([jnp.where(lane2 < SSM_P, x[:, ps[p]], 0.0), jnp.where(lane2 >= SSM_P, x[:, ps[p]], 0.0)],
                            axis=0) for p in pairs]
    yield
    intra = [_dot(scores[p], x_bd[p]) for p in pairs]
    yield
    inter = [_dot(cm[grp], s[grp]) for grp in groups]
    yield
    upd = [_dot_tn(bm[grp], wx[:, gs[grp]] * x[:, gs[grp]]) for grp in groups]
    yield
    for p in pairs:
        grp = 2 * p // rep
        ls = slice((2 * p % rep) * SSM_P, (2 * p % rep + 2) * SSM_P)
        o_ref[0, :, ps[p]] = intra[p] + egx[:, ps[p]] * inter[grp][:, ls]
    for grp in groups:
        s_ref[grp] = s[grp] * eglx[:, gs[grp]] + upd[grp]


def _expand_matrix(off, n_heads, width):
    row = lax.broadcasted_iota(jnp.int32, (LANES, n_heads * width), 0)
    col = lax.broadcasted_iota(jnp.int32, (LANES, n_heads * width), 1)
    lo = row * width - off * width
    return ((col >= lo) & (col < lo + width)).astype(BF16)


def ssd_scan(xs, bc, sm, neg_a, *, n_ctx):
    b, t, _ = xs.shape
    toks = [(xs, BRANCH, 0, 0), (bc, 2 * SSM_GROUPS * SSM_N, 0, 0), (sm, LANES, 0, 0)]
    state = (SSM_GROUPS, SSM_N, (SSM_HEADS // SSM_GROUPS) * SSM_P)
    return _bidir_scan(_ssd_body, toks, [(neg_a, neg_a)], state, b=b, t=t, n_ctx=n_ctx,
                       batch_block=_batch_block(b, 4))


def _gla_body(q_ref, k_ref, v_ref, glr_ref, w2_ref, gb_ref, o_ref, s_ref, *, reverse):
    incl, _ = _chunk_masks(reverse)
    last = 0 if reverse else CHUNK - 1
    logit = _dot_x3(glr_ref[0], w2_ref[...]) + gb_ref[...]
    yield
    la = -_softplus(-logit) * (1.0 / GLA_TAU)
    g_all = _chunk_cumsum(incl, la)
    yield
    heads = range(GLA_HEADS)
    ks = [slice(h * GLA_DK, (h + 1) * GLA_DK) for h in heads]
    vs = [slice(h * GLA_DV, (h + 1) * GLA_DV) for h in heads]
    g = [g_all[:, ks[h]] for h in heads]
    gl = [g[h][last:last + 1, :] for h in heads]
    k = [k_ref[0, :, ks[h]] for h in heads]
    v = [v_ref[0, :, vs[h]] for h in heads]
    qg = [q_ref[0, :, ks[h]] * (GLA_DK ** -0.5) * jnp.exp(g[h]) for h in heads]
    st = [s_ref[h] for h in heads]
    yield
    scores = [jnp.where(incl, _dot_nt(qg[h], k[h] * jnp.exp(-g[h])), 0.0) for h in heads]
    yield
    intra = [_dot(scores[h], v[h]) for h in heads]
    yield
    inter = [_dot_nt(qg[h], st[h]) for h in heads]
    yield
    upd = [_dot_tn(v[h], k[h] * jnp.exp(gl[h] - g[h])) for h in heads]
    yield
    for h in heads:
        o_ref[0, :, vs[h]] = intra[h] + inter[h]
        s_ref[h] = st[h] * jnp.exp(gl[h]) + upd[h]


def gla_scan(p, w2_pair, gb_pair, *, n_ctx):
    b, t, _ = p.shape
    kwid = GLA_HEADS * GLA_DK
    toks = [(p, kwid, 0, 0), (p, kwid, 1, 1), (p, BRANCH, 1, 1), (p, LANES, 12, 12)]
    return _bidir_scan(_gla_body, toks, [w2_pair, gb_pair], (GLA_HEADS, GLA_DV, GLA_DK), b=b, t=t, n_ctx=n_ctx,
                       batch_block=_batch_block(b, 4))


def _rwkv_prep_kernel(x_ref, xp_ref, xn_ref, mu_ref, w2_ref, w0_ref, a2_ref, a0_ref, g2_ref, kk_ref_w,
                      ka_ref, rk_ref, r_ref, k_ref, v_ref, kk_ref, a_ref, lw_ref, g_ref, bo_ref,
                      *, nct, nt):
    x = x_ref[0]
    xp, xn = _neighbours(x, xp_ref[0], xn_ref[0], nct=nct, nt=nt)
    x = x + mu_ref[...] * (0.5 * (xp + xn) - x)
    r, k, v = x[:, :BRANCH], x[:, BRANCH:2 * BRANCH], x[:, 2 * BRANCH:3 * BRANCH]
    wlr = x[:, 3 * BRANCH:3 * BRANCH + LANES]
    alr = x[:, 3 * BRANCH + LANES:3 * BRANCH + 2 * LANES]
    glr = x[:, 3 * BRANCH + 2 * LANES:]
    w_raw = _dot_x3(jnp.tanh(wlr), w2_ref[...]) + w0_ref[...]
    lw_ref[0] = _sigmoid(w_raw) * (-float(np.exp(-0.5)))
    a = _sigmoid(_dot_x3(alr, a2_ref[...]) + a0_ref[...])
    a_ref[0] = a
    g_ref[0] = _dot_x3(_sigmoid(glr), g2_ref[...])
    kk = k * kk_ref_w[...]
    kk_ref[0] = kk * lax.rsqrt(_group_sums(kk * kk, RWKV_N) + EPS)
    ksum = k * (2.0 + (a[:, :BRANCH] + a[:, BRANCH:] - 2.0) * ka_ref[...])
    bo_ref[0] = _group_sums(r * ksum * rk_ref[...], RWKV_N) * v
    r_ref[0] = r
    k_ref[0] = k
    v_ref[0] = v


def rwkv_prep(p, lp, *, n_ctx):
    b, t, w = p.shape
    tt = _pick_tile(n_ctx, ROW_TILE)

    def pair(wp):
        r, c = wp.shape[1:]
        return jnp.zeros((LANES, 2 * c), F32).at[:r, :c].set(wp[0]).at[r:2 * r, c:].set(wp[1])

    consts = (_row(lp["rwkv_mu"]), pair(lp["rwkv_w2"]), _row(lp["rwkv_w0"]), pair(lp["rwkv_a2"]),
              _row(lp["rwkv_a0"]), lp["rwkv_g2"], _row(lp["rwkv_k_k"]), _row(lp["rwkv_k_a"]),
              _row(lp["rwkv_r_k"]))
    specs = _halo_specs(tt, w, 0, t) + [_const_spec(c.shape) for c in consts]
    kern = functools.partial(_rwkv_prep_kernel, nct=(t - n_ctx) // tt, nt=t // tt)
    return _prep_call(kern, (p, p, p) + consts, specs,
                      (BRANCH, BRANCH, BRANCH, BRANCH, 2 * BRANCH, 2 * BRANCH, BRANCH, BRANCH),
                      b=b, t=t, tt=tt)


def _rwkv_body(r_ref, k_ref, v_ref, kk_ref, a_ref, lw_ref, ka_ref, o_ref, s_ref, *, reverse):
    incl, strict = _chunk_masks(reverse)
    last = 0 if reverse else CHUNK - 1
    lw_all = lw_ref[0]
    g_all = _chunk_cumsum(incl, lw_all)
    a_all = a_ref[0]
    k_all = k_ref[0] * (1.0 + (a_all - 1.0) * ka_ref[...])
    yield
    assert RWKV_N == CHUNK and 2 * RWKV_N == LANES
    del strict
    pairs = range(RWKV_HEADS // 2)
    ps = [slice(p * LANES, (p + 1) * LANES) for p in pairs]
    row2 = lax.broadcasted_iota(jnp.int32, (CHUNK, LANES), 0)
    col2 = lax.broadcasted_iota(jnp.int32, (CHUNK, LANES), 1) % CHUNK
    incl2, strict2 = ((col2 >= row2), (col2 > row2)) if reverse else ((col2 <= row2), (col2 < row2))
    mask2 = jnp.concatenate([strict2, incl2], axis=0)
    bd = _block_diag_pair
    g = [g_all[:, ps[p]] for p in pairs]
    gl = [g[p][last:last + 1, :] for p in pairs]
    eneg = [jnp.exp(-g[p]) for p in pairs]
    edec = [jnp.exp(gl[p] - g[p]) for p in pairs]
    kk = [kk_ref[0, :, ps[p]] for p in pairs]
    bvec = [kk[p] * a_all[:, ps[p]] for p in pairs]
    k = [k_all[:, ps[p]] for p in pairs]
    v = [v_ref[0, :, ps[p]] for p in pairs]
    kkg = [kk[p] * jnp.exp(g[p] - lw_all[:, ps[p]]) for p in pairs]
    rg = [r_ref[0, :, ps[p]] * jnp.exp(g[p]) for p in pairs]
    bh = [bvec[p] * eneg[p] for p in pairs]
    kh = [k[p] * eneg[p] for p in pairs]
    s = [s_ref[p] for p in pairs]
    yield
    both = [jnp.concatenate([kkg[p], rg[p]], axis=0) for p in pairs]
    mb = [jnp.where(mask2, _dot_nt(both[p], bd(bh[p])), 0.0) for p in pairs]
    yield
    mk = [jnp.where(mask2, _dot_nt(both[p], bd(kh[p])), 0.0) for p in pairs]
    yield
    part = [_dot(mk[p], bd(v[p])) + _dot_nt(both[p], s[p]) for p in pairs]
    yield
    pw = [mb[p][:CHUNK] for p in pairs]
    x = [part[p][:CHUNK] - _dot(pw[p], bd(part[p][:CHUNK])) for p in pairs]
    yield
    for _ in range(int(np.log2(CHUNK)) - 1):
        pw = [_dot(pw[p], bd(pw[p])) for p in pairs]
        yield
        x = [x[p] + _dot(pw[p], bd(x[p])) for p in pairs]
        yield
    u = [-x[p] for p in pairs]
    for p in pairs:
        o_ref[0, :, ps[p]] = part[p][CHUNK:] + _dot(mb[p][CHUNK:], bd(u[p]))
    yield
    for p in pairs:
        upd = _dot_tn(jnp.concatenate([bd(u[p]), bd(v[p])], axis=0),
                      jnp.concatenate([bd(bvec[p] * edec[p]), bd(k[p] * edec[p])], axis=0))
        s_ref[p] = s[p] * jnp.exp(gl[p]) + upd


def _block_diag_pair(m):
    lane = lax.broadcasted_iota(jnp.int32, m.shape, 1)
    w = m.shape[1] // 2
    return jnp.concatenate([jnp.where(lane < w, m, 0.0), jnp.where(lane >= w, m, 0.0)], axis=0)


def rwkv_scan(r, k, v, kk, a, lw, k_a, *, n_ctx):
    b, t, _ = r.shape
    toks = [(r, BRANCH, 0, 0), (k, BRANCH, 0, 0), (v, BRANCH, 0, 0), (kk, BRANCH, 0, 0),
            (a, BRANCH, 0, 1), (lw, BRANCH, 0, 1)]
    return _bidir_scan(_rwkv_body, toks, [(k_a, k_a)], (RWKV_HEADS // 2, LANES, LANES), b=b, t=t, n_ctx=n_ctx,
                       batch_block=_batch_block(b, 4))


def _gdn_prep_kernel(x_ref, xp_ref, xn_ref, ab_ref, cw_ref, na_ref, dtb_ref,
                     q_ref, k_ref, v_ref, sm_ref, *, nct, nt):
    x = x_ref[0]
    xp, xn = _neighbours(x, xp_ref[0], xn_ref[0], nct=nct, nt=nt)
    y = _silu(xp * cw_ref[0:1, :] + x * cw_ref[1:2, :] + xn * cw_ref[2:3, :])
    q, k = y[:, :BRANCH], y[:, BRANCH:2 * BRANCH]
    q_ref[0] = q * lax.rsqrt(_group_sums(q * q, GDN_N) + EPS) * (GDN_N ** -0.5)
    k_ref[0] = k * lax.rsqrt(_group_sums(k * k, GDN_N) + EPS)
    v_ref[0] = y[:, 2 * BRANCH:]
    ab = ab_ref[0]
    lane = lax.broadcasted_iota(jnp.int32, ab.shape, 1)
    sm_ref[0] = jnp.where(lane < 2 * GDN_HEADS, na_ref[...] * _softplus(ab + dtb_ref[...]), _sigmoid(ab))


def gdn_prep(p, lp, *, n_ctx):
    b, t, _ = p.shape
    tt = _pick_tile(n_ctx, ROW_TILE)
    padrow = lambda v: jnp.pad(v.reshape(1, -1), ((0, 0), (0, LANES - 2 * GDN_HEADS)))
    consts = (lp["gdn_conv_w"], padrow(-jnp.exp(lp["gdn_a_log"])), padrow(lp["gdn_dt_bias"]))
    specs = (_halo_specs(tt, 3 * BRANCH, 0, t) + [pl.BlockSpec((1, tt, LANES), lambda bi, i: (bi, i, 16))]
             + [_const_spec(c.shape) for c in consts])
    kern = functools.partial(_gdn_prep_kernel, nct=(t - n_ctx) // tt, nt=t // tt)
    return _prep_call(kern, (p, p, p, p) + consts, specs, (BRANCH, BRANCH, BRANCH, LANES), b=b, t=t, tt=tt)


def _gdn_body(q_ref, k_ref, v_ref, sm_ref, o_ref, s_ref, *, reverse):
    incl, strict = _chunk_masks(reverse)
    last = 0 if reverse else CHUNK - 1
    off = GDN_HEADS if reverse else 0
    sm = sm_ref[0]
    g_all = _chunk_cumsum(incl, sm)
    yield
    gt_all = _transpose_small(g_all)
    yield
    heads = range(GDN_HEADS)
    hs = [slice(h * GDN_N, (h + 1) * GDN_N) for h in heads]
    g = [g_all[:, off + h:off + h + 1] for h in heads]
    gl = [g[h][last:last + 1, :] for h in heads]
    beta = [sm[:, 2 * GDN_HEADS + off + h:2 * GDN_HEADS + off + h + 1] for h in heads]
    q = [q_ref[0, :, hs[h]] for h in heads]
    k = [k_ref[0, :, hs[h]] for h in heads]
    v = [v_ref[0, :, hs[h]] for h in heads]
    s = [s_ref[h] for h in heads]
    decay = [jnp.exp(jnp.where(incl, g[h] - gt_all[off + h:off + h + 1, :], -jnp.inf)) for h in heads]
    yield
    kq = [_dot_nt(jnp.concatenate([k[h], q[h]], axis=0), k[h]) for h in heads]
    yield
    lower = [jnp.where(strict, kq[h][:CHUNK] * decay[h] * beta[h], 0.0) for h in heads]
    attn = [kq[h][CHUNK:] * decay[h] for h in heads]
    o_part = [_dot(q[h] * jnp.exp(g[h]), s[h]) for h in heads]
    yield
    rhs = [jnp.concatenate([v[h] * beta[h], k[h] * (beta[h] * jnp.exp(g[h]))], axis=1) for h in heads]
    sol = yield from _unit_tri_solve(lower, rhs, precise_levels=2, explicit=True)
    v_new = [sol[h][:, :GDN_N] - _dot(sol[h][:, GDN_N:], s[h]) for h in heads]
    yield
    for h in heads:
        o_ref[0, :, hs[h]] = o_part[h] + _dot(attn[h], v_new[h])
    yield
    for h in heads:
        s_ref[h] = s[h] * jnp.exp(gl[h]) + _dot_tn(k[h] * jnp.exp(gl[h] - g[h]), v_new[h])


def gdn_scan(q, k, v, sm, *, n_ctx):
    b, t, _ = q.shape
    toks = [(q, BRANCH, 0, 0), (k, BRANCH, 0, 0), (v, BRANCH, 0, 0), (sm, LANES, 0, 0)]
    return _bidir_scan(_gdn_body, toks, [], (GDN_HEADS, GDN_N, GDN_N), b=b, t=t, n_ctx=n_ctx,
                       batch_block=_batch_block(b, 4))


def _merge_kernel(sf_ref, sb_ref, sx_ref, sz_ref, gf_ref, gb_ref, gr_ref, rf_ref, rb_ref, rg_ref, rbo_ref,
                  df_ref, db_ref, dz_ref, gate_ref, x_ref, m_ref,
                  sd_ref, sn_ref, gn_ref, lnw_ref, lnb_ref, dn_ref,
                  wb_ref, wo_ref, o_ref):
    def group_rms(y, n, w_ref):
        return y * lax.rsqrt(_group_sums(y * y, n) * (1.0 / n) + EPS) * w_ref[...]

    y = (sf_ref[0] + sb_ref[0] + sd_ref[...] * sx_ref[0]) * _silu(sz_ref[0])
    ys = group_rms(y, BRANCH // SSM_GROUPS, sn_ref)
    yg = group_rms(gf_ref[0] + gb_ref[0], GLA_DV, gn_ref) * _silu(gr_ref[0])
    y = rf_ref[0] + rb_ref[0]
    yc = y - _group_sums(y, RWKV_N) * (1.0 / RWKV_N)
    var = _group_sums(yc * yc, RWKV_N) * (1.0 / RWKV_N)
    yr = (yc * lax.rsqrt(var + RWKV_LN_EPS) * lnw_ref[...] + lnb_ref[...] + rbo_ref[0]) * rg_ref[0]
    yd = group_rms(df_ref[0] + db_ref[0], GDN_N, dn_ref) * _silu(dz_ref[0])
    acc = None
    for i, yi in enumerate((ys, yg, yr, yd)):
        term = gate_ref[0, :, i * D_MODEL:(i + 1) * D_MODEL].astype(F32) * _dot(yi, wb_ref[i])
        acc = term if acc is None else acc + term
    o_ref[0] = x_ref[0] + m_ref[0, 0] * _dot(acc, wo_ref[...])


def merge_residual(ssm, gla, rwkv, gdn, gates, x_all, gate_mod, lp, w_branch, w_out, *, n_ctx):
    b, t, d = x_all.shape
    tm = _pick_tile(n_ctx, ROW_TILE)
    tok = lambda bi, i: (bi, i, 0)
    blk = lambda c: pl.BlockSpec((1, tm, BRANCH), lambda bi, i: (bi, i, c))
    half = blk(0)
    consts = (_row(jnp.repeat(lp["ssm_d"], SSM_P)), _row(lp["ssm_norm"]),
              _row(jnp.tile(lp["gla_norm"], GLA_HEADS)), _row(lp["rwkv_ln_w"]), _row(lp["rwkv_ln_b"]),
              _row(jnp.tile(lp["gdn_norm"], GDN_HEADS)), w_branch, w_out)
    ins = (ssm[0], ssm[1], ssm[2], ssm[3], gla[0], gla[1], gla[2], rwkv[0], rwkv[1], rwkv[2], rwkv[3],
           gdn[0], gdn[1], gdn[2], gates, x_all, gate_mod) + consts
    specs = ([half, half, half, blk(0), half, half, blk(2), half, half, half, half, half, half, blk(3),
              pl.BlockSpec((1, tm, 4 * d), tok), pl.BlockSpec((1, tm, d), tok),
              pl.BlockSpec((1, 1, 1, d), _mod_sel((t - n_ctx) // tm))]
             + [_const_spec(c.shape) for c in consts])
    return pl.pallas_call(
        _merge_kernel,
        grid=(b, t // tm),
        in_specs=specs,
        out_specs=pl.BlockSpec((1, tm, d), tok),
        out_shape=jax.ShapeDtypeStruct((b, t, d), F32),
        compiler_params=_cparams("parallel", "parallel"),
    )(*ins)


def _route_kernel(x_ref, nw_ref, shift_ref, scale_ref, rw_ref, rb_ref, u_ref, o_ref, cnt_ref, hb_ref, *,
                  rows_kw, route_ctx):
    x = x_ref[...]
    shift, scale = _token_rows([shift_ref, scale_ref], pl.program_id(0), x.shape[0], **rows_kw)
    h = x * lax.rsqrt(jnp.mean(x * x, axis=-1, keepdims=True) + EPS) * nw_ref[...] * (1.0 + scale) + shift
    hb_ref[...] = h.astype(BF16)
    logits = lax.dot_general(rw_ref[...], h, (((1,), (1,)), ((), ())),
                             precision=HI, preferred_element_type=F32)
    scores = _sigmoid(logits)
    sel = scores + rb_ref[...]
    rows = [sel[e:e + 1, :] for e in range(N_EXPERTS)]
    sc = [scores[e:e + 1, :] for e in range(N_EXPERTS)]

    def top2(vals):
        v1, i1 = vals[0], jnp.zeros(vals[0].shape, jnp.int32)
        for j in range(1, len(vals)):
            better = vals[j] > v1
            v1 = jnp.where(better, vals[j], v1)
            i1 = jnp.where(better, j, i1)
        v2 = jnp.where(i1 == 0, vals[1], vals[0])
        i2 = jnp.where(i1 == 0, 1, 0)
        for j in range(1, len(vals)):
            better = (vals[j] > v2) & (i1 != j)
            v2 = jnp.where(better, vals[j], v2)
            i2 = jnp.where(better, j, i2)
        return v1, i1, v2, i2

    gsum = []
    for grp in range(N_GROUPS):
        v1, _, v2, _ = top2(rows[grp * EXPERTS_PER_GROUP:(grp + 1) * EXPERTS_PER_GROUP])
        gsum.append(v1 + v2)
    best, gidx = gsum[0], jnp.zeros(gsum[0].shape, jnp.int32)
    for grp in range(1, N_GROUPS):
        better = gsum[grp] > best
        best = jnp.where(better, gsum[grp], best)
        gidx = jnp.where(better, grp, gidx)
    chosen, chosen_sc = [], []
    for j in range(EXPERTS_PER_GROUP):
        cj, sj = rows[j], sc[j]
        for grp in range(1, N_GROUPS):
            cj = jnp.where(gidx == grp, rows[grp * EXPERTS_PER_GROUP + j], cj)
            sj = jnp.where(gidx == grp, sc[grp * EXPERTS_PER_GROUP + j], sj)
        chosen.append(cj)
        chosen_sc.append(sj)
    _, i1, _, i2 = top2(chosen)
    w1, w2 = jnp.zeros_like(best), jnp.zeros_like(best)
    for j in range(EXPERTS_PER_GROUP):
        w1 = jnp.where(i1 == j, chosen_sc[j], w1)
        w2 = jnp.where(i2 == j, chosen_sc[j], w2)
    tot = w1 + w2
    w1, w2 = w1 / tot, w2 / tot
    tm = scores.shape[1]
    if not route_ctx:
        tok = pl.program_id(0) * tm + lax.broadcasted_iota(jnp.int32, (1, tm), 1)
        for bi in range(rows_kw["bsz"]):
            end = (bi + 1) * rows_kw["t_all"]
            gidx = jnp.where((tok >= end - rows_kw["n_ctx"]) & (tok < end), -1, gidx)
    sub = lax.broadcasted_iota(jnp.int32, (SUBLANES, tm), 0)
    ind8 = jnp.zeros((SUBLANES, tm), F32)
    meta = jnp.zeros((SUBLANES, tm), F32)
    for j in range(EXPERTS_PER_GROUP):
        gate_j = jnp.where(i1 == j, w1, 0.0) + jnp.where(i2 == j, w2, 0.0)
        meta = jnp.where(sub == j, gate_j, meta)
    for grp in range(N_GROUPS):
        ind8 = jnp.where((sub == grp) & (gidx == grp), 1.0, ind8)
    before = jnp.dot(ind8.astype(BF16), u_ref[...], preferred_element_type=F32)
    rank = jnp.sum(ind8 * before, axis=0, keepdims=True)
    meta = jnp.where(sub == _META_GROUP, gidx.astype(F32), meta)
    meta = jnp.where(sub == _META_RANK, rank, meta)
    o_ref[...] = meta
    counts = jnp.sum(ind8, axis=1, keepdims=True)
    lane = lax.broadcasted_iota(jnp.int32, (SUBLANES, LANES), 1)
    row = lax.broadcasted_iota(jnp.int32, (SUBLANES, LANES), 0)
    cnt_ref[0] = jnp.broadcast_to(jnp.sum(jnp.where(lane == row, counts, 0.0), axis=0, keepdims=True),
                                  (SUBLANES, LANES)).astype(jnp.int32)


_META_GROUP, _META_RANK = EXPERTS_PER_GROUP, EXPERTS_PER_GROUP + 1


def moe_route(x_all, norm_w, shift_rows, scale_rows, router_w, router_b, *, n_ctx, tm, route_ctx=True):
    b, t, d = x_all.shape
    m = b * t
    upper = jnp.asarray(np.triu(np.ones((tm, tm), np.float32), 1), BF16)
    return pl.pallas_call(
        functools.partial(_route_kernel, rows_kw=dict(bsz=b, t_all=t, n_ctx=n_ctx), route_ctx=route_ctx),
        grid=(m // tm,),
        in_specs=[pl.BlockSpec((tm, d), lambda i: (i, 0)),
                  _const_spec((1, d)), _const_spec(shift_rows.shape), _const_spec(scale_rows.shape),
                  pl.BlockSpec((N_EXPERTS, d), lambda i: (0, 0)),
                  pl.BlockSpec((N_EXPERTS, 1), lambda i: (0, 0)),
                  _const_spec((tm, tm))],
        out_specs=[pl.BlockSpec((SUBLANES, tm), lambda i: (0, i)),
                   pl.BlockSpec((1, SUBLANES, LANES), lambda i: (i, 0, 0)),
                   pl.BlockSpec((tm, d), lambda i: (i, 0))],
        out_shape=[jax.ShapeDtypeStruct((SUBLANES, m), F32),
                   jax.ShapeDtypeStruct((m // tm, SUBLANES, LANES), jnp.int32),
                   jax.ShapeDtypeStruct((m, d), BF16)],
        compiler_params=_cparams("parallel"),
    )(x_all.reshape(m, d), norm_w.reshape(1, d), shift_rows, scale_rows,
      router_w.T, router_b.reshape(N_EXPERTS, 1), upper)


MOE_TILE = 1024
MOE_SUB_ROWS = 256
MOE_TAIL_ROWS = 128


def _expert_kernel(cnt_ref, h_ref, mr_ref, mc_ref, wg_ref, wu_ref, wd_ref, x_ref, gate_ref, o_ref, *, rows_kw):
    i, grp = pl.program_id(0), pl.program_id(1)

    @pl.when(grp == 0)
    def _():
        o_ref[...] = jnp.zeros_like(o_ref)

    tm = h_ref.shape[0]
    count = cnt_ref[i * N_GROUPS + grp]
    grp_f = grp.astype(F32)
    sel_row = jnp.where(mr_ref[_META_GROUP:_META_GROUP + 1, :] == grp_f, mr_ref[_META_RANK:_META_RANK + 1, :], -1.0)
    sel_col = jnp.where(mc_ref[:, _META_GROUP:_META_GROUP + 1] == grp_f, mc_ref[:, _META_RANK:_META_RANK + 1], -1.0)
    gate_parts = _split3(mc_ref[...])

    def sub_block(first, rows):
        base = first.astype(F32)
        slot_r = lax.broadcasted_iota(jnp.int32, (rows, tm), 0).astype(F32)
        slot_c = lax.broadcasted_iota(jnp.int32, (tm, rows), 1).astype(F32)
        pick = (sel_row - base == slot_r).astype(BF16)
        put = (sel_col - base == slot_c).astype(BF16)
        xg = jnp.dot(pick, h_ref[...], preferred_element_type=F32).astype(BF16)
        gates = sum(jnp.dot(pick, p, preferred_element_type=F32) for p in gate_parts)
        y = jnp.zeros((rows, o_ref.shape[1]), F32)
        for e in range(EXPERTS_PER_GROUP):
            hid = _silu(_dot(xg, wg_ref[e])) * _dot(xg, wu_ref[e])
            y = y + _dot(gates[:, e:e + 1] * hid, wd_ref[e])
        o_ref[...] += jnp.dot(put, y.astype(BF16), preferred_element_type=F32)

    n_full = count // MOE_SUB_ROWS
    rem = count - n_full * MOE_SUB_ROWS
    n_main = n_full + (rem > MOE_TAIL_ROWS).astype(jnp.int32)

    def main_block(s, carry):
        sub_block(s * MOE_SUB_ROWS, MOE_SUB_ROWS)
        return carry

    lax.fori_loop(0, n_main, main_block, 0)

    @pl.when((rem > 0) & (rem <= MOE_TAIL_ROWS))
    def _():
        sub_block(n_full * MOE_SUB_ROWS, MOE_TAIL_ROWS)

    @pl.when(grp == N_GROUPS - 1)
    def _():
        (gate,) = _token_rows([gate_ref], i, tm, **rows_kw)
        o_ref[...] = x_ref[...] + gate * o_ref[...]


def moe_experts(hb, meta, counts, wg, wu, wd, x_all, gate_rows, *, n_ctx, tm):
    b, t, d = x_all.shape
    m = b * t
    tok = lambda i, g, cnt: (i, 0)
    grid_spec = pltpu.PrefetchScalarGridSpec(
        num_scalar_prefetch=1,
        grid=(m // tm, N_GROUPS),
        in_specs=[pl.BlockSpec((tm, d), tok),
                  pl.BlockSpec((SUBLANES, tm), lambda i, g, cnt: (0, i)),
                  pl.BlockSpec((tm, SUBLANES), tok),
                  pl.BlockSpec((EXPERTS_PER_GROUP, d, EXPERT_FF), lambda i, g, cnt: (g, 0, 0)),
                  pl.BlockSpec((EXPERTS_PER_GROUP, d, EXPERT_FF), lambda i, g, cnt: (g, 0, 0)),
                  pl.BlockSpec((EXPERTS_PER_GROUP, EXPERT_FF, d), lambda i, g, cnt: (g, 0, 0)),
                  pl.BlockSpec((tm, d), tok),
                  pl.BlockSpec(gate_rows.shape, lambda i, g, cnt: (0, 0))],
        out_specs=pl.BlockSpec((tm, d), tok))
    out = pl.pallas_call(
        functools.partial(_expert_kernel, rows_kw=dict(bsz=b, t_all=t, n_ctx=n_ctx)),
        grid_spec=grid_spec,
        out_shape=jax.ShapeDtypeStruct((m, d), F32),
        compiler_params=pltpu.CompilerParams(dimension_semantics=("parallel", "arbitrary"),
                                             vmem_limit_bytes=MOE_VMEM_LIMIT),
    )(counts[:, 0, :N_GROUPS].reshape(-1), hb, meta, meta.T, wg, wu, wd, x_all.reshape(m, d), gate_rows)
    return out.reshape(b, t, d)


def _token_rows(m_refs, tile, tm, *, bsz, t_all, n_ctx):
    row = tile * tm + lax.broadcasted_iota(jnp.int32, (tm, 1), 0)
    ctx = jnp.zeros((tm, 1), jnp.bool_)
    lat = []
    for bi in range(bsz):
        lo, split = bi * t_all, bi * t_all + t_all - n_ctx
        lat.append((row >= lo) & (row < split))
        ctx = ctx | ((row >= split) & (row < lo + t_all))
    out = []
    for m_ref in m_refs:
        v = jnp.where(ctx, m_ref[bsz:bsz + 1, :], 0.0)
        for bi in range(bsz):
            v = v + jnp.where(lat[bi], m_ref[bi:bi + 1, :], 0.0)
        out.append(v)
    return out


GRID_TILE_COLS = SUBLANES
GRID_TILE = GRID_W * GRID_TILE_COLS


def _grid_view(x_all, n_lat):
    b, t, d = x_all.shape
    rows = n_lat // GRID_W
    assert rows == GRID_W and t % GRID_W == 0
    last = GRID_W // GRID_TILE_COLS - 1
    spec = pl.BlockSpec((1, rows, GRID_TILE_COLS, d), lambda bi, i: (bi, 0, jnp.minimum(i, last), 0))
    return x_all.reshape(b, t // GRID_W, GRID_W, d), spec


def _grid_tile(xg_ref):
    return jnp.concatenate([xg_ref[0, :, j, :] for j in range(xg_ref.shape[2])], axis=0)


def _transpose_grid_kernel(xn_ref, xg_ref, o_ref, *, n_lat_tiles):
    i = pl.program_id(1)

    @pl.when(i < n_lat_tiles)
    def _():
        o_ref[0] = _grid_tile(xg_ref)

    @pl.when(i >= n_lat_tiles)
    def _():
        o_ref[0] = xn_ref[0]


def transpose_grid(x_all, *, n_ctx):
    b, t, d = x_all.shape
    xg, gspec = _grid_view(x_all, t - n_ctx)
    tok = pl.BlockSpec((1, GRID_TILE, d), lambda bi, i: (bi, i, 0))
    return pl.pallas_call(
        functools.partial(_transpose_grid_kernel, n_lat_tiles=(t - n_ctx) // GRID_TILE),
        grid=(b, pl.cdiv(t, GRID_TILE)),
        in_specs=[tok, gspec],
        out_specs=tok,
        out_shape=jax.ShapeDtypeStruct((b, t, d), x_all.dtype),
        compiler_params=_cparams("parallel", "parallel"),
    )(x_all, xg)


def _final_norm_kernel(x_ref, w_ref, o_ref, *, from_grid):
    x = _grid_tile(x_ref) if from_grid else x_ref[0]
    o_ref[0] = x * lax.rsqrt(jnp.mean(x * x, axis=-1, keepdims=True) + EPS) * w_ref[...]


def final_rms_norm(x_all, w, *, n_ctx, from_grid):
    b, t, d = x_all.shape
    n_lat = t - n_ctx
    tt = GRID_TILE
    tok = pl.BlockSpec((1, tt, d), lambda bi, i: (bi, i, 0))
    if from_grid:
        x_in, spec = _grid_view(x_all, n_lat)
    else:
        x_in, spec = x_all, tok
    return pl.pallas_call(
        functools.partial(_final_norm_kernel, from_grid=from_grid),
        grid=(b, n_lat // tt),
        in_specs=[spec, _const_spec((1, d))],
        out_specs=tok,
        out_shape=jax.ShapeDtypeStruct((b, n_lat, d), F32),
        compiler_params=_cparams("parallel", "parallel"),
    )(x_in, w.reshape(1, d))


def _pack_w_in(w_in, mixer):
    cols = _SRC_COLS[mixer]
    pieces, i = [], 0
    while i < len(cols):
        j = i
        if cols[i] < 0:
            while j < len(cols) and cols[j] < 0:
                j += 1
            pieces.append(jnp.zeros((w_in.shape[0], j - i), w_in.dtype))
        else:
            while j < len(cols) and cols[j] == cols[i] + (j - i):
                j += 1
            pieces.append(w_in[:, int(cols[i]):int(cols[i]) + (j - i)])
        i = j
    return jnp.concatenate(pieces, axis=1).astype(BF16)


def mixer_scans(ps, lp, *, n_ctx):
    p_ssm, p_gla, p_rwkv, p_gdn = ps

    xs, bc, sm = ssm_prep(p_ssm, lp, n_ctx=n_ctx)
    neg_a = jnp.pad(-jnp.exp(lp["ssm_a_log"]).reshape(1, -1), ((0, 0), (0, LANES - 2 * SSM_HEADS)))
    ssm = tuple(ssd_scan(xs, bc, sm, neg_a, n_ctx=n_ctx)) + (xs, p_ssm)

    w2 = [jnp.zeros((LANES, GLA_HEADS * GLA_DK), F32).at[d * GLA_RANK:(d + 1) * GLA_RANK].set(lp["gla_w2"][d])
          for d in range(2)]
    gb = [_row(lp["gla_b"][d]) for d in range(2)]
    gla = tuple(gla_scan(p_gla, w2, gb, n_ctx=n_ctx)) + (p_gla,)

    r, k, v, kk, a, lw, g, bonus = rwkv_prep(p_rwkv, lp, n_ctx=n_ctx)
    rwkv = tuple(rwkv_scan(r, k, v, kk, a, lw, _row(lp["rwkv_k_a"]), n_ctx=n_ctx)) + (g, bonus)

    q, kd, vd, smd = gdn_prep(p_gdn, lp, n_ctx=n_ctx)
    gdn = tuple(gdn_scan(q, kd, vd, smd, n_ctx=n_ctx)) + (p_gdn,)
    return ssm, gla, rwkv, gdn


def kernel(x, c, ctx, c_ctx, ada_w, ada_b, norm_mix, norm_ffn, w_in, w_gate, w_branch, w_out, ssm_conv_w, ssm_conv_b, ssm_a_log, ssm_dt_bias, ssm_d, ssm_norm, gla_w2, gla_b, gla_norm, rwkv_mu, rwkv_w0, rwkv_w2, rwkv_a0, rwkv_a2, rwkv_g2, rwkv_k_k, rwkv_k_a, rwkv_r_k, rwkv_ln_w, rwkv_ln_b, gdn_conv_w, gdn_a_log, gdn_dt_bias, gdn_norm, router_w, router_b, moe_w_gate, moe_w_up, moe_w_down, final_norm):
    bsz, seq, d = x.shape
    n_ctx = ctx.shape[1]
    t_all = n_ctx + seq
    m_all = bsz * t_all

    cond = jnp.concatenate([jax.nn.silu(c), jax.nn.silu(c_ctx)[None]], 0)
    cond = jnp.pad(cond, ((0, SUBLANES - cond.shape[0]), (0, 0)))
    mods, mod_rows = [], []
    for l in range(DEPTH):
        mod = pmatmul(cond, ada_w[l], tm=SUBLANES, tn=1024, precise=True) + ada_b[l]
        mod_rows.append(mod)
        lat = mod[:bsz].reshape(bsz, 6, d)
        cx = jnp.broadcast_to(mod[bsz].reshape(1, 6, d), (bsz, 6, d))
        mods.append(jnp.stack([cx, lat], axis=1))

    x_all = jnp.concatenate([x, ctx], axis=1)
    scan_order = False
    for l in range(DEPTH):
        if (l % 2 == 1) != scan_order:
            x_all = transpose_grid(x_all, n_ctx=n_ctx)
            scan_order = not scan_order
        lp = dict(ssm_conv_w=ssm_conv_w[l], ssm_conv_b=ssm_conv_b[l], ssm_a_log=ssm_a_log[l],
                  ssm_dt_bias=ssm_dt_bias[l], ssm_d=ssm_d[l], ssm_norm=ssm_norm[l],
                  gla_w2=gla_w2[l], gla_b=gla_b[l], gla_norm=gla_norm[l],
                  rwkv_mu=rwkv_mu[l], rwkv_w0=rwkv_w0[l], rwkv_w2=rwkv_w2[l], rwkv_a0=rwkv_a0[l],
                  rwkv_a2=rwkv_a2[l], rwkv_g2=rwkv_g2[l], rwkv_k_k=rwkv_k_k[l], rwkv_k_a=rwkv_k_a[l],
                  rwkv_r_k=rwkv_r_k[l], rwkv_ln_w=rwkv_ln_w[l], rwkv_ln_b=rwkv_ln_b[l],
                  gdn_conv_w=gdn_conv_w[l], gdn_a_log=gdn_a_log[l], gdn_dt_bias=gdn_dt_bias[l],
                  gdn_norm=gdn_norm[l])
        mod = mods[l]
        msel = lambda i: mod[:, :, i][:, :, None, :]

        h = norm_modulate(x_all, norm_mix[l], msel(0), msel(1), n_ctx=n_ctx)
        h2d = h.reshape(m_all, d)
        ps = []
        for mixer in ("ssm", "gla", "rwkv", "gdn"):
            wp = _pack_w_in(w_in[l], mixer)
            ps.append(pmatmul(h2d, wp, tm=1024, tn=wp.shape[1]).reshape(bsz, t_all, wp.shape[1]))
        gates = pmatmul(h2d, w_gate[l].astype(BF16), tm=1024, tn=1024, act="sigmoid", out_dtype=BF16)
        gates = gates.reshape(bsz, t_all, 4 * d)

        ssm, gla, rwkv, gdn = mixer_scans(ps, lp, n_ctx=n_ctx)
        x_all = merge_residual(ssm, gla, rwkv, gdn, gates, x_all, msel(2), lp,
                               w_branch[l].astype(BF16), w_out[l].astype(BF16), n_ctx=n_ctx)

        tm_moe = _pick_tile(m_all, MOE_TILE)
        rows = lambda i: mod_rows[l][:, i * d:(i + 1) * d]
        meta, counts, hb = moe_route(x_all, norm_ffn[l], rows(3), rows(4), router_w, router_b,
                                     n_ctx=n_ctx, tm=tm_moe, route_ctx=l < DEPTH - 1)
        x_all = moe_experts(hb, meta, counts, moe_w_gate[l].astype(BF16), moe_w_up[l].astype(BF16),
                            moe_w_down[l].astype(BF16), x_all, rows(5), n_ctx=n_ctx, tm=tm_moe)

    return final_rms_norm(x_all, final_norm, n_ctx=n_ctx, from_grid=scan_order)
```
